```python
import math
import jax, jax.numpy as jnp
from jax import lax
import numpy as np

D_MODEL = 1024
BATCH = 16
SEQ = 4096
DEPTH = 1

GDN_HEADS = 8
GDN_DK = 128
GDN_DV = 128
GDN_CONV = 4
GDN_CHUNK = 64
MLA_HEADS = 8
MLA_Q_RANK = 384
MLA_KV_RANK = 256
MLA_NOPE = 128
MLA_ROPE = 64
MLA_V = 128
ROPE_THETA = 10000.0
Q_BLOCK = 128
D_FF = 2816
FFN_CONV = 3
EPS = 1e-6

SPLIT_SIZES = (
    3 * GDN_HEADS * GDN_DK if GDN_DK == GDN_DV else 2 * GDN_HEADS * GDN_DK + GDN_HEADS * GDN_DV,
    GDN_HEADS * GDN_DV,
    GDN_HEADS,
    GDN_HEADS,
    MLA_Q_RANK,
    MLA_KV_RANK,
    MLA_ROPE,
    D_MODEL,
    D_MODEL,
)
D_IN = sum(SPLIT_SIZES)

kernel_name = "hybrid_gdn_mla_convffn_block"


def rmsnorm(x, g):
    xf = x.astype(jnp.float32)
    xf = xf * lax.rsqrt(jnp.mean(xf * xf, axis=-1, keepdims=True) + EPS)
    return (xf * g.astype(jnp.float32)).astype(x.dtype)


def l2norm(x):
    return x * lax.rsqrt(jnp.sum(x * x, axis=-1, keepdims=True) + EPS)


def causal_dwconv(x, w):
    k = w.shape[0]
    return lax.conv_general_dilated(
        x, w[:, None, :].astype(x.dtype), window_strides=(1,), padding=[(k - 1, 0)],
        dimension_numbers=("NWC", "WIO", "NWC"), feature_group_count=x.shape[-1])


def split_in(z):
    offs = list(np.cumsum(SPLIT_SIZES)[:-1])
    return jnp.split(z, [int(o) for o in offs], axis=-1)


def rope(x, pos):
    half = x.shape[-1] // 2
    inv = ROPE_THETA ** (-jnp.arange(half, dtype=jnp.float32) / half)
    ang = pos.astype(jnp.float32)[:, None] * inv[None, :]
    cos = jnp.cos(ang)[:, None, :]
    sin = jnp.sin(ang)[:, None, :]
    xf = x.astype(jnp.float32)
    x1, x2 = xf[..., :half], xf[..., half:]
    return jnp.concatenate([x1 * cos - x2 * sin, x2 * cos + x1 * sin], axis=-1).astype(x.dtype)


def gdn_chunked(q, k, v, g, beta):
    b, s, h, dk = q.shape
    dv = v.shape[-1]
    c = GDN_CHUNK
    n = s // c
    f32 = jnp.float32
    q = l2norm(q.astype(f32)) * (dk ** -0.5)
    k = l2norm(k.astype(f32))
    v = v.astype(f32)

    def chunk(t):
        t = t.reshape((b, n, c, h) + t.shape[3:])
        return jnp.moveaxis(t, 3, 1)

    q, k, v = chunk(q), chunk(k), chunk(v)
    g = jnp.cumsum(chunk(g.astype(f32)), axis=-1)
    beta = chunk(beta.astype(f32))
    k_beta = k * beta[..., None]
    v_beta = v * beta[..., None]

    causal = jnp.tril(jnp.ones((c, c), dtype=bool))
    diff = g[..., :, None] - g[..., None, :]
    decay = jnp.where(causal, jnp.exp(jnp.where(causal, diff, 0.0)), 0.0)

    lmat = jnp.einsum("bhnid,bhnjd->bhnij", k_beta, k) * decay
    rhs = jnp.concatenate([v_beta, k_beta * jnp.exp(g)[..., None]], axis=-1)
    sol = lax.linalg.triangular_solve(lmat, rhs, left_side=True, lower=True, unit_diagonal=True)
    u, w = sol[..., :dv], sol[..., dv:]

    a_intra = jnp.einsum("bhnid,bhnjd->bhnij", q, k) * decay
    q_dec = q * jnp.exp(g)[..., None]
    k_dec = k * jnp.exp(g[..., -1:] - g)[..., None]
    g_last = jnp.exp(g[..., -1])

    xs = tuple(jnp.moveaxis(t, 2, 0) for t in (u, w, q_dec, k_dec, a_intra, g_last))

    def step(state, inp):
        u_n, w_n, qd_n, kd_n, a_n, gl_n = inp
        v_new = u_n - jnp.einsum("bhck,bhkv->bhcv", w_n, state)
        o = jnp.einsum("bhck,bhkv->bhcv", qd_n, state) + jnp.einsum("bhij,bhjv->bhiv", a_n, v_new)
        state = state * gl_n[..., None, None] + jnp.einsum("bhck,bhcv->bhkv", kd_n, v_new)
        return state, o

    s0 = jnp.zeros((b, h, dk, dv), dtype=f32)
    _, o = lax.scan(step, s0, xs)
    o = jnp.transpose(o, (1, 0, 3, 2, 4)).reshape(b, s, h, dv)
    return o


def mla_causal(q_nope, q_pe, k_nope, k_pe, v):
    s = q_nope.shape[1]
    scale = (MLA_NOPE + MLA_ROPE) ** -0.5
    outs = []
    for i in range(s // Q_BLOCK):
        q0, q1 = i * Q_BLOCK, (i + 1) * Q_BLOCK
        sc = (jnp.einsum("bqhd,bkhd->bhqk", q_nope[:, q0:q1], k_nope[:, :q1])
              + jnp.einsum("bqhr,bkr->bhqk", q_pe[:, q0:q1], k_pe[:, :q1]))
        sc = sc.astype(jnp.float32) * scale
        qpos = q0 + jnp.arange(Q_BLOCK)
        kpos = jnp.arange(q1)
        sc = jnp.where(qpos[:, None] >= kpos[None, :], sc, -jnp.inf)
        p = jax.nn.softmax(sc, axis=-1).astype(v.dtype)
        outs.append(jnp.einsum("bhqk,bkhd->bqhd", p, v[:, :q1]))
    return jnp.concatenate(outs, axis=1)


def _fwd_setup_inputs(seed: int = 0) -> dict:
    key = jax.random.key(seed)
    ks = jax.random.split(key, 24)
    L, D = DEPTH, D_MODEL
    nrm = lambda k, shape, fan: jax.random.normal(k, shape, jnp.float32) * (fan ** -0.5)
    gain = lambda k, n: 1.0 + 0.05 * jax.random.normal(k, (L, n), jnp.float32)
    a_log = jnp.log(jax.random.uniform(ks[3], (L, GDN_HEADS), jnp.float32, 1.0, 16.0))
    dt = jnp.exp(jax.random.uniform(ks[4], (L, GDN_HEADS), jnp.float32, math.log(1e-3), math.log(1e-1)))
    dt_bias = dt + jnp.log(-jnp.expm1(-dt))
    return {
        "x": jax.random.normal(ks[0], (BATCH, SEQ, D), jnp.float32),
        "norm_mix_g": gain(ks[1], D),
        "w_in": nrm(ks[2], (L, D, D_IN), D),
        "conv_qkv_w": nrm(ks[5], (L, GDN_CONV, SPLIT_SIZES[0]), GDN_CONV),
        "gdn_a_log": a_log,
        "gdn_dt_bias": dt_bias,
        "gdn_norm_g": gain(ks[6], GDN_DV),
        "mla_q_norm_g": gain(ks[7], MLA_Q_RANK),
        "w_uq": nrm(ks[8], (L, MLA_Q_RANK, MLA_HEADS * (MLA_NOPE + MLA_ROPE)), MLA_Q_RANK),
        "mla_kv_norm_g": gain(ks[9], MLA_KV_RANK),
        "w_ukv": nrm(ks[10], (L, MLA_KV_RANK, MLA_HEADS * (MLA_NOPE + MLA_V)), MLA_KV_RANK),
        "w_o_gdn": nrm(ks[11], (L, GDN_HEADS * GDN_DV, D), GDN_HEADS * GDN_DV),
        "w_o_mla": nrm(ks[12], (L, MLA_HEADS * MLA_V, D), MLA_HEADS * MLA_V),
        "w_out": nrm(ks[13], (L, D, D), D),
        "norm_ffn_g": gain(ks[14], D),
        "w_up": nrm(ks[15], (L, D, 2 * D_FF), D),
        "conv_ffn_w": nrm(ks[16], (L, FFN_CONV, 2 * D_FF), FFN_CONV),
        "w_down": nrm(ks[17], (L, D_FF, D), D_FF),
        "norm_final_g": 1.0 + 0.05 * jax.random.normal(ks[18], (D,), jnp.float32),
    }


def _fwd_reference(x, norm_mix_g, w_in, conv_qkv_w, gdn_a_log, gdn_dt_bias, gdn_norm_g,
              mla_q_norm_g, w_uq, mla_kv_norm_g, w_ukv, w_o_gdn, w_o_mla, w_out,
              norm_ffn_g, w_up, conv_ffn_w, w_down, norm_final_g):
    b, s, _ = x.shape
    pos = jnp.arange(s)
    hA, hB = GDN_HEADS, MLA_HEADS
    for l in range(DEPTH):
        h = rmsnorm(x, norm_mix_g[l])
        z = jnp.einsum("bsd,de->bse", h, w_in[l])
        qkv_a, gate_a, a_a, b_a, c_q, c_kv, k_pe, gate_br_a, gate_br_b = split_in(z)

        qkv_a = jax.nn.silu(causal_dwconv(qkv_a, conv_qkv_w[l]))
        q_a, k_a, v_a = jnp.split(qkv_a, [hA * GDN_DK, 2 * hA * GDN_DK], axis=-1)
        q_a = q_a.reshape(b, s, hA, GDN_DK)
        k_a = k_a.reshape(b, s, hA, GDN_DK)
        v_a = v_a.reshape(b, s, hA, GDN_DV)
        g_log = -jnp.exp(gdn_a_log[l].astype(jnp.float32)) * jax.nn.softplus(
            a_a.astype(jnp.float32) + gdn_dt_bias[l].astype(jnp.float32))
        beta = jax.nn.sigmoid(b_a.astype(jnp.float32))
        o_a = gdn_chunked(q_a, k_a, v_a, g_log, beta).astype(x.dtype)
        o_a = rmsnorm(o_a, gdn_norm_g[l]) * jax.nn.silu(gate_a.reshape(b, s, hA, GDN_DV))
        y_a = jnp.einsum("bse,ed->bsd", o_a.reshape(b, s, hA * GDN_DV), w_o_gdn[l])

        cq = rmsnorm(c_q, mla_q_norm_g[l])
        q_b = jnp.einsum("bsr,re->bse", cq, w_uq[l]).reshape(b, s, hB, MLA_NOPE + MLA_ROPE)
        q_nope, q_pe = q_b[..., :MLA_NOPE], rope(q_b[..., MLA_NOPE:], pos)
        ckv = rmsnorm(c_kv, mla_kv_norm_g[l])
        kv = jnp.einsum("bsr,re->bse", ckv, w_ukv[l]).reshape(b, s, hB, MLA_NOPE + MLA_V)
        k_nope, v_b = kv[..., :MLA_NOPE], kv[..., MLA_NOPE:]
        k_pe_r = rope(k_pe[:, :, None, :], pos)[:, :, 0, :]
        o_b = mla_causal(q_nope, q_pe, k_nope, k_pe_r, v_b)
        y_b = jnp.einsum("bse,ed->bsd", o_b.reshape(b, s, hB * MLA_V), w_o_mla[l])

        merged = jax.nn.sigmoid(gate_br_a) * y_a + jax.nn.sigmoid(gate_br_b) * y_b
        x = x + jnp.einsum("bsd,de->bse", merged, w_out[l])

        h = rmsnorm(x, norm_ffn_g[l])
        u = causal_dwconv(jnp.einsum("bsd,df->bsf", h, w_up[l]), conv_ffn_w[l])
        gate_f, up_f = u[..., :D_FF], u[..., D_FF:]
        x = x + jnp.einsum("bsf,fd->bsd", jax.nn.silu(gate_f) * up_f, w_down[l])
    return rmsnorm(x, norm_final_g)


import jax as _jax
import jax.numpy as _jnp

TWIN_FORMAT = 'train_step'
FWD_PARAMS = ['x', 'norm_mix_g', 'w_in', 'conv_qkv_w', 'gdn_a_log', 'gdn_dt_bias', 'gdn_norm_g', 'mla_q_norm_g', 'w_uq', 'mla_kv_norm_g', 'w_ukv', 'w_o_gdn', 'w_o_mla', 'w_out', 'norm_ffn_g', 'w_up', 'conv_ffn_w', 'w_down', 'norm_final_g']
TWIN_WEIGHTS = ['norm_mix_g', 'w_in', 'conv_qkv_w', 'gdn_a_log', 'gdn_dt_bias', 'gdn_norm_g', 'mla_q_norm_g', 'w_uq', 'mla_kv_norm_g', 'w_ukv', 'w_o_gdn', 'w_o_mla', 'w_out', 'norm_ffn_g', 'w_up', 'conv_ffn_w', 'w_down', 'norm_final_g']
TWIN_DIFF_INPUT = 'x'
TWIN_INPUTS = ['x', 'norm_mix_g', 'w_in', 'conv_qkv_w', 'gdn_a_log', 'gdn_dt_bias', 'gdn_norm_g', 'mla_q_norm_g', 'w_uq', 'mla_kv_norm_g', 'w_ukv', 'w_o_gdn', 'w_o_mla', 'w_out', 'norm_ffn_g', 'w_up', 'conv_ffn_w', 'w_down', 'norm_final_g', 'loss_target', 'm_norm_mix_g', 'm_w_in', 'm_conv_qkv_w', 'm_gdn_a_log', 'm_gdn_dt_bias', 'm_gdn_norm_g', 'm_mla_q_norm_g', 'm_w_uq', 'm_mla_kv_norm_g', 'm_w_ukv', 'm_w_o_gdn', 'm_w_o_mla', 'm_w_out', 'm_norm_ffn_g', 'm_w_up', 'm_conv_ffn_w', 'm_w_down', 'm_norm_final_g', 'v_norm_mix_g', 'v_w_in', 'v_conv_qkv_w', 'v_gdn_a_log', 'v_gdn_dt_bias', 'v_gdn_norm_g', 'v_mla_q_norm_g', 'v_w_uq', 'v_mla_kv_norm_g', 'v_w_ukv', 'v_w_o_gdn', 'v_w_o_mla', 'v_w_out', 'v_norm_ffn_g', 'v_w_up', 'v_conv_ffn_w', 'v_w_down', 'v_norm_final_g']
TWIN_OUTPUTS = ['loss', 'grad_x', 'grad_norm_mix_g', 'grad_w_in', 'grad_conv_qkv_w', 'grad_gdn_a_log', 'grad_gdn_dt_bias', 'grad_gdn_norm_g', 'grad_mla_q_norm_g', 'grad_w_uq', 'grad_mla_kv_norm_g', 'grad_w_ukv', 'grad_w_o_gdn', 'grad_w_o_mla', 'grad_w_out', 'grad_norm_ffn_g', 'grad_w_up', 'grad_conv_ffn_w', 'grad_w_down', 'grad_norm_final_g', 'delta_norm_mix_g', 'delta_w_in', 'delta_conv_qkv_w', 'delta_gdn_a_log', 'delta_gdn_dt_bias', 'delta_gdn_norm_g', 'delta_mla_q_norm_g', 'delta_w_uq', 'delta_mla_kv_norm_g', 'delta_w_ukv', 'delta_w_o_gdn', 'delta_w_o_mla', 'delta_w_out', 'delta_norm_ffn_g', 'delta_w_up', 'delta_conv_ffn_w', 'delta_w_down', 'delta_norm_final_g', 'new_m_norm_mix_g', 'new_m_w_in', 'new_m_conv_qkv_w', 'new_m_gdn_a_log', 'new_m_gdn_dt_bias', 'new_m_gdn_norm_g', 'new_m_mla_q_norm_g', 'new_m_w_uq', 'new_m_mla_kv_norm_g', 'new_m_w_ukv', 'new_m_w_o_gdn', 'new_m_w_o_mla', 'new_m_w_out', 'new_m_norm_ffn_g', 'new_m_w_up', 'new_m_conv_ffn_w', 'new_m_w_down', 'new_m_norm_final_g', 'new_v_norm_mix_g', 'new_v_w_in', 'new_v_conv_qkv_w', 'new_v_gdn_a_log', 'new_v_gdn_dt_bias', 'new_v_gdn_norm_g', 'new_v_mla_q_norm_g', 'new_v_w_uq', 'new_v_mla_kv_norm_g', 'new_v_w_ukv', 'new_v_w_o_gdn', 'new_v_w_o_mla', 'new_v_w_out', 'new_v_norm_ffn_g', 'new_v_w_up', 'new_v_conv_ffn_w', 'new_v_w_down', 'new_v_norm_final_g']
TWIN_LEAF_KINDS = {'loss': 'loss', 'grad_x': 'grad_x', 'grad_norm_mix_g': 'grad_w', 'grad_w_in': 'grad_w', 'grad_conv_qkv_w': 'grad_w', 'grad_gdn_a_log': 'grad_w', 'grad_gdn_dt_bias': 'grad_w', 'grad_gdn_norm_g': 'grad_w', 'grad_mla_q_norm_g': 'grad_w', 'grad_w_uq': 'grad_w', 'grad_mla_kv_norm_g': 'grad_w', 'grad_w_ukv': 'grad_w', 'grad_w_o_gdn': 'grad_w', 'grad_w_o_mla': 'grad_w', 'grad_w_out': 'grad_w', 'grad_norm_ffn_g': 'grad_w', 'grad_w_up': 'grad_w', 'grad_conv_ffn_w': 'grad_w', 'grad_w_down': 'grad_w', 'grad_norm_final_g': 'grad_w', 'delta_norm_mix_g': 'delta_w', 'delta_w_in': 'delta_w', 'delta_conv_qkv_w': 'delta_w', 'delta_gdn_a_log': 'delta_w', 'delta_gdn_dt_bias': 'delta_w', 'delta_gdn_norm_g': 'delta_w', 'delta_mla_q_norm_g': 'delta_w', 'delta_w_uq': 'delta_w', 'delta_mla_kv_norm_g': 'delta_w', 'delta_w_ukv': 'delta_w', 'delta_w_o_gdn': 'delta_w', 'delta_w_o_mla': 'delta_w', 'delta_w_out': 'delta_w', 'delta_norm_ffn_g': 'delta_w', 'delta_w_up': 'delta_w', 'delta_conv_ffn_w': 'delta_w', 'delta_w_down': 'delta_w', 'delta_norm_final_g': 'delta_w', 'new_m_norm_mix_g': 'new_m', 'new_m_w_in': 'new_m', 'new_m_conv_qkv_w': 'new_m', 'new_m_gdn_a_log': 'new_m', 'new_m_gdn_dt_bias': 'new_m', 'new_m_gdn_norm_g': 'new_m', 'new_m_mla_q_norm_g': 'new_m', 'new_m_w_uq': 'new_m', 'new_m_mla_kv_norm_g': 'new_m', 'new_m_w_ukv': 'new_m', 'new_m_w_o_gdn': 'new_m', 'new_m_w_o_mla': 'new_m', 'new_m_w_out': 'new_m', 'new_m_norm_ffn_g': 'new_m', 'new_m_w_up': 'new_m', 'new_m_conv_ffn_w': 'new_m', 'new_m_w_down': 'new_m', 'new_m_norm_final_g': 'new_m', 'new_v_norm_mix_g': 'new_v', 'new_v_w_in': 'new_v', 'new_v_conv_qkv_w': 'new_v', 'new_v_gdn_a_log': 'new_v', 'new_v_gdn_dt_bias': 'new_v', 'new_v_gdn_norm_g': 'new_v', 'new_v_mla_q_norm_g': 'new_v', 'new_v_w_uq': 'new_v', 'new_v_mla_kv_norm_g': 'new_v', 'new_v_w_ukv': 'new_v', 'new_v_w_o_gdn': 'new_v', 'new_v_w_o_mla': 'new_v', 'new_v_w_out': 'new_v', 'new_v_norm_ffn_g': 'new_v', 'new_v_w_up': 'new_v', 'new_v_conv_ffn_w': 'new_v', 'new_v_w_down': 'new_v', 'new_v_norm_final_g': 'new_v'}


def _forward(args):
    return _fwd_reference(*[args[k] for k in FWD_PARAMS])


def _output_shape():
    out = _jax.eval_shape(lambda: _forward(_fwd_setup_inputs(0)))
    return out.shape, out.dtype

N_MICROBATCH = 1
ADAM_LR = 0.001
ADAM_B1 = 0.9
ADAM_B2 = 0.999
ADAM_EPS = 1e-08
ADAM_WD = 0.01
ADAM_STEP = 10
PER_EXAMPLE_BATCH_AXIS = {'x': 0, 'loss_target': 0}
SHARED_INPUTS = []
_WEIGHT_DTYPES = {'norm_mix_g': _jnp.float32, 'w_in': _jnp.float32, 'conv_qkv_w': _jnp.float32, 'gdn_a_log': _jnp.float32, 'gdn_dt_bias': _jnp.float32, 'gdn_norm_g': _jnp.float32, 'mla_q_norm_g': _jnp.float32, 'w_uq': _jnp.float32, 'mla_kv_norm_g': _jnp.float32, 'w_ukv': _jnp.float32, 'w_o_gdn': _jnp.float32, 'w_o_mla': _jnp.float32, 'w_out': _jnp.float32, 'norm_ffn_g': _jnp.float32, 'w_up': _jnp.float32, 'conv_ffn_w': _jnp.float32, 'w_down': _jnp.float32, 'norm_final_g': _jnp.float32}
MOMENT_SCALE = {'norm_mix_g': 1.568230e-01, 'w_in': 6.080920e-02, 'conv_qkv_w': 7.040141e-02, 'gdn_a_log': 4.996405e-01, 'gdn_dt_bias': 4.746208e-01, 'gdn_norm_g': 2.738080e-01, 'mla_q_norm_g': 4.626887e-02, 'w_uq': 2.177409e-02, 'mla_kv_norm_g': 7.955577e-02, 'w_ukv': 2.732133e-02, 'w_o_gdn': 8.765595e-02, 'w_o_mla': 3.173747e-02, 'w_out': 9.069539e-02, 'norm_ffn_g': 1.951697e-01, 'w_up': 7.793110e-02, 'conv_ffn_w': 7.800570e-02, 'w_down': 1.295471e-01, 'norm_final_g': 6.404970e+01}


def _to_microbatches(a, axis):
    t = _jnp.moveaxis(a, axis, 0)
    t = t.reshape((N_MICROBATCH, t.shape[0] // N_MICROBATCH) + t.shape[1:])
    return _jnp.moveaxis(t, 1, axis + 1)


def setup_inputs(seed: int = 0) -> dict:
    inp = _fwd_setup_inputs(seed)
    key = _jax.random.fold_in(_jax.random.key(seed), 7919)
    shape, _ = _output_shape()
    out = dict(inp)
    out["loss_target"] = _jax.random.normal(_jax.random.fold_in(key, 0), shape, _jnp.float32)
    for i, name in enumerate(TWIN_WEIGHTS):
        w = inp[name].astype(_jnp.float32)
        if MOMENT_SCALE is None:
            s = _jnp.sqrt(_jnp.mean(_jnp.square(w)) + 1e-30)
        else:
            s = MOMENT_SCALE[name]
        km, kv = _jax.random.split(_jax.random.fold_in(key, i + 1))
        out[name] = w
        out["m_" + name] = s * _jax.random.normal(km, w.shape, _jnp.float32)
        out["v_" + name] = (s * s) * _jax.random.uniform(kv, w.shape, _jnp.float32, 0.5, 1.5)
    if N_MICROBATCH > 1:
        for name, axis in PER_EXAMPLE_BATCH_AXIS.items():
            out[name] = _to_microbatches(out[name], axis)
    return {'x': out['x'], 'norm_mix_g': out['norm_mix_g'], 'w_in': out['w_in'], 'conv_qkv_w': out['conv_qkv_w'], 'gdn_a_log': out['gdn_a_log'], 'gdn_dt_bias': out['gdn_dt_bias'], 'gdn_norm_g': out['gdn_norm_g'], 'mla_q_norm_g': out['mla_q_norm_g'], 'w_uq': out['w_uq'], 'mla_kv_norm_g': out['mla_kv_norm_g'], 'w_ukv': out['w_ukv'], 'w_o_gdn': out['w_o_gdn'], 'w_o_mla': out['w_o_mla'], 'w_out': out['w_out'], 'norm_ffn_g': out['norm_ffn_g'], 'w_up': out['w_up'], 'conv_ffn_w': out['conv_ffn_w'], 'w_down': out['w_down'], 'norm_final_g': out['norm_final_g'], 'loss_target': out['loss_target'], 'm_norm_mix_g': out['m_norm_mix_g'], 'm_w_in': out['m_w_in'], 'm_conv_qkv_w': out['m_conv_qkv_w'], 'm_gdn_a_log': out['m_gdn_a_log'], 'm_gdn_dt_bias': out['m_gdn_dt_bias'], 'm_gdn_norm_g': out['m_gdn_norm_g'], 'm_mla_q_norm_g': out['m_mla_q_norm_g'], 'm_w_uq': out['m_w_uq'], 'm_mla_kv_norm_g': out['m_mla_kv_norm_g'], 'm_w_ukv': out['m_w_ukv'], 'm_w_o_gdn': out['m_w_o_gdn'], 'm_w_o_mla': out['m_w_o_mla'], 'm_w_out': out['m_w_out'], 'm_norm_ffn_g': out['m_norm_ffn_g'], 'm_w_up': out['m_w_up'], 'm_conv_ffn_w': out['m_conv_ffn_w'], 'm_w_down': out['m_w_down'], 'm_norm_final_g': out['m_norm_final_g'], 'v_norm_mix_g': out['v_norm_mix_g'], 'v_w_in': out['v_w_in'], 'v_conv_qkv_w': out['v_conv_qkv_w'], 'v_gdn_a_log': out['v_gdn_a_log'], 'v_gdn_dt_bias': out['v_gdn_dt_bias'], 'v_gdn_norm_g': out['v_gdn_norm_g'], 'v_mla_q_norm_g': out['v_mla_q_norm_g'], 'v_w_uq': out['v_w_uq'], 'v_mla_kv_norm_g': out['v_mla_kv_norm_g'], 'v_w_ukv': out['v_w_ukv'], 'v_w_o_gdn': out['v_w_o_gdn'], 'v_w_o_mla': out['v_w_o_mla'], 'v_w_out': out['v_w_out'], 'v_norm_ffn_g': out['v_norm_ffn_g'], 'v_w_up': out['v_w_up'], 'v_conv_ffn_w': out['v_conv_ffn_w'], 'v_w_down': out['v_w_down'], 'v_norm_final_g': out['v_norm_final_g']}


def _loss(weights, diff, rest, loss_target):
    with _jax.named_scope("forward"):
        args = {**rest, TWIN_DIFF_INPUT: diff, **{k: w.astype(_WEIGHT_DTYPES[k]) for k, w in weights.items()}}
        y = _forward(args)
    with _jax.named_scope("loss_head"):
        err = _jnp.square(y.astype(_jnp.float32) - loss_target)
        return 0.5 * _jnp.sum(_jnp.mean(err, axis=-1)) if err.ndim else 0.5 * err


def _adamw(w, g, m, v):
    m = ADAM_B1 * m + (1.0 - ADAM_B1) * g
    v = ADAM_B2 * v + (1.0 - ADAM_B2) * _jnp.square(g)
    m_hat = m / (1.0 - ADAM_B1 ** ADAM_STEP)
    v_hat = v / (1.0 - ADAM_B2 ** ADAM_STEP)
    delta = -ADAM_LR * (m_hat / (_jnp.sqrt(v_hat) + ADAM_EPS) + ADAM_WD * w)
    return delta, m, v


def reference(x, norm_mix_g, w_in, conv_qkv_w, gdn_a_log, gdn_dt_bias, gdn_norm_g, mla_q_norm_g, w_uq, mla_kv_norm_g, w_ukv, w_o_gdn, w_o_mla, w_out, norm_ffn_g, w_up, conv_ffn_w, w_down, norm_final_g, loss_target, m_norm_mix_g, m_w_in, m_conv_qkv_w, m_gdn_a_log, m_gdn_dt_bias, m_gdn_norm_g, m_mla_q_norm_g, m_w_uq, m_mla_kv_norm_g, m_w_ukv, m_w_o_gdn, m_w_o_mla, m_w_out, m_norm_ffn_g, m_w_up, m_conv_ffn_w, m_w_down, m_norm_final_g, v_norm_mix_g, v_w_in, v_conv_qkv_w, v_gdn_a_log, v_gdn_dt_bias, v_gdn_norm_g, v_mla_q_norm_g, v_w_uq, v_mla_kv_norm_g, v_w_ukv, v_w_o_gdn, v_w_o_mla, v_w_out, v_norm_ffn_g, v_w_up, v_conv_ffn_w, v_w_down, v_norm_final_g):
    given = dict(x=x, norm_mix_g=norm_mix_g, w_in=w_in, conv_qkv_w=conv_qkv_w, gdn_a_log=gdn_a_log, gdn_dt_bias=gdn_dt_bias, gdn_norm_g=gdn_norm_g, mla_q_norm_g=mla_q_norm_g, w_uq=w_uq, mla_kv_norm_g=mla_kv_norm_g, w_ukv=w_ukv, w_o_gdn=w_o_gdn, w_o_mla=w_o_mla, w_out=w_out, norm_ffn_g=norm_ffn_g, w_up=w_up, conv_ffn_w=conv_ffn_w, w_down=w_down, norm_final_g=norm_final_g, loss_target=loss_target, m_norm_mix_g=m_norm_mix_g, m_w_in=m_w_in, m_conv_qkv_w=m_conv_qkv_w, m_gdn_a_log=m_gdn_a_log, m_gdn_dt_bias=m_gdn_dt_bias, m_gdn_norm_g=m_gdn_norm_g, m_mla_q_norm_g=m_mla_q_norm_g, m_w_uq=m_w_uq, m_mla_kv_norm_g=m_mla_kv_norm_g, m_w_ukv=m_w_ukv, m_w_o_gdn=m_w_o_gdn, m_w_o_mla=m_w_o_mla, m_w_out=m_w_out, m_norm_ffn_g=m_norm_ffn_g, m_w_up=m_w_up, m_conv_ffn_w=m_conv_ffn_w, m_w_down=m_w_down, m_norm_final_g=m_norm_final_g, v_norm_mix_g=v_norm_mix_g, v_w_in=v_w_in, v_conv_qkv_w=v_conv_qkv_w, v_gdn_a_log=v_gdn_a_log, v_gdn_dt_bias=v_gdn_dt_bias, v_gdn_norm_g=v_gdn_norm_g, v_mla_q_norm_g=v_mla_q_norm_g, v_w_uq=v_w_uq, v_mla_kv_norm_g=v_mla_kv_norm_g, v_w_ukv=v_w_ukv, v_w_o_gdn=v_w_o_gdn, v_w_o_mla=v_w_o_mla, v_w_out=v_w_out, v_norm_ffn_g=v_norm_ffn_g, v_w_up=v_w_up, v_conv_ffn_w=v_conv_ffn_w, v_w_down=v_w_down, v_norm_final_g=v_norm_final_g)
    weights = {n: given[n] for n in TWIN_WEIGHTS}
    shared = {n: given[n] for n in SHARED_INPUTS}
    per_example = {n: given[n] for n in ['x']}
    grad_fn = _jax.value_and_grad(_loss, argnums=(0, 1))

    def one_microbatch(ex, loss_target):
        ex = dict(ex)
        diff = ex.pop(TWIN_DIFF_INPUT)
        return grad_fn(weights, diff, {**shared, **ex}, loss_target)

    if N_MICROBATCH == 1:
        loss, (grad_w, grad_x) = one_microbatch(per_example, given["loss_target"])
    else:
        def body(carry, xs):
            loss_sum, grad_sum = carry
            l_k, (gw_k, gx_k) = one_microbatch(xs[0], xs[1])
            with _jax.named_scope("update"):
                return (loss_sum + l_k, _jax.tree.map(_jnp.add, grad_sum, gw_k)), gx_k

        init = (_jnp.zeros((), _jnp.float32), _jax.tree.map(_jnp.zeros_like, weights))
        (loss, grad_w), grad_x = _jax.lax.scan(body, init, (per_example, given["loss_target"]))
    with _jax.named_scope("update"):
        delta_w, new_m, new_v = {}, {}, {}
        for n in TWIN_WEIGHTS:
            delta_w[n], new_m[n], new_v[n] = _adamw(weights[n], grad_w[n], given["m_" + n], given["v_" + n])
    return (loss, grad_x, *[grad_w[n] for n in TWIN_WEIGHTS], *[delta_w[n] for n in TWIN_WEIGHTS],
            *[new_m[n] for n in TWIN_WEIGHTS], *[new_v[n] for n in TWIN_WEIGHTS])
```

```python
import functools
import math

import numpy as np
import jax
import jax.numpy as jnp
from jax import lax
from jax.experimental import pallas as pl
from jax.experimental.pallas import tpu as pltpu

F32 = jnp.float32
BF16 = jnp.bfloat16

D_MODEL = 1024
HEADS = 8
HD = 128
GDN_CONV = 4
CHUNK = 64
Q_RANK = 384
KV_RANK = 256
ROPE = 64
ROPE_THETA = 10000.0
D_FF = 2816
FFN_CONV = 3
EPS = 1e-6
SM_SCALE = (HD + ROPE) ** -0.5
N_DEV = 8

ADAM_LR, ADAM_B1, ADAM_B2, ADAM_EPS, ADAM_WD, ADAM_STEP = 0.001, 0.9, 0.999, 1e-08, 0.01, 10

LANES = 128
SUBLANES = 8
HALO = SUBLANES
VMEM_LIMIT = 56 * 1024 * 1024
HI = lax.Precision.HIGHEST

NN = (((1,), (0,)), ((), ()))
NT = (((1,), (1,)), ((), ()))
TN = (((0,), (0,)), ((), ()))


def _dot(a, b, dims=NN, precision=None):
    return lax.dot_general(a, b, dims, precision=precision, preferred_element_type=F32)


def _pick(dim, target, align):
    best = None
    for t in range(align, min(dim, target) + 1, align):
        if dim % t == 0:
            best = t
    return dim if best is None else best


def _call(body, ins, outs, grid, *, name, scratch=(), semantics=None):
    n_in, n_out = len(ins), len(outs)

    def kern(*refs):
        body(refs[:n_in], refs[n_in:n_in + n_out], refs[n_in + n_out:])

    res = pl.pallas_call(
        kern,
        grid=grid,
        in_specs=[pl.BlockSpec(bs, im) for _, bs, im in ins],
        out_specs=[pl.BlockSpec(bs, im) for _, _, bs, im in outs],
        out_shape=[jax.ShapeDtypeStruct(s, d) for s, d, _, _ in outs],
        scratch_shapes=list(scratch),
        name=name,
        compiler_params=pltpu.CompilerParams(
            dimension_semantics=semantics or ("arbitrary",) * len(grid), vmem_limit_bytes=VMEM_LIMIT),
    )(*[a for a, _, _ in ins])
    return res


def _mm(a, b, mode, out_dtype, *, name, add=None, tm=512, tn=1024, tk=1024):
    if mode == "nn":
        (M, K), (K2, N) = a.shape, b.shape
    elif mode == "nt":
        (M, K), (N, K2) = a.shape, b.shape
    else:
        (K, M), (K2, N) = a.shape, b.shape
    assert K == K2, (a.shape, b.shape, mode)
    tm = _pick(M, tm, LANES if mode == "tn" else 16)
    tn = _pick(N, tn, LANES)
    tk = _pick(K, tk, 16 if mode == "tn" else LANES)
    nk = K // tk
    dims = {"nn": NN, "nt": NT, "tn": TN}[mode]
    if mode == "nn":
        a_spec, b_spec = ((tm, tk), lambda i, j, k: (i, k)), ((tk, tn), lambda i, j, k: (k, j))
    elif mode == "nt":
        a_spec, b_spec = ((tm, tk), lambda i, j, k: (i, k)), ((tn, tk), lambda i, j, k: (j, k))
    else:
        a_spec, b_spec = ((tk, tm), lambda i, j, k: (k, i)), ((tk, tn), lambda i, j, k: (k, j))
    ins = [(a,) + a_spec, (b,) + b_spec]
    if add is not None:
        ins.append((add, (tm, tn), lambda i, j, k: (i, j)))
    outs = [((M, N), out_dtype, (tm, tn), lambda i, j, k: (i, j))]

    def body(in_refs, out_refs, scr):
        k = pl.program_id(2)
        acc = scr[0]

        @pl.when(k == 0)
        def _():
            acc[...] = jnp.zeros_like(acc)

        acc[...] += _dot(in_refs[0][...].astype(BF16), in_refs[1][...].astype(BF16), dims)

        @pl.when(k == nk - 1)
        def _():
            r = acc[...]
            if add is not None:
                r = r + in_refs[2][...].astype(F32)
            out_refs[0][...] = r.astype(out_dtype)

    return _call(body, ins, outs, (M // tm, N // tn, nk), name=name, scratch=[pltpu.VMEM((tm, tn), F32)],
                 semantics=("parallel", "parallel", "arbitrary"))[0]


def _row_call(fn, rows, consts, out_rows, out_accs=(), *, T, tm, name):
    nt = T // tm
    ins = []
    for r in rows:
        ins.append(r if isinstance(r, tuple) else (r, (tm, r.shape[1]), lambda i: (i, 0)))
    for c in consts:
        ins.append((c, c.shape, lambda i, nd=c.ndim: (0,) * nd))
    outs = []
    for o in out_rows:
        outs.append(((T, o[0]), o[1], (tm, o[0]), lambda i: (i, 0)) if len(o) == 2 else o)
    for shp, dt in out_accs:
        outs.append((shp, dt, shp, lambda i, nd=len(shp): (0,) * nd))
    n_r = len(out_rows)

    def body(in_refs, out_refs, _):
        i = pl.program_id(0)
        vals = fn(*[r[...] for r in in_refs])
        for o_ref, v in zip(out_refs[:n_r], vals[:n_r]):
            o_ref[...] = v.astype(o_ref.dtype)
        for o_ref, v in zip(out_refs[n_r:], vals[n_r:]):
            @pl.when(i == 0)
            def _(o_ref=o_ref):
                o_ref[...] = jnp.zeros_like(o_ref)

            o_ref[...] += v.astype(o_ref.dtype)

    return _call(body, ins, outs, (nt,), name=name)


def _rms(x, g):
    return x * lax.rsqrt(jnp.mean(x * x, axis=-1, keepdims=True) + EPS) * g


def _norm_fwd(x, g, *, T, tm, name):
    return _row_call(lambda xt, gt: (_rms(xt, gt),), [x], [g], [(x.shape[1], BF16)], T=T, tm=tm, name=name)[0]


def _norm_bwd(x, g, dh, dres, *, T, tm, name):
    def fn(xt, dht, drt, gt):
        _, vjp = jax.vjp(_rms, xt, gt)
        dx, dg = vjp(dht)
        return drt + dx, dg

    return _row_call(fn, [x, dh, dres], [g], [(x.shape[1], F32)], [(g.shape, F32)], T=T, tm=tm, name=name)


def _dwconv(tail, x, w):
    K, tm = w.shape[0], x.shape[0]
    xx = jnp.concatenate([tail, x], axis=0)
    acc = None
    for k in range(K):
        s = HALO - (K - 1) + k
        term = w[k:k + 1, :] * xx[s:s + tm, :]
        acc = term if acc is None else acc + term
    return acc


def _conv_fwd(fn, xs, ws, out_c, out_dtype, *, T, S, tm, cb, ncb, name):
    nt, tps, hb = T // tm, S // tm, tm // HALO
    ins = []
    for arr, off in xs:
        ins.append((arr, (tm, cb), lambda j, i, off=off: (i, off + j)))
        ins.append((arr, (HALO, cb), lambda j, i, off=off: (jnp.maximum(i * hb - 1, 0), off + j)))
    for arr, off in ws:
        ins.append((arr, (arr.shape[0], cb), lambda j, i, off=off: (0, off + j)))
    outs = [((T, out_c), out_dtype, (tm, cb), lambda j, i: (i, j))]
    nx = len(xs)

    def body(in_refs, out_refs, _):
        j, i = pl.program_id(0), pl.program_id(1)
        first = (i % tps) == 0
        xts = [in_refs[2 * m][...].astype(F32) for m in range(nx)]
        tails = [jnp.where(first, 0.0, in_refs[2 * m + 1][...].astype(F32)) for m in range(nx)]
        wts = [r[...] for r in in_refs[2 * nx:]]
        out_refs[0][...] = fn(j, tails, xts, wts).astype(out_dtype)

    return _call(body, ins, outs, (ncb, nt), name=name)[0]


def _conv_bwd(fn, xs, ws, dout, dx_dtype, *, T, S, tm, cb, ncb, name):
    nt, tps, hb = T // tm, S // tm, tm // HALO
    ins = []
    for arr, off in xs:
        ins.append((arr, (tm, cb), lambda j, i, off=off: (nt - 1 - i, off + j)))
        ins.append((arr, (HALO, cb), lambda j, i, off=off: (jnp.maximum((nt - 1 - i) * hb - 1, 0), off + j)))
    for arr, off in ws:
        ins.append((arr, (arr.shape[0], cb), lambda j, i, off=off: (0, off + j)))
    ins.append((dout, (tm, cb), lambda j, i: (nt - 1 - i, j)))
    nx, nw = len(xs), len(ws)
    outs = [((T, ncb * cb), dx_dtype, (tm, cb), lambda j, i: (nt - 1 - i, j)) for _ in xs]
    outs += [((arr.shape[0], ncb * cb), F32, (arr.shape[0], cb), lambda j, i: (0, j)) for arr, _ in ws]
    scratch = [pltpu.VMEM((HALO, cb), F32) for _ in xs]

    def body(in_refs, out_refs, carry):
        j, i = pl.program_id(0), pl.program_id(1)
        r = nt - 1 - i
        first = (r % tps) == 0
        xts = [in_refs[2 * m][...].astype(F32) for m in range(nx)]
        tails = [jnp.where(first, 0.0, in_refs[2 * m + 1][...].astype(F32)) for m in range(nx)]
        wts = [ref[...] for ref in in_refs[2 * nx:2 * nx + nw]]
        _, vjp = jax.vjp(lambda tl, xt, wt: fn(j, tl, xt, wt), tails, xts, wts)
        dtails, dxts, dwts = vjp(in_refs[-1][...].astype(F32))

        @pl.when(i == 0)
        def _():
            for c in carry:
                c[...] = jnp.zeros_like(c)

        for m in range(nx):
            pad = jnp.concatenate([jnp.zeros((tm - HALO, cb), F32), carry[m][...]], axis=0)
            out_refs[m][...] = (dxts[m] + pad).astype(dx_dtype)
            carry[m][...] = jnp.where(first, 0.0, dtails[m])
        for m in range(nw):
            o_ref = out_refs[nx + m]

            @pl.when(i == 0)
            def _(o_ref=o_ref):
                o_ref[...] = jnp.zeros_like(o_ref)

            o_ref[...] += dwts[m]

    return _call(body, ins, outs, (ncb, nt), name=name, scratch=scratch)


QKV_CB = 512


def _qkv_fn(j, tails, xts, wts):
    y = jax.nn.silu(_dwconv(tails[0], xts[0], wts[0]))
    scale = jnp.where(j < 2, HD ** -0.5, 1.0)
    parts = []
    for h in range(QKV_CB // HD):
        yh = y[:, h * HD:(h + 1) * HD]
        nh = yh * lax.rsqrt(jnp.sum(yh * yh, axis=-1, keepdims=True) + EPS)
        parts.append(jnp.where(j < 4, nh * scale, yh))
    return jnp.concatenate(parts, axis=1)


def _ffn_fn(j, tails, xts, wts):
    return jax.nn.silu(_dwconv(tails[0], xts[0], wts[0])) * _dwconv(tails[1], xts[1], wts[1])


def _tri_inv(L):
    C = L.shape[0]
    ii = lax.broadcasted_iota(jnp.int32, (C, C), 0)
    jj = lax.broadcasted_iota(jnp.int32, (C, C), 1)
    eye = (ii == jj).astype(F32)
    X = eye - jnp.where((ii >> 1) == (jj >> 1), L, 0.0)
    s = 1
    while (2 << s) <= C:
        E = jnp.where(((ii >> (s + 1)) == (jj >> (s + 1))) & ((ii >> s) != (jj >> s)), L, 0.0)
        X = X - _dot(_dot(X, E, precision=HI), X, precision=HI)
        s += 1
    return X


def _gdn_chunk(q, k, v, gc, gr, beta, S):
    C = q.shape[0]
    ii = lax.broadcasted_iota(jnp.int32, (C, C), 0)
    jj = lax.broadcasted_iota(jnp.int32, (C, C), 1)
    lower = ii >= jj
    decay = jnp.where(lower, jnp.exp(jnp.where(lower, gc - gr, 0.0)), 0.0)
    kb, vb = k * beta, v * beta
    L = jnp.where(ii > jj, _dot(kb, k, NT) * decay, 0.0)
    Tinv = _tri_inv(L)
    eg = jnp.exp(gc)
    u = _dot(Tinv, vb, precision=HI)
    w = _dot(Tinv, kb * eg, precision=HI)
    a = _dot(q, k, NT) * decay
    g_last = gc[C - 1:C, :]
    kd = k * jnp.exp(g_last - gc)
    v_new = u - _dot(w, S)
    o = _dot(q * eg, S) + _dot(a, v_new)
    S_new = S * jnp.exp(g_last) + _dot(kd, v_new, TN)
    return o, S_new


def _gdn_fwd(qkvn, gcum, grT, beta, *, B, S):
    N, T = S // CHUNK, B * S
    row = lambda c: (lambda b, n: (b * N + n, c))
    ins = [(qkvn, (CHUNK, 1024), row(0)), (qkvn, (CHUNK, 1024), row(1)), (qkvn, (CHUNK, 1024), row(2)),
           (gcum, (CHUNK, LANES), row(0)), (grT, (1, HEADS, CHUNK), lambda b, n: (b * N + n, 0, 0)),
           (beta, (CHUNK, LANES), row(0))]
    outs = [((T, 1024), F32, (CHUNK, 1024), row(0)),
            ((B * N, HEADS, HD, HD), F32, (1, HEADS, HD, HD), lambda b, n: (b * N + n, 0, 0, 0))]

    def body(in_refs, out_refs, scr):
        q_ref, k_ref, v_ref, gc_ref, gr_ref, b_ref = in_refs
        o_ref, st_ref = out_refs
        S_ref = scr[0]

        @pl.when(pl.program_id(1) == 0)
        def _():
            S_ref[...] = jnp.zeros_like(S_ref)

        for h in range(HEADS):
            sl = slice(h * HD, (h + 1) * HD)
            Sh = S_ref[h]
            st_ref[0, h] = Sh
            o, Sn = _gdn_chunk(q_ref[:, sl], k_ref[:, sl], v_ref[:, sl], gc_ref[:, h:h + 1],
                               gr_ref[0, h:h + 1, :], b_ref[:, h:h + 1], Sh)
            o_ref[:, sl] = o
            S_ref[h] = Sn

    return _call(body, ins, outs, (B, N), name="gdn_core_fwd", scratch=[pltpu.VMEM((HEADS, HD, HD), F32)])


def _gdn_bwd(qkvn, gcum, grT, beta, states, do, *, B, S):
    N, T = S // CHUNK, B * S
    row = lambda c: (lambda b, n: (b * N + N - 1 - n, c))
    ins = [(qkvn, (CHUNK, 1024), row(0)), (qkvn, (CHUNK, 1024), row(1)), (qkvn, (CHUNK, 1024), row(2)),
           (gcum, (CHUNK, LANES), row(0)), (grT, (1, HEADS, CHUNK), lambda b, n: (b * N + N - 1 - n, 0, 0)),
           (beta, (CHUNK, LANES), row(0)),
           (states, (1, HEADS, HD, HD), lambda b, n: (b * N + N - 1 - n, 0, 0, 0)), (do, (CHUNK, 1024), row(0))]
    outs = [((T, 3072), F32, (CHUNK, 3072), row(0)), ((T, LANES), F32, (CHUNK, LANES), row(0)),
            ((B * N, HEADS, CHUNK), F32, (1, HEADS, CHUNK), lambda b, n: (b * N + N - 1 - n, 0, 0)),
            ((T, LANES), F32, (CHUNK, LANES), row(0))]

    def body(in_refs, out_refs, scr):
        q_ref, k_ref, v_ref, gc_ref, gr_ref, b_ref, st_ref, do_ref = in_refs
        dqkv_ref, dgc_ref, dgr_ref, db_ref = out_refs
        dS_ref = scr[0]

        @pl.when(pl.program_id(1) == 0)
        def _():
            dS_ref[...] = jnp.zeros_like(dS_ref)

        lane = lax.broadcasted_iota(jnp.int32, (CHUNK, LANES), 1)
        dgc_all = jnp.zeros((CHUNK, LANES), F32)
        db_all = jnp.zeros((CHUNK, LANES), F32)
        for h in range(HEADS):
            sl = slice(h * HD, (h + 1) * HD)
            args = (q_ref[:, sl], k_ref[:, sl], v_ref[:, sl], gc_ref[:, h:h + 1], gr_ref[0, h:h + 1, :],
                    b_ref[:, h:h + 1], st_ref[0, h])
            _, vjp = jax.vjp(_gdn_chunk, *args)
            dq, dk, dv, dgc, dgr, db, dS = vjp((do_ref[:, sl], dS_ref[h]))
            dqkv_ref[:, sl] = dq
            dqkv_ref[:, 1024 + h * HD:1024 + (h + 1) * HD] = dk
            dqkv_ref[:, 2048 + h * HD:2048 + (h + 1) * HD] = dv
            dgc_all = jnp.where(lane == h, dgc, dgc_all)
            db_all = jnp.where(lane == h, db, db_all)
            dgr_ref[0, h:h + 1, :] = dgr
            dS_ref[h] = dS
        dgc_ref[...] = dgc_all
        db_ref[...] = db_all

    return _call(body, ins, outs, (B, N), name="gdn_core_bwd", scratch=[pltpu.VMEM((HEADS, HD, HD), F32)])


def _gate_fn(za, zb, alog, dtb):
    tm = za.shape[0]
    g = -jnp.exp(alog) * jax.nn.softplus(za + dtb)
    ii = lax.broadcasted_iota(jnp.int32, (tm, tm), 0)
    jj = lax.broadcasted_iota(jnp.int32, (tm, tm), 1)
    tri = ((ii >= jj) & ((ii >> 6) == (jj >> 6))).astype(F32)
    return _dot(tri, g, precision=HI), jax.nn.sigmoid(zb)


def _causal(s, diag):
    t = s.shape[0]
    ii = lax.broadcasted_iota(jnp.int32, (t, t), 0)
    jj = lax.broadcasted_iota(jnp.int32, (t, t), 1)
    return jnp.where(jnp.logical_or(jnp.logical_not(diag), ii >= jj), s, -jnp.inf)


def _flash_fwd(qn, qp, kn, kp, v, *, B, S, t):
    nb, T = S // t, B * S
    qmap = lambda b, h, qi, ki: (b * nb + qi, h)
    kmap = lambda b, h, qi, ki: (b * nb + jnp.minimum(ki, qi), h)
    kpmap = lambda b, h, qi, ki: (b * nb + jnp.minimum(ki, qi), 0)
    ins = [(qn, (t, HD), qmap), (qp, (t, HD), qmap), (kn, (t, HD), kmap), (kp, (t, HD), kpmap), (v, (t, HD), kmap)]
    outs = [((T, 1024), BF16, (t, HD), qmap),
            ((HEADS, T, 1), F32, (1, t, 1), lambda b, h, qi, ki: (h, b * nb + qi, 0))]
    scratch = [pltpu.VMEM((t, 1), F32), pltpu.VMEM((t, 1), F32), pltpu.VMEM((t, HD), F32)]

    def body(in_refs, out_refs, scr):
        qn_ref, qp_ref, kn_ref, kp_ref, v_ref = in_refs
        o_ref, lse_ref = out_refs
        m_ref, l_ref, acc_ref = scr
        qi, ki = pl.program_id(2), pl.program_id(3)

        @pl.when(ki == 0)
        def _():
            m_ref[...] = jnp.full_like(m_ref, -jnp.inf)
            l_ref[...] = jnp.zeros_like(l_ref)
            acc_ref[...] = jnp.zeros_like(acc_ref)

        @pl.when(ki <= qi)
        def _():
            q = jnp.concatenate([qn_ref[...], qp_ref[...]], axis=1)
            k = jnp.concatenate([kn_ref[...], kp_ref[...]], axis=1)
            s = _causal(_dot(q, k, NT) * SM_SCALE, ki == qi)
            m_old = m_ref[...]
            m_new = jnp.maximum(m_old, jnp.max(s, axis=-1, keepdims=True))
            p = jnp.exp(s - m_new)
            alpha = jnp.exp(m_old - m_new)
            l_ref[...] = alpha * l_ref[...] + jnp.sum(p, axis=-1, keepdims=True)
            acc_ref[...] = alpha * acc_ref[...] + _dot(p.astype(BF16), v_ref[...])
            m_ref[...] = m_new

        @pl.when(ki == qi)
        def _():
            o_ref[...] = (acc_ref[...] / l_ref[...]).astype(BF16)
            lse_ref[0] = m_ref[...] + jnp.log(l_ref[...])

    return _call(body, ins, outs, (B, HEADS, nb, nb), name="mla_flash_fwd", scratch=scratch,
                 semantics=("parallel", "parallel", "parallel", "arbitrary"))


def _flash_bwd_dq(qn, qp, kn, kp, v, o, do, lse, *, B, S, t):
    nb, T = S // t, B * S
    qmap = lambda b, h, qi, ki: (b * nb + qi, h)
    kmap = lambda b, h, qi, ki: (b * nb + jnp.minimum(ki, qi), h)
    kpmap = lambda b, h, qi, ki: (b * nb + jnp.minimum(ki, qi), 0)
    ins = [(qn, (t, HD), qmap), (qp, (t, HD), qmap), (kn, (t, HD), kmap), (kp, (t, HD), kpmap), (v, (t, HD), kmap),
           (o, (t, HD), qmap), (do, (t, HD), qmap), (lse, (1, t, 1), lambda b, h, qi, ki: (h, b * nb + qi, 0))]
    outs = [((T, 1024), BF16, (t, HD), qmap), ((T, 1024), F32, (t, HD), qmap)]
    scratch = [pltpu.VMEM((t, 1), F32), pltpu.VMEM((t, 2 * HD), F32)]

    def body(in_refs, out_refs, scr):
        qn_ref, qp_ref, kn_ref, kp_ref, v_ref, o_ref, do_ref, lse_ref = in_refs
        dqn_ref, dqp_ref = out_refs
        dl_ref, acc_ref = scr
        qi, ki = pl.program_id(2), pl.program_id(3)

        @pl.when(ki == 0)
        def _():
            dl_ref[...] = jnp.sum(do_ref[...].astype(F32) * o_ref[...].astype(F32), axis=-1, keepdims=True)
            acc_ref[...] = jnp.zeros_like(acc_ref)

        @pl.when(ki <= qi)
        def _():
            q = jnp.concatenate([qn_ref[...], qp_ref[...]], axis=1)
            k = jnp.concatenate([kn_ref[...], kp_ref[...]], axis=1)
            s = _causal(_dot(q, k, NT) * SM_SCALE, ki == qi)
            p = jnp.exp(s - lse_ref[0])
            dp = _dot(do_ref[...], v_ref[...], NT)
            ds = p * (dp - dl_ref[...]) * SM_SCALE
            acc_ref[...] += _dot(ds.astype(BF16), k)

        @pl.when(ki == qi)
        def _():
            dqn_ref[...] = acc_ref[:, :HD].astype(BF16)
            dqp_ref[...] = acc_ref[:, HD:]

    return _call(body, ins, outs, (B, HEADS, nb, nb), name="mla_flash_bwd_dq", scratch=scratch,
                 semantics=("parallel", "parallel", "parallel", "arbitrary"))


def _flash_bwd_dkv(qn, qp, kn, kp, v, o, do, lse, *, B, S, t):
    nb, T = S // t, B * S
    qmap = lambda b, h, ki, qi: (b * nb + jnp.maximum(qi, ki), h)
    kmap = lambda b, h, ki, qi: (b * nb + ki, h)
    ins = [(qn, (t, HD), qmap), (qp, (t, HD), qmap), (kn, (t, HD), kmap),
           (kp, (t, HD), lambda b, h, ki, qi: (b * nb + ki, 0)), (v, (t, HD), kmap),
           (o, (t, HD), qmap), (do, (t, HD), qmap),
           (lse, (1, t, 1), lambda b, h, ki, qi: (h, b * nb + jnp.maximum(qi, ki), 0))]
    outs = [((T, 1024), BF16, (t, HD), kmap), ((HEADS, T, HD), F32, (1, t, HD), lambda b, h, ki, qi: (h, b * nb + ki, 0)),
            ((T, 1024), BF16, (t, HD), kmap)]
    scratch = [pltpu.VMEM((t, 2 * HD), F32), pltpu.VMEM((t, HD), F32)]

    def body(in_refs, out_refs, scr):
        qn_ref, qp_ref, kn_ref, kp_ref, v_ref, o_ref, do_ref, lse_ref = in_refs
        dkn_ref, dkp_ref, dv_ref = out_refs
        dk_acc, dv_acc = scr
        ki, qi = pl.program_id(2), pl.program_id(3)

        @pl.when(qi == 0)
        def _():
            dk_acc[...] = jnp.zeros_like(dk_acc)
            dv_acc[...] = jnp.zeros_like(dv_acc)

        @pl.when(qi >= ki)
        def _():
            q = jnp.concatenate([qn_ref[...], qp_ref[...]], axis=1)
            k = jnp.concatenate([kn_ref[...], kp_ref[...]], axis=1)
            do_t = do_ref[...]
            s = _causal(_dot(q, k, NT) * SM_SCALE, ki == qi)
            p = jnp.exp(s - lse_ref[0])
            dl = jnp.sum(do_t.astype(F32) * o_ref[...].astype(F32), axis=-1, keepdims=True)
            dp = _dot(do_t, v_ref[...], NT)
            ds = p * (dp - dl) * SM_SCALE
            dv_acc[...] += _dot(p.astype(BF16), do_t, TN)
            dk_acc[...] += _dot(ds.astype(BF16), q, TN)

        @pl.when(qi == nb - 1)
        def _():
            dkn_ref[...] = dk_acc[:, :HD].astype(BF16)
            dkp_ref[0] = dk_acc[:, HD:]
            dv_ref[...] = dv_acc[...].astype(BF16)

    return _call(body, ins, outs, (B, HEADS, nb, nb), name="mla_flash_bwd_dkv", scratch=scratch,
                 semantics=("parallel", "parallel", "parallel", "arbitrary"))


def _allgather(x_shard, *, name):
    m_per, n = x_shard.shape

    def body(x_ref, out_ref, send_sems, recv_sems, local_sem):
        x, y, c = lax.axis_index("x"), lax.axis_index("y"), lax.axis_index("c")
        me, sibling = (x, y, c), (x, y, 1 - c)
        chips = [(1 - x, y), (x, 1 - y), (1 - x, 1 - y)]

        def rows(px, py, pc):
            return out_ref.at[pl.ds((4 * px + 2 * py + pc) * m_per, m_per), :]

        def copy(k, block, to, src=None):
            return pltpu.make_async_remote_copy(
                src_ref=rows(*block) if src is None else src, dst_ref=rows(*block),
                send_sem=send_sems.at[k], recv_sem=recv_sems.at[k], device_id=to, device_id_type=pl.DeviceIdType.MESH)

        mine = pltpu.make_async_copy(x_ref, rows(*me), local_sem)
        mine.start()
        first = [copy(0, me, sibling, src=x_ref)]
        first += [copy(1 + j, me, (*chip, c), src=x_ref) for j, chip in enumerate(chips)]
        for cp in first:
            cp.start()
        passed = [copy(4 + j, (*chip, c), sibling) for j, chip in enumerate(chips)]
        for j, chip in enumerate(chips):
            copy(1 + j, (*chip, c), me).wait_recv()
            passed[j].start()
        copy(0, sibling, me).wait_recv()
        for j, chip in enumerate(chips):
            copy(4 + j, (*chip, 1 - c), me).wait_recv()
        for cp in first + passed:
            cp.wait_send()
        mine.wait()

    return pl.pallas_call(
        body,
        out_shape=jax.ShapeDtypeStruct((N_DEV * m_per, n), x_shard.dtype),
        in_specs=[pl.BlockSpec(memory_space=pl.ANY)],
        out_specs=pl.BlockSpec(memory_space=pl.ANY),
        scratch_shapes=[pltpu.SemaphoreType.DMA((7,)), pltpu.SemaphoreType.DMA((7,)), pltpu.SemaphoreType.DMA],
        name=name,
    )(x_shard)


def _alltoall(send, *, name):
    m_per, n = send.shape[0] // N_DEV, send.shape[1]

    def body(s_ref, r_ref, send_sems, recv_sems, local_sem):
        x, y, c = lax.axis_index("x"), lax.axis_index("y"), lax.axis_index("c")
        me = 4 * x + 2 * y + c

        def rows(ref, idx):
            return ref.at[pl.ds(idx * m_per, m_per), :]

        local = pltpu.make_async_copy(rows(s_ref, me), rows(r_ref, me), local_sem)
        local.start()
        copies = []
        for k in range(1, N_DEV):
            px = 1 - x if k & 4 else x
            py = 1 - y if k & 2 else y
            pc = 1 - c if k & 1 else c
            cp = pltpu.make_async_remote_copy(
                src_ref=rows(s_ref, 4 * px + 2 * py + pc), dst_ref=rows(r_ref, me),
                send_sem=send_sems.at[k - 1], recv_sem=recv_sems.at[k - 1],
                device_id=(px, py, pc), device_id_type=pl.DeviceIdType.MESH)
            cp.start()
            copies.append(cp)
        for cp in copies:
            cp.wait()
        local.wait()

    return pl.pallas_call(
        body,
        out_shape=jax.ShapeDtypeStruct(send.shape, send.dtype),
        in_specs=[pl.BlockSpec(memory_space=pl.ANY)],
        out_specs=pl.BlockSpec(memory_space=pl.ANY),
        scratch_shapes=[pltpu.SemaphoreType.DMA((7,)), pltpu.SemaphoreType.DMA((7,)), pltpu.SemaphoreType.DMA],
        name=name,
    )(send)


def _reduce_adam(parts, w, m, v, *, tr, name):
    R = w.shape[0]
    nR = R // tr
    ins = [(parts, (tr, LANES), lambda i, s=s: (s * nR + i, 0)) for s in range(N_DEV)]
    ins += [(a, (tr, LANES), lambda i: (i, 0)) for a in (w, m, v)]
    outs = [((R, LANES), F32, (tr, LANES), lambda i: (i, 0)) for _ in range(4)]
    c1 = 1.0 - ADAM_B1 ** ADAM_STEP
    c2 = 1.0 - ADAM_B2 ** ADAM_STEP

    def body(in_refs, out_refs, _):
        g = in_refs[0][...].astype(F32)
        for s in range(1, N_DEV):
            g = g + in_refs[s][...].astype(F32)
        wv, mv, vv = in_refs[8][...], in_refs[9][...], in_refs[10][...]
        mn = ADAM_B1 * mv + (1.0 - ADAM_B1) * g
        vn = ADAM_B2 * vv + (1.0 - ADAM_B2) * (g * g)
        delta = -ADAM_LR * ((mn / c1) / (jnp.sqrt(vn / c2) + ADAM_EPS) + ADAM_WD * wv)
        out_refs[0][...] = g
        out_refs[1][...] = delta
        out_refs[2][...] = mn
        out_refs[3][...] = vn

    return _call(body, ins, outs, (nR,), name=name, semantics=("parallel",))


BIG = [("w_in", 1, 1024, 858), ("conv_qkv_w", 1, 4, 384), ("w_uq", 1, 384, 192), ("w_ukv", 1, 256, 256),
       ("w_o_gdn", 0, 128, 1024), ("w_o_mla", 0, 128, 1024), ("w_out", 0, 128, 1024), ("w_up", 1, 1024, 704),
       ("conv_ffn_w", 1, 3, 704), ("w_down", 0, 352, 1024)]
SMALL = [("norm_mix_g", 1024), ("gdn_a_log", 8), ("gdn_dt_bias", 8), ("gdn_norm_g", 128), ("mla_q_norm_g", 384),
         ("mla_kv_norm_g", 256), ("norm_ffn_g", 1024), ("norm_final_g", 1024)]
PACK_TR = 512
BIG_ELEMS = sum(r * c for _, _, r, c in BIG)
BIG_ROWS = -(-BIG_ELEMS // (LANES * PACK_TR)) * PACK_TR
SMALL_ROWS = 32
CONV_ROWS = 32
WEIGHT_ORDER = ["norm_mix_g", "w_in", "conv_qkv_w", "gdn_a_log", "gdn_dt_bias", "gdn_norm_g", "mla_q_norm_g", "w_uq",
                "mla_kv_norm_g", "w_ukv", "w_o_gdn", "w_o_mla", "w_out", "norm_ffn_g", "w_up", "conv_ffn_w", "w_down",
                "norm_final_g"]


def _pack_flat(arrs, rows, dtype):
    flat = jnp.concatenate([a.reshape(-1).astype(dtype) for a in arrs])
    return jnp.pad(flat, (0, rows * LANES - flat.shape[0])).reshape(rows, LANES)


def _unpack_local_big(buf):
    flat, out, off = buf.reshape(-1), {}, 0
    for name, _, r, c in BIG:
        out[name] = flat[off:off + r * c].reshape(1, r, c)
        off += r * c
    return out


def _unpack_gathered_big(g):
    g, out, off = g.reshape(N_DEV, -1), {}, 0
    for name, ax, r, c in BIG:
        seg = g[:, off:off + r * c].reshape(N_DEV, r, c)
        out[name] = seg.reshape(N_DEV * r, c) if ax == 0 else seg.transpose(1, 0, 2).reshape(r, N_DEV * c)
        off += r * c
    return out


def _pack_grads_big(full, dtype):
    parts = []
    for name, ax, r, c in BIG:
        gfull = full[name]
        seg = gfull.reshape(N_DEV, r * c) if ax == 0 else gfull.reshape(r, N_DEV, c).transpose(1, 0, 2).reshape(N_DEV, r * c)
        parts.append(seg.astype(dtype))
    flat = jnp.concatenate(parts, axis=1)
    flat = jnp.pad(flat, ((0, 0), (0, BIG_ROWS * LANES - flat.shape[1])))
    return flat.reshape(N_DEV * BIG_ROWS, LANES)


def _unpack_small(buf, shapes):
    flat, out, off = buf.reshape(-1), {}, 0
    for name, n in SMALL:
        out[name] = flat[off:off + n].reshape(shapes[name])
        off += n
    return out


def _rot_cols(w):
    h = ROPE // 2
    return jnp.concatenate([-w[:, h:], w[:, :h]], axis=1)


def _unrot_cols(dw):
    h = ROPE // 2
    return jnp.concatenate([dw[:, h:], -dw[:, :h]], axis=1)


def _padc(w, n):
    return jnp.pad(w, ((0, 0), (0, n - w.shape[1])))


def _layout_weights(full):
    w_in = full["w_in"]
    o = [0, 3072, 4096, 4104, 4112, 4496, 4752, 4816, 5840, 6864]
    kpe = w_in[:, o[6]:o[7]]
    W = {
        "in_qkv": w_in[:, o[0]:o[1]],
        "in_ga": w_in[:, o[1]:o[2]],
        "in_ab": jnp.concatenate([_padc(w_in[:, o[2]:o[3]], LANES), _padc(w_in[:, o[3]:o[4]], LANES)], axis=1),
        "in_small": jnp.concatenate([w_in[:, o[4]:o[5]], w_in[:, o[5]:o[6]], _padc(kpe, LANES),
                                     _padc(_rot_cols(kpe), LANES)], axis=1),
        "in_gbr": w_in[:, o[7]:o[9]],
    }
    uq = full["w_uq"].reshape(Q_RANK, HEADS, HD + ROPE)
    W["uq_n"] = uq[:, :, :HD].reshape(Q_RANK, HEADS * HD)
    pe = uq[:, :, HD:]
    rot = jnp.concatenate([-pe[:, :, ROPE // 2:], pe[:, :, :ROPE // 2]], axis=2)
    pad3 = lambda a: jnp.pad(a, ((0, 0), (0, 0), (0, HD - ROPE))).reshape(Q_RANK, HEADS * HD)
    W["uq_p"] = jnp.concatenate([pad3(pe), pad3(rot)], axis=1)
    ukv = full["w_ukv"].reshape(KV_RANK, HEADS, 2, HD)
    W["ukv_k"] = ukv[:, :, 0].reshape(KV_RANK, HEADS * HD)
    W["ukv_v"] = ukv[:, :, 1].reshape(KV_RANK, HEADS * HD)
    for n in ("w_o_gdn", "w_o_mla", "w_out", "w_up", "w_down"):
        W[n] = full[n]
    return {k: v.astype(BF16) for k, v in W.items()}


def _unlayout_grads(dW):
    s = dW["in_small"]
    dkpe = s[:, 640:704] + _unrot_cols(s[:, 768:832])
    g = {"w_in": jnp.concatenate([dW["in_qkv"], dW["in_ga"], dW["in_ab"][:, 0:8], dW["in_ab"][:, 128:136],
                                  s[:, 0:384], s[:, 384:640], dkpe, dW["in_gbr"]], axis=1)}
    dn = dW["uq_n"].reshape(Q_RANK, HEADS, HD)
    dp = dW["uq_p"].reshape(Q_RANK, 2, HEADS, HD)
    dlin, drot = dp[:, 0, :, :ROPE], dp[:, 1, :, :ROPE]
    dpe = dlin + jnp.concatenate([drot[:, :, ROPE // 2:], -drot[:, :, :ROPE // 2]], axis=2)
    g["w_uq"] = jnp.concatenate([dn, dpe], axis=2).reshape(Q_RANK, HEADS * (HD + ROPE))
    g["w_ukv"] = jnp.stack([dW["ukv_k"].reshape(KV_RANK, HEADS, HD), dW["ukv_v"].reshape(KV_RANK, HEADS, HD)],
                           axis=2).reshape(KV_RANK, 2 * HEADS * HD)
    for n in ("w_o_gdn", "w_o_mla", "w_out", "w_up", "w_down"):
        g[n] = dW[n]
    return g


def _rope_tables(S):
    half = ROPE // 2
    inv = ROPE_THETA ** (-jnp.arange(half, dtype=F32) / half)
    ang = jnp.arange(S, dtype=F32)[:, None] * inv[None, :]
    cos = jnp.concatenate([jnp.cos(ang), jnp.cos(ang)], axis=1)
    sin = jnp.concatenate([jnp.sin(ang), jnp.sin(ang)], axis=1)
    return _padc(cos, HD), _padc(sin, HD)


def _local_step(x, tgt, W, conv_qkv_w, conv_ffn_w, small, tm=None, ta=None):
    B, S, _ = x.shape
    T = B * S
    tm = tm or _pick(S, 256, CHUNK)
    ta = ta or _pick(S, 512, LANES)
    x2d, tgt2d = x.reshape(T, D_MODEL), tgt.reshape(T, D_MODEL)
    row = lambda v: v.reshape(1, -1).astype(F32)
    pad_row = lambda v: _padc(row(v), LANES)
    g_mix, g_ffn, g_fin = row(small["norm_mix_g"]), row(small["norm_ffn_g"]), row(small["norm_final_g"])
    g_gdn, g_q, g_kv = row(small["gdn_norm_g"]), row(small["mla_q_norm_g"]), row(small["mla_kv_norm_g"])
    alog, dtb = pad_row(small["gdn_a_log"]), pad_row(small["gdn_dt_bias"])
    cos, sin = _rope_tables(S)
    tps = S // tm
    tab = lambda a: (a, (tm, HD), lambda i: (i % tps, 0))
    col = lambda a, c, w: (a, (tm, w), lambda i, c=c: (i, c))

    h1 = _norm_fwd(x2d, g_mix, T=T, tm=tm, name="norm_mix_fwd")
    z_qkv = _mm(h1, W["in_qkv"], "nn", F32, name="in_qkv_fwd")
    z_ga = _mm(h1, W["in_ga"], "nn", F32, name="in_ga_fwd")
    z_ab = _mm(h1, W["in_ab"], "nn", F32, name="in_ab_fwd")
    z_small = _mm(h1, W["in_small"], "nn", F32, name="in_small_fwd", tn=896)
    z_gbr = _mm(h1, W["in_gbr"], "nn", F32, name="in_gbr_fwd")

    qkvn = _conv_fwd(_qkv_fn, [(z_qkv, 0)], [(conv_qkv_w, 0)], 3072, F32, T=T, S=S, tm=tm, cb=QKV_CB,
                     ncb=3072 // QKV_CB, name="gdn_qkv_fwd")
    gcum, beta = _row_call(lambda za, zb, al, db: _gate_fn(za, zb, al, db), [col(z_ab, 0, LANES), col(z_ab, 1, LANES)],
                           [alog, dtb], [(LANES, F32), (LANES, F32)], T=T, tm=tm, name="gdn_gate_fwd")
    grT = gcum[:, :HEADS].reshape(T // CHUNK, CHUNK, HEADS).transpose(0, 2, 1)
    o_gdn, states = _gdn_fwd(qkvn, gcum, grT, beta, B=B, S=S)

    def gdn_out_fn(o, ga, g):
        parts = []
        for h in range(HEADS):
            sl = slice(h * HD, (h + 1) * HD)
            parts.append(_rms(o[:, sl], g) * jax.nn.silu(ga[:, sl]))
        return jnp.concatenate(parts, axis=1)

    oa = _row_call(lambda o, ga, g: (gdn_out_fn(o, ga, g),), [o_gdn, z_ga], [g_gdn], [(1024, BF16)], T=T, tm=tm,
                   name="gdn_out_fwd")[0]

    def mla_prep_fn(zq, zkv, zpl, zpr, c, s, gq, gkv):
        return _rms(zq, gq), _rms(zkv, gkv), zpl * c + zpr * s

    small_cols = [(z_small, (tm, Q_RANK), lambda i: (i, 0)), (z_small, (tm, LANES), lambda i: (i, 3)),
                  (z_small, (tm, LANES), lambda i: (i, 4)), (z_small, (tm, LANES), lambda i: (i, 5)),
                  (z_small, (tm, LANES), lambda i: (i, 6))]

    def mla_prep_fwd(zq, zkv0, zkv1, zpl, zpr, c, s, gq, gkv):
        return mla_prep_fn(zq, jnp.concatenate([zkv0, zkv1], axis=1), zpl, zpr, c, s, gq, gkv)

    cq, ckv, kpe = _row_call(mla_prep_fwd, small_cols + [tab(cos), tab(sin)], [g_q, g_kv],
                             [(Q_RANK, BF16), (KV_RANK, BF16), (HD, BF16)], T=T, tm=tm, name="mla_prep_fwd")
    qn = _mm(cq, W["uq_n"], "nn", BF16, name="uq_n_fwd")
    qpl = _mm(cq, W["uq_p"], "nn", F32, name="uq_p_fwd")
    kn = _mm(ckv, W["ukv_k"], "nn", BF16, name="ukv_k_fwd")
    vb = _mm(ckv, W["ukv_v"], "nn", BF16, name="ukv_v_fwd")

    def qrope_fn(lin, rot, c, s):
        return lin * jnp.tile(c, (1, HEADS)) + rot * jnp.tile(s, (1, HEADS))

    qp = _row_call(lambda lin, rot, c, s: (qrope_fn(lin, rot, c, s),), [col(qpl, 0, 1024), col(qpl, 1, 1024), tab(cos), tab(sin)],
                   [], [(1024, BF16)], T=T, tm=tm, name="q_rope_fwd")[0]
    ob, lse = _flash_fwd(qn, qp, kn, kpe, vb, B=B, S=S, t=ta)

    def merge_fn(ya, yb, ga, gb):
        return jax.nn.sigmoid(ga) * ya + jax.nn.sigmoid(gb) * yb

    def merge_fwd(oat, obt, ga, gb, wog, wom):
        ya, yb = _dot(oat, wog), _dot(obt, wom)
        return ya, yb, merge_fn(ya, yb, ga, gb)

    ya, yb, merged = _row_call(merge_fwd, [oa, ob, col(z_gbr, 0, 1024), col(z_gbr, 1, 1024)], [W["w_o_gdn"], W["w_o_mla"]],
                               [(1024, BF16), (1024, BF16), (1024, BF16)], T=T, tm=tm, name="merge_fwd")
    x1 = _mm(merged, W["w_out"], "nn", F32, add=x2d, name="w_out_fwd")

    h2 = _norm_fwd(x1, g_ffn, T=T, tm=tm, name="norm_ffn_fwd")
    up = _mm(h2, W["w_up"], "nn", F32, name="w_up_fwd")
    FCB = 256
    nfb = D_FF // FCB
    f = _conv_fwd(_ffn_fn, [(up, 0), (up, nfb)], [(conv_ffn_w, 0), (conv_ffn_w, nfb)], D_FF, BF16, T=T, S=S, tm=tm,
                  cb=FCB, ncb=nfb, name="ffn_act_fwd")
    x2 = _mm(f, W["w_down"], "nn", F32, add=x1, name="w_down_fwd", tk=1408)

    def final_fn(xt, tt, g):
        def lossf(xv, gv):
            e = _rms(xv, gv) - tt
            return 0.5 * jnp.sum(jnp.mean(e * e, axis=-1))

        l, vjp = jax.vjp(lossf, xt, g)
        dx, dg = vjp(jnp.ones((), F32))
        return dx, jnp.full((1, LANES), l, F32), dg

    dx2, loss_v, dg_fin = _row_call(final_fn, [x2, tgt2d], [g_fin], [(1024, F32)], [((1, LANES), F32), ((1, 1024), F32)],
                                    T=T, tm=tm, name="loss_head")

    dW = {}
    df = _mm(dx2, W["w_down"], "nt", F32, name="w_down_dx")
    dW["w_down"] = _mm(f, dx2, "tn", F32, name="w_down_dw")
    dug, duu, dcw_g, dcw_u = _conv_bwd(_ffn_fn, [(up, 0), (up, nfb)], [(conv_ffn_w, 0), (conv_ffn_w, nfb)], df, BF16,
                                       T=T, S=S, tm=tm, cb=FCB, ncb=nfb, name="ffn_act_bwd")
    d_conv_ffn = jnp.concatenate([dcw_g, dcw_u], axis=1)
    wup_g, wup_u = W["w_up"][:, :D_FF], W["w_up"][:, D_FF:]
    dh2 = _mm(dug, wup_g, "nt", F32, name="w_up_dx_g")
    dh2 = _mm(duu, wup_u, "nt", F32, add=dh2, name="w_up_dx_u")
    dW["w_up"] = jnp.concatenate([_mm(h2, dug, "tn", F32, name="w_up_dw_g"), _mm(h2, duu, "tn", F32, name="w_up_dw_u")], axis=1)
    dx1, dg_ffn = _norm_bwd(x1, g_ffn, dh2, dx2, T=T, tm=tm, name="norm_ffn_bwd")

    dmerged = _mm(dx1, W["w_out"], "nt", F32, name="w_out_dx")
    dW["w_out"] = _mm(merged, dx1, "tn", F32, name="w_out_dw")

    def merge_bwd(dm, yat, ybt, ga, gb):
        _, vjp = jax.vjp(merge_fn, yat.astype(F32), ybt.astype(F32), ga, gb)
        return vjp(dm)

    dya, dyb, dgbr_a, dgbr_b = _row_call(merge_bwd, [dmerged, ya, yb, col(z_gbr, 0, 1024), col(z_gbr, 1, 1024)], [],
                                         [(1024, BF16)] * 4, T=T, tm=tm, name="merge_bwd")
    doa = _mm(dya, W["w_o_gdn"], "nt", F32, name="w_o_gdn_dx")
    dob = _mm(dyb, W["w_o_mla"], "nt", BF16, name="w_o_mla_dx")
    dW["w_o_gdn"] = _mm(oa, dya, "tn", F32, name="w_o_gdn_dw")
    dW["w_o_mla"] = _mm(ob, dyb, "tn", F32, name="w_o_mla_dw")

    dqn, dqp = _flash_bwd_dq(qn, qp, kn, kpe, vb, ob, dob, lse, B=B, S=S, t=ta)
    dkn, dkp, dvb = _flash_bwd_dkv(qn, qp, kn, kpe, vb, ob, dob, lse, B=B, S=S, t=ta)

    def qrope_bwd(d, c, s):
        return d * jnp.tile(c, (1, HEADS)), d * jnp.tile(s, (1, HEADS))

    dq_lin, dq_rot = _row_call(qrope_bwd, [dqp, tab(cos), tab(sin)], [], [(1024, BF16), (1024, BF16)], T=T, tm=tm,
                               name="q_rope_bwd")
    wp_lin, wp_rot = W["uq_p"][:, :1024], W["uq_p"][:, 1024:]
    dcq = _mm(dqn, W["uq_n"], "nt", F32, name="uq_n_dx")
    dcq = _mm(dq_lin, wp_lin, "nt", F32, add=dcq, name="uq_pl_dx")
    dcq = _mm(dq_rot, wp_rot, "nt", F32, add=dcq, name="uq_pr_dx")
    dW["uq_n"] = _mm(cq, dqn, "tn", F32, name="uq_n_dw")
    dW["uq_p"] = jnp.concatenate([_mm(cq, dq_lin, "tn", F32, name="uq_pl_dw"), _mm(cq, dq_rot, "tn", F32, name="uq_pr_dw")], axis=1)
    dckv = _mm(dkn, W["ukv_k"], "nt", F32, name="ukv_k_dx")
    dckv = _mm(dvb, W["ukv_v"], "nt", F32, add=dckv, name="ukv_v_dx")
    dW["ukv_k"] = _mm(ckv, dkn, "tn", F32, name="ukv_k_dw")
    dW["ukv_v"] = _mm(ckv, dvb, "tn", F32, name="ukv_v_dw")

    def mla_prep_bwd(zq, zkv0, zkv1, zpl, zpr, c, s, dcqt, dckvt, dkpt, gq, gkv):
        zkv = jnp.concatenate([zkv0, zkv1], axis=1)
        _, vjp = jax.vjp(lambda a, b, p, r, g1, g2: mla_prep_fn(a, b, p, r, c, s, g1, g2), zq, zkv, zpl, zpr, gq, gkv)
        dk = dkpt[0]
        for h in range(1, HEADS):
            dk = dk + dkpt[h]
        dzq, dzkv, dzpl, dzpr, dgq, dgkv = vjp((dcqt, dckvt, dk))
        return jnp.concatenate([dzq, dzkv, dzpl, dzpr], axis=1), dgq, dgkv

    dz_small, dg_q, dg_kv = _row_call(
        mla_prep_bwd, small_cols + [tab(cos), tab(sin), dcq, dckv, (dkp, (HEADS, tm, HD), lambda i: (0, i, 0))],
        [g_q, g_kv], [(896, BF16)], [((1, Q_RANK), F32), ((1, KV_RANK), F32)], T=T, tm=tm, name="mla_prep_bwd")

    def gdn_out_bwd(o, ga, dot_, g):
        _, vjp = jax.vjp(gdn_out_fn, o, ga, g)
        return vjp(dot_)

    do_gdn, dz_ga, dg_gdn = _row_call(gdn_out_bwd, [o_gdn, z_ga, doa], [g_gdn], [(1024, F32), (1024, BF16)],
                                      [((1, HD), F32)], T=T, tm=tm, name="gdn_out_bwd")
    dqkvn, dgc, dgrT, dbeta = _gdn_bwd(qkvn, gcum, grT, beta, states, do_gdn, B=B, S=S)
    dgc_tot = dgc + _padc(dgrT.transpose(0, 2, 1).reshape(T, HEADS), LANES)

    def gate_bwd(za, zb, dg, db, al, db_):
        _, vjp = jax.vjp(_gate_fn, za, zb, al, db_)
        return vjp((dg, db))

    dz_a, dz_b, d_alog, d_dtb = _row_call(gate_bwd, [col(z_ab, 0, LANES), col(z_ab, 1, LANES), dgc_tot, dbeta], [alog, dtb],
                                          [(LANES, BF16), (LANES, BF16)], [((1, LANES), F32), ((1, LANES), F32)],
                                          T=T, tm=tm, name="gdn_gate_bwd")
    dz_qkv, d_conv_qkv = _conv_bwd(_qkv_fn, [(z_qkv, 0)], [(conv_qkv_w, 0)], dqkvn, BF16, T=T, S=S, tm=tm, cb=QKV_CB,
                                   ncb=3072 // QKV_CB, name="gdn_qkv_bwd")

    dz_ab = jnp.concatenate([dz_a, dz_b], axis=1)
    dz_gbr = jnp.concatenate([dgbr_a, dgbr_b], axis=1)
    dh1 = None
    for key, dz in (("in_qkv", dz_qkv), ("in_ga", dz_ga), ("in_ab", dz_ab), ("in_small", dz_small), ("in_gbr", dz_gbr)):
        dh1 = _mm(dz, W[key], "nt", F32, add=dh1, name=key + "_dx", tk=896 if key == "in_small" else 1024)
        dW[key] = _mm(h1, dz, "tn", F32, name=key + "_dw", tn=896 if key == "in_small" else 1024)
    dx, dg_mix = _norm_bwd(x2d, g_mix, dh1, dx1, T=T, tm=tm, name="norm_mix_bwd")

    dsmall = {"norm_mix_g": dg_mix, "gdn_a_log": d_alog[:, :HEADS], "gdn_dt_bias": d_dtb[:, :HEADS], "gdn_norm_g": dg_gdn,
              "mla_q_norm_g": dg_q, "mla_kv_norm_g": dg_kv, "norm_ffn_g": dg_ffn, "norm_final_g": dg_fin}
    return loss_v[0, 0], dx.reshape(B, S, D_MODEL), dW, d_conv_qkv, d_conv_ffn, dsmall


def kernel(x, norm_mix_g, w_in, conv_qkv_w, gdn_a_log, gdn_dt_bias, gdn_norm_g, mla_q_norm_g, w_uq, mla_kv_norm_g, w_ukv, w_o_gdn, w_o_mla, w_out, norm_ffn_g, w_up, conv_ffn_w, w_down, norm_final_g, loss_target, m_norm_mix_g, m_w_in, m_conv_qkv_w, m_gdn_a_log, m_gdn_dt_bias, m_gdn_norm_g, m_mla_q_norm_g, m_w_uq, m_mla_kv_norm_g, m_w_ukv, m_w_o_gdn, m_w_o_mla, m_w_out, m_norm_ffn_g, m_w_up, m_conv_ffn_w, m_w_down, m_norm_final_g, v_norm_mix_g, v_w_in, v_conv_qkv_w, v_gdn_a_log, v_gdn_dt_bias, v_gdn_norm_g, v_mla_q_norm_g, v_w_uq, v_mla_kv_norm_g, v_w_ukv, v_w_o_gdn, v_w_o_mla, v_w_out, v_norm_ffn_g, v_w_up, v_conv_ffn_w, v_w_down, v_norm_final_g):
    w = dict(norm_mix_g=norm_mix_g, w_in=w_in, conv_qkv_w=conv_qkv_w, gdn_a_log=gdn_a_log, gdn_dt_bias=gdn_dt_bias,
             gdn_norm_g=gdn_norm_g, mla_q_norm_g=mla_q_norm_g, w_uq=w_uq, mla_kv_norm_g=mla_kv_norm_g, w_ukv=w_ukv,
             w_o_gdn=w_o_gdn, w_o_mla=w_o_mla, w_out=w_out, norm_ffn_g=norm_ffn_g, w_up=w_up, conv_ffn_w=conv_ffn_w,
             w_down=w_down, norm_final_g=norm_final_g)
    m = dict(norm_mix_g=m_norm_mix_g, w_in=m_w_in, conv_qkv_w=m_conv_qkv_w, gdn_a_log=m_gdn_a_log, gdn_dt_bias=m_gdn_dt_bias,
             gdn_norm_g=m_gdn_norm_g, mla_q_norm_g=m_mla_q_norm_g, w_uq=m_w_uq, mla_kv_norm_g=m_mla_kv_norm_g, w_ukv=m_w_ukv,
             w_o_gdn=m_w_o_gdn, w_o_mla=m_w_o_mla, w_out=m_w_out, norm_ffn_g=m_norm_ffn_g, w_up=m_w_up,
             conv_ffn_w=m_conv_ffn_w, w_down=m_w_down, norm_final_g=m_norm_final_g)
    v = dict(norm_mix_g=v_norm_mix_g, w_in=v_w_in, conv_qkv_w=v_conv_qkv_w, gdn_a_log=v_gdn_a_log, gdn_dt_bias=v_gdn_dt_bias,
             gdn_norm_g=v_gdn_norm_g, mla_q_norm_g=v_mla_q_norm_g, w_uq=v_w_uq, mla_kv_norm_g=v_mla_kv_norm_g, w_ukv=v_w_ukv,
             w_o_gdn=v_w_o_gdn, w_o_mla=v_w_o_mla, w_out=v_w_out, norm_ffn_g=v_norm_ffn_g, w_up=v_w_up,
             conv_ffn_w=v_conv_ffn_w, w_down=v_w_down, norm_final_g=v_norm_final_g)
    big_names = [n for n, _, _, _ in BIG]
    small_names = [n for n, _ in SMALL]
    small_shapes = {n: w[n].shape for n in small_names}

    w_big_f32 = _pack_flat([w[n] for n in big_names], BIG_ROWS, F32)
    gathered = _allgather(w_big_f32.astype(BF16), name="allgather_weights")
    conv_local = _pack_flat([w["conv_qkv_w"], w["conv_ffn_w"]], CONV_ROWS, F32)
    conv_all = _allgather(conv_local, name="allgather_conv").reshape(N_DEV, -1)
    full = _unpack_gathered_big(gathered)
    n_q, n_f = GDN_CONV * 384, FFN_CONV * 704
    conv_qkv_full = conv_all[:, :n_q].reshape(N_DEV, GDN_CONV, 384).transpose(1, 0, 2).reshape(GDN_CONV, 3072)
    conv_ffn_full = conv_all[:, n_q:n_q + n_f].reshape(N_DEV, FFN_CONV, 704).transpose(1, 0, 2).reshape(FFN_CONV, 2 * D_FF)
    W = _layout_weights(full)

    loss_local, dx, dW, d_conv_qkv, d_conv_ffn, dsmall = _local_step(
        x, loss_target, W, conv_qkv_full, conv_ffn_full, {n: w[n] for n in small_names})
    gfull = _unlayout_grads(dW)
    gfull["conv_qkv_w"], gfull["conv_ffn_w"] = d_conv_qkv, d_conv_ffn

    recv = _alltoall(_pack_grads_big(gfull, BF16), name="alltoall_grads")
    m_big = _pack_flat([m[n] for n in big_names], BIG_ROWS, F32)
    v_big = _pack_flat([v[n] for n in big_names], BIG_ROWS, F32)
    g_b, d_b, m_b, v_b = _reduce_adam(recv, w_big_f32, m_big, v_big, tr=PACK_TR, name="adam_big")
    small_parts = _allgather(_pack_flat([dsmall[n] for n in small_names], SMALL_ROWS, F32), name="allgather_small_grads")
    g_s, d_s, m_s, v_s = _reduce_adam(small_parts, _pack_flat([w[n] for n in small_names], SMALL_ROWS, F32),
                                      _pack_flat([m[n] for n in small_names], SMALL_ROWS, F32),
                                      _pack_flat([v[n] for n in small_names], SMALL_ROWS, F32), tr=SMALL_ROWS, name="adam_small")

    loss = lax.psum(loss_local, ("x", "y", "c"))
    groups = []
    for big_buf, small_buf in ((g_b, g_s), (d_b, d_s), (m_b, m_s), (v_b, v_s)):
        merged = {**_unpack_local_big(big_buf), **_unpack_small(small_buf, small_shapes)}
        groups.append([merged[n] for n in WEIGHT_ORDER])
    return (loss, dx, *groups[0], *groups[1], *groups[2], *groups[3])
```

```python
import functools
import math

import numpy as np
import jax
import jax.numpy as jnp
from jax import lax
from jax.experimental import pallas as pl
from jax.experimental.pallas import tpu as pltpu

F32 = jnp.float32
BF16 = jnp.bfloat16

D_MODEL = 1024
HEADS = 8
HD = 128
GDN_CONV = 4
CHUNK = 64
Q_RANK = 384
KV_RANK = 256
ROPE = 64
ROPE_THETA = 10000.0
D_FF = 2816
FFN_CONV = 3
EPS = 1e-6
SM_SCALE = (HD + ROPE) ** -0.5
N_DEV = 8

ADAM_LR, ADAM_B1, ADAM_B2, ADAM_EPS, ADAM_WD, ADAM_STEP = 0.001, 0.9, 0.999, 1e-08, 0.01, 10

LANES = 128
SUBLANES = 8
HALO = SUBLANES
VMEM_LIMIT = 56 * 1024 * 1024
HI = lax.Precision.HIGHEST
TRI_PRECISION = None

NN = (((1,), (0,)), ((), ()))
NT = (((1,), (1,)), ((), ()))
TN = (((0,), (0,)), ((), ()))


def _dot(a, b, dims=NN, precision=None):
    return lax.dot_general(a, b, dims, precision=precision, preferred_element_type=F32)


def _pick(dim, target, align):
    best = None
    for t in range(align, min(dim, target) + 1, align):
        if dim % t == 0:
            best = t
    return dim if best is None else best


def _call(body, ins, outs, grid, *, name, scratch=(), semantics=None):
    n_in, n_out = len(ins), len(outs)

    def kern(*refs):
        body(refs[:n_in], refs[n_in:n_in + n_out], refs[n_in + n_out:])

    res = pl.pallas_call(
        kern,
        grid=grid,
        in_specs=[pl.BlockSpec(bs, im) for _, bs, im in ins],
        out_specs=[pl.BlockSpec(bs, im) for _, _, bs, im in outs],
        out_shape=[jax.ShapeDtypeStruct(s, d) for s, d, _, _ in outs],
        scratch_shapes=list(scratch),
        name=name,
        compiler_params=pltpu.CompilerParams(
            dimension_semantics=semantics or ("arbitrary",) * len(grid), vmem_limit_bytes=VMEM_LIMIT),
    )(*[a for a, _, _ in ins])
    return res


def _mm(a, b, mode, out_dtype, *, name, add=None, tm=512, tn=1024, tk=1024):
    if mode == "nn":
        (M, K), (K2, N) = a.shape, b.shape
    elif mode == "nt":
        (M, K), (N, K2) = a.shape, b.shape
    else:
        (K, M), (K2, N) = a.shape, b.shape
    assert K == K2, (a.shape, b.shape, mode)
    tm = _pick(M, tm, LANES if mode == "tn" else 16)
    tn = _pick(N, tn, LANES)
    tk = _pick(K, tk, 16 if mode == "tn" else LANES)
    nk = K // tk
    dims = {"nn": NN, "nt": NT, "tn": TN}[mode]
    if mode == "nn":
        a_spec, b_spec = ((tm, tk), lambda i, j, k: (i, k)), ((tk, tn), lambda i, j, k: (k, j))
    elif mode == "nt":
        a_spec, b_spec = ((tm, tk), lambda i, j, k: (i, k)), ((tn, tk), lambda i, j, k: (j, k))
    else:
        a_spec, b_spec = ((tk, tm), lambda i, j, k: (k, i)), ((tk, tn), lambda i, j, k: (k, j))
    ins = [(a,) + a_spec, (b,) + b_spec]
    if add is not None:
        ins.append((add, (tm, tn), lambda i, j, k: (i, j)))
    outs = [((M, N), out_dtype, (tm, tn), lambda i, j, k: (i, j))]

    def body(in_refs, out_refs, scr):
        prod = _dot(in_refs[0][...].astype(BF16), in_refs[1][...].astype(BF16), dims)

        def finish(r):
            if add is not None:
                r = r + in_refs[2][...].astype(F32)
            out_refs[0][...] = r.astype(out_dtype)

        if nk == 1:
            finish(prod)
            return
        k = pl.program_id(2)
        acc = scr[0]

        @pl.when(k == 0)
        def _():
            acc[...] = prod

        @pl.when(k > 0)
        def _():
            acc[...] += prod

        @pl.when(k == nk - 1)
        def _():
            finish(acc[...])

    return _call(body, ins, outs, (M // tm, N // tn, nk), name=name,
                 scratch=[pltpu.VMEM((tm, tn), F32)] if nk > 1 else [],
                 semantics=("parallel", "parallel", "arbitrary"))[0]


def _row_call(fn, rows, consts, out_rows, out_accs=(), *, T, tm, name):
    nt = T // tm
    ins = []
    for r in rows:
        ins.append(r if isinstance(r, tuple) else (r, (tm, r.shape[1]), lambda i: (i, 0)))
    for c in consts:
        ins.append((c, c.shape, lambda i, nd=c.ndim: (0,) * nd))
    outs = []
    for o in out_rows:
        outs.append(((T, o[0]), o[1], (tm, o[0]), lambda i: (i, 0)) if len(o) == 2 else o)
    for shp, dt in out_accs:
        outs.append((shp, dt, shp, lambda i, nd=len(shp): (0,) * nd))
    n_r = len(out_rows)

    def body(in_refs, out_refs, _):
        i = pl.program_id(0)
        vals = fn(*[r[...] for r in in_refs])
        for o_ref, v in zip(out_refs[:n_r], vals[:n_r]):
            o_ref[...] = v.astype(o_ref.dtype)
        for o_ref, v in zip(out_refs[n_r:], vals[n_r:]):
            @pl.when(i == 0)
            def _(o_ref=o_ref):
                o_ref[...] = jnp.zeros_like(o_ref)

            o_ref[...] += v.astype(o_ref.dtype)

    return _call(body, ins, outs, (nt,), name=name)


def _rms(x, g):
    return x * lax.rsqrt(jnp.mean(x * x, axis=-1, keepdims=True) + EPS) * g


def _norm_fwd(x, g, *, T, tm, name):
    return _row_call(lambda xt, gt: (_rms(xt, gt),), [x], [g], [(x.shape[1], BF16)], T=T, tm=tm, name=name)[0]


def _norm_bwd(x, g, dh, dres, *, T, tm, name):
    def fn(xt, dht, drt, gt):
        _, vjp = jax.vjp(_rms, xt, gt)
        dx, dg = vjp(dht)
        return drt + dx, dg

    return _row_call(fn, [x, dh, dres], [g], [(x.shape[1], F32)], [(g.shape, F32)], T=T, tm=tm, name=name)


def _dwconv(tail, x, w):
    K, tm = w.shape[0], x.shape[0]
    xx = jnp.concatenate([tail, x], axis=0)
    acc = None
    for k in range(K):
        s = HALO - (K - 1) + k
        term = w[k:k + 1, :] * xx[s:s + tm, :]
        acc = term if acc is None else acc + term
    return acc


def _conv_fwd(fn, xs, ws, out_c, out_dtype, *, T, S, tm, cb, ncb, name):
    nt, tps, hb = T // tm, S // tm, tm // HALO
    ins = []
    for arr, off in xs:
        ins.append((arr, (tm, cb), lambda j, i, off=off: (i, off + j)))
        ins.append((arr, (HALO, cb), lambda j, i, off=off: (jnp.maximum(i * hb - 1, 0), off + j)))
    for arr, off in ws:
        ins.append((arr, (arr.shape[0], cb), lambda j, i, off=off: (0, off + j)))
    outs = [((T, out_c), out_dtype, (tm, cb), lambda j, i: (i, j))]
    nx = len(xs)

    def body(in_refs, out_refs, _):
        j, i = pl.program_id(0), pl.program_id(1)
        first = (i % tps) == 0
        xts = [in_refs[2 * m][...].astype(F32) for m in range(nx)]
        tails = [jnp.where(first, 0.0, in_refs[2 * m + 1][...].astype(F32)) for m in range(nx)]
        wts = [r[...] for r in in_refs[2 * nx:]]
        out_refs[0][...] = fn(j, tails, xts, wts).astype(out_dtype)

    return _call(body, ins, outs, (ncb, nt), name=name)[0]


def _conv_bwd(fn, xs, ws, dout, dx_dtype, *, T, S, tm, cb, ncb, name):
    nt, tps, hb = T // tm, S // tm, tm // HALO
    ins = []
    for arr, off in xs:
        ins.append((arr, (tm, cb), lambda j, i, off=off: (nt - 1 - i, off + j)))
        ins.append((arr, (HALO, cb), lambda j, i, off=off: (jnp.maximum((nt - 1 - i) * hb - 1, 0), off + j)))
    for arr, off in ws:
        ins.append((arr, (arr.shape[0], cb), lambda j, i, off=off: (0, off + j)))
    ins.append((dout, (tm, cb), lambda j, i: (nt - 1 - i, j)))
    nx, nw = len(xs), len(ws)
    outs = [((T, ncb * cb), dx_dtype, (tm, cb), lambda j, i: (nt - 1 - i, j)) for _ in xs]
    outs += [((arr.shape[0], ncb * cb), F32, (arr.shape[0], cb), lambda j, i: (0, j)) for arr, _ in ws]
    scratch = [pltpu.VMEM((HALO, cb), F32) for _ in xs]

    def body(in_refs, out_refs, carry):
        j, i = pl.program_id(0), pl.program_id(1)
        r = nt - 1 - i
        first = (r % tps) == 0
        xts = [in_refs[2 * m][...].astype(F32) for m in range(nx)]
        tails = [jnp.where(first, 0.0, in_refs[2 * m + 1][...].astype(F32)) for m in range(nx)]
        wts = [ref[...] for ref in in_refs[2 * nx:2 * nx + nw]]
        _, vjp = jax.vjp(lambda tl, xt, wt: fn(j, tl, xt, wt), tails, xts, wts)
        dtails, dxts, dwts = vjp(in_refs[-1][...].astype(F32))

        @pl.when(i == 0)
        def _():
            for c in carry:
                c[...] = jnp.zeros_like(c)

        for m in range(nx):
            pad = jnp.concatenate([jnp.zeros((tm - HALO, cb), F32), carry[m][...]], axis=0)
            out_refs[m][...] = (dxts[m] + pad).astype(dx_dtype)
            carry[m][...] = jnp.where(first, 0.0, dtails[m])
        for m in range(nw):
            o_ref = out_refs[nx + m]

            @pl.when(i == 0)
            def _(o_ref=o_ref):
                o_ref[...] = jnp.zeros_like(o_ref)

            o_ref[...] += dwts[m]

    return _call(body, ins, outs, (ncb, nt), name=name, scratch=scratch)


QKV_CB = 512


def _qkv_fn(j, tails, xts, wts):
    y = jax.nn.silu(_dwconv(tails[0], xts[0], wts[0]))
    scale = jnp.where(j < 2, HD ** -0.5, 1.0)
    parts = []
    for h in range(QKV_CB // HD):
        yh = y[:, h * HD:(h + 1) * HD]
        nh = yh * lax.rsqrt(jnp.sum(yh * yh, axis=-1, keepdims=True) + EPS)
        parts.append(jnp.where(j < 4, nh * scale, yh))
    return jnp.concatenate(parts, axis=1)


def _ffn_fn(j, tails, xts, wts):
    return jax.nn.silu(_dwconv(tails[0], xts[0], wts[0])) * _dwconv(tails[1], xts[1], wts[1])


BNN = (((2,), (1,)), ((0,), (0,)))
BNT = (((2,), (2,)), ((0,), (0,)))
BTN = (((1,), (1,)), ((0,), (0,)))


def _tri_inv(L):
    C = L.shape[-1]
    ii = lax.broadcasted_iota(jnp.int32, (C, C), 0)
    jj = lax.broadcasted_iota(jnp.int32, (C, C), 1)
    eye = (ii == jj).astype(F32)
    X = eye - jnp.where((ii >> 1) == (jj >> 1), L, 0.0)
    s = 1
    while (2 << s) <= C:
        E = jnp.where(((ii >> (s + 1)) == (jj >> (s + 1))) & ((ii >> s) != (jj >> s)), L, 0.0)
        X = X - _dot(_dot(X, E, BNN, precision=TRI_PRECISION), X, BNN, precision=TRI_PRECISION)
        s += 1
    return X


def _gdn_chunk(q, k, v, gc, gr, beta, S):
    C = q.shape[1]
    ii = lax.broadcasted_iota(jnp.int32, (C, C), 0)
    jj = lax.broadcasted_iota(jnp.int32, (C, C), 1)
    lower = ii >= jj
    decay = jnp.where(lower, jnp.exp(jnp.where(lower, gc - gr, 0.0)), 0.0)
    kb, vb = k * beta, v * beta
    L = jnp.where(ii > jj, _dot(kb, k, BNT) * decay, 0.0)
    Tinv = _tri_inv(L)
    eg = jnp.exp(gc)
    u = _dot(Tinv, vb, BNN, precision=TRI_PRECISION)
    w = _dot(Tinv, kb * eg, BNN, precision=TRI_PRECISION)
    a = _dot(q, k, BNT) * decay
    g_last = gc[:, C - 1:C, :]
    kd = k * jnp.exp(g_last - gc)
    v_new = u - _dot(w, S, BNN)
    o = _dot(q * eg, S, BNN) + _dot(a, v_new, BNN)
    S_new = S * jnp.exp(g_last) + _dot(kd, v_new, BTN)
    return o, S_new


def _heads(ref, width=HD):
    return jnp.stack([ref[:, h * width:(h + 1) * width] for h in range(HEADS)])


def _gdn_fwd(qkvn, gcum, grT, beta, *, B, S):
    N, T = S // CHUNK, B * S
    row = lambda c: (lambda b, n: (b * N + n, c))
    ins = [(qkvn, (CHUNK, 1024), row(0)), (qkvn, (CHUNK, 1024), row(1)), (qkvn, (CHUNK, 1024), row(2)),
           (gcum, (CHUNK, LANES), row(0)), (grT, (1, HEADS, 1, CHUNK), lambda b, n: (b * N + n, 0, 0, 0)),
           (beta, (CHUNK, LANES), row(0))]
    outs = [((T, 1024), F32, (CHUNK, 1024), row(0)),
            ((B * N, HEADS, HD, HD), F32, (1, HEADS, HD, HD), lambda b, n: (b * N + n, 0, 0, 0))]

    def body(in_refs, out_refs, scr):
        q_ref, k_ref, v_ref, gc_ref, gr_ref, b_ref = in_refs
        o_ref, st_ref = out_refs
        S_ref = scr[0]

        @pl.when(pl.program_id(1) == 0)
        def _():
            S_ref[...] = jnp.zeros_like(S_ref)

        S0 = S_ref[...]
        st_ref[0] = S0
        o, Sn = _gdn_chunk(_heads(q_ref), _heads(k_ref), _heads(v_ref), _heads(gc_ref, 1), gr_ref[0],
                           _heads(b_ref, 1), S0)
        for h in range(HEADS):
            o_ref[:, h * HD:(h + 1) * HD] = o[h]
        S_ref[...] = Sn

    return _call(body, ins, outs, (B, N), name="gdn_core_fwd", scratch=[pltpu.VMEM((HEADS, HD, HD), F32)])


def _gdn_bwd(qkvn, gcum, grT, beta, states, do, *, B, S):
    N, T = S // CHUNK, B * S
    row = lambda c: (lambda b, n: (b * N + N - 1 - n, c))
    ins = [(qkvn, (CHUNK, 1024), row(0)), (qkvn, (CHUNK, 1024), row(1)), (qkvn, (CHUNK, 1024), row(2)),
           (gcum, (CHUNK, LANES), row(0)), (grT, (1, HEADS, 1, CHUNK), lambda b, n: (b * N + N - 1 - n, 0, 0, 0)),
           (beta, (CHUNK, LANES), row(0)),
           (states, (1, HEADS, HD, HD), lambda b, n: (b * N + N - 1 - n, 0, 0, 0)), (do, (CHUNK, 1024), row(0))]
    outs = [((T, 3072), F32, (CHUNK, 3072), row(0)), ((T, LANES), F32, (CHUNK, LANES), row(0)),
            ((B * N, HEADS, 1, CHUNK), F32, (1, HEADS, 1, CHUNK), lambda b, n: (b * N + N - 1 - n, 0, 0, 0)),
            ((T, LANES), F32, (CHUNK, LANES), row(0))]

    def body(in_refs, out_refs, scr):
        q_ref, k_ref, v_ref, gc_ref, gr_ref, b_ref, st_ref, do_ref = in_refs
        dqkv_ref, dgc_ref, dgr_ref, db_ref = out_refs
        dS_ref = scr[0]

        @pl.when(pl.program_id(1) == 0)
        def _():
            dS_ref[...] = jnp.zeros_like(dS_ref)

        args = (_heads(q_ref), _heads(k_ref), _heads(v_ref), _heads(gc_ref, 1), gr_ref[0], _heads(b_ref, 1), st_ref[0])
        _, vjp = jax.vjp(_gdn_chunk, *args)
        dq, dk, dv, dgc, dgr, db, dS = vjp((_heads(do_ref), dS_ref[...]))
        lane = lax.broadcasted_iota(jnp.int32, (CHUNK, LANES), 1)
        dgc_all = jnp.zeros((CHUNK, LANES), F32)
        db_all = jnp.zeros((CHUNK, LANES), F32)
        for h in range(HEADS):
            dqkv_ref[:, h * HD:(h + 1) * HD] = dq[h]
            dqkv_ref[:, 1024 + h * HD:1024 + (h + 1) * HD] = dk[h]
            dqkv_ref[:, 2048 + h * HD:2048 + (h + 1) * HD] = dv[h]
            dgc_all = jnp.where(lane == h, dgc[h], dgc_all)
            db_all = jnp.where(lane == h, db[h], db_all)
        dgc_ref[...] = dgc_all
        db_ref[...] = db_all
        dgr_ref[0] = dgr
        dS_ref[...] = dS

    return _call(body, ins, outs, (B, N), name="gdn_core_bwd", scratch=[pltpu.VMEM((HEADS, HD, HD), F32)])


def _gate_fn(za, zb, alog, dtb):
    tm = za.shape[0]
    g = -jnp.exp(alog) * jax.nn.softplus(za + dtb)
    ii = lax.broadcasted_iota(jnp.int32, (tm, tm), 0)
    jj = lax.broadcasted_iota(jnp.int32, (tm, tm), 1)
    tri = ((ii >= jj) & ((ii >> 6) == (jj >> 6))).astype(F32)
    return _dot(tri, g, precision=HI), jax.nn.sigmoid(zb)


def _scores(qn_ref, qp_ref, kn_ref, kp_ref, diag):
    q = jnp.concatenate([qn_ref[...], qp_ref[...]], axis=1)
    k = jnp.concatenate([kn_ref[...], kp_ref[...]], axis=1)
    s = _dot(q, k, NT) * SM_SCALE
    if diag:
        t = s.shape[0]
        ii = lax.broadcasted_iota(jnp.int32, (t, t), 0)
        jj = lax.broadcasted_iota(jnp.int32, (t, t), 1)
        s = jnp.where(ii >= jj, s, -jnp.inf)
    return s, q, k


def _flash_fwd(qn, qp, kn, kp, v, *, B, S, t):
    nb, T = S // t, B * S
    qmap = lambda b, h, qi, ki: (b * nb + qi, h)
    kmap = lambda b, h, qi, ki: (b * nb + jnp.minimum(ki, qi), h)
    kpmap = lambda b, h, qi, ki: (b * nb + jnp.minimum(ki, qi), 0)
    ins = [(qn, (t, HD), qmap), (qp, (t, HD), qmap), (kn, (t, HD), kmap), (kp, (t, HD), kpmap), (v, (t, HD), kmap)]
    outs = [((T, 1024), BF16, (t, HD), qmap),
            ((HEADS, T, 1), F32, (1, t, 1), lambda b, h, qi, ki: (h, b * nb + qi, 0))]
    scratch = [pltpu.VMEM((t, 1), F32), pltpu.VMEM((t, 1), F32), pltpu.VMEM((t, HD), F32)]

    def body(in_refs, out_refs, scr):
        qn_ref, qp_ref, kn_ref, kp_ref, v_ref = in_refs
        o_ref, lse_ref = out_refs
        m_ref, l_ref, acc_ref = scr
        qi, ki = pl.program_id(2), pl.program_id(3)

        @pl.when(ki == 0)
        def _():
            m_ref[...] = jnp.full_like(m_ref, -jnp.inf)
            l_ref[...] = jnp.zeros_like(l_ref)
            acc_ref[...] = jnp.zeros_like(acc_ref)

        def step(diag):
            s, _, _ = _scores(qn_ref, qp_ref, kn_ref, kp_ref, diag)
            m_old = m_ref[...]
            m_new = jnp.maximum(m_old, jnp.max(s, axis=-1, keepdims=True))
            p = jnp.exp(s - m_new)
            alpha = jnp.exp(m_old - m_new)
            l_ref[...] = alpha * l_ref[...] + jnp.sum(p, axis=-1, keepdims=True)
            acc_ref[...] = alpha * acc_ref[...] + _dot(p.astype(BF16), v_ref[...])
            m_ref[...] = m_new

        @pl.when(ki < qi)
        def _():
            step(False)

        @pl.when(ki == qi)
        def _():
            step(True)
            o_ref[...] = (acc_ref[...] / l_ref[...]).astype(BF16)
            lse_ref[0] = m_ref[...] + jnp.log(l_ref[...])

    return _call(body, ins, outs, (B, HEADS, nb, nb), name="mla_flash_fwd", scratch=scratch,
                 semantics=("parallel", "parallel", "parallel", "arbitrary"))


def _flash_bwd_dq(qn, qp, kn, kp, v, o, do, lse, *, B, S, t):
    nb, T = S // t, B * S
    qmap = lambda b, h, qi, ki: (b * nb + qi, h)
    kmap = lambda b, h, qi, ki: (b * nb + jnp.minimum(ki, qi), h)
    kpmap = lambda b, h, qi, ki: (b * nb + jnp.minimum(ki, qi), 0)
    ins = [(qn, (t, HD), qmap), (qp, (t, HD), qmap), (kn, (t, HD), kmap), (kp, (t, HD), kpmap), (v, (t, HD), kmap),
           (o, (t, HD), qmap), (do, (t, HD), qmap), (lse, (1, t, 1), lambda b, h, qi, ki: (h, b * nb + qi, 0))]
    outs = [((T, 1024), BF16, (t, HD), qmap), ((T, 1024), F32, (t, HD), qmap)]
    scratch = [pltpu.VMEM((t, 1), F32), pltpu.VMEM((t, 2 * HD), F32)]

    def body(in_refs, out_refs, scr):
        qn_ref, qp_ref, kn_ref, kp_ref, v_ref, o_ref, do_ref, lse_ref = in_refs
        dqn_ref, dqp_ref = out_refs
        dl_ref, acc_ref = scr
        qi, ki = pl.program_id(2), pl.program_id(3)

        @pl.when(ki == 0)
        def _():
            dl_ref[...] = jnp.sum(do_ref[...].astype(F32) * o_ref[...].astype(F32), axis=-1, keepdims=True)
            acc_ref[...] = jnp.zeros_like(acc_ref)

        def step(diag):
            s, _, k = _scores(qn_ref, qp_ref, kn_ref, kp_ref, diag)
            p = jnp.exp(s - lse_ref[0])
            dp = _dot(do_ref[...], v_ref[...], NT)
            ds = p * (dp - dl_ref[...]) * SM_SCALE
            acc_ref[...] += _dot(ds.astype(BF16), k)

        @pl.when(ki < qi)
        def _():
            step(False)

        @pl.when(ki == qi)
        def _():
            step(True)
            dqn_ref[...] = acc_ref[:, :HD].astype(BF16)
            dqp_ref[...] = acc_ref[:, HD:]

    return _call(body, ins, outs, (B, HEADS, nb, nb), name="mla_flash_bwd_dq", scratch=scratch,
                 semantics=("parallel", "parallel", "parallel", "arbitrary"))


def _flash_bwd_dkv(qn, qp, kn, kp, v, o, do, lse, *, B, S, t):
    nb, T = S // t, B * S
    qmap = lambda b, h, ki, qi: (b * nb + jnp.maximum(qi, ki), h)
    kmap = lambda b, h, ki, qi: (b * nb + ki, h)
    ins = [(qn, (t, HD), qmap), (qp, (t, HD), qmap), (kn, (t, HD), kmap),
           (kp, (t, HD), lambda b, h, ki, qi: (b * nb + ki, 0)), (v, (t, HD), kmap),
           (o, (t, HD), qmap), (do, (t, HD), qmap),
           (lse, (1, t, 1), lambda b, h, ki, qi: (h, b * nb + jnp.maximum(qi, ki), 0))]
    outs = [((T, 1024), BF16, (t, HD), kmap), ((HEADS, T, HD), F32, (1, t, HD), lambda b, h, ki, qi: (h, b * nb + ki, 0)),
            ((T, 1024), BF16, (t, HD), kmap)]
    scratch = [pltpu.VMEM((t, 2 * HD), F32), pltpu.VMEM((t, HD), F32)]

    def body(in_refs, out_refs, scr):
        qn_ref, qp_ref, kn_ref, kp_ref, v_ref, o_ref, do_ref, lse_ref = in_refs
        dkn_ref, dkp_ref, dv_ref = out_refs
        dk_acc, dv_acc = scr
        ki, qi = pl.program_id(2), pl.program_id(3)

        @pl.when(qi == 0)
        def _():
            dk_acc[...] = jnp.zeros_like(dk_acc)
            dv_acc[...] = jnp.zeros_like(dv_acc)

        def step(diag):
            s, q, _ = _scores(qn_ref, qp_ref, kn_ref, kp_ref, diag)
            do_t = do_ref[...]
            p = jnp.exp(s - lse_ref[0])
            dl = jnp.sum(do_t.astype(F32) * o_ref[...].astype(F32), axis=-1, keepdims=True)
            dp = _dot(do_t, v_ref[...], NT)
            ds = p * (dp - dl) * SM_SCALE
            dv_acc[...] += _dot(p.astype(BF16), do_t, TN)
            dk_acc[...] += _dot(ds.astype(BF16), q, TN)

        @pl.when(qi > ki)
        def _():
            step(False)

        @pl.when(qi == ki)
        def _():
            step(True)

        @pl.when(qi == nb - 1)
        def _():
            dkn_ref[...] = dk_acc[:, :HD].astype(BF16)
            dkp_ref[0] = dk_acc[:, HD:]
            dv_ref[...] = dv_acc[...].astype(BF16)

    return _call(body, ins, outs, (B, HEADS, nb, nb), name="mla_flash_bwd_dkv", scratch=scratch,
                 semantics=("parallel", "parallel", "parallel", "arbitrary"))


def _allgather(shards, *, name):
    n_arr = len(shards)

    def body(*refs):
        x_refs, out_refs = refs[:n_arr], refs[n_arr:2 * n_arr]
        send_sems, recv_sems, local_sems = refs[2 * n_arr:]
        x, y, c = lax.axis_index("x"), lax.axis_index("y"), lax.axis_index("c")
        me, sibling = (x, y, c), (x, y, 1 - c)
        chips = [(1 - x, y), (x, 1 - y), (1 - x, 1 - y)]

        def rows(a, px, py, pc):
            m_per = shards[a].shape[0]
            return out_refs[a].at[pl.ds((4 * px + 2 * py + pc) * m_per, m_per), :]

        def copy(a, k, block, to, src=None):
            return pltpu.make_async_remote_copy(
                src_ref=rows(a, *block) if src is None else src, dst_ref=rows(a, *block),
                send_sem=send_sems.at[a, k], recv_sem=recv_sems.at[a, k], device_id=to,
                device_id_type=pl.DeviceIdType.MESH)

        mine = [pltpu.make_async_copy(x_refs[a], rows(a, *me), local_sems.at[a]) for a in range(n_arr)]
        for cp in mine:
            cp.start()
        first = []
        for a in range(n_arr):
            first.append(copy(a, 0, me, sibling, src=x_refs[a]))
            first += [copy(a, 1 + j, me, (*chip, c), src=x_refs[a]) for j, chip in enumerate(chips)]
        for cp in first:
            cp.start()
        passed = []
        for j, chip in enumerate(chips):
            for a in range(n_arr):
                copy(a, 1 + j, (*chip, c), me).wait_recv()
                cp = copy(a, 4 + j, (*chip, c), sibling)
                cp.start()
                passed.append(cp)
        for a in range(n_arr):
            copy(a, 0, sibling, me).wait_recv()
        for j, chip in enumerate(chips):
            for a in range(n_arr):
                copy(a, 4 + j, (*chip, 1 - c), me).wait_recv()
        for cp in first + passed:
            cp.wait_send()
        for cp in mine:
            cp.wait()

    return pl.pallas_call(
        body,
        out_shape=[jax.ShapeDtypeStruct((N_DEV * s.shape[0], s.shape[1]), s.dtype) for s in shards],
        in_specs=[pl.BlockSpec(memory_space=pl.ANY)] * n_arr,
        out_specs=[pl.BlockSpec(memory_space=pl.ANY)] * n_arr,
        scratch_shapes=[pltpu.SemaphoreType.DMA((n_arr, 7)), pltpu.SemaphoreType.DMA((n_arr, 7)),
                        pltpu.SemaphoreType.DMA((n_arr,))],
        name=name,
    )(*shards)


def _alltoall(sends, *, name):
    n_arr = len(sends)

    def body(*refs):
        s_refs, r_refs = refs[:n_arr], refs[n_arr:2 * n_arr]
        send_sems, recv_sems, local_sems = refs[2 * n_arr:]
        x, y, c = lax.axis_index("x"), lax.axis_index("y"), lax.axis_index("c")
        me = 4 * x + 2 * y + c

        def rows(ref, a, idx):
            m_per = sends[a].shape[0] // N_DEV
            return ref.at[pl.ds(idx * m_per, m_per), :]

        local = [pltpu.make_async_copy(rows(s_refs[a], a, me), rows(r_refs[a], a, me), local_sems.at[a])
                 for a in range(n_arr)]
        for cp in local:
            cp.start()
        copies = []
        for k in range(1, N_DEV):
            px = 1 - x if k & 4 else x
            py = 1 - y if k & 2 else y
            pc = 1 - c if k & 1 else c
            for a in range(n_arr):
                cp = pltpu.make_async_remote_copy(
                    src_ref=rows(s_refs[a], a, 4 * px + 2 * py + pc), dst_ref=rows(r_refs[a], a, me),
                    send_sem=send_sems.at[a, k - 1], recv_sem=recv_sems.at[a, k - 1],
                    device_id=(px, py, pc), device_id_type=pl.DeviceIdType.MESH)
                cp.start()
                copies.append(cp)
        for cp in copies:
            cp.wait()
        for cp in local:
            cp.wait()

    return pl.pallas_call(
        body,
        out_shape=[jax.ShapeDtypeStruct(s.shape, s.dtype) for s in sends],
        in_specs=[pl.BlockSpec(memory_space=pl.ANY)] * n_arr,
        out_specs=[pl.BlockSpec(memory_space=pl.ANY)] * n_arr,
        scratch_shapes=[pltpu.SemaphoreType.DMA((n_arr, 7)), pltpu.SemaphoreType.DMA((n_arr, 7)),
                        pltpu.SemaphoreType.DMA((n_arr,))],
        name=name,
    )(*sends)


def _reduce_adam(parts, w, m, v, *, tr, name):
    R, C = w.shape
    nR = R // tr
    ins = [(parts, (tr, C), lambda i, s=s: (s * nR + i, 0)) for s in range(N_DEV)]
    ins += [(a, (tr, C), lambda i: (i, 0)) for a in (w, m, v)]
    outs = [((R, C), F32, (tr, C), lambda i: (i, 0)) for _ in range(4)]
    c1 = 1.0 - ADAM_B1 ** ADAM_STEP
    c2 = 1.0 - ADAM_B2 ** ADAM_STEP

    def body(in_refs, out_refs, _):
        g = in_refs[0][...].astype(F32)
        for s in range(1, N_DEV):
            g = g + in_refs[s][...].astype(F32)
        wv, mv, vv = in_refs[8][...], in_refs[9][...], in_refs[10][...]
        mn = ADAM_B1 * mv + (1.0 - ADAM_B1) * g
        vn = ADAM_B2 * vv + (1.0 - ADAM_B2) * (g * g)
        delta = -ADAM_LR * ((mn / c1) / (jnp.sqrt(vn / c2) + ADAM_EPS) + ADAM_WD * wv)
        out_refs[0][...] = g
        out_refs[1][...] = delta
        out_refs[2][...] = mn
        out_refs[3][...] = vn

    return _call(body, ins, outs, (nR,), name=name, semantics=("parallel",))


IN_C, UP_C, UQ_C, QKV_C = 858, 704, 192, 384
A_W, Q_W, V_W = 896, 256, 768
SLAB_TR = {"A": 256, "Q": 128, "C": 368, "V": 16}
SMALL = [("norm_mix_g", 1024), ("gdn_a_log", 8), ("gdn_dt_bias", 8), ("gdn_norm_g", 128), ("mla_q_norm_g", 384),
         ("mla_kv_norm_g", 256), ("norm_ffn_g", 1024), ("norm_final_g", 1024)]
SMALL_ROWS = 32
WEIGHT_ORDER = ["norm_mix_g", "w_in", "conv_qkv_w", "gdn_a_log", "gdn_dt_bias", "gdn_norm_g", "mla_q_norm_g", "w_uq",
                "mla_kv_norm_g", "w_ukv", "w_o_gdn", "w_o_mla", "w_out", "norm_ffn_g", "w_up", "conv_ffn_w", "w_down",
                "norm_final_g"]


def _padc(w, n):
    return jnp.pad(w, ((0, 0), (0, n - w.shape[1])))


def _padrc(w, r, n):
    return jnp.pad(w, ((0, r - w.shape[0]), (0, n - w.shape[1])))


def _slabs(p, dtype):
    A = jnp.concatenate([_padc(p["w_in"], A_W), _padc(p["w_up"], A_W)], axis=0).astype(dtype)
    Q = jnp.concatenate([_padc(p["w_uq"], Q_W), p["w_ukv"]], axis=0).astype(dtype)
    C = jnp.concatenate([p["w_o_gdn"], p["w_o_mla"], p["w_out"], p["w_down"]], axis=0).astype(dtype)
    V = jnp.concatenate([_padrc(p["conv_qkv_w"], 8, V_W), _padrc(p["conv_ffn_w"], 8, V_W)], axis=0).astype(F32)
    return {"A": A, "Q": Q, "C": C, "V": V}


def _unslab(sl):
    A, Q, C, V = sl["A"], sl["Q"], sl["C"], sl["V"]
    out = {"w_in": A[:1024, :IN_C], "w_up": A[1024:, :UP_C], "w_uq": Q[:384, :UQ_C], "w_ukv": Q[384:],
           "w_o_gdn": C[0:128], "w_o_mla": C[128:256], "w_out": C[256:384], "w_down": C[384:],
           "conv_qkv_w": V[0:GDN_CONV, :QKV_C], "conv_ffn_w": V[8:8 + FFN_CONV, :UP_C]}
    return {k: a[None] for k, a in out.items()}


def _take_cols(pieces, lo, hi):
    out, off = [], 0
    for arr, a, b in pieces:
        s, e = max(lo, off), min(hi, off + b - a)
        if s < e:
            out.append(arr[:, a + s - off:a + e - off])
        off += b - a
    return out[0] if len(out) == 1 else jnp.concatenate(out, axis=1)


def _pack_small(d):
    flat = jnp.concatenate([d[n].reshape(-1).astype(F32) for n, _ in SMALL])
    return jnp.pad(flat, (0, SMALL_ROWS * LANES - flat.shape[0])).reshape(SMALL_ROWS, LANES)


def _unpack_small(buf, shapes):
    flat, out, off = buf.reshape(-1), {}, 0
    for name, n in SMALL:
        out[name] = flat[off:off + n].reshape(shapes[name])
        off += n
    return out


def _rot_cols(w):
    h = ROPE // 2
    return jnp.concatenate([-w[:, h:], w[:, :h]], axis=1)


def _unrot_cols(dw):
    h = ROPE // 2
    return jnp.concatenate([dw[:, h:], -dw[:, :h]], axis=1)


IN_SPLITS = [0, 3072, 4096, 4104, 4112, 4496, 4752, 4816, 5840, 6864]


def _layout_weights(g):
    A, Q, C, V = g["A"], g["Q"], g["C"], g["V"]
    in_pieces = [(A[j, :1024], 0, IN_C) for j in range(N_DEV)]
    o = IN_SPLITS
    take = lambda lo, hi: _take_cols(in_pieces, lo, hi)
    kpe = take(o[6], o[7])
    W = {
        "in_qkv": take(o[0], o[1]),
        "in_ga": take(o[1], o[2]),
        "in_ab": jnp.concatenate([_padc(take(o[2], o[3]), LANES), _padc(take(o[3], o[4]), LANES)], axis=1),
        "in_small": jnp.concatenate([take(o[4], o[6]), _padc(kpe, LANES), _padc(_rot_cols(kpe), LANES)], axis=1),
        "in_gbr": take(o[7], o[9]),
        "w_up": jnp.concatenate([A[j, 1024:, :UP_C] for j in range(N_DEV)], axis=1),
        "uq_n": jnp.concatenate([Q[j, :384, :HD] for j in range(N_DEV)], axis=1),
        "ukv_k": jnp.concatenate([Q[j, 384:, :HD] for j in range(N_DEV)], axis=1),
        "ukv_v": jnp.concatenate([Q[j, 384:, HD:] for j in range(N_DEV)], axis=1),
        "w_o_gdn": C[:, 0:128].reshape(1024, D_MODEL),
        "w_o_mla": C[:, 128:256].reshape(1024, D_MODEL),
        "w_out": C[:, 256:384].reshape(1024, D_MODEL),
        "w_down": C[:, 384:].reshape(D_FF, D_MODEL),
    }
    pe = [Q[j, :384, HD:HD + ROPE] for j in range(N_DEV)]
    W["uq_p"] = jnp.concatenate([_padc(p, HD) for p in pe] + [_padc(_rot_cols(p), HD) for p in pe], axis=1)
    conv_qkv = jnp.concatenate([V[j, 0:GDN_CONV, :QKV_C] for j in range(N_DEV)], axis=1)
    conv_ffn = jnp.concatenate([V[j, 8:8 + FFN_CONV, :UP_C] for j in range(N_DEV)], axis=1)
    return {k: v.astype(BF16) for k, v in W.items()}, conv_qkv, conv_ffn


def _full_grads(dW):
    s = dW["in_small"]
    dkpe = s[:, 640:704] + _unrot_cols(s[:, 768:832])
    in_pieces = [(dW["in_qkv"], 0, 3072), (dW["in_ga"], 0, 1024), (dW["in_ab"], 0, 8), (dW["in_ab"], 128, 136),
                 (s, 0, 640), (dkpe, 0, ROPE), (dW["in_gbr"], 0, 2048)]
    pe = []
    for j in range(N_DEV):
        lin = dW["uq_p"][:, j * HD:j * HD + ROPE]
        rot = dW["uq_p"][:, 1024 + j * HD:1024 + j * HD + ROPE]
        pe.append(lin + _unrot_cols(rot))
    return in_pieces, pe


def _send_slabs(dW, d_conv_qkv, d_conv_ffn):
    in_pieces, pe = _full_grads(dW)
    A, Q, V = [], [], []
    for j in range(N_DEV):
        gin = _padc(_take_cols(in_pieces, j * IN_C, (j + 1) * IN_C), A_W)
        gup = _padc(dW["w_up"][:, j * UP_C:(j + 1) * UP_C], A_W)
        A.append(jnp.concatenate([gin, gup], axis=0))
        guq = _padc(jnp.concatenate([dW["uq_n"][:, j * HD:(j + 1) * HD], pe[j]], axis=1), Q_W)
        gukv = jnp.concatenate([dW["ukv_k"][:, j * HD:(j + 1) * HD], dW["ukv_v"][:, j * HD:(j + 1) * HD]], axis=1)
        Q.append(jnp.concatenate([guq, gukv], axis=0))
        V.append(jnp.concatenate([_padrc(d_conv_qkv[:, j * QKV_C:(j + 1) * QKV_C], 8, V_W),
                                  _padrc(d_conv_ffn[:, j * UP_C:(j + 1) * UP_C], 8, V_W)], axis=0))
    C = jnp.concatenate([dW["w_o_gdn"].reshape(N_DEV, 128, D_MODEL), dW["w_o_mla"].reshape(N_DEV, 128, D_MODEL),
                         dW["w_out"].reshape(N_DEV, 128, D_MODEL), dW["w_down"].reshape(N_DEV, 352, D_MODEL)], axis=1)
    return {"A": jnp.concatenate(A, axis=0).astype(BF16), "Q": jnp.concatenate(Q, axis=0).astype(BF16),
            "C": C.reshape(N_DEV * 736, D_MODEL).astype(BF16), "V": jnp.concatenate(V, axis=0)}


def _rope_tables(S):
    half = ROPE // 2
    inv = ROPE_THETA ** (-jnp.arange(half, dtype=F32) / half)
    ang = jnp.arange(S, dtype=F32)[:, None] * inv[None, :]
    cos = jnp.concatenate([jnp.cos(ang), jnp.cos(ang)], axis=1)
    sin = jnp.concatenate([jnp.sin(ang), jnp.sin(ang)], axis=1)
    return _padc(cos, HD), _padc(sin, HD)


def _local_step(x, tgt, W, conv_qkv_w, conv_ffn_w, small, tm=None, ta=None):
    B, S, _ = x.shape
    T = B * S
    tm = tm or _pick(S, 256, CHUNK)
    ta = ta or _pick(S, 512, LANES)
    x2d, tgt2d = x.reshape(T, D_MODEL), tgt.reshape(T, D_MODEL)
    row = lambda v: v.reshape(1, -1).astype(F32)
    pad_row = lambda v: _padc(row(v), LANES)
    g_mix, g_ffn, g_fin = row(small["norm_mix_g"]), row(small["norm_ffn_g"]), row(small["norm_final_g"])
    g_gdn, g_q, g_kv = row(small["gdn_norm_g"]), row(small["mla_q_norm_g"]), row(small["mla_kv_norm_g"])
    alog, dtb = pad_row(small["gdn_a_log"]), pad_row(small["gdn_dt_bias"])
    cos, sin = _rope_tables(S)
    tps = S // tm
    tab = lambda a: (a, (tm, HD), lambda i: (i % tps, 0))
    col = lambda a, c, w: (a, (tm, w), lambda i, c=c: (i, c))

    h1 = _norm_fwd(x2d, g_mix, T=T, tm=tm, name="norm_mix_fwd")
    z_qkv = _mm(h1, W["in_qkv"], "nn", F32, name="in_qkv_fwd")
    z_ga = _mm(h1, W["in_ga"], "nn", F32, name="in_ga_fwd")
    z_ab = _mm(h1, W["in_ab"], "nn", F32, name="in_ab_fwd")
    z_small = _mm(h1, W["in_small"], "nn", F32, name="in_small_fwd", tn=896)
    z_gbr = _mm(h1, W["in_gbr"], "nn", F32, name="in_gbr_fwd")

    qkvn = _conv_fwd(_qkv_fn, [(z_qkv, 0)], [(conv_qkv_w, 0)], 3072, F32, T=T, S=S, tm=tm, cb=QKV_CB,
                     ncb=3072 // QKV_CB, name="gdn_qkv_fwd")
    gcum, beta = _row_call(lambda za, zb, al, db: _gate_fn(za, zb, al, db), [col(z_ab, 0, LANES), col(z_ab, 1, LANES)],
                           [alog, dtb], [(LANES, F32), (LANES, F32)], T=T, tm=tm, name="gdn_gate_fwd")
    grT = gcum[:, :HEADS].reshape(T // CHUNK, CHUNK, HEADS).transpose(0, 2, 1)[:, :, None, :]
    o_gdn, states = _gdn_fwd(qkvn, gcum, grT, beta, B=B, S=S)

    def gdn_out_fn(o, ga, g):
        parts = []
        for h in range(HEADS):
            sl = slice(h * HD, (h + 1) * HD)
            parts.append(_rms(o[:, sl], g) * jax.nn.silu(ga[:, sl]))
        return jnp.concatenate(parts, axis=1)

    oa = _row_call(lambda o, ga, g: (gdn_out_fn(o, ga, g),), [o_gdn, z_ga], [g_gdn], [(1024, BF16)], T=T, tm=tm,
                   name="gdn_out_fwd")[0]

    def mla_prep_fn(zq, zkv, zpl, zpr, c, s, gq, gkv):
        return _rms(zq, gq), _rms(zkv, gkv), zpl * c + zpr * s

    small_cols = [(z_small, (tm, Q_RANK), lambda i: (i, 0)), (z_small, (tm, LANES), lambda i: (i, 3)),
                  (z_small, (tm, LANES), lambda i: (i, 4)), (z_small, (tm, LANES), lambda i: (i, 5)),
                  (z_small, (tm, LANES), lambda i: (i, 6))]

    def mla_prep_fwd(zq, zkv0, zkv1, zpl, zpr, c, s, gq, gkv):
        return mla_prep_fn(zq, jnp.concatenate([zkv0, zkv1], axis=1), zpl, zpr, c, s, gq, gkv)

    cq, ckv, kpe = _row_call(mla_prep_fwd, small_cols + [tab(cos), tab(sin)], [g_q, g_kv],
                             [(Q_RANK, BF16), (KV_RANK, BF16), (HD, BF16)], T=T, tm=tm, name="mla_prep_fwd")
    qn = _mm(cq, W["uq_n"], "nn", BF16, name="uq_n_fwd")
    qpl = _mm(cq, W["uq_p"], "nn", F32, name="uq_p_fwd")
    kn = _mm(ckv, W["ukv_k"], "nn", BF16, name="ukv_k_fwd")
    vb = _mm(ckv, W["ukv_v"], "nn", BF16, name="ukv_v_fwd")

    def qrope_fn(lin, rot, c, s):
        return lin * jnp.tile(c, (1, HEADS)) + rot * jnp.tile(s, (1, HEADS))

    qp = _row_call(lambda lin, rot, c, s: (qrope_fn(lin, rot, c, s),), [col(qpl, 0, 1024), col(qpl, 1, 1024), tab(cos), tab(sin)],
                   [], [(1024, BF16)], T=T, tm=tm, name="q_rope_fwd")[0]
    ob, lse = _flash_fwd(qn, qp, kn, kpe, vb, B=B, S=S, t=ta)

    def merge_fn(ya, yb, ga, gb):
        return jax.nn.sigmoid(ga) * ya + jax.nn.sigmoid(gb) * yb

    def merge_fwd(oat, obt, ga, gb, wog, wom):
        ya, yb = _dot(oat, wog), _dot(obt, wom)
        return ya, yb, merge_fn(ya, yb, ga, gb)

    ya, yb, merged = _row_call(merge_fwd, [oa, ob, col(z_gbr, 0, 1024), col(z_gbr, 1, 1024)], [W["w_o_gdn"], W["w_o_mla"]],
                               [(1024, BF16), (1024, BF16), (1024, BF16)], T=T, tm=tm, name="merge_fwd")
    x1 = _mm(merged, W["w_out"], "nn", F32, add=x2d, name="w_out_fwd")

    h2 = _norm_fwd(x1, g_ffn, T=T, tm=tm, name="norm_ffn_fwd")
    up = _mm(h2, W["w_up"], "nn", F32, name="w_up_fwd")
    FCB = 256
    nfb = D_FF // FCB
    f = _conv_fwd(_ffn_fn, [(up, 0), (up, nfb)], [(conv_ffn_w, 0), (conv_ffn_w, nfb)], D_FF, BF16, T=T, S=S, tm=tm,
                  cb=FCB, ncb=nfb, name="ffn_act_fwd")
    x2 = _mm(f, W["w_down"], "nn", F32, add=x1, name="w_down_fwd", tk=1408)

    def final_fn(xt, tt, g):
        def lossf(xv, gv):
            e = _rms(xv, gv) - tt
            return 0.5 * jnp.sum(jnp.mean(e * e, axis=-1))

        l, vjp = jax.vjp(lossf, xt, g)
        dx, dg = vjp(jnp.ones((), F32))
        return dx, jnp.full((1, LANES), l, F32), dg

    dx2, loss_v, dg_fin = _row_call(final_fn, [x2, tgt2d], [g_fin], [(1024, F32)], [((1, LANES), F32), ((1, 1024), F32)],
                                    T=T, tm=tm, name="loss_head")

    dW = {}
    df = _mm(dx2, W["w_down"], "nt", F32, name="w_down_dx")
    dW["w_down"] = _mm(f, dx2, "tn", F32, name="w_down_dw")
    dug, duu, dcw_g, dcw_u = _conv_bwd(_ffn_fn, [(up, 0), (up, nfb)], [(conv_ffn_w, 0), (conv_ffn_w, nfb)], df, BF16,
                                       T=T, S=S, tm=tm, cb=FCB, ncb=nfb, name="ffn_act_bwd")
    d_conv_ffn = jnp.concatenate([dcw_g, dcw_u], axis=1)
    wup_g, wup_u = W["w_up"][:, :D_FF], W["w_up"][:, D_FF:]
    dh2 = _mm(dug, wup_g, "nt", F32, name="w_up_dx_g")
    dh2 = _mm(duu, wup_u, "nt", F32, add=dh2, name="w_up_dx_u")
    dW["w_up"] = jnp.concatenate([_mm(h2, dug, "tn", F32, name="w_up_dw_g"), _mm(h2, duu, "tn", F32, name="w_up_dw_u")], axis=1)
    dx1, dg_ffn = _norm_bwd(x1, g_ffn, dh2, dx2, T=T, tm=tm, name="norm_ffn_bwd")

    dmerged = _mm(dx1, W["w_out"], "nt", F32, name="w_out_dx")
    dW["w_out"] = _mm(merged, dx1, "tn", F32, name="w_out_dw")

    def merge_bwd(dm, yat, ybt, ga, gb):
        _, vjp = jax.vjp(merge_fn, yat.astype(F32), ybt.astype(F32), ga, gb)
        return vjp(dm)

    dya, dyb, dgbr_a, dgbr_b = _row_call(merge_bwd, [dmerged, ya, yb, col(z_gbr, 0, 1024), col(z_gbr, 1, 1024)], [],
                                         [(1024, BF16)] * 4, T=T, tm=tm, name="merge_bwd")
    doa = _mm(dya, W["w_o_gdn"], "nt", F32, name="w_o_gdn_dx")
    dob = _mm(dyb, W["w_o_mla"], "nt", BF16, name="w_o_mla_dx")
    dW["w_o_gdn"] = _mm(oa, dya, "tn", F32, name="w_o_gdn_dw")
    dW["w_o_mla"] = _mm(ob, dyb, "tn", F32, name="w_o_mla_dw")

    dqn, dqp = _flash_bwd_dq(qn, qp, kn, kpe, vb, ob, dob, lse, B=B, S=S, t=ta)
    dkn, dkp, dvb = _flash_bwd_dkv(qn, qp, kn, kpe, vb, ob, dob, lse, B=B, S=S, t=ta)

    def qrope_bwd(d, c, s):
        return d * jnp.tile(c, (1, HEADS)), d * jnp.tile(s, (1, HEADS))

    dq_lin, dq_rot = _row_call(qrope_bwd, [dqp, tab(cos), tab(sin)], [], [(1024, BF16), (1024, BF16)], T=T, tm=tm,
                               name="q_rope_bwd")
    wp_lin, wp_rot = W["uq_p"][:, :1024], W["uq_p"][:, 1024:]
    dcq = _mm(dqn, W["uq_n"], "nt", F32, name="uq_n_dx")
    dcq = _mm(dq_lin, wp_lin, "nt", F32, add=dcq, name="uq_pl_dx")
    dcq = _mm(dq_rot, wp_rot, "nt", F32, add=dcq, name="uq_pr_dx")
    dW["uq_n"] = _mm(cq, dqn, "tn", F32, name="uq_n_dw")
    dW["uq_p"] = jnp.concatenate([_mm(cq, dq_lin, "tn", F32, name="uq_pl_dw"), _mm(cq, dq_rot, "tn", F32, name="uq_pr_dw")], axis=1)
    dckv = _mm(dkn, W["ukv_k"], "nt", F32, name="ukv_k_dx")
    dckv = _mm(dvb, W["ukv_v"], "nt", F32, add=dckv, name="ukv_v_dx")
    dW["ukv_k"] = _mm(ckv, dkn, "tn", F32, name="ukv_k_dw")
    dW["ukv_v"] = _mm(ckv, dvb, "tn", F32, name="ukv_v_dw")

    def mla_prep_bwd(zq, zkv0, zkv1, zpl, zpr, c, s, dcqt, dckvt, dkpt, gq, gkv):
        zkv = jnp.concatenate([zkv0, zkv1], axis=1)
        _, vjp = jax.vjp(lambda a, b, p, r, g1, g2: mla_prep_fn(a, b, p, r, c, s, g1, g2), zq, zkv, zpl, zpr, gq, gkv)
        dk = dkpt[0]
        for h in range(1, HEADS):
            dk = dk + dkpt[h]
        dzq, dzkv, dzpl, dzpr, dgq, dgkv = vjp((dcqt, dckvt, dk))
        return jnp.concatenate([dzq, dzkv, dzpl, dzpr], axis=1), dgq, dgkv

    dz_small, dg_q, dg_kv = _row_call(
        mla_prep_bwd, small_cols + [tab(cos), tab(sin), dcq, dckv, (dkp, (HEADS, tm, HD), lambda i: (0, i, 0))],
        [g_q, g_kv], [(896, BF16)], [((1, Q_RANK), F32), ((1, KV_RANK), F32)], T=T, tm=tm, name="mla_prep_bwd")

    def gdn_out_bwd(o, ga, dot_, g):
        _, vjp = jax.vjp(gdn_out_fn, o, ga, g)
        return vjp(dot_)

    do_gdn, dz_ga, dg_gdn = _row_call(gdn_out_bwd, [o_gdn, z_ga, doa], [g_gdn], [(1024, F32), (1024, BF16)],
                                      [((1, HD), F32)], T=T, tm=tm, name="gdn_out_bwd")
    dqkvn, dgc, dgrT, dbeta = _gdn_bwd(qkvn, gcum, grT, beta, states, do_gdn, B=B, S=S)
    dgc_tot = dgc + _padc(dgrT[:, :, 0, :].transpose(0, 2, 1).reshape(T, HEADS), LANES)

    def gate_bwd(za, zb, dg, db, al, db_):
        _, vjp = jax.vjp(_gate_fn, za, zb, al, db_)
        return vjp((dg, db))

    dz_a, dz_b, d_alog, d_dtb = _row_call(gate_bwd, [col(z_ab, 0, LANES), col(z_ab, 1, LANES), dgc_tot, dbeta], [alog, dtb],
                                          [(LANES, BF16), (LANES, BF16)], [((1, LANES), F32), ((1, LANES), F32)],
                                          T=T, tm=tm, name="gdn_gate_bwd")
    dz_qkv, d_conv_qkv = _conv_bwd(_qkv_fn, [(z_qkv, 0)], [(conv_qkv_w, 0)], dqkvn, BF16, T=T, S=S, tm=tm, cb=QKV_CB,
                                   ncb=3072 // QKV_CB, name="gdn_qkv_bwd")

    dz_ab = jnp.concatenate([dz_a, dz_b], axis=1)
    dz_gbr = jnp.concatenate([dgbr_a, dgbr_b], axis=1)
    dh1 = None
    for key, dz in (("in_qkv", dz_qkv), ("in_ga", dz_ga), ("in_ab", dz_ab), ("in_small", dz_small), ("in_gbr", dz_gbr)):
        dh1 = _mm(dz, W[key], "nt", F32, add=dh1, name=key + "_dx", tk=896 if key == "in_small" else 1024)
        dW[key] = _mm(h1, dz, "tn", F32, name=key + "_dw", tn=896 if key == "in_small" else 1024)
    dx, dg_mix = _norm_bwd(x2d, g_mix, dh1, dx1, T=T, tm=tm, name="norm_mix_bwd")

    dsmall = {"norm_mix_g": dg_mix, "gdn_a_log": d_alog[:, :HEADS], "gdn_dt_bias": d_dtb[:, :HEADS], "gdn_norm_g": dg_gdn,
              "mla_q_norm_g": dg_q, "mla_kv_norm_g": dg_kv, "norm_ffn_g": dg_ffn, "norm_final_g": dg_fin}
    return loss_v[0, 0], dx.reshape(B, S, D_MODEL), dW, d_conv_qkv, d_conv_ffn, dsmall


def kernel(x, norm_mix_g, w_in, conv_qkv_w, gdn_a_log, gdn_dt_bias, gdn_norm_g, mla_q_norm_g, w_uq, mla_kv_norm_g, w_ukv, w_o_gdn, w_o_mla, w_out, norm_ffn_g, w_up, conv_ffn_w, w_down, norm_final_g, loss_target, m_norm_mix_g, m_w_in, m_conv_qkv_w, m_gdn_a_log, m_gdn_dt_bias, m_gdn_norm_g, m_mla_q_norm_g, m_w_uq, m_mla_kv_norm_g, m_w_ukv, m_w_o_gdn, m_w_o_mla, m_w_out, m_norm_ffn_g, m_w_up, m_conv_ffn_w, m_w_down, m_norm_final_g, v_norm_mix_g, v_w_in, v_conv_qkv_w, v_gdn_a_log, v_gdn_dt_bias, v_gdn_norm_g, v_mla_q_norm_g, v_w_uq, v_mla_kv_norm_g, v_w_ukv, v_w_o_gdn, v_w_o_mla, v_w_out, v_norm_ffn_g, v_w_up, v_conv_ffn_w, v_w_down, v_norm_final_g):
    w = dict(norm_mix_g=norm_mix_g, w_in=w_in, conv_qkv_w=conv_qkv_w, gdn_a_log=gdn_a_log, gdn_dt_bias=gdn_dt_bias,
             gdn_norm_g=gdn_norm_g, mla_q_norm_g=mla_q_norm_g, w_uq=w_uq, mla_kv_norm_g=mla_kv_norm_g, w_ukv=w_ukv,
             w_o_gdn=w_o_gdn, w_o_mla=w_o_mla, w_out=w_out, norm_ffn_g=norm_ffn_g, w_up=w_up, conv_ffn_w=conv_ffn_w,
             w_down=w_down, norm_final_g=norm_final_g)
    m = dict(norm_mix_g=m_norm_mix_g, w_in=m_w_in, conv_qkv_w=m_conv_qkv_w, gdn_a_log=m_gdn_a_log, gdn_dt_bias=m_gdn_dt_bias,
             gdn_norm_g=m_gdn_norm_g, mla_q_norm_g=m_mla_q_norm_g, w_uq=m_w_uq, mla_kv_norm_g=m_mla_kv_norm_g, w_ukv=m_w_ukv,
             w_o_gdn=m_w_o_gdn, w_o_mla=m_w_o_mla, w_out=m_w_out, norm_ffn_g=m_norm_ffn_g, w_up=m_w_up,
             conv_ffn_w=m_conv_ffn_w, w_down=m_w_down, norm_final_g=m_norm_final_g)
    v = dict(norm_mix_g=v_norm_mix_g, w_in=v_w_in, conv_qkv_w=v_conv_qkv_w, gdn_a_log=v_gdn_a_log, gdn_dt_bias=v_gdn_dt_bias,
             gdn_norm_g=v_gdn_norm_g, mla_q_norm_g=v_mla_q_norm_g, w_uq=v_w_uq, mla_kv_norm_g=v_mla_kv_norm_g, w_ukv=v_w_ukv,
             w_o_gdn=v_w_o_gdn, w_o_mla=v_w_o_mla, w_out=v_w_out, norm_ffn_g=v_norm_ffn_g, w_up=v_w_up,
             conv_ffn_w=v_conv_ffn_w, w_down=v_w_down, norm_final_g=v_norm_final_g)
    slab_names = ("A", "Q", "C", "V")
    big_names = ("w_in", "w_up", "w_uq", "w_ukv", "w_o_gdn", "w_o_mla", "w_out", "w_down", "conv_qkv_w", "conv_ffn_w")
    small_names = [n for n, _ in SMALL]
    small_shapes = {n: w[n].shape for n in small_names}
    local2d = lambda d: {n: d[n][0] for n in big_names}

    w_slabs = _slabs(local2d(w), F32)
    send = [w_slabs[k].astype(BF16) if k != "V" else w_slabs[k] for k in slab_names]
    gathered = _allgather(send, name="allgather_weights")
    gathered = {k: g.reshape(N_DEV, -1, g.shape[1]) for k, g in zip(slab_names, gathered)}
    W, conv_qkv_full, conv_ffn_full = _layout_weights(gathered)

    loss_local, dx, dW, d_conv_qkv, d_conv_ffn, dsmall = _local_step(
        x, loss_target, W, conv_qkv_full, conv_ffn_full, {n: w[n] for n in small_names})

    g_send = _send_slabs(dW, d_conv_qkv, d_conv_ffn)
    recv = _alltoall([g_send[k] for k in slab_names], name="alltoall_grads")
    m_slabs, v_slabs = _slabs(local2d(m), F32), _slabs(local2d(v), F32)
    upd = {k: _reduce_adam(r, w_slabs[k], m_slabs[k], v_slabs[k], tr=SLAB_TR[k], name="adam_" + k)
           for k, r in zip(slab_names, recv)}
    small_parts = _allgather([_pack_small(dsmall)], name="allgather_small_grads")[0]
    upd_small = _reduce_adam(small_parts, _pack_small({n: w[n] for n in small_names}), _pack_small({n: m[n] for n in small_names}),
                             _pack_small({n: v[n] for n in small_names}), tr=SMALL_ROWS, name="adam_small")

    loss = lax.psum(loss_local, ("x", "y", "c"))
    groups = []
    for i in range(4):
        merged = {**_unslab({k: upd[k][i] for k in slab_names}), **_unpack_small(upd_small[i], small_shapes)}
        groups.append([merged[n] for n in WEIGHT_ORDER])
    return (loss, dx, *groups[0], *groups[1], *groups[2], *groups[3])
```

```python
import functools
import math

import numpy as np
import jax
import jax.numpy as jnp
from jax import lax
from jax.experimental import pallas as pl
from jax.experimental.pallas import tpu as pltpu

F32 = jnp.float32
BF16 = jnp.bfloat16

D_MODEL = 1024
HEADS = 8
HD = 128
GDN_CONV = 4
CHUNK = 64
Q_RANK = 384
KV_RANK = 256
ROPE = 64
ROPE_THETA = 10000.0
D_FF = 2816
FFN_CONV = 3
EPS = 1e-6
SM_SCALE = (HD + ROPE) ** -0.5
N_DEV = 8

ADAM_LR, ADAM_B1, ADAM_B2, ADAM_EPS, ADAM_WD, ADAM_STEP = 0.001, 0.9, 0.999, 1e-08, 0.01, 10

LANES = 128
SUBLANES = 8
HALO = 2 * SUBLANES
VMEM_LIMIT = 56 * 1024 * 1024
HI = lax.Precision.HIGHEST
TRI_PRECISION = None

NN = (((1,), (0,)), ((), ()))
NT = (((1,), (1,)), ((), ()))
TN = (((0,), (0,)), ((), ()))


def _dot(a, b, dims=NN, precision=None):
    return lax.dot_general(a, b, dims, precision=precision, preferred_element_type=F32)


def _pick(dim, target, align):
    best = None
    for t in range(align, min(dim, target) + 1, align):
        if dim % t == 0:
            best = t
    return dim if best is None else best


def _call(body, ins, outs, grid, *, name, scratch=(), semantics=None):
    n_in, n_out = len(ins), len(outs)

    def kern(*refs):
        body(refs[:n_in], refs[n_in:n_in + n_out], refs[n_in + n_out:])

    res = pl.pallas_call(
        kern,
        grid=grid,
        in_specs=[pl.BlockSpec(bs, im) for _, bs, im in ins],
        out_specs=[pl.BlockSpec(bs, im) for _, _, bs, im in outs],
        out_shape=[jax.ShapeDtypeStruct(s, d) for s, d, _, _ in outs],
        scratch_shapes=list(scratch),
        name=name,
        compiler_params=pltpu.CompilerParams(
            dimension_semantics=semantics or ("arbitrary",) * len(grid), vmem_limit_bytes=VMEM_LIMIT),
    )(*[a for a, _, _ in ins])
    return res


def _mm(a, b, mode, out_dtype, *, name, add=None, tm=1408, tn=1408, tk=1408):
    if mode == "nn":
        (M, K), (K2, N) = a.shape, b.shape
    elif mode == "nt":
        (M, K), (N, K2) = a.shape, b.shape
    else:
        (K, M), (K2, N) = a.shape, b.shape
    assert K == K2, (a.shape, b.shape, mode)
    tm = _pick(M, tm, LANES if mode == "tn" else 16)
    tn = _pick(N, tn, LANES)
    tk = _pick(K, tk, 16 if mode == "tn" else LANES)
    nk = K // tk
    dims = {"nn": NN, "nt": NT, "tn": TN}[mode]
    if mode == "nn":
        a_spec, b_spec = ((tm, tk), lambda i, j, k: (i, k)), ((tk, tn), lambda i, j, k: (k, j))
    elif mode == "nt":
        a_spec, b_spec = ((tm, tk), lambda i, j, k: (i, k)), ((tn, tk), lambda i, j, k: (j, k))
    else:
        a_spec, b_spec = ((tk, tm), lambda i, j, k: (k, i)), ((tk, tn), lambda i, j, k: (k, j))
    ins = [(a,) + a_spec, (b,) + b_spec]
    if add is not None:
        ins.append((add, (tm, tn), lambda i, j, k: (i, j)))
    outs = [((M, N), out_dtype, (tm, tn), lambda i, j, k: (i, j))]

    def body(in_refs, out_refs, scr):
        prod = _dot(in_refs[0][...].astype(BF16), in_refs[1][...].astype(BF16), dims)

        def finish(r):
            if add is not None:
                r = r + in_refs[2][...].astype(F32)
            out_refs[0][...] = r.astype(out_dtype)

        if nk == 1:
            finish(prod)
            return
        k = pl.program_id(2)
        acc = scr[0]

        @pl.when(k == 0)
        def _():
            acc[...] = prod

        @pl.when(k > 0)
        def _():
            acc[...] += prod

        @pl.when(k == nk - 1)
        def _():
            finish(acc[...])

    return _call(body, ins, outs, (M // tm, N // tn, nk), name=name,
                 scratch=[pltpu.VMEM((tm, tn), F32)] if nk > 1 else [],
                 semantics=("parallel", "parallel", "arbitrary"))[0]


def _row_call(fn, rows, consts, out_rows, out_accs=(), *, T, tm, name):
    nt = T // tm
    ins = []
    for r in rows:
        ins.append(r if isinstance(r, tuple) else (r, (tm, r.shape[1]), lambda i: (i, 0)))
    for c in consts:
        ins.append((c, c.shape, lambda i, nd=c.ndim: (0,) * nd))
    outs = []
    for o in out_rows:
        outs.append(((T, o[0]), o[1], (tm, o[0]), lambda i: (i, 0)) if len(o) == 2 else o)
    for shp, dt in out_accs:
        outs.append((shp, dt, shp, lambda i, nd=len(shp): (0,) * nd))
    n_r = len(out_rows)

    def body(in_refs, out_refs, _):
        i = pl.program_id(0)
        vals = fn(*[r[...] for r in in_refs])
        for o_ref, v in zip(out_refs[:n_r], vals[:n_r]):
            o_ref[...] = v.astype(o_ref.dtype)
        for o_ref, v in zip(out_refs[n_r:], vals[n_r:]):
            @pl.when(i == 0)
            def _(o_ref=o_ref):
                o_ref[...] = jnp.zeros_like(o_ref)

            o_ref[...] += v.astype(o_ref.dtype)

    return _call(body, ins, outs, (nt,), name=name)


def _rms(x, g):
    return x * lax.rsqrt(jnp.mean(x * x, axis=-1, keepdims=True) + EPS) * g


def _norm_fwd(x, g, *, T, tm, name):
    return _row_call(lambda xt, gt: (_rms(xt, gt),), [x], [g], [(x.shape[1], BF16)], T=T, tm=tm, name=name)[0]


def _norm_bwd(x, g, dh, dres, *, T, tm, name):
    def fn(xt, dht, drt, gt):
        _, vjp = jax.vjp(_rms, xt, gt)
        dx, dg = vjp(dht)
        return drt + dx, dg

    return _row_call(fn, [x, dh, dres], [g], [(x.shape[1], F32)], [(g.shape, F32)], T=T, tm=tm, name=name)


def _dwconv(tail, x, w):
    K, tm = w.shape[0], x.shape[0]
    xx = jnp.concatenate([tail, x], axis=0)
    acc = None
    for k in range(K):
        s = HALO - (K - 1) + k
        term = w[k:k + 1, :] * xx[s:s + tm, :]
        acc = term if acc is None else acc + term
    return acc


def _conv_fwd(fn, xs, ws, out_c, out_dtype, *, T, S, tm, cb, ncb, name):
    nt, tps, hb = T // tm, S // tm, tm // HALO
    ins = []
    for arr, off in xs:
        ins.append((arr, (tm, cb), lambda j, i, off=off: (i, off + j)))
        ins.append((arr, (HALO, cb), lambda j, i, off=off: (jnp.maximum(i * hb - 1, 0), off + j)))
    for arr, off in ws:
        ins.append((arr, (arr.shape[0], cb), lambda j, i, off=off: (0, off + j)))
    outs = [((T, out_c), out_dtype, (tm, cb), lambda j, i: (i, j))]
    nx = len(xs)

    def body(in_refs, out_refs, _):
        j, i = pl.program_id(0), pl.program_id(1)
        first = (i % tps) == 0
        xts = [in_refs[2 * m][...].astype(F32) for m in range(nx)]
        tails = [jnp.where(first, 0.0, in_refs[2 * m + 1][...].astype(F32)) for m in range(nx)]
        wts = [r[...] for r in in_refs[2 * nx:]]
        out_refs[0][...] = fn(j, tails, xts, wts).astype(out_dtype)

    return _call(body, ins, outs, (ncb, nt), name=name)[0]


def _conv_bwd(fn, xs, ws, dout, dx_dtype, *, T, S, tm, cb, ncb, name):
    nt, tps, hb = T // tm, S // tm, tm // HALO
    ins = []
    for arr, off in xs:
        ins.append((arr, (tm, cb), lambda j, i, off=off: (nt - 1 - i, off + j)))
        ins.append((arr, (HALO, cb), lambda j, i, off=off: (jnp.maximum((nt - 1 - i) * hb - 1, 0), off + j)))
    for arr, off in ws:
        ins.append((arr, (arr.shape[0], cb), lambda j, i, off=off: (0, off + j)))
    ins.append((dout, (tm, cb), lambda j, i: (nt - 1 - i, j)))
    nx, nw = len(xs), len(ws)
    outs = [((T, ncb * cb), dx_dtype, (tm, cb), lambda j, i: (nt - 1 - i, j)) for _ in xs]
    outs += [((arr.shape[0], ncb * cb), F32, (arr.shape[0], cb), lambda j, i: (0, j)) for arr, _ in ws]
    scratch = [pltpu.VMEM((HALO, cb), F32) for _ in xs]

    def body(in_refs, out_refs, carry):
        j, i = pl.program_id(0), pl.program_id(1)
        r = nt - 1 - i
        first = (r % tps) == 0
        xts = [in_refs[2 * m][...].astype(F32) for m in range(nx)]
        tails = [jnp.where(first, 0.0, in_refs[2 * m + 1][...].astype(F32)) for m in range(nx)]
        wts = [ref[...] for ref in in_refs[2 * nx:2 * nx + nw]]
        _, vjp = jax.vjp(lambda tl, xt, wt: fn(j, tl, xt, wt), tails, xts, wts)
        dtails, dxts, dwts = vjp(in_refs[-1][...].astype(F32))

        @pl.when(i == 0)
        def _():
            for c in carry:
                c[...] = jnp.zeros_like(c)

        for m in range(nx):
            pad = jnp.concatenate([jnp.zeros((tm - HALO, cb), F32), carry[m][...]], axis=0)
            out_refs[m][...] = (dxts[m] + pad).astype(dx_dtype)
            carry[m][...] = jnp.where(first, 0.0, dtails[m])
        for m in range(nw):
            o_ref = out_refs[nx + m]

            @pl.when(i == 0)
            def _(o_ref=o_ref):
                o_ref[...] = jnp.zeros_like(o_ref)

            o_ref[...] += dwts[m]

    return _call(body, ins, outs, (ncb, nt), name=name, scratch=scratch)


QKV_CB = 512


def _qkv_fn(j, tails, xts, wts):
    y = jax.nn.silu(_dwconv(tails[0], xts[0], wts[0]))
    scale = jnp.where(j < 2, HD ** -0.5, 1.0)
    parts = []
    for h in range(QKV_CB // HD):
        yh = y[:, h * HD:(h + 1) * HD]
        nh = yh * lax.rsqrt(jnp.sum(yh * yh, axis=-1, keepdims=True) + EPS)
        parts.append(jnp.where(j < 4, nh * scale, yh))
    return jnp.concatenate(parts, axis=1)


def _ffn_fn(j, tails, xts, wts):
    return jax.nn.silu(_dwconv(tails[0], xts[0], wts[0])) * _dwconv(tails[1], xts[1], wts[1])


BNN = (((2,), (1,)), ((0,), (0,)))
BNT = (((2,), (2,)), ((0,), (0,)))
BTN = (((1,), (1,)), ((0,), (0,)))


def _tri_inv(L):
    C = L.shape[-1]
    ii = lax.broadcasted_iota(jnp.int32, (C, C), 0)
    jj = lax.broadcasted_iota(jnp.int32, (C, C), 1)
    eye = (ii == jj).astype(F32)
    X = eye - jnp.where((ii >> 1) == (jj >> 1), L, 0.0)
    s = 1
    while (2 << s) <= C:
        E = jnp.where(((ii >> (s + 1)) == (jj >> (s + 1))) & ((ii >> s) != (jj >> s)), L, 0.0)
        X = X - _dot(_dot(X, E, BNN, precision=TRI_PRECISION), X, BNN, precision=TRI_PRECISION)
        s += 1
    return X


def _gdn_chunk(q, k, v, gc, gr, beta, S):
    C = q.shape[1]
    ii = lax.broadcasted_iota(jnp.int32, (C, C), 0)
    jj = lax.broadcasted_iota(jnp.int32, (C, C), 1)
    lower = ii >= jj
    decay = jnp.where(lower, jnp.exp(jnp.where(lower, gc - gr, 0.0)), 0.0)
    kb, vb = k * beta, v * beta
    L = jnp.where(ii > jj, _dot(kb, k, BNT) * decay, 0.0)
    Tinv = _tri_inv(L)
    eg = jnp.exp(gc)
    u = _dot(Tinv, vb, BNN, precision=TRI_PRECISION)
    w = _dot(Tinv, kb * eg, BNN, precision=TRI_PRECISION)
    a = _dot(q, k, BNT) * decay
    g_last = gc[:, C - 1:C, :]
    kd = k * jnp.exp(g_last - gc)
    v_new = u - _dot(w, S, BNN)
    o = _dot(q * eg, S, BNN) + _dot(a, v_new, BNN)
    S_new = S * jnp.exp(g_last) + _dot(kd, v_new, BTN)
    return o, S_new


def _heads(ref, width=HD):
    return jnp.stack([ref[:, h * width:(h + 1) * width].astype(F32) for h in range(HEADS)])


def _gdn_fwd(qkvn, gcum, grT, beta, *, B, S):
    N, T = S // CHUNK, B * S
    row = lambda c: (lambda b, n: (b * N + n, c))
    ins = [(qkvn, (CHUNK, 1024), row(0)), (qkvn, (CHUNK, 1024), row(1)), (qkvn, (CHUNK, 1024), row(2)),
           (gcum, (CHUNK, LANES), row(0)), (grT, (1, HEADS, 1, CHUNK), lambda b, n: (b * N + n, 0, 0, 0)),
           (beta, (CHUNK, LANES), row(0))]
    outs = [((T, 1024), F32, (CHUNK, 1024), row(0)),
            ((B * N, HEADS, HD, HD), BF16, (1, HEADS, HD, HD), lambda b, n: (b * N + n, 0, 0, 0))]

    def body(in_refs, out_refs, scr):
        q_ref, k_ref, v_ref, gc_ref, gr_ref, b_ref = in_refs
        o_ref, st_ref = out_refs
        S_ref = scr[0]

        @pl.when(pl.program_id(1) == 0)
        def _():
            S_ref[...] = jnp.zeros_like(S_ref)

        S0 = S_ref[...]
        st_ref[0] = S0.astype(BF16)
        o, Sn = _gdn_chunk(_heads(q_ref), _heads(k_ref), _heads(v_ref), _heads(gc_ref, 1), gr_ref[0],
                           _heads(b_ref, 1), S0)
        for h in range(HEADS):
            o_ref[:, h * HD:(h + 1) * HD] = o[h]
        S_ref[...] = Sn

    return _call(body, ins, outs, (B, N), name="gdn_core_fwd", scratch=[pltpu.VMEM((HEADS, HD, HD), F32)])


def _gdn_bwd(qkvn, gcum, grT, beta, states, do, *, B, S):
    N, T = S // CHUNK, B * S
    row = lambda c: (lambda b, n: (b * N + N - 1 - n, c))
    ins = [(qkvn, (CHUNK, 1024), row(0)), (qkvn, (CHUNK, 1024), row(1)), (qkvn, (CHUNK, 1024), row(2)),
           (gcum, (CHUNK, LANES), row(0)), (grT, (1, HEADS, 1, CHUNK), lambda b, n: (b * N + N - 1 - n, 0, 0, 0)),
           (beta, (CHUNK, LANES), row(0)),
           (states, (1, HEADS, HD, HD), lambda b, n: (b * N + N - 1 - n, 0, 0, 0)), (do, (CHUNK, 1024), row(0))]
    outs = [((T, 3072), BF16, (CHUNK, 3072), row(0)), ((T, LANES), F32, (CHUNK, LANES), row(0)),
            ((B * N, HEADS, 1, CHUNK), F32, (1, HEADS, 1, CHUNK), lambda b, n: (b * N + N - 1 - n, 0, 0, 0)),
            ((T, LANES), F32, (CHUNK, LANES), row(0))]

    def body(in_refs, out_refs, scr):
        q_ref, k_ref, v_ref, gc_ref, gr_ref, b_ref, st_ref, do_ref = in_refs
        dqkv_ref, dgc_ref, dgr_ref, db_ref = out_refs
        dS_ref = scr[0]

        @pl.when(pl.program_id(1) == 0)
        def _():
            dS_ref[...] = jnp.zeros_like(dS_ref)

        args = (_heads(q_ref), _heads(k_ref), _heads(v_ref), _heads(gc_ref, 1), gr_ref[0], _heads(b_ref, 1),
                st_ref[0].astype(F32))
        _, vjp = jax.vjp(_gdn_chunk, *args)
        dq, dk, dv, dgc, dgr, db, dS = vjp((_heads(do_ref), dS_ref[...]))
        lane = lax.broadcasted_iota(jnp.int32, (CHUNK, LANES), 1)
        dgc_all = jnp.zeros((CHUNK, LANES), F32)
        db_all = jnp.zeros((CHUNK, LANES), F32)
        for h in range(HEADS):
            dqkv_ref[:, h * HD:(h + 1) * HD] = dq[h].astype(BF16)
            dqkv_ref[:, 1024 + h * HD:1024 + (h + 1) * HD] = dk[h].astype(BF16)
            dqkv_ref[:, 2048 + h * HD:2048 + (h + 1) * HD] = dv[h].astype(BF16)
            dgc_all = jnp.where(lane == h, dgc[h], dgc_all)
            db_all = jnp.where(lane == h, db[h], db_all)
        dgc_ref[...] = dgc_all
        db_ref[...] = db_all
        dgr_ref[0] = dgr
        dS_ref[...] = dS

    return _call(body, ins, outs, (B, N), name="gdn_core_bwd", scratch=[pltpu.VMEM((HEADS, HD, HD), F32)])


def _gate_fn(za, zb, alog, dtb):
    tm = za.shape[0]
    g = -jnp.exp(alog) * jax.nn.softplus(za + dtb)
    ii = lax.broadcasted_iota(jnp.int32, (tm, tm), 0)
    jj = lax.broadcasted_iota(jnp.int32, (tm, tm), 1)
    tri = ((ii >= jj) & ((ii >> 6) == (jj >> 6))).astype(F32)
    return _dot(tri, g, precision=HI), jax.nn.sigmoid(zb)


def _scores(qn_ref, qp_ref, kn_ref, kp_ref, diag):
    q = jnp.concatenate([qn_ref[...], qp_ref[...]], axis=1)
    k = jnp.concatenate([kn_ref[...], kp_ref[...]], axis=1)
    s = _dot(q, k, NT) * SM_SCALE
    if diag:
        t = s.shape[0]
        ii = lax.broadcasted_iota(jnp.int32, (t, t), 0)
        jj = lax.broadcasted_iota(jnp.int32, (t, t), 1)
        s = jnp.where(ii >= jj, s, -jnp.inf)
    return s, q, k


HPB = 4
HW = HPB * HD


def _head_refs(refs, hh):
    return [r.at[:, hh * HD:(hh + 1) * HD] for r in refs]


def _flash_fwd(qn, qp, kn, kp, v, *, B, S, t):
    nb, T = S // t, B * S
    qmap = lambda b, h, qi, ki: (b * nb + qi, h)
    kmap = lambda b, h, qi, ki: (b * nb + jnp.minimum(ki, qi), h)
    kpmap = lambda b, h, qi, ki: (b * nb + jnp.minimum(ki, qi), 0)
    ins = [(qn, (t, HW), qmap), (qp, (t, HW), qmap), (kn, (t, HW), kmap), (kp, (t, HD), kpmap), (v, (t, HW), kmap)]
    outs = [((T, 1024), BF16, (t, HW), qmap),
            ((HEADS, T, 1), F32, (HPB, t, 1), lambda b, h, qi, ki: (h, b * nb + qi, 0))]
    scratch = [pltpu.VMEM((HPB, t, 1), F32), pltpu.VMEM((HPB, t, 1), F32), pltpu.VMEM((HPB, t, HD), F32)]

    def body(in_refs, out_refs, scr):
        qn_ref, qp_ref, kn_ref, kp_ref, v_ref = in_refs
        o_ref, lse_ref = out_refs
        m_ref, l_ref, acc_ref = scr
        qi, ki = pl.program_id(2), pl.program_id(3)

        @pl.when(ki == 0)
        def _():
            m_ref[...] = jnp.full_like(m_ref, -jnp.inf)
            l_ref[...] = jnp.zeros_like(l_ref)
            acc_ref[...] = jnp.zeros_like(acc_ref)

        def step(diag):
            for hh in range(HPB):
                qn_h, qp_h, kn_h, v_h = _head_refs((qn_ref, qp_ref, kn_ref, v_ref), hh)
                s, _, _ = _scores(qn_h, qp_h, kn_h, kp_ref, diag)
                m_old = m_ref[hh]
                m_new = jnp.maximum(m_old, jnp.max(s, axis=-1, keepdims=True))
                p = jnp.exp(s - m_new)
                alpha = jnp.exp(m_old - m_new)
                l_ref[hh] = alpha * l_ref[hh] + jnp.sum(p, axis=-1, keepdims=True)
                acc_ref[hh] = alpha * acc_ref[hh] + _dot(p.astype(BF16), v_h[...])
                m_ref[hh] = m_new

        @pl.when(ki < qi)
        def _():
            step(False)

        @pl.when(ki == qi)
        def _():
            step(True)
            for hh in range(HPB):
                o_ref[:, hh * HD:(hh + 1) * HD] = (acc_ref[hh] / l_ref[hh]).astype(BF16)
                lse_ref[hh] = m_ref[hh] + jnp.log(l_ref[hh])

    return _call(body, ins, outs, (B, HEADS // HPB, nb, nb), name="mla_flash_fwd", scratch=scratch,
                 semantics=("parallel", "parallel", "parallel", "arbitrary"))


def _flash_bwd_dq(qn, qp, kn, kp, v, o, do, lse, *, B, S, t):
    nb, T = S // t, B * S
    qmap = lambda b, h, qi, ki: (b * nb + qi, h)
    kmap = lambda b, h, qi, ki: (b * nb + jnp.minimum(ki, qi), h)
    kpmap = lambda b, h, qi, ki: (b * nb + jnp.minimum(ki, qi), 0)
    ins = [(qn, (t, HW), qmap), (qp, (t, HW), qmap), (kn, (t, HW), kmap), (kp, (t, HD), kpmap), (v, (t, HW), kmap),
           (o, (t, HW), qmap), (do, (t, HW), qmap), (lse, (HPB, t, 1), lambda b, h, qi, ki: (h, b * nb + qi, 0))]
    outs = [((T, 1024), BF16, (t, HW), qmap), ((T, 1024), F32, (t, HW), qmap)]
    scratch = [pltpu.VMEM((HPB, t, 1), F32), pltpu.VMEM((HPB, t, 2 * HD), F32)]

    def body(in_refs, out_refs, scr):
        qn_ref, qp_ref, kn_ref, kp_ref, v_ref, o_ref, do_ref, lse_ref = in_refs
        dqn_ref, dqp_ref = out_refs
        dl_ref, acc_ref = scr
        qi, ki = pl.program_id(2), pl.program_id(3)

        @pl.when(ki == 0)
        def _():
            for hh in range(HPB):
                o_h, do_h = _head_refs((o_ref, do_ref), hh)
                dl_ref[hh] = jnp.sum(do_h[...].astype(F32) * o_h[...].astype(F32), axis=-1, keepdims=True)
            acc_ref[...] = jnp.zeros_like(acc_ref)

        def step(diag):
            for hh in range(HPB):
                qn_h, qp_h, kn_h, v_h, do_h = _head_refs((qn_ref, qp_ref, kn_ref, v_ref, do_ref), hh)
                s, _, k = _scores(qn_h, qp_h, kn_h, kp_ref, diag)
                p = jnp.exp(s - lse_ref[hh])
                dp = _dot(do_h[...], v_h[...], NT)
                ds = p * (dp - dl_ref[hh]) * SM_SCALE
                acc_ref[hh] += _dot(ds.astype(BF16), k)

        @pl.when(ki < qi)
        def _():
            step(False)

        @pl.when(ki == qi)
        def _():
            step(True)
            for hh in range(HPB):
                dqn_ref[:, hh * HD:(hh + 1) * HD] = acc_ref[hh, :, :HD].astype(BF16)
                dqp_ref[:, hh * HD:(hh + 1) * HD] = acc_ref[hh, :, HD:]

    return _call(body, ins, outs, (B, HEADS // HPB, nb, nb), name="mla_flash_bwd_dq", scratch=scratch,
                 semantics=("parallel", "parallel", "parallel", "arbitrary"))


def _flash_bwd_dkv(qn, qp, kn, kp, v, o, do, lse, *, B, S, t):
    nb, T = S // t, B * S
    qmap = lambda b, h, ki, qi: (b * nb + jnp.maximum(qi, ki), h)
    kmap = lambda b, h, ki, qi: (b * nb + ki, h)
    ins = [(qn, (t, HW), qmap), (qp, (t, HW), qmap), (kn, (t, HW), kmap),
           (kp, (t, HD), lambda b, h, ki, qi: (b * nb + ki, 0)), (v, (t, HW), kmap),
           (o, (t, HW), qmap), (do, (t, HW), qmap),
           (lse, (HPB, t, 1), lambda b, h, ki, qi: (h, b * nb + jnp.maximum(qi, ki), 0))]
    outs = [((T, 1024), BF16, (t, HW), kmap), ((HEADS, T, HD), F32, (HPB, t, HD), lambda b, h, ki, qi: (h, b * nb + ki, 0)),
            ((T, 1024), BF16, (t, HW), kmap)]
    scratch = [pltpu.VMEM((HPB, t, 2 * HD), F32), pltpu.VMEM((HPB, t, HD), F32)]

    def body(in_refs, out_refs, scr):
        qn_ref, qp_ref, kn_ref, kp_ref, v_ref, o_ref, do_ref, lse_ref = in_refs
        dkn_ref, dkp_ref, dv_ref = out_refs
        dk_acc, dv_acc = scr
        ki, qi = pl.program_id(2), pl.program_id(3)

        @pl.when(qi == 0)
        def _():
            dk_acc[...] = jnp.zeros_like(dk_acc)
            dv_acc[...] = jnp.zeros_like(dv_acc)

        def step(diag):
            for hh in range(HPB):
                qn_h, qp_h, kn_h, v_h, o_h, do_h = _head_refs((qn_ref, qp_ref, kn_ref, v_ref, o_ref, do_ref), hh)
                s, q, _ = _scores(qn_h, qp_h, kn_h, kp_ref, diag)
                do_t = do_h[...]
                p = jnp.exp(s - lse_ref[hh])
                dl = jnp.sum(do_t.astype(F32) * o_h[...].astype(F32), axis=-1, keepdims=True)
                dp = _dot(do_t, v_h[...], NT)
                ds = p * (dp - dl) * SM_SCALE
                dv_acc[hh] += _dot(p.astype(BF16), do_t, TN)
                dk_acc[hh] += _dot(ds.astype(BF16), q, TN)

        @pl.when(qi > ki)
        def _():
            step(False)

        @pl.when(qi == ki)
        def _():
            step(True)

        @pl.when(qi == nb - 1)
        def _():
            for hh in range(HPB):
                dkn_ref[:, hh * HD:(hh + 1) * HD] = dk_acc[hh, :, :HD].astype(BF16)
                dkp_ref[hh] = dk_acc[hh, :, HD:]
                dv_ref[:, hh * HD:(hh + 1) * HD] = dv_acc[hh].astype(BF16)

    return _call(body, ins, outs, (B, HEADS // HPB, nb, nb), name="mla_flash_bwd_dkv", scratch=scratch,
                 semantics=("parallel", "parallel", "parallel", "arbitrary"))


def _allgather(shards, *, name):
    n_arr = len(shards)

    def body(*refs):
        x_refs, out_refs = refs[:n_arr], refs[n_arr:2 * n_arr]
        send_sems, recv_sems, local_sems = refs[2 * n_arr:]
        x, y, c = lax.axis_index("x"), lax.axis_index("y"), lax.axis_index("c")
        me, sibling = (x, y, c), (x, y, 1 - c)
        chips = [(1 - x, y), (x, 1 - y), (1 - x, 1 - y)]

        def rows(a, px, py, pc):
            m_per = shards[a].shape[0]
            return out_refs[a].at[pl.ds((4 * px + 2 * py + pc) * m_per, m_per), :]

        def copy(a, k, block, to, src=None):
            return pltpu.make_async_remote_copy(
                src_ref=rows(a, *block) if src is None else src, dst_ref=rows(a, *block),
                send_sem=send_sems.at[a, k], recv_sem=recv_sems.at[a, k], device_id=to,
                device_id_type=pl.DeviceIdType.MESH)

        mine = [pltpu.make_async_copy(x_refs[a], rows(a, *me), local_sems.at[a]) for a in range(n_arr)]
        for cp in mine:
            cp.start()
        first = []
        for a in range(n_arr):
            first.append(copy(a, 0, me, sibling, src=x_refs[a]))
            first += [copy(a, 1 + j, me, (*chip, c), src=x_refs[a]) for j, chip in enumerate(chips)]
        for cp in first:
            cp.start()
        passed = []
        for j, chip in enumerate(chips):
            for a in range(n_arr):
                copy(a, 1 + j, (*chip, c), me).wait_recv()
                cp = copy(a, 4 + j, (*chip, c), sibling)
                cp.start()
                passed.append(cp)
        for a in range(n_arr):
            copy(a, 0, sibling, me).wait_recv()
        for j, chip in enumerate(chips):
            for a in range(n_arr):
                copy(a, 4 + j, (*chip, 1 - c), me).wait_recv()
        for cp in first + passed:
            cp.wait_send()
        for cp in mine:
            cp.wait()

    return pl.pallas_call(
        body,
        out_shape=[jax.ShapeDtypeStruct((N_DEV * s.shape[0], s.shape[1]), s.dtype) for s in shards],
        in_specs=[pl.BlockSpec(memory_space=pl.ANY)] * n_arr,
        out_specs=[pl.BlockSpec(memory_space=pl.ANY)] * n_arr,
        scratch_shapes=[pltpu.SemaphoreType.DMA((n_arr, 7)), pltpu.SemaphoreType.DMA((n_arr, 7)),
                        pltpu.SemaphoreType.DMA((n_arr,))],
        name=name,
    )(*shards)


def _alltoall(sends, *, name):
    n_arr = len(sends)

    def body(*refs):
        s_refs, r_refs = refs[:n_arr], refs[n_arr:2 * n_arr]
        send_sems, recv_sems, local_sems = refs[2 * n_arr:]
        x, y, c = lax.axis_index("x"), lax.axis_index("y"), lax.axis_index("c")
        me = 4 * x + 2 * y + c

        def rows(ref, a, idx):
            m_per = sends[a].shape[0] // N_DEV
            return ref.at[pl.ds(idx * m_per, m_per), :]

        local = [pltpu.make_async_copy(rows(s_refs[a], a, me), rows(r_refs[a], a, me), local_sems.at[a])
                 for a in range(n_arr)]
        for cp in local:
            cp.start()
        copies = []
        for k in range(1, N_DEV):
            px = 1 - x if k & 4 else x
            py = 1 - y if k & 2 else y
            pc = 1 - c if k & 1 else c
            for a in range(n_arr):
                cp = pltpu.make_async_remote_copy(
                    src_ref=rows(s_refs[a], a, 4 * px + 2 * py + pc), dst_ref=rows(r_refs[a], a, me),
                    send_sem=send_sems.at[a, k - 1], recv_sem=recv_sems.at[a, k - 1],
                    device_id=(px, py, pc), device_id_type=pl.DeviceIdType.MESH)
                cp.start()
                copies.append(cp)
        for cp in copies:
            cp.wait()
        for cp in local:
            cp.wait()

    return pl.pallas_call(
        body,
        out_shape=[jax.ShapeDtypeStruct(s.shape, s.dtype) for s in sends],
        in_specs=[pl.BlockSpec(memory_space=pl.ANY)] * n_arr,
        out_specs=[pl.BlockSpec(memory_space=pl.ANY)] * n_arr,
        scratch_shapes=[pltpu.SemaphoreType.DMA((n_arr, 7)), pltpu.SemaphoreType.DMA((n_arr, 7)),
                        pltpu.SemaphoreType.DMA((n_arr,))],
        name=name,
    )(*sends)


def _reduce_adam(parts, w, m, v, *, tr, name):
    R, C = w.shape
    nR = R // tr
    ins = [(parts, (tr, C), lambda i, s=s: (s * nR + i, 0)) for s in range(N_DEV)]
    ins += [(a, (tr, C), lambda i: (i, 0)) for a in (w, m, v)]
    outs = [((R, C), F32, (tr, C), lambda i: (i, 0)) for _ in range(4)]
    c1 = 1.0 - ADAM_B1 ** ADAM_STEP
    c2 = 1.0 - ADAM_B2 ** ADAM_STEP

    def body(in_refs, out_refs, _):
        g = in_refs[0][...].astype(F32)
        for s in range(1, N_DEV):
            g = g + in_refs[s][...].astype(F32)
        wv, mv, vv = in_refs[8][...], in_refs[9][...], in_refs[10][...]
        mn = ADAM_B1 * mv + (1.0 - ADAM_B1) * g
        vn = ADAM_B2 * vv + (1.0 - ADAM_B2) * (g * g)
        delta = -ADAM_LR * ((mn / c1) / (jnp.sqrt(vn / c2) + ADAM_EPS) + ADAM_WD * wv)
        out_refs[0][...] = g
        out_refs[1][...] = delta
        out_refs[2][...] = mn
        out_refs[3][...] = vn

    return _call(body, ins, outs, (nR,), name=name, semantics=("parallel",))


IN_C, UP_C, UQ_C, QKV_C = 858, 704, 192, 384
A_W, Q_W, V_W = 896, 256, 768
SLAB_TR = {"A": 256, "Q": 128, "C": 368, "V": 16}
SMALL = [("norm_mix_g", 1024), ("gdn_a_log", 8), ("gdn_dt_bias", 8), ("gdn_norm_g", 128), ("mla_q_norm_g", 384),
         ("mla_kv_norm_g", 256), ("norm_ffn_g", 1024), ("norm_final_g", 1024)]
SMALL_ROWS = 32
WEIGHT_ORDER = ["norm_mix_g", "w_in", "conv_qkv_w", "gdn_a_log", "gdn_dt_bias", "gdn_norm_g", "mla_q_norm_g", "w_uq",
                "mla_kv_norm_g", "w_ukv", "w_o_gdn", "w_o_mla", "w_out", "norm_ffn_g", "w_up", "conv_ffn_w", "w_down",
                "norm_final_g"]


def _padc(w, n):
    return jnp.pad(w, ((0, 0), (0, n - w.shape[1])))


def _padrc(w, r, n):
    return jnp.pad(w, ((0, r - w.shape[0]), (0, n - w.shape[1])))


def _slabs(p, dtype):
    A = jnp.concatenate([_padc(p["w_in"], A_W), _padc(p["w_up"], A_W)], axis=0).astype(dtype)
    Q = jnp.concatenate([_padc(p["w_uq"], Q_W), p["w_ukv"]], axis=0).astype(dtype)
    C = jnp.concatenate([p["w_o_gdn"], p["w_o_mla"], p["w_out"], p["w_down"]], axis=0).astype(dtype)
    V = jnp.concatenate([_padrc(p["conv_qkv_w"], 8, V_W), _padrc(p["conv_ffn_w"], 8, V_W)], axis=0).astype(F32)
    return {"A": A, "Q": Q, "C": C, "V": V}


def _unslab(sl):
    A, Q, C, V = sl["A"], sl["Q"], sl["C"], sl["V"]
    out = {"w_in": A[:1024, :IN_C], "w_up": A[1024:, :UP_C], "w_uq": Q[:384, :UQ_C], "w_ukv": Q[384:],
           "w_o_gdn": C[0:128], "w_o_mla": C[128:256], "w_out": C[256:384], "w_down": C[384:],
           "conv_qkv_w": V[0:GDN_CONV, :QKV_C], "conv_ffn_w": V[8:8 + FFN_CONV, :UP_C]}
    return {k: a[None] for k, a in out.items()}


def _take_cols(pieces, lo, hi):
    out, off = [], 0
    for arr, a, b in pieces:
        s, e = max(lo, off), min(hi, off + b - a)
        if s < e:
            out.append(arr[:, a + s - off:a + e - off])
        off += b - a
    return out[0] if len(out) == 1 else jnp.concatenate(out, axis=1)


def _pack_small(d):
    flat = jnp.concatenate([d[n].reshape(-1).astype(F32) for n, _ in SMALL])
    return jnp.pad(flat, (0, SMALL_ROWS * LANES - flat.shape[0])).reshape(SMALL_ROWS, LANES)


def _unpack_small(buf, shapes):
    flat, out, off = buf.reshape(-1), {}, 0
    for name, n in SMALL:
        out[name] = flat[off:off + n].reshape(shapes[name])
        off += n
    return out


def _rot_cols(w):
    h = ROPE // 2
    return jnp.concatenate([-w[:, h:], w[:, :h]], axis=1)


def _unrot_cols(dw):
    h = ROPE // 2
    return jnp.concatenate([dw[:, h:], -dw[:, :h]], axis=1)


IN_SPLITS = [0, 3072, 4096, 4104, 4112, 4496, 4752, 4816, 5840, 6864]


def _layout_weights(g):
    A, Q, C, V = g["A"], g["Q"], g["C"], g["V"]
    in_pieces = [(A[j, :1024], 0, IN_C) for j in range(N_DEV)]
    o = IN_SPLITS
    take = lambda lo, hi: _take_cols(in_pieces, lo, hi)
    kpe = take(o[6], o[7])
    W = {
        "in_qkv": take(o[0], o[1]),
        "in_ga": take(o[1], o[2]),
        "in_ab": jnp.concatenate([_padc(take(o[2], o[3]), LANES), _padc(take(o[3], o[4]), LANES)], axis=1),
        "in_small": jnp.concatenate([take(o[4], o[6]), _padc(kpe, LANES), _padc(_rot_cols(kpe), LANES)], axis=1),
        "in_gbr": take(o[7], o[9]),
        "w_up": jnp.concatenate([A[j, 1024:, :UP_C] for j in range(N_DEV)], axis=1),
        "uq_n": jnp.concatenate([Q[j, :384, :HD] for j in range(N_DEV)], axis=1),
        "ukv_k": jnp.concatenate([Q[j, 384:, :HD] for j in range(N_DEV)], axis=1),
        "ukv_v": jnp.concatenate([Q[j, 384:, HD:] for j in range(N_DEV)], axis=1),
        "w_o_gdn": C[:, 0:128].reshape(1024, D_MODEL),
        "w_o_mla": C[:, 128:256].reshape(1024, D_MODEL),
        "w_out": C[:, 256:384].reshape(1024, D_MODEL),
        "w_down": C[:, 384:].reshape(D_FF, D_MODEL),
    }
    pe = [Q[j, :384, HD:HD + ROPE] for j in range(N_DEV)]
    W["uq_p"] = jnp.concatenate([_padc(p, HD) for p in pe] + [_padc(_rot_cols(p), HD) for p in pe], axis=1)
    conv_qkv = jnp.concatenate([V[j, 0:GDN_CONV, :QKV_C] for j in range(N_DEV)], axis=1)
    conv_ffn = jnp.concatenate([V[j, 8:8 + FFN_CONV, :UP_C] for j in range(N_DEV)], axis=1)
    return {k: v.astype(BF16) for k, v in W.items()}, conv_qkv, conv_ffn


def _full_grads(dW):
    s = dW["in_small"]
    dkpe = s[:, 640:704] + _unrot_cols(s[:, 768:832])
    in_pieces = [(dW["in_qkv"], 0, 3072), (dW["in_ga"], 0, 1024), (dW["in_ab"], 0, 8), (dW["in_ab"], 128, 136),
                 (s, 0, 640), (dkpe, 0, ROPE), (dW["in_gbr"], 0, 2048)]
    pe = []
    for j in range(N_DEV):
        lin = dW["uq_p"][:, j * HD:j * HD + ROPE]
        rot = dW["uq_p"][:, 1024 + j * HD:1024 + j * HD + ROPE]
        pe.append(lin + _unrot_cols(rot))
    return in_pieces, pe


def _send_slabs(dW, d_conv_qkv, d_conv_ffn):
    in_pieces, pe = _full_grads(dW)
    A, Q, V = [], [], []
    for j in range(N_DEV):
        gin = _padc(_take_cols(in_pieces, j * IN_C, (j + 1) * IN_C), A_W)
        gup = _padc(dW["w_up"][:, j * UP_C:(j + 1) * UP_C], A_W)
        A.append(jnp.concatenate([gin, gup], axis=0))
        guq = _padc(jnp.concatenate([dW["uq_n"][:, j * HD:(j + 1) * HD], pe[j]], axis=1), Q_W)
        gukv = jnp.concatenate([dW["ukv_k"][:, j * HD:(j + 1) * HD], dW["ukv_v"][:, j * HD:(j + 1) * HD]], axis=1)
        Q.append(jnp.concatenate([guq, gukv], axis=0))
        V.append(jnp.concatenate([_padrc(d_conv_qkv[:, j * QKV_C:(j + 1) * QKV_C], 8, V_W),
                                  _padrc(d_conv_ffn[:, j * UP_C:(j + 1) * UP_C], 8, V_W)], axis=0))
    C = jnp.concatenate([dW["w_o_gdn"].reshape(N_DEV, 128, D_MODEL), dW["w_o_mla"].reshape(N_DEV, 128, D_MODEL),
                         dW["w_out"].reshape(N_DEV, 128, D_MODEL), dW["w_down"].reshape(N_DEV, 352, D_MODEL)], axis=1)
    return {"A": jnp.concatenate(A, axis=0).astype(BF16), "Q": jnp.concatenate(Q, axis=0).astype(BF16),
            "C": C.reshape(N_DEV * 736, D_MODEL).astype(BF16), "V": jnp.concatenate(V, axis=0)}


def _rope_tables(S):
    half = ROPE // 2
    inv = ROPE_THETA ** (-jnp.arange(half, dtype=F32) / half)
    ang = jnp.arange(S, dtype=F32)[:, None] * inv[None, :]
    cos = jnp.concatenate([jnp.cos(ang), jnp.cos(ang)], axis=1)
    sin = jnp.concatenate([jnp.sin(ang), jnp.sin(ang)], axis=1)
    return _padc(cos, HD), _padc(sin, HD)


def _local_step(x, tgt, W, conv_qkv_w, conv_ffn_w, small, tm=None, ta=None):
    B, S, _ = x.shape
    T = B * S
    tm = tm or _pick(S, 256, CHUNK)
    ta = ta or _pick(S, 512, LANES)
    x2d, tgt2d = x.reshape(T, D_MODEL), tgt.reshape(T, D_MODEL)
    row = lambda v: v.reshape(1, -1).astype(F32)
    pad_row = lambda v: _padc(row(v), LANES)
    g_mix, g_ffn, g_fin = row(small["norm_mix_g"]), row(small["norm_ffn_g"]), row(small["norm_final_g"])
    g_gdn, g_q, g_kv = row(small["gdn_norm_g"]), row(small["mla_q_norm_g"]), row(small["mla_kv_norm_g"])
    alog, dtb = pad_row(small["gdn_a_log"]), pad_row(small["gdn_dt_bias"])
    cos, sin = _rope_tables(S)
    tps = S // tm
    tab = lambda a: (a, (tm, HD), lambda i: (i % tps, 0))
    col = lambda a, c, w: (a, (tm, w), lambda i, c=c: (i, c))

    h1 = _norm_fwd(x2d, g_mix, T=T, tm=tm, name="norm_mix_fwd")
    z_qkv = _mm(h1, W["in_qkv"], "nn", BF16, name="in_qkv_fwd")
    z_ga = _mm(h1, W["in_ga"], "nn", BF16, name="in_ga_fwd")
    z_ab = _mm(h1, W["in_ab"], "nn", F32, name="in_ab_fwd")
    z_small = _mm(h1, W["in_small"], "nn", F32, name="in_small_fwd", tn=896)
    z_gbr = _mm(h1, W["in_gbr"], "nn", BF16, name="in_gbr_fwd")

    qkvn = _conv_fwd(_qkv_fn, [(z_qkv, 0)], [(conv_qkv_w, 0)], 3072, BF16, T=T, S=S, tm=tm, cb=QKV_CB,
                     ncb=3072 // QKV_CB, name="gdn_qkv_fwd")
    gcum, beta = _row_call(lambda za, zb, al, db: _gate_fn(za, zb, al, db), [col(z_ab, 0, LANES), col(z_ab, 1, LANES)],
                           [alog, dtb], [(LANES, F32), (LANES, F32)], T=T, tm=tm, name="gdn_gate_fwd")
    grT = gcum[:, :HEADS].reshape(T // CHUNK, CHUNK, HEADS).transpose(0, 2, 1)[:, :, None, :]
    o_gdn, states = _gdn_fwd(qkvn, gcum, grT, beta, B=B, S=S)

    def gdn_out_fn(o, ga, g):
        parts = []
        for h in range(HEADS):
            sl = slice(h * HD, (h + 1) * HD)
            parts.append(_rms(o[:, sl], g) * jax.nn.silu(ga[:, sl].astype(F32)))
        return jnp.concatenate(parts, axis=1)

    oa = _row_call(lambda o, ga, g: (gdn_out_fn(o, ga, g),), [o_gdn, z_ga], [g_gdn], [(1024, BF16)], T=T, tm=tm,
                   name="gdn_out_fwd")[0]

    def mla_prep_fn(zq, zkv, zpl, zpr, c, s, gq, gkv):
        return _rms(zq, gq), _rms(zkv, gkv), zpl * c + zpr * s

    small_cols = [(z_small, (tm, Q_RANK), lambda i: (i, 0)), (z_small, (tm, LANES), lambda i: (i, 3)),
                  (z_small, (tm, LANES), lambda i: (i, 4)), (z_small, (tm, LANES), lambda i: (i, 5)),
                  (z_small, (tm, LANES), lambda i: (i, 6))]

    def mla_prep_fwd(zq, zkv0, zkv1, zpl, zpr, c, s, gq, gkv):
        return mla_prep_fn(zq, jnp.concatenate([zkv0, zkv1], axis=1), zpl, zpr, c, s, gq, gkv)

    cq, ckv, kpe = _row_call(mla_prep_fwd, small_cols + [tab(cos), tab(sin)], [g_q, g_kv],
                             [(Q_RANK, BF16), (KV_RANK, BF16), (HD, BF16)], T=T, tm=tm, name="mla_prep_fwd")
    qn = _mm(cq, W["uq_n"], "nn", BF16, name="uq_n_fwd")
    qpl = _mm(cq, W["uq_p"], "nn", F32, name="uq_p_fwd")
    kn = _mm(ckv, W["ukv_k"], "nn", BF16, name="ukv_k_fwd")
    vb = _mm(ckv, W["ukv_v"], "nn", BF16, name="ukv_v_fwd")

    def qrope_fn(lin, rot, c, s):
        return lin * jnp.tile(c, (1, HEADS)) + rot * jnp.tile(s, (1, HEADS))

    qp = _row_call(lambda lin, rot, c, s: (qrope_fn(lin, rot, c, s),), [col(qpl, 0, 1024), col(qpl, 1, 1024), tab(cos), tab(sin)],
                   [], [(1024, BF16)], T=T, tm=tm, name="q_rope_fwd")[0]
    ob, lse = _flash_fwd(qn, qp, kn, kpe, vb, B=B, S=S, t=ta)

    def merge_fn(ya, yb, ga, gb):
        return jax.nn.sigmoid(ga.astype(F32)) * ya + jax.nn.sigmoid(gb.astype(F32)) * yb

    def merge_fwd(oat, obt, ga, gb, wog, wom):
        ya, yb = _dot(oat, wog), _dot(obt, wom)
        return ya, yb, merge_fn(ya, yb, ga, gb)

    ya, yb, merged = _row_call(merge_fwd, [oa, ob, col(z_gbr, 0, 1024), col(z_gbr, 1, 1024)], [W["w_o_gdn"], W["w_o_mla"]],
                               [(1024, BF16), (1024, BF16), (1024, BF16)], T=T, tm=tm, name="merge_fwd")
    x1 = _mm(merged, W["w_out"], "nn", F32, add=x2d, name="w_out_fwd")

    h2 = _norm_fwd(x1, g_ffn, T=T, tm=tm, name="norm_ffn_fwd")
    up = _mm(h2, W["w_up"], "nn", BF16, name="w_up_fwd")
    FCB = 256
    nfb = D_FF // FCB
    f = _conv_fwd(_ffn_fn, [(up, 0), (up, 2)], [(conv_ffn_w, 0), (conv_ffn_w, 2)], D_FF, BF16, T=T, S=S, tm=tm,
                  cb=D_FF // 2, ncb=2, name="ffn_act_fwd")
    x2 = _mm(f, W["w_down"], "nn", F32, add=x1, name="w_down_fwd", tk=1408)

    def final_fn(xt, tt, g):
        def lossf(xv, gv):
            e = _rms(xv, gv) - tt
            return 0.5 * jnp.sum(jnp.mean(e * e, axis=-1))

        l, vjp = jax.vjp(lossf, xt, g)
        dx, dg = vjp(jnp.ones((), F32))
        return dx, jnp.full((1, LANES), l, F32), dg

    dx2, loss_v, dg_fin = _row_call(final_fn, [x2, tgt2d], [g_fin], [(1024, F32)], [((1, LANES), F32), ((1, 1024), F32)],
                                    T=T, tm=tm, name="loss_head")

    dW = {}
    df = _mm(dx2, W["w_down"], "nt", BF16, name="w_down_dx")
    dW["w_down"] = _mm(f, dx2, "tn", F32, name="w_down_dw")
    dug, duu, dcw_g, dcw_u = _conv_bwd(_ffn_fn, [(up, 0), (up, nfb)], [(conv_ffn_w, 0), (conv_ffn_w, nfb)], df, BF16,
                                       T=T, S=S, tm=tm, cb=FCB, ncb=nfb, name="ffn_act_bwd")
    d_conv_ffn = jnp.concatenate([dcw_g, dcw_u], axis=1)
    wup_g, wup_u = W["w_up"][:, :D_FF], W["w_up"][:, D_FF:]
    dh2 = _mm(dug, wup_g, "nt", F32, name="w_up_dx_g")
    dh2 = _mm(duu, wup_u, "nt", F32, add=dh2, name="w_up_dx_u")
    dW["w_up"] = jnp.concatenate([_mm(h2, dug, "tn", F32, name="w_up_dw_g"), _mm(h2, duu, "tn", F32, name="w_up_dw_u")], axis=1)
    dx1, dg_ffn = _norm_bwd(x1, g_ffn, dh2, dx2, T=T, tm=tm, name="norm_ffn_bwd")

    dmerged = _mm(dx1, W["w_out"], "nt", F32, name="w_out_dx")
    dW["w_out"] = _mm(merged, dx1, "tn", F32, name="w_out_dw")

    def merge_bwd(dm, yat, ybt, ga, gb):
        _, vjp = jax.vjp(merge_fn, yat.astype(F32), ybt.astype(F32), ga, gb)
        return vjp(dm)

    dya, dyb, dgbr_a, dgbr_b = _row_call(merge_bwd, [dmerged, ya, yb, col(z_gbr, 0, 1024), col(z_gbr, 1, 1024)], [],
                                         [(1024, BF16)] * 4, T=T, tm=tm, name="merge_bwd")
    doa = _mm(dya, W["w_o_gdn"], "nt", F32, name="w_o_gdn_dx")
    dob = _mm(dyb, W["w_o_mla"], "nt", BF16, name="w_o_mla_dx")
    dW["w_o_gdn"] = _mm(oa, dya, "tn", F32, name="w_o_gdn_dw")
    dW["w_o_mla"] = _mm(ob, dyb, "tn", F32, name="w_o_mla_dw")

    dqn, dqp = _flash_bwd_dq(qn, qp, kn, kpe, vb, ob, dob, lse, B=B, S=S, t=ta)
    dkn, dkp, dvb = _flash_bwd_dkv(qn, qp, kn, kpe, vb, ob, dob, lse, B=B, S=S, t=ta)

    def qrope_bwd(d, c, s):
        return d * jnp.tile(c, (1, HEADS)), d * jnp.tile(s, (1, HEADS))

    dq_lin, dq_rot = _row_call(qrope_bwd, [dqp, tab(cos), tab(sin)], [], [(1024, BF16), (1024, BF16)], T=T, tm=tm,
                               name="q_rope_bwd")
    wp_lin, wp_rot = W["uq_p"][:, :1024], W["uq_p"][:, 1024:]
    dcq = _mm(dqn, W["uq_n"], "nt", F32, name="uq_n_dx")
    dcq = _mm(dq_lin, wp_lin, "nt", F32, add=dcq, name="uq_pl_dx")
    dcq = _mm(dq_rot, wp_rot, "nt", F32, add=dcq, name="uq_pr_dx")
    dW["uq_n"] = _mm(cq, dqn, "tn", F32, name="uq_n_dw")
    dW["uq_p"] = jnp.concatenate([_mm(cq, dq_lin, "tn", F32, name="uq_pl_dw"), _mm(cq, dq_rot, "tn", F32, name="uq_pr_dw")], axis=1)
    dckv = _mm(dkn, W["ukv_k"], "nt", F32, name="ukv_k_dx")
    dckv = _mm(dvb, W["ukv_v"], "nt", F32, add=dckv, name="ukv_v_dx")
    dW["ukv_k"] = _mm(ckv, dkn, "tn", F32, name="ukv_k_dw")
    dW["ukv_v"] = _mm(ckv, dvb, "tn", F32, name="ukv_v_dw")

    def mla_prep_bwd(zq, zkv0, zkv1, zpl, zpr, c, s, dcqt, dckvt, dkpt, gq, gkv):
        zkv = jnp.concatenate([zkv0, zkv1], axis=1)
        _, vjp = jax.vjp(lambda a, b, p, r, g1, g2: mla_prep_fn(a, b, p, r, c, s, g1, g2), zq, zkv, zpl, zpr, gq, gkv)
        dk = dkpt[0]
        for h in range(1, HEADS):
            dk = dk + dkpt[h]
        dzq, dzkv, dzpl, dzpr, dgq, dgkv = vjp((dcqt, dckvt, dk))
        return jnp.concatenate([dzq, dzkv, dzpl, dzpr], axis=1), dgq, dgkv

    dz_small, dg_q, dg_kv = _row_call(
        mla_prep_bwd, small_cols + [tab(cos), tab(sin), dcq, dckv, (dkp, (HEADS, tm, HD), lambda i: (0, i, 0))],
        [g_q, g_kv], [(896, BF16)], [((1, Q_RANK), F32), ((1, KV_RANK), F32)], T=T, tm=tm, name="mla_prep_bwd")

    def gdn_out_bwd(o, ga, dot_, g):
        _, vjp = jax.vjp(gdn_out_fn, o, ga, g)
        return vjp(dot_)

    do_gdn, dz_ga, dg_gdn = _row_call(gdn_out_bwd, [o_gdn, z_ga, doa], [g_gdn], [(1024, F32), (1024, BF16)],
                                      [((1, HD), F32)], T=T, tm=tm, name="gdn_out_bwd")
    dqkvn, dgc, dgrT, dbeta = _gdn_bwd(qkvn, gcum, grT, beta, states, do_gdn, B=B, S=S)
    dgc_tot = dgc + _padc(dgrT[:, :, 0, :].transpose(0, 2, 1).reshape(T, HEADS), LANES)

    def gate_bwd(za, zb, dg, db, al, db_):
        _, vjp = jax.vjp(_gate_fn, za, zb, al, db_)
        return vjp((dg, db))

    dz_a, dz_b, d_alog, d_dtb = _row_call(gate_bwd, [col(z_ab, 0, LANES), col(z_ab, 1, LANES), dgc_tot, dbeta], [alog, dtb],
                                          [(LANES, BF16), (LANES, BF16)], [((1, LANES), F32), ((1, LANES), F32)],
                                          T=T, tm=tm, name="gdn_gate_bwd")
    dz_qkv, d_conv_qkv = _conv_bwd(_qkv_fn, [(z_qkv, 0)], [(conv_qkv_w, 0)], dqkvn, BF16, T=T, S=S, tm=tm, cb=QKV_CB,
                                   ncb=3072 // QKV_CB, name="gdn_qkv_bwd")

    dz_ab = jnp.concatenate([dz_a, dz_b], axis=1)
    dz_gbr = jnp.concatenate([dgbr_a, dgbr_b], axis=1)
    dh1 = None
    for key, dz in (("in_qkv", dz_qkv), ("in_ga", dz_ga), ("in_ab", dz_ab), ("in_small", dz_small), ("in_gbr", dz_gbr)):
        dh1 = _mm(dz, W[key], "nt", F32, add=dh1, name=key + "_dx", tk=896 if key == "in_small" else 1024)
        dW[key] = _mm(h1, dz, "tn", F32, name=key + "_dw", tn=896 if key == "in_small" else 1024)
    dx, dg_mix = _norm_bwd(x2d, g_mix, dh1, dx1, T=T, tm=tm, name="norm_mix_bwd")

    dsmall = {"norm_mix_g": dg_mix, "gdn_a_log": d_alog[:, :HEADS], "gdn_dt_bias": d_dtb[:, :HEADS], "gdn_norm_g": dg_gdn,
              "mla_q_norm_g": dg_q, "mla_kv_norm_g": dg_kv, "norm_ffn_g": dg_ffn, "norm_final_g": dg_fin}
    return loss_v[0, 0], dx.reshape(B, S, D_MODEL), dW, d_conv_qkv, d_conv_ffn, dsmall


def kernel(x, norm_mix_g, w_in, conv_qkv_w, gdn_a_log, gdn_dt_bias, gdn_norm_g, mla_q_norm_g, w_uq, mla_kv_norm_g, w_ukv, w_o_gdn, w_o_mla, w_out, norm_ffn_g, w_up, conv_ffn_w, w_down, norm_final_g, loss_target, m_norm_mix_g, m_w_in, m_conv_qkv_w, m_gdn_a_log, m_gdn_dt_bias, m_gdn_norm_g, m_mla_q_norm_g, m_w_uq, m_mla_kv_norm_g, m_w_ukv, m_w_o_gdn, m_w_o_mla, m_w_out, m_norm_ffn_g, m_w_up, m_conv_ffn_w, m_w_down, m_norm_final_g, v_norm_mix_g, v_w_in, v_conv_qkv_w, v_gdn_a_log, v_gdn_dt_bias, v_gdn_norm_g, v_mla_q_norm_g, v_w_uq, v_mla_kv_norm_g, v_w_ukv, v_w_o_gdn, v_w_o_mla, v_w_out, v_norm_ffn_g, v_w_up, v_conv_ffn_w, v_w_down, v_norm_final_g):
    w = dict(norm_mix_g=norm_mix_g, w_in=w_in, conv_qkv_w=conv_qkv_w, gdn_a_log=gdn_a_log, gdn_dt_bias=gdn_dt_bias,
             gdn_norm_g=gdn_norm_g, mla_q_norm_g=mla_q_norm_g, w_uq=w_uq, mla_kv_norm_g=mla_kv_norm_g, w_ukv=w_ukv,
             w_o_gdn=w_o_gdn, w_o_mla=w_o_mla, w_out=w_out, norm_ffn_g=norm_ffn_g, w_up=w_up, conv_ffn_w=conv_ffn_w,
             w_down=w_down, norm_final_g=norm_final_g)
    m = dict(norm_mix_g=m_norm_mix_g, w_in=m_w_in, conv_qkv_w=m_conv_qkv_w, gdn_a_log=m_gdn_a_log, gdn_dt_bias=m_gdn_dt_bias,
             gdn_norm_g=m_gdn_norm_g, mla_q_norm_g=m_mla_q_norm_g, w_uq=m_w_uq, mla_kv_norm_g=m_mla_kv_norm_g, w_ukv=m_w_ukv,
             w_o_gdn=m_w_o_gdn, w_o_mla=m_w_o_mla, w_out=m_w_out, norm_ffn_g=m_norm_ffn_g, w_up=m_w_up,
             conv_ffn_w=m_conv_ffn_w, w_down=m_w_down, norm_final_g=m_norm_final_g)
    v = dict(norm_mix_g=v_norm_mix_g, w_in=v_w_in, conv_qkv_w=v_conv_qkv_w, gdn_a_log=v_gdn_a_log, gdn_dt_bias=v_gdn_dt_bias,
             gdn_norm_g=v_gdn_norm_g, mla_q_norm_g=v_mla_q_norm_g, w_uq=v_w_uq, mla_kv_norm_g=v_mla_kv_norm_g, w_ukv=v_w_ukv,
             w_o_gdn=v_w_o_gdn, w_o_mla=v_w_o_mla, w_out=v_w_out, norm_ffn_g=v_norm_ffn_g, w_up=v_w_up,
             conv_ffn_w=v_conv_ffn_w, w_down=v_w_down, norm_final_g=v_norm_final_g)
    slab_names = ("A", "Q", "C", "V")
    big_names = ("w_in", "w_up", "w_uq", "w_ukv", "w_o_gdn", "w_o_mla", "w_out", "w_down", "conv_qkv_w", "conv_ffn_w")
    small_names = [n for n, _ in SMALL]
    small_shapes = {n: w[n].shape for n in small_names}
    local2d = lambda d: {n: d[n][0] for n in big_names}

    w_slabs = _slabs(local2d(w), F32)
    send = [w_slabs[k].astype(BF16) if k != "V" else w_slabs[k] for k in slab_names]
    gathered = _allgather(send, name="allgather_weights")
    gathered = {k: g.reshape(N_DEV, -1, g.shape[1]) for k, g in zip(slab_names, gathered)}
    W, conv_qkv_full, conv_ffn_full = _layout_weights(gathered)

    loss_local, dx, dW, d_conv_qkv, d_conv_ffn, dsmall = _local_step(
        x, loss_target, W, conv_qkv_full, conv_ffn_full, {n: w[n] for n in small_names})

    g_send = _send_slabs(dW, d_conv_qkv, d_conv_ffn)
    recv = _alltoall([g_send[k] for k in slab_names], name="alltoall_grads")
    m_slabs, v_slabs = _slabs(local2d(m), F32), _slabs(local2d(v), F32)
    upd = {k: _reduce_adam(r, w_slabs[k], m_slabs[k], v_slabs[k], tr=SLAB_TR[k], name="adam_" + k)
           for k, r in zip(slab_names, recv)}
    small_parts = _allgather([_pack_small(dsmall)], name="allgather_small_grads")[0]
    upd_small = _reduce_adam(small_parts, _pack_small({n: w[n] for n in small_names}), _pack_small({n: m[n] for n in small_names}),
                             _pack_small({n: v[n] for n in small_names}), tr=SMALL_ROWS, name="adam_small")

    loss = lax.psum(loss_local, ("x", "y", "c"))
    groups = []
    for i in range(4):
        merged = {**_unslab({k: upd[k][i] for k in slab_names}), **_unpack_small(upd_small[i], small_shapes)}
        groups.append([merged[n] for n in WEIGHT_ORDER])
    return (loss, dx, *groups[0], *groups[1], *groups[2], *groups[3])
```

```python
import functools
import math

import numpy as np
import jax
import jax.numpy as jnp
from jax import lax
from jax.experimental import pallas as pl
from jax.experimental.pallas import tpu as pltpu

F32 = jnp.float32
BF16 = jnp.bfloat16

D_MODEL = 1024
HEADS = 8
HD = 128
GDN_CONV = 4
CHUNK = 64
Q_RANK = 384
KV_RANK = 256
ROPE = 64
ROPE_THETA = 10000.0
D_FF = 2816
FFN_CONV = 3
EPS = 1e-6
SM_SCALE = (HD + ROPE) ** -0.5
N_DEV = 8

ADAM_LR, ADAM_B1, ADAM_B2, ADAM_EPS, ADAM_WD, ADAM_STEP = 0.001, 0.9, 0.999, 1e-08, 0.01, 10

LANES = 128
SUBLANES = 8
HALO = 2 * SUBLANES
VMEM_LIMIT = 56 * 1024 * 1024
HI = lax.Precision.HIGHEST
TRI_PRECISION = None

NN = (((1,), (0,)), ((), ()))
NT = (((1,), (1,)), ((), ()))
TN = (((0,), (0,)), ((), ()))


def _dot(a, b, dims=NN, precision=None):
    return lax.dot_general(a, b, dims, precision=precision, preferred_element_type=F32)


def _pick(dim, target, align):
    best = None
    for t in range(align, min(dim, target) + 1, align):
        if dim % t == 0:
            best = t
    return dim if best is None else best


def _call(body, ins, outs, grid, *, name, scratch=(), semantics=None):
    n_in, n_out = len(ins), len(outs)

    def kern(*refs):
        body(refs[:n_in], refs[n_in:n_in + n_out], refs[n_in + n_out:])

    res = pl.pallas_call(
        kern,
        grid=grid,
        in_specs=[pl.BlockSpec(bs, im) for _, bs, im in ins],
        out_specs=[pl.BlockSpec(bs, im) for _, _, bs, im in outs],
        out_shape=[jax.ShapeDtypeStruct(s, d) for s, d, _, _ in outs],
        scratch_shapes=list(scratch),
        name=name,
        compiler_params=pltpu.CompilerParams(
            dimension_semantics=semantics or ("arbitrary",) * len(grid), vmem_limit_bytes=VMEM_LIMIT),
    )(*[a for a, _, _ in ins])
    return res


def _mm(a, b, mode, out_dtype, *, name, add=None, tm=1408, tn=1408, tk=1408):
    if mode == "nn":
        (M, K), (K2, N) = a.shape, b.shape
    elif mode == "nt":
        (M, K), (N, K2) = a.shape, b.shape
    else:
        (K, M), (K2, N) = a.shape, b.shape
    assert K == K2, (a.shape, b.shape, mode)
    tm = _pick(M, tm, LANES if mode == "tn" else 16)
    tn = _pick(N, tn, LANES)
    tk = _pick(K, tk, 16 if mode == "tn" else LANES)
    nk = K // tk
    dims = {"nn": NN, "nt": NT, "tn": TN}[mode]
    if mode == "nn":
        a_spec, b_spec = ((tm, tk), lambda i, j, k: (i, k)), ((tk, tn), lambda i, j, k: (k, j))
    elif mode == "nt":
        a_spec, b_spec = ((tm, tk), lambda i, j, k: (i, k)), ((tn, tk), lambda i, j, k: (j, k))
    else:
        a_spec, b_spec = ((tk, tm), lambda i, j, k: (k, i)), ((tk, tn), lambda i, j, k: (k, j))
    ins = [(a,) + a_spec, (b,) + b_spec]
    if add is not None:
        ins.append((add, (tm, tn), lambda i, j, k: (i, j)))
    outs = [((M, N), out_dtype, (tm, tn), lambda i, j, k: (i, j))]

    def body(in_refs, out_refs, scr):
        prod = _dot(in_refs[0][...].astype(BF16), in_refs[1][...].astype(BF16), dims)

        def finish(r):
            if add is not None:
                r = r + in_refs[2][...].astype(F32)
            out_refs[0][...] = r.astype(out_dtype)

        if nk == 1:
            finish(prod)
            return
        k = pl.program_id(2)
        acc = scr[0]

        @pl.when(k == 0)
        def _():
            acc[...] = prod

        @pl.when(k > 0)
        def _():
            acc[...] += prod

        @pl.when(k == nk - 1)
        def _():
            finish(acc[...])

    return _call(body, ins, outs, (M // tm, N // tn, nk), name=name,
                 scratch=[pltpu.VMEM((tm, tn), F32)] if nk > 1 else [],
                 semantics=("parallel", "parallel", "arbitrary"))[0]


def _row_call(fn, rows, consts, out_rows, out_accs=(), *, T, tm, name):
    nt = T // tm
    ins = []
    for r in rows:
        ins.append(r if isinstance(r, tuple) else (r, (tm, r.shape[1]), lambda i: (i, 0)))
    for c in consts:
        ins.append((c, c.shape, lambda i, nd=c.ndim: (0,) * nd))
    outs = []
    for o in out_rows:
        outs.append(((T, o[0]), o[1], (tm, o[0]), lambda i: (i, 0)) if len(o) == 2 else o)
    for shp, dt in out_accs:
        outs.append((shp, dt, shp, lambda i, nd=len(shp): (0,) * nd))
    n_r = len(out_rows)

    def body(in_refs, out_refs, _):
        i = pl.program_id(0)
        vals = fn(*[r[...] for r in in_refs])
        for o_ref, v in zip(out_refs[:n_r], vals[:n_r]):
            o_ref[...] = v.astype(o_ref.dtype)
        for o_ref, v in zip(out_refs[n_r:], vals[n_r:]):
            @pl.when(i == 0)
            def _(o_ref=o_ref):
                o_ref[...] = jnp.zeros_like(o_ref)

            o_ref[...] += v.astype(o_ref.dtype)

    return _call(body, ins, outs, (nt,), name=name)


def _rms(x, g):
    return x * lax.rsqrt(jnp.mean(x * x, axis=-1, keepdims=True) + EPS) * g


def _norm_fwd(x, g, *, T, tm, name):
    return _row_call(lambda xt, gt: (_rms(xt, gt),), [x], [g], [(x.shape[1], BF16)], T=T, tm=tm, name=name)[0]


def _norm_bwd(x, g, dh, dres, *, T, tm, name):
    def fn(xt, dht, drt, gt):
        _, vjp = jax.vjp(_rms, xt, gt)
        dx, dg = vjp(dht)
        return drt + dx, dg

    return _row_call(fn, [x, dh, dres], [g], [(x.shape[1], F32)], [(g.shape, F32)], T=T, tm=tm, name=name)


def _dwconv(tail, x, w):
    K, tm = w.shape[0], x.shape[0]
    xx = jnp.concatenate([tail, x], axis=0)
    acc = None
    for k in range(K):
        s = HALO - (K - 1) + k
        term = w[k:k + 1, :] * xx[s:s + tm, :]
        acc = term if acc is None else acc + term
    return acc


def _conv_fwd(fn, xs, ws, out_c, out_dtype, *, T, S, tm, cb, ncb, name):
    nt, tps, hb = T // tm, S // tm, tm // HALO
    ins = []
    for arr, off in xs:
        ins.append((arr, (tm, cb), lambda j, i, off=off: (i, off + j)))
        ins.append((arr, (HALO, cb), lambda j, i, off=off: (jnp.maximum(i * hb - 1, 0), off + j)))
    for arr, off in ws:
        ins.append((arr, (arr.shape[0], cb), lambda j, i, off=off: (0, off + j)))
    outs = [((T, out_c), out_dtype, (tm, cb), lambda j, i: (i, j))]
    nx = len(xs)

    def body(in_refs, out_refs, _):
        j, i = pl.program_id(0), pl.program_id(1)
        first = (i % tps) == 0
        xts = [in_refs[2 * m][...].astype(F32) for m in range(nx)]
        tails = [jnp.where(first, 0.0, in_refs[2 * m + 1][...].astype(F32)) for m in range(nx)]
        wts = [r[...] for r in in_refs[2 * nx:]]
        out_refs[0][...] = fn(j, tails, xts, wts).astype(out_dtype)

    return _call(body, ins, outs, (ncb, nt), name=name)[0]


def _conv_bwd(fn, xs, ws, dout, dx_dtype, *, T, S, tm, cb, ncb, name):
    nt, tps, hb = T // tm, S // tm, tm // HALO
    ins = []
    for arr, off in xs:
        ins.append((arr, (tm, cb), lambda j, i, off=off: (nt - 1 - i, off + j)))
        ins.append((arr, (HALO, cb), lambda j, i, off=off: (jnp.maximum((nt - 1 - i) * hb - 1, 0), off + j)))
    for arr, off in ws:
        ins.append((arr, (arr.shape[0], cb), lambda j, i, off=off: (0, off + j)))
    ins.append((dout, (tm, cb), lambda j, i: (nt - 1 - i, j)))
    nx, nw = len(xs), len(ws)
    outs = [((T, ncb * cb), dx_dtype, (tm, cb), lambda j, i: (nt - 1 - i, j)) for _ in xs]
    outs += [((arr.shape[0], ncb * cb), F32, (arr.shape[0], cb), lambda j, i: (0, j)) for arr, _ in ws]
    scratch = [pltpu.VMEM((HALO, cb), F32) for _ in xs]

    def body(in_refs, out_refs, carry):
        j, i = pl.program_id(0), pl.program_id(1)
        r = nt - 1 - i
        first = (r % tps) == 0
        xts = [in_refs[2 * m][...].astype(F32) for m in range(nx)]
        tails = [jnp.where(first, 0.0, in_refs[2 * m + 1][...].astype(F32)) for m in range(nx)]
        wts = [ref[...] for ref in in_refs[2 * nx:2 * nx + nw]]
        _, vjp = jax.vjp(lambda tl, xt, wt: fn(j, tl, xt, wt), tails, xts, wts)
        dtails, dxts, dwts = vjp(in_refs[-1][...].astype(F32))

        @pl.when(i == 0)
        def _():
            for c in carry:
                c[...] = jnp.zeros_like(c)

        for m in range(nx):
            pad = jnp.concatenate([jnp.zeros((tm - HALO, cb), F32), carry[m][...]], axis=0)
            out_refs[m][...] = (dxts[m] + pad).astype(dx_dtype)
            carry[m][...] = jnp.where(first, 0.0, dtails[m])
        for m in range(nw):
            o_ref = out_refs[nx + m]

            @pl.when(i == 0)
            def _(o_ref=o_ref):
                o_ref[...] = jnp.zeros_like(o_ref)

            o_ref[...] += dwts[m]

    return _call(body, ins, outs, (ncb, nt), name=name, scratch=scratch)


QKV_CB = 512


def _qkv_fn(j, tails, xts, wts):
    y = jax.nn.silu(_dwconv(tails[0], xts[0], wts[0]))
    scale = jnp.where(j < 2, HD ** -0.5, 1.0)
    parts = []
    for h in range(QKV_CB // HD):
        yh = y[:, h * HD:(h + 1) * HD]
        nh = yh * lax.rsqrt(jnp.sum(yh * yh, axis=-1, keepdims=True) + EPS)
        parts.append(jnp.where(j < 4, nh * scale, yh))
    return jnp.concatenate(parts, axis=1)


def _ffn_fn(j, tails, xts, wts):
    return jax.nn.silu(_dwconv(tails[0], xts[0], wts[0])) * _dwconv(tails[1], xts[1], wts[1])


BNN = (((2,), (1,)), ((0,), (0,)))
BNT = (((2,), (2,)), ((0,), (0,)))
BTN = (((1,), (1,)), ((0,), (0,)))


@jax.custom_vjp
def _tri_inv(L):
    C = L.shape[-1]
    ii = lax.broadcasted_iota(jnp.int32, (C, C), 0)
    jj = lax.broadcasted_iota(jnp.int32, (C, C), 1)
    eye = (ii == jj).astype(F32)
    X = eye - jnp.where((ii >> 1) == (jj >> 1), L, 0.0)
    s = 1
    while (2 << s) <= C:
        E = jnp.where(((ii >> (s + 1)) == (jj >> (s + 1))) & ((ii >> s) != (jj >> s)), L, 0.0)
        X = X - _dot(_dot(X, E, BNN, precision=TRI_PRECISION), X, BNN, precision=TRI_PRECISION)
        s += 1
    return X


def _tri_inv_fwd(L):
    X = _tri_inv(L)
    return X, X


def _tri_inv_bwd(X, dX):
    return (-_dot(_dot(X, dX, BTN, precision=TRI_PRECISION), X, BNT, precision=TRI_PRECISION),)


_tri_inv.defvjp(_tri_inv_fwd, _tri_inv_bwd)


def _gdn_chunk(q, k, v, gc, gr, beta, S):
    C = q.shape[1]
    ii = lax.broadcasted_iota(jnp.int32, (C, C), 0)
    jj = lax.broadcasted_iota(jnp.int32, (C, C), 1)
    lower = ii >= jj
    decay = jnp.where(lower, jnp.exp(jnp.where(lower, gc - gr, 0.0)), 0.0)
    kb, vb = k * beta, v * beta
    L = jnp.where(ii > jj, _dot(kb, k, BNT) * decay, 0.0)
    Tinv = _tri_inv(L)
    eg = jnp.exp(gc)
    u = _dot(Tinv, vb, BNN, precision=TRI_PRECISION)
    w = _dot(Tinv, kb * eg, BNN, precision=TRI_PRECISION)
    a = _dot(q, k, BNT) * decay
    g_last = gc[:, C - 1:C, :]
    kd = k * jnp.exp(g_last - gc)
    v_new = u - _dot(w, S, BNN)
    o = _dot(q * eg, S, BNN) + _dot(a, v_new, BNN)
    S_new = S * jnp.exp(g_last) + _dot(kd, v_new, BTN)
    return o, S_new


def _heads(ref, width=HD):
    return jnp.stack([ref[:, h * width:(h + 1) * width].astype(F32) for h in range(HEADS)])


def _gdn_fwd(qkvn, gcum, grT, beta, *, B, S):
    N, T = S // CHUNK, B * S
    row = lambda c: (lambda b, n: (b * N + n, c))
    ins = [(qkvn, (CHUNK, 1024), row(0)), (qkvn, (CHUNK, 1024), row(1)), (qkvn, (CHUNK, 1024), row(2)),
           (gcum, (CHUNK, LANES), row(0)), (grT, (1, HEADS, 1, CHUNK), lambda b, n: (b * N + n, 0, 0, 0)),
           (beta, (CHUNK, LANES), row(0))]
    outs = [((T, 1024), F32, (CHUNK, 1024), row(0)),
            ((B * N, HEADS, HD, HD), BF16, (1, HEADS, HD, HD), lambda b, n: (b * N + n, 0, 0, 0))]

    def body(in_refs, out_refs, scr):
        q_ref, k_ref, v_ref, gc_ref, gr_ref, b_ref = in_refs
        o_ref, st_ref = out_refs
        S_ref = scr[0]

        @pl.when(pl.program_id(1) == 0)
        def _():
            S_ref[...] = jnp.zeros_like(S_ref)

        S0 = S_ref[...]
        st_ref[0] = S0.astype(BF16)
        o, Sn = _gdn_chunk(_heads(q_ref), _heads(k_ref), _heads(v_ref), _heads(gc_ref, 1), gr_ref[0],
                           _heads(b_ref, 1), S0)
        for h in range(HEADS):
            o_ref[:, h * HD:(h + 1) * HD] = o[h]
        S_ref[...] = Sn

    return _call(body, ins, outs, (B, N), name="gdn_core_fwd", scratch=[pltpu.VMEM((HEADS, HD, HD), F32)])


def _gdn_bwd(qkvn, gcum, grT, beta, states, do, *, B, S):
    N, T = S // CHUNK, B * S
    row = lambda c: (lambda b, n: (b * N + N - 1 - n, c))
    ins = [(qkvn, (CHUNK, 1024), row(0)), (qkvn, (CHUNK, 1024), row(1)), (qkvn, (CHUNK, 1024), row(2)),
           (gcum, (CHUNK, LANES), row(0)), (grT, (1, HEADS, 1, CHUNK), lambda b, n: (b * N + N - 1 - n, 0, 0, 0)),
           (beta, (CHUNK, LANES), row(0)),
           (states, (1, HEADS, HD, HD), lambda b, n: (b * N + N - 1 - n, 0, 0, 0)), (do, (CHUNK, 1024), row(0))]
    outs = [((T, 3072), BF16, (CHUNK, 3072), row(0)), ((T, LANES), F32, (CHUNK, LANES), row(0)),
            ((B * N, HEADS, 1, CHUNK), F32, (1, HEADS, 1, CHUNK), lambda b, n: (b * N + N - 1 - n, 0, 0, 0)),
            ((T, LANES), F32, (CHUNK, LANES), row(0))]

    def body(in_refs, out_refs, scr):
        q_ref, k_ref, v_ref, gc_ref, gr_ref, b_ref, st_ref, do_ref = in_refs
        dqkv_ref, dgc_ref, dgr_ref, db_ref = out_refs
        dS_ref = scr[0]

        @pl.when(pl.program_id(1) == 0)
        def _():
            dS_ref[...] = jnp.zeros_like(dS_ref)

        args = (_heads(q_ref), _heads(k_ref), _heads(v_ref), _heads(gc_ref, 1), gr_ref[0], _heads(b_ref, 1),
                st_ref[0].astype(F32))
        _, vjp = jax.vjp(_gdn_chunk, *args)
        dq, dk, dv, dgc, dgr, db, dS = vjp((_heads(do_ref), dS_ref[...]))
        lane = lax.broadcasted_iota(jnp.int32, (CHUNK, LANES), 1)
        dgc_all = jnp.zeros((CHUNK, LANES), F32)
        db_all = jnp.zeros((CHUNK, LANES), F32)
        for h in range(HEADS):
            dqkv_ref[:, h * HD:(h + 1) * HD] = dq[h].astype(BF16)
            dqkv_ref[:, 1024 + h * HD:1024 + (h + 1) * HD] = dk[h].astype(BF16)
            dqkv_ref[:, 2048 + h * HD:2048 + (h + 1) * HD] = dv[h].astype(BF16)
            dgc_all = jnp.where(lane == h, dgc[h], dgc_all)
            db_all = jnp.where(lane == h, db[h], db_all)
        dgc_ref[...] = dgc_all
        db_ref[...] = db_all
        dgr_ref[0] = dgr
        dS_ref[...] = dS

    return _call(body, ins, outs, (B, N), name="gdn_core_bwd", scratch=[pltpu.VMEM((HEADS, HD, HD), F32)])


def _gate_fn(za, zb, alog, dtb):
    tm = za.shape[0]
    g = -jnp.exp(alog) * jax.nn.softplus(za + dtb)
    ii = lax.broadcasted_iota(jnp.int32, (tm, tm), 0)
    jj = lax.broadcasted_iota(jnp.int32, (tm, tm), 1)
    tri = ((ii >= jj) & ((ii >> 6) == (jj >> 6))).astype(F32)
    return _dot(tri, g, precision=HI), jax.nn.sigmoid(zb)


def _scores(qn_ref, qp_ref, kn_ref, kp_ref, diag):
    q = jnp.concatenate([qn_ref[...], qp_ref[...]], axis=1)
    k = jnp.concatenate([kn_ref[...], kp_ref[...]], axis=1)
    s = _dot(q, k, NT) * SM_SCALE
    if diag:
        t = s.shape[0]
        ii = lax.broadcasted_iota(jnp.int32, (t, t), 0)
        jj = lax.broadcasted_iota(jnp.int32, (t, t), 1)
        s = jnp.where(ii >= jj, s, -jnp.inf)
    return s, q, k


HPB = 4
HW = HPB * HD


def _head_refs(refs, hh):
    return [r.at[:, hh * HD:(hh + 1) * HD] for r in refs]


def _flash_fwd(qn, qp, kn, kp, v, *, B, S, t):
    nb, T = S // t, B * S
    qmap = lambda b, h, qi, ki: (b * nb + qi, h)
    kmap = lambda b, h, qi, ki: (b * nb + jnp.minimum(ki, qi), h)
    kpmap = lambda b, h, qi, ki: (b * nb + jnp.minimum(ki, qi), 0)
    ins = [(qn, (t, HW), qmap), (qp, (t, HW), qmap), (kn, (t, HW), kmap), (kp, (t, HD), kpmap), (v, (t, HW), kmap)]
    outs = [((T, 1024), BF16, (t, HW), qmap),
            ((HEADS, T, 1), F32, (HPB, t, 1), lambda b, h, qi, ki: (h, b * nb + qi, 0))]
    scratch = [pltpu.VMEM((HPB, t, 1), F32), pltpu.VMEM((HPB, t, 2 * HD), F32)]

    def body(in_refs, out_refs, scr):
        qn_ref, qp_ref, kn_ref, kp_ref, v_ref = in_refs
        o_ref, lse_ref = out_refs
        m_ref, acc_ref = scr
        qi, ki = pl.program_id(2), pl.program_id(3)

        @pl.when(ki == 0)
        def _():
            m_ref[...] = jnp.full_like(m_ref, -jnp.inf)
            acc_ref[...] = jnp.zeros_like(acc_ref)

        def step(diag):
            for hh in range(HPB):
                qn_h, qp_h, kn_h, v_h = _head_refs((qn_ref, qp_ref, kn_ref, v_ref), hh)
                s, _, _ = _scores(qn_h, qp_h, kn_h, kp_ref, diag)
                m_old = m_ref[hh]
                m_new = jnp.maximum(m_old, jnp.max(s, axis=-1, keepdims=True))
                p = jnp.exp(s - m_new)
                alpha = jnp.exp(m_old - m_new)
                v1 = jnp.concatenate([v_h[...], jnp.ones((t, HD), BF16)], axis=1)
                acc_ref[hh] = alpha * acc_ref[hh] + _dot(p.astype(BF16), v1)
                m_ref[hh] = m_new

        @pl.when(ki < qi)
        def _():
            step(False)

        @pl.when(ki == qi)
        def _():
            step(True)
            for hh in range(HPB):
                o_ref[:, hh * HD:(hh + 1) * HD] = (acc_ref[hh, :, :HD] / acc_ref[hh, :, HD:]).astype(BF16)
                lse_ref[hh] = m_ref[hh] + jnp.log(acc_ref[hh, :, HD:HD + 1])

    return _call(body, ins, outs, (B, HEADS // HPB, nb, nb), name="mla_flash_fwd", scratch=scratch,
                 semantics=("parallel", "parallel", "parallel", "arbitrary"))


def _flash_bwd_dq(qn, qp, kn, kp, v, o, do, lse, *, B, S, t):
    nb, T = S // t, B * S
    qmap = lambda b, h, qi, ki: (b * nb + qi, h)
    kmap = lambda b, h, qi, ki: (b * nb + jnp.minimum(ki, qi), h)
    kpmap = lambda b, h, qi, ki: (b * nb + jnp.minimum(ki, qi), 0)
    ins = [(qn, (t, HW), qmap), (qp, (t, HW), qmap), (kn, (t, HW), kmap), (kp, (t, HD), kpmap), (v, (t, HW), kmap),
           (o, (t, HW), qmap), (do, (t, HW), qmap), (lse, (HPB, t, 1), lambda b, h, qi, ki: (h, b * nb + qi, 0))]
    outs = [((T, 1024), BF16, (t, HW), qmap), ((T, 1024), F32, (t, HW), qmap),
            ((HEADS, T, 1), F32, (HPB, t, 1), lambda b, h, qi, ki: (h, b * nb + qi, 0))]
    scratch = [pltpu.VMEM((HPB, t, 1), F32), pltpu.VMEM((HPB, t, 2 * HD), F32)]

    def body(in_refs, out_refs, scr):
        qn_ref, qp_ref, kn_ref, kp_ref, v_ref, o_ref, do_ref, lse_ref = in_refs
        dqn_ref, dqp_ref, dlo_ref = out_refs
        dl_ref, acc_ref = scr
        qi, ki = pl.program_id(2), pl.program_id(3)

        @pl.when(ki == 0)
        def _():
            for hh in range(HPB):
                o_h, do_h = _head_refs((o_ref, do_ref), hh)
                dl_ref[hh] = jnp.sum(do_h[...].astype(F32) * o_h[...].astype(F32), axis=-1, keepdims=True)
            acc_ref[...] = jnp.zeros_like(acc_ref)

        def step(diag):
            for hh in range(HPB):
                qn_h, qp_h, kn_h, v_h, do_h = _head_refs((qn_ref, qp_ref, kn_ref, v_ref, do_ref), hh)
                s, _, k = _scores(qn_h, qp_h, kn_h, kp_ref, diag)
                p = jnp.exp(s - lse_ref[hh])
                dp = _dot(do_h[...], v_h[...], NT)
                ds = p * (dp - dl_ref[hh]) * SM_SCALE
                acc_ref[hh] += _dot(ds.astype(BF16), k)

        @pl.when(ki < qi)
        def _():
            step(False)

        @pl.when(ki == qi)
        def _():
            step(True)
            for hh in range(HPB):
                dqn_ref[:, hh * HD:(hh + 1) * HD] = acc_ref[hh, :, :HD].astype(BF16)
                dqp_ref[:, hh * HD:(hh + 1) * HD] = acc_ref[hh, :, HD:]
            dlo_ref[...] = dl_ref[...]

    return _call(body, ins, outs, (B, HEADS // HPB, nb, nb), name="mla_flash_bwd_dq", scratch=scratch,
                 semantics=("parallel", "parallel", "parallel", "arbitrary"))


def _flash_bwd_dkv(qn, qp, kn, kp, v, do, lse_t, dl_t, *, B, S, t):
    nb, T = S // t, B * S
    qmap = lambda b, h, ki, qi: (b * nb + jnp.maximum(qi, ki), h)
    kmap = lambda b, h, ki, qi: (b * nb + ki, h)
    tmap = lambda b, h, ki, qi: (h, 0, b * nb + jnp.maximum(qi, ki))
    ins = [(qn, (t, HW), qmap), (qp, (t, HW), qmap), (kn, (t, HW), kmap),
           (kp, (t, HD), lambda b, h, ki, qi: (b * nb + ki, 0)), (v, (t, HW), kmap), (do, (t, HW), qmap),
           (lse_t, (HPB, 1, t), tmap), (dl_t, (HPB, 1, t), tmap)]
    outs = [((T, 1024), BF16, (t, HW), kmap), ((HEADS, T, HD), F32, (HPB, t, HD), lambda b, h, ki, qi: (h, b * nb + ki, 0)),
            ((T, 1024), BF16, (t, HW), kmap)]
    scratch = [pltpu.VMEM((HPB, t, 2 * HD), F32), pltpu.VMEM((HPB, t, HD), F32)]

    def body(in_refs, out_refs, scr):
        qn_ref, qp_ref, kn_ref, kp_ref, v_ref, do_ref, lse_ref, dl_ref = in_refs
        dkn_ref, dkp_ref, dv_ref = out_refs
        dk_acc, dv_acc = scr
        ki, qi = pl.program_id(2), pl.program_id(3)

        @pl.when(qi == 0)
        def _():
            dk_acc[...] = jnp.zeros_like(dk_acc)
            dv_acc[...] = jnp.zeros_like(dv_acc)

        def step(diag):
            for hh in range(HPB):
                qn_h, qp_h, kn_h, v_h, do_h = _head_refs((qn_ref, qp_ref, kn_ref, v_ref, do_ref), hh)
                q = jnp.concatenate([qn_h[...], qp_h[...]], axis=1)
                k = jnp.concatenate([kn_h[...], kp_ref[...]], axis=1)
                st = _dot(k, q, NT) * SM_SCALE
                if diag:
                    ii = lax.broadcasted_iota(jnp.int32, (t, t), 0)
                    jj = lax.broadcasted_iota(jnp.int32, (t, t), 1)
                    st = jnp.where(ii <= jj, st, -jnp.inf)
                do_t = do_h[...]
                pt = jnp.exp(st - lse_ref[hh])
                dst = pt * (_dot(v_h[...], do_t, NT) - dl_ref[hh]) * SM_SCALE
                dv_acc[hh] += _dot(pt.astype(BF16), do_t)
                dk_acc[hh] += _dot(dst.astype(BF16), q)

        @pl.when(qi > ki)
        def _():
            step(False)

        @pl.when(qi == ki)
        def _():
            step(True)

        @pl.when(qi == nb - 1)
        def _():
            for hh in range(HPB):
                dkn_ref[:, hh * HD:(hh + 1) * HD] = dk_acc[hh, :, :HD].astype(BF16)
                dkp_ref[hh] = dk_acc[hh, :, HD:]
                dv_ref[:, hh * HD:(hh + 1) * HD] = dv_acc[hh].astype(BF16)

    return _call(body, ins, outs, (B, HEADS // HPB, nb, nb), name="mla_flash_bwd_dkv", scratch=scratch,
                 semantics=("parallel", "parallel", "parallel", "arbitrary"))


def _allgather(shards, *, name):
    n_arr = len(shards)

    def body(*refs):
        x_refs, out_refs = refs[:n_arr], refs[n_arr:2 * n_arr]
        send_sems, recv_sems, local_sems = refs[2 * n_arr:]
        x, y, c = lax.axis_index("x"), lax.axis_index("y"), lax.axis_index("c")
        me, sibling = (x, y, c), (x, y, 1 - c)
        chips = [(1 - x, y), (x, 1 - y), (1 - x, 1 - y)]

        def rows(a, px, py, pc):
            m_per = shards[a].shape[0]
            return out_refs[a].at[pl.ds((4 * px + 2 * py + pc) * m_per, m_per), :]

        def copy(a, k, block, to, src=None):
            return pltpu.make_async_remote_copy(
                src_ref=rows(a, *block) if src is None else src, dst_ref=rows(a, *block),
                send_sem=send_sems.at[a, k], recv_sem=recv_sems.at[a, k], device_id=to,
                device_id_type=pl.DeviceIdType.MESH)

        mine = [pltpu.make_async_copy(x_refs[a], rows(a, *me), local_sems.at[a]) for a in range(n_arr)]
        for cp in mine:
            cp.start()
        first = []
        for a in range(n_arr):
            first.append(copy(a, 0, me, sibling, src=x_refs[a]))
            first += [copy(a, 1 + j, me, (*chip, c), src=x_refs[a]) for j, chip in enumerate(chips)]
        for cp in first:
            cp.start()
        passed = []
        for j, chip in enumerate(chips):
            for a in range(n_arr):
                copy(a, 1 + j, (*chip, c), me).wait_recv()
                cp = copy(a, 4 + j, (*chip, c), sibling)
                cp.start()
                passed.append(cp)
        for a in range(n_arr):
            copy(a, 0, sibling, me).wait_recv()
        for j, chip in enumerate(chips):
            for a in range(n_arr):
                copy(a, 4 + j, (*chip, 1 - c), me).wait_recv()
        for cp in first + passed:
            cp.wait_send()
        for cp in mine:
            cp.wait()

    return pl.pallas_call(
        body,
        out_shape=[jax.ShapeDtypeStruct((N_DEV * s.shape[0], s.shape[1]), s.dtype) for s in shards],
        in_specs=[pl.BlockSpec(memory_space=pl.ANY)] * n_arr,
        out_specs=[pl.BlockSpec(memory_space=pl.ANY)] * n_arr,
        scratch_shapes=[pltpu.SemaphoreType.DMA((n_arr, 7)), pltpu.SemaphoreType.DMA((n_arr, 7)),
                        pltpu.SemaphoreType.DMA((n_arr,))],
        name=name,
    )(*shards)


def _alltoall(sends, *, name):
    n_arr = len(sends)

    def body(*refs):
        s_refs, r_refs = refs[:n_arr], refs[n_arr:2 * n_arr]
        send_sems, recv_sems, local_sems = refs[2 * n_arr:]
        x, y, c = lax.axis_index("x"), lax.axis_index("y"), lax.axis_index("c")
        me = 4 * x + 2 * y + c

        def rows(ref, a, idx):
            m_per = sends[a].shape[0] // N_DEV
            return ref.at[pl.ds(idx * m_per, m_per), :]

        local = [pltpu.make_async_copy(rows(s_refs[a], a, me), rows(r_refs[a], a, me), local_sems.at[a])
                 for a in range(n_arr)]
        for cp in local:
            cp.start()
        copies = []
        for k in range(1, N_DEV):
            px = 1 - x if k & 4 else x
            py = 1 - y if k & 2 else y
            pc = 1 - c if k & 1 else c
            for a in range(n_arr):
                cp = pltpu.make_async_remote_copy(
                    src_ref=rows(s_refs[a], a, 4 * px + 2 * py + pc), dst_ref=rows(r_refs[a], a, me),
                    send_sem=send_sems.at[a, k - 1], recv_sem=recv_sems.at[a, k - 1],
                    device_id=(px, py, pc), device_id_type=pl.DeviceIdType.MESH)
                cp.start()
                copies.append(cp)
        for cp in copies:
            cp.wait()
        for cp in local:
            cp.wait()

    return pl.pallas_call(
        body,
        out_shape=[jax.ShapeDtypeStruct(s.shape, s.dtype) for s in sends],
        in_specs=[pl.BlockSpec(memory_space=pl.ANY)] * n_arr,
        out_specs=[pl.BlockSpec(memory_space=pl.ANY)] * n_arr,
        scratch_shapes=[pltpu.SemaphoreType.DMA((n_arr, 7)), pltpu.SemaphoreType.DMA((n_arr, 7)),
                        pltpu.SemaphoreType.DMA((n_arr,))],
        name=name,
    )(*sends)


def _reduce_adam(parts, w, m, v, *, tr, name):
    R, C = w.shape
    nR = R // tr
    ins = [(parts, (tr, C), lambda i, s=s: (s * nR + i, 0)) for s in range(N_DEV)]
    ins += [(a, (tr, C), lambda i: (i, 0)) for a in (w, m, v)]
    outs = [((R, C), F32, (tr, C), lambda i: (i, 0)) for _ in range(4)]
    c1 = 1.0 - ADAM_B1 ** ADAM_STEP
    c2 = 1.0 - ADAM_B2 ** ADAM_STEP

    def body(in_refs, out_refs, _):
        g = in_refs[0][...].astype(F32)
        for s in range(1, N_DEV):
            g = g + in_refs[s][...].astype(F32)
        wv, mv, vv = in_refs[8][...], in_refs[9][...], in_refs[10][...]
        mn = ADAM_B1 * mv + (1.0 - ADAM_B1) * g
        vn = ADAM_B2 * vv + (1.0 - ADAM_B2) * (g * g)
        delta = -ADAM_LR * ((mn / c1) / (jnp.sqrt(vn / c2) + ADAM_EPS) + ADAM_WD * wv)
        out_refs[0][...] = g
        out_refs[1][...] = delta
        out_refs[2][...] = mn
        out_refs[3][...] = vn

    return _call(body, ins, outs, (nR,), name=name, semantics=("parallel",))


IN_C, UP_C, UQ_C, QKV_C = 858, 704, 192, 384
A_W, Q_W, V_W = 896, 256, 768
SLAB_TR = {"A": 256, "Q": 128, "C": 368, "V": 16}
SMALL = [("norm_mix_g", 1024), ("gdn_a_log", 8), ("gdn_dt_bias", 8), ("gdn_norm_g", 128), ("mla_q_norm_g", 384),
         ("mla_kv_norm_g", 256), ("norm_ffn_g", 1024), ("norm_final_g", 1024)]
SMALL_ROWS = 32
WEIGHT_ORDER = ["norm_mix_g", "w_in", "conv_qkv_w", "gdn_a_log", "gdn_dt_bias", "gdn_norm_g", "mla_q_norm_g", "w_uq",
                "mla_kv_norm_g", "w_ukv", "w_o_gdn", "w_o_mla", "w_out", "norm_ffn_g", "w_up", "conv_ffn_w", "w_down",
                "norm_final_g"]


def _padc(w, n):
    return jnp.pad(w, ((0, 0), (0, n - w.shape[1])))


def _padrc(w, r, n):
    return jnp.pad(w, ((0, r - w.shape[0]), (0, n - w.shape[1])))


def _slabs(p, dtype):
    A = jnp.concatenate([_padc(p["w_in"], A_W), _padc(p["w_up"], A_W)], axis=0).astype(dtype)
    Q = jnp.concatenate([_padc(p["w_uq"], Q_W), p["w_ukv"]], axis=0).astype(dtype)
    C = jnp.concatenate([p["w_o_gdn"], p["w_o_mla"], p["w_out"], p["w_down"]], axis=0).astype(dtype)
    V = jnp.concatenate([_padrc(p["conv_qkv_w"], 8, V_W), _padrc(p["conv_ffn_w"], 8, V_W)], axis=0).astype(F32)
    return {"A": A, "Q": Q, "C": C, "V": V}


def _unslab(sl):
    A, Q, C, V = sl["A"], sl["Q"], sl["C"], sl["V"]
    out = {"w_in": A[:1024, :IN_C], "w_up": A[1024:, :UP_C], "w_uq": Q[:384, :UQ_C], "w_ukv": Q[384:],
           "w_o_gdn": C[0:128], "w_o_mla": C[128:256], "w_out": C[256:384], "w_down": C[384:],
           "conv_qkv_w": V[0:GDN_CONV, :QKV_C], "conv_ffn_w": V[8:8 + FFN_CONV, :UP_C]}
    return {k: a[None] for k, a in out.items()}


def _take_cols(pieces, lo, hi):
    out, off = [], 0
    for arr, a, b in pieces:
        s, e = max(lo, off), min(hi, off + b - a)
        if s < e:
            out.append(arr[:, a + s - off:a + e - off])
        off += b - a
    return out[0] if len(out) == 1 else jnp.concatenate(out, axis=1)


def _pack_small(d):
    flat = jnp.concatenate([d[n].reshape(-1).astype(F32) for n, _ in SMALL])
    return jnp.pad(flat, (0, SMALL_ROWS * LANES - flat.shape[0])).reshape(SMALL_ROWS, LANES)


def _unpack_small(buf, shapes):
    flat, out, off = buf.reshape(-1), {}, 0
    for name, n in SMALL:
        out[name] = flat[off:off + n].reshape(shapes[name])
        off += n
    return out


def _rot_cols(w):
    h = ROPE // 2
    return jnp.concatenate([-w[:, h:], w[:, :h]], axis=1)


def _unrot_cols(dw):
    h = ROPE // 2
    return jnp.concatenate([dw[:, h:], -dw[:, :h]], axis=1)


IN_SPLITS = [0, 3072, 4096, 4104, 4112, 4496, 4752, 4816, 5840, 6864]


def _layout_weights(g):
    A, Q, C, V = g["A"], g["Q"], g["C"], g["V"]
    in_pieces = [(A[j, :1024], 0, IN_C) for j in range(N_DEV)]
    o = IN_SPLITS
    take = lambda lo, hi: _take_cols(in_pieces, lo, hi)
    kpe = take(o[6], o[7])
    W = {
        "in_qkv": take(o[0], o[1]),
        "in_ga": take(o[1], o[2]),
        "in_ab": jnp.concatenate([_padc(take(o[2], o[3]), LANES), _padc(take(o[3], o[4]), LANES)], axis=1),
        "in_small": jnp.concatenate([take(o[4], o[6]), _padc(kpe, LANES), _padc(_rot_cols(kpe), LANES)], axis=1),
        "in_gbr": take(o[7], o[9]),
        "w_up": jnp.concatenate([A[j, 1024:, :UP_C] for j in range(N_DEV)], axis=1),
        "uq_n": jnp.concatenate([Q[j, :384, :HD] for j in range(N_DEV)], axis=1),
        "ukv_k": jnp.concatenate([Q[j, 384:, :HD] for j in range(N_DEV)], axis=1),
        "ukv_v": jnp.concatenate([Q[j, 384:, HD:] for j in range(N_DEV)], axis=1),
        "w_o_gdn": C[:, 0:128].reshape(1024, D_MODEL),
        "w_o_mla": C[:, 128:256].reshape(1024, D_MODEL),
        "w_out": C[:, 256:384].reshape(1024, D_MODEL),
        "w_down": C[:, 384:].reshape(D_FF, D_MODEL),
    }
    pe = [Q[j, :384, HD:HD + ROPE] for j in range(N_DEV)]
    W["uq_p"] = jnp.concatenate([_padc(p, HD) for p in pe] + [_padc(_rot_cols(p), HD) for p in pe], axis=1)
    conv_qkv = jnp.concatenate([V[j, 0:GDN_CONV, :QKV_C] for j in range(N_DEV)], axis=1)
    conv_ffn = jnp.concatenate([V[j, 8:8 + FFN_CONV, :UP_C] for j in range(N_DEV)], axis=1)
    return {k: v.astype(BF16) for k, v in W.items()}, conv_qkv, conv_ffn


def _full_grads(dW):
    s = dW["in_small"]
    dkpe = s[:, 640:704] + _unrot_cols(s[:, 768:832])
    in_pieces = [(dW["in_qkv"], 0, 3072), (dW["in_ga"], 0, 1024), (dW["in_ab"], 0, 8), (dW["in_ab"], 128, 136),
                 (s, 0, 640), (dkpe, 0, ROPE), (dW["in_gbr"], 0, 2048)]
    pe = []
    for j in range(N_DEV):
        lin = dW["uq_p"][:, j * HD:j * HD + ROPE]
        rot = dW["uq_p"][:, 1024 + j * HD:1024 + j * HD + ROPE]
        pe.append(lin + _unrot_cols(rot))
    return in_pieces, pe


def _send_slabs(dW, d_conv_qkv, d_conv_ffn):
    in_pieces, pe = _full_grads(dW)
    A, Q, V = [], [], []
    for j in range(N_DEV):
        gin = _padc(_take_cols(in_pieces, j * IN_C, (j + 1) * IN_C), A_W)
        gup = _padc(dW["w_up"][:, j * UP_C:(j + 1) * UP_C], A_W)
        A.append(jnp.concatenate([gin, gup], axis=0))
        guq = _padc(jnp.concatenate([dW["uq_n"][:, j * HD:(j + 1) * HD], pe[j]], axis=1), Q_W)
        gukv = jnp.concatenate([dW["ukv_k"][:, j * HD:(j + 1) * HD], dW["ukv_v"][:, j * HD:(j + 1) * HD]], axis=1)
        Q.append(jnp.concatenate([guq, gukv], axis=0))
        V.append(jnp.concatenate([_padrc(d_conv_qkv[:, j * QKV_C:(j + 1) * QKV_C], 8, V_W),
                                  _padrc(d_conv_ffn[:, j * UP_C:(j + 1) * UP_C], 8, V_W)], axis=0))
    C = jnp.concatenate([dW["w_o_gdn"].reshape(N_DEV, 128, D_MODEL), dW["w_o_mla"].reshape(N_DEV, 128, D_MODEL),
                         dW["w_out"].reshape(N_DEV, 128, D_MODEL), dW["w_down"].reshape(N_DEV, 352, D_MODEL)], axis=1)
    return {"A": jnp.concatenate(A, axis=0).astype(BF16), "Q": jnp.concatenate(Q, axis=0).astype(BF16),
            "C": C.reshape(N_DEV * 736, D_MODEL).astype(BF16), "V": jnp.concatenate(V, axis=0)}


def _rope_tables(S):
    half = ROPE // 2
    inv = ROPE_THETA ** (-jnp.arange(half, dtype=F32) / half)
    ang = jnp.arange(S, dtype=F32)[:, None] * inv[None, :]
    cos = jnp.concatenate([jnp.cos(ang), jnp.cos(ang)], axis=1)
    sin = jnp.concatenate([jnp.sin(ang), jnp.sin(ang)], axis=1)
    return _padc(cos, HD), _padc(sin, HD)


def _local_step(x, tgt, W, conv_qkv_w, conv_ffn_w, small, tm=None, ta=None):
    B, S, _ = x.shape
    T = B * S
    tm = tm or _pick(S, 256, CHUNK)
    ta = ta or _pick(S, 512, LANES)
    x2d, tgt2d = x.reshape(T, D_MODEL), tgt.reshape(T, D_MODEL)
    row = lambda v: v.reshape(1, -1).astype(F32)
    pad_row = lambda v: _padc(row(v), LANES)
    g_mix, g_ffn, g_fin = row(small["norm_mix_g"]), row(small["norm_ffn_g"]), row(small["norm_final_g"])
    g_gdn, g_q, g_kv = row(small["gdn_norm_g"]), row(small["mla_q_norm_g"]), row(small["mla_kv_norm_g"])
    alog, dtb = pad_row(small["gdn_a_log"]), pad_row(small["gdn_dt_bias"])
    cos, sin = _rope_tables(S)
    tps = S // tm
    tab = lambda a: (a, (tm, HD), lambda i: (i % tps, 0))
    col = lambda a, c, w: (a, (tm, w), lambda i, c=c: (i, c))

    h1 = _norm_fwd(x2d, g_mix, T=T, tm=tm, name="norm_mix_fwd")
    z_qkv = _mm(h1, W["in_qkv"], "nn", BF16, name="in_qkv_fwd")
    z_ga = _mm(h1, W["in_ga"], "nn", BF16, name="in_ga_fwd")
    z_ab = _mm(h1, W["in_ab"], "nn", F32, name="in_ab_fwd")
    z_small = _mm(h1, W["in_small"], "nn", F32, name="in_small_fwd", tn=896)
    z_gbr = _mm(h1, W["in_gbr"], "nn", BF16, name="in_gbr_fwd")

    qkvn = _conv_fwd(_qkv_fn, [(z_qkv, 0)], [(conv_qkv_w, 0)], 3072, BF16, T=T, S=S, tm=tm, cb=QKV_CB,
                     ncb=3072 // QKV_CB, name="gdn_qkv_fwd")
    gcum, beta = _row_call(lambda za, zb, al, db: _gate_fn(za, zb, al, db), [col(z_ab, 0, LANES), col(z_ab, 1, LANES)],
                           [alog, dtb], [(LANES, F32), (LANES, F32)], T=T, tm=tm, name="gdn_gate_fwd")
    grT = gcum[:, :HEADS].reshape(T // CHUNK, CHUNK, HEADS).transpose(0, 2, 1)[:, :, None, :]
    o_gdn, states = _gdn_fwd(qkvn, gcum, grT, beta, B=B, S=S)

    def gdn_out_fn(o, ga, g):
        parts = []
        for h in range(HEADS):
            sl = slice(h * HD, (h + 1) * HD)
            parts.append(_rms(o[:, sl], g) * jax.nn.silu(ga[:, sl].astype(F32)))
        return jnp.concatenate(parts, axis=1)

    oa = _row_call(lambda o, ga, g: (gdn_out_fn(o, ga, g),), [o_gdn, z_ga], [g_gdn], [(1024, BF16)], T=T, tm=tm,
                   name="gdn_out_fwd")[0]

    def mla_prep_fn(zq, zkv, zpl, zpr, c, s, gq, gkv):
        return _rms(zq, gq), _rms(zkv, gkv), zpl * c + zpr * s

    small_cols = [(z_small, (tm, Q_RANK), lambda i: (i, 0)), (z_small, (tm, LANES), lambda i: (i, 3)),
                  (z_small, (tm, LANES), lambda i: (i, 4)), (z_small, (tm, LANES), lambda i: (i, 5)),
                  (z_small, (tm, LANES), lambda i: (i, 6))]

    def mla_prep_fwd(zq, zkv0, zkv1, zpl, zpr, c, s, gq, gkv):
        return mla_prep_fn(zq, jnp.concatenate([zkv0, zkv1], axis=1), zpl, zpr, c, s, gq, gkv)

    cq, ckv, kpe = _row_call(mla_prep_fwd, small_cols + [tab(cos), tab(sin)], [g_q, g_kv],
                             [(Q_RANK, BF16), (KV_RANK, BF16), (HD, BF16)], T=T, tm=tm, name="mla_prep_fwd")
    qn = _mm(cq, W["uq_n"], "nn", BF16, name="uq_n_fwd")
    qpl = _mm(cq, W["uq_p"], "nn", F32, name="uq_p_fwd")
    kn = _mm(ckv, W["ukv_k"], "nn", BF16, name="ukv_k_fwd")
    vb = _mm(ckv, W["ukv_v"], "nn", BF16, name="ukv_v_fwd")

    def qrope_fn(lin, rot, c, s):
        return lin * jnp.tile(c, (1, HEADS)) + rot * jnp.tile(s, (1, HEADS))

    qp = _row_call(lambda lin, rot, c, s: (qrope_fn(lin, rot, c, s),), [col(qpl, 0, 1024), col(qpl, 1, 1024), tab(cos), tab(sin)],
                   [], [(1024, BF16)], T=T, tm=tm, name="q_rope_fwd")[0]
    ob, lse = _flash_fwd(qn, qp, kn, kpe, vb, B=B, S=S, t=ta)

    def merge_fn(ya, yb, ga, gb):
        return jax.nn.sigmoid(ga.astype(F32)) * ya + jax.nn.sigmoid(gb.astype(F32)) * yb

    def merge_fwd(oat, obt, ga, gb, wog, wom):
        ya, yb = _dot(oat, wog), _dot(obt, wom)
        return ya, yb, merge_fn(ya, yb, ga, gb)

    ya, yb, merged = _row_call(merge_fwd, [oa, ob, col(z_gbr, 0, 1024), col(z_gbr, 1, 1024)], [W["w_o_gdn"], W["w_o_mla"]],
                               [(1024, BF16), (1024, BF16), (1024, BF16)], T=T, tm=tm, name="merge_fwd")
    x1 = _mm(merged, W["w_out"], "nn", F32, add=x2d, name="w_out_fwd")

    h2 = _norm_fwd(x1, g_ffn, T=T, tm=tm, name="norm_ffn_fwd")
    up = _mm(h2, W["w_up"], "nn", BF16, name="w_up_fwd")
    FCB = 256
    nfb = D_FF // FCB
    f = _conv_fwd(_ffn_fn, [(up, 0), (up, 2)], [(conv_ffn_w, 0), (conv_ffn_w, 2)], D_FF, BF16, T=T, S=S, tm=tm,
                  cb=D_FF // 2, ncb=2, name="ffn_act_fwd")
    x2 = _mm(f, W["w_down"], "nn", F32, add=x1, name="w_down_fwd", tk=1408)

    def final_fn(xt, tt, g):
        def lossf(xv, gv):
            e = _rms(xv, gv) - tt
            return 0.5 * jnp.sum(jnp.mean(e * e, axis=-1))

        l, vjp = jax.vjp(lossf, xt, g)
        dx, dg = vjp(jnp.ones((), F32))
        return dx, jnp.full((1, LANES), l, F32), dg

    dx2, loss_v, dg_fin = _row_call(final_fn, [x2, tgt2d], [g_fin], [(1024, F32)], [((1, LANES), F32), ((1, 1024), F32)],
                                    T=T, tm=tm, name="loss_head")

    dW = {}
    df = _mm(dx2, W["w_down"], "nt", BF16, name="w_down_dx")
    dW["w_down"] = _mm(f, dx2, "tn", F32, name="w_down_dw")
    dug, duu, dcw_g, dcw_u = _conv_bwd(_ffn_fn, [(up, 0), (up, nfb)], [(conv_ffn_w, 0), (conv_ffn_w, nfb)], df, BF16,
                                       T=T, S=S, tm=tm, cb=FCB, ncb=nfb, name="ffn_act_bwd")
    d_conv_ffn = jnp.concatenate([dcw_g, dcw_u], axis=1)
    wup_g, wup_u = W["w_up"][:, :D_FF], W["w_up"][:, D_FF:]
    dh2 = _mm(dug, wup_g, "nt", F32, name="w_up_dx_g")
    dh2 = _mm(duu, wup_u, "nt", F32, add=dh2, name="w_up_dx_u")
    dW["w_up"] = jnp.concatenate([_mm(h2, dug, "tn", F32, name="w_up_dw_g"), _mm(h2, duu, "tn", F32, name="w_up_dw_u")], axis=1)
    dx1, dg_ffn = _norm_bwd(x1, g_ffn, dh2, dx2, T=T, tm=tm, name="norm_ffn_bwd")

    dmerged = _mm(dx1, W["w_out"], "nt", F32, name="w_out_dx")
    dW["w_out"] = _mm(merged, dx1, "tn", F32, name="w_out_dw")

    def merge_bwd(dm, yat, ybt, ga, gb):
        _, vjp = jax.vjp(merge_fn, yat.astype(F32), ybt.astype(F32), ga, gb)
        return vjp(dm)

    dya, dyb, dgbr_a, dgbr_b = _row_call(merge_bwd, [dmerged, ya, yb, col(z_gbr, 0, 1024), col(z_gbr, 1, 1024)], [],
                                         [(1024, BF16)] * 4, T=T, tm=tm, name="merge_bwd")
    doa = _mm(dya, W["w_o_gdn"], "nt", F32, name="w_o_gdn_dx")
    dob = _mm(dyb, W["w_o_mla"], "nt", BF16, name="w_o_mla_dx")
    dW["w_o_gdn"] = _mm(oa, dya, "tn", F32, name="w_o_gdn_dw")
    dW["w_o_mla"] = _mm(ob, dyb, "tn", F32, name="w_o_mla_dw")

    dqn, dqp, dl = _flash_bwd_dq(qn, qp, kn, kpe, vb, ob, dob, lse, B=B, S=S, t=ta)
    dkn, dkp, dvb = _flash_bwd_dkv(qn, qp, kn, kpe, vb, dob, lse.reshape(HEADS, 1, T), dl.reshape(HEADS, 1, T),
                                   B=B, S=S, t=ta)

    def qrope_bwd(d, c, s):
        return d * jnp.tile(c, (1, HEADS)), d * jnp.tile(s, (1, HEADS))

    dq_lin, dq_rot = _row_call(qrope_bwd, [dqp, tab(cos), tab(sin)], [], [(1024, BF16), (1024, BF16)], T=T, tm=tm,
                               name="q_rope_bwd")
    wp_lin, wp_rot = W["uq_p"][:, :1024], W["uq_p"][:, 1024:]
    dcq = _mm(dqn, W["uq_n"], "nt", F32, name="uq_n_dx")
    dcq = _mm(dq_lin, wp_lin, "nt", F32, add=dcq, name="uq_pl_dx")
    dcq = _mm(dq_rot, wp_rot, "nt", F32, add=dcq, name="uq_pr_dx")
    dW["uq_n"] = _mm(cq, dqn, "tn", F32, name="uq_n_dw")
    dW["uq_p"] = jnp.concatenate([_mm(cq, dq_lin, "tn", F32, name="uq_pl_dw"), _mm(cq, dq_rot, "tn", F32, name="uq_pr_dw")], axis=1)
    dckv = _mm(dkn, W["ukv_k"], "nt", F32, name="ukv_k_dx")
    dckv = _mm(dvb, W["ukv_v"], "nt", F32, add=dckv, name="ukv_v_dx")
    dW["ukv_k"] = _mm(ckv, dkn, "tn", F32, name="ukv_k_dw")
    dW["ukv_v"] = _mm(ckv, dvb, "tn", F32, name="ukv_v_dw")

    def mla_prep_bwd(zq, zkv0, zkv1, zpl, zpr, c, s, dcqt, dckvt, dkpt, gq, gkv):
        zkv = jnp.concatenate([zkv0, zkv1], axis=1)
        _, vjp = jax.vjp(lambda a, b, p, r, g1, g2: mla_prep_fn(a, b, p, r, c, s, g1, g2), zq, zkv, zpl, zpr, gq, gkv)
        dk = dkpt[0]
        for h in range(1, HEADS):
            dk = dk + dkpt[h]
        dzq, dzkv, dzpl, dzpr, dgq, dgkv = vjp((dcqt, dckvt, dk))
        return jnp.concatenate([dzq, dzkv, dzpl, dzpr], axis=1), dgq, dgkv

    dz_small, dg_q, dg_kv = _row_call(
        mla_prep_bwd, small_cols + [tab(cos), tab(sin), dcq, dckv, (dkp, (HEADS, tm, HD), lambda i: (0, i, 0))],
        [g_q, g_kv], [(896, BF16)], [((1, Q_RANK), F32), ((1, KV_RANK), F32)], T=T, tm=tm, name="mla_prep_bwd")

    def gdn_out_bwd(o, ga, dot_, g):
        _, vjp = jax.vjp(gdn_out_fn, o, ga, g)
        return vjp(dot_)

    do_gdn, dz_ga, dg_gdn = _row_call(gdn_out_bwd, [o_gdn, z_ga, doa], [g_gdn], [(1024, F32), (1024, BF16)],
                                      [((1, HD), F32)], T=T, tm=tm, name="gdn_out_bwd")
    dqkvn, dgc, dgrT, dbeta = _gdn_bwd(qkvn, gcum, grT, beta, states, do_gdn, B=B, S=S)
    dgc_tot = dgc + _padc(dgrT[:, :, 0, :].transpose(0, 2, 1).reshape(T, HEADS), LANES)

    def gate_bwd(za, zb, dg, db, al, db_):
        _, vjp = jax.vjp(_gate_fn, za, zb, al, db_)
        return vjp((dg, db))

    dz_a, dz_b, d_alog, d_dtb = _row_call(gate_bwd, [col(z_ab, 0, LANES), col(z_ab, 1, LANES), dgc_tot, dbeta], [alog, dtb],
                                          [(LANES, BF16), (LANES, BF16)], [((1, LANES), F32), ((1, LANES), F32)],
                                          T=T, tm=tm, name="gdn_gate_bwd")
    dz_qkv, d_conv_qkv = _conv_bwd(_qkv_fn, [(z_qkv, 0)], [(conv_qkv_w, 0)], dqkvn, BF16, T=T, S=S, tm=tm, cb=QKV_CB,
                                   ncb=3072 // QKV_CB, name="gdn_qkv_bwd")

    dz_ab = jnp.concatenate([dz_a, dz_b], axis=1)
    dz_gbr = jnp.concatenate([dgbr_a, dgbr_b], axis=1)
    dh1 = None
    for key, dz in (("in_qkv", dz_qkv), ("in_ga", dz_ga), ("in_ab", dz_ab), ("in_small", dz_small), ("in_gbr", dz_gbr)):
        dh1 = _mm(dz, W[key], "nt", F32, add=dh1, name=key + "_dx", tk=896 if key == "in_small" else 1024)
        dW[key] = _mm(h1, dz, "tn", F32, name=key + "_dw", tn=896 if key == "in_small" else 1024)
    dx, dg_mix = _norm_bwd(x2d, g_mix, dh1, dx1, T=T, tm=tm, name="norm_mix_bwd")

    dsmall = {"norm_mix_g": dg_mix, "gdn_a_log": d_alog[:, :HEADS], "gdn_dt_bias": d_dtb[:, :HEADS], "gdn_norm_g": dg_gdn,
              "mla_q_norm_g": dg_q, "mla_kv_norm_g": dg_kv, "norm_ffn_g": dg_ffn, "norm_final_g": dg_fin}
    return loss_v[0, 0], dx.reshape(B, S, D_MODEL), dW, d_conv_qkv, d_conv_ffn, dsmall


def kernel(x, norm_mix_g, w_in, conv_qkv_w, gdn_a_log, gdn_dt_bias, gdn_norm_g, mla_q_norm_g, w_uq, mla_kv_norm_g, w_ukv, w_o_gdn, w_o_mla, w_out, norm_ffn_g, w_up, conv_ffn_w, w_down, norm_final_g, loss_target, m_norm_mix_g, m_w_in, m_conv_qkv_w, m_gdn_a_log, m_gdn_dt_bias, m_gdn_norm_g, m_mla_q_norm_g, m_w_uq, m_mla_kv_norm_g, m_w_ukv, m_w_o_gdn, m_w_o_mla, m_w_out, m_norm_ffn_g, m_w_up, m_conv_ffn_w, m_w_down, m_norm_final_g, v_norm_mix_g, v_w_in, v_conv_qkv_w, v_gdn_a_log, v_gdn_dt_bias, v_gdn_norm_g, v_mla_q_norm_g, v_w_uq, v_mla_kv_norm_g, v_w_ukv, v_w_o_gdn, v_w_o_mla, v_w_out, v_norm_ffn_g, v_w_up, v_conv_ffn_w, v_w_down, v_norm_final_g):
    w = dict(norm_mix_g=norm_mix_g, w_in=w_in, conv_qkv_w=conv_qkv_w, gdn_a_log=gdn_a_log, gdn_dt_bias=gdn_dt_bias,
             gdn_norm_g=gdn_norm_g, mla_q_norm_g=mla_q_norm_g, w_uq=w_uq, mla_kv_norm_g=mla_kv_norm_g, w_ukv=w_ukv,
             w_o_gdn=w_o_gdn, w_o_mla=w_o_mla, w_out=w_out, norm_ffn_g=norm_ffn_g, w_up=w_up, conv_ffn_w=conv_ffn_w,
             w_down=w_down, norm_final_g=norm_final_g)
    m = dict(norm_mix_g=m_norm_mix_g, w_in=m_w_in, conv_qkv_w=m_conv_qkv_w, gdn_a_log=m_gdn_a_log, gdn_dt_bias=m_gdn_dt_bias,
             gdn_norm_g=m_gdn_norm_g, mla_q_norm_g=m_mla_q_norm_g, w_uq=m_w_uq, mla_kv_norm_g=m_mla_kv_norm_g, w_ukv=m_w_ukv,
             w_o_gdn=m_w_o_gdn, w_o_mla=m_w_o_mla, w_out=m_w_out, norm_ffn_g=m_norm_ffn_g, w_up=m_w_up,
             conv_ffn_w=m_conv_ffn_w, w_down=m_w_down, norm_final_g=m_norm_final_g)
    v = dict(norm_mix_g=v_norm_mix_g, w_in=v_w_in, conv_qkv_w=v_conv_qkv_w, gdn_a_log=v_gdn_a_log, gdn_dt_bias=v_gdn_dt_bias,
             gdn_norm_g=v_gdn_norm_g, mla_q_norm_g=v_mla_q_norm_g, w_uq=v_w_uq, mla_kv_norm_g=v_mla_kv_norm_g, w_ukv=v_w_ukv,
             w_o_gdn=v_w_o_gdn, w_o_mla=v_w_o_mla, w_out=v_w_out, norm_ffn_g=v_norm_ffn_g, w_up=v_w_up,
             conv_ffn_w=v_conv_ffn_w, w_down=v_w_down, norm_final_g=v_norm_final_g)
    slab_names = ("A", "Q", "C", "V")
    big_names = ("w_in", "w_up", "w_uq", "w_ukv", "w_o_gdn", "w_o_mla", "w_out", "w_down", "conv_qkv_w", "conv_ffn_w")
    small_names = [n for n, _ in SMALL]
    small_shapes = {n: w[n].shape for n in small_names}
    local2d = lambda d: {n: d[n][0] for n in big_names}

    w_slabs = _slabs(local2d(w), F32)
    send = [w_slabs[k].astype(BF16) if k != "V" else w_slabs[k] for k in slab_names]
    gathered = _allgather(send, name="allgather_weights")
    gathered = {k: g.reshape(N_DEV, -1, g.shape[1]) for k, g in zip(slab_names, gathered)}
    W, conv_qkv_full, conv_ffn_full = _layout_weights(gathered)

    loss_local, dx, dW, d_conv_qkv, d_conv_ffn, dsmall = _local_step(
        x, loss_target, W, conv_qkv_full, conv_ffn_full, {n: w[n] for n in small_names})

    g_send = _send_slabs(dW, d_conv_qkv, d_conv_ffn)
    recv = _alltoall([g_send[k] for k in slab_names], name="alltoall_grads")
    m_slabs, v_slabs = _slabs(local2d(m), F32), _slabs(local2d(v), F32)
    upd = {k: _reduce_adam(r, w_slabs[k], m_slabs[k], v_slabs[k], tr=SLAB_TR[k], name="adam_" + k)
           for k, r in zip(slab_names, recv)}
    small_parts = _allgather([_pack_small(dsmall)], name="allgather_small_grads")[0]
    upd_small = _reduce_adam(small_parts, _pack_small({n: w[n] for n in small_names}), _pack_small({n: m[n] for n in small_names}),
                             _pack_small({n: v[n] for n in small_names}), tr=SMALL_ROWS, name="adam_small")

    loss = lax.psum(loss_local, ("x", "y", "c"))
    groups = []
    for i in range(4):
        merged = {**_unslab({k: upd[k][i] for k in slab_names}), **_unpack_small(upd_small[i], small_shapes)}
        groups.append([merged[n] for n in WEIGHT_ORDER])
    return (loss, dx, *groups[0], *groups[1], *groups[2], *groups[3])
```

```python
import functools
import math

import numpy as np
import jax
import jax.numpy as jnp
from jax import lax
from jax.experimental import pallas as pl
from jax.experimental.pallas import tpu as pltpu
from jax.experimental.pallas import tpu_sc as plsc

F32 = jnp.float32
BF16 = jnp.bfloat16

D_MODEL = 1024
HEADS = 8
HD = 128
GDN_CONV = 4
CHUNK = 64
Q_RANK = 384
KV_RANK = 256
ROPE = 64
ROPE_THETA = 10000.0
D_FF = 2816
FFN_CONV = 3
EPS = 1e-6
SM_SCALE = (HD + ROPE) ** -0.5
N_DEV = 8

ADAM_LR, ADAM_B1, ADAM_B2, ADAM_EPS, ADAM_WD, ADAM_STEP = 0.001, 0.9, 0.999, 1e-08, 0.01, 10

LANES = 128
SUBLANES = 8
HALO = 2 * SUBLANES
VMEM_LIMIT = 56 * 1024 * 1024
HI = lax.Precision.HIGHEST
TRI_PRECISION = None

NN = (((1,), (0,)), ((), ()))
NT = (((1,), (1,)), ((), ()))
TN = (((0,), (0,)), ((), ()))


def _dot(a, b, dims=NN, precision=None):
    return lax.dot_general(a, b, dims, precision=precision, preferred_element_type=F32)


def _pick(dim, target, align):
    best = None
    for t in range(align, min(dim, target) + 1, align):
        if dim % t == 0:
            best = t
    return dim if best is None else best


def _call(body, ins, outs, grid, *, name, scratch=(), semantics=None):
    n_in, n_out = len(ins), len(outs)

    def kern(*refs):
        body(refs[:n_in], refs[n_in:n_in + n_out], refs[n_in + n_out:])

    res = pl.pallas_call(
        kern,
        grid=grid,
        in_specs=[pl.BlockSpec(bs, im) for _, bs, im in ins],
        out_specs=[pl.BlockSpec(bs, im) for _, _, bs, im in outs],
        out_shape=[jax.ShapeDtypeStruct(s, d) for s, d, _, _ in outs],
        scratch_shapes=list(scratch),
        name=name,
        compiler_params=pltpu.CompilerParams(
            dimension_semantics=semantics or ("arbitrary",) * len(grid), vmem_limit_bytes=VMEM_LIMIT),
    )(*[a for a, _, _ in ins])
    return res


def _mm(a, b, mode, out_dtype, *, name, add=None, tm=1408, tn=1408, tk=1408):
    if mode == "nn":
        (M, K), (K2, N) = a.shape, b.shape
    elif mode == "nt":
        (M, K), (N, K2) = a.shape, b.shape
    else:
        (K, M), (K2, N) = a.shape, b.shape
    assert K == K2, (a.shape, b.shape, mode)
    tm = _pick(M, tm, LANES if mode == "tn" else 16)
    tn = _pick(N, tn, LANES)
    tk = _pick(K, tk, 16 if mode == "tn" else LANES)
    nk = K // tk
    dims = {"nn": NN, "nt": NT, "tn": TN}[mode]
    if mode == "nn":
        a_spec, b_spec = ((tm, tk), lambda i, j, k: (i, k)), ((tk, tn), lambda i, j, k: (k, j))
    elif mode == "nt":
        a_spec, b_spec = ((tm, tk), lambda i, j, k: (i, k)), ((tn, tk), lambda i, j, k: (j, k))
    else:
        a_spec, b_spec = ((tk, tm), lambda i, j, k: (k, i)), ((tk, tn), lambda i, j, k: (k, j))
    ins = [(a,) + a_spec, (b,) + b_spec]
    if add is not None:
        ins.append((add, (tm, tn), lambda i, j, k: (i, j)))
    outs = [((M, N), out_dtype, (tm, tn), lambda i, j, k: (i, j))]

    def body(in_refs, out_refs, scr):
        prod = _dot(in_refs[0][...].astype(BF16), in_refs[1][...].astype(BF16), dims)

        def finish(r):
            if add is not None:
                r = r + in_refs[2][...].astype(F32)
            out_refs[0][...] = r.astype(out_dtype)

        if nk == 1:
            finish(prod)
            return
        k = pl.program_id(2)
        acc = scr[0]

        @pl.when(k == 0)
        def _():
            acc[...] = prod

        @pl.when(k > 0)
        def _():
            acc[...] += prod

        @pl.when(k == nk - 1)
        def _():
            finish(acc[...])

    return _call(body, ins, outs, (M // tm, N // tn, nk), name=name,
                 scratch=[pltpu.VMEM((tm, tn), F32)] if nk > 1 else [],
                 semantics=("parallel", "parallel", "arbitrary"))[0]


def _row_call(fn, rows, consts, out_rows, out_accs=(), *, T, tm, name):
    nt = T // tm
    ins = []
    for r in rows:
        ins.append(r if isinstance(r, tuple) else (r, (tm, r.shape[1]), lambda i: (i, 0)))
    for c in consts:
        ins.append((c, c.shape, lambda i, nd=c.ndim: (0,) * nd))
    outs = []
    for o in out_rows:
        outs.append(((T, o[0]), o[1], (tm, o[0]), lambda i: (i, 0)) if len(o) == 2 else o)
    for shp, dt in out_accs:
        outs.append((shp, dt, shp, lambda i, nd=len(shp): (0,) * nd))
    n_r = len(out_rows)

    def body(in_refs, out_refs, _):
        i = pl.program_id(0)
        vals = fn(*[r[...] for r in in_refs])
        for o_ref, v in zip(out_refs[:n_r], vals[:n_r]):
            o_ref[...] = v.astype(o_ref.dtype)
        for o_ref, v in zip(out_refs[n_r:], vals[n_r:]):
            @pl.when(i == 0)
            def _(o_ref=o_ref):
                o_ref[...] = jnp.zeros_like(o_ref)

            o_ref[...] += v.astype(o_ref.dtype)

    return _call(body, ins, outs, (nt,), name=name)


def _rms(x, g):
    return x * lax.rsqrt(jnp.mean(x * x, axis=-1, keepdims=True) + EPS) * g


def _norm_fwd(x, g, *, T, tm, name):
    return _row_call(lambda xt, gt: (_rms(xt, gt),), [x], [g], [(x.shape[1], BF16)], T=T, tm=tm, name=name)[0]


def _norm_bwd(x, g, dh, dres, *, T, tm, name):
    def fn(xt, dht, drt, gt):
        _, vjp = jax.vjp(_rms, xt, gt)
        dx, dg = vjp(dht)
        return drt + dx, dg

    return _row_call(fn, [x, dh, dres], [g], [(x.shape[1], F32)], [(g.shape, F32)], T=T, tm=tm, name=name)


def _rows16(c):
    return lax.broadcasted_iota(jnp.int32, (HALO, c), 0)


@functools.lru_cache(maxsize=None)
def _shift_fn(j):
    @jax.custom_vjp
    def shift(x, halo):
        xr = pltpu.roll(x, j, 0)
        top = jnp.where(_rows16(x.shape[1]) < j, pltpu.roll(halo, j, 0), xr[:HALO])
        return jnp.concatenate([top, xr[HALO:]], axis=0)

    def fwd(x, halo):
        return shift(x, halo), None

    def bwd(_, dy):
        tm, c = dy.shape
        keep = _rows16(c) >= HALO - j
        dxr = pltpu.roll(dy, tm - j, 0)
        dx = jnp.concatenate([dxr[:tm - HALO], jnp.where(keep, 0.0, dxr[tm - HALO:])], axis=0)
        dhalo = jnp.where(keep, pltpu.roll(dy[:HALO], HALO - j, 0), 0.0)
        return dx, dhalo

    shift.defvjp(fwd, bwd)
    return shift


def _dwconv(tail, x, w):
    K = w.shape[0]
    acc = w[K - 1:K, :] * x
    for k in range(K - 1):
        acc = acc + w[k:k + 1, :] * _shift_fn(K - 1 - k)(x, tail)
    return acc


STRIP = 64


def _conv_fwd(fn, xs, ws, out_c, out_dtype, *, T, S, tm, cb, ncb, name):
    nt, tps, hb = T // tm, S // tm, tm // HALO
    ins = []
    for arr, off in xs:
        ins.append((arr, (tm, cb), lambda j, i, off=off: (i, off + j)))
        ins.append((arr, (HALO, cb), lambda j, i, off=off: (jnp.maximum(i * hb - 1, 0), off + j)))
    for arr, off in ws:
        ins.append((arr, (arr.shape[0], cb), lambda j, i, off=off: (0, off + j)))
    outs = [((T, out_c), out_dtype, (tm, cb), lambda j, i: (i, j))]
    nx = len(xs)

    def body(in_refs, out_refs, _):
        j, i = pl.program_id(0), pl.program_id(1)
        first = (i % tps) == 0
        wts = [r[...] for r in in_refs[2 * nx:]]
        for r in range(0, tm, STRIP):
            xts = [in_refs[2 * m][r:r + STRIP, :].astype(F32) for m in range(nx)]
            if r == 0:
                tails = [jnp.where(first, 0.0, in_refs[2 * m + 1][...].astype(F32)) for m in range(nx)]
            else:
                tails = [in_refs[2 * m][r - HALO:r, :].astype(F32) for m in range(nx)]
            out_refs[0][r:r + STRIP, :] = fn(j, tails, xts, wts).astype(out_dtype)

    return _call(body, ins, outs, (ncb, nt), name=name)[0]


def _conv_bwd(fn, xs, ws, dout, dx_dtype, *, T, S, tm, cb, ncb, name):
    nt, tps, hb = T // tm, S // tm, tm // HALO
    ins = []
    for arr, off in xs:
        ins.append((arr, (tm, cb), lambda j, i, off=off: (nt - 1 - i, off + j)))
        ins.append((arr, (HALO, cb), lambda j, i, off=off: (jnp.maximum((nt - 1 - i) * hb - 1, 0), off + j)))
    for arr, off in ws:
        ins.append((arr, (arr.shape[0], cb), lambda j, i, off=off: (0, off + j)))
    ins.append((dout, (tm, cb), lambda j, i: (nt - 1 - i, j)))
    nx, nw = len(xs), len(ws)
    outs = [((T, ncb * cb), dx_dtype, (tm, cb), lambda j, i: (nt - 1 - i, j)) for _ in xs]
    outs += [((arr.shape[0], ncb * cb), F32, (arr.shape[0], cb), lambda j, i: (0, j)) for arr, _ in ws]
    scratch = [pltpu.VMEM((HALO, cb), F32) for _ in xs]

    def body(in_refs, out_refs, carry):
        j, i = pl.program_id(0), pl.program_id(1)
        first = ((nt - 1 - i) % tps) == 0
        wts = [ref[...] for ref in in_refs[2 * nx:2 * nx + nw]]

        @pl.when(i == 0)
        def _():
            for c in carry:
                c[...] = jnp.zeros_like(c)

        carried = [c[...] for c in carry]
        dw_sum = None
        for r in reversed(range(0, tm, STRIP)):
            xts = [in_refs[2 * m][r:r + STRIP, :].astype(F32) for m in range(nx)]
            if r == 0:
                tails = [jnp.where(first, 0.0, in_refs[2 * m + 1][...].astype(F32)) for m in range(nx)]
            else:
                tails = [in_refs[2 * m][r - HALO:r, :].astype(F32) for m in range(nx)]
            _, vjp = jax.vjp(lambda tl, xt, wt: fn(j, tl, xt, wt), tails, xts, wts)
            dtails, dxts, dwts = vjp(in_refs[-1][r:r + STRIP, :].astype(F32))
            for m in range(nx):
                pad = jnp.concatenate([jnp.zeros((STRIP - HALO, cb), F32), carried[m]], axis=0)
                out_refs[m][r:r + STRIP, :] = (dxts[m] + pad).astype(dx_dtype)
            carried = [jnp.where(first, 0.0, dt) for dt in dtails] if r == 0 else list(dtails)
            dw_sum = list(dwts) if dw_sum is None else [a + b for a, b in zip(dw_sum, dwts)]
        for m in range(nx):
            carry[m][...] = carried[m]
        for m in range(nw):
            o_ref = out_refs[nx + m]

            @pl.when(i == 0)
            def _(o_ref=o_ref):
                o_ref[...] = jnp.zeros_like(o_ref)

            o_ref[...] += dw_sum[m]

    return _call(body, ins, outs, (ncb, nt), name=name, scratch=scratch)


QKV_CB = 512


def _qkv_fn(j, tails, xts, wts):
    y = jax.nn.silu(_dwconv(tails[0], xts[0], wts[0]))
    scale = jnp.where(j < 1024 // QKV_CB, HD ** -0.5, 1.0)
    parts = []
    for h in range(QKV_CB // HD):
        yh = y[:, h * HD:(h + 1) * HD]
        nh = yh * lax.rsqrt(jnp.sum(yh * yh, axis=-1, keepdims=True) + EPS)
        parts.append(jnp.where(j < 2048 // QKV_CB, nh * scale, yh))
    return jnp.concatenate(parts, axis=1)


def _ffn_fn(j, tails, xts, wts):
    return jax.nn.silu(_dwconv(tails[0], xts[0], wts[0])) * _dwconv(tails[1], xts[1], wts[1])


BNN = (((2,), (1,)), ((0,), (0,)))
BNT = (((2,), (2,)), ((0,), (0,)))
BTN = (((1,), (1,)), ((0,), (0,)))


@jax.custom_vjp
def _tri_inv(L):
    C = L.shape[-1]
    ii = lax.broadcasted_iota(jnp.int32, (C, C), 0)
    jj = lax.broadcasted_iota(jnp.int32, (C, C), 1)
    eye = (ii == jj).astype(F32)
    X = eye - jnp.where((ii >> 1) == (jj >> 1), L, 0.0)
    s = 1
    while (2 << s) <= C:
        E = jnp.where(((ii >> (s + 1)) == (jj >> (s + 1))) & ((ii >> s) != (jj >> s)), L, 0.0)
        X = X - _dot(_dot(X, E, BNN, precision=TRI_PRECISION), X, BNN, precision=TRI_PRECISION)
        s += 1
    return X


def _tri_inv_fwd(L):
    X = _tri_inv(L)
    return X, X


def _tri_inv_bwd(X, dX):
    return (-_dot(_dot(X, dX, BTN, precision=TRI_PRECISION), X, BNT, precision=TRI_PRECISION),)


_tri_inv.defvjp(_tri_inv_fwd, _tri_inv_bwd)


def _gdn_chunk(q, k, v, gc, gr, beta, S):
    C = q.shape[1]
    ii = lax.broadcasted_iota(jnp.int32, (C, C), 0)
    jj = lax.broadcasted_iota(jnp.int32, (C, C), 1)
    lower = ii >= jj
    decay = jnp.where(lower, jnp.exp(jnp.where(lower, gc - gr, 0.0)), 0.0)
    kb, vb = k * beta, v * beta
    L = jnp.where(ii > jj, _dot(kb, k, BNT) * decay, 0.0)
    Tinv = _tri_inv(L)
    eg = jnp.exp(gc)
    u = _dot(Tinv, vb, BNN, precision=TRI_PRECISION)
    w = _dot(Tinv, kb * eg, BNN, precision=TRI_PRECISION)
    a = _dot(q, k, BNT) * decay
    g_last = gc[:, C - 1:C, :]
    kd = k * jnp.exp(g_last - gc)
    v_new = u - _dot(w, S, BNN)
    o = _dot(q * eg, S, BNN) + _dot(a, v_new, BNN)
    S_new = S * jnp.exp(g_last) + _dot(kd, v_new, BTN)
    return o, S_new


def _heads(ref, width=HD):
    return jnp.stack([ref[:, h * width:(h + 1) * width].astype(F32) for h in range(HEADS)])


def _gdn_fwd(qkvn, gcum, grT, beta, *, B, S):
    N, T = S // CHUNK, B * S
    row = lambda c: (lambda b, n: (b * N + n, c))
    ins = [(qkvn, (CHUNK, 1024), row(0)), (qkvn, (CHUNK, 1024), row(1)), (qkvn, (CHUNK, 1024), row(2)),
           (gcum, (CHUNK, LANES), row(0)), (grT, (1, HEADS, 1, CHUNK), lambda b, n: (b * N + n, 0, 0, 0)),
           (beta, (CHUNK, LANES), row(0))]
    outs = [((T, 1024), F32, (CHUNK, 1024), row(0)),
            ((B * N, HEADS, HD, HD), BF16, (1, HEADS, HD, HD), lambda b, n: (b * N + n, 0, 0, 0))]

    def body(in_refs, out_refs, scr):
        q_ref, k_ref, v_ref, gc_ref, gr_ref, b_ref = in_refs
        o_ref, st_ref = out_refs
        S_ref = scr[0]

        @pl.when(pl.program_id(1) == 0)
        def _():
            S_ref[...] = jnp.zeros_like(S_ref)

        S0 = S_ref[...]
        st_ref[0] = S0.astype(BF16)
        o, Sn = _gdn_chunk(_heads(q_ref), _heads(k_ref), _heads(v_ref), _heads(gc_ref, 1), gr_ref[0],
                           _heads(b_ref, 1), S0)
        for h in range(HEADS):
            o_ref[:, h * HD:(h + 1) * HD] = o[h]
        S_ref[...] = Sn

    return _call(body, ins, outs, (B, N), name="gdn_core_fwd", scratch=[pltpu.VMEM((HEADS, HD, HD), F32)])


def _gdn_bwd(qkvn, gcum, grT, beta, states, do, *, B, S):
    N, T = S // CHUNK, B * S
    row = lambda c: (lambda b, n: (b * N + N - 1 - n, c))
    ins = [(qkvn, (CHUNK, 1024), row(0)), (qkvn, (CHUNK, 1024), row(1)), (qkvn, (CHUNK, 1024), row(2)),
           (gcum, (CHUNK, LANES), row(0)), (grT, (1, HEADS, 1, CHUNK), lambda b, n: (b * N + N - 1 - n, 0, 0, 0)),
           (beta, (CHUNK, LANES), row(0)),
           (states, (1, HEADS, HD, HD), lambda b, n: (b * N + N - 1 - n, 0, 0, 0)), (do, (CHUNK, 1024), row(0))]
    outs = [((T, 3072), BF16, (CHUNK, 3072), row(0)), ((T, LANES), F32, (CHUNK, LANES), row(0)),
            ((B * N, HEADS, 1, CHUNK), F32, (1, HEADS, 1, CHUNK), lambda b, n: (b * N + N - 1 - n, 0, 0, 0)),
            ((T, LANES), F32, (CHUNK, LANES), row(0))]

    def body(in_refs, out_refs, scr):
        q_ref, k_ref, v_ref, gc_ref, gr_ref, b_ref, st_ref, do_ref = in_refs
        dqkv_ref, dgc_ref, dgr_ref, db_ref = out_refs
        dS_ref = scr[0]

        @pl.when(pl.program_id(1) == 0)
        def _():
            dS_ref[...] = jnp.zeros_like(dS_ref)

        args = (_heads(q_ref), _heads(k_ref), _heads(v_ref), _heads(gc_ref, 1), gr_ref[0], _heads(b_ref, 1),
                st_ref[0].astype(F32))
        _, vjp = jax.vjp(_gdn_chunk, *args)
        dq, dk, dv, dgc, dgr, db, dS = vjp((_heads(do_ref), dS_ref[...]))
        lane = lax.broadcasted_iota(jnp.int32, (CHUNK, LANES), 1)
        dgc_all = jnp.zeros((CHUNK, LANES), F32)
        db_all = jnp.zeros((CHUNK, LANES), F32)
        for h in range(HEADS):
            dqkv_ref[:, h * HD:(h + 1) * HD] = dq[h].astype(BF16)
            dqkv_ref[:, 1024 + h * HD:1024 + (h + 1) * HD] = dk[h].astype(BF16)
            dqkv_ref[:, 2048 + h * HD:2048 + (h + 1) * HD] = dv[h].astype(BF16)
            dgc_all = jnp.where(lane == h, dgc[h], dgc_all)
            db_all = jnp.where(lane == h, db[h], db_all)
        dgc_ref[...] = dgc_all
        db_ref[...] = db_all
        dgr_ref[0] = dgr
        dS_ref[...] = dS

    return _call(body, ins, outs, (B, N), name="gdn_core_bwd", scratch=[pltpu.VMEM((HEADS, HD, HD), F32)])


def _gate_fn(za, zb, alog, dtb):
    tm = za.shape[0]
    g = -jnp.exp(alog) * jax.nn.softplus(za + dtb)
    ii = lax.broadcasted_iota(jnp.int32, (tm, tm), 0)
    jj = lax.broadcasted_iota(jnp.int32, (tm, tm), 1)
    tri = ((ii >= jj) & ((ii >> 6) == (jj >> 6))).astype(F32)
    return _dot(tri, g, precision=HI), jax.nn.sigmoid(zb)


def _scores(qn_ref, qp_ref, kn_ref, kp_ref, diag):
    q = jnp.concatenate([qn_ref[...], qp_ref[...]], axis=1)
    k = jnp.concatenate([kn_ref[...], kp_ref[...]], axis=1)
    s = _dot(q, k, NT) * SM_SCALE
    if diag:
        t = s.shape[0]
        ii = lax.broadcasted_iota(jnp.int32, (t, t), 0)
        jj = lax.broadcasted_iota(jnp.int32, (t, t), 1)
        s = jnp.where(ii >= jj, s, -jnp.inf)
    return s, q, k


HPB = 4
HW = HPB * HD


def _head_refs(refs, hh):
    return [r.at[:, hh * HD:(hh + 1) * HD] for r in refs]


def _flash_fwd(qn, qp, kn, kp, v, *, B, S, t):
    nb, T = S // t, B * S
    qmap = lambda b, h, qi, ki: (b * nb + qi, h)
    kmap = lambda b, h, qi, ki: (b * nb + jnp.minimum(ki, qi), h)
    kpmap = lambda b, h, qi, ki: (b * nb + jnp.minimum(ki, qi), 0)
    ins = [(qn, (t, HW), qmap), (qp, (t, HW), qmap), (kn, (t, HW), kmap), (kp, (t, HD), kpmap), (v, (t, HW), kmap)]
    outs = [((T, 1024), BF16, (t, HW), qmap),
            ((HEADS, T, 1), F32, (HPB, t, 1), lambda b, h, qi, ki: (h, b * nb + qi, 0))]
    scratch = [pltpu.VMEM((HPB, t, 1), F32), pltpu.VMEM((HPB, t, 2 * HD), F32)]

    def body(in_refs, out_refs, scr):
        qn_ref, qp_ref, kn_ref, kp_ref, v_ref = in_refs
        o_ref, lse_ref = out_refs
        m_ref, acc_ref = scr
        qi, ki = pl.program_id(2), pl.program_id(3)

        @pl.when(ki == 0)
        def _():
            m_ref[...] = jnp.full_like(m_ref, -jnp.inf)
            acc_ref[...] = jnp.zeros_like(acc_ref)

        def step(diag):
            for hh in range(HPB):
                qn_h, qp_h, kn_h, v_h = _head_refs((qn_ref, qp_ref, kn_ref, v_ref), hh)
                s, _, _ = _scores(qn_h, qp_h, kn_h, kp_ref, diag)
                m_old = m_ref[hh]
                m_new = jnp.maximum(m_old, jnp.max(s, axis=-1, keepdims=True))
                p = jnp.exp(s - m_new)
                alpha = jnp.exp(m_old - m_new)
                v1 = jnp.concatenate([v_h[...], jnp.ones((t, HD), BF16)], axis=1)
                acc_ref[hh] = alpha * acc_ref[hh] + _dot(p.astype(BF16), v1)
                m_ref[hh] = m_new

        @pl.when(ki < qi)
        def _():
            step(False)

        @pl.when(ki == qi)
        def _():
            step(True)
            for hh in range(HPB):
                o_ref[:, hh * HD:(hh + 1) * HD] = (acc_ref[hh, :, :HD] / acc_ref[hh, :, HD:]).astype(BF16)
                lse_ref[hh] = m_ref[hh] + jnp.log(acc_ref[hh, :, HD:HD + 1])

    return _call(body, ins, outs, (B, HEADS // HPB, nb, nb), name="mla_flash_fwd", scratch=scratch,
                 semantics=("parallel", "parallel", "parallel", "arbitrary"))


def _flash_bwd_dq(qn, qp, kn, kp, v, o, do, lse, *, B, S, t):
    nb, T = S // t, B * S
    qmap = lambda b, h, qi, ki: (b * nb + qi, h)
    kmap = lambda b, h, qi, ki: (b * nb + jnp.minimum(ki, qi), h)
    kpmap = lambda b, h, qi, ki: (b * nb + jnp.minimum(ki, qi), 0)
    ins = [(qn, (t, HW), qmap), (qp, (t, HW), qmap), (kn, (t, HW), kmap), (kp, (t, HD), kpmap), (v, (t, HW), kmap),
           (o, (t, HW), qmap), (do, (t, HW), qmap), (lse, (HPB, t, 1), lambda b, h, qi, ki: (h, b * nb + qi, 0))]
    outs = [((T, 1024), BF16, (t, HW), qmap), ((T, 1024), F32, (t, HW), qmap),
            ((HEADS, T, 1), F32, (HPB, t, 1), lambda b, h, qi, ki: (h, b * nb + qi, 0))]
    scratch = [pltpu.VMEM((HPB, t, 1), F32), pltpu.VMEM((HPB, t, 2 * HD), F32)]

    def body(in_refs, out_refs, scr):
        qn_ref, qp_ref, kn_ref, kp_ref, v_ref, o_ref, do_ref, lse_ref = in_refs
        dqn_ref, dqp_ref, dlo_ref = out_refs
        dl_ref, acc_ref = scr
        qi, ki = pl.program_id(2), pl.program_id(3)

        @pl.when(ki == 0)
        def _():
            for hh in range(HPB):
                o_h, do_h = _head_refs((o_ref, do_ref), hh)
                dl_ref[hh] = jnp.sum(do_h[...].astype(F32) * o_h[...].astype(F32), axis=-1, keepdims=True)
            acc_ref[...] = jnp.zeros_like(acc_ref)

        def step(diag):
            for hh in range(HPB):
                qn_h, qp_h, kn_h, v_h, do_h = _head_refs((qn_ref, qp_ref, kn_ref, v_ref, do_ref), hh)
                s, _, k = _scores(qn_h, qp_h, kn_h, kp_ref, diag)
                p = jnp.exp(s - lse_ref[hh])
                dp = _dot(do_h[...], v_h[...], NT)
                ds = p * (dp - dl_ref[hh]) * SM_SCALE
                acc_ref[hh] += _dot(ds.astype(BF16), k)

        @pl.when(ki < qi)
        def _():
            step(False)

        @pl.when(ki == qi)
        def _():
            step(True)
            for hh in range(HPB):
                dqn_ref[:, hh * HD:(hh + 1) * HD] = acc_ref[hh, :, :HD].astype(BF16)
                dqp_ref[:, hh * HD:(hh + 1) * HD] = acc_ref[hh, :, HD:]
            dlo_ref[...] = dl_ref[...]

    return _call(body, ins, outs, (B, HEADS // HPB, nb, nb), name="mla_flash_bwd_dq", scratch=scratch,
                 semantics=("parallel", "parallel", "parallel", "arbitrary"))


def _flash_bwd_dkv(qn, qp, kn, kp, v, do, lse_t, dl_t, *, B, S, t):
    nb, T = S // t, B * S
    qmap = lambda b, h, ki, qi: (b * nb + jnp.maximum(qi, ki), h)
    kmap = lambda b, h, ki, qi: (b * nb + ki, h)
    tmap = lambda b, h, ki, qi: (h, 0, b * nb + jnp.maximum(qi, ki))
    ins = [(qn, (t, HW), qmap), (qp, (t, HW), qmap), (kn, (t, HW), kmap),
           (kp, (t, HD), lambda b, h, ki, qi: (b * nb + ki, 0)), (v, (t, HW), kmap), (do, (t, HW), qmap),
           (lse_t, (HPB, 1, t), tmap), (dl_t, (HPB, 1, t), tmap)]
    outs = [((T, 1024), BF16, (t, HW), kmap), ((HEADS, T, HD), F32, (HPB, t, HD), lambda b, h, ki, qi: (h, b * nb + ki, 0)),
            ((T, 1024), BF16, (t, HW), kmap)]
    scratch = [pltpu.VMEM((HPB, t, 2 * HD), F32), pltpu.VMEM((HPB, t, HD), F32)]

    def body(in_refs, out_refs, scr):
        qn_ref, qp_ref, kn_ref, kp_ref, v_ref, do_ref, lse_ref, dl_ref = in_refs
        dkn_ref, dkp_ref, dv_ref = out_refs
        dk_acc, dv_acc = scr
        ki, qi = pl.program_id(2), pl.program_id(3)

        @pl.when(qi == 0)
        def _():
            dk_acc[...] = jnp.zeros_like(dk_acc)
            dv_acc[...] = jnp.zeros_like(dv_acc)

        def step(diag):
            for hh in range(HPB):
                qn_h, qp_h, kn_h, v_h, do_h = _head_refs((qn_ref, qp_ref, kn_ref, v_ref, do_ref), hh)
                q = jnp.concatenate([qn_h[...], qp_h[...]], axis=1)
                k = jnp.concatenate([kn_h[...], kp_ref[...]], axis=1)
                st = _dot(k, q, NT) * SM_SCALE
                if diag:
                    ii = lax.broadcasted_iota(jnp.int32, (t, t), 0)
                    jj = lax.broadcasted_iota(jnp.int32, (t, t), 1)
                    st = jnp.where(ii <= jj, st, -jnp.inf)
                do_t = do_h[...]
                pt = jnp.exp(st - lse_ref[hh])
                dst = pt * (_dot(v_h[...], do_t, NT) - dl_ref[hh]) * SM_SCALE
                dv_acc[hh] += _dot(pt.astype(BF16), do_t)
                dk_acc[hh] += _dot(dst.astype(BF16), q)

        @pl.when(qi > ki)
        def _():
            step(False)

        @pl.when(qi == ki)
        def _():
            step(True)

        @pl.when(qi == nb - 1)
        def _():
            for hh in range(HPB):
                dkn_ref[:, hh * HD:(hh + 1) * HD] = dk_acc[hh, :, :HD].astype(BF16)
                dkp_ref[hh] = dk_acc[hh, :, HD:]
                dv_ref[:, hh * HD:(hh + 1) * HD] = dv_acc[hh].astype(BF16)

    return _call(body, ins, outs, (B, HEADS // HPB, nb, nb), name="mla_flash_bwd_dkv", scratch=scratch,
                 semantics=("parallel", "parallel", "parallel", "arbitrary"))


def _allgather(shards, *, name):
    n_arr = len(shards)

    def body(*refs):
        x_refs, out_refs = refs[:n_arr], refs[n_arr:2 * n_arr]
        send_sems, recv_sems, local_sems = refs[2 * n_arr:]
        x, y, c = lax.axis_index("x"), lax.axis_index("y"), lax.axis_index("c")
        me, sibling = (x, y, c), (x, y, 1 - c)
        chips = [(1 - x, y), (x, 1 - y), (1 - x, 1 - y)]

        def rows(a, px, py, pc):
            m_per = shards[a].shape[0]
            return out_refs[a].at[pl.ds((4 * px + 2 * py + pc) * m_per, m_per), :]

        def copy(a, k, block, to, src=None):
            return pltpu.make_async_remote_copy(
                src_ref=rows(a, *block) if src is None else src, dst_ref=rows(a, *block),
                send_sem=send_sems.at[a, k], recv_sem=recv_sems.at[a, k], device_id=to,
                device_id_type=pl.DeviceIdType.MESH)

        mine = [pltpu.make_async_copy(x_refs[a], rows(a, *me), local_sems.at[a]) for a in range(n_arr)]
        for cp in mine:
            cp.start()
        first = []
        for a in range(n_arr):
            first.append(copy(a, 0, me, sibling, src=x_refs[a]))
            first += [copy(a, 1 + j, me, (*chip, c), src=x_refs[a]) for j, chip in enumerate(chips)]
        for cp in first:
            cp.start()
        passed = []
        for j, chip in enumerate(chips):
            for a in range(n_arr):
                copy(a, 1 + j, (*chip, c), me).wait_recv()
                cp = copy(a, 4 + j, (*chip, c), sibling)
                cp.start()
                passed.append(cp)
        for a in range(n_arr):
            copy(a, 0, sibling, me).wait_recv()
        for j, chip in enumerate(chips):
            for a in range(n_arr):
                copy(a, 4 + j, (*chip, 1 - c), me).wait_recv()
        for cp in first + passed:
            cp.wait_send()
        for cp in mine:
            cp.wait()

    return pl.pallas_call(
        body,
        out_shape=[jax.ShapeDtypeStruct((N_DEV * s.shape[0], s.shape[1]), s.dtype) for s in shards],
        in_specs=[pl.BlockSpec(memory_space=pl.ANY)] * n_arr,
        out_specs=[pl.BlockSpec(memory_space=pl.ANY)] * n_arr,
        scratch_shapes=[pltpu.SemaphoreType.DMA((n_arr, 7)), pltpu.SemaphoreType.DMA((n_arr, 7)),
                        pltpu.SemaphoreType.DMA((n_arr,))],
        name=name,
    )(*shards)


def _alltoall(sends, *, name):
    n_arr = len(sends)

    def body(*refs):
        s_refs, r_refs = refs[:n_arr], refs[n_arr:2 * n_arr]
        send_sems, recv_sems, local_sems = refs[2 * n_arr:]
        x, y, c = lax.axis_index("x"), lax.axis_index("y"), lax.axis_index("c")
        me = 4 * x + 2 * y + c

        def rows(ref, a, idx):
            m_per = sends[a].shape[0] // N_DEV
            return ref.at[pl.ds(idx * m_per, m_per), :]

        local = [pltpu.make_async_copy(rows(s_refs[a], a, me), rows(r_refs[a], a, me), local_sems.at[a])
                 for a in range(n_arr)]
        for cp in local:
            cp.start()
        copies = []
        for k in range(1, N_DEV):
            px = 1 - x if k & 4 else x
            py = 1 - y if k & 2 else y
            pc = 1 - c if k & 1 else c
            for a in range(n_arr):
                cp = pltpu.make_async_remote_copy(
                    src_ref=rows(s_refs[a], a, 4 * px + 2 * py + pc), dst_ref=rows(r_refs[a], a, me),
                    send_sem=send_sems.at[a, k - 1], recv_sem=recv_sems.at[a, k - 1],
                    device_id=(px, py, pc), device_id_type=pl.DeviceIdType.MESH)
                cp.start()
                copies.append(cp)
        for cp in copies:
            cp.wait()
        for cp in local:
            cp.wait()

    return pl.pallas_call(
        body,
        out_shape=[jax.ShapeDtypeStruct(s.shape, s.dtype) for s in sends],
        in_specs=[pl.BlockSpec(memory_space=pl.ANY)] * n_arr,
        out_specs=[pl.BlockSpec(memory_space=pl.ANY)] * n_arr,
        scratch_shapes=[pltpu.SemaphoreType.DMA((n_arr, 7)), pltpu.SemaphoreType.DMA((n_arr, 7)),
                        pltpu.SemaphoreType.DMA((n_arr,))],
        name=name,
    )(*sends)


def _alltoall_async(sends, *, name, collective_id):
    n_arr = len(sends)
    hbm = pltpu.MemorySpace.HBM
    s_refs = [jax.new_ref(a, memory_space=hbm) for a in sends]
    r_refs = [jax.empty_ref(jax.ShapeDtypeStruct(a.shape, a.dtype), memory_space=hbm) for a in sends]

    @pl.kernel(mesh=plsc.ScalarSubcoreMesh(axis_name="seq", num_cores=1), name=name,
               scratch_types=(pltpu.SemaphoreType.DMA((n_arr, 7)), pltpu.SemaphoreType.DMA((n_arr, 7)),
                              pltpu.SemaphoreType.DMA((n_arr,))),
               compiler_params=pltpu.CompilerParams(collective_id=collective_id))
    def launch(send_sems, recv_sems, local_sems):
        x, y, c = lax.axis_index("x"), lax.axis_index("y"), lax.axis_index("c")
        me = 4 * x + 2 * y + c
        peers = [(1 - x if k & 4 else x, 1 - y if k & 2 else y, 1 - c if k & 1 else c) for k in range(1, N_DEV)]
        barrier = pltpu.get_barrier_semaphore()
        for p in peers:
            pl.semaphore_signal(barrier, inc=1, device_id=p, device_id_type=pl.DeviceIdType.MESH)
        pl.semaphore_wait(barrier, N_DEV - 1)

        def rows(ref, a, idx):
            m_per = sends[a].shape[0] // N_DEV
            return ref.at[pl.ds(idx * m_per, m_per), :]

        local = [pltpu.make_async_copy(rows(s_refs[a], a, me), rows(r_refs[a], a, me), local_sems.at[a])
                 for a in range(n_arr)]
        for cp in local:
            cp.start()
        copies = []
        for k, (px, py, pc) in enumerate(peers):
            for a in range(n_arr):
                cp = pltpu.make_async_remote_copy(
                    src_ref=rows(s_refs[a], a, 4 * px + 2 * py + pc), dst_ref=rows(r_refs[a], a, me),
                    send_sem=send_sems.at[a, k], recv_sem=recv_sems.at[a, k],
                    device_id=(px, py, pc), device_id_type=pl.DeviceIdType.MESH)
                cp.start()
                copies.append(cp)
        for cp in copies:
            cp.wait()
        for cp in local:
            cp.wait()

    launch()
    return [r[...] for r in r_refs]


def _reduce_adam(parts, w, m, v, *, tr, name):
    R, C = w.shape
    nR = R // tr
    ins = [(parts, (tr, C), lambda i, s=s: (s * nR + i, 0)) for s in range(N_DEV)]
    ins += [(a, (tr, C), lambda i: (i, 0)) for a in (w, m, v)]
    outs = [((R, C), F32, (tr, C), lambda i: (i, 0)) for _ in range(4)]
    c1 = 1.0 - ADAM_B1 ** ADAM_STEP
    c2 = 1.0 - ADAM_B2 ** ADAM_STEP

    def body(in_refs, out_refs, _):
        g = in_refs[0][...].astype(F32)
        for s in range(1, N_DEV):
            g = g + in_refs[s][...].astype(F32)
        wv, mv, vv = in_refs[8][...], in_refs[9][...], in_refs[10][...]
        mn = ADAM_B1 * mv + (1.0 - ADAM_B1) * g
        vn = ADAM_B2 * vv + (1.0 - ADAM_B2) * (g * g)
        delta = -ADAM_LR * ((mn / c1) / (jnp.sqrt(vn / c2) + ADAM_EPS) + ADAM_WD * wv)
        out_refs[0][...] = g
        out_refs[1][...] = delta
        out_refs[2][...] = mn
        out_refs[3][...] = vn

    return _call(body, ins, outs, (nR,), name=name, semantics=("parallel",))


IN_C, UP_C, UQ_C, QKV_C = 858, 704, 192, 384
A_W, Q_W, V_W = 896, 256, 768
SLAB_TR = {"A": 256, "Q": 128, "C": 368, "V": 16}
SMALL = [("norm_mix_g", 1024), ("gdn_a_log", 8), ("gdn_dt_bias", 8), ("gdn_norm_g", 128), ("mla_q_norm_g", 384),
         ("mla_kv_norm_g", 256), ("norm_ffn_g", 1024), ("norm_final_g", 1024)]
SMALL_ROWS = 32
WEIGHT_ORDER = ["norm_mix_g", "w_in", "conv_qkv_w", "gdn_a_log", "gdn_dt_bias", "gdn_norm_g", "mla_q_norm_g", "w_uq",
                "mla_kv_norm_g", "w_ukv", "w_o_gdn", "w_o_mla", "w_out", "norm_ffn_g", "w_up", "conv_ffn_w", "w_down",
                "norm_final_g"]


def _padc(w, n):
    return jnp.pad(w, ((0, 0), (0, n - w.shape[1])))


def _padrc(w, r, n):
    return jnp.pad(w, ((0, r - w.shape[0]), (0, n - w.shape[1])))


def _slabs(p, dtype):
    A = jnp.concatenate([_padc(p["w_in"], A_W), _padc(p["w_up"], A_W)], axis=0).astype(dtype)
    Q = jnp.concatenate([_padc(p["w_uq"], Q_W), p["w_ukv"]], axis=0).astype(dtype)
    C = jnp.concatenate([p["w_o_gdn"], p["w_o_mla"], p["w_out"], p["w_down"]], axis=0).astype(dtype)
    V = jnp.concatenate([_padrc(p["conv_qkv_w"], 8, V_W), _padrc(p["conv_ffn_w"], 8, V_W)], axis=0).astype(F32)
    return {"A": A, "Q": Q, "C": C, "V": V}


def _unslab(sl):
    A, Q, C, V = sl["A"], sl["Q"], sl["C"], sl["V"]
    out = {"w_in": A[:1024, :IN_C], "w_up": A[1024:, :UP_C], "w_uq": Q[:384, :UQ_C], "w_ukv": Q[384:],
           "w_o_gdn": C[0:128], "w_o_mla": C[128:256], "w_out": C[256:384], "w_down": C[384:],
           "conv_qkv_w": V[0:GDN_CONV, :QKV_C], "conv_ffn_w": V[8:8 + FFN_CONV, :UP_C]}
    return {k: a[None] for k, a in out.items()}


def _take_cols(pieces, lo, hi):
    out, off = [], 0
    for arr, a, b in pieces:
        s, e = max(lo, off), min(hi, off + b - a)
        if s < e:
            out.append(arr[:, a + s - off:a + e - off])
        off += b - a
    return out[0] if len(out) == 1 else jnp.concatenate(out, axis=1)


def _pack_small(d):
    flat = jnp.concatenate([d[n].reshape(-1).astype(F32) for n, _ in SMALL])
    return jnp.pad(flat, (0, SMALL_ROWS * LANES - flat.shape[0])).reshape(SMALL_ROWS, LANES)


def _unpack_small(buf, shapes):
    flat, out, off = buf.reshape(-1), {}, 0
    for name, n in SMALL:
        out[name] = flat[off:off + n].reshape(shapes[name])
        off += n
    return out


def _rot_cols(w):
    h = ROPE // 2
    return jnp.concatenate([-w[:, h:], w[:, :h]], axis=1)


def _unrot_cols(dw):
    h = ROPE // 2
    return jnp.concatenate([dw[:, h:], -dw[:, :h]], axis=1)


IN_SPLITS = [0, 3072, 4096, 4104, 4112, 4496, 4752, 4816, 5840, 6864]


def _layout_weights(g):
    A, Q, C, V = g["A"], g["Q"], g["C"], g["V"]
    in_pieces = [(A[j, :1024], 0, IN_C) for j in range(N_DEV)]
    o = IN_SPLITS
    take = lambda lo, hi: _take_cols(in_pieces, lo, hi)
    kpe = take(o[6], o[7])
    W = {
        "in_qkv": take(o[0], o[1]),
        "in_ga": take(o[1], o[2]),
        "in_ab": jnp.concatenate([_padc(take(o[2], o[3]), LANES), _padc(take(o[3], o[4]), LANES)], axis=1),
        "in_small": jnp.concatenate([take(o[4], o[6]), _padc(kpe, LANES), _padc(_rot_cols(kpe), LANES)], axis=1),
        "in_gbr": take(o[7], o[9]),
        "w_up": jnp.concatenate([A[j, 1024:, :UP_C] for j in range(N_DEV)], axis=1),
        "uq_n": jnp.concatenate([Q[j, :384, :HD] for j in range(N_DEV)], axis=1),
        "ukv_k": jnp.concatenate([Q[j, 384:, :HD] for j in range(N_DEV)], axis=1),
        "ukv_v": jnp.concatenate([Q[j, 384:, HD:] for j in range(N_DEV)], axis=1),
        "w_o_gdn": C[:, 0:128].reshape(1024, D_MODEL),
        "w_o_mla": C[:, 128:256].reshape(1024, D_MODEL),
        "w_out": C[:, 256:384].reshape(1024, D_MODEL),
        "w_down": C[:, 384:].reshape(D_FF, D_MODEL),
    }
    pe = [Q[j, :384, HD:HD + ROPE] for j in range(N_DEV)]
    W["uq_p"] = jnp.concatenate([_padc(p, HD) for p in pe] + [_padc(_rot_cols(p), HD) for p in pe], axis=1)
    conv_qkv = jnp.concatenate([V[j, 0:GDN_CONV, :QKV_C] for j in range(N_DEV)], axis=1)
    conv_ffn = jnp.concatenate([V[j, 8:8 + FFN_CONV, :UP_C] for j in range(N_DEV)], axis=1)
    return {k: v.astype(BF16) for k, v in W.items()}, conv_qkv, conv_ffn


def _full_grads(dW):
    s = dW["in_small"]
    dkpe = s[:, 640:704] + _unrot_cols(s[:, 768:832])
    in_pieces = [(dW["in_qkv"], 0, 3072), (dW["in_ga"], 0, 1024), (dW["in_ab"], 0, 8), (dW["in_ab"], 128, 136),
                 (s, 0, 640), (dkpe, 0, ROPE), (dW["in_gbr"], 0, 2048)]
    pe = []
    for j in range(N_DEV):
        lin = dW["uq_p"][:, j * HD:j * HD + ROPE]
        rot = dW["uq_p"][:, 1024 + j * HD:1024 + j * HD + ROPE]
        pe.append(lin + _unrot_cols(rot))
    return in_pieces, pe


def _send_slabs(dW, d_conv_qkv, d_conv_ffn):
    in_pieces, pe = _full_grads(dW)
    A, Q, V = [], [], []
    for j in range(N_DEV):
        gin = _padc(_take_cols(in_pieces, j * IN_C, (j + 1) * IN_C), A_W)
        gup = _padc(dW["w_up"][:, j * UP_C:(j + 1) * UP_C], A_W)
        A.append(jnp.concatenate([gin, gup], axis=0))
        guq = _padc(jnp.concatenate([dW["uq_n"][:, j * HD:(j + 1) * HD], pe[j]], axis=1), Q_W)
        gukv = jnp.concatenate([dW["ukv_k"][:, j * HD:(j + 1) * HD], dW["ukv_v"][:, j * HD:(j + 1) * HD]], axis=1)
        Q.append(jnp.concatenate([guq, gukv], axis=0))
        V.append(jnp.concatenate([_padrc(d_conv_qkv[:, j * QKV_C:(j + 1) * QKV_C], 8, V_W),
                                  _padrc(d_conv_ffn[:, j * UP_C:(j + 1) * UP_C], 8, V_W)], axis=0))
    C = jnp.concatenate([dW["w_o_gdn"].reshape(N_DEV, 128, D_MODEL), dW["w_o_mla"].reshape(N_DEV, 128, D_MODEL),
                         dW["w_out"].reshape(N_DEV, 128, D_MODEL), dW["w_down"].reshape(N_DEV, 352, D_MODEL)], axis=1)
    return {"A": jnp.concatenate(A, axis=0).astype(BF16), "Q": jnp.concatenate(Q, axis=0).astype(BF16),
            "C": C.reshape(N_DEV * 736, D_MODEL).astype(BF16), "V": jnp.concatenate(V, axis=0)}


def _rope_tables(S):
    half = ROPE // 2
    inv = ROPE_THETA ** (-jnp.arange(half, dtype=F32) / half)
    ang = jnp.arange(S, dtype=F32)[:, None] * inv[None, :]
    cos = jnp.concatenate([jnp.cos(ang), jnp.cos(ang)], axis=1)
    sin = jnp.concatenate([jnp.sin(ang), jnp.sin(ang)], axis=1)
    return _padc(cos, HD), _padc(sin, HD)


def _local_step(x, tgt, W, conv_qkv_w, conv_ffn_w, small, tm=None, ta=None):
    B, S, _ = x.shape
    T = B * S
    tm = tm or _pick(S, 256, CHUNK)
    ta = ta or _pick(S, 512, LANES)
    x2d, tgt2d = x.reshape(T, D_MODEL), tgt.reshape(T, D_MODEL)
    row = lambda v: v.reshape(1, -1).astype(F32)
    pad_row = lambda v: _padc(row(v), LANES)
    g_mix, g_ffn, g_fin = row(small["norm_mix_g"]), row(small["norm_ffn_g"]), row(small["norm_final_g"])
    g_gdn, g_q, g_kv = row(small["gdn_norm_g"]), row(small["mla_q_norm_g"]), row(small["mla_kv_norm_g"])
    alog, dtb = pad_row(small["gdn_a_log"]), pad_row(small["gdn_dt_bias"])
    cos, sin = _rope_tables(S)
    tps = S // tm
    tab = lambda a: (a, (tm, HD), lambda i: (i % tps, 0))
    col = lambda a, c, w: (a, (tm, w), lambda i, c=c: (i, c))

    h1 = _norm_fwd(x2d, g_mix, T=T, tm=tm, name="norm_mix_fwd")
    z_qkv = _mm(h1, W["in_qkv"], "nn", BF16, name="in_qkv_fwd")
    z_ga = _mm(h1, W["in_ga"], "nn", BF16, name="in_ga_fwd")
    z_ab = _mm(h1, W["in_ab"], "nn", F32, name="in_ab_fwd")
    z_small = _mm(h1, W["in_small"], "nn", F32, name="in_small_fwd", tn=896)
    z_gbr = _mm(h1, W["in_gbr"], "nn", BF16, name="in_gbr_fwd")

    qkvn = _conv_fwd(_qkv_fn, [(z_qkv, 0)], [(conv_qkv_w, 0)], 3072, BF16, T=T, S=S, tm=tm, cb=QKV_CB,
                     ncb=3072 // QKV_CB, name="gdn_qkv_fwd")
    gcum, beta = _row_call(lambda za, zb, al, db: _gate_fn(za, zb, al, db), [col(z_ab, 0, LANES), col(z_ab, 1, LANES)],
                           [alog, dtb], [(LANES, F32), (LANES, F32)], T=T, tm=tm, name="gdn_gate_fwd")
    grT = gcum[:, :HEADS].reshape(T // CHUNK, CHUNK, HEADS).transpose(0, 2, 1)[:, :, None, :]
    o_gdn, states = _gdn_fwd(qkvn, gcum, grT, beta, B=B, S=S)

    def gdn_out_fn(o, ga, g):
        parts = []
        for h in range(HEADS):
            sl = slice(h * HD, (h + 1) * HD)
            parts.append(_rms(o[:, sl], g) * jax.nn.silu(ga[:, sl].astype(F32)))
        return jnp.concatenate(parts, axis=1)

    oa = _row_call(lambda o, ga, g: (gdn_out_fn(o, ga, g),), [o_gdn, z_ga], [g_gdn], [(1024, BF16)], T=T, tm=tm,
                   name="gdn_out_fwd")[0]

    def mla_prep_fn(zq, zkv, zpl, zpr, c, s, gq, gkv):
        return _rms(zq, gq), _rms(zkv, gkv), zpl * c + zpr * s

    small_cols = [(z_small, (tm, Q_RANK), lambda i: (i, 0)), (z_small, (tm, LANES), lambda i: (i, 3)),
                  (z_small, (tm, LANES), lambda i: (i, 4)), (z_small, (tm, LANES), lambda i: (i, 5)),
                  (z_small, (tm, LANES), lambda i: (i, 6))]

    def mla_prep_fwd(zq, zkv0, zkv1, zpl, zpr, c, s, gq, gkv):
        return mla_prep_fn(zq, jnp.concatenate([zkv0, zkv1], axis=1), zpl, zpr, c, s, gq, gkv)

    cq, ckv, kpe = _row_call(mla_prep_fwd, small_cols + [tab(cos), tab(sin)], [g_q, g_kv],
                             [(Q_RANK, BF16), (KV_RANK, BF16), (HD, BF16)], T=T, tm=tm, name="mla_prep_fwd")
    qn = _mm(cq, W["uq_n"], "nn", BF16, name="uq_n_fwd")
    qpl = _mm(cq, W["uq_p"], "nn", F32, name="uq_p_fwd")
    kn = _mm(ckv, W["ukv_k"], "nn", BF16, name="ukv_k_fwd")
    vb = _mm(ckv, W["ukv_v"], "nn", BF16, name="ukv_v_fwd")

    def qrope_fn(lin, rot, c, s):
        return lin * jnp.tile(c, (1, HEADS)) + rot * jnp.tile(s, (1, HEADS))

    qp = _row_call(lambda lin, rot, c, s: (qrope_fn(lin, rot, c, s),), [col(qpl, 0, 1024), col(qpl, 1, 1024), tab(cos), tab(sin)],
                   [], [(1024, BF16)], T=T, tm=tm, name="q_rope_fwd")[0]
    ob, lse = _flash_fwd(qn, qp, kn, kpe, vb, B=B, S=S, t=ta)

    def merge_fn(ya, yb, ga, gb):
        return jax.nn.sigmoid(ga.astype(F32)) * ya + jax.nn.sigmoid(gb.astype(F32)) * yb

    def merge_fwd(oat, obt, ga, gb, wog, wom):
        ya, yb = _dot(oat, wog), _dot(obt, wom)
        return ya, yb, merge_fn(ya, yb, ga, gb)

    ya, yb, merged = _row_call(merge_fwd, [oa, ob, col(z_gbr, 0, 1024), col(z_gbr, 1, 1024)], [W["w_o_gdn"], W["w_o_mla"]],
                               [(1024, BF16), (1024, BF16), (1024, BF16)], T=T, tm=tm, name="merge_fwd")
    x1 = _mm(merged, W["w_out"], "nn", F32, add=x2d, name="w_out_fwd")

    h2 = _norm_fwd(x1, g_ffn, T=T, tm=tm, name="norm_ffn_fwd")
    up = _mm(h2, W["w_up"], "nn", BF16, name="w_up_fwd")
    FCB = 256
    nfb = D_FF // FCB
    f = _conv_fwd(_ffn_fn, [(up, 0), (up, 2)], [(conv_ffn_w, 0), (conv_ffn_w, 2)], D_FF, BF16, T=T, S=S, tm=tm,
                  cb=D_FF // 2, ncb=2, name="ffn_act_fwd")
    x2 = _mm(f, W["w_down"], "nn", F32, add=x1, name="w_down_fwd", tk=1408)

    def final_fn(xt, tt, g):
        def lossf(xv, gv):
            e = _rms(xv, gv) - tt
            return 0.5 * jnp.sum(jnp.mean(e * e, axis=-1))

        l, vjp = jax.vjp(lossf, xt, g)
        dx, dg = vjp(jnp.ones((), F32))
        return dx, jnp.full((1, LANES), l, F32), dg

    dx2, loss_v, dg_fin = _row_call(final_fn, [x2, tgt2d], [g_fin], [(1024, F32)], [((1, LANES), F32), ((1, 1024), F32)],
                                    T=T, tm=tm, name="loss_head")

    dW = {}
    df = _mm(dx2, W["w_down"], "nt", BF16, name="w_down_dx")
    dW["w_down"] = _mm(f, dx2, "tn", F32, name="w_down_dw")
    dug, duu, dcw_g, dcw_u = _conv_bwd(_ffn_fn, [(up, 0), (up, nfb)], [(conv_ffn_w, 0), (conv_ffn_w, nfb)], df, BF16,
                                       T=T, S=S, tm=tm, cb=FCB, ncb=nfb, name="ffn_act_bwd")
    d_conv_ffn = jnp.concatenate([dcw_g, dcw_u], axis=1)
    wup_g, wup_u = W["w_up"][:, :D_FF], W["w_up"][:, D_FF:]
    dh2 = _mm(dug, wup_g, "nt", F32, name="w_up_dx_g")
    dh2 = _mm(duu, wup_u, "nt", F32, add=dh2, name="w_up_dx_u")
    dW["w_up"] = jnp.concatenate([_mm(h2, dug, "tn", F32, name="w_up_dw_g"), _mm(h2, duu, "tn", F32, name="w_up_dw_u")], axis=1)
    dx1, dg_ffn = _norm_bwd(x1, g_ffn, dh2, dx2, T=T, tm=tm, name="norm_ffn_bwd")

    dmerged = _mm(dx1, W["w_out"], "nt", F32, name="w_out_dx")
    dW["w_out"] = _mm(merged, dx1, "tn", F32, name="w_out_dw")

    def merge_bwd(dm, yat, ybt, ga, gb):
        _, vjp = jax.vjp(merge_fn, yat.astype(F32), ybt.astype(F32), ga, gb)
        return vjp(dm)

    dya, dyb, dgbr_a, dgbr_b = _row_call(merge_bwd, [dmerged, ya, yb, col(z_gbr, 0, 1024), col(z_gbr, 1, 1024)], [],
                                         [(1024, BF16)] * 4, T=T, tm=tm, name="merge_bwd")
    doa = _mm(dya, W["w_o_gdn"], "nt", F32, name="w_o_gdn_dx")
    dob = _mm(dyb, W["w_o_mla"], "nt", BF16, name="w_o_mla_dx")
    dW["w_o_gdn"] = _mm(oa, dya, "tn", F32, name="w_o_gdn_dw")
    dW["w_o_mla"] = _mm(ob, dyb, "tn", F32, name="w_o_mla_dw")

    dqn, dqp, dl = _flash_bwd_dq(qn, qp, kn, kpe, vb, ob, dob, lse, B=B, S=S, t=ta)
    dkn, dkp, dvb = _flash_bwd_dkv(qn, qp, kn, kpe, vb, dob, lse.reshape(HEADS, 1, T), dl.reshape(HEADS, 1, T),
                                   B=B, S=S, t=ta)

    def qrope_bwd(d, c, s):
        return d * jnp.tile(c, (1, HEADS)), d * jnp.tile(s, (1, HEADS))

    dq_lin, dq_rot = _row_call(qrope_bwd, [dqp, tab(cos), tab(sin)], [], [(1024, BF16), (1024, BF16)], T=T, tm=tm,
                               name="q_rope_bwd")
    wp_lin, wp_rot = W["uq_p"][:, :1024], W["uq_p"][:, 1024:]
    dcq = _mm(dqn, W["uq_n"], "nt", F32, name="uq_n_dx")
    dcq = _mm(dq_lin, wp_lin, "nt", F32, add=dcq, name="uq_pl_dx")
    dcq = _mm(dq_rot, wp_rot, "nt", F32, add=dcq, name="uq_pr_dx")
    dW["uq_n"] = _mm(cq, dqn, "tn", F32, name="uq_n_dw")
    dW["uq_p"] = jnp.concatenate([_mm(cq, dq_lin, "tn", F32, name="uq_pl_dw"), _mm(cq, dq_rot, "tn", F32, name="uq_pr_dw")], axis=1)
    dckv = _mm(dkn, W["ukv_k"], "nt", F32, name="ukv_k_dx")
    dckv = _mm(dvb, W["ukv_v"], "nt", F32, add=dckv, name="ukv_v_dx")
    dW["ukv_k"] = _mm(ckv, dkn, "tn", F32, name="ukv_k_dw")
    dW["ukv_v"] = _mm(ckv, dvb, "tn", F32, name="ukv_v_dw")

    def mla_prep_bwd(zq, zkv0, zkv1, zpl, zpr, c, s, dcqt, dckvt, dkpt, gq, gkv):
        zkv = jnp.concatenate([zkv0, zkv1], axis=1)
        _, vjp = jax.vjp(lambda a, b, p, r, g1, g2: mla_prep_fn(a, b, p, r, c, s, g1, g2), zq, zkv, zpl, zpr, gq, gkv)
        dk = dkpt[0]
        for h in range(1, HEADS):
            dk = dk + dkpt[h]
        dzq, dzkv, dzpl, dzpr, dgq, dgkv = vjp((dcqt, dckvt, dk))
        return jnp.concatenate([dzq, dzkv, dzpl, dzpr], axis=1), dgq, dgkv

    dz_small, dg_q, dg_kv = _row_call(
        mla_prep_bwd, small_cols + [tab(cos), tab(sin), dcq, dckv, (dkp, (HEADS, tm, HD), lambda i: (0, i, 0))],
        [g_q, g_kv], [(896, BF16)], [((1, Q_RANK), F32), ((1, KV_RANK), F32)], T=T, tm=tm, name="mla_prep_bwd")

    def gdn_out_bwd(o, ga, dot_, g):
        _, vjp = jax.vjp(gdn_out_fn, o, ga, g)
        return vjp(dot_)

    do_gdn, dz_ga, dg_gdn = _row_call(gdn_out_bwd, [o_gdn, z_ga, doa], [g_gdn], [(1024, F32), (1024, BF16)],
                                      [((1, HD), F32)], T=T, tm=tm, name="gdn_out_bwd")
    dqkvn, dgc, dgrT, dbeta = _gdn_bwd(qkvn, gcum, grT, beta, states, do_gdn, B=B, S=S)
    dgc_tot = dgc + _padc(dgrT[:, :, 0, :].transpose(0, 2, 1).reshape(T, HEADS), LANES)

    def gate_bwd(za, zb, dg, db, al, db_):
        _, vjp = jax.vjp(_gate_fn, za, zb, al, db_)
        return vjp((dg, db))

    dz_a, dz_b, d_alog, d_dtb = _row_call(gate_bwd, [col(z_ab, 0, LANES), col(z_ab, 1, LANES), dgc_tot, dbeta], [alog, dtb],
                                          [(LANES, BF16), (LANES, BF16)], [((1, LANES), F32), ((1, LANES), F32)],
                                          T=T, tm=tm, name="gdn_gate_bwd")
    dz_qkv, d_conv_qkv = _conv_bwd(_qkv_fn, [(z_qkv, 0)], [(conv_qkv_w, 0)], dqkvn, BF16, T=T, S=S, tm=tm, cb=QKV_CB,
                                   ncb=3072 // QKV_CB, name="gdn_qkv_bwd")

    dz_ab = jnp.concatenate([dz_a, dz_b], axis=1)
    dz_gbr = jnp.concatenate([dgbr_a, dgbr_b], axis=1)
    dh1 = None
    for key, dz in (("in_qkv", dz_qkv), ("in_ga", dz_ga), ("in_ab", dz_ab), ("in_small", dz_small), ("in_gbr", dz_gbr)):
        dh1 = _mm(dz, W[key], "nt", F32, add=dh1, name=key + "_dx", tk=896 if key == "in_small" else 1024)
        dW[key] = _mm(h1, dz, "tn", F32, name=key + "_dw", tn=896 if key == "in_small" else 1024)
    dx, dg_mix = _norm_bwd(x2d, g_mix, dh1, dx1, T=T, tm=tm, name="norm_mix_bwd")

    dsmall = {"norm_mix_g": dg_mix, "gdn_a_log": d_alog[:, :HEADS], "gdn_dt_bias": d_dtb[:, :HEADS], "gdn_norm_g": dg_gdn,
              "mla_q_norm_g": dg_q, "mla_kv_norm_g": dg_kv, "norm_ffn_g": dg_ffn, "norm_final_g": dg_fin}
    return loss_v[0, 0], dx.reshape(B, S, D_MODEL), dW, d_conv_qkv, d_conv_ffn, dsmall


def kernel(x, norm_mix_g, w_in, conv_qkv_w, gdn_a_log, gdn_dt_bias, gdn_norm_g, mla_q_norm_g, w_uq, mla_kv_norm_g, w_ukv, w_o_gdn, w_o_mla, w_out, norm_ffn_g, w_up, conv_ffn_w, w_down, norm_final_g, loss_target, m_norm_mix_g, m_w_in, m_conv_qkv_w, m_gdn_a_log, m_gdn_dt_bias, m_gdn_norm_g, m_mla_q_norm_g, m_w_uq, m_mla_kv_norm_g, m_w_ukv, m_w_o_gdn, m_w_o_mla, m_w_out, m_norm_ffn_g, m_w_up, m_conv_ffn_w, m_w_down, m_norm_final_g, v_norm_mix_g, v_w_in, v_conv_qkv_w, v_gdn_a_log, v_gdn_dt_bias, v_gdn_norm_g, v_mla_q_norm_g, v_w_uq, v_mla_kv_norm_g, v_w_ukv, v_w_o_gdn, v_w_o_mla, v_w_out, v_norm_ffn_g, v_w_up, v_conv_ffn_w, v_w_down, v_norm_final_g):
    w = dict(norm_mix_g=norm_mix_g, w_in=w_in, conv_qkv_w=conv_qkv_w, gdn_a_log=gdn_a_log, gdn_dt_bias=gdn_dt_bias,
             gdn_norm_g=gdn_norm_g, mla_q_norm_g=mla_q_norm_g, w_uq=w_uq, mla_kv_norm_g=mla_kv_norm_g, w_ukv=w_ukv,
             w_o_gdn=w_o_gdn, w_o_mla=w_o_mla, w_out=w_out, norm_ffn_g=norm_ffn_g, w_up=w_up, conv_ffn_w=conv_ffn_w,
             w_down=w_down, norm_final_g=norm_final_g)
    m = dict(norm_mix_g=m_norm_mix_g, w_in=m_w_in, conv_qkv_w=m_conv_qkv_w, gdn_a_log=m_gdn_a_log, gdn_dt_bias=m_gdn_dt_bias,
             gdn_norm_g=m_gdn_norm_g, mla_q_norm_g=m_mla_q_norm_g, w_uq=m_w_uq, mla_kv_norm_g=m_mla_kv_norm_g, w_ukv=m_w_ukv,
             w_o_gdn=m_w_o_gdn, w_o_mla=m_w_o_mla, w_out=m_w_out, norm_ffn_g=m_norm_ffn_g, w_up=m_w_up,
             conv_ffn_w=m_conv_ffn_w, w_down=m_w_down, norm_final_g=m_norm_final_g)
    v = dict(norm_mix_g=v_norm_mix_g, w_in=v_w_in, conv_qkv_w=v_conv_qkv_w, gdn_a_log=v_gdn_a_log, gdn_dt_bias=v_gdn_dt_bias,
             gdn_norm_g=v_gdn_norm_g, mla_q_norm_g=v_mla_q_norm_g, w_uq=v_w_uq, mla_kv_norm_g=v_mla_kv_norm_g, w_ukv=v_w_ukv,
             w_o_gdn=v_w_o_gdn, w_o_mla=v_w_o_mla, w_out=v_w_out, norm_ffn_g=v_norm_ffn_g, w_up=v_w_up,
             conv_ffn_w=v_conv_ffn_w, w_down=v_w_down, norm_final_g=v_norm_final_g)
    slab_names = ("A", "Q", "C", "V")
    big_names = ("w_in", "w_up", "w_uq", "w_ukv", "w_o_gdn", "w_o_mla", "w_out", "w_down", "conv_qkv_w", "conv_ffn_w")
    small_names = [n for n, _ in SMALL]
    small_shapes = {n: w[n].shape for n in small_names}
    local2d = lambda d: {n: d[n][0] for n in big_names}

    w_slabs = _slabs(local2d(w), F32)
    send = [w_slabs[k].astype(BF16) if k != "V" else w_slabs[k] for k in slab_names]
    gathered = _allgather(send, name="allgather_weights")
    gathered = {k: g.reshape(N_DEV, -1, g.shape[1]) for k, g in zip(slab_names, gathered)}
    W, conv_qkv_full, conv_ffn_full = _layout_weights(gathered)

    loss_local, dx, dW, d_conv_qkv, d_conv_ffn, dsmall = _local_step(
        x, loss_target, W, conv_qkv_full, conv_ffn_full, {n: w[n] for n in small_names})

    g_send = _send_slabs(dW, d_conv_qkv, d_conv_ffn)
    recv = _alltoall_async([g_send[k] for k in slab_names], name="alltoall_grads", collective_id=0)
    m_slabs, v_slabs = _slabs(local2d(m), F32), _slabs(local2d(v), F32)
    upd = {k: _reduce_adam(r, w_slabs[k], m_slabs[k], v_slabs[k], tr=SLAB_TR[k], name="adam_" + k)
           for k, r in zip(slab_names, recv)}
    small_parts = _allgather([_pack_small(dsmall)], name="allgather_small_grads")[0]
    upd_small = _reduce_adam(small_parts, _pack_small({n: w[n] for n in small_names}), _pack_small({n: m[n] for n in small_names}),
                             _pack_small({n: v[n] for n in small_names}), tr=SMALL_ROWS, name="adam_small")

    loss = lax.psum(loss_local, ("x", "y", "c"))
    groups = []
    for i in range(4):
        merged = {**_unslab({k: upd[k][i] for k in slab_names}), **_unpack_small(upd_small[i], small_shapes)}
        groups.append([merged[n] for n in WEIGHT_ORDER])
    return (loss, dx, *groups[0], *groups[1], *groups[2], *groups[3])
```

```python
import functools
import math

import numpy as np
import jax
import jax.numpy as jnp
from jax import lax
from jax.experimental import pallas as pl
from jax.experimental.pallas import tpu as pltpu
from jax.experimental.pallas import tpu_sc as plsc

F32 = jnp.float32
BF16 = jnp.bfloat16

D_MODEL = 1024
HEADS = 8
HD = 128
GDN_CONV = 4
CHUNK = 64
Q_RANK = 384
KV_RANK = 256
ROPE = 64
ROPE_THETA = 10000.0
D_FF = 2816
FFN_CONV = 3
EPS = 1e-6
SM_SCALE = (HD + ROPE) ** -0.5
N_DEV = 8

ADAM_LR, ADAM_B1, ADAM_B2, ADAM_EPS, ADAM_WD, ADAM_STEP = 0.001, 0.9, 0.999, 1e-08, 0.01, 10

LANES = 128
SUBLANES = 8
HALO = 2 * SUBLANES
VMEM_LIMIT = 56 * 1024 * 1024
HI = lax.Precision.HIGHEST
TRI_PRECISION = None

NN = (((1,), (0,)), ((), ()))
NT = (((1,), (1,)), ((), ()))
TN = (((0,), (0,)), ((), ()))


def _dot(a, b, dims=NN, precision=None):
    return lax.dot_general(a, b, dims, precision=precision, preferred_element_type=F32)


def _pick(dim, target, align):
    best = None
    for t in range(align, min(dim, target) + 1, align):
        if dim % t == 0:
            best = t
    return dim if best is None else best


def _call(body, ins, outs, grid, *, name, scratch=(), semantics=None):
    n_in, n_out = len(ins), len(outs)

    def kern(*refs):
        body(refs[:n_in], refs[n_in:n_in + n_out], refs[n_in + n_out:])

    res = pl.pallas_call(
        kern,
        grid=grid,
        in_specs=[pl.BlockSpec(bs, im) for _, bs, im in ins],
        out_specs=[pl.BlockSpec(bs, im) for _, _, bs, im in outs],
        out_shape=[jax.ShapeDtypeStruct(s, d) for s, d, _, _ in outs],
        scratch_shapes=list(scratch),
        name=name,
        compiler_params=pltpu.CompilerParams(
            dimension_semantics=semantics or ("arbitrary",) * len(grid), vmem_limit_bytes=VMEM_LIMIT),
    )(*[a for a, _, _ in ins])
    return res


def _mm(a, b, mode, out_dtype, *, name, add=None, tm=1408, tn=1408, tk=1408):
    if mode == "nn":
        (M, K), (K2, N) = a.shape, b.shape
    elif mode == "nt":
        (M, K), (N, K2) = a.shape, b.shape
    else:
        (K, M), (K2, N) = a.shape, b.shape
    assert K == K2, (a.shape, b.shape, mode)
    tm = _pick(M, tm, LANES if mode == "tn" else 16)
    tn = _pick(N, tn, LANES)
    tk = _pick(K, tk, 16 if mode == "tn" else LANES)
    nk = K // tk
    dims = {"nn": NN, "nt": NT, "tn": TN}[mode]
    if mode == "nn":
        a_spec, b_spec = ((tm, tk), lambda i, j, k: (i, k)), ((tk, tn), lambda i, j, k: (k, j))
    elif mode == "nt":
        a_spec, b_spec = ((tm, tk), lambda i, j, k: (i, k)), ((tn, tk), lambda i, j, k: (j, k))
    else:
        a_spec, b_spec = ((tk, tm), lambda i, j, k: (k, i)), ((tk, tn), lambda i, j, k: (k, j))
    ins = [(a,) + a_spec, (b,) + b_spec]
    if add is not None:
        ins.append((add, (tm, tn), lambda i, j, k: (i, j)))
    outs = [((M, N), out_dtype, (tm, tn), lambda i, j, k: (i, j))]

    def body(in_refs, out_refs, scr):
        prod = _dot(in_refs[0][...].astype(BF16), in_refs[1][...].astype(BF16), dims)

        def finish(r):
            if add is not None:
                r = r + in_refs[2][...].astype(F32)
            out_refs[0][...] = r.astype(out_dtype)

        if nk == 1:
            finish(prod)
            return
        k = pl.program_id(2)
        acc = scr[0]

        @pl.when(k == 0)
        def _():
            acc[...] = prod

        @pl.when(k > 0)
        def _():
            acc[...] += prod

        @pl.when(k == nk - 1)
        def _():
            finish(acc[...])

    return _call(body, ins, outs, (M // tm, N // tn, nk), name=name,
                 scratch=[pltpu.VMEM((tm, tn), F32)] if nk > 1 else [],
                 semantics=("parallel", "parallel", "arbitrary"))[0]


def _row_call(fn, rows, consts, out_rows, out_accs=(), *, T, tm, name):
    nt = T // tm
    ins = []
    for r in rows:
        ins.append(r if isinstance(r, tuple) else (r, (tm, r.shape[1]), lambda i: (i, 0)))
    for c in consts:
        ins.append((c, c.shape, lambda i, nd=c.ndim: (0,) * nd))
    outs = []
    for o in out_rows:
        outs.append(((T, o[0]), o[1], (tm, o[0]), lambda i: (i, 0)) if len(o) == 2 else o)
    for shp, dt in out_accs:
        outs.append((shp, dt, shp, lambda i, nd=len(shp): (0,) * nd))
    n_r = len(out_rows)

    def body(in_refs, out_refs, _):
        i = pl.program_id(0)
        vals = fn(*[r[...] for r in in_refs])
        for o_ref, v in zip(out_refs[:n_r], vals[:n_r]):
            o_ref[...] = v.astype(o_ref.dtype)
        for o_ref, v in zip(out_refs[n_r:], vals[n_r:]):
            @pl.when(i == 0)
            def _(o_ref=o_ref):
                o_ref[...] = jnp.zeros_like(o_ref)

            o_ref[...] += v.astype(o_ref.dtype)

    return _call(body, ins, outs, (nt,), name=name)


def _rms(x, g):
    return x * lax.rsqrt(jnp.mean(x * x, axis=-1, keepdims=True) + EPS) * g


def _norm_fwd(x, g, *, T, tm, name):
    return _row_call(lambda xt, gt: (_rms(xt, gt),), [x], [g], [(x.shape[1], BF16)], T=T, tm=tm, name=name)[0]


def _norm_bwd(x, g, dh, dres, *, T, tm, name):
    def fn(xt, dht, drt, gt):
        _, vjp = jax.vjp(_rms, xt, gt)
        dx, dg = vjp(dht)
        return drt + dx, dg

    return _row_call(fn, [x, dh, dres], [g], [(x.shape[1], F32)], [(g.shape, F32)], T=T, tm=tm, name=name)


def _rows16(c):
    return lax.broadcasted_iota(jnp.int32, (HALO, c), 0)


@functools.lru_cache(maxsize=None)
def _shift_fn(j):
    @jax.custom_vjp
    def shift(x, halo):
        xr = pltpu.roll(x, j, 0)
        top = jnp.where(_rows16(x.shape[1]) < j, pltpu.roll(halo, j, 0), xr[:HALO])
        return jnp.concatenate([top, xr[HALO:]], axis=0)

    def fwd(x, halo):
        return shift(x, halo), None

    def bwd(_, dy):
        tm, c = dy.shape
        keep = _rows16(c) >= HALO - j
        dxr = pltpu.roll(dy, tm - j, 0)
        dx = jnp.concatenate([dxr[:tm - HALO], jnp.where(keep, 0.0, dxr[tm - HALO:])], axis=0)
        dhalo = jnp.where(keep, pltpu.roll(dy[:HALO], HALO - j, 0), 0.0)
        return dx, dhalo

    shift.defvjp(fwd, bwd)
    return shift


def _dwconv(tail, x, w):
    K = w.shape[0]
    acc = w[K - 1:K, :] * x
    for k in range(K - 1):
        acc = acc + w[k:k + 1, :] * _shift_fn(K - 1 - k)(x, tail)
    return acc


STRIP = 64


def _conv_fwd(fn, xs, ws, out_c, out_dtype, *, T, S, tm, cb, ncb, name):
    nt, tps, hb = T // tm, S // tm, tm // HALO
    ins = []
    for arr, off in xs:
        ins.append((arr, (tm, cb), lambda j, i, off=off: (i, off + j)))
        ins.append((arr, (HALO, cb), lambda j, i, off=off: (jnp.maximum(i * hb - 1, 0), off + j)))
    for arr, off in ws:
        ins.append((arr, (arr.shape[0], cb), lambda j, i, off=off: (0, off + j)))
    outs = [((T, out_c), out_dtype, (tm, cb), lambda j, i: (i, j))]
    nx = len(xs)

    def body(in_refs, out_refs, _):
        j, i = pl.program_id(0), pl.program_id(1)
        first = (i % tps) == 0
        wts = [r[...] for r in in_refs[2 * nx:]]
        for r in range(0, tm, STRIP):
            xts = [in_refs[2 * m][r:r + STRIP, :].astype(F32) for m in range(nx)]
            if r == 0:
                tails = [jnp.where(first, 0.0, in_refs[2 * m + 1][...].astype(F32)) for m in range(nx)]
            else:
                tails = [in_refs[2 * m][r - HALO:r, :].astype(F32) for m in range(nx)]
            out_refs[0][r:r + STRIP, :] = fn(j, tails, xts, wts).astype(out_dtype)

    return _call(body, ins, outs, (ncb, nt), name=name)[0]


def _conv_bwd(fn, xs, ws, dout, dx_dtype, *, T, S, tm, cb, ncb, name):
    nt, tps, hb = T // tm, S // tm, tm // HALO
    ins = []
    for arr, off in xs:
        ins.append((arr, (tm, cb), lambda j, i, off=off: (nt - 1 - i, off + j)))
        ins.append((arr, (HALO, cb), lambda j, i, off=off: (jnp.maximum((nt - 1 - i) * hb - 1, 0), off + j)))
    for arr, off in ws:
        ins.append((arr, (arr.shape[0], cb), lambda j, i, off=off: (0, off + j)))
    ins.append((dout, (tm, cb), lambda j, i: (nt - 1 - i, j)))
    nx, nw = len(xs), len(ws)
    outs = [((T, ncb * cb), dx_dtype, (tm, cb), lambda j, i: (nt - 1 - i, j)) for _ in xs]
    outs += [((arr.shape[0], ncb * cb), F32, (arr.shape[0], cb), lambda j, i: (0, j)) for arr, _ in ws]
    scratch = [pltpu.VMEM((HALO, cb), F32) for _ in xs]

    def body(in_refs, out_refs, carry):
        j, i = pl.program_id(0), pl.program_id(1)
        first = ((nt - 1 - i) % tps) == 0
        wts = [ref[...] for ref in in_refs[2 * nx:2 * nx + nw]]

        @pl.when(i == 0)
        def _():
            for c in carry:
                c[...] = jnp.zeros_like(c)

        carried = [c[...] for c in carry]
        dw_sum = None
        for r in reversed(range(0, tm, STRIP)):
            xts = [in_refs[2 * m][r:r + STRIP, :].astype(F32) for m in range(nx)]
            if r == 0:
                tails = [jnp.where(first, 0.0, in_refs[2 * m + 1][...].astype(F32)) for m in range(nx)]
            else:
                tails = [in_refs[2 * m][r - HALO:r, :].astype(F32) for m in range(nx)]
            _, vjp = jax.vjp(lambda tl, xt, wt: fn(j, tl, xt, wt), tails, xts, wts)
            dtails, dxts, dwts = vjp(in_refs[-1][r:r + STRIP, :].astype(F32))
            for m in range(nx):
                pad = jnp.concatenate([jnp.zeros((STRIP - HALO, cb), F32), carried[m]], axis=0)
                out_refs[m][r:r + STRIP, :] = (dxts[m] + pad).astype(dx_dtype)
            carried = [jnp.where(first, 0.0, dt) for dt in dtails] if r == 0 else list(dtails)
            dw_sum = list(dwts) if dw_sum is None else [a + b for a, b in zip(dw_sum, dwts)]
        for m in range(nx):
            carry[m][...] = carried[m]
        for m in range(nw):
            o_ref = out_refs[nx + m]

            @pl.when(i == 0)
            def _(o_ref=o_ref):
                o_ref[...] = jnp.zeros_like(o_ref)

            o_ref[...] += dw_sum[m]

    return _call(body, ins, outs, (ncb, nt), name=name, scratch=scratch)


QKV_CB = 512


def _qkv_fn(j, tails, xts, wts):
    y = jax.nn.silu(_dwconv(tails[0], xts[0], wts[0]))
    scale = jnp.where(j < 1024 // QKV_CB, HD ** -0.5, 1.0)
    parts = []
    for h in range(QKV_CB // HD):
        yh = y[:, h * HD:(h + 1) * HD]
        nh = yh * lax.rsqrt(jnp.sum(yh * yh, axis=-1, keepdims=True) + EPS)
        parts.append(jnp.where(j < 2048 // QKV_CB, nh * scale, yh))
    return jnp.concatenate(parts, axis=1)


def _ffn_fn(j, tails, xts, wts):
    return jax.nn.silu(_dwconv(tails[0], xts[0], wts[0])) * _dwconv(tails[1], xts[1], wts[1])


BNN = (((2,), (1,)), ((0,), (0,)))
BNT = (((2,), (2,)), ((0,), (0,)))
BTN = (((1,), (1,)), ((0,), (0,)))


@jax.custom_vjp
def _tri_inv(L):
    C = L.shape[-1]
    ii = lax.broadcasted_iota(jnp.int32, (C, C), 0)
    jj = lax.broadcasted_iota(jnp.int32, (C, C), 1)
    eye = (ii == jj).astype(F32)
    X = eye - jnp.where((ii >> 1) == (jj >> 1), L, 0.0)
    s = 1
    while (2 << s) <= C:
        E = jnp.where(((ii >> (s + 1)) == (jj >> (s + 1))) & ((ii >> s) != (jj >> s)), L, 0.0)
        X = X - _dot(_dot(X, E, BNN, precision=TRI_PRECISION), X, BNN, precision=TRI_PRECISION)
        s += 1
    return X


def _tri_inv_fwd(L):
    X = _tri_inv(L)
    return X, X


def _tri_inv_bwd(X, dX):
    return (-_dot(_dot(X, dX, BTN, precision=TRI_PRECISION), X, BNT, precision=TRI_PRECISION),)


_tri_inv.defvjp(_tri_inv_fwd, _tri_inv_bwd)


def _gdn_chunk(q, k, v, gc, gr, beta, S):
    C = q.shape[1]
    ii = lax.broadcasted_iota(jnp.int32, (C, C), 0)
    jj = lax.broadcasted_iota(jnp.int32, (C, C), 1)
    lower = ii >= jj
    decay = jnp.where(lower, jnp.exp(jnp.where(lower, gc - gr, 0.0)), 0.0)
    kb, vb = k * beta, v * beta
    L = jnp.where(ii > jj, _dot(kb, k, BNT) * decay, 0.0)
    Tinv = _tri_inv(L)
    eg = jnp.exp(gc)
    u = _dot(Tinv, vb, BNN, precision=TRI_PRECISION)
    w = _dot(Tinv, kb * eg, BNN, precision=TRI_PRECISION)
    a = _dot(q, k, BNT) * decay
    g_last = gc[:, C - 1:C, :]
    kd = k * jnp.exp(g_last - gc)
    v_new = u - _dot(w, S, BNN)
    o = _dot(q * eg, S, BNN) + _dot(a, v_new, BNN)
    S_new = S * jnp.exp(g_last) + _dot(kd, v_new, BTN)
    return o, S_new


def _heads(ref, width=HD):
    return jnp.stack([ref[:, h * width:(h + 1) * width].astype(F32) for h in range(HEADS)])


def _gdn_fwd(qkvn, gcum, grT, beta, *, B, S):
    N, T = S // CHUNK, B * S
    row = lambda c: (lambda b, n: (b * N + n, c))
    ins = [(qkvn, (CHUNK, 1024), row(0)), (qkvn, (CHUNK, 1024), row(1)), (qkvn, (CHUNK, 1024), row(2)),
           (gcum, (CHUNK, LANES), row(0)), (grT, (1, HEADS, 1, CHUNK), lambda b, n: (b * N + n, 0, 0, 0)),
           (beta, (CHUNK, LANES), row(0))]
    outs = [((T, 1024), F32, (CHUNK, 1024), row(0)),
            ((B * N, HEADS, HD, HD), BF16, (1, HEADS, HD, HD), lambda b, n: (b * N + n, 0, 0, 0))]

    def body(in_refs, out_refs, scr):
        q_ref, k_ref, v_ref, gc_ref, gr_ref, b_ref = in_refs
        o_ref, st_ref = out_refs
        S_ref = scr[0]

        @pl.when(pl.program_id(1) == 0)
        def _():
            S_ref[...] = jnp.zeros_like(S_ref)

        S0 = S_ref[...]
        st_ref[0] = S0.astype(BF16)
        o, Sn = _gdn_chunk(_heads(q_ref), _heads(k_ref), _heads(v_ref), _heads(gc_ref, 1), gr_ref[0],
                           _heads(b_ref, 1), S0)
        for h in range(HEADS):
            o_ref[:, h * HD:(h + 1) * HD] = o[h]
        S_ref[...] = Sn

    return _call(body, ins, outs, (B, N), name="gdn_core_fwd", scratch=[pltpu.VMEM((HEADS, HD, HD), F32)])


def _gdn_bwd(qkvn, gcum, grT, beta, states, do, *, B, S):
    N, T = S // CHUNK, B * S
    row = lambda c: (lambda b, n: (b * N + N - 1 - n, c))
    ins = [(qkvn, (CHUNK, 1024), row(0)), (qkvn, (CHUNK, 1024), row(1)), (qkvn, (CHUNK, 1024), row(2)),
           (gcum, (CHUNK, LANES), row(0)), (grT, (1, HEADS, 1, CHUNK), lambda b, n: (b * N + N - 1 - n, 0, 0, 0)),
           (beta, (CHUNK, LANES), row(0)),
           (states, (1, HEADS, HD, HD), lambda b, n: (b * N + N - 1 - n, 0, 0, 0)), (do, (CHUNK, 1024), row(0))]
    outs = [((T, 3072), BF16, (CHUNK, 3072), row(0)), ((T, LANES), F32, (CHUNK, LANES), row(0)),
            ((B * N, HEADS, 1, CHUNK), F32, (1, HEADS, 1, CHUNK), lambda b, n: (b * N + N - 1 - n, 0, 0, 0)),
            ((T, LANES), F32, (CHUNK, LANES), row(0))]

    def body(in_refs, out_refs, scr):
        q_ref, k_ref, v_ref, gc_ref, gr_ref, b_ref, st_ref, do_ref = in_refs
        dqkv_ref, dgc_ref, dgr_ref, db_ref = out_refs
        dS_ref = scr[0]

        @pl.when(pl.program_id(1) == 0)
        def _():
            dS_ref[...] = jnp.zeros_like(dS_ref)

        args = (_heads(q_ref), _heads(k_ref), _heads(v_ref), _heads(gc_ref, 1), gr_ref[0], _heads(b_ref, 1),
                st_ref[0].astype(F32))
        _, vjp = jax.vjp(_gdn_chunk, *args)
        dq, dk, dv, dgc, dgr, db, dS = vjp((_heads(do_ref), dS_ref[...]))
        lane = lax.broadcasted_iota(jnp.int32, (CHUNK, LANES), 1)
        dgc_all = jnp.zeros((CHUNK, LANES), F32)
        db_all = jnp.zeros((CHUNK, LANES), F32)
        for h in range(HEADS):
            dqkv_ref[:, h * HD:(h + 1) * HD] = dq[h].astype(BF16)
            dqkv_ref[:, 1024 + h * HD:1024 + (h + 1) * HD] = dk[h].astype(BF16)
            dqkv_ref[:, 2048 + h * HD:2048 + (h + 1) * HD] = dv[h].astype(BF16)
            dgc_all = jnp.where(lane == h, dgc[h], dgc_all)
            db_all = jnp.where(lane == h, db[h], db_all)
        dgc_ref[...] = dgc_all
        db_ref[...] = db_all
        dgr_ref[0] = dgr
        dS_ref[...] = dS

    return _call(body, ins, outs, (B, N), name="gdn_core_bwd", scratch=[pltpu.VMEM((HEADS, HD, HD), F32)])


def _gate_fn(za, zb, alog, dtb):
    tm = za.shape[0]
    g = -jnp.exp(alog) * jax.nn.softplus(za + dtb)
    ii = lax.broadcasted_iota(jnp.int32, (tm, tm), 0)
    jj = lax.broadcasted_iota(jnp.int32, (tm, tm), 1)
    tri = ((ii >= jj) & ((ii >> 6) == (jj >> 6))).astype(F32)
    return _dot(tri, g, precision=HI), jax.nn.sigmoid(zb)


def _scores(qn_ref, qp_ref, kn_ref, kp_ref, diag):
    q = jnp.concatenate([qn_ref[...], qp_ref[...]], axis=1)
    k = jnp.concatenate([kn_ref[...], kp_ref[...]], axis=1)
    s = _dot(q, k, NT) * SM_SCALE
    if diag:
        t = s.shape[0]
        ii = lax.broadcasted_iota(jnp.int32, (t, t), 0)
        jj = lax.broadcasted_iota(jnp.int32, (t, t), 1)
        s = jnp.where(ii >= jj, s, -jnp.inf)
    return s, q, k


HPB = 4
HW = HPB * HD


def _head_refs(refs, hh):
    return [r.at[:, hh * HD:(hh + 1) * HD] for r in refs]


def _flash_fwd(qn, qp, kn, kp, v, *, B, S, t):
    nb, T = S // t, B * S
    qmap = lambda b, h, qi, ki: (b * nb + qi, h)
    kmap = lambda b, h, qi, ki: (b * nb + jnp.minimum(ki, qi), h)
    kpmap = lambda b, h, qi, ki: (b * nb + jnp.minimum(ki, qi), 0)
    ins = [(qn, (t, HW), qmap), (qp, (t, HW), qmap), (kn, (t, HW), kmap), (kp, (t, HD), kpmap), (v, (t, HW), kmap)]
    outs = [((T, 1024), BF16, (t, HW), qmap),
            ((HEADS, T, 1), F32, (HPB, t, 1), lambda b, h, qi, ki: (h, b * nb + qi, 0))]
    scratch = [pltpu.VMEM((HPB, t, 1), F32), pltpu.VMEM((HPB, t, 2 * HD), F32)]

    def body(in_refs, out_refs, scr):
        qn_ref, qp_ref, kn_ref, kp_ref, v_ref = in_refs
        o_ref, lse_ref = out_refs
        m_ref, acc_ref = scr
        qi, ki = pl.program_id(2), pl.program_id(3)

        @pl.when(ki == 0)
        def _():
            m_ref[...] = jnp.full_like(m_ref, -jnp.inf)
            acc_ref[...] = jnp.zeros_like(acc_ref)

        def step(diag):
            for hh in range(HPB):
                qn_h, qp_h, kn_h, v_h = _head_refs((qn_ref, qp_ref, kn_ref, v_ref), hh)
                s, _, _ = _scores(qn_h, qp_h, kn_h, kp_ref, diag)
                m_old = m_ref[hh]
                m_new = jnp.maximum(m_old, jnp.max(s, axis=-1, keepdims=True))
                p = jnp.exp(s - m_new)
                alpha = jnp.exp(m_old - m_new)
                v1 = jnp.concatenate([v_h[...], jnp.ones((t, HD), BF16)], axis=1)
                acc_ref[hh] = alpha * acc_ref[hh] + _dot(p.astype(BF16), v1)
                m_ref[hh] = m_new

        @pl.when(ki < qi)
        def _():
            step(False)

        @pl.when(ki == qi)
        def _():
            step(True)
            for hh in range(HPB):
                o_ref[:, hh * HD:(hh + 1) * HD] = (acc_ref[hh, :, :HD] / acc_ref[hh, :, HD:]).astype(BF16)
                lse_ref[hh] = m_ref[hh] + jnp.log(acc_ref[hh, :, HD:HD + 1])

    return _call(body, ins, outs, (B, HEADS // HPB, nb, nb), name="mla_flash_fwd", scratch=scratch,
                 semantics=("parallel", "parallel", "parallel", "arbitrary"))


def _flash_bwd_dq(qn, qp, kn, kp, v, o, do, lse, *, B, S, t):
    nb, T = S // t, B * S
    qmap = lambda b, h, qi, ki: (b * nb + qi, h)
    kmap = lambda b, h, qi, ki: (b * nb + jnp.minimum(ki, qi), h)
    kpmap = lambda b, h, qi, ki: (b * nb + jnp.minimum(ki, qi), 0)
    ins = [(qn, (t, HW), qmap), (qp, (t, HW), qmap), (kn, (t, HW), kmap), (kp, (t, HD), kpmap), (v, (t, HW), kmap),
           (o, (t, HW), qmap), (do, (t, HW), qmap), (lse, (HPB, t, 1), lambda b, h, qi, ki: (h, b * nb + qi, 0))]
    outs = [((T, 1024), BF16, (t, HW), qmap), ((T, 1024), F32, (t, HW), qmap),
            ((HEADS, T, 1), F32, (HPB, t, 1), lambda b, h, qi, ki: (h, b * nb + qi, 0))]
    scratch = [pltpu.VMEM((HPB, t, 1), F32), pltpu.VMEM((HPB, t, 2 * HD), F32)]

    def body(in_refs, out_refs, scr):
        qn_ref, qp_ref, kn_ref, kp_ref, v_ref, o_ref, do_ref, lse_ref = in_refs
        dqn_ref, dqp_ref, dlo_ref = out_refs
        dl_ref, acc_ref = scr
        qi, ki = pl.program_id(2), pl.program_id(3)

        @pl.when(ki == 0)
        def _():
            for hh in range(HPB):
                o_h, do_h = _head_refs((o_ref, do_ref), hh)
                dl_ref[hh] = jnp.sum(do_h[...].astype(F32) * o_h[...].astype(F32), axis=-1, keepdims=True)
            acc_ref[...] = jnp.zeros_like(acc_ref)

        def step(diag):
            for hh in range(HPB):
                qn_h, qp_h, kn_h, v_h, do_h = _head_refs((qn_ref, qp_ref, kn_ref, v_ref, do_ref), hh)
                s, _, k = _scores(qn_h, qp_h, kn_h, kp_ref, diag)
                p = jnp.exp(s - lse_ref[hh])
                dp = _dot(do_h[...], v_h[...], NT)
                ds = p * (dp - dl_ref[hh]) * SM_SCALE
                acc_ref[hh] += _dot(ds.astype(BF16), k)

        @pl.when(ki < qi)
        def _():
            step(False)

        @pl.when(ki == qi)
        def _():
            step(True)
            for hh in range(HPB):
                dqn_ref[:, hh * HD:(hh + 1) * HD] = acc_ref[hh, :, :HD].astype(BF16)
                dqp_ref[:, hh * HD:(hh + 1) * HD] = acc_ref[hh, :, HD:]
            dlo_ref[...] = dl_ref[...]

    return _call(body, ins, outs, (B, HEADS // HPB, nb, nb), name="mla_flash_bwd_dq", scratch=scratch,
                 semantics=("parallel", "parallel", "parallel", "arbitrary"))


def _flash_bwd_dkv(qn, qp, kn, kp, v, do, lse_t, dl_t, *, B, S, t):
    nb, T = S // t, B * S
    qmap = lambda b, h, ki, qi: (b * nb + jnp.maximum(qi, ki), h)
    kmap = lambda b, h, ki, qi: (b * nb + ki, h)
    tmap = lambda b, h, ki, qi: (h, 0, b * nb + jnp.maximum(qi, ki))
    ins = [(qn, (t, HW), qmap), (qp, (t, HW), qmap), (kn, (t, HW), kmap),
           (kp, (t, HD), lambda b, h, ki, qi: (b * nb + ki, 0)), (v, (t, HW), kmap), (do, (t, HW), qmap),
           (lse_t, (HPB, 1, t), tmap), (dl_t, (HPB, 1, t), tmap)]
    outs = [((T, 1024), BF16, (t, HW), kmap), ((HEADS, T, HD), F32, (HPB, t, HD), lambda b, h, ki, qi: (h, b * nb + ki, 0)),
            ((T, 1024), BF16, (t, HW), kmap)]
    scratch = [pltpu.VMEM((HPB, t, 2 * HD), F32), pltpu.VMEM((HPB, t, HD), F32)]

    def body(in_refs, out_refs, scr):
        qn_ref, qp_ref, kn_ref, kp_ref, v_ref, do_ref, lse_ref, dl_ref = in_refs
        dkn_ref, dkp_ref, dv_ref = out_refs
        dk_acc, dv_acc = scr
        ki, qi = pl.program_id(2), pl.program_id(3)

        @pl.when(qi == 0)
        def _():
            dk_acc[...] = jnp.zeros_like(dk_acc)
            dv_acc[...] = jnp.zeros_like(dv_acc)

        def step(diag):
            for hh in range(HPB):
                qn_h, qp_h, kn_h, v_h, do_h = _head_refs((qn_ref, qp_ref, kn_ref, v_ref, do_ref), hh)
                q = jnp.concatenate([qn_h[...], qp_h[...]], axis=1)
                k = jnp.concatenate([kn_h[...], kp_ref[...]], axis=1)
                st = _dot(k, q, NT) * SM_SCALE
                if diag:
                    ii = lax.broadcasted_iota(jnp.int32, (t, t), 0)
                    jj = lax.broadcasted_iota(jnp.int32, (t, t), 1)
                    st = jnp.where(ii <= jj, st, -jnp.inf)
                do_t = do_h[...]
                pt = jnp.exp(st - lse_ref[hh])
                dst = pt * (_dot(v_h[...], do_t, NT) - dl_ref[hh]) * SM_SCALE
                dv_acc[hh] += _dot(pt.astype(BF16), do_t)
                dk_acc[hh] += _dot(dst.astype(BF16), q)

        @pl.when(qi > ki)
        def _():
            step(False)

        @pl.when(qi == ki)
        def _():
            step(True)

        @pl.when(qi == nb - 1)
        def _():
            for hh in range(HPB):
                dkn_ref[:, hh * HD:(hh + 1) * HD] = dk_acc[hh, :, :HD].astype(BF16)
                dkp_ref[hh] = dk_acc[hh, :, HD:]
                dv_ref[:, hh * HD:(hh + 1) * HD] = dv_acc[hh].astype(BF16)

    return _call(body, ins, outs, (B, HEADS // HPB, nb, nb), name="mla_flash_bwd_dkv", scratch=scratch,
                 semantics=("parallel", "parallel", "parallel", "arbitrary"))


def _allgather(shards, *, name):
    n_arr = len(shards)

    def body(*refs):
        x_refs, out_refs = refs[:n_arr], refs[n_arr:2 * n_arr]
        send_sems, recv_sems, local_sems = refs[2 * n_arr:]
        x, y, c = lax.axis_index("x"), lax.axis_index("y"), lax.axis_index("c")
        me, sibling = (x, y, c), (x, y, 1 - c)
        chips = [(1 - x, y), (x, 1 - y), (1 - x, 1 - y)]

        def rows(a, px, py, pc):
            m_per = shards[a].shape[0]
            return out_refs[a].at[pl.ds((4 * px + 2 * py + pc) * m_per, m_per), :]

        def copy(a, k, block, to, src=None):
            return pltpu.make_async_remote_copy(
                src_ref=rows(a, *block) if src is None else src, dst_ref=rows(a, *block),
                send_sem=send_sems.at[a, k], recv_sem=recv_sems.at[a, k], device_id=to,
                device_id_type=pl.DeviceIdType.MESH)

        mine = [pltpu.make_async_copy(x_refs[a], rows(a, *me), local_sems.at[a]) for a in range(n_arr)]
        for cp in mine:
            cp.start()
        first = []
        for a in range(n_arr):
            first.append(copy(a, 0, me, sibling, src=x_refs[a]))
            first += [copy(a, 1 + j, me, (*chip, c), src=x_refs[a]) for j, chip in enumerate(chips)]
        for cp in first:
            cp.start()
        passed = []
        for j, chip in enumerate(chips):
            for a in range(n_arr):
                copy(a, 1 + j, (*chip, c), me).wait_recv()
                cp = copy(a, 4 + j, (*chip, c), sibling)
                cp.start()
                passed.append(cp)
        for a in range(n_arr):
            copy(a, 0, sibling, me).wait_recv()
        for j, chip in enumerate(chips):
            for a in range(n_arr):
                copy(a, 4 + j, (*chip, 1 - c), me).wait_recv()
        for cp in first + passed:
            cp.wait_send()
        for cp in mine:
            cp.wait()

    return pl.pallas_call(
        body,
        out_shape=[jax.ShapeDtypeStruct((N_DEV * s.shape[0], s.shape[1]), s.dtype) for s in shards],
        in_specs=[pl.BlockSpec(memory_space=pl.ANY)] * n_arr,
        out_specs=[pl.BlockSpec(memory_space=pl.ANY)] * n_arr,
        scratch_shapes=[pltpu.SemaphoreType.DMA((n_arr, 7)), pltpu.SemaphoreType.DMA((n_arr, 7)),
                        pltpu.SemaphoreType.DMA((n_arr,))],
        name=name,
    )(*shards)


def _alltoall(sends, *, name):
    n_arr = len(sends)

    def body(*refs):
        s_refs, r_refs = refs[:n_arr], refs[n_arr:2 * n_arr]
        send_sems, recv_sems, local_sems = refs[2 * n_arr:]
        x, y, c = lax.axis_index("x"), lax.axis_index("y"), lax.axis_index("c")
        me = 4 * x + 2 * y + c

        def rows(ref, a, idx):
            m_per = sends[a].shape[0] // N_DEV
            return ref.at[pl.ds(idx * m_per, m_per), :]

        local = [pltpu.make_async_copy(rows(s_refs[a], a, me), rows(r_refs[a], a, me), local_sems.at[a])
                 for a in range(n_arr)]
        for cp in local:
            cp.start()
        copies = []
        for k in range(1, N_DEV):
            px = 1 - x if k & 4 else x
            py = 1 - y if k & 2 else y
            pc = 1 - c if k & 1 else c
            for a in range(n_arr):
                cp = pltpu.make_async_remote_copy(
                    src_ref=rows(s_refs[a], a, 4 * px + 2 * py + pc), dst_ref=rows(r_refs[a], a, me),
                    send_sem=send_sems.at[a, k - 1], recv_sem=recv_sems.at[a, k - 1],
                    device_id=(px, py, pc), device_id_type=pl.DeviceIdType.MESH)
                cp.start()
                copies.append(cp)
        for cp in copies:
            cp.wait()
        for cp in local:
            cp.wait()

    return pl.pallas_call(
        body,
        out_shape=[jax.ShapeDtypeStruct(s.shape, s.dtype) for s in sends],
        in_specs=[pl.BlockSpec(memory_space=pl.ANY)] * n_arr,
        out_specs=[pl.BlockSpec(memory_space=pl.ANY)] * n_arr,
        scratch_shapes=[pltpu.SemaphoreType.DMA((n_arr, 7)), pltpu.SemaphoreType.DMA((n_arr, 7)),
                        pltpu.SemaphoreType.DMA((n_arr,))],
        name=name,
    )(*sends)


def _allgather_async(shards, *, name, collective_id):
    n_arr = len(shards)
    hbm = pltpu.MemorySpace.HBM
    x_refs = [jax.new_ref(a, memory_space=hbm) for a in shards]
    out_refs = [jax.empty_ref(jax.ShapeDtypeStruct((N_DEV * a.shape[0], a.shape[1]), a.dtype), memory_space=hbm)
                for a in shards]

    @pl.kernel(mesh=plsc.ScalarSubcoreMesh(axis_name="seq", num_cores=1), name=name,
               scratch_types=(pltpu.SemaphoreType.DMA((n_arr, 7)), pltpu.SemaphoreType.DMA((n_arr, 7)),
                              pltpu.SemaphoreType.DMA((n_arr,))),
               compiler_params=pltpu.CompilerParams(collective_id=collective_id))
    def launch(send_sems, recv_sems, local_sems):
        x, y, c = lax.axis_index("x"), lax.axis_index("y"), lax.axis_index("c")
        me, sibling = (x, y, c), (x, y, 1 - c)
        chips = [(1 - x, y), (x, 1 - y), (1 - x, 1 - y)]
        barrier = pltpu.get_barrier_semaphore()
        for p in [sibling] + [(*chip, c) for chip in chips]:
            pl.semaphore_signal(barrier, inc=1, device_id=p, device_id_type=pl.DeviceIdType.MESH)
        pl.semaphore_wait(barrier, 4)

        def rows(a, px, py, pc):
            m_per = shards[a].shape[0]
            return out_refs[a].at[pl.ds((4 * px + 2 * py + pc) * m_per, m_per), :]

        def copy(a, k, block, to, src=None):
            return pltpu.make_async_remote_copy(
                src_ref=rows(a, *block) if src is None else src, dst_ref=rows(a, *block),
                send_sem=send_sems.at[a, k], recv_sem=recv_sems.at[a, k], device_id=to,
                device_id_type=pl.DeviceIdType.MESH)

        mine = [pltpu.make_async_copy(x_refs[a], rows(a, *me), local_sems.at[a]) for a in range(n_arr)]
        for cp in mine:
            cp.start()
        first = []
        for a in range(n_arr):
            first.append(copy(a, 0, me, sibling, src=x_refs[a]))
            first += [copy(a, 1 + j, me, (*chip, c), src=x_refs[a]) for j, chip in enumerate(chips)]
        for cp in first:
            cp.start()
        passed = []
        for j, chip in enumerate(chips):
            for a in range(n_arr):
                copy(a, 1 + j, (*chip, c), me).wait_recv()
                cp = copy(a, 4 + j, (*chip, c), sibling)
                cp.start()
                passed.append(cp)
        for a in range(n_arr):
            copy(a, 0, sibling, me).wait_recv()
        for j, chip in enumerate(chips):
            for a in range(n_arr):
                copy(a, 4 + j, (*chip, 1 - c), me).wait_recv()
        for cp in first + passed:
            cp.wait_send()
        for cp in mine:
            cp.wait()

    launch()
    return [r[...] for r in out_refs]


def _alltoall_async(sends, *, name, collective_id):
    n_arr = len(sends)
    hbm = pltpu.MemorySpace.HBM
    s_refs = [jax.new_ref(a, memory_space=hbm) for a in sends]
    r_refs = [jax.empty_ref(jax.ShapeDtypeStruct(a.shape, a.dtype), memory_space=hbm) for a in sends]

    @pl.kernel(mesh=plsc.ScalarSubcoreMesh(axis_name="seq", num_cores=1), name=name,
               scratch_types=(pltpu.SemaphoreType.DMA((n_arr, 7)), pltpu.SemaphoreType.DMA((n_arr, 7)),
                              pltpu.SemaphoreType.DMA((n_arr,))),
               compiler_params=pltpu.CompilerParams(collective_id=collective_id))
    def launch(send_sems, recv_sems, local_sems):
        x, y, c = lax.axis_index("x"), lax.axis_index("y"), lax.axis_index("c")
        me = 4 * x + 2 * y + c
        peers = [(1 - x if k & 4 else x, 1 - y if k & 2 else y, 1 - c if k & 1 else c) for k in range(1, N_DEV)]
        barrier = pltpu.get_barrier_semaphore()
        for p in peers:
            pl.semaphore_signal(barrier, inc=1, device_id=p, device_id_type=pl.DeviceIdType.MESH)
        pl.semaphore_wait(barrier, N_DEV - 1)

        def rows(ref, a, idx):
            m_per = sends[a].shape[0] // N_DEV
            return ref.at[pl.ds(idx * m_per, m_per), :]

        local = [pltpu.make_async_copy(rows(s_refs[a], a, me), rows(r_refs[a], a, me), local_sems.at[a])
                 for a in range(n_arr)]
        for cp in local:
            cp.start()
        copies = []
        for k, (px, py, pc) in enumerate(peers):
            for a in range(n_arr):
                cp = pltpu.make_async_remote_copy(
                    src_ref=rows(s_refs[a], a, 4 * px + 2 * py + pc), dst_ref=rows(r_refs[a], a, me),
                    send_sem=send_sems.at[a, k], recv_sem=recv_sems.at[a, k],
                    device_id=(px, py, pc), device_id_type=pl.DeviceIdType.MESH)
                cp.start()
                copies.append(cp)
        for cp in copies:
            cp.wait()
        for cp in local:
            cp.wait()

    launch()
    return [r[...] for r in r_refs]


def _reduce_adam(parts, w, m, v, *, tr, name):
    R, C = w.shape
    nR = R // tr
    ins = [(parts, (tr, C), lambda i, s=s: (s * nR + i, 0)) for s in range(N_DEV)]
    ins += [(a, (tr, C), lambda i: (i, 0)) for a in (w, m, v)]
    outs = [((R, C), F32, (tr, C), lambda i: (i, 0)) for _ in range(4)]
    c1 = 1.0 - ADAM_B1 ** ADAM_STEP
    c2 = 1.0 - ADAM_B2 ** ADAM_STEP

    def body(in_refs, out_refs, _):
        g = in_refs[0][...].astype(F32)
        for s in range(1, N_DEV):
            g = g + in_refs[s][...].astype(F32)
        wv, mv, vv = in_refs[8][...], in_refs[9][...], in_refs[10][...]
        mn = ADAM_B1 * mv + (1.0 - ADAM_B1) * g
        vn = ADAM_B2 * vv + (1.0 - ADAM_B2) * (g * g)
        delta = -ADAM_LR * ((mn / c1) / (jnp.sqrt(vn / c2) + ADAM_EPS) + ADAM_WD * wv)
        out_refs[0][...] = g
        out_refs[1][...] = delta
        out_refs[2][...] = mn
        out_refs[3][...] = vn

    return _call(body, ins, outs, (nR,), name=name, semantics=("parallel",))


IN_C, UP_C, UQ_C, QKV_C = 858, 704, 192, 384
A_W, Q_W, V_W = 896, 256, 768
SLAB_TR = {"A": 256, "Q": 128, "C": 368, "V": 16}
SMALL = [("norm_mix_g", 1024), ("gdn_a_log", 8), ("gdn_dt_bias", 8), ("gdn_norm_g", 128), ("mla_q_norm_g", 384),
         ("mla_kv_norm_g", 256), ("norm_ffn_g", 1024), ("norm_final_g", 1024)]
SMALL_ROWS = 32
WEIGHT_ORDER = ["norm_mix_g", "w_in", "conv_qkv_w", "gdn_a_log", "gdn_dt_bias", "gdn_norm_g", "mla_q_norm_g", "w_uq",
                "mla_kv_norm_g", "w_ukv", "w_o_gdn", "w_o_mla", "w_out", "norm_ffn_g", "w_up", "conv_ffn_w", "w_down",
                "norm_final_g"]


def _padc(w, n):
    return jnp.pad(w, ((0, 0), (0, n - w.shape[1])))


def _padrc(w, r, n):
    return jnp.pad(w, ((0, r - w.shape[0]), (0, n - w.shape[1])))


def _slabs(p, dtype):
    A = jnp.concatenate([_padc(p["w_in"], A_W), _padc(p["w_up"], A_W)], axis=0).astype(dtype)
    Q = jnp.concatenate([_padc(p["w_uq"], Q_W), p["w_ukv"]], axis=0).astype(dtype)
    C = jnp.concatenate([p["w_o_gdn"], p["w_o_mla"], p["w_out"], p["w_down"]], axis=0).astype(dtype)
    V = jnp.concatenate([_padrc(p["conv_qkv_w"], 8, V_W), _padrc(p["conv_ffn_w"], 8, V_W)], axis=0).astype(F32)
    return {"A": A, "Q": Q, "C": C, "V": V}


def _unslab(sl):
    A, Q, C, V = sl["A"], sl["Q"], sl["C"], sl["V"]
    out = {"w_in": A[:1024, :IN_C], "w_up": A[1024:, :UP_C], "w_uq": Q[:384, :UQ_C], "w_ukv": Q[384:],
           "w_o_gdn": C[0:128], "w_o_mla": C[128:256], "w_out": C[256:384], "w_down": C[384:],
           "conv_qkv_w": V[0:GDN_CONV, :QKV_C], "conv_ffn_w": V[8:8 + FFN_CONV, :UP_C]}
    return {k: a[None] for k, a in out.items()}


def _take_cols(pieces, lo, hi):
    out, off = [], 0
    for arr, a, b in pieces:
        s, e = max(lo, off), min(hi, off + b - a)
        if s < e:
            out.append(arr[:, a + s - off:a + e - off])
        off += b - a
    return out[0] if len(out) == 1 else jnp.concatenate(out, axis=1)


def _pack_small(d):
    flat = jnp.concatenate([d[n].reshape(-1).astype(F32) for n, _ in SMALL])
    return jnp.pad(flat, (0, SMALL_ROWS * LANES - flat.shape[0])).reshape(SMALL_ROWS, LANES)


def _unpack_small(buf, shapes):
    flat, out, off = buf.reshape(-1), {}, 0
    for name, n in SMALL:
        out[name] = flat[off:off + n].reshape(shapes[name])
        off += n
    return out


def _rot_cols(w):
    h = ROPE // 2
    return jnp.concatenate([-w[:, h:], w[:, :h]], axis=1)


def _unrot_cols(dw):
    h = ROPE // 2
    return jnp.concatenate([dw[:, h:], -dw[:, :h]], axis=1)


IN_SPLITS = [0, 3072, 4096, 4104, 4112, 4496, 4752, 4816, 5840, 6864]


def _layout_weights(g):
    A_in, A_up, Q, C, V = g["A_in"], g["A_up"], g["Q"], g["C"], g["V"]
    in_pieces = [(A_in[j], 0, IN_C) for j in range(N_DEV)]
    o = IN_SPLITS
    take = lambda lo, hi: _take_cols(in_pieces, lo, hi)
    kpe = take(o[6], o[7])
    W = {
        "in_qkv": take(o[0], o[1]),
        "in_ga": take(o[1], o[2]),
        "in_ab": jnp.concatenate([_padc(take(o[2], o[3]), LANES), _padc(take(o[3], o[4]), LANES)], axis=1),
        "in_small": jnp.concatenate([take(o[4], o[6]), _padc(kpe, LANES), _padc(_rot_cols(kpe), LANES)], axis=1),
        "in_gbr": take(o[7], o[9]),
        "w_up": jnp.concatenate([A_up[j, :, :UP_C] for j in range(N_DEV)], axis=1),
        "uq_n": jnp.concatenate([Q[j, :384, :HD] for j in range(N_DEV)], axis=1),
        "ukv_k": jnp.concatenate([Q[j, 384:, :HD] for j in range(N_DEV)], axis=1),
        "ukv_v": jnp.concatenate([Q[j, 384:, HD:] for j in range(N_DEV)], axis=1),
        "w_o_gdn": C[:, 0:128].reshape(1024, D_MODEL),
        "w_o_mla": C[:, 128:256].reshape(1024, D_MODEL),
        "w_out": C[:, 256:384].reshape(1024, D_MODEL),
        "w_down": C[:, 384:].reshape(D_FF, D_MODEL),
    }
    pe = [Q[j, :384, HD:HD + ROPE] for j in range(N_DEV)]
    W["uq_p"] = jnp.concatenate([_padc(p, HD) for p in pe] + [_padc(_rot_cols(p), HD) for p in pe], axis=1)
    conv_qkv = jnp.concatenate([V[j, 0:GDN_CONV, :QKV_C] for j in range(N_DEV)], axis=1)
    conv_ffn = jnp.concatenate([V[j, 8:8 + FFN_CONV, :UP_C] for j in range(N_DEV)], axis=1)
    return {k: v.astype(BF16) for k, v in W.items()}, conv_qkv, conv_ffn


def _full_grads(dW):
    s = dW["in_small"]
    dkpe = s[:, 640:704] + _unrot_cols(s[:, 768:832])
    in_pieces = [(dW["in_qkv"], 0, 3072), (dW["in_ga"], 0, 1024), (dW["in_ab"], 0, 8), (dW["in_ab"], 128, 136),
                 (s, 0, 640), (dkpe, 0, ROPE), (dW["in_gbr"], 0, 2048)]
    pe = []
    for j in range(N_DEV):
        lin = dW["uq_p"][:, j * HD:j * HD + ROPE]
        rot = dW["uq_p"][:, 1024 + j * HD:1024 + j * HD + ROPE]
        pe.append(lin + _unrot_cols(rot))
    return in_pieces, pe


def _send_slabs(dW, d_conv_qkv, d_conv_ffn):
    in_pieces, pe = _full_grads(dW)
    A, Q, V = [], [], []
    for j in range(N_DEV):
        gin = _padc(_take_cols(in_pieces, j * IN_C, (j + 1) * IN_C), A_W)
        gup = _padc(dW["w_up"][:, j * UP_C:(j + 1) * UP_C], A_W)
        A.append(jnp.concatenate([gin, gup], axis=0))
        guq = _padc(jnp.concatenate([dW["uq_n"][:, j * HD:(j + 1) * HD], pe[j]], axis=1), Q_W)
        gukv = jnp.concatenate([dW["ukv_k"][:, j * HD:(j + 1) * HD], dW["ukv_v"][:, j * HD:(j + 1) * HD]], axis=1)
        Q.append(jnp.concatenate([guq, gukv], axis=0))
        V.append(jnp.concatenate([_padrc(d_conv_qkv[:, j * QKV_C:(j + 1) * QKV_C], 8, V_W),
                                  _padrc(d_conv_ffn[:, j * UP_C:(j + 1) * UP_C], 8, V_W)], axis=0))
    C = jnp.concatenate([dW["w_o_gdn"].reshape(N_DEV, 128, D_MODEL), dW["w_o_mla"].reshape(N_DEV, 128, D_MODEL),
                         dW["w_out"].reshape(N_DEV, 128, D_MODEL), dW["w_down"].reshape(N_DEV, 352, D_MODEL)], axis=1)
    return {"A": jnp.concatenate(A, axis=0).astype(BF16), "Q": jnp.concatenate(Q, axis=0).astype(BF16),
            "C": C.reshape(N_DEV * 736, D_MODEL).astype(BF16), "V": jnp.concatenate(V, axis=0)}


def _rope_tables(S):
    half = ROPE // 2
    inv = ROPE_THETA ** (-jnp.arange(half, dtype=F32) / half)
    ang = jnp.arange(S, dtype=F32)[:, None] * inv[None, :]
    cos = jnp.concatenate([jnp.cos(ang), jnp.cos(ang)], axis=1)
    sin = jnp.concatenate([jnp.sin(ang), jnp.sin(ang)], axis=1)
    return _padc(cos, HD), _padc(sin, HD)


def _local_step(x, tgt, W, conv_qkv_w, conv_ffn_w, small, tm=None, ta=None):
    B, S, _ = x.shape
    T = B * S
    tm = tm or _pick(S, 256, CHUNK)
    ta = ta or _pick(S, 512, LANES)
    x2d, tgt2d = x.reshape(T, D_MODEL), tgt.reshape(T, D_MODEL)
    row = lambda v: v.reshape(1, -1).astype(F32)
    pad_row = lambda v: _padc(row(v), LANES)
    g_mix, g_ffn, g_fin = row(small["norm_mix_g"]), row(small["norm_ffn_g"]), row(small["norm_final_g"])
    g_gdn, g_q, g_kv = row(small["gdn_norm_g"]), row(small["mla_q_norm_g"]), row(small["mla_kv_norm_g"])
    alog, dtb = pad_row(small["gdn_a_log"]), pad_row(small["gdn_dt_bias"])
    cos, sin = _rope_tables(S)
    tps = S // tm
    tab = lambda a: (a, (tm, HD), lambda i: (i % tps, 0))
    col = lambda a, c, w: (a, (tm, w), lambda i, c=c: (i, c))

    h1 = _norm_fwd(x2d, g_mix, T=T, tm=tm, name="norm_mix_fwd")
    z_qkv = _mm(h1, W["in_qkv"], "nn", BF16, name="in_qkv_fwd")
    z_ga = _mm(h1, W["in_ga"], "nn", BF16, name="in_ga_fwd")
    z_ab = _mm(h1, W["in_ab"], "nn", F32, name="in_ab_fwd")
    z_small = _mm(h1, W["in_small"], "nn", F32, name="in_small_fwd", tn=896)
    z_gbr = _mm(h1, W["in_gbr"], "nn", BF16, name="in_gbr_fwd")

    qkvn = _conv_fwd(_qkv_fn, [(z_qkv, 0)], [(conv_qkv_w, 0)], 3072, BF16, T=T, S=S, tm=tm, cb=QKV_CB,
                     ncb=3072 // QKV_CB, name="gdn_qkv_fwd")
    gcum, beta = _row_call(lambda za, zb, al, db: _gate_fn(za, zb, al, db), [col(z_ab, 0, LANES), col(z_ab, 1, LANES)],
                           [alog, dtb], [(LANES, F32), (LANES, F32)], T=T, tm=tm, name="gdn_gate_fwd")
    grT = gcum[:, :HEADS].reshape(T // CHUNK, CHUNK, HEADS).transpose(0, 2, 1)[:, :, None, :]
    o_gdn, states = _gdn_fwd(qkvn, gcum, grT, beta, B=B, S=S)

    def gdn_out_fn(o, ga, g):
        parts = []
        for h in range(HEADS):
            sl = slice(h * HD, (h + 1) * HD)
            parts.append(_rms(o[:, sl], g) * jax.nn.silu(ga[:, sl].astype(F32)))
        return jnp.concatenate(parts, axis=1)

    oa = _row_call(lambda o, ga, g: (gdn_out_fn(o, ga, g),), [o_gdn, z_ga], [g_gdn], [(1024, BF16)], T=T, tm=tm,
                   name="gdn_out_fwd")[0]

    def mla_prep_fn(zq, zkv, zpl, zpr, c, s, gq, gkv):
        return _rms(zq, gq), _rms(zkv, gkv), zpl * c + zpr * s

    small_cols = [(z_small, (tm, Q_RANK), lambda i: (i, 0)), (z_small, (tm, LANES), lambda i: (i, 3)),
                  (z_small, (tm, LANES), lambda i: (i, 4)), (z_small, (tm, LANES), lambda i: (i, 5)),
                  (z_small, (tm, LANES), lambda i: (i, 6))]

    def mla_prep_fwd(zq, zkv0, zkv1, zpl, zpr, c, s, gq, gkv):
        return mla_prep_fn(zq, jnp.concatenate([zkv0, zkv1], axis=1), zpl, zpr, c, s, gq, gkv)

    cq, ckv, kpe = _row_call(mla_prep_fwd, small_cols + [tab(cos), tab(sin)], [g_q, g_kv],
                             [(Q_RANK, BF16), (KV_RANK, BF16), (HD, BF16)], T=T, tm=tm, name="mla_prep_fwd")
    qn = _mm(cq, W["uq_n"], "nn", BF16, name="uq_n_fwd")
    qpl = _mm(cq, W["uq_p"], "nn", F32, name="uq_p_fwd")
    kn = _mm(ckv, W["ukv_k"], "nn", BF16, name="ukv_k_fwd")
    vb = _mm(ckv, W["ukv_v"], "nn", BF16, name="ukv_v_fwd")

    def qrope_fn(lin, rot, c, s):
        return lin * jnp.tile(c, (1, HEADS)) + rot * jnp.tile(s, (1, HEADS))

    qp = _row_call(lambda lin, rot, c, s: (qrope_fn(lin, rot, c, s),), [col(qpl, 0, 1024), col(qpl, 1, 1024), tab(cos), tab(sin)],
                   [], [(1024, BF16)], T=T, tm=tm, name="q_rope_fwd")[0]
    ob, lse = _flash_fwd(qn, qp, kn, kpe, vb, B=B, S=S, t=ta)

    def merge_fn(ya, yb, ga, gb):
        return jax.nn.sigmoid(ga.astype(F32)) * ya + jax.nn.sigmoid(gb.astype(F32)) * yb

    def merge_fwd(oat, obt, ga, gb, wog, wom):
        ya, yb = _dot(oat, wog), _dot(obt, wom)
        return ya, yb, merge_fn(ya, yb, ga, gb)

    ya, yb, merged = _row_call(merge_fwd, [oa, ob, col(z_gbr, 0, 1024), col(z_gbr, 1, 1024)], [W["w_o_gdn"], W["w_o_mla"]],
                               [(1024, BF16), (1024, BF16), (1024, BF16)], T=T, tm=tm, name="merge_fwd")
    x1 = _mm(merged, W["w_out"], "nn", F32, add=x2d, name="w_out_fwd")

    h2 = _norm_fwd(x1, g_ffn, T=T, tm=tm, name="norm_ffn_fwd")
    up = _mm(h2, W["w_up"], "nn", BF16, name="w_up_fwd")
    FCB = 256
    nfb = D_FF // FCB
    f = _conv_fwd(_ffn_fn, [(up, 0), (up, 2)], [(conv_ffn_w, 0), (conv_ffn_w, 2)], D_FF, BF16, T=T, S=S, tm=tm,
                  cb=D_FF // 2, ncb=2, name="ffn_act_fwd")
    x2 = _mm(f, W["w_down"], "nn", F32, add=x1, name="w_down_fwd", tk=1408)

    def final_fn(xt, tt, g):
        def lossf(xv, gv):
            e = _rms(xv, gv) - tt
            return 0.5 * jnp.sum(jnp.mean(e * e, axis=-1))

        l, vjp = jax.vjp(lossf, xt, g)
        dx, dg = vjp(jnp.ones((), F32))
        return dx, jnp.full((1, LANES), l, F32), dg

    dx2, loss_v, dg_fin = _row_call(final_fn, [x2, tgt2d], [g_fin], [(1024, F32)], [((1, LANES), F32), ((1, 1024), F32)],
                                    T=T, tm=tm, name="loss_head")

    dW = {}
    df = _mm(dx2, W["w_down"], "nt", BF16, name="w_down_dx")
    dW["w_down"] = _mm(f, dx2, "tn", F32, name="w_down_dw")
    dug, duu, dcw_g, dcw_u = _conv_bwd(_ffn_fn, [(up, 0), (up, nfb)], [(conv_ffn_w, 0), (conv_ffn_w, nfb)], df, BF16,
                                       T=T, S=S, tm=tm, cb=FCB, ncb=nfb, name="ffn_act_bwd")
    d_conv_ffn = jnp.concatenate([dcw_g, dcw_u], axis=1)
    wup_g, wup_u = W["w_up"][:, :D_FF], W["w_up"][:, D_FF:]
    dh2 = _mm(dug, wup_g, "nt", F32, name="w_up_dx_g")
    dh2 = _mm(duu, wup_u, "nt", F32, add=dh2, name="w_up_dx_u")
    dW["w_up"] = jnp.concatenate([_mm(h2, dug, "tn", F32, name="w_up_dw_g"), _mm(h2, duu, "tn", F32, name="w_up_dw_u")], axis=1)
    dx1, dg_ffn = _norm_bwd(x1, g_ffn, dh2, dx2, T=T, tm=tm, name="norm_ffn_bwd")

    dmerged = _mm(dx1, W["w_out"], "nt", F32, name="w_out_dx")
    dW["w_out"] = _mm(merged, dx1, "tn", F32, name="w_out_dw")

    def merge_bwd(dm, yat, ybt, ga, gb):
        _, vjp = jax.vjp(merge_fn, yat.astype(F32), ybt.astype(F32), ga, gb)
        return vjp(dm)

    dya, dyb, dgbr_a, dgbr_b = _row_call(merge_bwd, [dmerged, ya, yb, col(z_gbr, 0, 1024), col(z_gbr, 1, 1024)], [],
                                         [(1024, BF16)] * 4, T=T, tm=tm, name="merge_bwd")
    doa = _mm(dya, W["w_o_gdn"], "nt", F32, name="w_o_gdn_dx")
    dob = _mm(dyb, W["w_o_mla"], "nt", BF16, name="w_o_mla_dx")
    dW["w_o_gdn"] = _mm(oa, dya, "tn", F32, name="w_o_gdn_dw")
    dW["w_o_mla"] = _mm(ob, dyb, "tn", F32, name="w_o_mla_dw")

    dqn, dqp, dl = _flash_bwd_dq(qn, qp, kn, kpe, vb, ob, dob, lse, B=B, S=S, t=ta)
    dkn, dkp, dvb = _flash_bwd_dkv(qn, qp, kn, kpe, vb, dob, lse.reshape(HEADS, 1, T), dl.reshape(HEADS, 1, T),
                                   B=B, S=S, t=ta)

    def qrope_bwd(d, c, s):
        return d * jnp.tile(c, (1, HEADS)), d * jnp.tile(s, (1, HEADS))

    dq_lin, dq_rot = _row_call(qrope_bwd, [dqp, tab(cos), tab(sin)], [], [(1024, BF16), (1024, BF16)], T=T, tm=tm,
                               name="q_rope_bwd")
    wp_lin, wp_rot = W["uq_p"][:, :1024], W["uq_p"][:, 1024:]
    dcq = _mm(dqn, W["uq_n"], "nt", F32, name="uq_n_dx")
    dcq = _mm(dq_lin, wp_lin, "nt", F32, add=dcq, name="uq_pl_dx")
    dcq = _mm(dq_rot, wp_rot, "nt", F32, add=dcq, name="uq_pr_dx")
    dW["uq_n"] = _mm(cq, dqn, "tn", F32, name="uq_n_dw")
    dW["uq_p"] = jnp.concatenate([_mm(cq, dq_lin, "tn", F32, name="uq_pl_dw"), _mm(cq, dq_rot, "tn", F32, name="uq_pr_dw")], axis=1)
    dckv = _mm(dkn, W["ukv_k"], "nt", F32, name="ukv_k_dx")
    dckv = _mm(dvb, W["ukv_v"], "nt", F32, add=dckv, name="ukv_v_dx")
    dW["ukv_k"] = _mm(ckv, dkn, "tn", F32, name="ukv_k_dw")
    dW["ukv_v"] = _mm(ckv, dvb, "tn", F32, name="ukv_v_dw")

    def mla_prep_bwd(zq, zkv0, zkv1, zpl, zpr, c, s, dcqt, dckvt, dkpt, gq, gkv):
        zkv = jnp.concatenate([zkv0, zkv1], axis=1)
        _, vjp = jax.vjp(lambda a, b, p, r, g1, g2: mla_prep_fn(a, b, p, r, c, s, g1, g2), zq, zkv, zpl, zpr, gq, gkv)
        dk = dkpt[0]
        for h in range(1, HEADS):
            dk = dk + dkpt[h]
        dzq, dzkv, dzpl, dzpr, dgq, dgkv = vjp((dcqt, dckvt, dk))
        return jnp.concatenate([dzq, dzkv, dzpl, dzpr], axis=1), dgq, dgkv

    dz_small, dg_q, dg_kv = _row_call(
        mla_prep_bwd, small_cols + [tab(cos), tab(sin), dcq, dckv, (dkp, (HEADS, tm, HD), lambda i: (0, i, 0))],
        [g_q, g_kv], [(896, BF16)], [((1, Q_RANK), F32), ((1, KV_RANK), F32)], T=T, tm=tm, name="mla_prep_bwd")

    def gdn_out_bwd(o, ga, dot_, g):
        _, vjp = jax.vjp(gdn_out_fn, o, ga, g)
        return vjp(dot_)

    do_gdn, dz_ga, dg_gdn = _row_call(gdn_out_bwd, [o_gdn, z_ga, doa], [g_gdn], [(1024, F32), (1024, BF16)],
                                      [((1, HD), F32)], T=T, tm=tm, name="gdn_out_bwd")
    dqkvn, dgc, dgrT, dbeta = _gdn_bwd(qkvn, gcum, grT, beta, states, do_gdn, B=B, S=S)
    dgc_tot = dgc + _padc(dgrT[:, :, 0, :].transpose(0, 2, 1).reshape(T, HEADS), LANES)

    def gate_bwd(za, zb, dg, db, al, db_):
        _, vjp = jax.vjp(_gate_fn, za, zb, al, db_)
        return vjp((dg, db))

    dz_a, dz_b, d_alog, d_dtb = _row_call(gate_bwd, [col(z_ab, 0, LANES), col(z_ab, 1, LANES), dgc_tot, dbeta], [alog, dtb],
                                          [(LANES, BF16), (LANES, BF16)], [((1, LANES), F32), ((1, LANES), F32)],
                                          T=T, tm=tm, name="gdn_gate_bwd")
    dz_qkv, d_conv_qkv = _conv_bwd(_qkv_fn, [(z_qkv, 0)], [(conv_qkv_w, 0)], dqkvn, BF16, T=T, S=S, tm=tm, cb=QKV_CB,
                                   ncb=3072 // QKV_CB, name="gdn_qkv_bwd")

    dz_ab = jnp.concatenate([dz_a, dz_b], axis=1)
    dz_gbr = jnp.concatenate([dgbr_a, dgbr_b], axis=1)
    dh1 = None
    for key, dz in (("in_qkv", dz_qkv), ("in_ga", dz_ga), ("in_ab", dz_ab), ("in_small", dz_small), ("in_gbr", dz_gbr)):
        dh1 = _mm(dz, W[key], "nt", F32, add=dh1, name=key + "_dx", tk=896 if key == "in_small" else 1024)
        dW[key] = _mm(h1, dz, "tn", F32, name=key + "_dw", tn=896 if key == "in_small" else 1024)
    dx, dg_mix = _norm_bwd(x2d, g_mix, dh1, dx1, T=T, tm=tm, name="norm_mix_bwd")

    dsmall = {"norm_mix_g": dg_mix, "gdn_a_log": d_alog[:, :HEADS], "gdn_dt_bias": d_dtb[:, :HEADS], "gdn_norm_g": dg_gdn,
              "mla_q_norm_g": dg_q, "mla_kv_norm_g": dg_kv, "norm_ffn_g": dg_ffn, "norm_final_g": dg_fin}
    return loss_v[0, 0], dx.reshape(B, S, D_MODEL), dW, d_conv_qkv, d_conv_ffn, dsmall


def kernel(x, norm_mix_g, w_in, conv_qkv_w, gdn_a_log, gdn_dt_bias, gdn_norm_g, mla_q_norm_g, w_uq, mla_kv_norm_g, w_ukv, w_o_gdn, w_o_mla, w_out, norm_ffn_g, w_up, conv_ffn_w, w_down, norm_final_g, loss_target, m_norm_mix_g, m_w_in, m_conv_qkv_w, m_gdn_a_log, m_gdn_dt_bias, m_gdn_norm_g, m_mla_q_norm_g, m_w_uq, m_mla_kv_norm_g, m_w_ukv, m_w_o_gdn, m_w_o_mla, m_w_out, m_norm_ffn_g, m_w_up, m_conv_ffn_w, m_w_down, m_norm_final_g, v_norm_mix_g, v_w_in, v_conv_qkv_w, v_gdn_a_log, v_gdn_dt_bias, v_gdn_norm_g, v_mla_q_norm_g, v_w_uq, v_mla_kv_norm_g, v_w_ukv, v_w_o_gdn, v_w_o_mla, v_w_out, v_norm_ffn_g, v_w_up, v_conv_ffn_w, v_w_down, v_norm_final_g):
    w = dict(norm_mix_g=norm_mix_g, w_in=w_in, conv_qkv_w=conv_qkv_w, gdn_a_log=gdn_a_log, gdn_dt_bias=gdn_dt_bias,
             gdn_norm_g=gdn_norm_g, mla_q_norm_g=mla_q_norm_g, w_uq=w_uq, mla_kv_norm_g=mla_kv_norm_g, w_ukv=w_ukv,
             w_o_gdn=w_o_gdn, w_o_mla=w_o_mla, w_out=w_out, norm_ffn_g=norm_ffn_g, w_up=w_up, conv_ffn_w=conv_ffn_w,
             w_down=w_down, norm_final_g=norm_final_g)
    m = dict(norm_mix_g=m_norm_mix_g, w_in=m_w_in, conv_qkv_w=m_conv_qkv_w, gdn_a_log=m_gdn_a_log, gdn_dt_bias=m_gdn_dt_bias,
             gdn_norm_g=m_gdn_norm_g, mla_q_norm_g=m_mla_q_norm_g, w_uq=m_w_uq, mla_kv_norm_g=m_mla_kv_norm_g, w_ukv=m_w_ukv,
             w_o_gdn=m_w_o_gdn, w_o_mla=m_w_o_mla, w_out=m_w_out, norm_ffn_g=m_norm_ffn_g, w_up=m_w_up,
             conv_ffn_w=m_conv_ffn_w, w_down=m_w_down, norm_final_g=m_norm_final_g)
    v = dict(norm_mix_g=v_norm_mix_g, w_in=v_w_in, conv_qkv_w=v_conv_qkv_w, gdn_a_log=v_gdn_a_log, gdn_dt_bias=v_gdn_dt_bias,
             gdn_norm_g=v_gdn_norm_g, mla_q_norm_g=v_mla_q_norm_g, w_uq=v_w_uq, mla_kv_norm_g=v_mla_kv_norm_g, w_ukv=v_w_ukv,
             w_o_gdn=v_w_o_gdn, w_o_mla=v_w_o_mla, w_out=v_w_out, norm_ffn_g=v_norm_ffn_g, w_up=v_w_up,
             conv_ffn_w=v_conv_ffn_w, w_down=v_w_down, norm_final_g=v_norm_final_g)
    slab_names = ("A", "Q", "C", "V")
    big_names = ("w_in", "w_up", "w_uq", "w_ukv", "w_o_gdn", "w_o_mla", "w_out", "w_down", "conv_qkv_w", "conv_ffn_w")
    small_names = [n for n, _ in SMALL]
    small_shapes = {n: w[n].shape for n in small_names}
    local2d = lambda d: {n: d[n][0] for n in big_names}

    w_slabs = _slabs(local2d(w), F32)
    a_bf = w_slabs["A"].astype(BF16)
    first = _allgather_async([a_bf[:1024]], name="allgather_w_in", collective_id=1)
    rest = _allgather_async([a_bf[1024:], w_slabs["Q"].astype(BF16), w_slabs["C"].astype(BF16), w_slabs["V"]],
                            name="allgather_weights", collective_id=2)
    gathered = {k: g.reshape(N_DEV, -1, g.shape[1]) for k, g in zip(("A_in", "A_up", "Q", "C", "V"), first + rest)}
    W, conv_qkv_full, conv_ffn_full = _layout_weights(gathered)

    loss_local, dx, dW, d_conv_qkv, d_conv_ffn, dsmall = _local_step(
        x, loss_target, W, conv_qkv_full, conv_ffn_full, {n: w[n] for n in small_names})

    g_send = _send_slabs(dW, d_conv_qkv, d_conv_ffn)
    small_send = jnp.tile(_pack_small(dsmall), (N_DEV, 1))
    recv = _alltoall_async([g_send[k] for k in slab_names] + [small_send], name="alltoall_grads", collective_id=0)
    small_parts = recv[len(slab_names)]
    m_slabs, v_slabs = _slabs(local2d(m), F32), _slabs(local2d(v), F32)
    upd = {k: _reduce_adam(r, w_slabs[k], m_slabs[k], v_slabs[k], tr=SLAB_TR[k], name="adam_" + k)
           for k, r in zip(slab_names, recv)}
    upd_small = _reduce_adam(small_parts, _pack_small({n: w[n] for n in small_names}), _pack_small({n: m[n] for n in small_names}),
                             _pack_small({n: v[n] for n in small_names}), tr=SMALL_ROWS, name="adam_small")

    loss = lax.psum(loss_local, ("x", "y", "c"))
    groups = []
    for i in range(4):
        merged = {**_unslab({k: upd[k][i] for k in slab_names}), **_unpack_small(upd_small[i], small_shapes)}
        groups.append([merged[n] for n in WEIGHT_ORDER])
    return (loss, dx, *groups[0], *groups[1], *groups[2], *groups[3])
```

```python
import functools
import math

import numpy as np
import jax
import jax.numpy as jnp
from jax import lax
from jax.experimental import pallas as pl
from jax.experimental.pallas import tpu as pltpu
from jax.experimental.pallas import tpu_sc as plsc

F32 = jnp.float32
BF16 = jnp.bfloat16

D_MODEL = 1024
HEADS = 8
HD = 128
GDN_CONV = 4
CHUNK = 64
Q_RANK = 384
KV_RANK = 256
ROPE = 64
ROPE_THETA = 10000.0
D_FF = 2816
FFN_CONV = 3
EPS = 1e-6
SM_SCALE = (HD + ROPE) ** -0.5
N_DEV = 8

ADAM_LR, ADAM_B1, ADAM_B2, ADAM_EPS, ADAM_WD, ADAM_STEP = 0.001, 0.9, 0.999, 1e-08, 0.01, 10

LANES = 128
SUBLANES = 8
HALO = 2 * SUBLANES
VMEM_LIMIT = 56 * 1024 * 1024
HI = lax.Precision.HIGHEST
TRI_PRECISION = None

NN = (((1,), (0,)), ((), ()))
NT = (((1,), (1,)), ((), ()))
TN = (((0,), (0,)), ((), ()))


def _dot(a, b, dims=NN, precision=None):
    return lax.dot_general(a, b, dims, precision=precision, preferred_element_type=F32)


def _pick(dim, target, align):
    best = None
    for t in range(align, min(dim, target) + 1, align):
        if dim % t == 0:
            best = t
    return dim if best is None else best


def _call(body, ins, outs, grid, *, name, scratch=(), semantics=None):
    n_in, n_out = len(ins), len(outs)

    def kern(*refs):
        body(refs[:n_in], refs[n_in:n_in + n_out], refs[n_in + n_out:])

    res = pl.pallas_call(
        kern,
        grid=grid,
        in_specs=[pl.BlockSpec(bs, im) for _, bs, im in ins],
        out_specs=[pl.BlockSpec(bs, im) for _, _, bs, im in outs],
        out_shape=[jax.ShapeDtypeStruct(s, d) for s, d, _, _ in outs],
        scratch_shapes=list(scratch),
        name=name,
        compiler_params=pltpu.CompilerParams(
            dimension_semantics=semantics or ("arbitrary",) * len(grid), vmem_limit_bytes=VMEM_LIMIT),
    )(*[a for a, _, _ in ins])
    return res


def _mm(a, b, mode, out_dtype, *, name, add=None, tm=1408, tn=1408, tk=1408):
    if mode == "nn":
        (M, K), (K2, N) = a.shape, b.shape
    elif mode == "nt":
        (M, K), (N, K2) = a.shape, b.shape
    else:
        (K, M), (K2, N) = a.shape, b.shape
    assert K == K2, (a.shape, b.shape, mode)
    tm = _pick(M, tm, LANES if mode == "tn" else 16)
    tn = _pick(N, tn, LANES)
    tk = _pick(K, tk, 16 if mode == "tn" else LANES)
    nk = K // tk
    dims = {"nn": NN, "nt": NT, "tn": TN}[mode]
    if mode == "nn":
        a_spec, b_spec = ((tm, tk), lambda i, j, k: (i, k)), ((tk, tn), lambda i, j, k: (k, j))
    elif mode == "nt":
        a_spec, b_spec = ((tm, tk), lambda i, j, k: (i, k)), ((tn, tk), lambda i, j, k: (j, k))
    else:
        a_spec, b_spec = ((tk, tm), lambda i, j, k: (k, i)), ((tk, tn), lambda i, j, k: (k, j))
    ins = [(a,) + a_spec, (b,) + b_spec]
    if add is not None:
        ins.append((add, (tm, tn), lambda i, j, k: (i, j)))
    outs = [((M, N), out_dtype, (tm, tn), lambda i, j, k: (i, j))]

    def body(in_refs, out_refs, scr):
        prod = _dot(in_refs[0][...].astype(BF16), in_refs[1][...].astype(BF16), dims)

        def finish(r):
            if add is not None:
                r = r + in_refs[2][...].astype(F32)
            out_refs[0][...] = r.astype(out_dtype)

        if nk == 1:
            finish(prod)
            return
        k = pl.program_id(2)
        acc = scr[0]

        @pl.when(k == 0)
        def _():
            acc[...] = prod

        @pl.when(k > 0)
        def _():
            acc[...] += prod

        @pl.when(k == nk - 1)
        def _():
            finish(acc[...])

    return _call(body, ins, outs, (M // tm, N // tn, nk), name=name,
                 scratch=[pltpu.VMEM((tm, tn), F32)] if nk > 1 else [],
                 semantics=("parallel", "parallel", "arbitrary"))[0]


def _row_call(fn, rows, consts, out_rows, out_accs=(), *, T, tm, name):
    nt = T // tm
    ins = []
    for r in rows:
        ins.append(r if isinstance(r, tuple) else (r, (tm, r.shape[1]), lambda i: (i, 0)))
    for c in consts:
        ins.append((c, c.shape, lambda i, nd=c.ndim: (0,) * nd))
    outs = []
    for o in out_rows:
        outs.append(((T, o[0]), o[1], (tm, o[0]), lambda i: (i, 0)) if len(o) == 2 else o)
    for shp, dt in out_accs:
        outs.append((shp, dt, shp, lambda i, nd=len(shp): (0,) * nd))
    n_r = len(out_rows)

    def body(in_refs, out_refs, _):
        i = pl.program_id(0)
        vals = fn(*[r[...] for r in in_refs])
        for o_ref, v in zip(out_refs[:n_r], vals[:n_r]):
            o_ref[...] = v.astype(o_ref.dtype)
        for o_ref, v in zip(out_refs[n_r:], vals[n_r:]):
            @pl.when(i == 0)
            def _(o_ref=o_ref):
                o_ref[...] = jnp.zeros_like(o_ref)

            o_ref[...] += v.astype(o_ref.dtype)

    return _call(body, ins, outs, (nt,), name=name)


def _rms(x, g):
    return x * lax.rsqrt(jnp.mean(x * x, axis=-1, keepdims=True) + EPS) * g


def _norm_fwd(x, g, *, T, tm, name):
    return _row_call(lambda xt, gt: (_rms(xt, gt),), [x], [g], [(x.shape[1], BF16)], T=T, tm=tm, name=name)[0]


def _norm_bwd(x, g, dh, dres, *, T, tm, name):
    def fn(xt, dht, drt, gt):
        _, vjp = jax.vjp(_rms, xt, gt)
        dx, dg = vjp(dht)
        return drt + dx, dg

    return _row_call(fn, [x, dh, dres], [g], [(x.shape[1], F32)], [(g.shape, F32)], T=T, tm=tm, name=name)


def _rows16(c):
    return lax.broadcasted_iota(jnp.int32, (HALO, c), 0)


@functools.lru_cache(maxsize=None)
def _shift_fn(j):
    @jax.custom_vjp
    def shift(x, halo):
        xr = pltpu.roll(x, j, 0)
        top = jnp.where(_rows16(x.shape[1]) < j, pltpu.roll(halo, j, 0), xr[:HALO])
        return jnp.concatenate([top, xr[HALO:]], axis=0)

    def fwd(x, halo):
        return shift(x, halo), None

    def bwd(_, dy):
        tm, c = dy.shape
        keep = _rows16(c) >= HALO - j
        dxr = pltpu.roll(dy, tm - j, 0)
        dx = jnp.concatenate([dxr[:tm - HALO], jnp.where(keep, 0.0, dxr[tm - HALO:])], axis=0)
        dhalo = jnp.where(keep, pltpu.roll(dy[:HALO], HALO - j, 0), 0.0)
        return dx, dhalo

    shift.defvjp(fwd, bwd)
    return shift


def _dwconv(tail, x, w):
    K = w.shape[0]
    acc = w[K - 1:K, :] * x
    for k in range(K - 1):
        acc = acc + w[k:k + 1, :] * _shift_fn(K - 1 - k)(x, tail)
    return acc


STRIP = 64


def _conv_fwd(fn, xs, ws, out_c, out_dtype, *, T, S, tm, cb, ncb, name):
    nt, tps, hb = T // tm, S // tm, tm // HALO
    ins = []
    for arr, off in xs:
        ins.append((arr, (tm, cb), lambda j, i, off=off: (i, off + j)))
        ins.append((arr, (HALO, cb), lambda j, i, off=off: (jnp.maximum(i * hb - 1, 0), off + j)))
    for arr, off in ws:
        ins.append((arr, (arr.shape[0], cb), lambda j, i, off=off: (0, off + j)))
    outs = [((T, out_c), out_dtype, (tm, cb), lambda j, i: (i, j))]
    nx = len(xs)

    def body(in_refs, out_refs, _):
        j, i = pl.program_id(0), pl.program_id(1)
        first = (i % tps) == 0
        wts = [r[...] for r in in_refs[2 * nx:]]
        for r in range(0, tm, STRIP):
            xts = [in_refs[2 * m][r:r + STRIP, :].astype(F32) for m in range(nx)]
            if r == 0:
                tails = [jnp.where(first, 0.0, in_refs[2 * m + 1][...].astype(F32)) for m in range(nx)]
            else:
                tails = [in_refs[2 * m][r - HALO:r, :].astype(F32) for m in range(nx)]
            out_refs[0][r:r + STRIP, :] = fn(j, tails, xts, wts).astype(out_dtype)

    return _call(body, ins, outs, (ncb, nt), name=name)[0]


def _conv_bwd(fn, xs, ws, dout, dx_dtype, *, T, S, tm, cb, ncb, name):
    nt, tps, hb = T // tm, S // tm, tm // HALO
    ins = []
    for arr, off in xs:
        ins.append((arr, (tm, cb), lambda j, i, off=off: (nt - 1 - i, off + j)))
        ins.append((arr, (HALO, cb), lambda j, i, off=off: (jnp.maximum((nt - 1 - i) * hb - 1, 0), off + j)))
    for arr, off in ws:
        ins.append((arr, (arr.shape[0], cb), lambda j, i, off=off: (0, off + j)))
    ins.append((dout, (tm, cb), lambda j, i: (nt - 1 - i, j)))
    nx, nw = len(xs), len(ws)
    outs = [((T, ncb * cb), dx_dtype, (tm, cb), lambda j, i: (nt - 1 - i, j)) for _ in xs]
    outs += [((arr.shape[0], ncb * cb), F32, (arr.shape[0], cb), lambda j, i: (0, j)) for arr, _ in ws]
    scratch = [pltpu.VMEM((HALO, cb), F32) for _ in xs]

    def body(in_refs, out_refs, carry):
        j, i = pl.program_id(0), pl.program_id(1)
        first = ((nt - 1 - i) % tps) == 0
        wts = [ref[...] for ref in in_refs[2 * nx:2 * nx + nw]]

        @pl.when(i == 0)
        def _():
            for c in carry:
                c[...] = jnp.zeros_like(c)

        carried = [c[...] for c in carry]
        dw_sum = None
        for r in reversed(range(0, tm, STRIP)):
            xts = [in_refs[2 * m][r:r + STRIP, :].astype(F32) for m in range(nx)]
            if r == 0:
                tails = [jnp.where(first, 0.0, in_refs[2 * m + 1][...].astype(F32)) for m in range(nx)]
            else:
                tails = [in_refs[2 * m][r - HALO:r, :].astype(F32) for m in range(nx)]
            _, vjp = jax.vjp(lambda tl, xt, wt: fn(j, tl, xt, wt), tails, xts, wts)
            dtails, dxts, dwts = vjp(in_refs[-1][r:r + STRIP, :].astype(F32))
            for m in range(nx):
                pad = jnp.concatenate([jnp.zeros((STRIP - HALO, cb), F32), carried[m]], axis=0)
                out_refs[m][r:r + STRIP, :] = (dxts[m] + pad).astype(dx_dtype)
            carried = [jnp.where(first, 0.0, dt) for dt in dtails] if r == 0 else list(dtails)
            dw_sum = list(dwts) if dw_sum is None else [a + b for a, b in zip(dw_sum, dwts)]
        for m in range(nx):
            carry[m][...] = carried[m]
        for m in range(nw):
            o_ref = out_refs[nx + m]

            @pl.when(i == 0)
            def _(o_ref=o_ref):
                o_ref[...] = jnp.zeros_like(o_ref)

            o_ref[...] += dw_sum[m]

    return _call(body, ins, outs, (ncb, nt), name=name, scratch=scratch)


QKV_CB = 512


def _qkv_fn(j, tails, xts, wts):
    y = jax.nn.silu(_dwconv(tails[0], xts[0], wts[0]))
    scale = jnp.where(j < 1024 // QKV_CB, HD ** -0.5, 1.0)
    parts = []
    for h in range(QKV_CB // HD):
        yh = y[:, h * HD:(h + 1) * HD]
        nh = yh * lax.rsqrt(jnp.sum(yh * yh, axis=-1, keepdims=True) + EPS)
        parts.append(jnp.where(j < 2048 // QKV_CB, nh * scale, yh))
    return jnp.concatenate(parts, axis=1)


def _ffn_fn(j, tails, xts, wts):
    return jax.nn.silu(_dwconv(tails[0], xts[0], wts[0])) * _dwconv(tails[1], xts[1], wts[1])


BNN = (((2,), (1,)), ((0,), (0,)))
BNT = (((2,), (2,)), ((0,), (0,)))
BTN = (((1,), (1,)), ((0,), (0,)))


@jax.custom_vjp
def _tri_inv(L):
    C = L.shape[-1]
    ii = lax.broadcasted_iota(jnp.int32, (C, C), 0)
    jj = lax.broadcasted_iota(jnp.int32, (C, C), 1)
    eye = (ii == jj).astype(F32)
    X = eye - jnp.where((ii >> 1) == (jj >> 1), L, 0.0)
    s = 1
    while (2 << s) <= C:
        E = jnp.where(((ii >> (s + 1)) == (jj >> (s + 1))) & ((ii >> s) != (jj >> s)), L, 0.0)
        X = X - _dot(_dot(X, E, BNN, precision=TRI_PRECISION), X, BNN, precision=TRI_PRECISION)
        s += 1
    return X


def _tri_inv_fwd(L):
    X = _tri_inv(L)
    return X, X


def _tri_inv_bwd(X, dX):
    return (-_dot(_dot(X, dX, BTN, precision=TRI_PRECISION), X, BNT, precision=TRI_PRECISION),)


_tri_inv.defvjp(_tri_inv_fwd, _tri_inv_bwd)


def _gdn_chunk(q, k, v, gc, gr, beta, S):
    C = q.shape[1]
    ii = lax.broadcasted_iota(jnp.int32, (C, C), 0)
    jj = lax.broadcasted_iota(jnp.int32, (C, C), 1)
    lower = ii >= jj
    decay = jnp.where(lower, jnp.exp(jnp.where(lower, gc - gr, 0.0)), 0.0)
    kb, vb = k * beta, v * beta
    L = jnp.where(ii > jj, _dot(kb, k, BNT) * decay, 0.0)
    Tinv = _tri_inv(L)
    eg = jnp.exp(gc)
    u = _dot(Tinv, vb, BNN, precision=TRI_PRECISION)
    w = _dot(Tinv, kb * eg, BNN, precision=TRI_PRECISION)
    a = _dot(q, k, BNT) * decay
    g_last = gc[:, C - 1:C, :]
    kd = k * jnp.exp(g_last - gc)
    v_new = u - _dot(w, S, BNN)
    o = _dot(q * eg, S, BNN) + _dot(a, v_new, BNN)
    S_new = S * jnp.exp(g_last) + _dot(kd, v_new, BTN)
    return o, S_new


def _heads(ref, width=HD):
    return jnp.stack([ref[:, h * width:(h + 1) * width].astype(F32) for h in range(HEADS)])


def _gdn_fwd(qkvn, gcum, grT, beta, *, B, S):
    N, T = S // CHUNK, B * S
    row = lambda c: (lambda b, n: (b * N + n, c))
    ins = [(qkvn, (CHUNK, 1024), row(0)), (qkvn, (CHUNK, 1024), row(1)), (qkvn, (CHUNK, 1024), row(2)),
           (gcum, (CHUNK, LANES), row(0)), (grT, (1, HEADS, 1, CHUNK), lambda b, n: (b * N + n, 0, 0, 0)),
           (beta, (CHUNK, LANES), row(0))]
    outs = [((T, 1024), F32, (CHUNK, 1024), row(0)),
            ((B * N, HEADS, HD, HD), BF16, (1, HEADS, HD, HD), lambda b, n: (b * N + n, 0, 0, 0))]

    def body(in_refs, out_refs, scr):
        q_ref, k_ref, v_ref, gc_ref, gr_ref, b_ref = in_refs
        o_ref, st_ref = out_refs
        S_ref = scr[0]

        @pl.when(pl.program_id(1) == 0)
        def _():
            S_ref[...] = jnp.zeros_like(S_ref)

        S0 = S_ref[...]
        st_ref[0] = S0.astype(BF16)
        o, Sn = _gdn_chunk(_heads(q_ref), _heads(k_ref), _heads(v_ref), _heads(gc_ref, 1), gr_ref[0],
                           _heads(b_ref, 1), S0)
        for h in range(HEADS):
            o_ref[:, h * HD:(h + 1) * HD] = o[h]
        S_ref[...] = Sn

    return _call(body, ins, outs, (B, N), name="gdn_core_fwd", scratch=[pltpu.VMEM((HEADS, HD, HD), F32)])


def _gdn_bwd(qkvn, gcum, grT, beta, states, do, *, B, S):
    N, T = S // CHUNK, B * S
    row = lambda c: (lambda b, n: (b * N + N - 1 - n, c))
    ins = [(qkvn, (CHUNK, 1024), row(0)), (qkvn, (CHUNK, 1024), row(1)), (qkvn, (CHUNK, 1024), row(2)),
           (gcum, (CHUNK, LANES), row(0)), (grT, (1, HEADS, 1, CHUNK), lambda b, n: (b * N + N - 1 - n, 0, 0, 0)),
           (beta, (CHUNK, LANES), row(0)),
           (states, (1, HEADS, HD, HD), lambda b, n: (b * N + N - 1 - n, 0, 0, 0)), (do, (CHUNK, 1024), row(0))]
    outs = [((T, 3072), BF16, (CHUNK, 3072), row(0)), ((T, LANES), F32, (CHUNK, LANES), row(0)),
            ((B * N, HEADS, 1, CHUNK), F32, (1, HEADS, 1, CHUNK), lambda b, n: (b * N + N - 1 - n, 0, 0, 0)),
            ((T, LANES), F32, (CHUNK, LANES), row(0))]

    def body(in_refs, out_refs, scr):
        q_ref, k_ref, v_ref, gc_ref, gr_ref, b_ref, st_ref, do_ref = in_refs
        dqkv_ref, dgc_ref, dgr_ref, db_ref = out_refs
        dS_ref = scr[0]

        @pl.when(pl.program_id(1) == 0)
        def _():
            dS_ref[...] = jnp.zeros_like(dS_ref)

        args = (_heads(q_ref), _heads(k_ref), _heads(v_ref), _heads(gc_ref, 1), gr_ref[0], _heads(b_ref, 1),
                st_ref[0].astype(F32))
        _, vjp = jax.vjp(_gdn_chunk, *args)
        dq, dk, dv, dgc, dgr, db, dS = vjp((_heads(do_ref), dS_ref[...]))
        lane = lax.broadcasted_iota(jnp.int32, (CHUNK, LANES), 1)
        dgc_all = jnp.zeros((CHUNK, LANES), F32)
        db_all = jnp.zeros((CHUNK, LANES), F32)
        for h in range(HEADS):
            dqkv_ref[:, h * HD:(h + 1) * HD] = dq[h].astype(BF16)
            dqkv_ref[:, 1024 + h * HD:1024 + (h + 1) * HD] = dk[h].astype(BF16)
            dqkv_ref[:, 2048 + h * HD:2048 + (h + 1) * HD] = dv[h].astype(BF16)
            dgc_all = jnp.where(lane == h, dgc[h], dgc_all)
            db_all = jnp.where(lane == h, db[h], db_all)
        dgc_ref[...] = dgc_all
        db_ref[...] = db_all
        dgr_ref[0] = dgr
        dS_ref[...] = dS

    return _call(body, ins, outs, (B, N), name="gdn_core_bwd", scratch=[pltpu.VMEM((HEADS, HD, HD), F32)])


def _gate_fn(za, zb, alog, dtb):
    tm = za.shape[0]
    g = -jnp.exp(alog) * jax.nn.softplus(za + dtb)
    ii = lax.broadcasted_iota(jnp.int32, (tm, tm), 0)
    jj = lax.broadcasted_iota(jnp.int32, (tm, tm), 1)
    tri = ((ii >= jj) & ((ii >> 6) == (jj >> 6))).astype(F32)
    return _dot(tri, g, precision=HI), jax.nn.sigmoid(zb)


def _scores(qn_ref, qp_ref, kn_ref, kp_ref, diag):
    q = jnp.concatenate([qn_ref[...], qp_ref[...]], axis=1)
    k = jnp.concatenate([kn_ref[...], kp_ref[...]], axis=1)
    s = _dot(q, k, NT) * SM_SCALE
    if diag:
        t = s.shape[0]
        ii = lax.broadcasted_iota(jnp.int32, (t, t), 0)
        jj = lax.broadcasted_iota(jnp.int32, (t, t), 1)
        s = jnp.where(ii >= jj, s, -jnp.inf)
    return s, q, k


HPB = 4
HW = HPB * HD


def _head_refs(refs, hh):
    return [r.at[:, hh * HD:(hh + 1) * HD] for r in refs]


def _flash_fwd(qn, qp, kn, kp, v, *, B, S, t):
    nb, T = S // t, B * S
    qmap = lambda b, h, qi, ki: (b * nb + qi, h)
    kmap = lambda b, h, qi, ki: (b * nb + jnp.minimum(ki, qi), h)
    kpmap = lambda b, h, qi, ki: (b * nb + jnp.minimum(ki, qi), 0)
    ins = [(qn, (t, HW), qmap), (qp, (t, HW), qmap), (kn, (t, HW), kmap), (kp, (t, HD), kpmap), (v, (t, HW), kmap)]
    outs = [((T, 1024), BF16, (t, HW), qmap),
            ((HEADS, T, 1), F32, (HPB, t, 1), lambda b, h, qi, ki: (h, b * nb + qi, 0))]
    scratch = [pltpu.VMEM((HPB, t, 1), F32), pltpu.VMEM((HPB, t, 2 * HD), F32)]

    def body(in_refs, out_refs, scr):
        qn_ref, qp_ref, kn_ref, kp_ref, v_ref = in_refs
        o_ref, lse_ref = out_refs
        m_ref, acc_ref = scr
        qi, ki = pl.program_id(2), pl.program_id(3)

        @pl.when(ki == 0)
        def _():
            m_ref[...] = jnp.full_like(m_ref, -jnp.inf)
            acc_ref[...] = jnp.zeros_like(acc_ref)

        def step(diag):
            for hh in range(HPB):
                qn_h, qp_h, kn_h, v_h = _head_refs((qn_ref, qp_ref, kn_ref, v_ref), hh)
                s, _, _ = _scores(qn_h, qp_h, kn_h, kp_ref, diag)
                m_old = m_ref[hh]
                m_new = jnp.maximum(m_old, jnp.max(s, axis=-1, keepdims=True))
                p = jnp.exp(s - m_new)
                alpha = jnp.exp(m_old - m_new)
                v1 = jnp.concatenate([v_h[...], jnp.ones((t, HD), BF16)], axis=1)
                acc_ref[hh] = alpha * acc_ref[hh] + _dot(p.astype(BF16), v1)
                m_ref[hh] = m_new

        @pl.when(ki < qi)
        def _():
            step(False)

        @pl.when(ki == qi)
        def _():
            step(True)
            for hh in range(HPB):
                o_ref[:, hh * HD:(hh + 1) * HD] = (acc_ref[hh, :, :HD] / acc_ref[hh, :, HD:]).astype(BF16)
                lse_ref[hh] = m_ref[hh] + jnp.log(acc_ref[hh, :, HD:HD + 1])

    return _call(body, ins, outs, (B, HEADS // HPB, nb, nb), name="mla_flash_fwd", scratch=scratch,
                 semantics=("parallel", "parallel", "parallel", "arbitrary"))


def _flash_bwd_dq(qn, qp, kn, kp, v, o, do, lse, *, B, S, t):
    nb, T = S // t, B * S
    qmap = lambda b, h, qi, ki: (b * nb + qi, h)
    kmap = lambda b, h, qi, ki: (b * nb + jnp.minimum(ki, qi), h)
    kpmap = lambda b, h, qi, ki: (b * nb + jnp.minimum(ki, qi), 0)
    ins = [(qn, (t, HW), qmap), (qp, (t, HW), qmap), (kn, (t, HW), kmap), (kp, (t, HD), kpmap), (v, (t, HW), kmap),
           (o, (t, HW), qmap), (do, (t, HW), qmap), (lse, (HPB, t, 1), lambda b, h, qi, ki: (h, b * nb + qi, 0))]
    outs = [((T, 1024), BF16, (t, HW), qmap), ((T, 1024), F32, (t, HW), qmap),
            ((HEADS, T, 1), F32, (HPB, t, 1), lambda b, h, qi, ki: (h, b * nb + qi, 0))]
    scratch = [pltpu.VMEM((HPB, t, 1), F32), pltpu.VMEM((HPB, t, 2 * HD), F32)]

    def body(in_refs, out_refs, scr):
        qn_ref, qp_ref, kn_ref, kp_ref, v_ref, o_ref, do_ref, lse_ref = in_refs
        dqn_ref, dqp_ref, dlo_ref = out_refs
        dl_ref, acc_ref = scr
        qi, ki = pl.program_id(2), pl.program_id(3)

        @pl.when(ki == 0)
        def _():
            for hh in range(HPB):
                o_h, do_h = _head_refs((o_ref, do_ref), hh)
                dl_ref[hh] = jnp.sum(do_h[...].astype(F32) * o_h[...].astype(F32), axis=-1, keepdims=True)
            acc_ref[...] = jnp.zeros_like(acc_ref)

        def step(diag):
            for hh in range(HPB):
                qn_h, qp_h, kn_h, v_h, do_h = _head_refs((qn_ref, qp_ref, kn_ref, v_ref, do_ref), hh)
                s, _, k = _scores(qn_h, qp_h, kn_h, kp_ref, diag)
                p = jnp.exp(s - lse_ref[hh])
                dp = _dot(do_h[...], v_h[...], NT)
                ds = p * (dp - dl_ref[hh]) * SM_SCALE
                acc_ref[hh] += _dot(ds.astype(BF16), k)

        @pl.when(ki < qi)
        def _():
            step(False)

        @pl.when(ki == qi)
        def _():
            step(True)
            for hh in range(HPB):
                dqn_ref[:, hh * HD:(hh + 1) * HD] = acc_ref[hh, :, :HD].astype(BF16)
                dqp_ref[:, hh * HD:(hh + 1) * HD] = acc_ref[hh, :, HD:]
            dlo_ref[...] = dl_ref[...]

    return _call(body, ins, outs, (B, HEADS // HPB, nb, nb), name="mla_flash_bwd_dq", scratch=scratch,
                 semantics=("parallel", "parallel", "parallel", "arbitrary"))


def _flash_bwd_dkv(qn, qp, kn, kp, v, do, lse_t, dl_t, *, B, S, t):
    nb, T = S // t, B * S
    qmap = lambda b, h, ki, qi: (b * nb + jnp.maximum(qi, ki), h)
    kmap = lambda b, h, ki, qi: (b * nb + ki, h)
    tmap = lambda b, h, ki, qi: (h, 0, b * nb + jnp.maximum(qi, ki))
    ins = [(qn, (t, HW), qmap), (qp, (t, HW), qmap), (kn, (t, HW), kmap),
           (kp, (t, HD), lambda b, h, ki, qi: (b * nb + ki, 0)), (v, (t, HW), kmap), (do, (t, HW), qmap),
           (lse_t, (HPB, 1, t), tmap), (dl_t, (HPB, 1, t), tmap)]
    outs = [((T, 1024), BF16, (t, HW), kmap), ((HEADS, T, HD), F32, (HPB, t, HD), lambda b, h, ki, qi: (h, b * nb + ki, 0)),
            ((T, 1024), BF16, (t, HW), kmap)]
    scratch = [pltpu.VMEM((HPB, t, 2 * HD), F32), pltpu.VMEM((HPB, t, HD), F32)]

    def body(in_refs, out_refs, scr):
        qn_ref, qp_ref, kn_ref, kp_ref, v_ref, do_ref, lse_ref, dl_ref = in_refs
        dkn_ref, dkp_ref, dv_ref = out_refs
        dk_acc, dv_acc = scr
        ki, qi = pl.program_id(2), pl.program_id(3)

        @pl.when(qi == 0)
        def _():
            dk_acc[...] = jnp.zeros_like(dk_acc)
            dv_acc[...] = jnp.zeros_like(dv_acc)

        def step(diag):
            for hh in range(HPB):
                qn_h, qp_h, kn_h, v_h, do_h = _head_refs((qn_ref, qp_ref, kn_ref, v_ref, do_ref), hh)
                q = jnp.concatenate([qn_h[...], qp_h[...]], axis=1)
                k = jnp.concatenate([kn_h[...], kp_ref[...]], axis=1)
                st = _dot(k, q, NT) * SM_SCALE
                if diag:
                    ii = lax.broadcasted_iota(jnp.int32, (t, t), 0)
                    jj = lax.broadcasted_iota(jnp.int32, (t, t), 1)
                    st = jnp.where(ii <= jj, st, -jnp.inf)
                do_t = do_h[...]
                pt = jnp.exp(st - lse_ref[hh])
                dst = pt * (_dot(v_h[...], do_t, NT) - dl_ref[hh]) * SM_SCALE
                dv_acc[hh] += _dot(pt.astype(BF16), do_t)
                dk_acc[hh] += _dot(dst.astype(BF16), q)

        @pl.when(qi > ki)
        def _():
            step(False)

        @pl.when(qi == ki)
        def _():
            step(True)

        @pl.when(qi == nb - 1)
        def _():
            for hh in range(HPB):
                dkn_ref[:, hh * HD:(hh + 1) * HD] = dk_acc[hh, :, :HD].astype(BF16)
                dkp_ref[hh] = dk_acc[hh, :, HD:]
                dv_ref[:, hh * HD:(hh + 1) * HD] = dv_acc[hh].astype(BF16)

    return _call(body, ins, outs, (B, HEADS // HPB, nb, nb), name="mla_flash_bwd_dkv", scratch=scratch,
                 semantics=("parallel", "parallel", "parallel", "arbitrary"))


def _allgather(shards, *, name):
    n_arr = len(shards)

    def body(*refs):
        x_refs, out_refs = refs[:n_arr], refs[n_arr:2 * n_arr]
        send_sems, recv_sems, local_sems = refs[2 * n_arr:]
        x, y, c = lax.axis_index("x"), lax.axis_index("y"), lax.axis_index("c")
        me, sibling = (x, y, c), (x, y, 1 - c)
        chips = [(1 - x, y), (x, 1 - y), (1 - x, 1 - y)]

        def rows(a, px, py, pc):
            m_per = shards[a].shape[0]
            return out_refs[a].at[pl.ds((4 * px + 2 * py + pc) * m_per, m_per), :]

        def copy(a, k, block, to, src=None):
            return pltpu.make_async_remote_copy(
                src_ref=rows(a, *block) if src is None else src, dst_ref=rows(a, *block),
                send_sem=send_sems.at[a, k], recv_sem=recv_sems.at[a, k], device_id=to,
                device_id_type=pl.DeviceIdType.MESH)

        mine = [pltpu.make_async_copy(x_refs[a], rows(a, *me), local_sems.at[a]) for a in range(n_arr)]
        for cp in mine:
            cp.start()
        first = []
        for a in range(n_arr):
            first.append(copy(a, 0, me, sibling, src=x_refs[a]))
            first += [copy(a, 1 + j, me, (*chip, c), src=x_refs[a]) for j, chip in enumerate(chips)]
        for cp in first:
            cp.start()
        passed = []
        for j, chip in enumerate(chips):
            for a in range(n_arr):
                copy(a, 1 + j, (*chip, c), me).wait_recv()
                cp = copy(a, 4 + j, (*chip, c), sibling)
                cp.start()
                passed.append(cp)
        for a in range(n_arr):
            copy(a, 0, sibling, me).wait_recv()
        for j, chip in enumerate(chips):
            for a in range(n_arr):
                copy(a, 4 + j, (*chip, 1 - c), me).wait_recv()
        for cp in first + passed:
            cp.wait_send()
        for cp in mine:
            cp.wait()

    return pl.pallas_call(
        body,
        out_shape=[jax.ShapeDtypeStruct((N_DEV * s.shape[0], s.shape[1]), s.dtype) for s in shards],
        in_specs=[pl.BlockSpec(memory_space=pl.ANY)] * n_arr,
        out_specs=[pl.BlockSpec(memory_space=pl.ANY)] * n_arr,
        scratch_shapes=[pltpu.SemaphoreType.DMA((n_arr, 7)), pltpu.SemaphoreType.DMA((n_arr, 7)),
                        pltpu.SemaphoreType.DMA((n_arr,))],
        name=name,
    )(*shards)


def _alltoall(sends, *, name):
    n_arr = len(sends)

    def body(*refs):
        s_refs, r_refs = refs[:n_arr], refs[n_arr:2 * n_arr]
        send_sems, recv_sems, local_sems = refs[2 * n_arr:]
        x, y, c = lax.axis_index("x"), lax.axis_index("y"), lax.axis_index("c")
        me = 4 * x + 2 * y + c

        def rows(ref, a, idx):
            m_per = sends[a].shape[0] // N_DEV
            return ref.at[pl.ds(idx * m_per, m_per), :]

        local = [pltpu.make_async_copy(rows(s_refs[a], a, me), rows(r_refs[a], a, me), local_sems.at[a])
                 for a in range(n_arr)]
        for cp in local:
            cp.start()
        copies = []
        for k in range(1, N_DEV):
            px = 1 - x if k & 4 else x
            py = 1 - y if k & 2 else y
            pc = 1 - c if k & 1 else c
            for a in range(n_arr):
                cp = pltpu.make_async_remote_copy(
                    src_ref=rows(s_refs[a], a, 4 * px + 2 * py + pc), dst_ref=rows(r_refs[a], a, me),
                    send_sem=send_sems.at[a, k - 1], recv_sem=recv_sems.at[a, k - 1],
                    device_id=(px, py, pc), device_id_type=pl.DeviceIdType.MESH)
                cp.start()
                copies.append(cp)
        for cp in copies:
            cp.wait()
        for cp in local:
            cp.wait()

    return pl.pallas_call(
        body,
        out_shape=[jax.ShapeDtypeStruct(s.shape, s.dtype) for s in sends],
        in_specs=[pl.BlockSpec(memory_space=pl.ANY)] * n_arr,
        out_specs=[pl.BlockSpec(memory_space=pl.ANY)] * n_arr,
        scratch_shapes=[pltpu.SemaphoreType.DMA((n_arr, 7)), pltpu.SemaphoreType.DMA((n_arr, 7)),
                        pltpu.SemaphoreType.DMA((n_arr,))],
        name=name,
    )(*sends)


def _allgather_async(shards, *, name, collective_id):
    n_arr = len(shards)
    hbm = pltpu.MemorySpace.HBM
    x_refs = [jax.new_ref(a, memory_space=hbm) for a in shards]
    out_refs = [jax.empty_ref(jax.ShapeDtypeStruct((N_DEV * a.shape[0], a.shape[1]), a.dtype), memory_space=hbm)
                for a in shards]

    @pl.kernel(mesh=plsc.ScalarSubcoreMesh(axis_name="seq", num_cores=1), name=name,
               scratch_types=(pltpu.SemaphoreType.DMA((n_arr, 7)), pltpu.SemaphoreType.DMA((n_arr, 7)),
                              pltpu.SemaphoreType.DMA((n_arr,))),
               compiler_params=pltpu.CompilerParams(collective_id=collective_id))
    def launch(send_sems, recv_sems, local_sems):
        x, y, c = lax.axis_index("x"), lax.axis_index("y"), lax.axis_index("c")
        me, sibling = (x, y, c), (x, y, 1 - c)
        chips = [(1 - x, y), (x, 1 - y), (1 - x, 1 - y)]
        barrier = pltpu.get_barrier_semaphore()
        for p in [sibling] + [(*chip, c) for chip in chips]:
            pl.semaphore_signal(barrier, inc=1, device_id=p, device_id_type=pl.DeviceIdType.MESH)
        pl.semaphore_wait(barrier, 4)

        def rows(a, px, py, pc):
            m_per = shards[a].shape[0]
            return out_refs[a].at[pl.ds((4 * px + 2 * py + pc) * m_per, m_per), :]

        def copy(a, k, block, to, src=None):
            return pltpu.make_async_remote_copy(
                src_ref=rows(a, *block) if src is None else src, dst_ref=rows(a, *block),
                send_sem=send_sems.at[a, k], recv_sem=recv_sems.at[a, k], device_id=to,
                device_id_type=pl.DeviceIdType.MESH)

        mine = [pltpu.make_async_copy(x_refs[a], rows(a, *me), local_sems.at[a]) for a in range(n_arr)]
        for cp in mine:
            cp.start()
        first = []
        for a in range(n_arr):
            first.append(copy(a, 0, me, sibling, src=x_refs[a]))
            first += [copy(a, 1 + j, me, (*chip, c), src=x_refs[a]) for j, chip in enumerate(chips)]
        for cp in first:
            cp.start()
        passed = []
        for j, chip in enumerate(chips):
            for a in range(n_arr):
                copy(a, 1 + j, (*chip, c), me).wait_recv()
                cp = copy(a, 4 + j, (*chip, c), sibling)
                cp.start()
                passed.append(cp)
        for a in range(n_arr):
            copy(a, 0, sibling, me).wait_recv()
        for j, chip in enumerate(chips):
            for a in range(n_arr):
                copy(a, 4 + j, (*chip, 1 - c), me).wait_recv()
        for cp in first + passed:
            cp.wait_send()
        for cp in mine:
            cp.wait()

    launch()
    return [r[...] for r in out_refs]


def _alltoall_async(sends, *, name, collective_id):
    n_arr = len(sends)
    hbm = pltpu.MemorySpace.HBM
    s_refs = [jax.new_ref(a, memory_space=hbm) for a in sends]
    r_refs = [jax.empty_ref(jax.ShapeDtypeStruct(a.shape, a.dtype), memory_space=hbm) for a in sends]

    @pl.kernel(mesh=plsc.ScalarSubcoreMesh(axis_name="seq", num_cores=1), name=name,
               scratch_types=(pltpu.SemaphoreType.DMA((n_arr, 7)), pltpu.SemaphoreType.DMA((n_arr, 7)),
                              pltpu.SemaphoreType.DMA((n_arr,))),
               compiler_params=pltpu.CompilerParams(collective_id=collective_id))
    def launch(send_sems, recv_sems, local_sems):
        x, y, c = lax.axis_index("x"), lax.axis_index("y"), lax.axis_index("c")
        me = 4 * x + 2 * y + c
        peers = [(1 - x if k & 4 else x, 1 - y if k & 2 else y, 1 - c if k & 1 else c) for k in range(1, N_DEV)]
        barrier = pltpu.get_barrier_semaphore()
        for p in peers:
            pl.semaphore_signal(barrier, inc=1, device_id=p, device_id_type=pl.DeviceIdType.MESH)
        pl.semaphore_wait(barrier, N_DEV - 1)

        def rows(ref, a, idx):
            m_per = sends[a].shape[0] // N_DEV
            return ref.at[pl.ds(idx * m_per, m_per), :]

        local = [pltpu.make_async_copy(rows(s_refs[a], a, me), rows(r_refs[a], a, me), local_sems.at[a])
                 for a in range(n_arr)]
        for cp in local:
            cp.start()
        copies = []
        for k, (px, py, pc) in enumerate(peers):
            for a in range(n_arr):
                cp = pltpu.make_async_remote_copy(
                    src_ref=rows(s_refs[a], a, 4 * px + 2 * py + pc), dst_ref=rows(r_refs[a], a, me),
                    send_sem=send_sems.at[a, k], recv_sem=recv_sems.at[a, k],
                    device_id=(px, py, pc), device_id_type=pl.DeviceIdType.MESH)
                cp.start()
                copies.append(cp)
        for cp in copies:
            cp.wait()
        for cp in local:
            cp.wait()

    launch()
    return [r[...] for r in r_refs]


def _reduce_adam(parts, w, m, v, *, tr, name):
    R, C = w.shape
    nR = R // tr
    ins = [(parts, (tr, C), lambda i, s=s: (s * nR + i, 0)) for s in range(N_DEV)]
    ins += [(a, (tr, C), lambda i: (i, 0)) for a in (w, m, v)]
    outs = [((R, C), F32, (tr, C), lambda i: (i, 0)) for _ in range(4)]
    c1 = 1.0 - ADAM_B1 ** ADAM_STEP
    c2 = 1.0 - ADAM_B2 ** ADAM_STEP

    def body(in_refs, out_refs, _):
        g = in_refs[0][...].astype(F32)
        for s in range(1, N_DEV):
            g = g + in_refs[s][...].astype(F32)
        wv, mv, vv = in_refs[8][...], in_refs[9][...], in_refs[10][...]
        mn = ADAM_B1 * mv + (1.0 - ADAM_B1) * g
        vn = ADAM_B2 * vv + (1.0 - ADAM_B2) * (g * g)
        delta = -ADAM_LR * ((mn / c1) / (jnp.sqrt(vn / c2) + ADAM_EPS) + ADAM_WD * wv)
        out_refs[0][...] = g
        out_refs[1][...] = delta
        out_refs[2][...] = mn
        out_refs[3][...] = vn

    return _call(body, ins, outs, (nR,), name=name, semantics=("parallel",))


IN_C, UP_C, UQ_C, QKV_C = 858, 704, 192, 384
A_W, Q_W, V_W = 896, 256, 768
SLAB_TR = {"A": 256, "Q": 128, "C": 368, "V": 16}
SMALL = [("norm_mix_g", 1024), ("gdn_a_log", 8), ("gdn_dt_bias", 8), ("gdn_norm_g", 128), ("mla_q_norm_g", 384),
         ("mla_kv_norm_g", 256), ("norm_ffn_g", 1024), ("norm_final_g", 1024)]
SMALL_ROWS = 32
WEIGHT_ORDER = ["norm_mix_g", "w_in", "conv_qkv_w", "gdn_a_log", "gdn_dt_bias", "gdn_norm_g", "mla_q_norm_g", "w_uq",
                "mla_kv_norm_g", "w_ukv", "w_o_gdn", "w_o_mla", "w_out", "norm_ffn_g", "w_up", "conv_ffn_w", "w_down",
                "norm_final_g"]


def _padc(w, n):
    return jnp.pad(w, ((0, 0), (0, n - w.shape[1])))


def _padrc(w, r, n):
    return jnp.pad(w, ((0, r - w.shape[0]), (0, n - w.shape[1])))


def _slabs(p, dtype):
    A = jnp.concatenate([_padc(p["w_in"], A_W), _padc(p["w_up"], A_W)], axis=0).astype(dtype)
    Q = jnp.concatenate([_padc(p["w_uq"], Q_W), p["w_ukv"]], axis=0).astype(dtype)
    C = jnp.concatenate([p["w_o_gdn"], p["w_o_mla"], p["w_out"], p["w_down"]], axis=0).astype(dtype)
    V = jnp.concatenate([_padrc(p["conv_qkv_w"], 8, V_W), _padrc(p["conv_ffn_w"], 8, V_W)], axis=0).astype(F32)
    return {"A": A, "Q": Q, "C": C, "V": V}


def _unslab(sl):
    A, Q, C, V = sl["A"], sl["Q"], sl["C"], sl["V"]
    out = {"w_in": A[:1024, :IN_C], "w_up": A[1024:, :UP_C], "w_uq": Q[:384, :UQ_C], "w_ukv": Q[384:],
           "w_o_gdn": C[0:128], "w_o_mla": C[128:256], "w_out": C[256:384], "w_down": C[384:],
           "conv_qkv_w": V[0:GDN_CONV, :QKV_C], "conv_ffn_w": V[8:8 + FFN_CONV, :UP_C]}
    return {k: a[None] for k, a in out.items()}


def _take_cols(pieces, lo, hi):
    out, off = [], 0
    for arr, a, b in pieces:
        s, e = max(lo, off), min(hi, off + b - a)
        if s < e:
            out.append(arr[:, a + s - off:a + e - off])
        off += b - a
    return out[0] if len(out) == 1 else jnp.concatenate(out, axis=1)


def _pack_small(d):
    flat = jnp.concatenate([d[n].reshape(-1).astype(F32) for n, _ in SMALL])
    return jnp.pad(flat, (0, SMALL_ROWS * LANES - flat.shape[0])).reshape(SMALL_ROWS, LANES)


def _unpack_small(buf, shapes):
    flat, out, off = buf.reshape(-1), {}, 0
    for name, n in SMALL:
        out[name] = flat[off:off + n].reshape(shapes[name])
        off += n
    return out


def _rot_cols(w):
    h = ROPE // 2
    return jnp.concatenate([-w[:, h:], w[:, :h]], axis=1)


def _unrot_cols(dw):
    h = ROPE // 2
    return jnp.concatenate([dw[:, h:], -dw[:, :h]], axis=1)


IN_SPLITS = [0, 3072, 4096, 4104, 4112, 4496, 4752, 4816, 5840, 6864]


def _layout_weights(g):
    A_in, A_up, Q, C, V = g["A_in"], g["A_up"], g["Q"], g["C"], g["V"]
    in_pieces = [(A_in[j], 0, IN_C) for j in range(N_DEV)]
    o = IN_SPLITS
    take = lambda lo, hi: _take_cols(in_pieces, lo, hi)
    kpe = take(o[6], o[7])
    W = {
        "in_qkv": take(o[0], o[1]),
        "in_ga": take(o[1], o[2]),
        "in_ab": jnp.concatenate([_padc(take(o[2], o[3]), LANES), _padc(take(o[3], o[4]), LANES)], axis=1),
        "in_small": jnp.concatenate([take(o[4], o[6]), _padc(kpe, LANES), _padc(_rot_cols(kpe), LANES)], axis=1),
        "in_gbr": take(o[7], o[9]),
        "w_up": jnp.concatenate([A_up[j, :, :UP_C] for j in range(N_DEV)], axis=1),
        "uq_n": jnp.concatenate([Q[j, :384, :HD] for j in range(N_DEV)], axis=1),
        "ukv_k": jnp.concatenate([Q[j, 384:, :HD] for j in range(N_DEV)], axis=1),
        "ukv_v": jnp.concatenate([Q[j, 384:, HD:] for j in range(N_DEV)], axis=1),
        "w_o_gdn": C[:, 0:128].reshape(1024, D_MODEL),
        "w_o_mla": C[:, 128:256].reshape(1024, D_MODEL),
        "w_out": C[:, 256:384].reshape(1024, D_MODEL),
        "w_down": C[:, 384:].reshape(D_FF, D_MODEL),
    }
    pe = [Q[j, :384, HD:HD + ROPE] for j in range(N_DEV)]
    W["uq_p"] = jnp.concatenate([_padc(p, HD) for p in pe] + [_padc(_rot_cols(p), HD) for p in pe], axis=1)
    conv_qkv = jnp.concatenate([V[j, 0:GDN_CONV, :QKV_C] for j in range(N_DEV)], axis=1)
    conv_ffn = jnp.concatenate([V[j, 8:8 + FFN_CONV, :UP_C] for j in range(N_DEV)], axis=1)
    return {k: v.astype(BF16) for k, v in W.items()}, conv_qkv, conv_ffn


def _full_grads(dW):
    s = dW["in_small"]
    dkpe = s[:, 640:704] + _unrot_cols(s[:, 768:832])
    in_pieces = [(dW["in_qkv"], 0, 3072), (dW["in_ga"], 0, 1024), (dW["in_ab"], 0, 8), (dW["in_ab"], 128, 136),
                 (s, 0, 640), (dkpe, 0, ROPE), (dW["in_gbr"], 0, 2048)]
    pe = []
    for j in range(N_DEV):
        lin = dW["uq_p"][:, j * HD:j * HD + ROPE]
        rot = dW["uq_p"][:, 1024 + j * HD:1024 + j * HD + ROPE]
        pe.append(lin + _unrot_cols(rot))
    return in_pieces, pe


def _send_slabs(dW, d_conv_qkv, d_conv_ffn):
    in_pieces, pe = _full_grads(dW)
    A, Q, V = [], [], []
    for j in range(N_DEV):
        gin = _padc(_take_cols(in_pieces, j * IN_C, (j + 1) * IN_C), A_W)
        gup = _padc(dW["w_up"][:, j * UP_C:(j + 1) * UP_C], A_W)
        A.append(jnp.concatenate([gin, gup], axis=0))
        guq = _padc(jnp.concatenate([dW["uq_n"][:, j * HD:(j + 1) * HD], pe[j]], axis=1), Q_W)
        gukv = jnp.concatenate([dW["ukv_k"][:, j * HD:(j + 1) * HD], dW["ukv_v"][:, j * HD:(j + 1) * HD]], axis=1)
        Q.append(jnp.concatenate([guq, gukv], axis=0))
        V.append(jnp.concatenate([_padrc(d_conv_qkv[:, j * QKV_C:(j + 1) * QKV_C], 8, V_W),
                                  _padrc(d_conv_ffn[:, j * UP_C:(j + 1) * UP_C], 8, V_W)], axis=0))
    C = jnp.concatenate([dW["w_o_gdn"].reshape(N_DEV, 128, D_MODEL), dW["w_o_mla"].reshape(N_DEV, 128, D_MODEL),
                         dW["w_out"].reshape(N_DEV, 128, D_MODEL), dW["w_down"].reshape(N_DEV, 352, D_MODEL)], axis=1)
    return {"A": jnp.concatenate(A, axis=0).astype(BF16), "Q": jnp.concatenate(Q, axis=0).astype(BF16),
            "C": C.reshape(N_DEV * 736, D_MODEL).astype(BF16), "V": jnp.concatenate(V, axis=0)}


def _rope_tables(S):
    half = ROPE // 2
    inv = ROPE_THETA ** (-jnp.arange(half, dtype=F32) / half)
    ang = jnp.arange(S, dtype=F32)[:, None] * inv[None, :]
    cos = jnp.concatenate([jnp.cos(ang), jnp.cos(ang)], axis=1)
    sin = jnp.concatenate([jnp.sin(ang), jnp.sin(ang)], axis=1)
    return _padc(cos, HD), _padc(sin, HD)


def _local_step(x, tgt, W, conv_qkv_w, conv_ffn_w, small, tm=None, ta=None):
    B, S, _ = x.shape
    T = B * S
    tm = tm or _pick(S, 256, CHUNK)
    ta = ta or _pick(S, 512, LANES)
    x2d, tgt2d = x.reshape(T, D_MODEL), tgt.reshape(T, D_MODEL)
    row = lambda v: v.reshape(1, -1).astype(F32)
    pad_row = lambda v: _padc(row(v), LANES)
    g_mix, g_ffn, g_fin = row(small["norm_mix_g"]), row(small["norm_ffn_g"]), row(small["norm_final_g"])
    g_gdn, g_q, g_kv = row(small["gdn_norm_g"]), row(small["mla_q_norm_g"]), row(small["mla_kv_norm_g"])
    alog, dtb = pad_row(small["gdn_a_log"]), pad_row(small["gdn_dt_bias"])
    cos, sin = _rope_tables(S)
    tps = S // tm
    tab = lambda a: (a, (tm, HD), lambda i: (i % tps, 0))
    col = lambda a, c, w: (a, (tm, w), lambda i, c=c: (i, c))

    h1 = _norm_fwd(x2d, g_mix, T=T, tm=tm, name="norm_mix_fwd")
    z_qkv = _mm(h1, W["in_qkv"], "nn", BF16, name="in_qkv_fwd")
    z_ga = _mm(h1, W["in_ga"], "nn", BF16, name="in_ga_fwd")
    z_ab = _mm(h1, W["in_ab"], "nn", F32, name="in_ab_fwd")
    z_small = _mm(h1, W["in_small"], "nn", F32, name="in_small_fwd", tn=896)
    z_gbr = _mm(h1, W["in_gbr"], "nn", BF16, name="in_gbr_fwd")

    qkvn = _conv_fwd(_qkv_fn, [(z_qkv, 0)], [(conv_qkv_w, 0)], 3072, BF16, T=T, S=S, tm=tm, cb=QKV_CB,
                     ncb=3072 // QKV_CB, name="gdn_qkv_fwd")
    gcum, beta = _row_call(lambda za, zb, al, db: _gate_fn(za, zb, al, db), [col(z_ab, 0, LANES), col(z_ab, 1, LANES)],
                           [alog, dtb], [(LANES, F32), (LANES, F32)], T=T, tm=tm, name="gdn_gate_fwd")
    grT = gcum[:, :HEADS].reshape(T // CHUNK, CHUNK, HEADS).transpose(0, 2, 1)[:, :, None, :]
    o_gdn, states = _gdn_fwd(qkvn, gcum, grT, beta, B=B, S=S)

    def gdn_out_fn(o, ga, g):
        parts = []
        for h in range(HEADS):
            sl = slice(h * HD, (h + 1) * HD)
            parts.append(_rms(o[:, sl], g) * jax.nn.silu(ga[:, sl].astype(F32)))
        return jnp.concatenate(parts, axis=1)

    oa = _row_call(lambda o, ga, g: (gdn_out_fn(o, ga, g),), [o_gdn, z_ga], [g_gdn], [(1024, BF16)], T=T, tm=tm,
                   name="gdn_out_fwd")[0]

    def mla_prep_fn(zq, zkv, zpl, zpr, c, s, gq, gkv):
        return _rms(zq, gq), _rms(zkv, gkv), zpl * c + zpr * s

    small_cols = [(z_small, (tm, Q_RANK), lambda i: (i, 0)), (z_small, (tm, LANES), lambda i: (i, 3)),
                  (z_small, (tm, LANES), lambda i: (i, 4)), (z_small, (tm, LANES), lambda i: (i, 5)),
                  (z_small, (tm, LANES), lambda i: (i, 6))]

    def mla_prep_fwd(zq, zkv0, zkv1, zpl, zpr, c, s, gq, gkv):
        return mla_prep_fn(zq, jnp.concatenate([zkv0, zkv1], axis=1), zpl, zpr, c, s, gq, gkv)

    cq, ckv, kpe = _row_call(mla_prep_fwd, small_cols + [tab(cos), tab(sin)], [g_q, g_kv],
                             [(Q_RANK, BF16), (KV_RANK, BF16), (HD, BF16)], T=T, tm=tm, name="mla_prep_fwd")
    qn = _mm(cq, W["uq_n"], "nn", BF16, name="uq_n_fwd")
    qpl = _mm(cq, W["uq_p"], "nn", F32, name="uq_p_fwd")
    kn = _mm(ckv, W["ukv_k"], "nn", BF16, name="ukv_k_fwd")
    vb = _mm(ckv, W["ukv_v"], "nn", BF16, name="ukv_v_fwd")

    def qrope_fn(lin, rot, c, s):
        return lin * jnp.tile(c, (1, HEADS)) + rot * jnp.tile(s, (1, HEADS))

    qp = _row_call(lambda lin, rot, c, s: (qrope_fn(lin, rot, c, s),), [col(qpl, 0, 1024), col(qpl, 1, 1024), tab(cos), tab(sin)],
                   [], [(1024, BF16)], T=T, tm=tm, name="q_rope_fwd")[0]
    ob, lse = _flash_fwd(qn, qp, kn, kpe, vb, B=B, S=S, t=ta)

    def merge_fn(ya, yb, ga, gb):
        return jax.nn.sigmoid(ga.astype(F32)) * ya + jax.nn.sigmoid(gb.astype(F32)) * yb

    def merge_fwd(oat, obt, ga, gb, wog, wom):
        ya, yb = _dot(oat, wog), _dot(obt, wom)
        return ya, yb, merge_fn(ya, yb, ga, gb)

    ya, yb, merged = _row_call(merge_fwd, [oa, ob, col(z_gbr, 0, 1024), col(z_gbr, 1, 1024)], [W["w_o_gdn"], W["w_o_mla"]],
                               [(1024, BF16), (1024, BF16), (1024, BF16)], T=T, tm=tm, name="merge_fwd")
    x1 = _mm(merged, W["w_out"], "nn", F32, add=x2d, name="w_out_fwd")

    h2 = _norm_fwd(x1, g_ffn, T=T, tm=tm, name="norm_ffn_fwd")
    up = _mm(h2, W["w_up"], "nn", BF16, name="w_up_fwd")
    FCB = 256
    nfb = D_FF // FCB
    f = _conv_fwd(_ffn_fn, [(up, 0), (up, 2)], [(conv_ffn_w, 0), (conv_ffn_w, 2)], D_FF, BF16, T=T, S=S, tm=tm,
                  cb=D_FF // 2, ncb=2, name="ffn_act_fwd")
    x2 = _mm(f, W["w_down"], "nn", F32, add=x1, name="w_down_fwd", tk=1408)

    def final_fn(xt, tt, g):
        def lossf(xv, gv):
            e = _rms(xv, gv) - tt
            return 0.5 * jnp.sum(jnp.mean(e * e, axis=-1))

        l, vjp = jax.vjp(lossf, xt, g)
        dx, dg = vjp(jnp.ones((), F32))
        return dx, jnp.full((1, LANES), l, F32), dg

    dx2, loss_v, dg_fin = _row_call(final_fn, [x2, tgt2d], [g_fin], [(1024, F32)], [((1, LANES), F32), ((1, 1024), F32)],
                                    T=T, tm=tm, name="loss_head")

    dW = {}
    df = _mm(dx2, W["w_down"], "nt", BF16, name="w_down_dx")
    dW["w_down"] = _mm(f, dx2, "tn", F32, name="w_down_dw")
    dug, duu, dcw_g, dcw_u = _conv_bwd(_ffn_fn, [(up, 0), (up, nfb)], [(conv_ffn_w, 0), (conv_ffn_w, nfb)], df, BF16,
                                       T=T, S=S, tm=tm, cb=FCB, ncb=nfb, name="ffn_act_bwd")
    d_conv_ffn = jnp.concatenate([dcw_g, dcw_u], axis=1)
    wup_g, wup_u = W["w_up"][:, :D_FF], W["w_up"][:, D_FF:]
    dh2 = _mm(dug, wup_g, "nt", F32, name="w_up_dx_g")
    dh2 = _mm(duu, wup_u, "nt", F32, add=dh2, name="w_up_dx_u")
    dW["w_up"] = jnp.concatenate([_mm(h2, dug, "tn", F32, name="w_up_dw_g"), _mm(h2, duu, "tn", F32, name="w_up_dw_u")], axis=1)
    dx1, dg_ffn = _norm_bwd(x1, g_ffn, dh2, dx2, T=T, tm=tm, name="norm_ffn_bwd")

    dmerged = _mm(dx1, W["w_out"], "nt", F32, name="w_out_dx")
    dW["w_out"] = _mm(merged, dx1, "tn", F32, name="w_out_dw")

    def merge_bwd(dm, yat, ybt, ga, gb):
        _, vjp = jax.vjp(merge_fn, yat.astype(F32), ybt.astype(F32), ga, gb)
        return vjp(dm)

    dya, dyb, dgbr_a, dgbr_b = _row_call(merge_bwd, [dmerged, ya, yb, col(z_gbr, 0, 1024), col(z_gbr, 1, 1024)], [],
                                         [(1024, BF16)] * 4, T=T, tm=tm, name="merge_bwd")
    doa = _mm(dya, W["w_o_gdn"], "nt", F32, name="w_o_gdn_dx")
    dob = _mm(dyb, W["w_o_mla"], "nt", BF16, name="w_o_mla_dx")
    dW["w_o_gdn"] = _mm(oa, dya, "tn", F32, name="w_o_gdn_dw")
    dW["w_o_mla"] = _mm(ob, dyb, "tn", F32, name="w_o_mla_dw")

    dqn, dqp, dl = _flash_bwd_dq(qn, qp, kn, kpe, vb, ob, dob, lse, B=B, S=S, t=ta)
    dkn, dkp, dvb = _flash_bwd_dkv(qn, qp, kn, kpe, vb, dob, lse.reshape(HEADS, 1, T), dl.reshape(HEADS, 1, T),
                                   B=B, S=S, t=ta)

    def qrope_bwd(d, c, s):
        return d * jnp.tile(c, (1, HEADS)), d * jnp.tile(s, (1, HEADS))

    dq_lin, dq_rot = _row_call(qrope_bwd, [dqp, tab(cos), tab(sin)], [], [(1024, BF16), (1024, BF16)], T=T, tm=tm,
                               name="q_rope_bwd")
    wp_lin, wp_rot = W["uq_p"][:, :1024], W["uq_p"][:, 1024:]
    dcq = _mm(dqn, W["uq_n"], "nt", F32, name="uq_n_dx")
    dcq = _mm(dq_lin, wp_lin, "nt", F32, add=dcq, name="uq_pl_dx")
    dcq = _mm(dq_rot, wp_rot, "nt", F32, add=dcq, name="uq_pr_dx")
    dW["uq_n"] = _mm(cq, dqn, "tn", F32, name="uq_n_dw")
    dW["uq_p"] = jnp.concatenate([_mm(cq, dq_lin, "tn", F32, name="uq_pl_dw"), _mm(cq, dq_rot, "tn", F32, name="uq_pr_dw")], axis=1)
    dckv = _mm(dkn, W["ukv_k"], "nt", F32, name="ukv_k_dx")
    dckv = _mm(dvb, W["ukv_v"], "nt", F32, add=dckv, name="ukv_v_dx")
    dW["ukv_k"] = _mm(ckv, dkn, "tn", F32, name="ukv_k_dw")
    dW["ukv_v"] = _mm(ckv, dvb, "tn", F32, name="ukv_v_dw")

    def mla_prep_bwd(zq, zkv0, zkv1, zpl, zpr, c, s, dcqt, dckvt, dkpt, gq, gkv):
        zkv = jnp.concatenate([zkv0, zkv1], axis=1)
        _, vjp = jax.vjp(lambda a, b, p, r, g1, g2: mla_prep_fn(a, b, p, r, c, s, g1, g2), zq, zkv, zpl, zpr, gq, gkv)
        dk = dkpt[0]
        for h in range(1, HEADS):
            dk = dk + dkpt[h]
        dzq, dzkv, dzpl, dzpr, dgq, dgkv = vjp((dcqt, dckvt, dk))
        return jnp.concatenate([dzq, dzkv, dzpl, dzpr], axis=1), dgq, dgkv

    dz_small, dg_q, dg_kv = _row_call(
        mla_prep_bwd, small_cols + [tab(cos), tab(sin), dcq, dckv, (dkp, (HEADS, tm, HD), lambda i: (0, i, 0))],
        [g_q, g_kv], [(896, BF16)], [((1, Q_RANK), F32), ((1, KV_RANK), F32)], T=T, tm=tm, name="mla_prep_bwd")

    def gdn_out_bwd(o, ga, dot_, g):
        _, vjp = jax.vjp(gdn_out_fn, o, ga, g)
        return vjp(dot_)

    do_gdn, dz_ga, dg_gdn = _row_call(gdn_out_bwd, [o_gdn, z_ga, doa], [g_gdn], [(1024, F32), (1024, BF16)],
                                      [((1, HD), F32)], T=T, tm=tm, name="gdn_out_bwd")
    dqkvn, dgc, dgrT, dbeta = _gdn_bwd(qkvn, gcum, grT, beta, states, do_gdn, B=B, S=S)
    dgc_tot = dgc + _padc(dgrT[:, :, 0, :].transpose(0, 2, 1).reshape(T, HEADS), LANES)

    def gate_bwd(za, zb, dg, db, al, db_):
        _, vjp = jax.vjp(_gate_fn, za, zb, al, db_)
        return vjp((dg, db))

    dz_a, dz_b, d_alog, d_dtb = _row_call(gate_bwd, [col(z_ab, 0, LANES), col(z_ab, 1, LANES), dgc_tot, dbeta], [alog, dtb],
                                          [(LANES, BF16), (LANES, BF16)], [((1, LANES), F32), ((1, LANES), F32)],
                                          T=T, tm=tm, name="gdn_gate_bwd")
    dz_qkv, d_conv_qkv = _conv_bwd(_qkv_fn, [(z_qkv, 0)], [(conv_qkv_w, 0)], dqkvn, BF16, T=T, S=S, tm=tm, cb=QKV_CB,
                                   ncb=3072 // QKV_CB, name="gdn_qkv_bwd")

    dz_ab = jnp.concatenate([dz_a, dz_b], axis=1)
    dz_gbr = jnp.concatenate([dgbr_a, dgbr_b], axis=1)
    dh1 = None
    for key, dz in (("in_qkv", dz_qkv), ("in_ga", dz_ga), ("in_ab", dz_ab), ("in_small", dz_small), ("in_gbr", dz_gbr)):
        dh1 = _mm(dz, W[key], "nt", F32, add=dh1, name=key + "_dx", tk=896 if key == "in_small" else 1024)
        dW[key] = _mm(h1, dz, "tn", F32, name=key + "_dw", tn=896 if key == "in_small" else 1024)
    dx, dg_mix = _norm_bwd(x2d, g_mix, dh1, dx1, T=T, tm=tm, name="norm_mix_bwd")

    dsmall = {"norm_mix_g": dg_mix, "gdn_a_log": d_alog[:, :HEADS], "gdn_dt_bias": d_dtb[:, :HEADS], "gdn_norm_g": dg_gdn,
              "mla_q_norm_g": dg_q, "mla_kv_norm_g": dg_kv, "norm_ffn_g": dg_ffn, "norm_final_g": dg_fin}
    return loss_v[0, 0], dx.reshape(B, S, D_MODEL), dW, d_conv_qkv, d_conv_ffn, dsmall


def kernel(x, norm_mix_g, w_in, conv_qkv_w, gdn_a_log, gdn_dt_bias, gdn_norm_g, mla_q_norm_g, w_uq, mla_kv_norm_g, w_ukv, w_o_gdn, w_o_mla, w_out, norm_ffn_g, w_up, conv_ffn_w, w_down, norm_final_g, loss_target, m_norm_mix_g, m_w_in, m_conv_qkv_w, m_gdn_a_log, m_gdn_dt_bias, m_gdn_norm_g, m_mla_q_norm_g, m_w_uq, m_mla_kv_norm_g, m_w_ukv, m_w_o_gdn, m_w_o_mla, m_w_out, m_norm_ffn_g, m_w_up, m_conv_ffn_w, m_w_down, m_norm_final_g, v_norm_mix_g, v_w_in, v_conv_qkv_w, v_gdn_a_log, v_gdn_dt_bias, v_gdn_norm_g, v_mla_q_norm_g, v_w_uq, v_mla_kv_norm_g, v_w_ukv, v_w_o_gdn, v_w_o_mla, v_w_out, v_norm_ffn_g, v_w_up, v_conv_ffn_w, v_w_down, v_norm_final_g):
    w = dict(norm_mix_g=norm_mix_g, w_in=w_in, conv_qkv_w=conv_qkv_w, gdn_a_log=gdn_a_log, gdn_dt_bias=gdn_dt_bias,
             gdn_norm_g=gdn_norm_g, mla_q_norm_g=mla_q_norm_g, w_uq=w_uq, mla_kv_norm_g=mla_kv_norm_g, w_ukv=w_ukv,
             w_o_gdn=w_o_gdn, w_o_mla=w_o_mla, w_out=w_out, norm_ffn_g=norm_ffn_g, w_up=w_up, conv_ffn_w=conv_ffn_w,
             w_down=w_down, norm_final_g=norm_final_g)
    m = dict(norm_mix_g=m_norm_mix_g, w_in=m_w_in, conv_qkv_w=m_conv_qkv_w, gdn_a_log=m_gdn_a_log, gdn_dt_bias=m_gdn_dt_bias,
             gdn_norm_g=m_gdn_norm_g, mla_q_norm_g=m_mla_q_norm_g, w_uq=m_w_uq, mla_kv_norm_g=m_mla_kv_norm_g, w_ukv=m_w_ukv,
             w_o_gdn=m_w_o_gdn, w_o_mla=m_w_o_mla, w_out=m_w_out, norm_ffn_g=m_norm_ffn_g, w_up=m_w_up,
             conv_ffn_w=m_conv_ffn_w, w_down=m_w_down, norm_final_g=m_norm_final_g)
    v = dict(norm_mix_g=v_norm_mix_g, w_in=v_w_in, conv_qkv_w=v_conv_qkv_w, gdn_a_log=v_gdn_a_log, gdn_dt_bias=v_gdn_dt_bias,
             gdn_norm_g=v_gdn_norm_g, mla_q_norm_g=v_mla_q_norm_g, w_uq=v_w_uq, mla_kv_norm_g=v_mla_kv_norm_g, w_ukv=v_w_ukv,
             w_o_gdn=v_w_o_gdn, w_o_mla=v_w_o_mla, w_out=v_w_out, norm_ffn_g=v_norm_ffn_g, w_up=v_w_up,
             conv_ffn_w=v_conv_ffn_w, w_down=v_w_down, norm_final_g=v_norm_final_g)
    slab_names = ("A", "Q", "C", "V")
    big_names = ("w_in", "w_up", "w_uq", "w_ukv", "w_o_gdn", "w_o_mla", "w_out", "w_down", "conv_qkv_w", "conv_ffn_w")
    small_names = [n for n, _ in SMALL]
    small_shapes = {n: w[n].shape for n in small_names}
    local2d = lambda d: {n: d[n][0] for n in big_names}

    w_slabs = _slabs(local2d(w), F32)
    a_bf = w_slabs["A"].astype(BF16)
    first = _allgather_async([a_bf[:1024]], name="allgather_w_in", collective_id=1)
    second = _allgather_async([w_slabs["Q"].astype(BF16), w_slabs["V"]], name="allgather_mixers", collective_id=2)
    third = _allgather_async([a_bf[1024:], w_slabs["C"].astype(BF16)], name="allgather_ffn_out", collective_id=3)
    gathered = {k: g.reshape(N_DEV, -1, g.shape[1])
                for k, g in zip(("A_in", "Q", "V", "A_up", "C"), first + second + third)}
    W, conv_qkv_full, conv_ffn_full = _layout_weights(gathered)

    loss_local, dx, dW, d_conv_qkv, d_conv_ffn, dsmall = _local_step(
        x, loss_target, W, conv_qkv_full, conv_ffn_full, {n: w[n] for n in small_names})

    g_send = _send_slabs(dW, d_conv_qkv, d_conv_ffn)
    recv = _alltoall_async([g_send[k] for k in slab_names], name="alltoall_grads", collective_id=0)
    small_parts = _alltoall_async([jnp.tile(_pack_small(dsmall), (N_DEV, 1))], name="alltoall_small_grads",
                                  collective_id=4)[0]
    m_slabs, v_slabs = _slabs(local2d(m), F32), _slabs(local2d(v), F32)
    upd = {k: _reduce_adam(r, w_slabs[k], m_slabs[k], v_slabs[k], tr=SLAB_TR[k], name="adam_" + k)
           for k, r in zip(slab_names, recv)}
    upd_small = _reduce_adam(small_parts, _pack_small({n: w[n] for n in small_names}), _pack_small({n: m[n] for n in small_names}),
                             _pack_small({n: v[n] for n in small_names}), tr=SMALL_ROWS, name="adam_small")

    loss = lax.psum(loss_local, ("x", "y", "c"))
    groups = []
    for i in range(4):
        merged = {**_unslab({k: upd[k][i] for k in slab_names}), **_unpack_small(upd_small[i], small_shapes)}
        groups.append([merged[n] for n in WEIGHT_ORDER])
    return (loss, dx, *groups[0], *groups[1], *groups[2], *groups[3])
```

```python
import functools
import math

import numpy as np
import jax
import jax.numpy as jnp
from jax import lax
from jax.experimental import pallas as pl
from jax.experimental.pallas import tpu as pltpu
from jax.experimental.pallas import tpu_sc as plsc

F32 = jnp.float32
BF16 = jnp.bfloat16

D_MODEL = 1024
HEADS = 8
HD = 128
GDN_CONV = 4
CHUNK = 64
Q_RANK = 384
KV_RANK = 256
ROPE = 64
ROPE_THETA = 10000.0
D_FF = 2816
FFN_CONV = 3
EPS = 1e-6
SM_SCALE = (HD + ROPE) ** -0.5
N_DEV = 8

ADAM_LR, ADAM_B1, ADAM_B2, ADAM_EPS, ADAM_WD, ADAM_STEP = 0.001, 0.9, 0.999, 1e-08, 0.01, 10

LANES = 128
SUBLANES = 8
HALO = 2 * SUBLANES
VMEM_LIMIT = 56 * 1024 * 1024
HI = lax.Precision.HIGHEST
TRI_PRECISION = None

NN = (((1,), (0,)), ((), ()))
NT = (((1,), (1,)), ((), ()))
TN = (((0,), (0,)), ((), ()))


def _dot(a, b, dims=NN, precision=None):
    return lax.dot_general(a, b, dims, precision=precision, preferred_element_type=F32)


def _pick(dim, target, align):
    best = None
    for t in range(align, min(dim, target) + 1, align):
        if dim % t == 0:
            best = t
    return dim if best is None else best


def _call(body, ins, outs, grid, *, name, scratch=(), semantics=None):
    n_in, n_out = len(ins), len(outs)

    def kern(*refs):
        body(refs[:n_in], refs[n_in:n_in + n_out], refs[n_in + n_out:])

    res = pl.pallas_call(
        kern,
        grid=grid,
        in_specs=[pl.BlockSpec(bs, im) for _, bs, im in ins],
        out_specs=[pl.BlockSpec(bs, im) for _, _, bs, im in outs],
        out_shape=[jax.ShapeDtypeStruct(s, d) for s, d, _, _ in outs],
        scratch_shapes=list(scratch),
        name=name,
        compiler_params=pltpu.CompilerParams(
            dimension_semantics=semantics or ("arbitrary",) * len(grid), vmem_limit_bytes=VMEM_LIMIT),
    )(*[a for a, _, _ in ins])
    return res


def _mm(a, b, mode, out_dtype, *, name, add=None, tm=1408, tn=1408, tk=1408):
    if mode == "nn":
        (M, K), (K2, N) = a.shape, b.shape
    elif mode == "nt":
        (M, K), (N, K2) = a.shape, b.shape
    else:
        (K, M), (K2, N) = a.shape, b.shape
    assert K == K2, (a.shape, b.shape, mode)
    tm = _pick(M, tm, LANES if mode == "tn" else 16)
    tn = _pick(N, tn, LANES)
    tk = _pick(K, tk, 16 if mode == "tn" else LANES)
    nk = K // tk
    dims = {"nn": NN, "nt": NT, "tn": TN}[mode]
    if mode == "nn":
        a_spec, b_spec = ((tm, tk), lambda i, j, k: (i, k)), ((tk, tn), lambda i, j, k: (k, j))
    elif mode == "nt":
        a_spec, b_spec = ((tm, tk), lambda i, j, k: (i, k)), ((tn, tk), lambda i, j, k: (j, k))
    else:
        a_spec, b_spec = ((tk, tm), lambda i, j, k: (k, i)), ((tk, tn), lambda i, j, k: (k, j))
    ins = [(a,) + a_spec, (b,) + b_spec]
    if add is not None:
        ins.append((add, (tm, tn), lambda i, j, k: (i, j)))
    outs = [((M, N), out_dtype, (tm, tn), lambda i, j, k: (i, j))]

    def body(in_refs, out_refs, scr):
        prod = _dot(in_refs[0][...].astype(BF16), in_refs[1][...].astype(BF16), dims)

        def finish(r):
            if add is not None:
                r = r + in_refs[2][...].astype(F32)
            out_refs[0][...] = r.astype(out_dtype)

        if nk == 1:
            finish(prod)
            return
        k = pl.program_id(2)
        acc = scr[0]

        @pl.when(k == 0)
        def _():
            acc[...] = prod

        @pl.when(k > 0)
        def _():
            acc[...] += prod

        @pl.when(k == nk - 1)
        def _():
            finish(acc[...])

    return _call(body, ins, outs, (M // tm, N // tn, nk), name=name,
                 scratch=[pltpu.VMEM((tm, tn), F32)] if nk > 1 else [],
                 semantics=("parallel", "parallel", "arbitrary"))[0]


def _row_call(fn, rows, consts, out_rows, out_accs=(), *, T, tm, name):
    nt = T // tm
    ins = []
    for r in rows:
        ins.append(r if isinstance(r, tuple) else (r, (tm, r.shape[1]), lambda i: (i, 0)))
    for c in consts:
        ins.append((c, c.shape, lambda i, nd=c.ndim: (0,) * nd))
    outs = []
    for o in out_rows:
        outs.append(((T, o[0]), o[1], (tm, o[0]), lambda i: (i, 0)) if len(o) == 2 else o)
    for shp, dt in out_accs:
        outs.append((shp, dt, shp, lambda i, nd=len(shp): (0,) * nd))
    n_r = len(out_rows)

    def body(in_refs, out_refs, _):
        i = pl.program_id(0)
        vals = fn(*[r[...] for r in in_refs])
        for o_ref, v in zip(out_refs[:n_r], vals[:n_r]):
            o_ref[...] = v.astype(o_ref.dtype)
        for o_ref, v in zip(out_refs[n_r:], vals[n_r:]):
            @pl.when(i == 0)
            def _(o_ref=o_ref):
                o_ref[...] = jnp.zeros_like(o_ref)

            o_ref[...] += v.astype(o_ref.dtype)

    return _call(body, ins, outs, (nt,), name=name)


def _rms(x, g):
    return x * lax.rsqrt(jnp.mean(x * x, axis=-1, keepdims=True) + EPS) * g


def _norm_fwd(x, g, *, T, tm, name):
    return _row_call(lambda xt, gt: (_rms(xt, gt),), [x], [g], [(x.shape[1], BF16)], T=T, tm=tm, name=name)[0]


def _norm_bwd(x, g, dh, dres, *, T, tm, name):
    def fn(xt, dht, drt, gt):
        _, vjp = jax.vjp(_rms, xt, gt)
        dx, dg = vjp(dht)
        return drt + dx, dg

    return _row_call(fn, [x, dh, dres], [g], [(x.shape[1], F32)], [(g.shape, F32)], T=T, tm=tm, name=name)


def _rows16(c):
    return lax.broadcasted_iota(jnp.int32, (HALO, c), 0)


@functools.lru_cache(maxsize=None)
def _shift_fn(j):
    @jax.custom_vjp
    def shift(x, halo):
        xr = pltpu.roll(x, j, 0)
        top = jnp.where(_rows16(x.shape[1]) < j, pltpu.roll(halo, j, 0), xr[:HALO])
        return jnp.concatenate([top, xr[HALO:]], axis=0)

    def fwd(x, halo):
        return shift(x, halo), None

    def bwd(_, dy):
        tm, c = dy.shape
        keep = _rows16(c) >= HALO - j
        dxr = pltpu.roll(dy, tm - j, 0)
        dx = jnp.concatenate([dxr[:tm - HALO], jnp.where(keep, 0.0, dxr[tm - HALO:])], axis=0)
        dhalo = jnp.where(keep, pltpu.roll(dy[:HALO], HALO - j, 0), 0.0)
        return dx, dhalo

    shift.defvjp(fwd, bwd)
    return shift


def _dwconv(tail, x, w):
    K = w.shape[0]
    acc = w[K - 1:K, :] * x
    for k in range(K - 1):
        acc = acc + w[k:k + 1, :] * _shift_fn(K - 1 - k)(x, tail)
    return acc


STRIP = 64


def _conv_fwd(fn, xs, ws, out_c, out_dtype, *, T, S, tm, cb, ncb, name):
    nt, tps, hb = T // tm, S // tm, tm // HALO
    ins = []
    for arr, off in xs:
        ins.append((arr, (tm, cb), lambda j, i, off=off: (i, off + j)))
        ins.append((arr, (HALO, cb), lambda j, i, off=off: (jnp.maximum(i * hb - 1, 0), off + j)))
    for arr, off in ws:
        ins.append((arr, (arr.shape[0], cb), lambda j, i, off=off: (0, off + j)))
    outs = [((T, out_c), out_dtype, (tm, cb), lambda j, i: (i, j))]
    nx = len(xs)

    def body(in_refs, out_refs, _):
        j, i = pl.program_id(0), pl.program_id(1)
        first = (i % tps) == 0
        wts = [r[...] for r in in_refs[2 * nx:]]
        for r in range(0, tm, STRIP):
            xts = [in_refs[2 * m][r:r + STRIP, :].astype(F32) for m in range(nx)]
            if r == 0:
                tails = [jnp.where(first, 0.0, in_refs[2 * m + 1][...].astype(F32)) for m in range(nx)]
            else:
                tails = [in_refs[2 * m][r - HALO:r, :].astype(F32) for m in range(nx)]
            out_refs[0][r:r + STRIP, :] = fn(j, tails, xts, wts).astype(out_dtype)

    return _call(body, ins, outs, (ncb, nt), name=name)[0]


def _conv_bwd(fn, xs, ws, dout, dx_dtype, *, T, S, tm, cb, ncb, name):
    nt, tps, hb = T // tm, S // tm, tm // HALO
    ins = []
    for arr, off in xs:
        ins.append((arr, (tm, cb), lambda j, i, off=off: (nt - 1 - i, off + j)))
        ins.append((arr, (HALO, cb), lambda j, i, off=off: (jnp.maximum((nt - 1 - i) * hb - 1, 0), off + j)))
    for arr, off in ws:
        ins.append((arr, (arr.shape[0], cb), lambda j, i, off=off: (0, off + j)))
    ins.append((dout, (tm, cb), lambda j, i: (nt - 1 - i, j)))
    nx, nw = len(xs), len(ws)
    outs = [((T, ncb * cb), dx_dtype, (tm, cb), lambda j, i: (nt - 1 - i, j)) for _ in xs]
    outs += [((arr.shape[0], ncb * cb), F32, (arr.shape[0], cb), lambda j, i: (0, j)) for arr, _ in ws]
    scratch = [pltpu.VMEM((HALO, cb), F32) for _ in xs]

    def body(in_refs, out_refs, carry):
        j, i = pl.program_id(0), pl.program_id(1)
        first = ((nt - 1 - i) % tps) == 0
        wts = [ref[...] for ref in in_refs[2 * nx:2 * nx + nw]]

        @pl.when(i == 0)
        def _():
            for c in carry:
                c[...] = jnp.zeros_like(c)

        carried = [c[...] for c in carry]
        dw_sum = None
        for r in reversed(range(0, tm, STRIP)):
            xts = [in_refs[2 * m][r:r + STRIP, :].astype(F32) for m in range(nx)]
            if r == 0:
                tails = [jnp.where(first, 0.0, in_refs[2 * m + 1][...].astype(F32)) for m in range(nx)]
            else:
                tails = [in_refs[2 * m][r - HALO:r, :].astype(F32) for m in range(nx)]
            _, vjp = jax.vjp(lambda tl, xt, wt: fn(j, tl, xt, wt), tails, xts, wts)
            dtails, dxts, dwts = vjp(in_refs[-1][r:r + STRIP, :].astype(F32))
            for m in range(nx):
                pad = jnp.concatenate([jnp.zeros((STRIP - HALO, cb), F32), carried[m]], axis=0)
                out_refs[m][r:r + STRIP, :] = (dxts[m] + pad).astype(dx_dtype)
            carried = [jnp.where(first, 0.0, dt) for dt in dtails] if r == 0 else list(dtails)
            dw_sum = list(dwts) if dw_sum is None else [a + b for a, b in zip(dw_sum, dwts)]
        for m in range(nx):
            carry[m][...] = carried[m]
        for m in range(nw):
            o_ref = out_refs[nx + m]

            @pl.when(i == 0)
            def _(o_ref=o_ref):
                o_ref[...] = jnp.zeros_like(o_ref)

            o_ref[...] += dw_sum[m]

    return _call(body, ins, outs, (ncb, nt), name=name, scratch=scratch)


QKV_CB = 512


def _qkv_fn(j, tails, xts, wts):
    y = jax.nn.silu(_dwconv(tails[0], xts[0], wts[0]))
    scale = jnp.where(j < 1024 // QKV_CB, HD ** -0.5, 1.0)
    parts = []
    for h in range(QKV_CB // HD):
        yh = y[:, h * HD:(h + 1) * HD]
        nh = yh * lax.rsqrt(jnp.sum(yh * yh, axis=-1, keepdims=True) + EPS)
        parts.append(jnp.where(j < 2048 // QKV_CB, nh * scale, yh))
    return jnp.concatenate(parts, axis=1)


def _ffn_fn(j, tails, xts, wts):
    return jax.nn.silu(_dwconv(tails[0], xts[0], wts[0])) * _dwconv(tails[1], xts[1], wts[1])


BNN = (((2,), (1,)), ((0,), (0,)))
BNT = (((2,), (2,)), ((0,), (0,)))
BTN = (((1,), (1,)), ((0,), (0,)))


@jax.custom_vjp
def _tri_inv(L):
    C = L.shape[-1]
    ii = lax.broadcasted_iota(jnp.int32, (C, C), 0)
    jj = lax.broadcasted_iota(jnp.int32, (C, C), 1)
    eye = (ii == jj).astype(F32)
    X = eye - jnp.where((ii >> 1) == (jj >> 1), L, 0.0)
    s = 1
    while (2 << s) <= C:
        E = jnp.where(((ii >> (s + 1)) == (jj >> (s + 1))) & ((ii >> s) != (jj >> s)), L, 0.0)
        X = X - _dot(_dot(X, E, BNN, precision=TRI_PRECISION), X, BNN, precision=TRI_PRECISION)
        s += 1
    return X


def _tri_inv_fwd(L):
    X = _tri_inv(L)
    return X, X


def _tri_inv_bwd(X, dX):
    return (-_dot(_dot(X, dX, BTN, precision=TRI_PRECISION), X, BNT, precision=TRI_PRECISION),)


_tri_inv.defvjp(_tri_inv_fwd, _tri_inv_bwd)


def _gdn_chunk(q, k, v, gc, gr, beta, S):
    C = q.shape[1]
    ii = lax.broadcasted_iota(jnp.int32, (C, C), 0)
    jj = lax.broadcasted_iota(jnp.int32, (C, C), 1)
    lower = ii >= jj
    decay = jnp.where(lower, jnp.exp(jnp.where(lower, gc - gr, 0.0)), 0.0)
    kb, vb = k * beta, v * beta
    L = jnp.where(ii > jj, _dot(kb, k, BNT) * decay, 0.0)
    Tinv = _tri_inv(L)
    eg = jnp.exp(gc)
    u = _dot(Tinv, vb, BNN, precision=TRI_PRECISION)
    w = _dot(Tinv, kb * eg, BNN, precision=TRI_PRECISION)
    a = _dot(q, k, BNT) * decay
    g_last = gc[:, C - 1:C, :]
    kd = k * jnp.exp(g_last - gc)
    v_new = u - _dot(w, S, BNN)
    o = _dot(q * eg, S, BNN) + _dot(a, v_new, BNN)
    S_new = S * jnp.exp(g_last) + _dot(kd, v_new, BTN)
    return o, S_new


def _heads(ref, width=HD):
    return jnp.stack([ref[:, h * width:(h + 1) * width].astype(F32) for h in range(HEADS)])


def _gdn_fwd(qkvn, gcum, grT, beta, *, B, S):
    N, T = S // CHUNK, B * S
    row = lambda c: (lambda b, n: (b * N + n, c))
    ins = [(qkvn, (CHUNK, 1024), row(0)), (qkvn, (CHUNK, 1024), row(1)), (qkvn, (CHUNK, 1024), row(2)),
           (gcum, (CHUNK, LANES), row(0)), (grT, (1, HEADS, 1, CHUNK), lambda b, n: (b * N + n, 0, 0, 0)),
           (beta, (CHUNK, LANES), row(0))]
    outs = [((T, 1024), F32, (CHUNK, 1024), row(0)),
            ((B * N, HEADS, HD, HD), BF16, (1, HEADS, HD, HD), lambda b, n: (b * N + n, 0, 0, 0))]

    def body(in_refs, out_refs, scr):
        q_ref, k_ref, v_ref, gc_ref, gr_ref, b_ref = in_refs
        o_ref, st_ref = out_refs
        S_ref = scr[0]

        @pl.when(pl.program_id(1) == 0)
        def _():
            S_ref[...] = jnp.zeros_like(S_ref)

        S0 = S_ref[...]
        st_ref[0] = S0.astype(BF16)
        o, Sn = _gdn_chunk(_heads(q_ref), _heads(k_ref), _heads(v_ref), _heads(gc_ref, 1), gr_ref[0],
                           _heads(b_ref, 1), S0)
        for h in range(HEADS):
            o_ref[:, h * HD:(h + 1) * HD] = o[h]
        S_ref[...] = Sn

    return _call(body, ins, outs, (B, N), name="gdn_core_fwd", scratch=[pltpu.VMEM((HEADS, HD, HD), F32)])


def _gdn_bwd(qkvn, gcum, grT, beta, states, do, *, B, S):
    N, T = S // CHUNK, B * S
    row = lambda c: (lambda b, n: (b * N + N - 1 - n, c))
    ins = [(qkvn, (CHUNK, 1024), row(0)), (qkvn, (CHUNK, 1024), row(1)), (qkvn, (CHUNK, 1024), row(2)),
           (gcum, (CHUNK, LANES), row(0)), (grT, (1, HEADS, 1, CHUNK), lambda b, n: (b * N + N - 1 - n, 0, 0, 0)),
           (beta, (CHUNK, LANES), row(0)),
           (states, (1, HEADS, HD, HD), lambda b, n: (b * N + N - 1 - n, 0, 0, 0)), (do, (CHUNK, 1024), row(0))]
    outs = [((T, 3072), BF16, (CHUNK, 3072), row(0)), ((T, LANES), F32, (CHUNK, LANES), row(0)),
            ((B * N, HEADS, 1, CHUNK), F32, (1, HEADS, 1, CHUNK), lambda b, n: (b * N + N - 1 - n, 0, 0, 0)),
            ((T, LANES), F32, (CHUNK, LANES), row(0))]

    def body(in_refs, out_refs, scr):
        q_ref, k_ref, v_ref, gc_ref, gr_ref, b_ref, st_ref, do_ref = in_refs
        dqkv_ref, dgc_ref, dgr_ref, db_ref = out_refs
        dS_ref = scr[0]

        @pl.when(pl.program_id(1) == 0)
        def _():
            dS_ref[...] = jnp.zeros_like(dS_ref)

        args = (_heads(q_ref), _heads(k_ref), _heads(v_ref), _heads(gc_ref, 1), gr_ref[0], _heads(b_ref, 1),
                st_ref[0].astype(F32))
        _, vjp = jax.vjp(_gdn_chunk, *args)
        dq, dk, dv, dgc, dgr, db, dS = vjp((_heads(do_ref), dS_ref[...]))
        lane = lax.broadcasted_iota(jnp.int32, (CHUNK, LANES), 1)
        dgc_all = jnp.zeros((CHUNK, LANES), F32)
        db_all = jnp.zeros((CHUNK, LANES), F32)
        for h in range(HEADS):
            dqkv_ref[:, h * HD:(h + 1) * HD] = dq[h].astype(BF16)
            dqkv_ref[:, 1024 + h * HD:1024 + (h + 1) * HD] = dk[h].astype(BF16)
            dqkv_ref[:, 2048 + h * HD:2048 + (h + 1) * HD] = dv[h].astype(BF16)
            dgc_all = jnp.where(lane == h, dgc[h], dgc_all)
            db_all = jnp.where(lane == h, db[h], db_all)
        dgc_ref[...] = dgc_all
        db_ref[...] = db_all
        dgr_ref[0] = dgr
        dS_ref[...] = dS

    return _call(body, ins, outs, (B, N), name="gdn_core_bwd", scratch=[pltpu.VMEM((HEADS, HD, HD), F32)])


def _gate_fn(za, zb, alog, dtb):
    tm = za.shape[0]
    g = -jnp.exp(alog) * jax.nn.softplus(za + dtb)
    ii = lax.broadcasted_iota(jnp.int32, (tm, tm), 0)
    jj = lax.broadcasted_iota(jnp.int32, (tm, tm), 1)
    tri = ((ii >= jj) & ((ii >> 6) == (jj >> 6))).astype(F32)
    return _dot(tri, g, precision=HI), jax.nn.sigmoid(zb)


def _scores(qn_ref, qp_ref, kn_ref, kp_ref, diag):
    q = jnp.concatenate([qn_ref[...], qp_ref[...]], axis=1)
    k = jnp.concatenate([kn_ref[...], kp_ref[...]], axis=1)
    s = _dot(q, k, NT) * SM_SCALE
    if diag:
        t = s.shape[0]
        ii = lax.broadcasted_iota(jnp.int32, (t, t), 0)
        jj = lax.broadcasted_iota(jnp.int32, (t, t), 1)
        s = jnp.where(ii >= jj, s, -jnp.inf)
    return s, q, k


HPB = 4
HW = HPB * HD


def _head_refs(refs, hh):
    return [r.at[:, hh * HD:(hh + 1) * HD] for r in refs]


def _flash_fwd(qn, qp, kn, kp, v, *, B, S, t):
    nb, T = S // t, B * S
    qmap = lambda b, h, qi, ki: (b * nb + qi, h)
    kmap = lambda b, h, qi, ki: (b * nb + jnp.minimum(ki, qi), h)
    kpmap = lambda b, h, qi, ki: (b * nb + jnp.minimum(ki, qi), 0)
    ins = [(qn, (t, HW), qmap), (qp, (t, HW), qmap), (kn, (t, HW), kmap), (kp, (t, HD), kpmap), (v, (t, HW), kmap)]
    outs = [((T, 1024), BF16, (t, HW), qmap),
            ((HEADS, T, 1), F32, (HPB, t, 1), lambda b, h, qi, ki: (h, b * nb + qi, 0))]
    scratch = [pltpu.VMEM((HPB, t, 1), F32), pltpu.VMEM((HPB, t, 2 * HD), F32)]

    def body(in_refs, out_refs, scr):
        qn_ref, qp_ref, kn_ref, kp_ref, v_ref = in_refs
        o_ref, lse_ref = out_refs
        m_ref, acc_ref = scr
        qi, ki = pl.program_id(2), pl.program_id(3)

        @pl.when(ki == 0)
        def _():
            m_ref[...] = jnp.full_like(m_ref, -jnp.inf)
            acc_ref[...] = jnp.zeros_like(acc_ref)

        def step(diag):
            for hh in range(HPB):
                qn_h, qp_h, kn_h, v_h = _head_refs((qn_ref, qp_ref, kn_ref, v_ref), hh)
                s, _, _ = _scores(qn_h, qp_h, kn_h, kp_ref, diag)
                m_old = m_ref[hh]
                m_new = jnp.maximum(m_old, jnp.max(s, axis=-1, keepdims=True))
                p = jnp.exp(s - m_new)
                alpha = jnp.exp(m_old - m_new)
                v1 = jnp.concatenate([v_h[...], jnp.ones((t, HD), BF16)], axis=1)
                acc_ref[hh] = alpha * acc_ref[hh] + _dot(p.astype(BF16), v1)
                m_ref[hh] = m_new

        @pl.when(ki < qi)
        def _():
            step(False)

        @pl.when(ki == qi)
        def _():
            step(True)
            for hh in range(HPB):
                o_ref[:, hh * HD:(hh + 1) * HD] = (acc_ref[hh, :, :HD] / acc_ref[hh, :, HD:]).astype(BF16)
                lse_ref[hh] = m_ref[hh] + jnp.log(acc_ref[hh, :, HD:HD + 1])

    return _call(body, ins, outs, (B, HEADS // HPB, nb, nb), name="mla_flash_fwd", scratch=scratch,
                 semantics=("parallel", "parallel", "parallel", "arbitrary"))


def _flash_bwd_dq(qn, qp, kn, kp, v, o, do, lse, *, B, S, t):
    nb, T = S // t, B * S
    qmap = lambda b, h, qi, ki: (b * nb + qi, h)
    kmap = lambda b, h, qi, ki: (b * nb + jnp.minimum(ki, qi), h)
    kpmap = lambda b, h, qi, ki: (b * nb + jnp.minimum(ki, qi), 0)
    ins = [(qn, (t, HW), qmap), (qp, (t, HW), qmap), (kn, (t, HW), kmap), (kp, (t, HD), kpmap), (v, (t, HW), kmap),
           (o, (t, HW), qmap), (do, (t, HW), qmap), (lse, (HPB, t, 1), lambda b, h, qi, ki: (h, b * nb + qi, 0))]
    outs = [((T, 1024), BF16, (t, HW), qmap), ((T, 1024), F32, (t, HW), qmap),
            ((HEADS, T, 1), F32, (HPB, t, 1), lambda b, h, qi, ki: (h, b * nb + qi, 0))]
    scratch = [pltpu.VMEM((HPB, t, 1), F32), pltpu.VMEM((HPB, t, 2 * HD), F32)]

    def body(in_refs, out_refs, scr):
        qn_ref, qp_ref, kn_ref, kp_ref, v_ref, o_ref, do_ref, lse_ref = in_refs
        dqn_ref, dqp_ref, dlo_ref = out_refs
        dl_ref, acc_ref = scr
        qi, ki = pl.program_id(2), pl.program_id(3)

        @pl.when(ki == 0)
        def _():
            for hh in range(HPB):
                o_h, do_h = _head_refs((o_ref, do_ref), hh)
                dl_ref[hh] = jnp.sum(do_h[...].astype(F32) * o_h[...].astype(F32), axis=-1, keepdims=True)
            acc_ref[...] = jnp.zeros_like(acc_ref)

        def step(diag):
            for hh in range(HPB):
                qn_h, qp_h, kn_h, v_h, do_h = _head_refs((qn_ref, qp_ref, kn_ref, v_ref, do_ref), hh)
                s, _, k = _scores(qn_h, qp_h, kn_h, kp_ref, diag)
                p = jnp.exp(s - lse_ref[hh])
                dp = _dot(do_h[...], v_h[...], NT)
                ds = p * (dp - dl_ref[hh]) * SM_SCALE
                acc_ref[hh] += _dot(ds.astype(BF16), k)

        @pl.when(ki < qi)
        def _():
            step(False)

        @pl.when(ki == qi)
        def _():
            step(True)
            for hh in range(HPB):
                dqn_ref[:, hh * HD:(hh + 1) * HD] = acc_ref[hh, :, :HD].astype(BF16)
                dqp_ref[:, hh * HD:(hh + 1) * HD] = acc_ref[hh, :, HD:]
            dlo_ref[...] = dl_ref[...]

    return _call(body, ins, outs, (B, HEADS // HPB, nb, nb), name="mla_flash_bwd_dq", scratch=scratch,
                 semantics=("parallel", "parallel", "parallel", "arbitrary"))


def _flash_bwd_dkv(qn, qp, kn, kp, v, do, lse_t, dl_t, *, B, S, t):
    nb, T = S // t, B * S
    qmap = lambda b, h, ki, qi: (b * nb + jnp.maximum(qi, ki), h)
    kmap = lambda b, h, ki, qi: (b * nb + ki, h)
    tmap = lambda b, h, ki, qi: (h, 0, b * nb + jnp.maximum(qi, ki))
    ins = [(qn, (t, HW), qmap), (qp, (t, HW), qmap), (kn, (t, HW), kmap),
           (kp, (t, HD), lambda b, h, ki, qi: (b * nb + ki, 0)), (v, (t, HW), kmap), (do, (t, HW), qmap),
           (lse_t, (HPB, 1, t), tmap), (dl_t, (HPB, 1, t), tmap)]
    outs = [((T, 1024), BF16, (t, HW), kmap), ((HEADS, T, HD), F32, (HPB, t, HD), lambda b, h, ki, qi: (h, b * nb + ki, 0)),
            ((T, 1024), BF16, (t, HW), kmap)]
    scratch = [pltpu.VMEM((HPB, t, 2 * HD), F32), pltpu.VMEM((HPB, t, HD), F32)]

    def body(in_refs, out_refs, scr):
        qn_ref, qp_ref, kn_ref, kp_ref, v_ref, do_ref, lse_ref, dl_ref = in_refs
        dkn_ref, dkp_ref, dv_ref = out_refs
        dk_acc, dv_acc = scr
        ki, qi = pl.program_id(2), pl.program_id(3)

        @pl.when(qi == 0)
        def _():
            dk_acc[...] = jnp.zeros_like(dk_acc)
            dv_acc[...] = jnp.zeros_like(dv_acc)

        def step(diag):
            for hh in range(HPB):
                qn_h, qp_h, kn_h, v_h, do_h = _head_refs((qn_ref, qp_ref, kn_ref, v_ref, do_ref), hh)
                q = jnp.concatenate([qn_h[...], qp_h[...]], axis=1)
                k = jnp.concatenate([kn_h[...], kp_ref[...]], axis=1)
                st = _dot(k, q, NT) * SM_SCALE
                if diag:
                    ii = lax.broadcasted_iota(jnp.int32, (t, t), 0)
                    jj = lax.broadcasted_iota(jnp.int32, (t, t), 1)
                    st = jnp.where(ii <= jj, st, -jnp.inf)
                do_t = do_h[...]
                pt = jnp.exp(st - lse_ref[hh])
                dst = pt * (_dot(v_h[...], do_t, NT) - dl_ref[hh]) * SM_SCALE
                dv_acc[hh] += _dot(pt.astype(BF16), do_t)
                dk_acc[hh] += _dot(dst.astype(BF16), q)

        @pl.when(qi > ki)
        def _():
            step(False)

        @pl.when(qi == ki)
        def _():
            step(True)

        @pl.when(qi == nb - 1)
        def _():
            for hh in range(HPB):
                dkn_ref[:, hh * HD:(hh + 1) * HD] = dk_acc[hh, :, :HD].astype(BF16)
                dkp_ref[hh] = dk_acc[hh, :, HD:]
                dv_ref[:, hh * HD:(hh + 1) * HD] = dv_acc[hh].astype(BF16)

    return _call(body, ins, outs, (B, HEADS // HPB, nb, nb), name="mla_flash_bwd_dkv", scratch=scratch,
                 semantics=("parallel", "parallel", "parallel", "arbitrary"))


def _allgather_async(shards, *, name, collective_id):
    n_arr = len(shards)
    hbm = pltpu.MemorySpace.HBM
    x_refs = [jax.new_ref(a, memory_space=hbm) for a in shards]
    out_refs = [jax.empty_ref(jax.ShapeDtypeStruct((N_DEV * a.shape[0], a.shape[1]), a.dtype), memory_space=hbm)
                for a in shards]

    @pl.kernel(mesh=plsc.ScalarSubcoreMesh(axis_name="seq", num_cores=1), name=name,
               scratch_types=(pltpu.SemaphoreType.DMA((n_arr, 7)), pltpu.SemaphoreType.DMA((n_arr, 7)),
                              pltpu.SemaphoreType.DMA((n_arr,))),
               compiler_params=pltpu.CompilerParams(collective_id=collective_id))
    def launch(send_sems, recv_sems, local_sems):
        x, y, c = lax.axis_index("x"), lax.axis_index("y"), lax.axis_index("c")
        me, sibling = (x, y, c), (x, y, 1 - c)
        chips = [(1 - x, y), (x, 1 - y), (1 - x, 1 - y)]
        barrier = pltpu.get_barrier_semaphore()
        for p in [sibling] + [(*chip, c) for chip in chips]:
            pl.semaphore_signal(barrier, inc=1, device_id=p, device_id_type=pl.DeviceIdType.MESH)
        pl.semaphore_wait(barrier, 4)

        def rows(a, px, py, pc):
            m_per = shards[a].shape[0]
            return out_refs[a].at[pl.ds((4 * px + 2 * py + pc) * m_per, m_per), :]

        def copy(a, k, block, to, src=None):
            return pltpu.make_async_remote_copy(
                src_ref=rows(a, *block) if src is None else src, dst_ref=rows(a, *block),
                send_sem=send_sems.at[a, k], recv_sem=recv_sems.at[a, k], device_id=to,
                device_id_type=pl.DeviceIdType.MESH)

        mine = [pltpu.make_async_copy(x_refs[a], rows(a, *me), local_sems.at[a]) for a in range(n_arr)]
        for cp in mine:
            cp.start()
        first = []
        for a in range(n_arr):
            first.append(copy(a, 0, me, sibling, src=x_refs[a]))
            first += [copy(a, 1 + j, me, (*chip, c), src=x_refs[a]) for j, chip in enumerate(chips)]
        for cp in first:
            cp.start()
        passed = []
        for j, chip in enumerate(chips):
            for a in range(n_arr):
                copy(a, 1 + j, (*chip, c), me).wait_recv()
                cp = copy(a, 4 + j, (*chip, c), sibling)
                cp.start()
                passed.append(cp)
        for a in range(n_arr):
            copy(a, 0, sibling, me).wait_recv()
        for j, chip in enumerate(chips):
            for a in range(n_arr):
                copy(a, 4 + j, (*chip, 1 - c), me).wait_recv()
        for cp in first + passed:
            cp.wait_send()
        for cp in mine:
            cp.wait()

    launch()
    return [r[...] for r in out_refs]


def _alltoall_async(sends, *, name, collective_id):
    n_arr = len(sends)
    hbm = pltpu.MemorySpace.HBM
    s_refs = [jax.new_ref(a, memory_space=hbm) for a in sends]
    r_refs = [jax.empty_ref(jax.ShapeDtypeStruct(a.shape, a.dtype), memory_space=hbm) for a in sends]

    @pl.kernel(mesh=plsc.ScalarSubcoreMesh(axis_name="seq", num_cores=1), name=name,
               scratch_types=(pltpu.SemaphoreType.DMA((n_arr, 7)), pltpu.SemaphoreType.DMA((n_arr, 7)),
                              pltpu.SemaphoreType.DMA((n_arr,))),
               compiler_params=pltpu.CompilerParams(collective_id=collective_id))
    def launch(send_sems, recv_sems, local_sems):
        x, y, c = lax.axis_index("x"), lax.axis_index("y"), lax.axis_index("c")
        me = 4 * x + 2 * y + c
        peers = [(1 - x if k & 4 else x, 1 - y if k & 2 else y, 1 - c if k & 1 else c) for k in range(1, N_DEV)]
        barrier = pltpu.get_barrier_semaphore()
        for p in peers:
            pl.semaphore_signal(barrier, inc=1, device_id=p, device_id_type=pl.DeviceIdType.MESH)
        pl.semaphore_wait(barrier, N_DEV - 1)

        def rows(ref, a, idx):
            m_per = sends[a].shape[0] // N_DEV
            return ref.at[pl.ds(idx * m_per, m_per), :]

        local = [pltpu.make_async_copy(rows(s_refs[a], a, me), rows(r_refs[a], a, me), local_sems.at[a])
                 for a in range(n_arr)]
        for cp in local:
            cp.start()
        copies = []
        for k, (px, py, pc) in enumerate(peers):
            for a in range(n_arr):
                cp = pltpu.make_async_remote_copy(
                    src_ref=rows(s_refs[a], a, 4 * px + 2 * py + pc), dst_ref=rows(r_refs[a], a, me),
                    send_sem=send_sems.at[a, k], recv_sem=recv_sems.at[a, k],
                    device_id=(px, py, pc), device_id_type=pl.DeviceIdType.MESH)
                cp.start()
                copies.append(cp)
        for cp in copies:
            cp.wait()
        for cp in local:
            cp.wait()

    launch()
    return [r[...] for r in r_refs]


def _reduce_adam(parts, w, m, v, *, tr, name):
    R, C = w.shape
    nR = R // tr
    ins = [(parts, (tr, C), lambda i, s=s: (s * nR + i, 0)) for s in range(N_DEV)]
    ins += [(a, (tr, C), lambda i: (i, 0)) for a in (w, m, v)]
    outs = [((R, C), F32, (tr, C), lambda i: (i, 0)) for _ in range(4)]
    c1 = 1.0 - ADAM_B1 ** ADAM_STEP
    c2 = 1.0 - ADAM_B2 ** ADAM_STEP

    def body(in_refs, out_refs, _):
        g = in_refs[0][...].astype(F32)
        for s in range(1, N_DEV):
            g = g + in_refs[s][...].astype(F32)
        wv, mv, vv = in_refs[8][...], in_refs[9][...], in_refs[10][...]
        mn = ADAM_B1 * mv + (1.0 - ADAM_B1) * g
        vn = ADAM_B2 * vv + (1.0 - ADAM_B2) * (g * g)
        delta = -ADAM_LR * ((mn / c1) / (jnp.sqrt(vn / c2) + ADAM_EPS) + ADAM_WD * wv)
        out_refs[0][...] = g
        out_refs[1][...] = delta
        out_refs[2][...] = mn
        out_refs[3][...] = vn

    return _call(body, ins, outs, (nR,), name=name, semantics=("parallel",))


IN_C, UP_C, UQ_C, QKV_C = 858, 704, 192, 384
A_W, Q_W, V_W = 896, 256, 768
SLAB_TR = {"A": 256, "Q": 128, "C": 368, "V": 16}
SMALL = [("norm_mix_g", 1024), ("gdn_a_log", 8), ("gdn_dt_bias", 8), ("gdn_norm_g", 128), ("mla_q_norm_g", 384),
         ("mla_kv_norm_g", 256), ("norm_ffn_g", 1024), ("norm_final_g", 1024)]
SMALL_ROWS = 32
WEIGHT_ORDER = ["norm_mix_g", "w_in", "conv_qkv_w", "gdn_a_log", "gdn_dt_bias", "gdn_norm_g", "mla_q_norm_g", "w_uq",
                "mla_kv_norm_g", "w_ukv", "w_o_gdn", "w_o_mla", "w_out", "norm_ffn_g", "w_up", "conv_ffn_w", "w_down",
                "norm_final_g"]


def _padc(w, n):
    return jnp.pad(w, ((0, 0), (0, n - w.shape[1])))


def _padrc(w, r, n):
    return jnp.pad(w, ((0, r - w.shape[0]), (0, n - w.shape[1])))


def _slabs(p, dtype):
    A = jnp.concatenate([_padc(p["w_in"], A_W), _padc(p["w_up"], A_W)], axis=0).astype(dtype)
    Q = jnp.concatenate([_padc(p["w_uq"], Q_W), p["w_ukv"]], axis=0).astype(dtype)
    C = jnp.concatenate([p["w_o_gdn"], p["w_o_mla"], p["w_out"], p["w_down"]], axis=0).astype(dtype)
    V = jnp.concatenate([_padrc(p["conv_qkv_w"], 8, V_W), _padrc(p["conv_ffn_w"], 8, V_W)], axis=0).astype(F32)
    return {"A": A, "Q": Q, "C": C, "V": V}


def _unslab(sl):
    A, Q, C, V = sl["A"], sl["Q"], sl["C"], sl["V"]
    out = {"w_in": A[:1024, :IN_C], "w_up": A[1024:, :UP_C], "w_uq": Q[:384, :UQ_C], "w_ukv": Q[384:],
           "w_o_gdn": C[0:128], "w_o_mla": C[128:256], "w_out": C[256:384], "w_down": C[384:],
           "conv_qkv_w": V[0:GDN_CONV, :QKV_C], "conv_ffn_w": V[8:8 + FFN_CONV, :UP_C]}
    return {k: a[None] for k, a in out.items()}


def _take_cols(pieces, lo, hi):
    out, off = [], 0
    for arr, a, b in pieces:
        s, e = max(lo, off), min(hi, off + b - a)
        if s < e:
            out.append(arr[:, a + s - off:a + e - off])
        off += b - a
    return out[0] if len(out) == 1 else jnp.concatenate(out, axis=1)


def _pack_small(d):
    flat = jnp.concatenate([d[n].reshape(-1).astype(F32) for n, _ in SMALL])
    return jnp.pad(flat, (0, SMALL_ROWS * LANES - flat.shape[0])).reshape(SMALL_ROWS, LANES)


def _unpack_small(buf, shapes):
    flat, out, off = buf.reshape(-1), {}, 0
    for name, n in SMALL:
        out[name] = flat[off:off + n].reshape(shapes[name])
        off += n
    return out


def _rot_cols(w):
    h = ROPE // 2
    return jnp.concatenate([-w[:, h:], w[:, :h]], axis=1)


def _unrot_cols(dw):
    h = ROPE // 2
    return jnp.concatenate([dw[:, h:], -dw[:, :h]], axis=1)


IN_SPLITS = [0, 3072, 4096, 4104, 4112, 4496, 4752, 4816, 5840, 6864]


def _layout_late(A_up, C):
    W = {"w_up": jnp.concatenate([A_up[j, :, :UP_C] for j in range(N_DEV)], axis=1),
         "w_o_gdn": C[:, 0:128].reshape(1024, D_MODEL), "w_o_mla": C[:, 128:256].reshape(1024, D_MODEL),
         "w_out": C[:, 256:384].reshape(1024, D_MODEL), "w_down": C[:, 384:].reshape(D_FF, D_MODEL)}
    return {k: v.astype(BF16) for k, v in W.items()}


def _layout_weights(g):
    A_in, Q, V = g["A_in"], g["Q"], g["V"]
    in_pieces = [(A_in[j], 0, IN_C) for j in range(N_DEV)]
    o = IN_SPLITS
    take = lambda lo, hi: _take_cols(in_pieces, lo, hi)
    kpe = take(o[6], o[7])
    W = {
        "in_qkv": take(o[0], o[1]),
        "in_ga": take(o[1], o[2]),
        "in_ab": jnp.concatenate([_padc(take(o[2], o[3]), LANES), _padc(take(o[3], o[4]), LANES)], axis=1),
        "in_small": jnp.concatenate([take(o[4], o[6]), _padc(kpe, LANES), _padc(_rot_cols(kpe), LANES)], axis=1),
        "in_gbr": take(o[7], o[9]),
        "uq_n": jnp.concatenate([Q[j, :384, :HD] for j in range(N_DEV)], axis=1),
        "ukv_k": jnp.concatenate([Q[j, 384:, :HD] for j in range(N_DEV)], axis=1),
        "ukv_v": jnp.concatenate([Q[j, 384:, HD:] for j in range(N_DEV)], axis=1),
    }
    pe = [Q[j, :384, HD:HD + ROPE] for j in range(N_DEV)]
    W["uq_p"] = jnp.concatenate([_padc(p, HD) for p in pe] + [_padc(_rot_cols(p), HD) for p in pe], axis=1)
    conv_qkv = jnp.concatenate([V[j, 0:GDN_CONV, :QKV_C] for j in range(N_DEV)], axis=1)
    conv_ffn = jnp.concatenate([V[j, 8:8 + FFN_CONV, :UP_C] for j in range(N_DEV)], axis=1)
    return {k: v.astype(BF16) for k, v in W.items()}, conv_qkv, conv_ffn


def _full_grads(dW):
    s = dW["in_small"]
    dkpe = s[:, 640:704] + _unrot_cols(s[:, 768:832])
    in_pieces = [(dW["in_qkv"], 0, 3072), (dW["in_ga"], 0, 1024), (dW["in_ab"], 0, 8), (dW["in_ab"], 128, 136),
                 (s, 0, 640), (dkpe, 0, ROPE), (dW["in_gbr"], 0, 2048)]
    pe = []
    for j in range(N_DEV):
        lin = dW["uq_p"][:, j * HD:j * HD + ROPE]
        rot = dW["uq_p"][:, 1024 + j * HD:1024 + j * HD + ROPE]
        pe.append(lin + _unrot_cols(rot))
    return in_pieces, pe


def _send_slabs(dW, d_conv_qkv, d_conv_ffn):
    in_pieces, pe = _full_grads(dW)
    A, Q, V = [], [], []
    for j in range(N_DEV):
        gin = _padc(_take_cols(in_pieces, j * IN_C, (j + 1) * IN_C), A_W)
        gup = _padc(dW["w_up"][:, j * UP_C:(j + 1) * UP_C], A_W)
        A.append(jnp.concatenate([gin, gup], axis=0))
        guq = _padc(jnp.concatenate([dW["uq_n"][:, j * HD:(j + 1) * HD], pe[j]], axis=1), Q_W)
        gukv = jnp.concatenate([dW["ukv_k"][:, j * HD:(j + 1) * HD], dW["ukv_v"][:, j * HD:(j + 1) * HD]], axis=1)
        Q.append(jnp.concatenate([guq, gukv], axis=0))
        V.append(jnp.concatenate([_padrc(d_conv_qkv[:, j * QKV_C:(j + 1) * QKV_C], 8, V_W),
                                  _padrc(d_conv_ffn[:, j * UP_C:(j + 1) * UP_C], 8, V_W)], axis=0))
    C = jnp.concatenate([dW["w_o_gdn"].reshape(N_DEV, 128, D_MODEL), dW["w_o_mla"].reshape(N_DEV, 128, D_MODEL),
                         dW["w_out"].reshape(N_DEV, 128, D_MODEL), dW["w_down"].reshape(N_DEV, 352, D_MODEL)], axis=1)
    return {"A": jnp.concatenate(A, axis=0).astype(BF16), "Q": jnp.concatenate(Q, axis=0).astype(BF16),
            "C": C.reshape(N_DEV * 736, D_MODEL).astype(BF16), "V": jnp.concatenate(V, axis=0)}


def _rope_tables(S):
    half = ROPE // 2
    inv = ROPE_THETA ** (-jnp.arange(half, dtype=F32) / half)
    ang = jnp.arange(S, dtype=F32)[:, None] * inv[None, :]
    cos = jnp.concatenate([jnp.cos(ang), jnp.cos(ang)], axis=1)
    sin = jnp.concatenate([jnp.sin(ang), jnp.sin(ang)], axis=1)
    return _padc(cos, HD), _padc(sin, HD)


def _local_step(x, tgt, W, late_weights, conv_qkv_w, conv_ffn_w, small, tm=None, ta=None):
    B, S, _ = x.shape
    T = B * S
    tm = tm or _pick(S, 512, CHUNK)
    ta = ta or _pick(S, 512, LANES)
    x2d, tgt2d = x.reshape(T, D_MODEL), tgt.reshape(T, D_MODEL)
    row = lambda v: v.reshape(1, -1).astype(F32)
    pad_row = lambda v: _padc(row(v), LANES)
    g_mix, g_ffn, g_fin = row(small["norm_mix_g"]), row(small["norm_ffn_g"]), row(small["norm_final_g"])
    g_gdn, g_q, g_kv = row(small["gdn_norm_g"]), row(small["mla_q_norm_g"]), row(small["mla_kv_norm_g"])
    alog, dtb = pad_row(small["gdn_a_log"]), pad_row(small["gdn_dt_bias"])
    cos, sin = _rope_tables(S)
    tps = S // tm
    tab = lambda a: (a, (tm, HD), lambda i: (i % tps, 0))
    col = lambda a, c, w: (a, (tm, w), lambda i, c=c: (i, c))

    h1 = _norm_fwd(x2d, g_mix, T=T, tm=tm, name="norm_mix_fwd")
    z_qkv = _mm(h1, W["in_qkv"], "nn", BF16, name="in_qkv_fwd")
    z_ga = _mm(h1, W["in_ga"], "nn", BF16, name="in_ga_fwd")
    z_ab = _mm(h1, W["in_ab"], "nn", F32, name="in_ab_fwd")
    z_small = _mm(h1, W["in_small"], "nn", F32, name="in_small_fwd", tn=896)
    z_gbr = _mm(h1, W["in_gbr"], "nn", BF16, name="in_gbr_fwd")

    qkvn = _conv_fwd(_qkv_fn, [(z_qkv, 0)], [(conv_qkv_w, 0)], 3072, BF16, T=T, S=S, tm=tm, cb=QKV_CB,
                     ncb=3072 // QKV_CB, name="gdn_qkv_fwd")
    gcum, beta = _row_call(lambda za, zb, al, db: _gate_fn(za, zb, al, db), [col(z_ab, 0, LANES), col(z_ab, 1, LANES)],
                           [alog, dtb], [(LANES, F32), (LANES, F32)], T=T, tm=tm, name="gdn_gate_fwd")
    grT = gcum[:, :HEADS].reshape(T // CHUNK, CHUNK, HEADS).transpose(0, 2, 1)[:, :, None, :]
    qkvn, late = late_weights(qkvn)
    W = {**W, **late}
    o_gdn, states = _gdn_fwd(qkvn, gcum, grT, beta, B=B, S=S)

    def gdn_out_fn(o, ga, g):
        parts = []
        for h in range(HEADS):
            sl = slice(h * HD, (h + 1) * HD)
            parts.append(_rms(o[:, sl], g) * jax.nn.silu(ga[:, sl].astype(F32)))
        return jnp.concatenate(parts, axis=1)

    oa = _row_call(lambda o, ga, g: (gdn_out_fn(o, ga, g),), [o_gdn, z_ga], [g_gdn], [(1024, BF16)], T=T, tm=tm,
                   name="gdn_out_fwd")[0]

    def mla_prep_fn(zq, zkv, zpl, zpr, c, s, gq, gkv):
        return _rms(zq, gq), _rms(zkv, gkv), zpl * c + zpr * s

    small_cols = [(z_small, (tm, Q_RANK), lambda i: (i, 0)), (z_small, (tm, LANES), lambda i: (i, 3)),
                  (z_small, (tm, LANES), lambda i: (i, 4)), (z_small, (tm, LANES), lambda i: (i, 5)),
                  (z_small, (tm, LANES), lambda i: (i, 6))]

    def mla_prep_fwd(zq, zkv0, zkv1, zpl, zpr, c, s, gq, gkv):
        return mla_prep_fn(zq, jnp.concatenate([zkv0, zkv1], axis=1), zpl, zpr, c, s, gq, gkv)

    cq, ckv, kpe = _row_call(mla_prep_fwd, small_cols + [tab(cos), tab(sin)], [g_q, g_kv],
                             [(Q_RANK, BF16), (KV_RANK, BF16), (HD, BF16)], T=T, tm=tm, name="mla_prep_fwd")
    qn = _mm(cq, W["uq_n"], "nn", BF16, name="uq_n_fwd")
    qpl = _mm(cq, W["uq_p"], "nn", F32, name="uq_p_fwd")
    kn = _mm(ckv, W["ukv_k"], "nn", BF16, name="ukv_k_fwd")
    vb = _mm(ckv, W["ukv_v"], "nn", BF16, name="ukv_v_fwd")

    def qrope_fn(lin, rot, c, s):
        return lin * jnp.tile(c, (1, HEADS)) + rot * jnp.tile(s, (1, HEADS))

    qp = _row_call(lambda lin, rot, c, s: (qrope_fn(lin, rot, c, s),), [col(qpl, 0, 1024), col(qpl, 1, 1024), tab(cos), tab(sin)],
                   [], [(1024, BF16)], T=T, tm=tm, name="q_rope_fwd")[0]
    ob, lse = _flash_fwd(qn, qp, kn, kpe, vb, B=B, S=S, t=ta)

    def merge_fn(ya, yb, ga, gb):
        return jax.nn.sigmoid(ga.astype(F32)) * ya + jax.nn.sigmoid(gb.astype(F32)) * yb

    def merge_fwd(oat, obt, ga, gb, wog, wom):
        ya, yb = _dot(oat, wog), _dot(obt, wom)
        return ya, yb, merge_fn(ya, yb, ga, gb)

    ya, yb, merged = _row_call(merge_fwd, [oa, ob, col(z_gbr, 0, 1024), col(z_gbr, 1, 1024)], [W["w_o_gdn"], W["w_o_mla"]],
                               [(1024, BF16), (1024, BF16), (1024, BF16)], T=T, tm=tm, name="merge_fwd")
    x1 = _mm(merged, W["w_out"], "nn", F32, add=x2d, name="w_out_fwd")

    h2 = _norm_fwd(x1, g_ffn, T=T, tm=tm, name="norm_ffn_fwd")
    up = _mm(h2, W["w_up"], "nn", BF16, name="w_up_fwd")
    FCB = 256
    nfb = D_FF // FCB
    f = _conv_fwd(_ffn_fn, [(up, 0), (up, 2)], [(conv_ffn_w, 0), (conv_ffn_w, 2)], D_FF, BF16, T=T, S=S, tm=tm,
                  cb=D_FF // 2, ncb=2, name="ffn_act_fwd")
    x2 = _mm(f, W["w_down"], "nn", F32, add=x1, name="w_down_fwd", tk=1408)

    def final_fn(xt, tt, g):
        def lossf(xv, gv):
            e = _rms(xv, gv) - tt
            return 0.5 * jnp.sum(jnp.mean(e * e, axis=-1))

        l, vjp = jax.vjp(lossf, xt, g)
        dx, dg = vjp(jnp.ones((), F32))
        return dx, jnp.full((1, LANES), l, F32), dg

    dx2, loss_v, dg_fin = _row_call(final_fn, [x2, tgt2d], [g_fin], [(1024, F32)], [((1, LANES), F32), ((1, 1024), F32)],
                                    T=T, tm=tm, name="loss_head")

    dW = {}
    df = _mm(dx2, W["w_down"], "nt", BF16, name="w_down_dx")
    dW["w_down"] = _mm(f, dx2, "tn", F32, name="w_down_dw")
    dug, duu, dcw_g, dcw_u = _conv_bwd(_ffn_fn, [(up, 0), (up, nfb)], [(conv_ffn_w, 0), (conv_ffn_w, nfb)], df, BF16,
                                       T=T, S=S, tm=tm, cb=FCB, ncb=nfb, name="ffn_act_bwd")
    d_conv_ffn = jnp.concatenate([dcw_g, dcw_u], axis=1)
    wup_g, wup_u = W["w_up"][:, :D_FF], W["w_up"][:, D_FF:]
    dh2 = _mm(dug, wup_g, "nt", F32, name="w_up_dx_g")
    dh2 = _mm(duu, wup_u, "nt", F32, add=dh2, name="w_up_dx_u")
    dW["w_up"] = jnp.concatenate([_mm(h2, dug, "tn", F32, name="w_up_dw_g"), _mm(h2, duu, "tn", F32, name="w_up_dw_u")], axis=1)
    dx1, dg_ffn = _norm_bwd(x1, g_ffn, dh2, dx2, T=T, tm=tm, name="norm_ffn_bwd")

    dmerged = _mm(dx1, W["w_out"], "nt", F32, name="w_out_dx")
    dW["w_out"] = _mm(merged, dx1, "tn", F32, name="w_out_dw")

    def merge_bwd(dm, yat, ybt, ga, gb):
        _, vjp = jax.vjp(merge_fn, yat.astype(F32), ybt.astype(F32), ga, gb)
        return vjp(dm)

    dya, dyb, dgbr_a, dgbr_b = _row_call(merge_bwd, [dmerged, ya, yb, col(z_gbr, 0, 1024), col(z_gbr, 1, 1024)], [],
                                         [(1024, BF16)] * 4, T=T, tm=tm, name="merge_bwd")
    doa = _mm(dya, W["w_o_gdn"], "nt", F32, name="w_o_gdn_dx")
    dob = _mm(dyb, W["w_o_mla"], "nt", BF16, name="w_o_mla_dx")
    dW["w_o_gdn"] = _mm(oa, dya, "tn", F32, name="w_o_gdn_dw")
    dW["w_o_mla"] = _mm(ob, dyb, "tn", F32, name="w_o_mla_dw")

    dqn, dqp, dl = _flash_bwd_dq(qn, qp, kn, kpe, vb, ob, dob, lse, B=B, S=S, t=ta)
    dkn, dkp, dvb = _flash_bwd_dkv(qn, qp, kn, kpe, vb, dob, lse.reshape(HEADS, 1, T), dl.reshape(HEADS, 1, T),
                                   B=B, S=S, t=ta)

    def qrope_bwd(d, c, s):
        return d * jnp.tile(c, (1, HEADS)), d * jnp.tile(s, (1, HEADS))

    dq_lin, dq_rot = _row_call(qrope_bwd, [dqp, tab(cos), tab(sin)], [], [(1024, BF16), (1024, BF16)], T=T, tm=tm,
                               name="q_rope_bwd")
    wp_lin, wp_rot = W["uq_p"][:, :1024], W["uq_p"][:, 1024:]
    dcq = _mm(dqn, W["uq_n"], "nt", F32, name="uq_n_dx")
    dcq = _mm(dq_lin, wp_lin, "nt", F32, add=dcq, name="uq_pl_dx")
    dcq = _mm(dq_rot, wp_rot, "nt", F32, add=dcq, name="uq_pr_dx")
    dW["uq_n"] = _mm(cq, dqn, "tn", F32, name="uq_n_dw")
    dW["uq_p"] = jnp.concatenate([_mm(cq, dq_lin, "tn", F32, name="uq_pl_dw"), _mm(cq, dq_rot, "tn", F32, name="uq_pr_dw")], axis=1)
    dckv = _mm(dkn, W["ukv_k"], "nt", F32, name="ukv_k_dx")
    dckv = _mm(dvb, W["ukv_v"], "nt", F32, add=dckv, name="ukv_v_dx")
    dW["ukv_k"] = _mm(ckv, dkn, "tn", F32, name="ukv_k_dw")
    dW["ukv_v"] = _mm(ckv, dvb, "tn", F32, name="ukv_v_dw")

    def mla_prep_bwd(zq, zkv0, zkv1, zpl, zpr, c, s, dcqt, dckvt, dkpt, gq, gkv):
        zkv = jnp.concatenate([zkv0, zkv1], axis=1)
        _, vjp = jax.vjp(lambda a, b, p, r, g1, g2: mla_prep_fn(a, b, p, r, c, s, g1, g2), zq, zkv, zpl, zpr, gq, gkv)
        dk = dkpt[0]
        for h in range(1, HEADS):
            dk = dk + dkpt[h]
        dzq, dzkv, dzpl, dzpr, dgq, dgkv = vjp((dcqt, dckvt, dk))
        return jnp.concatenate([dzq, dzkv, dzpl, dzpr], axis=1), dgq, dgkv

    dz_small, dg_q, dg_kv = _row_call(
        mla_prep_bwd, small_cols + [tab(cos), tab(sin), dcq, dckv, (dkp, (HEADS, tm, HD), lambda i: (0, i, 0))],
        [g_q, g_kv], [(896, BF16)], [((1, Q_RANK), F32), ((1, KV_RANK), F32)], T=T, tm=tm, name="mla_prep_bwd")

    def gdn_out_bwd(o, ga, dot_, g):
        _, vjp = jax.vjp(gdn_out_fn, o, ga, g)
        return vjp(dot_)

    do_gdn, dz_ga, dg_gdn = _row_call(gdn_out_bwd, [o_gdn, z_ga, doa], [g_gdn], [(1024, F32), (1024, BF16)],
                                      [((1, HD), F32)], T=T, tm=tm, name="gdn_out_bwd")
    dqkvn, dgc, dgrT, dbeta = _gdn_bwd(qkvn, gcum, grT, beta, states, do_gdn, B=B, S=S)
    dgc_tot = dgc + _padc(dgrT[:, :, 0, :].transpose(0, 2, 1).reshape(T, HEADS), LANES)

    def gate_bwd(za, zb, dg, db, al, db_):
        _, vjp = jax.vjp(_gate_fn, za, zb, al, db_)
        return vjp((dg, db))

    dz_a, dz_b, d_alog, d_dtb = _row_call(gate_bwd, [col(z_ab, 0, LANES), col(z_ab, 1, LANES), dgc_tot, dbeta], [alog, dtb],
                                          [(LANES, BF16), (LANES, BF16)], [((1, LANES), F32), ((1, LANES), F32)],
                                          T=T, tm=tm, name="gdn_gate_bwd")
    dz_qkv, d_conv_qkv = _conv_bwd(_qkv_fn, [(z_qkv, 0)], [(conv_qkv_w, 0)], dqkvn, BF16, T=T, S=S, tm=tm, cb=QKV_CB,
                                   ncb=3072 // QKV_CB, name="gdn_qkv_bwd")

    dz_ab = jnp.concatenate([dz_a, dz_b], axis=1)
    dz_gbr = jnp.concatenate([dgbr_a, dgbr_b], axis=1)
    dh1 = None
    for key, dz in (("in_qkv", dz_qkv), ("in_ga", dz_ga), ("in_ab", dz_ab), ("in_small", dz_small), ("in_gbr", dz_gbr)):
        dh1 = _mm(dz, W[key], "nt", F32, add=dh1, name=key + "_dx", tk=896 if key == "in_small" else 1024)
        dW[key] = _mm(h1, dz, "tn", F32, name=key + "_dw", tn=896 if key == "in_small" else 1024)
    dx, dg_mix = _norm_bwd(x2d, g_mix, dh1, dx1, T=T, tm=tm, name="norm_mix_bwd")

    dsmall = {"norm_mix_g": dg_mix, "gdn_a_log": d_alog[:, :HEADS], "gdn_dt_bias": d_dtb[:, :HEADS], "gdn_norm_g": dg_gdn,
              "mla_q_norm_g": dg_q, "mla_kv_norm_g": dg_kv, "norm_ffn_g": dg_ffn, "norm_final_g": dg_fin}
    return loss_v[0, 0], dx.reshape(B, S, D_MODEL), dW, d_conv_qkv, d_conv_ffn, dsmall


def kernel(x, norm_mix_g, w_in, conv_qkv_w, gdn_a_log, gdn_dt_bias, gdn_norm_g, mla_q_norm_g, w_uq, mla_kv_norm_g, w_ukv, w_o_gdn, w_o_mla, w_out, norm_ffn_g, w_up, conv_ffn_w, w_down, norm_final_g, loss_target, m_norm_mix_g, m_w_in, m_conv_qkv_w, m_gdn_a_log, m_gdn_dt_bias, m_gdn_norm_g, m_mla_q_norm_g, m_w_uq, m_mla_kv_norm_g, m_w_ukv, m_w_o_gdn, m_w_o_mla, m_w_out, m_norm_ffn_g, m_w_up, m_conv_ffn_w, m_w_down, m_norm_final_g, v_norm_mix_g, v_w_in, v_conv_qkv_w, v_gdn_a_log, v_gdn_dt_bias, v_gdn_norm_g, v_mla_q_norm_g, v_w_uq, v_mla_kv_norm_g, v_w_ukv, v_w_o_gdn, v_w_o_mla, v_w_out, v_norm_ffn_g, v_w_up, v_conv_ffn_w, v_w_down, v_norm_final_g):
    w = dict(norm_mix_g=norm_mix_g, w_in=w_in, conv_qkv_w=conv_qkv_w, gdn_a_log=gdn_a_log, gdn_dt_bias=gdn_dt_bias,
             gdn_norm_g=gdn_norm_g, mla_q_norm_g=mla_q_norm_g, w_uq=w_uq, mla_kv_norm_g=mla_kv_norm_g, w_ukv=w_ukv,
             w_o_gdn=w_o_gdn, w_o_mla=w_o_mla, w_out=w_out, norm_ffn_g=norm_ffn_g, w_up=w_up, conv_ffn_w=conv_ffn_w,
             w_down=w_down, norm_final_g=norm_final_g)
    m = dict(norm_mix_g=m_norm_mix_g, w_in=m_w_in, conv_qkv_w=m_conv_qkv_w, gdn_a_log=m_gdn_a_log, gdn_dt_bias=m_gdn_dt_bias,
             gdn_norm_g=m_gdn_norm_g, mla_q_norm_g=m_mla_q_norm_g, w_uq=m_w_uq, mla_kv_norm_g=m_mla_kv_norm_g, w_ukv=m_w_ukv,
             w_o_gdn=m_w_o_gdn, w_o_mla=m_w_o_mla, w_out=m_w_out, norm_ffn_g=m_norm_ffn_g, w_up=m_w_up,
             conv_ffn_w=m_conv_ffn_w, w_down=m_w_down, norm_final_g=m_norm_final_g)
    v = dict(norm_mix_g=v_norm_mix_g, w_in=v_w_in, conv_qkv_w=v_conv_qkv_w, gdn_a_log=v_gdn_a_log, gdn_dt_bias=v_gdn_dt_bias,
             gdn_norm_g=v_gdn_norm_g, mla_q_norm_g=v_mla_q_norm_g, w_uq=v_w_uq, mla_kv_norm_g=v_mla_kv_norm_g, w_ukv=v_w_ukv,
             w_o_gdn=v_w_o_gdn, w_o_mla=v_w_o_mla, w_out=v_w_out, norm_ffn_g=v_norm_ffn_g, w_up=v_w_up,
             conv_ffn_w=v_conv_ffn_w, w_down=v_w_down, norm_final_g=v_norm_final_g)
    slab_names = ("A", "Q", "C", "V")
    big_names = ("w_in", "w_up", "w_uq", "w_ukv", "w_o_gdn", "w_o_mla", "w_out", "w_down", "conv_qkv_w", "conv_ffn_w")
    small_names = [n for n, _ in SMALL]
    small_shapes = {n: w[n].shape for n in small_names}
    local2d = lambda d: {n: d[n][0] for n in big_names}

    w_slabs = _slabs(local2d(w), F32)
    a_bf = w_slabs["A"].astype(BF16)
    first = _allgather_async([a_bf[:1024]], name="allgather_w_in", collective_id=1)
    second = _allgather_async([w_slabs["Q"].astype(BF16), w_slabs["V"]], name="allgather_mixers", collective_id=2)
    third = _allgather_async([a_bf[1024:], w_slabs["C"].astype(BF16)], name="allgather_ffn_out", collective_id=3)
    gathered = {k: g.reshape(N_DEV, -1, g.shape[1])
                for k, g in zip(("A_in", "Q", "V", "A_up", "C"), first + second + third)}
    W, conv_qkv_full, conv_ffn_full = _layout_weights(gathered)

    def late_weights(tie):
        tie, a_up, c_all = lax.optimization_barrier((tie, gathered["A_up"], gathered["C"]))
        return tie, _layout_late(a_up, c_all)

    loss_local, dx, dW, d_conv_qkv, d_conv_ffn, dsmall = _local_step(
        x, loss_target, W, late_weights, conv_qkv_full, conv_ffn_full, {n: w[n] for n in small_names})

    g_send = _send_slabs(dW, d_conv_qkv, d_conv_ffn)
    recv = _alltoall_async([g_send[k] for k in slab_names], name="alltoall_grads", collective_id=0)
    small_parts = _alltoall_async([jnp.tile(_pack_small(dsmall), (N_DEV, 1))], name="alltoall_small_grads",
                                  collective_id=4)[0]
    m_slabs, v_slabs = _slabs(local2d(m), F32), _slabs(local2d(v), F32)
    upd = {k: _reduce_adam(r, w_slabs[k], m_slabs[k], v_slabs[k], tr=SLAB_TR[k], name="adam_" + k)
           for k, r in zip(slab_names, recv)}
    upd_small = _reduce_adam(small_parts, _pack_small({n: w[n] for n in small_names}), _pack_small({n: m[n] for n in small_names}),
                             _pack_small({n: v[n] for n in small_names}), tr=SMALL_ROWS, name="adam_small")

    loss = lax.psum(loss_local, ("x", "y", "c"))
    groups = []
    for i in range(4):
        merged = {**_unslab({k: upd[k][i] for k in slab_names}), **_unpack_small(upd_small[i], small_shapes)}
        groups.append([merged[n] for n in WEIGHT_ORDER])
    return (loss, dx, *groups[0], *groups[1], *groups[2], *groups[3])
```

```python
import functools
import math

import numpy as np
import jax
import jax.numpy as jnp
from jax import lax
from jax.experimental import pallas as pl
from jax.experimental.pallas import tpu as pltpu
from jax.experimental.pallas import tpu_sc as plsc

F32 = jnp.float32
BF16 = jnp.bfloat16

D_MODEL = 1024
HEADS = 8
HD = 128
GDN_CONV = 4
CHUNK = 64
Q_RANK = 384
KV_RANK = 256
ROPE = 64
ROPE_THETA = 10000.0
D_FF = 2816
FFN_CONV = 3
EPS = 1e-6
SM_SCALE = (HD + ROPE) ** -0.5
N_DEV = 8

ADAM_LR, ADAM_B1, ADAM_B2, ADAM_EPS, ADAM_WD, ADAM_STEP = 0.001, 0.9, 0.999, 1e-08, 0.01, 10

LANES = 128
SUBLANES = 8
HALO = 2 * SUBLANES
VMEM_LIMIT = 56 * 1024 * 1024
HI = lax.Precision.HIGHEST
TRI_PRECISION = None

NN = (((1,), (0,)), ((), ()))
NT = (((1,), (1,)), ((), ()))
TN = (((0,), (0,)), ((), ()))


def _dot(a, b, dims=NN, precision=None):
    return lax.dot_general(a, b, dims, precision=precision, preferred_element_type=F32)


def _pick(dim, target, align):
    best = None
    for t in range(align, min(dim, target) + 1, align):
        if dim % t == 0:
            best = t
    return dim if best is None else best


def _call(body, ins, outs, grid, *, name, scratch=(), semantics=None):
    n_in, n_out = len(ins), len(outs)

    def kern(*refs):
        body(refs[:n_in], refs[n_in:n_in + n_out], refs[n_in + n_out:])

    res = pl.pallas_call(
        kern,
        grid=grid,
        in_specs=[pl.BlockSpec(bs, im) for _, bs, im in ins],
        out_specs=[pl.BlockSpec(bs, im) for _, _, bs, im in outs],
        out_shape=[jax.ShapeDtypeStruct(s, d) for s, d, _, _ in outs],
        scratch_shapes=list(scratch),
        name=name,
        compiler_params=pltpu.CompilerParams(
            dimension_semantics=semantics or ("arbitrary",) * len(grid), vmem_limit_bytes=VMEM_LIMIT),
    )(*[a for a, _, _ in ins])
    return res


def _mm(a, b, mode, out_dtype, *, name, add=None, tm=1408, tn=1408, tk=1408):
    if mode == "nn":
        (M, K), (K2, N) = a.shape, b.shape
    elif mode == "nt":
        (M, K), (N, K2) = a.shape, b.shape
    else:
        (K, M), (K2, N) = a.shape, b.shape
    assert K == K2, (a.shape, b.shape, mode)
    tm = _pick(M, tm, LANES if mode == "tn" else 16)
    tn = _pick(N, tn, LANES)
    tk = _pick(K, tk, 16 if mode == "tn" else LANES)
    nk = K // tk
    dims = {"nn": NN, "nt": NT, "tn": TN}[mode]
    if mode == "nn":
        a_spec, b_spec = ((tm, tk), lambda i, j, k: (i, k)), ((tk, tn), lambda i, j, k: (k, j))
    elif mode == "nt":
        a_spec, b_spec = ((tm, tk), lambda i, j, k: (i, k)), ((tn, tk), lambda i, j, k: (j, k))
    else:
        a_spec, b_spec = ((tk, tm), lambda i, j, k: (k, i)), ((tk, tn), lambda i, j, k: (k, j))
    ins = [(a,) + a_spec, (b,) + b_spec]
    if add is not None:
        ins.append((add, (tm, tn), lambda i, j, k: (i, j)))
    outs = [((M, N), out_dtype, (tm, tn), lambda i, j, k: (i, j))]

    def body(in_refs, out_refs, scr):
        prod = _dot(in_refs[0][...].astype(BF16), in_refs[1][...].astype(BF16), dims)

        def finish(r):
            if add is not None:
                r = r + in_refs[2][...].astype(F32)
            out_refs[0][...] = r.astype(out_dtype)

        if nk == 1:
            finish(prod)
            return
        k = pl.program_id(2)
        acc = scr[0]

        @pl.when(k == 0)
        def _():
            acc[...] = prod

        @pl.when(k > 0)
        def _():
            acc[...] += prod

        @pl.when(k == nk - 1)
        def _():
            finish(acc[...])

    return _call(body, ins, outs, (M // tm, N // tn, nk), name=name,
                 scratch=[pltpu.VMEM((tm, tn), F32)] if nk > 1 else [],
                 semantics=("parallel", "parallel", "arbitrary"))[0]


def _row_call(fn, rows, consts, out_rows, out_accs=(), *, T, tm, name):
    nt = T // tm
    ins = []
    for r in rows:
        ins.append(r if isinstance(r, tuple) else (r, (tm, r.shape[1]), lambda i: (i, 0)))
    for c in consts:
        ins.append((c, c.shape, lambda i, nd=c.ndim: (0,) * nd))
    outs = []
    for o in out_rows:
        outs.append(((T, o[0]), o[1], (tm, o[0]), lambda i: (i, 0)) if len(o) == 2 else o)
    for shp, dt in out_accs:
        outs.append((shp, dt, shp, lambda i, nd=len(shp): (0,) * nd))
    n_r = len(out_rows)

    def body(in_refs, out_refs, _):
        i = pl.program_id(0)
        vals = fn(*[r[...] for r in in_refs])
        for o_ref, v in zip(out_refs[:n_r], vals[:n_r]):
            o_ref[...] = v.astype(o_ref.dtype)
        for o_ref, v in zip(out_refs[n_r:], vals[n_r:]):
            @pl.when(i == 0)
            def _(o_ref=o_ref):
                o_ref[...] = jnp.zeros_like(o_ref)

            o_ref[...] += v.astype(o_ref.dtype)

    return _call(body, ins, outs, (nt,), name=name)


def _rms(x, g):
    return x * lax.rsqrt(jnp.mean(x * x, axis=-1, keepdims=True) + EPS) * g


def _norm_fwd(x, g, *, T, tm, name):
    return _row_call(lambda xt, gt: (_rms(xt, gt),), [x], [g], [(x.shape[1], BF16)], T=T, tm=tm, name=name)[0]


def _norm_bwd(x, g, dh, dres, *, T, tm, name):
    def fn(xt, dht, drt, gt):
        _, vjp = jax.vjp(_rms, xt, gt)
        dx, dg = vjp(dht)
        return drt + dx, dg

    return _row_call(fn, [x, dh, dres], [g], [(x.shape[1], F32)], [(g.shape, F32)], T=T, tm=tm, name=name)


def _rows16(c):
    return lax.broadcasted_iota(jnp.int32, (HALO, c), 0)


@functools.lru_cache(maxsize=None)
def _shift_fn(j):
    @jax.custom_vjp
    def shift(x, halo):
        xr = pltpu.roll(x, j, 0)
        top = jnp.where(_rows16(x.shape[1]) < j, pltpu.roll(halo, j, 0), xr[:HALO])
        return jnp.concatenate([top, xr[HALO:]], axis=0)

    def fwd(x, halo):
        return shift(x, halo), None

    def bwd(_, dy):
        tm, c = dy.shape
        keep = _rows16(c) >= HALO - j
        dxr = pltpu.roll(dy, tm - j, 0)
        dx = jnp.concatenate([dxr[:tm - HALO], jnp.where(keep, 0.0, dxr[tm - HALO:])], axis=0)
        dhalo = jnp.where(keep, pltpu.roll(dy[:HALO], HALO - j, 0), 0.0)
        return dx, dhalo

    shift.defvjp(fwd, bwd)
    return shift


def _dwconv(tail, x, w):
    K = w.shape[0]
    acc = w[K - 1:K, :] * x
    for k in range(K - 1):
        acc = acc + w[k:k + 1, :] * _shift_fn(K - 1 - k)(x, tail)
    return acc


STRIP = 64


def _conv_fwd(fn, xs, ws, out_c, out_dtype, *, T, S, tm, cb, ncb, name):
    nt, tps, hb = T // tm, S // tm, tm // HALO
    ins = []
    for arr, off in xs:
        ins.append((arr, (tm, cb), lambda j, i, off=off: (i, off + j)))
        ins.append((arr, (HALO, cb), lambda j, i, off=off: (jnp.maximum(i * hb - 1, 0), off + j)))
    for arr, off in ws:
        ins.append((arr, (arr.shape[0], cb), lambda j, i, off=off: (0, off + j)))
    outs = [((T, out_c), out_dtype, (tm, cb), lambda j, i: (i, j))]
    nx = len(xs)

    def body(in_refs, out_refs, _):
        j, i = pl.program_id(0), pl.program_id(1)
        first = (i % tps) == 0
        wts = [r[...] for r in in_refs[2 * nx:]]
        for r in range(0, tm, STRIP):
            xts = [in_refs[2 * m][r:r + STRIP, :].astype(F32) for m in range(nx)]
            if r == 0:
                tails = [jnp.where(first, 0.0, in_refs[2 * m + 1][...].astype(F32)) for m in range(nx)]
            else:
                tails = [in_refs[2 * m][r - HALO:r, :].astype(F32) for m in range(nx)]
            out_refs[0][r:r + STRIP, :] = fn(j, tails, xts, wts).astype(out_dtype)

    return _call(body, ins, outs, (ncb, nt), name=name)[0]


def _conv_bwd(fn, xs, ws, dout, dx_dtype, *, T, S, tm, cb, ncb, name):
    nt, tps, hb = T // tm, S // tm, tm // HALO
    ins = []
    for arr, off in xs:
        ins.append((arr, (tm, cb), lambda j, i, off=off: (nt - 1 - i, off + j)))
        ins.append((arr, (HALO, cb), lambda j, i, off=off: (jnp.maximum((nt - 1 - i) * hb - 1, 0), off + j)))
    for arr, off in ws:
        ins.append((arr, (arr.shape[0], cb), lambda j, i, off=off: (0, off + j)))
    ins.append((dout, (tm, cb), lambda j, i: (nt - 1 - i, j)))
    nx, nw = len(xs), len(ws)
    outs = [((T, ncb * cb), dx_dtype, (tm, cb), lambda j, i: (nt - 1 - i, j)) for _ in xs]
    outs += [((arr.shape[0], ncb * cb), F32, (arr.shape[0], cb), lambda j, i: (0, j)) for arr, _ in ws]
    scratch = [pltpu.VMEM((HALO, cb), F32) for _ in xs]

    def body(in_refs, out_refs, carry):
        j, i = pl.program_id(0), pl.program_id(1)
        first = ((nt - 1 - i) % tps) == 0
        wts = [ref[...] for ref in in_refs[2 * nx:2 * nx + nw]]

        @pl.when(i == 0)
        def _():
            for c in carry:
                c[...] = jnp.zeros_like(c)

        carried = [c[...] for c in carry]
        dw_sum = None
        for r in reversed(range(0, tm, STRIP)):
            xts = [in_refs[2 * m][r:r + STRIP, :].astype(F32) for m in range(nx)]
            if r == 0:
                tails = [jnp.where(first, 0.0, in_refs[2 * m + 1][...].astype(F32)) for m in range(nx)]
            else:
                tails = [in_refs[2 * m][r - HALO:r, :].astype(F32) for m in range(nx)]
            _, vjp = jax.vjp(lambda tl, xt, wt: fn(j, tl, xt, wt), tails, xts, wts)
            dtails, dxts, dwts = vjp(in_refs[-1][r:r + STRIP, :].astype(F32))
            for m in range(nx):
                pad = jnp.concatenate([jnp.zeros((STRIP - HALO, cb), F32), carried[m]], axis=0)
                out_refs[m][r:r + STRIP, :] = (dxts[m] + pad).astype(dx_dtype)
            carried = [jnp.where(first, 0.0, dt) for dt in dtails] if r == 0 else list(dtails)
            dw_sum = list(dwts) if dw_sum is None else [a + b for a, b in zip(dw_sum, dwts)]
        for m in range(nx):
            carry[m][...] = carried[m]
        for m in range(nw):
            o_ref = out_refs[nx + m]

            @pl.when(i == 0)
            def _(o_ref=o_ref):
                o_ref[...] = jnp.zeros_like(o_ref)

            o_ref[...] += dw_sum[m]

    return _call(body, ins, outs, (ncb, nt), name=name, scratch=scratch)


QKV_CB = 512


def _qkv_fn(j, tails, xts, wts):
    y = jax.nn.silu(_dwconv(tails[0], xts[0], wts[0]))
    scale = jnp.where(j < 1024 // QKV_CB, HD ** -0.5, 1.0)
    parts = []
    for h in range(QKV_CB // HD):
        yh = y[:, h * HD:(h + 1) * HD]
        nh = yh * lax.rsqrt(jnp.sum(yh * yh, axis=-1, keepdims=True) + EPS)
        parts.append(jnp.where(j < 2048 // QKV_CB, nh * scale, yh))
    return jnp.concatenate(parts, axis=1)


def _ffn_fn(j, tails, xts, wts):
    return jax.nn.silu(_dwconv(tails[0], xts[0], wts[0])) * _dwconv(tails[1], xts[1], wts[1])


BNN = (((2,), (1,)), ((0,), (0,)))
BNT = (((2,), (2,)), ((0,), (0,)))
BTN = (((1,), (1,)), ((0,), (0,)))


@jax.custom_vjp
def _tri_inv(L):
    C = L.shape[-1]
    ii = lax.broadcasted_iota(jnp.int32, (C, C), 0)
    jj = lax.broadcasted_iota(jnp.int32, (C, C), 1)
    eye = (ii == jj).astype(F32)
    X = eye - jnp.where((ii >> 1) == (jj >> 1), L, 0.0)
    s = 1
    while (2 << s) <= C:
        E = jnp.where(((ii >> (s + 1)) == (jj >> (s + 1))) & ((ii >> s) != (jj >> s)), L, 0.0)
        X = X - _dot(_dot(X, E, BNN, precision=TRI_PRECISION), X, BNN, precision=TRI_PRECISION)
        s += 1
    return X


def _tri_inv_fwd(L):
    X = _tri_inv(L)
    return X, X


def _tri_inv_bwd(X, dX):
    return (-_dot(_dot(X, dX, BTN, precision=TRI_PRECISION), X, BNT, precision=TRI_PRECISION),)


_tri_inv.defvjp(_tri_inv_fwd, _tri_inv_bwd)


def _gdn_chunk(q, k, v, gc, gr, beta, S):
    C = q.shape[1]
    ii = lax.broadcasted_iota(jnp.int32, (C, C), 0)
    jj = lax.broadcasted_iota(jnp.int32, (C, C), 1)
    lower = ii >= jj
    decay = jnp.where(lower, jnp.exp(jnp.where(lower, gc - gr, 0.0)), 0.0)
    kb, vb = k * beta, v * beta
    L = jnp.where(ii > jj, _dot(kb, k, BNT) * decay, 0.0)
    Tinv = _tri_inv(L)
    eg = jnp.exp(gc)
    u = _dot(Tinv, vb, BNN, precision=TRI_PRECISION)
    w = _dot(Tinv, kb * eg, BNN, precision=TRI_PRECISION)
    a = _dot(q, k, BNT) * decay
    g_last = gc[:, C - 1:C, :]
    kd = k * jnp.exp(g_last - gc)
    v_new = u - _dot(w, S, BNN)
    o = _dot(q * eg, S, BNN) + _dot(a, v_new, BNN)
    S_new = S * jnp.exp(g_last) + _dot(kd, v_new, BTN)
    return o, S_new


def _heads(ref, width=HD):
    return jnp.stack([ref[:, h * width:(h + 1) * width].astype(F32) for h in range(HEADS)])


def _gdn_fwd(qkvn, gcum, grT, beta, *, B, S):
    N, T = S // CHUNK, B * S
    row = lambda c: (lambda b, n: (b * N + n, c))
    ins = [(qkvn, (CHUNK, 1024), row(0)), (qkvn, (CHUNK, 1024), row(1)), (qkvn, (CHUNK, 1024), row(2)),
           (gcum, (CHUNK, LANES), row(0)), (grT, (1, HEADS, 1, CHUNK), lambda b, n: (b * N + n, 0, 0, 0)),
           (beta, (CHUNK, LANES), row(0))]
    outs = [((T, 1024), F32, (CHUNK, 1024), row(0)),
            ((B * N, HEADS, HD, HD), BF16, (1, HEADS, HD, HD), lambda b, n: (b * N + n, 0, 0, 0))]

    def body(in_refs, out_refs, scr):
        q_ref, k_ref, v_ref, gc_ref, gr_ref, b_ref = in_refs
        o_ref, st_ref = out_refs
        S_ref = scr[0]

        @pl.when(pl.program_id(1) == 0)
        def _():
            S_ref[...] = jnp.zeros_like(S_ref)

        S0 = S_ref[...]
        st_ref[0] = S0.astype(BF16)
        o, Sn = _gdn_chunk(_heads(q_ref), _heads(k_ref), _heads(v_ref), _heads(gc_ref, 1), gr_ref[0],
                           _heads(b_ref, 1), S0)
        for h in range(HEADS):
            o_ref[:, h * HD:(h + 1) * HD] = o[h]
        S_ref[...] = Sn

    return _call(body, ins, outs, (B, N), name="gdn_core_fwd", scratch=[pltpu.VMEM((HEADS, HD, HD), F32)])


def _gdn_bwd(qkvn, gcum, grT, beta, states, do, *, B, S):
    N, T = S // CHUNK, B * S
    row = lambda c: (lambda b, n: (b * N + N - 1 - n, c))
    ins = [(qkvn, (CHUNK, 1024), row(0)), (qkvn, (CHUNK, 1024), row(1)), (qkvn, (CHUNK, 1024), row(2)),
           (gcum, (CHUNK, LANES), row(0)), (grT, (1, HEADS, 1, CHUNK), lambda b, n: (b * N + N - 1 - n, 0, 0, 0)),
           (beta, (CHUNK, LANES), row(0)),
           (states, (1, HEADS, HD, HD), lambda b, n: (b * N + N - 1 - n, 0, 0, 0)), (do, (CHUNK, 1024), row(0))]
    outs = [((T, 3072), BF16, (CHUNK, 3072), row(0)), ((T, LANES), F32, (CHUNK, LANES), row(0)),
            ((B * N, HEADS, 1, CHUNK), F32, (1, HEADS, 1, CHUNK), lambda b, n: (b * N + N - 1 - n, 0, 0, 0)),
            ((T, LANES), F32, (CHUNK, LANES), row(0))]

    def body(in_refs, out_refs, scr):
        q_ref, k_ref, v_ref, gc_ref, gr_ref, b_ref, st_ref, do_ref = in_refs
        dqkv_ref, dgc_ref, dgr_ref, db_ref = out_refs
        dS_ref = scr[0]

        @pl.when(pl.program_id(1) == 0)
        def _():
            dS_ref[...] = jnp.zeros_like(dS_ref)

        args = (_heads(q_ref), _heads(k_ref), _heads(v_ref), _heads(gc_ref, 1), gr_ref[0], _heads(b_ref, 1),
                st_ref[0].astype(F32))
        _, vjp = jax.vjp(_gdn_chunk, *args)
        dq, dk, dv, dgc, dgr, db, dS = vjp((_heads(do_ref), dS_ref[...]))
        lane = lax.broadcasted_iota(jnp.int32, (CHUNK, LANES), 1)
        dgc_all = jnp.zeros((CHUNK, LANES), F32)
        db_all = jnp.zeros((CHUNK, LANES), F32)
        for h in range(HEADS):
            dqkv_ref[:, h * HD:(h + 1) * HD] = dq[h].astype(BF16)
            dqkv_ref[:, 1024 + h * HD:1024 + (h + 1) * HD] = dk[h].astype(BF16)
            dqkv_ref[:, 2048 + h * HD:2048 + (h + 1) * HD] = dv[h].astype(BF16)
            dgc_all = jnp.where(lane == h, dgc[h], dgc_all)
            db_all = jnp.where(lane == h, db[h], db_all)
        dgc_ref[...] = dgc_all
        db_ref[...] = db_all
        dgr_ref[0] = dgr
        dS_ref[...] = dS

    return _call(body, ins, outs, (B, N), name="gdn_core_bwd", scratch=[pltpu.VMEM((HEADS, HD, HD), F32)])


def _gate_fn(za, zb, alog, dtb):
    tm = za.shape[0]
    g = -jnp.exp(alog) * jax.nn.softplus(za + dtb)
    ii = lax.broadcasted_iota(jnp.int32, (tm, tm), 0)
    jj = lax.broadcasted_iota(jnp.int32, (tm, tm), 1)
    tri = ((ii >= jj) & ((ii >> 6) == (jj >> 6))).astype(F32)
    return _dot(tri, g, precision=HI), jax.nn.sigmoid(zb)


def _scores(qn_ref, qp_ref, kn_ref, kp_ref, diag):
    q = jnp.concatenate([qn_ref[...], qp_ref[...]], axis=1)
    k = jnp.concatenate([kn_ref[...], kp_ref[...]], axis=1)
    s = _dot(q, k, NT) * SM_SCALE
    if diag:
        t = s.shape[0]
        ii = lax.broadcasted_iota(jnp.int32, (t, t), 0)
        jj = lax.broadcasted_iota(jnp.int32, (t, t), 1)
        s = jnp.where(ii >= jj, s, -jnp.inf)
    return s, q, k


HPB = 8
HW = HPB * HD


def _head_refs(refs, hh):
    return [r.at[:, hh * HD:(hh + 1) * HD] for r in refs]


def _flash_fwd(qn, qp, kn, kp, v, *, B, S, t):
    nb, T = S // t, B * S
    qmap = lambda b, h, qi, ki: (b * nb + qi, h)
    kmap = lambda b, h, qi, ki: (b * nb + jnp.minimum(ki, qi), h)
    kpmap = lambda b, h, qi, ki: (b * nb + jnp.minimum(ki, qi), 0)
    ins = [(qn, (t, HW), qmap), (qp, (t, HW), qmap), (kn, (t, HW), kmap), (kp, (t, HD), kpmap), (v, (t, HW), kmap)]
    outs = [((T, 1024), BF16, (t, HW), qmap),
            ((HEADS, T, 1), F32, (HPB, t, 1), lambda b, h, qi, ki: (h, b * nb + qi, 0))]
    scratch = [pltpu.VMEM((HPB, t, 1), F32), pltpu.VMEM((HPB, t, 2 * HD), F32)]

    def body(in_refs, out_refs, scr):
        qn_ref, qp_ref, kn_ref, kp_ref, v_ref = in_refs
        o_ref, lse_ref = out_refs
        m_ref, acc_ref = scr
        qi, ki = pl.program_id(2), pl.program_id(3)

        @pl.when(ki == 0)
        def _():
            m_ref[...] = jnp.full_like(m_ref, -jnp.inf)
            acc_ref[...] = jnp.zeros_like(acc_ref)

        def step(diag):
            for hh in range(HPB):
                qn_h, qp_h, kn_h, v_h = _head_refs((qn_ref, qp_ref, kn_ref, v_ref), hh)
                s, _, _ = _scores(qn_h, qp_h, kn_h, kp_ref, diag)
                m_old = m_ref[hh]
                m_new = jnp.maximum(m_old, jnp.max(s, axis=-1, keepdims=True))
                p = jnp.exp(s - m_new)
                alpha = jnp.exp(m_old - m_new)
                v1 = jnp.concatenate([v_h[...], jnp.ones((t, HD), BF16)], axis=1)
                acc_ref[hh] = alpha * acc_ref[hh] + _dot(p.astype(BF16), v1)
                m_ref[hh] = m_new

        @pl.when(ki < qi)
        def _():
            step(False)

        @pl.when(ki == qi)
        def _():
            step(True)
            for hh in range(HPB):
                o_ref[:, hh * HD:(hh + 1) * HD] = (acc_ref[hh, :, :HD] / acc_ref[hh, :, HD:]).astype(BF16)
                lse_ref[hh] = m_ref[hh] + jnp.log(acc_ref[hh, :, HD:HD + 1])

    return _call(body, ins, outs, (B, HEADS // HPB, nb, nb), name="mla_flash_fwd", scratch=scratch,
                 semantics=("parallel", "parallel", "parallel", "arbitrary"))


def _flash_bwd_dq(qn, qp, kn, kp, v, o, do, lse, *, B, S, t):
    nb, T = S // t, B * S
    qmap = lambda b, h, qi, ki: (b * nb + qi, h)
    kmap = lambda b, h, qi, ki: (b * nb + jnp.minimum(ki, qi), h)
    kpmap = lambda b, h, qi, ki: (b * nb + jnp.minimum(ki, qi), 0)
    ins = [(qn, (t, HW), qmap), (qp, (t, HW), qmap), (kn, (t, HW), kmap), (kp, (t, HD), kpmap), (v, (t, HW), kmap),
           (o, (t, HW), qmap), (do, (t, HW), qmap), (lse, (HPB, t, 1), lambda b, h, qi, ki: (h, b * nb + qi, 0))]
    outs = [((T, 1024), BF16, (t, HW), qmap), ((T, 1024), F32, (t, HW), qmap),
            ((HEADS, T, 1), F32, (HPB, t, 1), lambda b, h, qi, ki: (h, b * nb + qi, 0))]
    scratch = [pltpu.VMEM((HPB, t, 1), F32), pltpu.VMEM((HPB, t, 2 * HD), F32)]

    def body(in_refs, out_refs, scr):
        qn_ref, qp_ref, kn_ref, kp_ref, v_ref, o_ref, do_ref, lse_ref = in_refs
        dqn_ref, dqp_ref, dlo_ref = out_refs
        dl_ref, acc_ref = scr
        qi, ki = pl.program_id(2), pl.program_id(3)

        @pl.when(ki == 0)
        def _():
            for hh in range(HPB):
                o_h, do_h = _head_refs((o_ref, do_ref), hh)
                dl_ref[hh] = jnp.sum(do_h[...].astype(F32) * o_h[...].astype(F32), axis=-1, keepdims=True)
            acc_ref[...] = jnp.zeros_like(acc_ref)

        def step(diag):
            for hh in range(HPB):
                qn_h, qp_h, kn_h, v_h, do_h = _head_refs((qn_ref, qp_ref, kn_ref, v_ref, do_ref), hh)
                s, _, k = _scores(qn_h, qp_h, kn_h, kp_ref, diag)
                p = jnp.exp(s - lse_ref[hh])
                dp = _dot(do_h[...], v_h[...], NT)
                ds = p * (dp - dl_ref[hh]) * SM_SCALE
                acc_ref[hh] += _dot(ds.astype(BF16), k)

        @pl.when(ki < qi)
        def _():
            step(False)

        @pl.when(ki == qi)
        def _():
            step(True)
            for hh in range(HPB):
                dqn_ref[:, hh * HD:(hh + 1) * HD] = acc_ref[hh, :, :HD].astype(BF16)
                dqp_ref[:, hh * HD:(hh + 1) * HD] = acc_ref[hh, :, HD:]
            dlo_ref[...] = dl_ref[...]

    return _call(body, ins, outs, (B, HEADS // HPB, nb, nb), name="mla_flash_bwd_dq", scratch=scratch,
                 semantics=("parallel", "parallel", "parallel", "arbitrary"))


def _flash_bwd_dkv(qn, qp, kn, kp, v, do, lse_t, dl_t, *, B, S, t):
    nb, T = S // t, B * S
    qmap = lambda b, h, ki, qi: (b * nb + jnp.maximum(qi, ki), h)
    kmap = lambda b, h, ki, qi: (b * nb + ki, h)
    tmap = lambda b, h, ki, qi: (h, 0, b * nb + jnp.maximum(qi, ki))
    ins = [(qn, (t, HW), qmap), (qp, (t, HW), qmap), (kn, (t, HW), kmap),
           (kp, (t, HD), lambda b, h, ki, qi: (b * nb + ki, 0)), (v, (t, HW), kmap), (do, (t, HW), qmap),
           (lse_t, (HPB, 1, t), tmap), (dl_t, (HPB, 1, t), tmap)]
    outs = [((T, 1024), BF16, (t, HW), kmap), ((HEADS, T, HD), F32, (HPB, t, HD), lambda b, h, ki, qi: (h, b * nb + ki, 0)),
            ((T, 1024), BF16, (t, HW), kmap)]
    scratch = [pltpu.VMEM((HPB, t, 2 * HD), F32), pltpu.VMEM((HPB, t, HD), F32)]

    def body(in_refs, out_refs, scr):
        qn_ref, qp_ref, kn_ref, kp_ref, v_ref, do_ref, lse_ref, dl_ref = in_refs
        dkn_ref, dkp_ref, dv_ref = out_refs
        dk_acc, dv_acc = scr
        ki, qi = pl.program_id(2), pl.program_id(3)

        @pl.when(qi == 0)
        def _():
            dk_acc[...] = jnp.zeros_like(dk_acc)
            dv_acc[...] = jnp.zeros_like(dv_acc)

        def step(diag):
            for hh in range(HPB):
                qn_h, qp_h, kn_h, v_h, do_h = _head_refs((qn_ref, qp_ref, kn_ref, v_ref, do_ref), hh)
                q = jnp.concatenate([qn_h[...], qp_h[...]], axis=1)
                k = jnp.concatenate([kn_h[...], kp_ref[...]], axis=1)
                st = _dot(k, q, NT) * SM_SCALE
                if diag:
                    ii = lax.broadcasted_iota(jnp.int32, (t, t), 0)
                    jj = lax.broadcasted_iota(jnp.int32, (t, t), 1)
                    st = jnp.where(ii <= jj, st, -jnp.inf)
                do_t = do_h[...]
                pt = jnp.exp(st - lse_ref[hh])
                dst = pt * (_dot(v_h[...], do_t, NT) - dl_ref[hh]) * SM_SCALE
                dv_acc[hh] += _dot(pt.astype(BF16), do_t)
                dk_acc[hh] += _dot(dst.astype(BF16), q)

        @pl.when(qi > ki)
        def _():
            step(False)

        @pl.when(qi == ki)
        def _():
            step(True)

        @pl.when(qi == nb - 1)
        def _():
            for hh in range(HPB):
                dkn_ref[:, hh * HD:(hh + 1) * HD] = dk_acc[hh, :, :HD].astype(BF16)
                dkp_ref[hh] = dk_acc[hh, :, HD:]
                dv_ref[:, hh * HD:(hh + 1) * HD] = dv_acc[hh].astype(BF16)

    return _call(body, ins, outs, (B, HEADS // HPB, nb, nb), name="mla_flash_bwd_dkv", scratch=scratch,
                 semantics=("parallel", "parallel", "parallel", "arbitrary"))


def _allgather_async(shards, *, name, collective_id):
    n_arr = len(shards)
    hbm = pltpu.MemorySpace.HBM
    x_refs = [jax.new_ref(a, memory_space=hbm) for a in shards]
    out_refs = [jax.empty_ref(jax.ShapeDtypeStruct((N_DEV * a.shape[0], a.shape[1]), a.dtype), memory_space=hbm)
                for a in shards]

    @pl.kernel(mesh=plsc.ScalarSubcoreMesh(axis_name="seq", num_cores=1), name=name,
               scratch_types=(pltpu.SemaphoreType.DMA((n_arr, 7)), pltpu.SemaphoreType.DMA((n_arr, 7)),
                              pltpu.SemaphoreType.DMA((n_arr,))),
               compiler_params=pltpu.CompilerParams(collective_id=collective_id))
    def launch(send_sems, recv_sems, local_sems):
        x, y, c = lax.axis_index("x"), lax.axis_index("y"), lax.axis_index("c")
        me, sibling = (x, y, c), (x, y, 1 - c)
        chips = [(1 - x, y), (x, 1 - y), (1 - x, 1 - y)]
        barrier = pltpu.get_barrier_semaphore()
        for p in [sibling] + [(*chip, c) for chip in chips]:
            pl.semaphore_signal(barrier, inc=1, device_id=p, device_id_type=pl.DeviceIdType.MESH)
        pl.semaphore_wait(barrier, 4)

        def rows(a, px, py, pc):
            m_per = shards[a].shape[0]
            return out_refs[a].at[pl.ds((4 * px + 2 * py + pc) * m_per, m_per), :]

        def copy(a, k, block, to, src=None):
            return pltpu.make_async_remote_copy(
                src_ref=rows(a, *block) if src is None else src, dst_ref=rows(a, *block),
                send_sem=send_sems.at[a, k], recv_sem=recv_sems.at[a, k], device_id=to,
                device_id_type=pl.DeviceIdType.MESH)

        mine = [pltpu.make_async_copy(x_refs[a], rows(a, *me), local_sems.at[a]) for a in range(n_arr)]
        for cp in mine:
            cp.start()
        first = []
        for a in range(n_arr):
            first.append(copy(a, 0, me, sibling, src=x_refs[a]))
            first += [copy(a, 1 + j, me, (*chip, c), src=x_refs[a]) for j, chip in enumerate(chips)]
        for cp in first:
            cp.start()
        passed = []
        for j, chip in enumerate(chips):
            for a in range(n_arr):
                copy(a, 1 + j, (*chip, c), me).wait_recv()
                cp = copy(a, 4 + j, (*chip, c), sibling)
                cp.start()
                passed.append(cp)
        for a in range(n_arr):
            copy(a, 0, sibling, me).wait_recv()
        for j, chip in enumerate(chips):
            for a in range(n_arr):
                copy(a, 4 + j, (*chip, 1 - c), me).wait_recv()
        for cp in first + passed:
            cp.wait_send()
        for cp in mine:
            cp.wait()

    launch()
    return [r[...] for r in out_refs]


def _alltoall_async(sends, *, name, collective_id):
    n_arr = len(sends)
    hbm = pltpu.MemorySpace.HBM
    s_refs = [jax.new_ref(a, memory_space=hbm) for a in sends]
    r_refs = [jax.empty_ref(jax.ShapeDtypeStruct(a.shape, a.dtype), memory_space=hbm) for a in sends]

    @pl.kernel(mesh=plsc.ScalarSubcoreMesh(axis_name="seq", num_cores=1), name=name,
               scratch_types=(pltpu.SemaphoreType.DMA((n_arr, 7)), pltpu.SemaphoreType.DMA((n_arr, 7)),
                              pltpu.SemaphoreType.DMA((n_arr,))),
               compiler_params=pltpu.CompilerParams(collective_id=collective_id))
    def launch(send_sems, recv_sems, local_sems):
        x, y, c = lax.axis_index("x"), lax.axis_index("y"), lax.axis_index("c")
        me = 4 * x + 2 * y + c
        peers = [(1 - x if k & 4 else x, 1 - y if k & 2 else y, 1 - c if k & 1 else c) for k in range(1, N_DEV)]
        barrier = pltpu.get_barrier_semaphore()
        for p in peers:
            pl.semaphore_signal(barrier, inc=1, device_id=p, device_id_type=pl.DeviceIdType.MESH)
        pl.semaphore_wait(barrier, N_DEV - 1)

        def rows(ref, a, idx):
            m_per = sends[a].shape[0] // N_DEV
            return ref.at[pl.ds(idx * m_per, m_per), :]

        local = [pltpu.make_async_copy(rows(s_refs[a], a, me), rows(r_refs[a], a, me), local_sems.at[a])
                 for a in range(n_arr)]
        for cp in local:
            cp.start()
        copies = []
        for k, (px, py, pc) in enumerate(peers):
            for a in range(n_arr):
                cp = pltpu.make_async_remote_copy(
                    src_ref=rows(s_refs[a], a, 4 * px + 2 * py + pc), dst_ref=rows(r_refs[a], a, me),
                    send_sem=send_sems.at[a, k], recv_sem=recv_sems.at[a, k],
                    device_id=(px, py, pc), device_id_type=pl.DeviceIdType.MESH)
                cp.start()
                copies.append(cp)
        for cp in copies:
            cp.wait()
        for cp in local:
            cp.wait()

    launch()
    return [r[...] for r in r_refs]


def _reduce_adam(parts, w, m, v, *, tr, name):
    R, C = w.shape
    nR = R // tr
    ins = [(parts, (tr, C), lambda i, s=s: (s * nR + i, 0)) for s in range(N_DEV)]
    ins += [(a, (tr, C), lambda i: (i, 0)) for a in (w, m, v)]
    outs = [((R, C), F32, (tr, C), lambda i: (i, 0)) for _ in range(4)]
    c1 = 1.0 - ADAM_B1 ** ADAM_STEP
    c2 = 1.0 - ADAM_B2 ** ADAM_STEP

    def body(in_refs, out_refs, _):
        g = in_refs[0][...].astype(F32)
        for s in range(1, N_DEV):
            g = g + in_refs[s][...].astype(F32)
        wv, mv, vv = in_refs[8][...], in_refs[9][...], in_refs[10][...]
        mn = ADAM_B1 * mv + (1.0 - ADAM_B1) * g
        vn = ADAM_B2 * vv + (1.0 - ADAM_B2) * (g * g)
        delta = -ADAM_LR * ((mn / c1) / (jnp.sqrt(vn / c2) + ADAM_EPS) + ADAM_WD * wv)
        out_refs[0][...] = g
        out_refs[1][...] = delta
        out_refs[2][...] = mn
        out_refs[3][...] = vn

    return _call(body, ins, outs, (nR,), name=name, semantics=("parallel",))


IN_C, UP_C, UQ_C, QKV_C = 858, 704, 192, 384
A_W, Q_W, V_W = 896, 256, 768
SLAB_TR = {"A": 256, "Q": 128, "C": 368, "V": 16}
SMALL = [("norm_mix_g", 1024), ("gdn_a_log", 8), ("gdn_dt_bias", 8), ("gdn_norm_g", 128), ("mla_q_norm_g", 384),
         ("mla_kv_norm_g", 256), ("norm_ffn_g", 1024), ("norm_final_g", 1024)]
SMALL_ROWS = 32
WEIGHT_ORDER = ["norm_mix_g", "w_in", "conv_qkv_w", "gdn_a_log", "gdn_dt_bias", "gdn_norm_g", "mla_q_norm_g", "w_uq",
                "mla_kv_norm_g", "w_ukv", "w_o_gdn", "w_o_mla", "w_out", "norm_ffn_g", "w_up", "conv_ffn_w", "w_down",
                "norm_final_g"]


def _padc(w, n):
    return jnp.pad(w, ((0, 0), (0, n - w.shape[1])))


def _padrc(w, r, n):
    return jnp.pad(w, ((0, r - w.shape[0]), (0, n - w.shape[1])))


def _slabs(p, dtype):
    A = jnp.concatenate([_padc(p["w_in"], A_W), _padc(p["w_up"], A_W)], axis=0).astype(dtype)
    Q = jnp.concatenate([_padc(p["w_uq"], Q_W), p["w_ukv"]], axis=0).astype(dtype)
    C = jnp.concatenate([p["w_o_gdn"], p["w_o_mla"], p["w_out"], p["w_down"]], axis=0).astype(dtype)
    V = jnp.concatenate([_padrc(p["conv_qkv_w"], 8, V_W), _padrc(p["conv_ffn_w"], 8, V_W)], axis=0).astype(F32)
    return {"A": A, "Q": Q, "C": C, "V": V}


def _unslab(sl):
    A, Q, C, V = sl["A"], sl["Q"], sl["C"], sl["V"]
    out = {"w_in": A[:1024, :IN_C], "w_up": A[1024:, :UP_C], "w_uq": Q[:384, :UQ_C], "w_ukv": Q[384:],
           "w_o_gdn": C[0:128], "w_o_mla": C[128:256], "w_out": C[256:384], "w_down": C[384:],
           "conv_qkv_w": V[0:GDN_CONV, :QKV_C], "conv_ffn_w": V[8:8 + FFN_CONV, :UP_C]}
    return {k: a[None] for k, a in out.items()}


def _take_cols(pieces, lo, hi):
    out, off = [], 0
    for arr, a, b in pieces:
        s, e = max(lo, off), min(hi, off + b - a)
        if s < e:
            out.append(arr[:, a + s - off:a + e - off])
        off += b - a
    return out[0] if len(out) == 1 else jnp.concatenate(out, axis=1)


LOSS_SLOT = sum(n for _, n in SMALL)


def _pack_small(d, loss=None):
    flat = jnp.concatenate([d[n].reshape(-1).astype(F32) for n, _ in SMALL]
                           + ([] if loss is None else [loss.reshape(1).astype(F32)]))
    return jnp.pad(flat, (0, SMALL_ROWS * LANES - flat.shape[0])).reshape(SMALL_ROWS, LANES)


def _unpack_small(buf, shapes):
    flat, out, off = buf.reshape(-1), {}, 0
    for name, n in SMALL:
        out[name] = flat[off:off + n].reshape(shapes[name])
        off += n
    return out


def _rot_cols(w):
    h = ROPE // 2
    return jnp.concatenate([-w[:, h:], w[:, :h]], axis=1)


def _unrot_cols(dw):
    h = ROPE // 2
    return jnp.concatenate([dw[:, h:], -dw[:, :h]], axis=1)


IN_SPLITS = [0, 3072, 4096, 4104, 4112, 4496, 4752, 4816, 5840, 6864]


def _layout_late(A_up, C):
    W = {"w_up": jnp.concatenate([A_up[j, :, :UP_C] for j in range(N_DEV)], axis=1),
         "w_o_gdn": C[:, 0:128].reshape(1024, D_MODEL), "w_o_mla": C[:, 128:256].reshape(1024, D_MODEL),
         "w_out": C[:, 256:384].reshape(1024, D_MODEL), "w_down": C[:, 384:].reshape(D_FF, D_MODEL)}
    return {k: v.astype(BF16) for k, v in W.items()}


def _layout_weights(g):
    A_in, Q, V = g["A_in"], g["Q"], g["V"]
    in_pieces = [(A_in[j], 0, IN_C) for j in range(N_DEV)]
    o = IN_SPLITS
    take = lambda lo, hi: _take_cols(in_pieces, lo, hi)
    kpe = take(o[6], o[7])
    W = {
        "in_qkv": take(o[0], o[1]),
        "in_ga": take(o[1], o[2]),
        "in_ab": jnp.concatenate([_padc(take(o[2], o[3]), LANES), _padc(take(o[3], o[4]), LANES)], axis=1),
        "in_small": jnp.concatenate([take(o[4], o[6]), _padc(kpe, LANES), _padc(_rot_cols(kpe), LANES)], axis=1),
        "in_gbr": take(o[7], o[9]),
        "uq_n": jnp.concatenate([Q[j, :384, :HD] for j in range(N_DEV)], axis=1),
        "ukv_k": jnp.concatenate([Q[j, 384:, :HD] for j in range(N_DEV)], axis=1),
        "ukv_v": jnp.concatenate([Q[j, 384:, HD:] for j in range(N_DEV)], axis=1),
    }
    pe = [Q[j, :384, HD:HD + ROPE] for j in range(N_DEV)]
    W["uq_p"] = jnp.concatenate([_padc(p, HD) for p in pe] + [_padc(_rot_cols(p), HD) for p in pe], axis=1)
    conv_qkv = jnp.concatenate([V[j, 0:GDN_CONV, :QKV_C] for j in range(N_DEV)], axis=1)
    conv_ffn = jnp.concatenate([V[j, 8:8 + FFN_CONV, :UP_C] for j in range(N_DEV)], axis=1)
    return {k: v.astype(BF16) for k, v in W.items()}, conv_qkv, conv_ffn


def _full_grads(dW):
    s = dW["in_small"]
    dkpe = s[:, 640:704] + _unrot_cols(s[:, 768:832])
    in_pieces = [(dW["in_qkv"], 0, 3072), (dW["in_ga"], 0, 1024), (dW["in_ab"], 0, 8), (dW["in_ab"], 128, 136),
                 (s, 0, 640), (dkpe, 0, ROPE), (dW["in_gbr"], 0, 2048)]
    pe = []
    for j in range(N_DEV):
        lin = dW["uq_p"][:, j * HD:j * HD + ROPE]
        rot = dW["uq_p"][:, 1024 + j * HD:1024 + j * HD + ROPE]
        pe.append(lin + _unrot_cols(rot))
    return in_pieces, pe


def _send_slabs(dW, d_conv_qkv, d_conv_ffn):
    in_pieces, pe = _full_grads(dW)
    A, Q, V = [], [], []
    for j in range(N_DEV):
        gin = _padc(_take_cols(in_pieces, j * IN_C, (j + 1) * IN_C), A_W)
        gup = _padc(dW["w_up"][:, j * UP_C:(j + 1) * UP_C], A_W)
        A.append(jnp.concatenate([gin, gup], axis=0))
        guq = _padc(jnp.concatenate([dW["uq_n"][:, j * HD:(j + 1) * HD], pe[j]], axis=1), Q_W)
        gukv = jnp.concatenate([dW["ukv_k"][:, j * HD:(j + 1) * HD], dW["ukv_v"][:, j * HD:(j + 1) * HD]], axis=1)
        Q.append(jnp.concatenate([guq, gukv], axis=0))
        V.append(jnp.concatenate([_padrc(d_conv_qkv[:, j * QKV_C:(j + 1) * QKV_C], 8, V_W),
                                  _padrc(d_conv_ffn[:, j * UP_C:(j + 1) * UP_C], 8, V_W)], axis=0))
    C = jnp.concatenate([dW["w_o_gdn"].reshape(N_DEV, 128, D_MODEL), dW["w_o_mla"].reshape(N_DEV, 128, D_MODEL),
                         dW["w_out"].reshape(N_DEV, 128, D_MODEL), dW["w_down"].reshape(N_DEV, 352, D_MODEL)], axis=1)
    return {"A": jnp.concatenate(A, axis=0).astype(BF16), "Q": jnp.concatenate(Q, axis=0).astype(BF16),
            "C": C.reshape(N_DEV * 736, D_MODEL).astype(BF16), "V": jnp.concatenate(V, axis=0)}


def _rope_tables(S):
    half = ROPE // 2
    inv = ROPE_THETA ** (-jnp.arange(half, dtype=F32) / half)
    ang = jnp.arange(S, dtype=F32)[:, None] * inv[None, :]
    cos = jnp.concatenate([jnp.cos(ang), jnp.cos(ang)], axis=1)
    sin = jnp.concatenate([jnp.sin(ang), jnp.sin(ang)], axis=1)
    return _padc(cos, HD), _padc(sin, HD)


def _local_step(x, tgt, W, late_weights, conv_qkv_w, conv_ffn_w, small, tm=None, ta=None):
    B, S, _ = x.shape
    T = B * S
    tm = tm or _pick(S, 512, CHUNK)
    ta = ta or _pick(S, 512, LANES)
    x2d, tgt2d = x.reshape(T, D_MODEL), tgt.reshape(T, D_MODEL)
    row = lambda v: v.reshape(1, -1).astype(F32)
    pad_row = lambda v: _padc(row(v), LANES)
    g_mix, g_ffn, g_fin = row(small["norm_mix_g"]), row(small["norm_ffn_g"]), row(small["norm_final_g"])
    g_gdn, g_q, g_kv = row(small["gdn_norm_g"]), row(small["mla_q_norm_g"]), row(small["mla_kv_norm_g"])
    alog, dtb = pad_row(small["gdn_a_log"]), pad_row(small["gdn_dt_bias"])
    cos, sin = _rope_tables(S)
    tps = S // tm
    tab = lambda a: (a, (tm, HD), lambda i: (i % tps, 0))
    col = lambda a, c, w: (a, (tm, w), lambda i, c=c: (i, c))

    h1 = _norm_fwd(x2d, g_mix, T=T, tm=tm, name="norm_mix_fwd")
    z_qkv = _mm(h1, W["in_qkv"], "nn", BF16, name="in_qkv_fwd")
    z_ga = _mm(h1, W["in_ga"], "nn", BF16, name="in_ga_fwd")
    z_ab = _mm(h1, W["in_ab"], "nn", F32, name="in_ab_fwd")
    z_small = _mm(h1, W["in_small"], "nn", F32, name="in_small_fwd", tn=896)
    z_gbr = _mm(h1, W["in_gbr"], "nn", BF16, name="in_gbr_fwd")

    qkvn = _conv_fwd(_qkv_fn, [(z_qkv, 0)], [(conv_qkv_w, 0)], 3072, BF16, T=T, S=S, tm=tm, cb=QKV_CB,
                     ncb=3072 // QKV_CB, name="gdn_qkv_fwd")
    gcum, beta = _row_call(lambda za, zb, al, db: _gate_fn(za, zb, al, db), [col(z_ab, 0, LANES), col(z_ab, 1, LANES)],
                           [alog, dtb], [(LANES, F32), (LANES, F32)], T=T, tm=tm, name="gdn_gate_fwd")
    grT = gcum[:, :HEADS].reshape(T // CHUNK, CHUNK, HEADS).transpose(0, 2, 1)[:, :, None, :]
    qkvn, late = late_weights(qkvn)
    W = {**W, **late}
    o_gdn, states = _gdn_fwd(qkvn, gcum, grT, beta, B=B, S=S)

    def gdn_out_fn(o, ga, g):
        parts = []
        for h in range(HEADS):
            sl = slice(h * HD, (h + 1) * HD)
            parts.append(_rms(o[:, sl], g) * jax.nn.silu(ga[:, sl].astype(F32)))
        return jnp.concatenate(parts, axis=1)

    oa = _row_call(lambda o, ga, g: (gdn_out_fn(o, ga, g),), [o_gdn, z_ga], [g_gdn], [(1024, BF16)], T=T, tm=tm,
                   name="gdn_out_fwd")[0]

    def mla_prep_fn(zq, zkv, zpl, zpr, c, s, gq, gkv):
        return _rms(zq, gq), _rms(zkv, gkv), zpl * c + zpr * s

    small_cols = [(z_small, (tm, Q_RANK), lambda i: (i, 0)), (z_small, (tm, LANES), lambda i: (i, 3)),
                  (z_small, (tm, LANES), lambda i: (i, 4)), (z_small, (tm, LANES), lambda i: (i, 5)),
                  (z_small, (tm, LANES), lambda i: (i, 6))]

    def mla_prep_fwd(zq, zkv0, zkv1, zpl, zpr, c, s, gq, gkv):
        return mla_prep_fn(zq, jnp.concatenate([zkv0, zkv1], axis=1), zpl, zpr, c, s, gq, gkv)

    cq, ckv, kpe = _row_call(mla_prep_fwd, small_cols + [tab(cos), tab(sin)], [g_q, g_kv],
                             [(Q_RANK, BF16), (KV_RANK, BF16), (HD, BF16)], T=T, tm=tm, name="mla_prep_fwd")
    qn = _mm(cq, W["uq_n"], "nn", BF16, name="uq_n_fwd")
    qpl = _mm(cq, W["uq_p"], "nn", F32, name="uq_p_fwd")
    kn = _mm(ckv, W["ukv_k"], "nn", BF16, name="ukv_k_fwd")
    vb = _mm(ckv, W["ukv_v"], "nn", BF16, name="ukv_v_fwd")

    def qrope_fn(lin, rot, c, s):
        return lin * jnp.tile(c, (1, HEADS)) + rot * jnp.tile(s, (1, HEADS))

    qp = _row_call(lambda lin, rot, c, s: (qrope_fn(lin, rot, c, s),), [col(qpl, 0, 1024), col(qpl, 1, 1024), tab(cos), tab(sin)],
                   [], [(1024, BF16)], T=T, tm=tm, name="q_rope_fwd")[0]
    ob, lse = _flash_fwd(qn, qp, kn, kpe, vb, B=B, S=S, t=ta)

    def merge_fn(ya, yb, ga, gb):
        return jax.nn.sigmoid(ga.astype(F32)) * ya + jax.nn.sigmoid(gb.astype(F32)) * yb

    def merge_fwd(oat, obt, ga, gb, wog, wom):
        ya, yb = _dot(oat, wog), _dot(obt, wom)
        return ya, yb, merge_fn(ya, yb, ga, gb)

    ya, yb, merged = _row_call(merge_fwd, [oa, ob, col(z_gbr, 0, 1024), col(z_gbr, 1, 1024)], [W["w_o_gdn"], W["w_o_mla"]],
                               [(1024, BF16), (1024, BF16), (1024, BF16)], T=T, tm=tm, name="merge_fwd")
    x1 = _mm(merged, W["w_out"], "nn", F32, add=x2d, name="w_out_fwd")

    h2 = _norm_fwd(x1, g_ffn, T=T, tm=tm, name="norm_ffn_fwd")
    up = _mm(h2, W["w_up"], "nn", BF16, name="w_up_fwd")
    FCB = 256
    nfb = D_FF // FCB
    f = _conv_fwd(_ffn_fn, [(up, 0), (up, 2)], [(conv_ffn_w, 0), (conv_ffn_w, 2)], D_FF, BF16, T=T, S=S, tm=tm,
                  cb=D_FF // 2, ncb=2, name="ffn_act_fwd")
    x2 = _mm(f, W["w_down"], "nn", F32, add=x1, name="w_down_fwd", tk=1408)

    def final_fn(xt, tt, g):
        def lossf(xv, gv):
            e = _rms(xv, gv) - tt
            return 0.5 * jnp.sum(jnp.mean(e * e, axis=-1))

        l, vjp = jax.vjp(lossf, xt, g)
        dx, dg = vjp(jnp.ones((), F32))
        return dx, jnp.full((1, LANES), l, F32), dg

    dx2, loss_v, dg_fin = _row_call(final_fn, [x2, tgt2d], [g_fin], [(1024, F32)], [((1, LANES), F32), ((1, 1024), F32)],
                                    T=T, tm=tm, name="loss_head")

    dW = {}
    df = _mm(dx2, W["w_down"], "nt", BF16, name="w_down_dx")
    dW["w_down"] = _mm(f, dx2, "tn", F32, name="w_down_dw")
    dug, duu, dcw_g, dcw_u = _conv_bwd(_ffn_fn, [(up, 0), (up, nfb)], [(conv_ffn_w, 0), (conv_ffn_w, nfb)], df, BF16,
                                       T=T, S=S, tm=tm, cb=FCB, ncb=nfb, name="ffn_act_bwd")
    d_conv_ffn = jnp.concatenate([dcw_g, dcw_u], axis=1)
    wup_g, wup_u = W["w_up"][:, :D_FF], W["w_up"][:, D_FF:]
    dh2 = _mm(dug, wup_g, "nt", F32, name="w_up_dx_g")
    dh2 = _mm(duu, wup_u, "nt", F32, add=dh2, name="w_up_dx_u")
    dW["w_up"] = jnp.concatenate([_mm(h2, dug, "tn", F32, name="w_up_dw_g"), _mm(h2, duu, "tn", F32, name="w_up_dw_u")], axis=1)
    dx1, dg_ffn = _norm_bwd(x1, g_ffn, dh2, dx2, T=T, tm=tm, name="norm_ffn_bwd")

    dmerged = _mm(dx1, W["w_out"], "nt", F32, name="w_out_dx")
    dW["w_out"] = _mm(merged, dx1, "tn", F32, name="w_out_dw")

    def merge_bwd(dm, yat, ybt, ga, gb):
        _, vjp = jax.vjp(merge_fn, yat.astype(F32), ybt.astype(F32), ga, gb)
        return vjp(dm)

    dya, dyb, dgbr_a, dgbr_b = _row_call(merge_bwd, [dmerged, ya, yb, col(z_gbr, 0, 1024), col(z_gbr, 1, 1024)], [],
                                         [(1024, BF16)] * 4, T=T, tm=tm, name="merge_bwd")
    doa = _mm(dya, W["w_o_gdn"], "nt", F32, name="w_o_gdn_dx")
    dob = _mm(dyb, W["w_o_mla"], "nt", BF16, name="w_o_mla_dx")
    dW["w_o_gdn"] = _mm(oa, dya, "tn", F32, name="w_o_gdn_dw")
    dW["w_o_mla"] = _mm(ob, dyb, "tn", F32, name="w_o_mla_dw")

    dqn, dqp, dl = _flash_bwd_dq(qn, qp, kn, kpe, vb, ob, dob, lse, B=B, S=S, t=ta)
    dkn, dkp, dvb = _flash_bwd_dkv(qn, qp, kn, kpe, vb, dob, lse.reshape(HEADS, 1, T), dl.reshape(HEADS, 1, T),
                                   B=B, S=S, t=ta)

    def qrope_bwd(d, c, s):
        return d * jnp.tile(c, (1, HEADS)), d * jnp.tile(s, (1, HEADS))

    dq_lin, dq_rot = _row_call(qrope_bwd, [dqp, tab(cos), tab(sin)], [], [(1024, BF16), (1024, BF16)], T=T, tm=tm,
                               name="q_rope_bwd")
    wp_lin, wp_rot = W["uq_p"][:, :1024], W["uq_p"][:, 1024:]
    dcq = _mm(dqn, W["uq_n"], "nt", F32, name="uq_n_dx")
    dcq = _mm(dq_lin, wp_lin, "nt", F32, add=dcq, name="uq_pl_dx")
    dcq = _mm(dq_rot, wp_rot, "nt", F32, add=dcq, name="uq_pr_dx")
    dW["uq_n"] = _mm(cq, dqn, "tn", F32, name="uq_n_dw")
    dW["uq_p"] = jnp.concatenate([_mm(cq, dq_lin, "tn", F32, name="uq_pl_dw"), _mm(cq, dq_rot, "tn", F32, name="uq_pr_dw")], axis=1)
    dckv = _mm(dkn, W["ukv_k"], "nt", F32, name="ukv_k_dx")
    dckv = _mm(dvb, W["ukv_v"], "nt", F32, add=dckv, name="ukv_v_dx")
    dW["ukv_k"] = _mm(ckv, dkn, "tn", F32, name="ukv_k_dw")
    dW["ukv_v"] = _mm(ckv, dvb, "tn", F32, name="ukv_v_dw")

    def mla_prep_bwd(zq, zkv0, zkv1, zpl, zpr, c, s, dcqt, dckvt, dkpt, gq, gkv):
        zkv = jnp.concatenate([zkv0, zkv1], axis=1)
        _, vjp = jax.vjp(lambda a, b, p, r, g1, g2: mla_prep_fn(a, b, p, r, c, s, g1, g2), zq, zkv, zpl, zpr, gq, gkv)
        dk = dkpt[0]
        for h in range(1, HEADS):
            dk = dk + dkpt[h]
        dzq, dzkv, dzpl, dzpr, dgq, dgkv = vjp((dcqt, dckvt, dk))
        return jnp.concatenate([dzq, dzkv, dzpl, dzpr], axis=1), dgq, dgkv

    dz_small, dg_q, dg_kv = _row_call(
        mla_prep_bwd, small_cols + [tab(cos), tab(sin), dcq, dckv, (dkp, (HEADS, tm, HD), lambda i: (0, i, 0))],
        [g_q, g_kv], [(896, BF16)], [((1, Q_RANK), F32), ((1, KV_RANK), F32)], T=T, tm=tm, name="mla_prep_bwd")

    def gdn_out_bwd(o, ga, dot_, g):
        _, vjp = jax.vjp(gdn_out_fn, o, ga, g)
        return vjp(dot_)

    do_gdn, dz_ga, dg_gdn = _row_call(gdn_out_bwd, [o_gdn, z_ga, doa], [g_gdn], [(1024, F32), (1024, BF16)],
                                      [((1, HD), F32)], T=T, tm=tm, name="gdn_out_bwd")
    dqkvn, dgc, dgrT, dbeta = _gdn_bwd(qkvn, gcum, grT, beta, states, do_gdn, B=B, S=S)
    dgc_tot = dgc + _padc(dgrT[:, :, 0, :].transpose(0, 2, 1).reshape(T, HEADS), LANES)

    def gate_bwd(za, zb, dg, db, al, db_):
        _, vjp = jax.vjp(_gate_fn, za, zb, al, db_)
        return vjp((dg, db))

    dz_a, dz_b, d_alog, d_dtb = _row_call(gate_bwd, [col(z_ab, 0, LANES), col(z_ab, 1, LANES), dgc_tot, dbeta], [alog, dtb],
                                          [(LANES, BF16), (LANES, BF16)], [((1, LANES), F32), ((1, LANES), F32)],
                                          T=T, tm=tm, name="gdn_gate_bwd")
    dz_qkv, d_conv_qkv = _conv_bwd(_qkv_fn, [(z_qkv, 0)], [(conv_qkv_w, 0)], dqkvn, BF16, T=T, S=S, tm=tm, cb=QKV_CB,
                                   ncb=3072 // QKV_CB, name="gdn_qkv_bwd")

    dz_ab = jnp.concatenate([dz_a, dz_b], axis=1)
    dz_gbr = jnp.concatenate([dgbr_a, dgbr_b], axis=1)
    dh1 = None
    for key, dz in (("in_qkv", dz_qkv), ("in_ga", dz_ga), ("in_ab", dz_ab), ("in_small", dz_small), ("in_gbr", dz_gbr)):
        dh1 = _mm(dz, W[key], "nt", F32, add=dh1, name=key + "_dx", tk=896 if key == "in_small" else 1024)
        dW[key] = _mm(h1, dz, "tn", F32, name=key + "_dw", tn=896 if key == "in_small" else 1024)
    dx, dg_mix = _norm_bwd(x2d, g_mix, dh1, dx1, T=T, tm=tm, name="norm_mix_bwd")

    dsmall = {"norm_mix_g": dg_mix, "gdn_a_log": d_alog[:, :HEADS], "gdn_dt_bias": d_dtb[:, :HEADS], "gdn_norm_g": dg_gdn,
              "mla_q_norm_g": dg_q, "mla_kv_norm_g": dg_kv, "norm_ffn_g": dg_ffn, "norm_final_g": dg_fin}
    return loss_v[0, 0], dx.reshape(B, S, D_MODEL), dW, d_conv_qkv, d_conv_ffn, dsmall


def kernel(x, norm_mix_g, w_in, conv_qkv_w, gdn_a_log, gdn_dt_bias, gdn_norm_g, mla_q_norm_g, w_uq, mla_kv_norm_g, w_ukv, w_o_gdn, w_o_mla, w_out, norm_ffn_g, w_up, conv_ffn_w, w_down, norm_final_g, loss_target, m_norm_mix_g, m_w_in, m_conv_qkv_w, m_gdn_a_log, m_gdn_dt_bias, m_gdn_norm_g, m_mla_q_norm_g, m_w_uq, m_mla_kv_norm_g, m_w_ukv, m_w_o_gdn, m_w_o_mla, m_w_out, m_norm_ffn_g, m_w_up, m_conv_ffn_w, m_w_down, m_norm_final_g, v_norm_mix_g, v_w_in, v_conv_qkv_w, v_gdn_a_log, v_gdn_dt_bias, v_gdn_norm_g, v_mla_q_norm_g, v_w_uq, v_mla_kv_norm_g, v_w_ukv, v_w_o_gdn, v_w_o_mla, v_w_out, v_norm_ffn_g, v_w_up, v_conv_ffn_w, v_w_down, v_norm_final_g):
    w = dict(norm_mix_g=norm_mix_g, w_in=w_in, conv_qkv_w=conv_qkv_w, gdn_a_log=gdn_a_log, gdn_dt_bias=gdn_dt_bias,
             gdn_norm_g=gdn_norm_g, mla_q_norm_g=mla_q_norm_g, w_uq=w_uq, mla_kv_norm_g=mla_kv_norm_g, w_ukv=w_ukv,
             w_o_gdn=w_o_gdn, w_o_mla=w_o_mla, w_out=w_out, norm_ffn_g=norm_ffn_g, w_up=w_up, conv_ffn_w=conv_ffn_w,
             w_down=w_down, norm_final_g=norm_final_g)
    m = dict(norm_mix_g=m_norm_mix_g, w_in=m_w_in, conv_qkv_w=m_conv_qkv_w, gdn_a_log=m_gdn_a_log, gdn_dt_bias=m_gdn_dt_bias,
             gdn_norm_g=m_gdn_norm_g, mla_q_norm_g=m_mla_q_norm_g, w_uq=m_w_uq, mla_kv_norm_g=m_mla_kv_norm_g, w_ukv=m_w_ukv,
             w_o_gdn=m_w_o_gdn, w_o_mla=m_w_o_mla, w_out=m_w_out, norm_ffn_g=m_norm_ffn_g, w_up=m_w_up,
             conv_ffn_w=m_conv_ffn_w, w_down=m_w_down, norm_final_g=m_norm_final_g)
    v = dict(norm_mix_g=v_norm_mix_g, w_in=v_w_in, conv_qkv_w=v_conv_qkv_w, gdn_a_log=v_gdn_a_log, gdn_dt_bias=v_gdn_dt_bias,
             gdn_norm_g=v_gdn_norm_g, mla_q_norm_g=v_mla_q_norm_g, w_uq=v_w_uq, mla_kv_norm_g=v_mla_kv_norm_g, w_ukv=v_w_ukv,
             w_o_gdn=v_w_o_gdn, w_o_mla=v_w_o_mla, w_out=v_w_out, norm_ffn_g=v_norm_ffn_g, w_up=v_w_up,
             conv_ffn_w=v_conv_ffn_w, w_down=v_w_down, norm_final_g=v_norm_final_g)
    slab_names = ("A", "Q", "C", "V")
    big_names = ("w_in", "w_up", "w_uq", "w_ukv", "w_o_gdn", "w_o_mla", "w_out", "w_down", "conv_qkv_w", "conv_ffn_w")
    small_names = [n for n, _ in SMALL]
    small_shapes = {n: w[n].shape for n in small_names}
    local2d = lambda d: {n: d[n][0] for n in big_names}

    w_slabs = _slabs(local2d(w), F32)
    a_bf = w_slabs["A"].astype(BF16)
    first = _allgather_async([a_bf[:1024]], name="allgather_w_in", collective_id=1)
    second = _allgather_async([w_slabs["Q"].astype(BF16), w_slabs["V"]], name="allgather_mixers", collective_id=2)
    third = _allgather_async([a_bf[1024:], w_slabs["C"].astype(BF16)], name="allgather_ffn_out", collective_id=3)
    gathered = {k: g.reshape(N_DEV, -1, g.shape[1])
                for k, g in zip(("A_in", "Q", "V", "A_up", "C"), first + second + third)}
    W, conv_qkv_full, conv_ffn_full = _layout_weights(gathered)

    def late_weights(tie):
        tie, a_up, c_all = lax.optimization_barrier((tie, gathered["A_up"], gathered["C"]))
        return tie, _layout_late(a_up, c_all)

    loss_local, dx, dW, d_conv_qkv, d_conv_ffn, dsmall = _local_step(
        x, loss_target, W, late_weights, conv_qkv_full, conv_ffn_full, {n: w[n] for n in small_names})

    g_send = _send_slabs(dW, d_conv_qkv, d_conv_ffn)
    recv = _alltoall_async([g_send[k] for k in slab_names], name="alltoall_grads", collective_id=0)
    small_parts = _alltoall_async([jnp.tile(_pack_small(dsmall, loss_local), (N_DEV, 1))], name="alltoall_small_grads",
                                  collective_id=4)[0]
    m_slabs, v_slabs = _slabs(local2d(m), F32), _slabs(local2d(v), F32)
    upd = {k: _reduce_adam(r, w_slabs[k], m_slabs[k], v_slabs[k], tr=SLAB_TR[k], name="adam_" + k)
           for k, r in zip(slab_names, recv)}
    upd_small = _reduce_adam(small_parts, _pack_small({n: w[n] for n in small_names}), _pack_small({n: m[n] for n in small_names}),
                             _pack_small({n: v[n] for n in small_names}), tr=SMALL_ROWS, name="adam_small")

    loss = upd_small[0].reshape(-1)[LOSS_SLOT]
    groups = []
    for i in range(4):
        merged = {**_unslab({k: upd[k][i] for k in slab_names}), **_unpack_small(upd_small[i], small_shapes)}
        groups.append([merged[n] for n in WEIGHT_ORDER])
    return (loss, dx, *groups[0], *groups[1], *groups[2], *groups[3])
```

```python
import functools
import math

import numpy as np
import jax
import jax.numpy as jnp
from jax import lax
from jax.experimental import pallas as pl
from jax.experimental.pallas import tpu as pltpu
from jax.experimental.pallas import tpu_sc as plsc

F32 = jnp.float32
BF16 = jnp.bfloat16

D_MODEL = 1024
HEADS = 8
HD = 128
GDN_CONV = 4
CHUNK = 64
Q_RANK = 384
KV_RANK = 256
ROPE = 64
ROPE_THETA = 10000.0
D_FF = 2816
FFN_CONV = 3
EPS = 1e-6
SM_SCALE = (HD + ROPE) ** -0.5
N_DEV = 8

ADAM_LR, ADAM_B1, ADAM_B2, ADAM_EPS, ADAM_WD, ADAM_STEP = 0.001, 0.9, 0.999, 1e-08, 0.01, 10

LANES = 128
SUBLANES = 8
HALO = 2 * SUBLANES
VMEM_LIMIT = 56 * 1024 * 1024
HI = lax.Precision.HIGHEST
TRI_PRECISION = None

NN = (((1,), (0,)), ((), ()))
NT = (((1,), (1,)), ((), ()))
TN = (((0,), (0,)), ((), ()))


def _dot(a, b, dims=NN, precision=None):
    return lax.dot_general(a, b, dims, precision=precision, preferred_element_type=F32)


def _pick(dim, target, align):
    best = None
    for t in range(align, min(dim, target) + 1, align):
        if dim % t == 0:
            best = t
    return dim if best is None else best


def _call(body, ins, outs, grid, *, name, scratch=(), semantics=None):
    n_in, n_out = len(ins), len(outs)

    def kern(*refs):
        body(refs[:n_in], refs[n_in:n_in + n_out], refs[n_in + n_out:])

    res = pl.pallas_call(
        kern,
        grid=grid,
        in_specs=[pl.BlockSpec(bs, im) for _, bs, im in ins],
        out_specs=[pl.BlockSpec(bs, im) for _, _, bs, im in outs],
        out_shape=[jax.ShapeDtypeStruct(s, d) for s, d, _, _ in outs],
        scratch_shapes=list(scratch),
        name=name,
        compiler_params=pltpu.CompilerParams(
            dimension_semantics=semantics or ("arbitrary",) * len(grid), vmem_limit_bytes=VMEM_LIMIT),
    )(*[a for a, _, _ in ins])
    return res


def _mm(a, b, mode, out_dtype, *, name, add=None, tm=1408, tn=1408, tk=1408):
    if mode == "nn":
        (M, K), (K2, N) = a.shape, b.shape
    elif mode == "nt":
        (M, K), (N, K2) = a.shape, b.shape
    else:
        (K, M), (K2, N) = a.shape, b.shape
    assert K == K2, (a.shape, b.shape, mode)
    tm = _pick(M, tm, LANES if mode == "tn" else 16)
    tn = _pick(N, tn, LANES)
    tk = _pick(K, tk, 16 if mode == "tn" else LANES)
    nk = K // tk
    dims = {"nn": NN, "nt": NT, "tn": TN}[mode]
    if mode == "nn":
        a_spec, b_spec = ((tm, tk), lambda i, j, k: (i, k)), ((tk, tn), lambda i, j, k: (k, j))
    elif mode == "nt":
        a_spec, b_spec = ((tm, tk), lambda i, j, k: (i, k)), ((tn, tk), lambda i, j, k: (j, k))
    else:
        a_spec, b_spec = ((tk, tm), lambda i, j, k: (k, i)), ((tk, tn), lambda i, j, k: (k, j))
    ins = [(a,) + a_spec, (b,) + b_spec]
    if add is not None:
        ins.append((add, (tm, tn), lambda i, j, k: (i, j)))
    outs = [((M, N), out_dtype, (tm, tn), lambda i, j, k: (i, j))]

    def body(in_refs, out_refs, scr):
        prod = _dot(in_refs[0][...].astype(BF16), in_refs[1][...].astype(BF16), dims)

        def finish(r):
            if add is not None:
                r = r + in_refs[2][...].astype(F32)
            out_refs[0][...] = r.astype(out_dtype)

        if nk == 1:
            finish(prod)
            return
        k = pl.program_id(2)
        acc = scr[0]

        @pl.when(k == 0)
        def _():
            acc[...] = prod

        @pl.when(k > 0)
        def _():
            acc[...] += prod

        @pl.when(k == nk - 1)
        def _():
            finish(acc[...])

    return _call(body, ins, outs, (M // tm, N // tn, nk), name=name,
                 scratch=[pltpu.VMEM((tm, tn), F32)] if nk > 1 else [],
                 semantics=("parallel", "parallel", "arbitrary"))[0]


def _row_call(fn, rows, consts, out_rows, out_accs=(), *, T, tm, name):
    nt = T // tm
    ins = []
    for r in rows:
        ins.append(r if isinstance(r, tuple) else (r, (tm, r.shape[1]), lambda i: (i, 0)))
    for c in consts:
        ins.append((c, c.shape, lambda i, nd=c.ndim: (0,) * nd))
    outs = []
    for o in out_rows:
        outs.append(((T, o[0]), o[1], (tm, o[0]), lambda i: (i, 0)) if len(o) == 2 else o)
    for shp, dt in out_accs:
        outs.append((shp, dt, shp, lambda i, nd=len(shp): (0,) * nd))
    n_r = len(out_rows)

    def body(in_refs, out_refs, _):
        i = pl.program_id(0)
        vals = fn(*[r[...] for r in in_refs])
        for o_ref, v in zip(out_refs[:n_r], vals[:n_r]):
            o_ref[...] = v.astype(o_ref.dtype)
        for o_ref, v in zip(out_refs[n_r:], vals[n_r:]):
            @pl.when(i == 0)
            def _(o_ref=o_ref):
                o_ref[...] = jnp.zeros_like(o_ref)

            o_ref[...] += v.astype(o_ref.dtype)

    return _call(body, ins, outs, (nt,), name=name)


def _rms(x, g):
    return x * lax.rsqrt(jnp.mean(x * x, axis=-1, keepdims=True) + EPS) * g


def _norm_fwd(x, g, *, T, tm, name):
    return _row_call(lambda xt, gt: (_rms(xt, gt),), [x], [g], [(x.shape[1], BF16)], T=T, tm=tm, name=name)[0]


def _norm_bwd(x, g, dh, dres, *, T, tm, name):
    def fn(xt, dht, drt, gt):
        _, vjp = jax.vjp(_rms, xt, gt)
        dx, dg = vjp(dht)
        return drt + dx, dg

    return _row_call(fn, [x, dh, dres], [g], [(x.shape[1], F32)], [(g.shape, F32)], T=T, tm=tm, name=name)


def _rows16(c):
    return lax.broadcasted_iota(jnp.int32, (HALO, c), 0)


@functools.lru_cache(maxsize=None)
def _shift_fn(j):
    @jax.custom_vjp
    def shift(x, halo):
        xr = pltpu.roll(x, j, 0)
        top = jnp.where(_rows16(x.shape[1]) < j, pltpu.roll(halo, j, 0), xr[:HALO])
        return jnp.concatenate([top, xr[HALO:]], axis=0)

    def fwd(x, halo):
        return shift(x, halo), None

    def bwd(_, dy):
        tm, c = dy.shape
        keep = _rows16(c) >= HALO - j
        dxr = pltpu.roll(dy, tm - j, 0)
        dx = jnp.concatenate([dxr[:tm - HALO], jnp.where(keep, 0.0, dxr[tm - HALO:])], axis=0)
        dhalo = jnp.where(keep, pltpu.roll(dy[:HALO], HALO - j, 0), 0.0)
        return dx, dhalo

    shift.defvjp(fwd, bwd)
    return shift


def _dwconv(tail, x, w):
    K = w.shape[0]
    acc = w[K - 1:K, :] * x
    for k in range(K - 1):
        acc = acc + w[k:k + 1, :] * _shift_fn(K - 1 - k)(x, tail)
    return acc


STRIP = 64


def _conv_fwd(fn, xs, ws, out_c, out_dtype, *, T, S, tm, cb, ncb, name):
    nt, tps, hb = T // tm, S // tm, tm // HALO
    ins = []
    for arr, off in xs:
        ins.append((arr, (tm, cb), lambda j, i, off=off: (i, off + j)))
        ins.append((arr, (HALO, cb), lambda j, i, off=off: (jnp.maximum(i * hb - 1, 0), off + j)))
    for arr, off in ws:
        ins.append((arr, (arr.shape[0], cb), lambda j, i, off=off: (0, off + j)))
    outs = [((T, out_c), out_dtype, (tm, cb), lambda j, i: (i, j))]
    nx = len(xs)

    def body(in_refs, out_refs, _):
        j, i = pl.program_id(0), pl.program_id(1)
        first = (i % tps) == 0
        wts = [r[...] for r in in_refs[2 * nx:]]
        for r in range(0, tm, STRIP):
            xts = [in_refs[2 * m][r:r + STRIP, :].astype(F32) for m in range(nx)]
            if r == 0:
                tails = [jnp.where(first, 0.0, in_refs[2 * m + 1][...].astype(F32)) for m in range(nx)]
            else:
                tails = [in_refs[2 * m][r - HALO:r, :].astype(F32) for m in range(nx)]
            out_refs[0][r:r + STRIP, :] = fn(j, tails, xts, wts).astype(out_dtype)

    return _call(body, ins, outs, (ncb, nt), name=name)[0]


def _conv_bwd(fn, xs, ws, dout, dx_dtype, *, T, S, tm, cb, ncb, name):
    nt, tps, hb = T // tm, S // tm, tm // HALO
    ins = []
    for arr, off in xs:
        ins.append((arr, (tm, cb), lambda j, i, off=off: (nt - 1 - i, off + j)))
        ins.append((arr, (HALO, cb), lambda j, i, off=off: (jnp.maximum((nt - 1 - i) * hb - 1, 0), off + j)))
    for arr, off in ws:
        ins.append((arr, (arr.shape[0], cb), lambda j, i, off=off: (0, off + j)))
    ins.append((dout, (tm, cb), lambda j, i: (nt - 1 - i, j)))
    nx, nw = len(xs), len(ws)
    outs = [((T, ncb * cb), dx_dtype, (tm, cb), lambda j, i: (nt - 1 - i, j)) for _ in xs]
    outs += [((arr.shape[0], ncb * cb), F32, (arr.shape[0], cb), lambda j, i: (0, j)) for arr, _ in ws]
    scratch = [pltpu.VMEM((HALO, cb), F32) for _ in xs]

    def body(in_refs, out_refs, carry):
        j, i = pl.program_id(0), pl.program_id(1)
        first = ((nt - 1 - i) % tps) == 0
        wts = [ref[...] for ref in in_refs[2 * nx:2 * nx + nw]]

        @pl.when(i == 0)
        def _():
            for c in carry:
                c[...] = jnp.zeros_like(c)

        carried = [c[...] for c in carry]
        dw_sum = None
        for r in reversed(range(0, tm, STRIP)):
            xts = [in_refs[2 * m][r:r + STRIP, :].astype(F32) for m in range(nx)]
            if r == 0:
                tails = [jnp.where(first, 0.0, in_refs[2 * m + 1][...].astype(F32)) for m in range(nx)]
            else:
                tails = [in_refs[2 * m][r - HALO:r, :].astype(F32) for m in range(nx)]
            _, vjp = jax.vjp(lambda tl, xt, wt: fn(j, tl, xt, wt), tails, xts, wts)
            dtails, dxts, dwts = vjp(in_refs[-1][r:r + STRIP, :].astype(F32))
            for m in range(nx):
                pad = jnp.concatenate([jnp.zeros((STRIP - HALO, cb), F32), carried[m]], axis=0)
                out_refs[m][r:r + STRIP, :] = (dxts[m] + pad).astype(dx_dtype)
            carried = [jnp.where(first, 0.0, dt) for dt in dtails] if r == 0 else list(dtails)
            dw_sum = list(dwts) if dw_sum is None else [a + b for a, b in zip(dw_sum, dwts)]
        for m in range(nx):
            carry[m][...] = carried[m]
        for m in range(nw):
            o_ref = out_refs[nx + m]

            @pl.when(i == 0)
            def _(o_ref=o_ref):
                o_ref[...] = jnp.zeros_like(o_ref)

            o_ref[...] += dw_sum[m]

    return _call(body, ins, outs, (ncb, nt), name=name, scratch=scratch)


QKV_CB = 512


def _qkv_fn(j, tails, xts, wts):
    y = jax.nn.silu(_dwconv(tails[0], xts[0], wts[0]))
    scale = jnp.where(j < 1024 // QKV_CB, HD ** -0.5, 1.0)
    parts = []
    for h in range(QKV_CB // HD):
        yh = y[:, h * HD:(h + 1) * HD]
        nh = yh * lax.rsqrt(jnp.sum(yh * yh, axis=-1, keepdims=True) + EPS)
        parts.append(jnp.where(j < 2048 // QKV_CB, nh * scale, yh))
    return jnp.concatenate(parts, axis=1)


def _ffn_fn(j, tails, xts, wts):
    return jax.nn.silu(_dwconv(tails[0], xts[0], wts[0])) * _dwconv(tails[1], xts[1], wts[1])


BNN = (((2,), (1,)), ((0,), (0,)))
BNT = (((2,), (2,)), ((0,), (0,)))
BTN = (((1,), (1,)), ((0,), (0,)))


@jax.custom_vjp
def _tri_inv(L):
    C = L.shape[-1]
    ii = lax.broadcasted_iota(jnp.int32, (C, C), 0)
    jj = lax.broadcasted_iota(jnp.int32, (C, C), 1)
    eye = (ii == jj).astype(F32)
    X = eye - jnp.where((ii >> 1) == (jj >> 1), L, 0.0)
    s = 1
    while (2 << s) <= C:
        E = jnp.where(((ii >> (s + 1)) == (jj >> (s + 1))) & ((ii >> s) != (jj >> s)), L, 0.0)
        X = X - _dot(_dot(X, E, BNN, precision=TRI_PRECISION), X, BNN, precision=TRI_PRECISION)
        s += 1
    return X


def _tri_inv_fwd(L):
    X = _tri_inv(L)
    return X, X


def _tri_inv_bwd(X, dX):
    return (-_dot(_dot(X, dX, BTN, precision=TRI_PRECISION), X, BNT, precision=TRI_PRECISION),)


_tri_inv.defvjp(_tri_inv_fwd, _tri_inv_bwd)


def _gdn_chunk(q, k, v, gc, gr, beta, S):
    C = q.shape[1]
    ii = lax.broadcasted_iota(jnp.int32, (C, C), 0)
    jj = lax.broadcasted_iota(jnp.int32, (C, C), 1)
    lower = ii >= jj
    decay = jnp.where(lower, jnp.exp(jnp.where(lower, gc - gr, 0.0)), 0.0)
    kb, vb = k * beta, v * beta
    L = jnp.where(ii > jj, _dot(kb, k, BNT) * decay, 0.0)
    Tinv = _tri_inv(L)
    eg = jnp.exp(gc)
    u = _dot(Tinv, vb, BNN, precision=TRI_PRECISION)
    w = _dot(Tinv, kb * eg, BNN, precision=TRI_PRECISION)
    a = _dot(q, k, BNT) * decay
    g_last = gc[:, C - 1:C, :]
    kd = k * jnp.exp(g_last - gc)
    v_new = u - _dot(w, S, BNN)
    o = _dot(q * eg, S, BNN) + _dot(a, v_new, BNN)
    S_new = S * jnp.exp(g_last) + _dot(kd, v_new, BTN)
    return o, S_new


def _heads(ref, width=HD):
    return jnp.stack([ref[:, h * width:(h + 1) * width].astype(F32) for h in range(HEADS)])


def _gdn_fwd(qkvn, gcum, grT, beta, *, B, S):
    N, T = S // CHUNK, B * S
    row = lambda c: (lambda b, n: (b * N + n, c))
    ins = [(qkvn, (CHUNK, 1024), row(0)), (qkvn, (CHUNK, 1024), row(1)), (qkvn, (CHUNK, 1024), row(2)),
           (gcum, (CHUNK, LANES), row(0)), (grT, (1, HEADS, 1, CHUNK), lambda b, n: (b * N + n, 0, 0, 0)),
           (beta, (CHUNK, LANES), row(0))]
    outs = [((T, 1024), F32, (CHUNK, 1024), row(0)),
            ((B * N, HEADS, HD, HD), BF16, (1, HEADS, HD, HD), lambda b, n: (b * N + n, 0, 0, 0))]

    def body(in_refs, out_refs, scr):
        q_ref, k_ref, v_ref, gc_ref, gr_ref, b_ref = in_refs
        o_ref, st_ref = out_refs
        S_ref = scr[0]

        @pl.when(pl.program_id(1) == 0)
        def _():
            S_ref[...] = jnp.zeros_like(S_ref)

        S0 = S_ref[...]
        st_ref[0] = S0.astype(BF16)
        o, Sn = _gdn_chunk(_heads(q_ref), _heads(k_ref), _heads(v_ref), _heads(gc_ref, 1), gr_ref[0],
                           _heads(b_ref, 1), S0)
        for h in range(HEADS):
            o_ref[:, h * HD:(h + 1) * HD] = o[h]
        S_ref[...] = Sn

    return _call(body, ins, outs, (B, N), name="gdn_core_fwd", scratch=[pltpu.VMEM((HEADS, HD, HD), F32)])


def _gdn_bwd(qkvn, gcum, grT, beta, states, do, *, B, S):
    N, T = S // CHUNK, B * S
    row = lambda c: (lambda b, n: (b * N + N - 1 - n, c))
    ins = [(qkvn, (CHUNK, 1024), row(0)), (qkvn, (CHUNK, 1024), row(1)), (qkvn, (CHUNK, 1024), row(2)),
           (gcum, (CHUNK, LANES), row(0)), (grT, (1, HEADS, 1, CHUNK), lambda b, n: (b * N + N - 1 - n, 0, 0, 0)),
           (beta, (CHUNK, LANES), row(0)),
           (states, (1, HEADS, HD, HD), lambda b, n: (b * N + N - 1 - n, 0, 0, 0)), (do, (CHUNK, 1024), row(0))]
    outs = [((T, 3072), BF16, (CHUNK, 3072), row(0)), ((T, LANES), F32, (CHUNK, LANES), row(0)),
            ((B * N, HEADS, 1, CHUNK), F32, (1, HEADS, 1, CHUNK), lambda b, n: (b * N + N - 1 - n, 0, 0, 0)),
            ((T, LANES), F32, (CHUNK, LANES), row(0))]

    def body(in_refs, out_refs, scr):
        q_ref, k_ref, v_ref, gc_ref, gr_ref, b_ref, st_ref, do_ref = in_refs
        dqkv_ref, dgc_ref, dgr_ref, db_ref = out_refs
        dS_ref = scr[0]

        @pl.when(pl.program_id(1) == 0)
        def _():
            dS_ref[...] = jnp.zeros_like(dS_ref)

        args = (_heads(q_ref), _heads(k_ref), _heads(v_ref), _heads(gc_ref, 1), gr_ref[0], _heads(b_ref, 1),
                st_ref[0].astype(F32))
        _, vjp = jax.vjp(_gdn_chunk, *args)
        dq, dk, dv, dgc, dgr, db, dS = vjp((_heads(do_ref), dS_ref[...]))
        lane = lax.broadcasted_iota(jnp.int32, (CHUNK, LANES), 1)
        dgc_all = jnp.zeros((CHUNK, LANES), F32)
        db_all = jnp.zeros((CHUNK, LANES), F32)
        for h in range(HEADS):
            dqkv_ref[:, h * HD:(h + 1) * HD] = dq[h].astype(BF16)
            dqkv_ref[:, 1024 + h * HD:1024 + (h + 1) * HD] = dk[h].astype(BF16)
            dqkv_ref[:, 2048 + h * HD:2048 + (h + 1) * HD] = dv[h].astype(BF16)
            dgc_all = jnp.where(lane == h, dgc[h], dgc_all)
            db_all = jnp.where(lane == h, db[h], db_all)
        dgc_ref[...] = dgc_all
        db_ref[...] = db_all
        dgr_ref[0] = dgr
        dS_ref[...] = dS

    return _call(body, ins, outs, (B, N), name="gdn_core_bwd", scratch=[pltpu.VMEM((HEADS, HD, HD), F32)])


def _gate_fn(za, zb, alog, dtb):
    tm = za.shape[0]
    g = -jnp.exp(alog) * jax.nn.softplus(za + dtb)
    ii = lax.broadcasted_iota(jnp.int32, (tm, tm), 0)
    jj = lax.broadcasted_iota(jnp.int32, (tm, tm), 1)
    tri = ((ii >= jj) & ((ii >> 6) == (jj >> 6))).astype(F32)
    return _dot(tri, g, precision=HI), jax.nn.sigmoid(zb)


def _scores(qn_ref, qp_ref, kn_ref, kp_ref, diag):
    q = jnp.concatenate([qn_ref[...], qp_ref[...]], axis=1)
    k = jnp.concatenate([kn_ref[...], kp_ref[...]], axis=1)
    s = _dot(q, k, NT) * SM_SCALE
    if diag:
        t = s.shape[0]
        ii = lax.broadcasted_iota(jnp.int32, (t, t), 0)
        jj = lax.broadcasted_iota(jnp.int32, (t, t), 1)
        s = jnp.where(ii >= jj, s, -jnp.inf)
    return s, q, k


HPB = 8
HW = HPB * HD


def _head_refs(refs, hh):
    return [r.at[:, hh * HD:(hh + 1) * HD] for r in refs]


def _flash_fwd(qn, qp, kn, kp, v, *, B, S, t):
    nb, T = S // t, B * S
    qmap = lambda b, h, qi, ki: (b * nb + qi, h)
    kmap = lambda b, h, qi, ki: (b * nb + jnp.minimum(ki, qi), h)
    kpmap = lambda b, h, qi, ki: (b * nb + jnp.minimum(ki, qi), 0)
    ins = [(qn, (t, HW), qmap), (qp, (t, HW), qmap), (kn, (t, HW), kmap), (kp, (t, HD), kpmap), (v, (t, HW), kmap)]
    outs = [((T, 1024), BF16, (t, HW), qmap),
            ((HEADS, T, 1), F32, (HPB, t, 1), lambda b, h, qi, ki: (h, b * nb + qi, 0))]
    scratch = [pltpu.VMEM((HPB, t, 1), F32), pltpu.VMEM((HPB, t, 2 * HD), F32)]

    def body(in_refs, out_refs, scr):
        qn_ref, qp_ref, kn_ref, kp_ref, v_ref = in_refs
        o_ref, lse_ref = out_refs
        m_ref, acc_ref = scr
        qi, ki = pl.program_id(2), pl.program_id(3)

        @pl.when(ki == 0)
        def _():
            m_ref[...] = jnp.full_like(m_ref, -jnp.inf)
            acc_ref[...] = jnp.zeros_like(acc_ref)

        def step(diag):
            for hh in range(HPB):
                qn_h, qp_h, kn_h, v_h = _head_refs((qn_ref, qp_ref, kn_ref, v_ref), hh)
                s, _, _ = _scores(qn_h, qp_h, kn_h, kp_ref, diag)
                m_old = m_ref[hh]
                m_new = jnp.maximum(m_old, jnp.max(s, axis=-1, keepdims=True))
                p = jnp.exp(s - m_new)
                alpha = jnp.exp(m_old - m_new)
                v1 = jnp.concatenate([v_h[...], jnp.ones((t, HD), BF16)], axis=1)
                acc_ref[hh] = alpha * acc_ref[hh] + _dot(p.astype(BF16), v1)
                m_ref[hh] = m_new

        @pl.when(ki < qi)
        def _():
            step(False)

        @pl.when(ki == qi)
        def _():
            step(True)
            for hh in range(HPB):
                o_ref[:, hh * HD:(hh + 1) * HD] = (acc_ref[hh, :, :HD] / acc_ref[hh, :, HD:]).astype(BF16)
                lse_ref[hh] = m_ref[hh] + jnp.log(acc_ref[hh, :, HD:HD + 1])

    return _call(body, ins, outs, (B, HEADS // HPB, nb, nb), name="mla_flash_fwd", scratch=scratch,
                 semantics=("parallel", "parallel", "parallel", "arbitrary"))


def _flash_bwd_dq(qn, qp, kn, kp, v, o, do, lse, *, B, S, t):
    nb, T = S // t, B * S
    qmap = lambda b, h, qi, ki: (b * nb + qi, h)
    kmap = lambda b, h, qi, ki: (b * nb + jnp.minimum(ki, qi), h)
    kpmap = lambda b, h, qi, ki: (b * nb + jnp.minimum(ki, qi), 0)
    ins = [(qn, (t, HW), qmap), (qp, (t, HW), qmap), (kn, (t, HW), kmap), (kp, (t, HD), kpmap), (v, (t, HW), kmap),
           (o, (t, HW), qmap), (do, (t, HW), qmap), (lse, (HPB, t, 1), lambda b, h, qi, ki: (h, b * nb + qi, 0))]
    outs = [((T, 1024), BF16, (t, HW), qmap), ((T, 1024), F32, (t, HW), qmap),
            ((HEADS, T, 1), F32, (HPB, t, 1), lambda b, h, qi, ki: (h, b * nb + qi, 0))]
    scratch = [pltpu.VMEM((HPB, t, 1), F32), pltpu.VMEM((HPB, t, 2 * HD), F32)]

    def body(in_refs, out_refs, scr):
        qn_ref, qp_ref, kn_ref, kp_ref, v_ref, o_ref, do_ref, lse_ref = in_refs
        dqn_ref, dqp_ref, dlo_ref = out_refs
        dl_ref, acc_ref = scr
        qi, ki = pl.program_id(2), pl.program_id(3)

        @pl.when(ki == 0)
        def _():
            for hh in range(HPB):
                o_h, do_h = _head_refs((o_ref, do_ref), hh)
                dl_ref[hh] = jnp.sum(do_h[...].astype(F32) * o_h[...].astype(F32), axis=-1, keepdims=True)
            acc_ref[...] = jnp.zeros_like(acc_ref)

        def step(diag):
            for hh in range(HPB):
                qn_h, qp_h, kn_h, v_h, do_h = _head_refs((qn_ref, qp_ref, kn_ref, v_ref, do_ref), hh)
                s, _, k = _scores(qn_h, qp_h, kn_h, kp_ref, diag)
                p = jnp.exp(s - lse_ref[hh])
                dp = _dot(do_h[...], v_h[...], NT)
                ds = p * (dp - dl_ref[hh]) * SM_SCALE
                acc_ref[hh] += _dot(ds.astype(BF16), k)

        @pl.when(ki < qi)
        def _():
            step(False)

        @pl.when(ki == qi)
        def _():
            step(True)
            for hh in range(HPB):
                dqn_ref[:, hh * HD:(hh + 1) * HD] = acc_ref[hh, :, :HD].astype(BF16)
                dqp_ref[:, hh * HD:(hh + 1) * HD] = acc_ref[hh, :, HD:]
            dlo_ref[...] = dl_ref[...]

    return _call(body, ins, outs, (B, HEADS // HPB, nb, nb), name="mla_flash_bwd_dq", scratch=scratch,
                 semantics=("parallel", "parallel", "parallel", "arbitrary"))


def _flash_bwd_dkv(qn, qp, kn, kp, v, do, lse_t, dl_t, *, B, S, t):
    nb, T = S // t, B * S
    qmap = lambda b, h, ki, qi: (b * nb + jnp.maximum(qi, ki), h)
    kmap = lambda b, h, ki, qi: (b * nb + ki, h)
    tmap = lambda b, h, ki, qi: (h, 0, b * nb + jnp.maximum(qi, ki))
    ins = [(qn, (t, HW), qmap), (qp, (t, HW), qmap), (kn, (t, HW), kmap),
           (kp, (t, HD), lambda b, h, ki, qi: (b * nb + ki, 0)), (v, (t, HW), kmap), (do, (t, HW), qmap),
           (lse_t, (HPB, 1, t), tmap), (dl_t, (HPB, 1, t), tmap)]
    outs = [((T, 1024), BF16, (t, HW), kmap), ((HEADS, T, HD), F32, (HPB, t, HD), lambda b, h, ki, qi: (h, b * nb + ki, 0)),
            ((T, 1024), BF16, (t, HW), kmap)]
    scratch = [pltpu.VMEM((HPB, t, 2 * HD), F32), pltpu.VMEM((HPB, t, HD), F32)]

    def body(in_refs, out_refs, scr):
        qn_ref, qp_ref, kn_ref, kp_ref, v_ref, do_ref, lse_ref, dl_ref = in_refs
        dkn_ref, dkp_ref, dv_ref = out_refs
        dk_acc, dv_acc = scr
        ki, qi = pl.program_id(2), pl.program_id(3)

        @pl.when(qi == 0)
        def _():
            dk_acc[...] = jnp.zeros_like(dk_acc)
            dv_acc[...] = jnp.zeros_like(dv_acc)

        def step(diag):
            for hh in range(HPB):
                qn_h, qp_h, kn_h, v_h, do_h = _head_refs((qn_ref, qp_ref, kn_ref, v_ref, do_ref), hh)
                q = jnp.concatenate([qn_h[...], qp_h[...]], axis=1)
                k = jnp.concatenate([kn_h[...], kp_ref[...]], axis=1)
                st = _dot(k, q, NT) * SM_SCALE
                if diag:
                    ii = lax.broadcasted_iota(jnp.int32, (t, t), 0)
                    jj = lax.broadcasted_iota(jnp.int32, (t, t), 1)
                    st = jnp.where(ii <= jj, st, -jnp.inf)
                do_t = do_h[...]
                pt = jnp.exp(st - lse_ref[hh])
                dst = pt * (_dot(v_h[...], do_t, NT) - dl_ref[hh]) * SM_SCALE
                dv_acc[hh] += _dot(pt.astype(BF16), do_t)
                dk_acc[hh] += _dot(dst.astype(BF16), q)

        @pl.when(qi > ki)
        def _():
            step(False)

        @pl.when(qi == ki)
        def _():
            step(True)

        @pl.when(qi == nb - 1)
        def _():
            for hh in range(HPB):
                dkn_ref[:, hh * HD:(hh + 1) * HD] = dk_acc[hh, :, :HD].astype(BF16)
                dkp_ref[hh] = dk_acc[hh, :, HD:]
                dv_ref[:, hh * HD:(hh + 1) * HD] = dv_acc[hh].astype(BF16)

    return _call(body, ins, outs, (B, HEADS // HPB, nb, nb), name="mla_flash_bwd_dkv", scratch=scratch,
                 semantics=("parallel", "parallel", "parallel", "arbitrary"))


def _allgather_async(shards, *, name, collective_id):
    n_arr = len(shards)
    hbm = pltpu.MemorySpace.HBM
    x_refs = [jax.new_ref(a, memory_space=hbm) for a in shards]
    out_refs = [jax.empty_ref(jax.ShapeDtypeStruct((N_DEV * a.shape[0], a.shape[1]), a.dtype), memory_space=hbm)
                for a in shards]

    @pl.kernel(mesh=plsc.ScalarSubcoreMesh(axis_name="seq", num_cores=1), name=name,
               scratch_types=(pltpu.SemaphoreType.DMA((n_arr, 7)), pltpu.SemaphoreType.DMA((n_arr, 7)),
                              pltpu.SemaphoreType.DMA((n_arr,))),
               compiler_params=pltpu.CompilerParams(collective_id=collective_id))
    def launch(send_sems, recv_sems, local_sems):
        x, y, c = lax.axis_index("x"), lax.axis_index("y"), lax.axis_index("c")
        me, sibling = (x, y, c), (x, y, 1 - c)
        chips = [(1 - x, y), (x, 1 - y), (1 - x, 1 - y)]
        barrier = pltpu.get_barrier_semaphore()
        for p in [sibling] + [(*chip, c) for chip in chips]:
            pl.semaphore_signal(barrier, inc=1, device_id=p, device_id_type=pl.DeviceIdType.MESH)
        pl.semaphore_wait(barrier, 4)

        def rows(a, px, py, pc):
            m_per = shards[a].shape[0]
            return out_refs[a].at[pl.ds((4 * px + 2 * py + pc) * m_per, m_per), :]

        def copy(a, k, block, to, src=None):
            return pltpu.make_async_remote_copy(
                src_ref=rows(a, *block) if src is None else src, dst_ref=rows(a, *block),
                send_sem=send_sems.at[a, k], recv_sem=recv_sems.at[a, k], device_id=to,
                device_id_type=pl.DeviceIdType.MESH)

        mine = [pltpu.make_async_copy(x_refs[a], rows(a, *me), local_sems.at[a]) for a in range(n_arr)]
        for cp in mine:
            cp.start()
        first = []
        for a in range(n_arr):
            first.append(copy(a, 0, me, sibling, src=x_refs[a]))
            first += [copy(a, 1 + j, me, (*chip, c), src=x_refs[a]) for j, chip in enumerate(chips)]
        for cp in first:
            cp.start()
        passed = []
        for j, chip in enumerate(chips):
            for a in range(n_arr):
                copy(a, 1 + j, (*chip, c), me).wait_recv()
                cp = copy(a, 4 + j, (*chip, c), sibling)
                cp.start()
                passed.append(cp)
        for a in range(n_arr):
            copy(a, 0, sibling, me).wait_recv()
        for j, chip in enumerate(chips):
            for a in range(n_arr):
                copy(a, 4 + j, (*chip, 1 - c), me).wait_recv()
        for cp in first + passed:
            cp.wait_send()
        for cp in mine:
            cp.wait()

    launch()
    return [r[...] for r in out_refs]


def _alltoall_async(sends, *, name, collective_id):
    n_arr = len(sends)
    hbm = pltpu.MemorySpace.HBM
    s_refs = [jax.new_ref(a, memory_space=hbm) for a in sends]
    r_refs = [jax.empty_ref(jax.ShapeDtypeStruct(a.shape, a.dtype), memory_space=hbm) for a in sends]

    @pl.kernel(mesh=plsc.ScalarSubcoreMesh(axis_name="seq", num_cores=1), name=name,
               scratch_types=(pltpu.SemaphoreType.DMA((n_arr, 7)), pltpu.SemaphoreType.DMA((n_arr, 7)),
                              pltpu.SemaphoreType.DMA((n_arr,))),
               compiler_params=pltpu.CompilerParams(collective_id=collective_id))
    def launch(send_sems, recv_sems, local_sems):
        x, y, c = lax.axis_index("x"), lax.axis_index("y"), lax.axis_index("c")
        me = 4 * x + 2 * y + c
        peers = [(1 - x if k & 4 else x, 1 - y if k & 2 else y, 1 - c if k & 1 else c) for k in range(1, N_DEV)]
        barrier = pltpu.get_barrier_semaphore()
        for p in peers:
            pl.semaphore_signal(barrier, inc=1, device_id=p, device_id_type=pl.DeviceIdType.MESH)
        pl.semaphore_wait(barrier, N_DEV - 1)

        def rows(ref, a, idx):
            m_per = sends[a].shape[0] // N_DEV
            return ref.at[pl.ds(idx * m_per, m_per), :]

        local = [pltpu.make_async_copy(rows(s_refs[a], a, me), rows(r_refs[a], a, me), local_sems.at[a])
                 for a in range(n_arr)]
        for cp in local:
            cp.start()
        copies = []
        for k, (px, py, pc) in enumerate(peers):
            for a in range(n_arr):
                cp = pltpu.make_async_remote_copy(
                    src_ref=rows(s_refs[a], a, 4 * px + 2 * py + pc), dst_ref=rows(r_refs[a], a, me),
                    send_sem=send_sems.at[a, k], recv_sem=recv_sems.at[a, k],
                    device_id=(px, py, pc), device_id_type=pl.DeviceIdType.MESH)
                cp.start()
                copies.append(cp)
        for cp in copies:
            cp.wait()
        for cp in local:
            cp.wait()

    launch()
    return [r[...] for r in r_refs]


def _reduce_adam(parts, w, m, v, *, tr, name):
    R, C = w.shape
    nR = R // tr
    ins = [(parts, (tr, C), lambda i, s=s: (s * nR + i, 0)) for s in range(N_DEV)]
    ins += [(a, (tr, C), lambda i: (i, 0)) for a in (w, m, v)]
    outs = [((R, C), F32, (tr, C), lambda i: (i, 0)) for _ in range(4)]
    c1 = 1.0 - ADAM_B1 ** ADAM_STEP
    c2 = 1.0 - ADAM_B2 ** ADAM_STEP

    def body(in_refs, out_refs, _):
        g = in_refs[0][...].astype(F32)
        for s in range(1, N_DEV):
            g = g + in_refs[s][...].astype(F32)
        wv, mv, vv = in_refs[8][...], in_refs[9][...], in_refs[10][...]
        mn = ADAM_B1 * mv + (1.0 - ADAM_B1) * g
        vn = ADAM_B2 * vv + (1.0 - ADAM_B2) * (g * g)
        delta = -ADAM_LR * ((mn / c1) / (jnp.sqrt(vn / c2) + ADAM_EPS) + ADAM_WD * wv)
        out_refs[0][...] = g
        out_refs[1][...] = delta
        out_refs[2][...] = mn
        out_refs[3][...] = vn

    return _call(body, ins, outs, (nR,), name=name, semantics=("parallel",))


IN_C, UP_C, UQ_C, QKV_C = 858, 704, 192, 384
A_W, Q_W, V_W = 896, 256, 768
SLAB_TR = {"A": 256, "Q": 128, "C": 368, "V": 16}
SMALL = [("norm_mix_g", 1024), ("gdn_a_log", 8), ("gdn_dt_bias", 8), ("gdn_norm_g", 128), ("mla_q_norm_g", 384),
         ("mla_kv_norm_g", 256), ("norm_ffn_g", 1024), ("norm_final_g", 1024)]
SMALL_ROWS = 32
WEIGHT_ORDER = ["norm_mix_g", "w_in", "conv_qkv_w", "gdn_a_log", "gdn_dt_bias", "gdn_norm_g", "mla_q_norm_g", "w_uq",
                "mla_kv_norm_g", "w_ukv", "w_o_gdn", "w_o_mla", "w_out", "norm_ffn_g", "w_up", "conv_ffn_w", "w_down",
                "norm_final_g"]


def _padc(w, n):
    return jnp.pad(w, ((0, 0), (0, n - w.shape[1])))


def _padrc(w, r, n):
    return jnp.pad(w, ((0, r - w.shape[0]), (0, n - w.shape[1])))


def _slabs(p, dtype):
    A = jnp.concatenate([_padc(p["w_in"], A_W), _padc(p["w_up"], A_W)], axis=0).astype(dtype)
    Q = jnp.concatenate([_padc(p["w_uq"], Q_W), p["w_ukv"]], axis=0).astype(dtype)
    C = jnp.concatenate([p["w_o_gdn"], p["w_o_mla"], p["w_out"], p["w_down"]], axis=0).astype(dtype)
    V = jnp.concatenate([_padrc(p["conv_qkv_w"], 8, V_W), _padrc(p["conv_ffn_w"], 8, V_W)], axis=0).astype(F32)
    return {"A": A, "Q": Q, "C": C, "V": V}


def _unslab(sl):
    A, Q, C, V = sl["A"], sl["Q"], sl["C"], sl["V"]
    out = {"w_in": A[:1024, :IN_C], "w_up": A[1024:, :UP_C], "w_uq": Q[:384, :UQ_C], "w_ukv": Q[384:],
           "w_o_gdn": C[0:128], "w_o_mla": C[128:256], "w_out": C[256:384], "w_down": C[384:],
           "conv_qkv_w": V[0:GDN_CONV, :QKV_C], "conv_ffn_w": V[8:8 + FFN_CONV, :UP_C]}
    return {k: a[None] for k, a in out.items()}


def _take_cols(pieces, lo, hi):
    out, off = [], 0
    for arr, a, b in pieces:
        s, e = max(lo, off), min(hi, off + b - a)
        if s < e:
            out.append(arr[:, a + s - off:a + e - off])
        off += b - a
    return out[0] if len(out) == 1 else jnp.concatenate(out, axis=1)


LOSS_SLOT = sum(n for _, n in SMALL)


def _pack_small(d, loss=None):
    flat = jnp.concatenate([d[n].reshape(-1).astype(F32) for n, _ in SMALL]
                           + ([] if loss is None else [loss.reshape(1).astype(F32)]))
    return jnp.pad(flat, (0, SMALL_ROWS * LANES - flat.shape[0])).reshape(SMALL_ROWS, LANES)


def _unpack_small(buf, shapes):
    flat, out, off = buf.reshape(-1), {}, 0
    for name, n in SMALL:
        out[name] = flat[off:off + n].reshape(shapes[name])
        off += n
    return out


def _rot_cols(w):
    h = ROPE // 2
    return jnp.concatenate([-w[:, h:], w[:, :h]], axis=1)


def _unrot_cols(dw):
    h = ROPE // 2
    return jnp.concatenate([dw[:, h:], -dw[:, :h]], axis=1)


IN_SPLITS = [0, 3072, 4096, 4104, 4112, 4496, 4752, 4816, 5840, 6864]


def _layout_late(A_up, C):
    W = {"w_up": jnp.concatenate([A_up[j, :, :UP_C] for j in range(N_DEV)], axis=1),
         "w_o_gdn": C[:, 0:128].reshape(1024, D_MODEL), "w_o_mla": C[:, 128:256].reshape(1024, D_MODEL),
         "w_out": C[:, 256:384].reshape(1024, D_MODEL), "w_down": C[:, 384:].reshape(D_FF, D_MODEL)}
    return {k: v.astype(BF16) for k, v in W.items()}


def _layout_weights(g):
    A_in, Q, V = g["A_in"], g["Q"], g["V"]
    in_pieces = [(A_in[j], 0, IN_C) for j in range(N_DEV)]
    o = IN_SPLITS
    take = lambda lo, hi: _take_cols(in_pieces, lo, hi)
    kpe = take(o[6], o[7])
    W = {
        "in_qkv": take(o[0], o[1]),
        "in_ga": take(o[1], o[2]),
        "in_ab": jnp.concatenate([_padc(take(o[2], o[3]), LANES), _padc(take(o[3], o[4]), LANES)], axis=1),
        "in_small": jnp.concatenate([take(o[4], o[6]), _padc(kpe, LANES), _padc(_rot_cols(kpe), LANES)], axis=1),
        "in_gbr": take(o[7], o[9]),
        "uq_n": jnp.concatenate([Q[j, :384, :HD] for j in range(N_DEV)], axis=1),
        "ukv_k": jnp.concatenate([Q[j, 384:, :HD] for j in range(N_DEV)], axis=1),
        "ukv_v": jnp.concatenate([Q[j, 384:, HD:] for j in range(N_DEV)], axis=1),
    }
    pe = [Q[j, :384, HD:HD + ROPE] for j in range(N_DEV)]
    W["uq_p"] = jnp.concatenate([_padc(p, HD) for p in pe] + [_padc(_rot_cols(p), HD) for p in pe], axis=1)
    conv_qkv = jnp.concatenate([V[j, 0:GDN_CONV, :QKV_C] for j in range(N_DEV)], axis=1)
    conv_ffn = jnp.concatenate([V[j, 8:8 + FFN_CONV, :UP_C] for j in range(N_DEV)], axis=1)
    return {k: v.astype(BF16) for k, v in W.items()}, conv_qkv, conv_ffn


def _full_grads(dW):
    s = dW["in_small"]
    dkpe = s[:, 640:704] + _unrot_cols(s[:, 768:832])
    in_pieces = [(dW["in_qkv"], 0, 3072), (dW["in_ga"], 0, 1024), (dW["in_ab"], 0, 8), (dW["in_ab"], 128, 136),
                 (s, 0, 640), (dkpe, 0, ROPE), (dW["in_gbr"], 0, 2048)]
    pe = []
    for j in range(N_DEV):
        lin = dW["uq_p"][:, j * HD:j * HD + ROPE]
        rot = dW["uq_p"][:, 1024 + j * HD:1024 + j * HD + ROPE]
        pe.append(lin + _unrot_cols(rot))
    return in_pieces, pe


def _send_slabs(dW, d_conv_qkv, d_conv_ffn):
    in_pieces, pe = _full_grads(dW)
    A, Q, V = [], [], []
    for j in range(N_DEV):
        gin = _padc(_take_cols(in_pieces, j * IN_C, (j + 1) * IN_C), A_W)
        gup = _padc(dW["w_up"][:, j * UP_C:(j + 1) * UP_C], A_W)
        A.append(jnp.concatenate([gin, gup], axis=0))
        guq = _padc(jnp.concatenate([dW["uq_n"][:, j * HD:(j + 1) * HD], pe[j]], axis=1), Q_W)
        gukv = jnp.concatenate([dW["ukv_k"][:, j * HD:(j + 1) * HD], dW["ukv_v"][:, j * HD:(j + 1) * HD]], axis=1)
        Q.append(jnp.concatenate([guq, gukv], axis=0))
        V.append(jnp.concatenate([_padrc(d_conv_qkv[:, j * QKV_C:(j + 1) * QKV_C], 8, V_W),
                                  _padrc(d_conv_ffn[:, j * UP_C:(j + 1) * UP_C], 8, V_W)], axis=0))
    C = jnp.concatenate([dW["w_o_gdn"].reshape(N_DEV, 128, D_MODEL), dW["w_o_mla"].reshape(N_DEV, 128, D_MODEL),
                         dW["w_out"].reshape(N_DEV, 128, D_MODEL), dW["w_down"].reshape(N_DEV, 352, D_MODEL)], axis=1)
    return {"A": jnp.concatenate(A, axis=0).astype(BF16), "Q": jnp.concatenate(Q, axis=0).astype(BF16),
            "C": C.reshape(N_DEV * 736, D_MODEL).astype(BF16), "V": jnp.concatenate(V, axis=0)}


def _rope_tables(S):
    half = ROPE // 2
    inv = ROPE_THETA ** (-jnp.arange(half, dtype=F32) / half)
    ang = jnp.arange(S, dtype=F32)[:, None] * inv[None, :]
    cos = jnp.concatenate([jnp.cos(ang), jnp.cos(ang)], axis=1)
    sin = jnp.concatenate([jnp.sin(ang), jnp.sin(ang)], axis=1)
    return _padc(cos, HD), _padc(sin, HD)


def _local_step(x, tgt, W, late_weights, exchange, conv_qkv_w, conv_ffn_w, small, tm=None, ta=None):
    B, S, _ = x.shape
    T = B * S
    tm = tm or _pick(S, 512, CHUNK)
    ta = ta or _pick(S, 512, LANES)
    x2d, tgt2d = x.reshape(T, D_MODEL), tgt.reshape(T, D_MODEL)
    row = lambda v: v.reshape(1, -1).astype(F32)
    pad_row = lambda v: _padc(row(v), LANES)
    g_mix, g_ffn, g_fin = row(small["norm_mix_g"]), row(small["norm_ffn_g"]), row(small["norm_final_g"])
    g_gdn, g_q, g_kv = row(small["gdn_norm_g"]), row(small["mla_q_norm_g"]), row(small["mla_kv_norm_g"])
    alog, dtb = pad_row(small["gdn_a_log"]), pad_row(small["gdn_dt_bias"])
    cos, sin = _rope_tables(S)
    tps = S // tm
    tab = lambda a: (a, (tm, HD), lambda i: (i % tps, 0))
    col = lambda a, c, w: (a, (tm, w), lambda i, c=c: (i, c))

    h1 = _norm_fwd(x2d, g_mix, T=T, tm=tm, name="norm_mix_fwd")
    z_qkv = _mm(h1, W["in_qkv"], "nn", BF16, name="in_qkv_fwd")
    z_ga = _mm(h1, W["in_ga"], "nn", BF16, name="in_ga_fwd")
    z_ab = _mm(h1, W["in_ab"], "nn", F32, name="in_ab_fwd")
    z_small = _mm(h1, W["in_small"], "nn", F32, name="in_small_fwd", tn=896)
    z_gbr = _mm(h1, W["in_gbr"], "nn", BF16, name="in_gbr_fwd")

    qkvn = _conv_fwd(_qkv_fn, [(z_qkv, 0)], [(conv_qkv_w, 0)], 3072, BF16, T=T, S=S, tm=tm, cb=QKV_CB,
                     ncb=3072 // QKV_CB, name="gdn_qkv_fwd")
    gcum, beta = _row_call(lambda za, zb, al, db: _gate_fn(za, zb, al, db), [col(z_ab, 0, LANES), col(z_ab, 1, LANES)],
                           [alog, dtb], [(LANES, F32), (LANES, F32)], T=T, tm=tm, name="gdn_gate_fwd")
    grT = gcum[:, :HEADS].reshape(T // CHUNK, CHUNK, HEADS).transpose(0, 2, 1)[:, :, None, :]
    qkvn, late = late_weights(qkvn)
    W = {**W, **late}
    o_gdn, states = _gdn_fwd(qkvn, gcum, grT, beta, B=B, S=S)

    def gdn_out_fn(o, ga, g):
        parts = []
        for h in range(HEADS):
            sl = slice(h * HD, (h + 1) * HD)
            parts.append(_rms(o[:, sl], g) * jax.nn.silu(ga[:, sl].astype(F32)))
        return jnp.concatenate(parts, axis=1)

    oa = _row_call(lambda o, ga, g: (gdn_out_fn(o, ga, g),), [o_gdn, z_ga], [g_gdn], [(1024, BF16)], T=T, tm=tm,
                   name="gdn_out_fwd")[0]

    def mla_prep_fn(zq, zkv, zpl, zpr, c, s, gq, gkv):
        return _rms(zq, gq), _rms(zkv, gkv), zpl * c + zpr * s

    small_cols = [(z_small, (tm, Q_RANK), lambda i: (i, 0)), (z_small, (tm, LANES), lambda i: (i, 3)),
                  (z_small, (tm, LANES), lambda i: (i, 4)), (z_small, (tm, LANES), lambda i: (i, 5)),
                  (z_small, (tm, LANES), lambda i: (i, 6))]

    def mla_prep_fwd(zq, zkv0, zkv1, zpl, zpr, c, s, gq, gkv):
        return mla_prep_fn(zq, jnp.concatenate([zkv0, zkv1], axis=1), zpl, zpr, c, s, gq, gkv)

    cq, ckv, kpe = _row_call(mla_prep_fwd, small_cols + [tab(cos), tab(sin)], [g_q, g_kv],
                             [(Q_RANK, BF16), (KV_RANK, BF16), (HD, BF16)], T=T, tm=tm, name="mla_prep_fwd")
    qn = _mm(cq, W["uq_n"], "nn", BF16, name="uq_n_fwd")
    qpl = _mm(cq, W["uq_p"], "nn", F32, name="uq_p_fwd")
    kn = _mm(ckv, W["ukv_k"], "nn", BF16, name="ukv_k_fwd")
    vb = _mm(ckv, W["ukv_v"], "nn", BF16, name="ukv_v_fwd")

    def qrope_fn(lin, rot, c, s):
        return lin * jnp.tile(c, (1, HEADS)) + rot * jnp.tile(s, (1, HEADS))

    qp = _row_call(lambda lin, rot, c, s: (qrope_fn(lin, rot, c, s),), [col(qpl, 0, 1024), col(qpl, 1, 1024), tab(cos), tab(sin)],
                   [], [(1024, BF16)], T=T, tm=tm, name="q_rope_fwd")[0]
    ob, lse = _flash_fwd(qn, qp, kn, kpe, vb, B=B, S=S, t=ta)

    def merge_fn(ya, yb, ga, gb):
        return jax.nn.sigmoid(ga.astype(F32)) * ya + jax.nn.sigmoid(gb.astype(F32)) * yb

    def merge_fwd(oat, obt, ga, gb, wog, wom):
        ya, yb = _dot(oat, wog), _dot(obt, wom)
        return ya, yb, merge_fn(ya, yb, ga, gb)

    ya, yb, merged = _row_call(merge_fwd, [oa, ob, col(z_gbr, 0, 1024), col(z_gbr, 1, 1024)], [W["w_o_gdn"], W["w_o_mla"]],
                               [(1024, BF16), (1024, BF16), (1024, BF16)], T=T, tm=tm, name="merge_fwd")
    x1 = _mm(merged, W["w_out"], "nn", F32, add=x2d, name="w_out_fwd")

    h2 = _norm_fwd(x1, g_ffn, T=T, tm=tm, name="norm_ffn_fwd")
    up = _mm(h2, W["w_up"], "nn", BF16, name="w_up_fwd")
    FCB = 256
    nfb = D_FF // FCB
    f = _conv_fwd(_ffn_fn, [(up, 0), (up, 2)], [(conv_ffn_w, 0), (conv_ffn_w, 2)], D_FF, BF16, T=T, S=S, tm=tm,
                  cb=D_FF // 2, ncb=2, name="ffn_act_fwd")
    x2 = _mm(f, W["w_down"], "nn", F32, add=x1, name="w_down_fwd", tk=1408)

    def final_fn(xt, tt, g):
        def lossf(xv, gv):
            e = _rms(xv, gv) - tt
            return 0.5 * jnp.sum(jnp.mean(e * e, axis=-1))

        l, vjp = jax.vjp(lossf, xt, g)
        dx, dg = vjp(jnp.ones((), F32))
        return dx, jnp.full((1, LANES), l, F32), dg

    dx2, loss_v, dg_fin = _row_call(final_fn, [x2, tgt2d], [g_fin], [(1024, F32)], [((1, LANES), F32), ((1, 1024), F32)],
                                    T=T, tm=tm, name="loss_head")

    dW = {}
    df = _mm(dx2, W["w_down"], "nt", BF16, name="w_down_dx")
    dW["w_down"] = _mm(f, dx2, "tn", F32, name="w_down_dw")
    dug, duu, dcw_g, dcw_u = _conv_bwd(_ffn_fn, [(up, 0), (up, nfb)], [(conv_ffn_w, 0), (conv_ffn_w, nfb)], df, BF16,
                                       T=T, S=S, tm=tm, cb=FCB, ncb=nfb, name="ffn_act_bwd")
    d_conv_ffn = jnp.concatenate([dcw_g, dcw_u], axis=1)
    wup_g, wup_u = W["w_up"][:, :D_FF], W["w_up"][:, D_FF:]
    dh2 = _mm(dug, wup_g, "nt", F32, name="w_up_dx_g")
    dh2 = _mm(duu, wup_u, "nt", F32, add=dh2, name="w_up_dx_u")
    dW["w_up"] = jnp.concatenate([_mm(h2, dug, "tn", F32, name="w_up_dw_g"), _mm(h2, duu, "tn", F32, name="w_up_dw_u")], axis=1)
    dx1, dg_ffn = _norm_bwd(x1, g_ffn, dh2, dx2, T=T, tm=tm, name="norm_ffn_bwd")

    dmerged = _mm(dx1, W["w_out"], "nt", F32, name="w_out_dx")
    dW["w_out"] = _mm(merged, dx1, "tn", F32, name="w_out_dw")

    def merge_bwd(dm, yat, ybt, ga, gb):
        _, vjp = jax.vjp(merge_fn, yat.astype(F32), ybt.astype(F32), ga, gb)
        return vjp(dm)

    dya, dyb, dgbr_a, dgbr_b = _row_call(merge_bwd, [dmerged, ya, yb, col(z_gbr, 0, 1024), col(z_gbr, 1, 1024)], [],
                                         [(1024, BF16)] * 4, T=T, tm=tm, name="merge_bwd")
    doa = _mm(dya, W["w_o_gdn"], "nt", F32, name="w_o_gdn_dx")
    dob = _mm(dyb, W["w_o_mla"], "nt", BF16, name="w_o_mla_dx")
    dW["w_o_gdn"] = _mm(oa, dya, "tn", F32, name="w_o_gdn_dw")
    dW["w_o_mla"] = _mm(ob, dyb, "tn", F32, name="w_o_mla_dw")

    dqn, dqp, dl = _flash_bwd_dq(qn, qp, kn, kpe, vb, ob, dob, lse, B=B, S=S, t=ta)
    dkn, dkp, dvb = _flash_bwd_dkv(qn, qp, kn, kpe, vb, dob, lse.reshape(HEADS, 1, T), dl.reshape(HEADS, 1, T),
                                   B=B, S=S, t=ta)

    def qrope_bwd(d, c, s):
        return d * jnp.tile(c, (1, HEADS)), d * jnp.tile(s, (1, HEADS))

    dq_lin, dq_rot = _row_call(qrope_bwd, [dqp, tab(cos), tab(sin)], [], [(1024, BF16), (1024, BF16)], T=T, tm=tm,
                               name="q_rope_bwd")
    wp_lin, wp_rot = W["uq_p"][:, :1024], W["uq_p"][:, 1024:]
    dcq = _mm(dqn, W["uq_n"], "nt", F32, name="uq_n_dx")
    dcq = _mm(dq_lin, wp_lin, "nt", F32, add=dcq, name="uq_pl_dx")
    dcq = _mm(dq_rot, wp_rot, "nt", F32, add=dcq, name="uq_pr_dx")
    dW["uq_n"] = _mm(cq, dqn, "tn", F32, name="uq_n_dw")
    dW["uq_p"] = jnp.concatenate([_mm(cq, dq_lin, "tn", F32, name="uq_pl_dw"), _mm(cq, dq_rot, "tn", F32, name="uq_pr_dw")], axis=1)
    dckv = _mm(dkn, W["ukv_k"], "nt", F32, name="ukv_k_dx")
    dckv = _mm(dvb, W["ukv_v"], "nt", F32, add=dckv, name="ukv_v_dx")
    dW["ukv_k"] = _mm(ckv, dkn, "tn", F32, name="ukv_k_dw")
    dW["ukv_v"] = _mm(ckv, dvb, "tn", F32, name="ukv_v_dw")

    def mla_prep_bwd(zq, zkv0, zkv1, zpl, zpr, c, s, dcqt, dckvt, dkpt, gq, gkv):
        zkv = jnp.concatenate([zkv0, zkv1], axis=1)
        _, vjp = jax.vjp(lambda a, b, p, r, g1, g2: mla_prep_fn(a, b, p, r, c, s, g1, g2), zq, zkv, zpl, zpr, gq, gkv)
        dk = dkpt[0]
        for h in range(1, HEADS):
            dk = dk + dkpt[h]
        dzq, dzkv, dzpl, dzpr, dgq, dgkv = vjp((dcqt, dckvt, dk))
        return jnp.concatenate([dzq, dzkv, dzpl, dzpr], axis=1), dgq, dgkv

    dz_small, dg_q, dg_kv = _row_call(
        mla_prep_bwd, small_cols + [tab(cos), tab(sin), dcq, dckv, (dkp, (HEADS, tm, HD), lambda i: (0, i, 0))],
        [g_q, g_kv], [(896, BF16)], [((1, Q_RANK), F32), ((1, KV_RANK), F32)], T=T, tm=tm, name="mla_prep_bwd")

    def gdn_out_bwd(o, ga, dot_, g):
        _, vjp = jax.vjp(gdn_out_fn, o, ga, g)
        return vjp(dot_)

    do_gdn, dz_ga, dg_gdn = _row_call(gdn_out_bwd, [o_gdn, z_ga, doa], [g_gdn], [(1024, F32), (1024, BF16)],
                                      [((1, HD), F32)], T=T, tm=tm, name="gdn_out_bwd")
    dqkvn, dgc, dgrT, dbeta = _gdn_bwd(qkvn, gcum, grT, beta, states, do_gdn, B=B, S=S)
    dgc_tot = dgc + _padc(dgrT[:, :, 0, :].transpose(0, 2, 1).reshape(T, HEADS), LANES)

    def gate_bwd(za, zb, dg, db, al, db_):
        _, vjp = jax.vjp(_gate_fn, za, zb, al, db_)
        return vjp((dg, db))

    dz_a, dz_b, d_alog, d_dtb = _row_call(gate_bwd, [col(z_ab, 0, LANES), col(z_ab, 1, LANES), dgc_tot, dbeta], [alog, dtb],
                                          [(LANES, BF16), (LANES, BF16)], [((1, LANES), F32), ((1, LANES), F32)],
                                          T=T, tm=tm, name="gdn_gate_bwd")
    dz_qkv, d_conv_qkv = _conv_bwd(_qkv_fn, [(z_qkv, 0)], [(conv_qkv_w, 0)], dqkvn, BF16, T=T, S=S, tm=tm, cb=QKV_CB,
                                   ncb=3072 // QKV_CB, name="gdn_qkv_bwd")

    dz_ab = jnp.concatenate([dz_a, dz_b], axis=1)
    dz_gbr = jnp.concatenate([dgbr_a, dgbr_b], axis=1)
    keys = ("in_qkv", "in_ga", "in_ab", "in_small", "in_gbr")
    dzs = (dz_qkv, dz_ga, dz_ab, dz_small, dz_gbr)
    for key, dz in zip(keys, dzs):
        dW[key] = _mm(h1, dz, "tn", F32, name=key + "_dw", tn=896 if key == "in_small" else 1024)
    dzs, exchanged = exchange(dW, d_conv_qkv, d_conv_ffn, dzs)
    dh1 = None
    for key, dz in zip(keys, dzs):
        dh1 = _mm(dz, W[key], "nt", F32, add=dh1, name=key + "_dx", tk=896 if key == "in_small" else 1024)
    dx, dg_mix = _norm_bwd(x2d, g_mix, dh1, dx1, T=T, tm=tm, name="norm_mix_bwd")

    dsmall = {"norm_mix_g": dg_mix, "gdn_a_log": d_alog[:, :HEADS], "gdn_dt_bias": d_dtb[:, :HEADS], "gdn_norm_g": dg_gdn,
              "mla_q_norm_g": dg_q, "mla_kv_norm_g": dg_kv, "norm_ffn_g": dg_ffn, "norm_final_g": dg_fin}
    return loss_v[0, 0], dx.reshape(B, S, D_MODEL), exchanged, dsmall


def kernel(x, norm_mix_g, w_in, conv_qkv_w, gdn_a_log, gdn_dt_bias, gdn_norm_g, mla_q_norm_g, w_uq, mla_kv_norm_g, w_ukv, w_o_gdn, w_o_mla, w_out, norm_ffn_g, w_up, conv_ffn_w, w_down, norm_final_g, loss_target, m_norm_mix_g, m_w_in, m_conv_qkv_w, m_gdn_a_log, m_gdn_dt_bias, m_gdn_norm_g, m_mla_q_norm_g, m_w_uq, m_mla_kv_norm_g, m_w_ukv, m_w_o_gdn, m_w_o_mla, m_w_out, m_norm_ffn_g, m_w_up, m_conv_ffn_w, m_w_down, m_norm_final_g, v_norm_mix_g, v_w_in, v_conv_qkv_w, v_gdn_a_log, v_gdn_dt_bias, v_gdn_norm_g, v_mla_q_norm_g, v_w_uq, v_mla_kv_norm_g, v_w_ukv, v_w_o_gdn, v_w_o_mla, v_w_out, v_norm_ffn_g, v_w_up, v_conv_ffn_w, v_w_down, v_norm_final_g):
    w = dict(norm_mix_g=norm_mix_g, w_in=w_in, conv_qkv_w=conv_qkv_w, gdn_a_log=gdn_a_log, gdn_dt_bias=gdn_dt_bias,
             gdn_norm_g=gdn_norm_g, mla_q_norm_g=mla_q_norm_g, w_uq=w_uq, mla_kv_norm_g=mla_kv_norm_g, w_ukv=w_ukv,
             w_o_gdn=w_o_gdn, w_o_mla=w_o_mla, w_out=w_out, norm_ffn_g=norm_ffn_g, w_up=w_up, conv_ffn_w=conv_ffn_w,
             w_down=w_down, norm_final_g=norm_final_g)
    m = dict(norm_mix_g=m_norm_mix_g, w_in=m_w_in, conv_qkv_w=m_conv_qkv_w, gdn_a_log=m_gdn_a_log, gdn_dt_bias=m_gdn_dt_bias,
             gdn_norm_g=m_gdn_norm_g, mla_q_norm_g=m_mla_q_norm_g, w_uq=m_w_uq, mla_kv_norm_g=m_mla_kv_norm_g, w_ukv=m_w_ukv,
             w_o_gdn=m_w_o_gdn, w_o_mla=m_w_o_mla, w_out=m_w_out, norm_ffn_g=m_norm_ffn_g, w_up=m_w_up,
             conv_ffn_w=m_conv_ffn_w, w_down=m_w_down, norm_final_g=m_norm_final_g)
    v = dict(norm_mix_g=v_norm_mix_g, w_in=v_w_in, conv_qkv_w=v_conv_qkv_w, gdn_a_log=v_gdn_a_log, gdn_dt_bias=v_gdn_dt_bias,
             gdn_norm_g=v_gdn_norm_g, mla_q_norm_g=v_mla_q_norm_g, w_uq=v_w_uq, mla_kv_norm_g=v_mla_kv_norm_g, w_ukv=v_w_ukv,
             w_o_gdn=v_w_o_gdn, w_o_mla=v_w_o_mla, w_out=v_w_out, norm_ffn_g=v_norm_ffn_g, w_up=v_w_up,
             conv_ffn_w=v_conv_ffn_w, w_down=v_w_down, norm_final_g=v_norm_final_g)
    slab_names = ("A", "Q", "C", "V")
    big_names = ("w_in", "w_up", "w_uq", "w_ukv", "w_o_gdn", "w_o_mla", "w_out", "w_down", "conv_qkv_w", "conv_ffn_w")
    small_names = [n for n, _ in SMALL]
    small_shapes = {n: w[n].shape for n in small_names}
    local2d = lambda d: {n: d[n][0] for n in big_names}

    w_slabs = _slabs(local2d(w), F32)
    a_bf = w_slabs["A"].astype(BF16)
    first = _allgather_async([a_bf[:1024]], name="allgather_w_in", collective_id=1)
    second = _allgather_async([w_slabs["Q"].astype(BF16), w_slabs["V"]], name="allgather_mixers", collective_id=2)
    third = _allgather_async([a_bf[1024:], w_slabs["C"].astype(BF16)], name="allgather_ffn_out", collective_id=3)
    gathered = {k: g.reshape(N_DEV, -1, g.shape[1])
                for k, g in zip(("A_in", "Q", "V", "A_up", "C"), first + second + third)}
    W, conv_qkv_full, conv_ffn_full = _layout_weights(gathered)

    def late_weights(tie):
        tie, a_up, c_all = lax.optimization_barrier((tie, gathered["A_up"], gathered["C"]))
        return tie, _layout_late(a_up, c_all)

    def exchange(dW, d_conv_qkv, d_conv_ffn, ties):
        g_send = _send_slabs(dW, d_conv_qkv, d_conv_ffn)
        sends, ties = lax.optimization_barrier(([g_send[k] for k in slab_names], ties))
        return ties, _alltoall_async(sends, name="alltoall_grads", collective_id=0)

    loss_local, dx, recv, dsmall = _local_step(
        x, loss_target, W, late_weights, exchange, conv_qkv_full, conv_ffn_full, {n: w[n] for n in small_names})

    dx, recv = lax.optimization_barrier((dx, recv))
    small_parts = _alltoall_async([jnp.tile(_pack_small(dsmall, loss_local), (N_DEV, 1))], name="alltoall_small_grads",
                                  collective_id=4)[0]
    m_slabs, v_slabs = _slabs(local2d(m), F32), _slabs(local2d(v), F32)
    upd = {k: _reduce_adam(r, w_slabs[k], m_slabs[k], v_slabs[k], tr=SLAB_TR[k], name="adam_" + k)
           for k, r in zip(slab_names, recv)}
    upd_small = _reduce_adam(small_parts, _pack_small({n: w[n] for n in small_names}), _pack_small({n: m[n] for n in small_names}),
                             _pack_small({n: v[n] for n in small_names}), tr=SMALL_ROWS, name="adam_small")

    loss = upd_small[0].reshape(-1)[LOSS_SLOT]
    groups = []
    for i in range(4):
        merged = {**_unslab({k: upd[k][i] for k in slab_names}), **_unpack_small(upd_small[i], small_shapes)}
        groups.append([merged[n] for n in WEIGHT_ORDER])
    return (loss, dx, *groups[0], *groups[1], *groups[2], *groups[3])
```

```python
import functools
import math

import numpy as np
import jax
import jax.numpy as jnp
from jax import lax
from jax.experimental import pallas as pl
from jax.experimental.pallas import tpu as pltpu
from jax.experimental.pallas import tpu_sc as plsc

F32 = jnp.float32
BF16 = jnp.bfloat16

D_MODEL = 1024
HEADS = 8
HD = 128
GDN_CONV = 4
CHUNK = 64
Q_RANK = 384
KV_RANK = 256
ROPE = 64
ROPE_THETA = 10000.0
D_FF = 2816
FFN_CONV = 3
EPS = 1e-6
SM_SCALE = (HD + ROPE) ** -0.5
N_DEV = 8

ADAM_LR, ADAM_B1, ADAM_B2, ADAM_EPS, ADAM_WD, ADAM_STEP = 0.001, 0.9, 0.999, 1e-08, 0.01, 10

LANES = 128
SUBLANES = 8
HALO = 2 * SUBLANES
VMEM_LIMIT = 56 * 1024 * 1024
HI = lax.Precision.HIGHEST
TRI_PRECISION = None

NN = (((1,), (0,)), ((), ()))
NT = (((1,), (1,)), ((), ()))
TN = (((0,), (0,)), ((), ()))


def _dot(a, b, dims=NN, precision=None):
    return lax.dot_general(a, b, dims, precision=precision, preferred_element_type=F32)


def _pick(dim, target, align):
    best = None
    for t in range(align, min(dim, target) + 1, align):
        if dim % t == 0:
            best = t
    return dim if best is None else best


def _call(body, ins, outs, grid, *, name, scratch=(), semantics=None):
    n_in, n_out = len(ins), len(outs)

    def kern(*refs):
        body(refs[:n_in], refs[n_in:n_in + n_out], refs[n_in + n_out:])

    res = pl.pallas_call(
        kern,
        grid=grid,
        in_specs=[pl.BlockSpec(bs, im) for _, bs, im in ins],
        out_specs=[pl.BlockSpec(bs, im) for _, _, bs, im in outs],
        out_shape=[jax.ShapeDtypeStruct(s, d) for s, d, _, _ in outs],
        scratch_shapes=list(scratch),
        name=name,
        compiler_params=pltpu.CompilerParams(
            dimension_semantics=semantics or ("arbitrary",) * len(grid), vmem_limit_bytes=VMEM_LIMIT),
    )(*[a for a, _, _ in ins])
    return res


def _mm(a, b, mode, out_dtype, *, name, add=None, tm=1408, tn=1408, tk=1408):
    if mode == "nn":
        (M, K), (K2, N) = a.shape, b.shape
    elif mode == "nt":
        (M, K), (N, K2) = a.shape, b.shape
    else:
        (K, M), (K2, N) = a.shape, b.shape
    assert K == K2, (a.shape, b.shape, mode)
    tm = _pick(M, tm, LANES if mode == "tn" else 16)
    tn = _pick(N, tn, LANES)
    tk = _pick(K, tk, 16 if mode == "tn" else LANES)
    nk = K // tk
    dims = {"nn": NN, "nt": NT, "tn": TN}[mode]
    if mode == "nn":
        a_spec, b_spec = ((tm, tk), lambda i, j, k: (i, k)), ((tk, tn), lambda i, j, k: (k, j))
    elif mode == "nt":
        a_spec, b_spec = ((tm, tk), lambda i, j, k: (i, k)), ((tn, tk), lambda i, j, k: (j, k))
    else:
        a_spec, b_spec = ((tk, tm), lambda i, j, k: (k, i)), ((tk, tn), lambda i, j, k: (k, j))
    ins = [(a,) + a_spec, (b,) + b_spec]
    if add is not None:
        ins.append((add, (tm, tn), lambda i, j, k: (i, j)))
    outs = [((M, N), out_dtype, (tm, tn), lambda i, j, k: (i, j))]

    def body(in_refs, out_refs, scr):
        prod = _dot(in_refs[0][...].astype(BF16), in_refs[1][...].astype(BF16), dims)

        def finish(r):
            if add is not None:
                r = r + in_refs[2][...].astype(F32)
            out_refs[0][...] = r.astype(out_dtype)

        if nk == 1:
            finish(prod)
            return
        k = pl.program_id(2)
        acc = scr[0]

        @pl.when(k == 0)
        def _():
            acc[...] = prod

        @pl.when(k > 0)
        def _():
            acc[...] += prod

        @pl.when(k == nk - 1)
        def _():
            finish(acc[...])

    return _call(body, ins, outs, (M // tm, N // tn, nk), name=name,
                 scratch=[pltpu.VMEM((tm, tn), F32)] if nk > 1 else [],
                 semantics=("parallel", "parallel", "arbitrary"))[0]


def _row_call(fn, rows, consts, out_rows, out_accs=(), *, T, tm, name):
    nt = T // tm
    ins = []
    for r in rows:
        ins.append(r if isinstance(r, tuple) else (r, (tm, r.shape[1]), lambda i: (i, 0)))
    for c in consts:
        ins.append((c, c.shape, lambda i, nd=c.ndim: (0,) * nd))
    outs = []
    for o in out_rows:
        outs.append(((T, o[0]), o[1], (tm, o[0]), lambda i: (i, 0)) if len(o) == 2 else o)
    for shp, dt in out_accs:
        outs.append((shp, dt, shp, lambda i, nd=len(shp): (0,) * nd))
    n_r = len(out_rows)

    def body(in_refs, out_refs, _):
        i = pl.program_id(0)
        vals = fn(*[r[...] for r in in_refs])
        for o_ref, v in zip(out_refs[:n_r], vals[:n_r]):
            o_ref[...] = v.astype(o_ref.dtype)
        for o_ref, v in zip(out_refs[n_r:], vals[n_r:]):
            @pl.when(i == 0)
            def _(o_ref=o_ref):
                o_ref[...] = jnp.zeros_like(o_ref)

            o_ref[...] += v.astype(o_ref.dtype)

    return _call(body, ins, outs, (nt,), name=name)


def _rms(x, g):
    return x * lax.rsqrt(jnp.mean(x * x, axis=-1, keepdims=True) + EPS) * g


def _norm_fwd(x, g, *, T, tm, name):
    return _row_call(lambda xt, gt: (_rms(xt, gt),), [x], [g], [(x.shape[1], BF16)], T=T, tm=tm, name=name)[0]


def _norm_bwd(x, g, dh, dres, *, T, tm, name):
    def fn(xt, dht, drt, gt):
        _, vjp = jax.vjp(_rms, xt, gt)
        dx, dg = vjp(dht)
        return drt + dx, dg

    return _row_call(fn, [x, dh, dres], [g], [(x.shape[1], F32)], [(g.shape, F32)], T=T, tm=tm, name=name)


def _rows16(c):
    return lax.broadcasted_iota(jnp.int32, (HALO, c), 0)


@functools.lru_cache(maxsize=None)
def _shift_fn(j):
    @jax.custom_vjp
    def shift(x, halo):
        xr = pltpu.roll(x, j, 0)
        top = jnp.where(_rows16(x.shape[1]) < j, pltpu.roll(halo, j, 0), xr[:HALO])
        return jnp.concatenate([top, xr[HALO:]], axis=0)

    def fwd(x, halo):
        return shift(x, halo), None

    def bwd(_, dy):
        tm, c = dy.shape
        keep = _rows16(c) >= HALO - j
        dxr = pltpu.roll(dy, tm - j, 0)
        dx = jnp.concatenate([dxr[:tm - HALO], jnp.where(keep, 0.0, dxr[tm - HALO:])], axis=0)
        dhalo = jnp.where(keep, pltpu.roll(dy[:HALO], HALO - j, 0), 0.0)
        return dx, dhalo

    shift.defvjp(fwd, bwd)
    return shift


def _dwconv(tail, x, w):
    K = w.shape[0]
    acc = w[K - 1:K, :] * x
    for k in range(K - 1):
        acc = acc + w[k:k + 1, :] * _shift_fn(K - 1 - k)(x, tail)
    return acc


STRIP = 64


def _conv_fwd(fn, xs, ws, out_c, out_dtype, *, T, S, tm, cb, ncb, name):
    nt, tps, hb = T // tm, S // tm, tm // HALO
    ins = []
    for arr, off in xs:
        ins.append((arr, (tm, cb), lambda j, i, off=off: (i, off + j)))
        ins.append((arr, (HALO, cb), lambda j, i, off=off: (jnp.maximum(i * hb - 1, 0), off + j)))
    for arr, off in ws:
        ins.append((arr, (arr.shape[0], cb), lambda j, i, off=off: (0, off + j)))
    outs = [((T, out_c), out_dtype, (tm, cb), lambda j, i: (i, j))]
    nx = len(xs)

    def body(in_refs, out_refs, _):
        j, i = pl.program_id(0), pl.program_id(1)
        first = (i % tps) == 0
        wts = [r[...] for r in in_refs[2 * nx:]]
        for r in range(0, tm, STRIP):
            xts = [in_refs[2 * m][r:r + STRIP, :].astype(F32) for m in range(nx)]
            if r == 0:
                tails = [jnp.where(first, 0.0, in_refs[2 * m + 1][...].astype(F32)) for m in range(nx)]
            else:
                tails = [in_refs[2 * m][r - HALO:r, :].astype(F32) for m in range(nx)]
            out_refs[0][r:r + STRIP, :] = fn(j, tails, xts, wts).astype(out_dtype)

    return _call(body, ins, outs, (ncb, nt), name=name)[0]


def _conv_bwd(fn, xs, ws, dout, dx_dtype, *, T, S, tm, cb, ncb, name):
    nt, tps, hb = T // tm, S // tm, tm // HALO
    ins = []
    for arr, off in xs:
        ins.append((arr, (tm, cb), lambda j, i, off=off: (nt - 1 - i, off + j)))
        ins.append((arr, (HALO, cb), lambda j, i, off=off: (jnp.maximum((nt - 1 - i) * hb - 1, 0), off + j)))
    for arr, off in ws:
        ins.append((arr, (arr.shape[0], cb), lambda j, i, off=off: (0, off + j)))
    ins.append((dout, (tm, cb), lambda j, i: (nt - 1 - i, j)))
    nx, nw = len(xs), len(ws)
    outs = [((T, ncb * cb), dx_dtype, (tm, cb), lambda j, i: (nt - 1 - i, j)) for _ in xs]
    outs += [((arr.shape[0], ncb * cb), F32, (arr.shape[0], cb), lambda j, i: (0, j)) for arr, _ in ws]
    scratch = [pltpu.VMEM((HALO, cb), F32) for _ in xs]

    def body(in_refs, out_refs, carry):
        j, i = pl.program_id(0), pl.program_id(1)
        first = ((nt - 1 - i) % tps) == 0
        wts = [ref[...] for ref in in_refs[2 * nx:2 * nx + nw]]

        @pl.when(i == 0)
        def _():
            for c in carry:
                c[...] = jnp.zeros_like(c)

        carried = [c[...] for c in carry]
        dw_sum = None
        for r in reversed(range(0, tm, STRIP)):
            xts = [in_refs[2 * m][r:r + STRIP, :].astype(F32) for m in range(nx)]
            if r == 0:
                tails = [jnp.where(first, 0.0, in_refs[2 * m + 1][...].astype(F32)) for m in range(nx)]
            else:
                tails = [in_refs[2 * m][r - HALO:r, :].astype(F32) for m in range(nx)]
            _, vjp = jax.vjp(lambda tl, xt, wt: fn(j, tl, xt, wt), tails, xts, wts)
            dtails, dxts, dwts = vjp(in_refs[-1][r:r + STRIP, :].astype(F32))
            for m in range(nx):
                pad = jnp.concatenate([jnp.zeros((STRIP - HALO, cb), F32), carried[m]], axis=0)
                out_refs[m][r:r + STRIP, :] = (dxts[m] + pad).astype(dx_dtype)
            carried = [jnp.where(first, 0.0, dt) for dt in dtails] if r == 0 else list(dtails)
            dw_sum = list(dwts) if dw_sum is None else [a + b for a, b in zip(dw_sum, dwts)]
        for m in range(nx):
            carry[m][...] = carried[m]
        for m in range(nw):
            o_ref = out_refs[nx + m]

            @pl.when(i == 0)
            def _(o_ref=o_ref):
                o_ref[...] = jnp.zeros_like(o_ref)

            o_ref[...] += dw_sum[m]

    return _call(body, ins, outs, (ncb, nt), name=name, scratch=scratch)


QKV_CB = 512


def _qkv_fn(j, tails, xts, wts):
    y = jax.nn.silu(_dwconv(tails[0], xts[0], wts[0]))
    scale = jnp.where(j < 1024 // QKV_CB, HD ** -0.5, 1.0)
    parts = []
    for h in range(QKV_CB // HD):
        yh = y[:, h * HD:(h + 1) * HD]
        nh = yh * lax.rsqrt(jnp.sum(yh * yh, axis=-1, keepdims=True) + EPS)
        parts.append(jnp.where(j < 2048 // QKV_CB, nh * scale, yh))
    return jnp.concatenate(parts, axis=1)


def _ffn_fn(j, tails, xts, wts):
    return jax.nn.silu(_dwconv(tails[0], xts[0], wts[0])) * _dwconv(tails[1], xts[1], wts[1])


BNN = (((2,), (1,)), ((0,), (0,)))
BNT = (((2,), (2,)), ((0,), (0,)))
BTN = (((1,), (1,)), ((0,), (0,)))


@jax.custom_vjp
def _tri_inv(L):
    C = L.shape[-1]
    ii = lax.broadcasted_iota(jnp.int32, (C, C), 0)
    jj = lax.broadcasted_iota(jnp.int32, (C, C), 1)
    eye = (ii == jj).astype(F32)
    X = eye - jnp.where((ii >> 1) == (jj >> 1), L, 0.0)
    s = 1
    while (2 << s) <= C:
        E = jnp.where(((ii >> (s + 1)) == (jj >> (s + 1))) & ((ii >> s) != (jj >> s)), L, 0.0)
        X = X - _dot(_dot(X, E, BNN, precision=TRI_PRECISION), X, BNN, precision=TRI_PRECISION)
        s += 1
    return X


def _tri_inv_fwd(L):
    X = _tri_inv(L)
    return X, X


def _tri_inv_bwd(X, dX):
    return (-_dot(_dot(X, dX, BTN, precision=TRI_PRECISION), X, BNT, precision=TRI_PRECISION),)


_tri_inv.defvjp(_tri_inv_fwd, _tri_inv_bwd)


def _gdn_chunk(q, k, v, gc, gr, beta, S):
    C = q.shape[1]
    ii = lax.broadcasted_iota(jnp.int32, (C, C), 0)
    jj = lax.broadcasted_iota(jnp.int32, (C, C), 1)
    lower = ii >= jj
    decay = jnp.where(lower, jnp.exp(jnp.where(lower, gc - gr, 0.0)), 0.0)
    kb, vb = k * beta, v * beta
    L = jnp.where(ii > jj, _dot(kb, k, BNT) * decay, 0.0)
    Tinv = _tri_inv(L)
    eg = jnp.exp(gc)
    u = _dot(Tinv, vb, BNN, precision=TRI_PRECISION)
    w = _dot(Tinv, kb * eg, BNN, precision=TRI_PRECISION)
    a = _dot(q, k, BNT) * decay
    g_last = gc[:, C - 1:C, :]
    kd = k * jnp.exp(g_last - gc)
    v_new = u - _dot(w, S, BNN)
    o = _dot(q * eg, S, BNN) + _dot(a, v_new, BNN)
    S_new = S * jnp.exp(g_last) + _dot(kd, v_new, BTN)
    return o, S_new


def _heads(ref, nb, width=HD):
    return jnp.stack([ref[b, :, h * width:(h + 1) * width].astype(F32) for b in range(nb) for h in range(HEADS)])


def _gdn_fwd(qkvn, gcum, grT, beta, *, B, S):
    N, T = S // CHUNK, B * S
    row = lambda c: (lambda n: (0, n, c))
    qkv3, gc3, b3 = qkvn.reshape(B, S, 3072), gcum.reshape(B, S, LANES), beta.reshape(B, S, LANES)
    gr5 = grT.reshape(B, N, HEADS, 1, CHUNK)
    ins = [(qkv3, (B, CHUNK, 1024), row(0)), (qkv3, (B, CHUNK, 1024), row(1)), (qkv3, (B, CHUNK, 1024), row(2)),
           (gc3, (B, CHUNK, LANES), row(0)), (gr5, (B, 1, HEADS, 1, CHUNK), lambda n: (0, n, 0, 0, 0)),
           (b3, (B, CHUNK, LANES), row(0))]
    outs = [((B, S, 1024), F32, (B, CHUNK, 1024), row(0)),
            ((B, N, HEADS, HD, HD), BF16, (B, 1, HEADS, HD, HD), lambda n: (0, n, 0, 0, 0))]

    def body(in_refs, out_refs, scr):
        q_ref, k_ref, v_ref, gc_ref, gr_ref, b_ref = in_refs
        o_ref, st_ref = out_refs
        S_ref = scr[0]

        @pl.when(pl.program_id(0) == 0)
        def _():
            S_ref[...] = jnp.zeros_like(S_ref)

        S0 = S_ref[...]
        for b in range(B):
            st_ref[b, 0] = S0[b * HEADS:(b + 1) * HEADS].astype(BF16)
        gr = jnp.concatenate([gr_ref[b, 0] for b in range(B)], axis=0)
        o, Sn = _gdn_chunk(_heads(q_ref, B), _heads(k_ref, B), _heads(v_ref, B), _heads(gc_ref, B, 1), gr,
                           _heads(b_ref, B, 1), S0)
        for b in range(B):
            for h in range(HEADS):
                o_ref[b, :, h * HD:(h + 1) * HD] = o[b * HEADS + h]
        S_ref[...] = Sn

    o, st = _call(body, ins, outs, (N,), name="gdn_core_fwd", scratch=[pltpu.VMEM((B * HEADS, HD, HD), F32)])
    return o.reshape(T, 1024), st


def _gdn_bwd(qkvn, gcum, grT, beta, states, do, *, B, S):
    N, T = S // CHUNK, B * S
    row = lambda c: (lambda n: (0, N - 1 - n, c))
    qkv3, gc3, b3 = qkvn.reshape(B, S, 3072), gcum.reshape(B, S, LANES), beta.reshape(B, S, LANES)
    gr5, do3 = grT.reshape(B, N, HEADS, 1, CHUNK), do.reshape(B, S, 1024)
    ins = [(qkv3, (B, CHUNK, 1024), row(0)), (qkv3, (B, CHUNK, 1024), row(1)), (qkv3, (B, CHUNK, 1024), row(2)),
           (gc3, (B, CHUNK, LANES), row(0)), (gr5, (B, 1, HEADS, 1, CHUNK), lambda n: (0, N - 1 - n, 0, 0, 0)),
           (b3, (B, CHUNK, LANES), row(0)),
           (states, (B, 1, HEADS, HD, HD), lambda n: (0, N - 1 - n, 0, 0, 0)), (do3, (B, CHUNK, 1024), row(0))]
    outs = [((B, S, 3072), BF16, (B, CHUNK, 3072), row(0)), ((B, S, LANES), F32, (B, CHUNK, LANES), row(0)),
            ((B, N, HEADS, 1, CHUNK), F32, (B, 1, HEADS, 1, CHUNK), lambda n: (0, N - 1 - n, 0, 0, 0)),
            ((B, S, LANES), F32, (B, CHUNK, LANES), row(0))]

    def body(in_refs, out_refs, scr):
        q_ref, k_ref, v_ref, gc_ref, gr_ref, b_ref, st_ref, do_ref = in_refs
        dqkv_ref, dgc_ref, dgr_ref, db_ref = out_refs
        dS_ref = scr[0]

        @pl.when(pl.program_id(0) == 0)
        def _():
            dS_ref[...] = jnp.zeros_like(dS_ref)

        gr = jnp.concatenate([gr_ref[b, 0] for b in range(B)], axis=0)
        st = jnp.concatenate([st_ref[b, 0] for b in range(B)], axis=0).astype(F32)
        args = (_heads(q_ref, B), _heads(k_ref, B), _heads(v_ref, B), _heads(gc_ref, B, 1), gr, _heads(b_ref, B, 1), st)
        _, vjp = jax.vjp(_gdn_chunk, *args)
        dq, dk, dv, dgc, dgr, db, dS = vjp((_heads(do_ref, B), dS_ref[...]))
        lane = lax.broadcasted_iota(jnp.int32, (CHUNK, LANES), 1)
        for b in range(B):
            dgc_all = jnp.zeros((CHUNK, LANES), F32)
            db_all = jnp.zeros((CHUNK, LANES), F32)
            for h in range(HEADS):
                i = b * HEADS + h
                dqkv_ref[b, :, h * HD:(h + 1) * HD] = dq[i].astype(BF16)
                dqkv_ref[b, :, 1024 + h * HD:1024 + (h + 1) * HD] = dk[i].astype(BF16)
                dqkv_ref[b, :, 2048 + h * HD:2048 + (h + 1) * HD] = dv[i].astype(BF16)
                dgc_all = jnp.where(lane == h, dgc[i], dgc_all)
                db_all = jnp.where(lane == h, db[i], db_all)
            dgc_ref[b] = dgc_all
            db_ref[b] = db_all
            dgr_ref[b, 0] = dgr[b * HEADS:(b + 1) * HEADS]
        dS_ref[...] = dS

    dqkv, dgc, dgr, db = _call(body, ins, outs, (N,), name="gdn_core_bwd",
                               scratch=[pltpu.VMEM((B * HEADS, HD, HD), F32)])
    return dqkv.reshape(T, 3072), dgc.reshape(T, LANES), dgr.reshape(B * N, HEADS, 1, CHUNK), db.reshape(T, LANES)


def _gate_fn(za, zb, alog, dtb):
    tm = za.shape[0]
    g = -jnp.exp(alog) * jax.nn.softplus(za + dtb)
    ii = lax.broadcasted_iota(jnp.int32, (tm, tm), 0)
    jj = lax.broadcasted_iota(jnp.int32, (tm, tm), 1)
    tri = ((ii >= jj) & ((ii >> 6) == (jj >> 6))).astype(F32)
    return _dot(tri, g, precision=HI), jax.nn.sigmoid(zb)


def _scores(qn_ref, qp_ref, kn_ref, kp_ref, diag):
    q = jnp.concatenate([qn_ref[...], qp_ref[...]], axis=1)
    k = jnp.concatenate([kn_ref[...], kp_ref[...]], axis=1)
    s = _dot(q, k, NT) * SM_SCALE
    if diag:
        t = s.shape[0]
        ii = lax.broadcasted_iota(jnp.int32, (t, t), 0)
        jj = lax.broadcasted_iota(jnp.int32, (t, t), 1)
        s = jnp.where(ii >= jj, s, -jnp.inf)
    return s, q, k


HPB = 8
HW = HPB * HD


def _head_refs(refs, hh):
    return [r.at[:, hh * HD:(hh + 1) * HD] for r in refs]


def _flash_fwd(qn, qp, kn, kp, v, *, B, S, t):
    nb, T = S // t, B * S
    qmap = lambda b, h, qi, ki: (b * nb + qi, h)
    kmap = lambda b, h, qi, ki: (b * nb + jnp.minimum(ki, qi), h)
    kpmap = lambda b, h, qi, ki: (b * nb + jnp.minimum(ki, qi), 0)
    ins = [(qn, (t, HW), qmap), (qp, (t, HW), qmap), (kn, (t, HW), kmap), (kp, (t, HD), kpmap), (v, (t, HW), kmap)]
    outs = [((T, 1024), BF16, (t, HW), qmap),
            ((HEADS, T, 1), F32, (HPB, t, 1), lambda b, h, qi, ki: (h, b * nb + qi, 0))]
    scratch = [pltpu.VMEM((HPB, t, 1), F32), pltpu.VMEM((HPB, t, 2 * HD), F32)]

    def body(in_refs, out_refs, scr):
        qn_ref, qp_ref, kn_ref, kp_ref, v_ref = in_refs
        o_ref, lse_ref = out_refs
        m_ref, acc_ref = scr
        qi, ki = pl.program_id(2), pl.program_id(3)

        @pl.when(ki == 0)
        def _():
            m_ref[...] = jnp.full_like(m_ref, -jnp.inf)
            acc_ref[...] = jnp.zeros_like(acc_ref)

        def step(diag):
            for hh in range(HPB):
                qn_h, qp_h, kn_h, v_h = _head_refs((qn_ref, qp_ref, kn_ref, v_ref), hh)
                s, _, _ = _scores(qn_h, qp_h, kn_h, kp_ref, diag)
                m_old = m_ref[hh]
                m_new = jnp.maximum(m_old, jnp.max(s, axis=-1, keepdims=True))
                p = jnp.exp(s - m_new)
                alpha = jnp.exp(m_old - m_new)
                v1 = jnp.concatenate([v_h[...], jnp.ones((t, HD), BF16)], axis=1)
                acc_ref[hh] = alpha * acc_ref[hh] + _dot(p.astype(BF16), v1)
                m_ref[hh] = m_new

        @pl.when(ki < qi)
        def _():
            step(False)

        @pl.when(ki == qi)
        def _():
            step(True)
            for hh in range(HPB):
                o_ref[:, hh * HD:(hh + 1) * HD] = (acc_ref[hh, :, :HD] / acc_ref[hh, :, HD:]).astype(BF16)
                lse_ref[hh] = m_ref[hh] + jnp.log(acc_ref[hh, :, HD:HD + 1])

    return _call(body, ins, outs, (B, HEADS // HPB, nb, nb), name="mla_flash_fwd", scratch=scratch,
                 semantics=("parallel", "parallel", "parallel", "arbitrary"))


def _flash_bwd_dq(qn, qp, kn, kp, v, o, do, lse, *, B, S, t):
    nb, T = S // t, B * S
    qmap = lambda b, h, qi, ki: (b * nb + qi, h)
    kmap = lambda b, h, qi, ki: (b * nb + jnp.minimum(ki, qi), h)
    kpmap = lambda b, h, qi, ki: (b * nb + jnp.minimum(ki, qi), 0)
    ins = [(qn, (t, HW), qmap), (qp, (t, HW), qmap), (kn, (t, HW), kmap), (kp, (t, HD), kpmap), (v, (t, HW), kmap),
           (o, (t, HW), qmap), (do, (t, HW), qmap), (lse, (HPB, t, 1), lambda b, h, qi, ki: (h, b * nb + qi, 0))]
    outs = [((T, 1024), BF16, (t, HW), qmap), ((T, 1024), F32, (t, HW), qmap),
            ((HEADS, T, 1), F32, (HPB, t, 1), lambda b, h, qi, ki: (h, b * nb + qi, 0))]
    scratch = [pltpu.VMEM((HPB, t, 1), F32), pltpu.VMEM((HPB, t, 2 * HD), F32)]

    def body(in_refs, out_refs, scr):
        qn_ref, qp_ref, kn_ref, kp_ref, v_ref, o_ref, do_ref, lse_ref = in_refs
        dqn_ref, dqp_ref, dlo_ref = out_refs
        dl_ref, acc_ref = scr
        qi, ki = pl.program_id(2), pl.program_id(3)

        @pl.when(ki == 0)
        def _():
            for hh in range(HPB):
                o_h, do_h = _head_refs((o_ref, do_ref), hh)
                dl_ref[hh] = jnp.sum(do_h[...].astype(F32) * o_h[...].astype(F32), axis=-1, keepdims=True)
            acc_ref[...] = jnp.zeros_like(acc_ref)

        def step(diag):
            for hh in range(HPB):
                qn_h, qp_h, kn_h, v_h, do_h = _head_refs((qn_ref, qp_ref, kn_ref, v_ref, do_ref), hh)
                s, _, k = _scores(qn_h, qp_h, kn_h, kp_ref, diag)
                p = jnp.exp(s - lse_ref[hh])
                dp = _dot(do_h[...], v_h[...], NT)
                ds = p * (dp - dl_ref[hh]) * SM_SCALE
                acc_ref[hh] += _dot(ds.astype(BF16), k)

        @pl.when(ki < qi)
        def _():
            step(False)

        @pl.when(ki == qi)
        def _():
            step(True)
            for hh in range(HPB):
                dqn_ref[:, hh * HD:(hh + 1) * HD] = acc_ref[hh, :, :HD].astype(BF16)
                dqp_ref[:, hh * HD:(hh + 1) * HD] = acc_ref[hh, :, HD:]
            dlo_ref[...] = dl_ref[...]

    return _call(body, ins, outs, (B, HEADS // HPB, nb, nb), name="mla_flash_bwd_dq", scratch=scratch,
                 semantics=("parallel", "parallel", "parallel", "arbitrary"))


def _flash_bwd_dkv(qn, qp, kn, kp, v, do, lse_t, dl_t, *, B, S, t):
    nb, T = S // t, B * S
    qmap = lambda b, h, ki, qi: (b * nb + jnp.maximum(qi, ki), h)
    kmap = lambda b, h, ki, qi: (b * nb + ki, h)
    tmap = lambda b, h, ki, qi: (h, 0, b * nb + jnp.maximum(qi, ki))
    ins = [(qn, (t, HW), qmap), (qp, (t, HW), qmap), (kn, (t, HW), kmap),
           (kp, (t, HD), lambda b, h, ki, qi: (b * nb + ki, 0)), (v, (t, HW), kmap), (do, (t, HW), qmap),
           (lse_t, (HPB, 1, t), tmap), (dl_t, (HPB, 1, t), tmap)]
    outs = [((T, 1024), BF16, (t, HW), kmap), ((HEADS, T, HD), F32, (HPB, t, HD), lambda b, h, ki, qi: (h, b * nb + ki, 0)),
            ((T, 1024), BF16, (t, HW), kmap)]
    scratch = [pltpu.VMEM((HPB, t, 2 * HD), F32), pltpu.VMEM((HPB, t, HD), F32)]

    def body(in_refs, out_refs, scr):
        qn_ref, qp_ref, kn_ref, kp_ref, v_ref, do_ref, lse_ref, dl_ref = in_refs
        dkn_ref, dkp_ref, dv_ref = out_refs
        dk_acc, dv_acc = scr
        ki, qi = pl.program_id(2), pl.program_id(3)

        @pl.when(qi == 0)
        def _():
            dk_acc[...] = jnp.zeros_like(dk_acc)
            dv_acc[...] = jnp.zeros_like(dv_acc)

        def step(diag):
            for hh in range(HPB):
                qn_h, qp_h, kn_h, v_h, do_h = _head_refs((qn_ref, qp_ref, kn_ref, v_ref, do_ref), hh)
                q = jnp.concatenate([qn_h[...], qp_h[...]], axis=1)
                k = jnp.concatenate([kn_h[...], kp_ref[...]], axis=1)
                st = _dot(k, q, NT) * SM_SCALE
                if diag:
                    ii = lax.broadcasted_iota(jnp.int32, (t, t), 0)
                    jj = lax.broadcasted_iota(jnp.int32, (t, t), 1)
                    st = jnp.where(ii <= jj, st, -jnp.inf)
                do_t = do_h[...]
                pt = jnp.exp(st - lse_ref[hh])
                dst = pt * (_dot(v_h[...], do_t, NT) - dl_ref[hh]) * SM_SCALE
                dv_acc[hh] += _dot(pt.astype(BF16), do_t)
                dk_acc[hh] += _dot(dst.astype(BF16), q)

        @pl.when(qi > ki)
        def _():
            step(False)

        @pl.when(qi == ki)
        def _():
            step(True)

        @pl.when(qi == nb - 1)
        def _():
            for hh in range(HPB):
                dkn_ref[:, hh * HD:(hh + 1) * HD] = dk_acc[hh, :, :HD].astype(BF16)
                dkp_ref[hh] = dk_acc[hh, :, HD:]
                dv_ref[:, hh * HD:(hh + 1) * HD] = dv_acc[hh].astype(BF16)

    return _call(body, ins, outs, (B, HEADS // HPB, nb, nb), name="mla_flash_bwd_dkv", scratch=scratch,
                 semantics=("parallel", "parallel", "parallel", "arbitrary"))


def _allgather_async(shards, *, name, collective_id):
    n_arr = len(shards)
    hbm = pltpu.MemorySpace.HBM
    x_refs = [jax.new_ref(a, memory_space=hbm) for a in shards]
    out_refs = [jax.empty_ref(jax.ShapeDtypeStruct((N_DEV * a.shape[0], a.shape[1]), a.dtype), memory_space=hbm)
                for a in shards]

    @pl.kernel(mesh=plsc.ScalarSubcoreMesh(axis_name="seq", num_cores=1), name=name,
               scratch_types=(pltpu.SemaphoreType.DMA((n_arr, 7)), pltpu.SemaphoreType.DMA((n_arr, 7)),
                              pltpu.SemaphoreType.DMA((n_arr,))),
               compiler_params=pltpu.CompilerParams(collective_id=collective_id))
    def launch(send_sems, recv_sems, local_sems):
        x, y, c = lax.axis_index("x"), lax.axis_index("y"), lax.axis_index("c")
        me, sibling = (x, y, c), (x, y, 1 - c)
        chips = [(1 - x, y), (x, 1 - y), (1 - x, 1 - y)]
        barrier = pltpu.get_barrier_semaphore()
        for p in [sibling] + [(*chip, c) for chip in chips]:
            pl.semaphore_signal(barrier, inc=1, device_id=p, device_id_type=pl.DeviceIdType.MESH)
        pl.semaphore_wait(barrier, 4)

        def rows(a, px, py, pc):
            m_per = shards[a].shape[0]
            return out_refs[a].at[pl.ds((4 * px + 2 * py + pc) * m_per, m_per), :]

        def copy(a, k, block, to, src=None):
            return pltpu.make_async_remote_copy(
                src_ref=rows(a, *block) if src is None else src, dst_ref=rows(a, *block),
                send_sem=send_sems.at[a, k], recv_sem=recv_sems.at[a, k], device_id=to,
                device_id_type=pl.DeviceIdType.MESH)

        mine = [pltpu.make_async_copy(x_refs[a], rows(a, *me), local_sems.at[a]) for a in range(n_arr)]
        for cp in mine:
            cp.start()
        first = []
        for a in range(n_arr):
            first.append(copy(a, 0, me, sibling, src=x_refs[a]))
            first += [copy(a, 1 + j, me, (*chip, c), src=x_refs[a]) for j, chip in enumerate(chips)]
        for cp in first:
            cp.start()
        passed = []
        for j, chip in enumerate(chips):
            for a in range(n_arr):
                copy(a, 1 + j, (*chip, c), me).wait_recv()
                cp = copy(a, 4 + j, (*chip, c), sibling)
                cp.start()
                passed.append(cp)
        for a in range(n_arr):
            copy(a, 0, sibling, me).wait_recv()
        for j, chip in enumerate(chips):
            for a in range(n_arr):
                copy(a, 4 + j, (*chip, 1 - c), me).wait_recv()
        for cp in first + passed:
            cp.wait_send()
        for cp in mine:
            cp.wait()

    launch()
    return [r[...] for r in out_refs]


def _alltoall_async(sends, *, name, collective_id):
    n_arr = len(sends)
    hbm = pltpu.MemorySpace.HBM
    s_refs = [jax.new_ref(a, memory_space=hbm) for a in sends]
    r_refs = [jax.empty_ref(jax.ShapeDtypeStruct(a.shape, a.dtype), memory_space=hbm) for a in sends]

    @pl.kernel(mesh=plsc.ScalarSubcoreMesh(axis_name="seq", num_cores=1), name=name,
               scratch_types=(pltpu.SemaphoreType.DMA((n_arr, 7)), pltpu.SemaphoreType.DMA((n_arr, 7)),
                              pltpu.SemaphoreType.DMA((n_arr,))),
               compiler_params=pltpu.CompilerParams(collective_id=collective_id))
    def launch(send_sems, recv_sems, local_sems):
        x, y, c = lax.axis_index("x"), lax.axis_index("y"), lax.axis_index("c")
        me = 4 * x + 2 * y + c
        peers = [(1 - x if k & 4 else x, 1 - y if k & 2 else y, 1 - c if k & 1 else c) for k in range(1, N_DEV)]
        barrier = pltpu.get_barrier_semaphore()
        for p in peers:
            pl.semaphore_signal(barrier, inc=1, device_id=p, device_id_type=pl.DeviceIdType.MESH)
        pl.semaphore_wait(barrier, N_DEV - 1)

        def rows(ref, a, idx):
            m_per = sends[a].shape[0] // N_DEV
            return ref.at[pl.ds(idx * m_per, m_per), :]

        local = [pltpu.make_async_copy(rows(s_refs[a], a, me), rows(r_refs[a], a, me), local_sems.at[a])
                 for a in range(n_arr)]
        for cp in local:
            cp.start()
        copies = []
        for k, (px, py, pc) in enumerate(peers):
            for a in range(n_arr):
                cp = pltpu.make_async_remote_copy(
                    src_ref=rows(s_refs[a], a, 4 * px + 2 * py + pc), dst_ref=rows(r_refs[a], a, me),
                    send_sem=send_sems.at[a, k], recv_sem=recv_sems.at[a, k],
                    device_id=(px, py, pc), device_id_type=pl.DeviceIdType.MESH)
                cp.start()
                copies.append(cp)
        for cp in copies:
            cp.wait()
        for cp in local:
            cp.wait()

    launch()
    return [r[...] for r in r_refs]


def _reduce_adam(parts, w, m, v, *, tr, name):
    R, C = w.shape
    nR = R // tr
    ins = [(parts, (tr, C), lambda i, s=s: (s * nR + i, 0)) for s in range(N_DEV)]
    ins += [(a, (tr, C), lambda i: (i, 0)) for a in (w, m, v)]
    outs = [((R, C), F32, (tr, C), lambda i: (i, 0)) for _ in range(4)]
    c1 = 1.0 - ADAM_B1 ** ADAM_STEP
    c2 = 1.0 - ADAM_B2 ** ADAM_STEP

    def body(in_refs, out_refs, _):
        g = in_refs[0][...].astype(F32)
        for s in range(1, N_DEV):
            g = g + in_refs[s][...].astype(F32)
        wv, mv, vv = in_refs[8][...], in_refs[9][...], in_refs[10][...]
        mn = ADAM_B1 * mv + (1.0 - ADAM_B1) * g
        vn = ADAM_B2 * vv + (1.0 - ADAM_B2) * (g * g)
        delta = -ADAM_LR * ((mn / c1) / (jnp.sqrt(vn / c2) + ADAM_EPS) + ADAM_WD * wv)
        out_refs[0][...] = g
        out_refs[1][...] = delta
        out_refs[2][...] = mn
        out_refs[3][...] = vn

    return _call(body, ins, outs, (nR,), name=name, semantics=("parallel",))


IN_C, UP_C, UQ_C, QKV_C = 858, 704, 192, 384
A_W, Q_W, V_W = 896, 256, 768
SLAB_TR = {"A": 256, "Q": 128, "C": 368, "V": 16}
SMALL = [("norm_mix_g", 1024), ("gdn_a_log", 8), ("gdn_dt_bias", 8), ("gdn_norm_g", 128), ("mla_q_norm_g", 384),
         ("mla_kv_norm_g", 256), ("norm_ffn_g", 1024), ("norm_final_g", 1024)]
SMALL_ROWS = 32
WEIGHT_ORDER = ["norm_mix_g", "w_in", "conv_qkv_w", "gdn_a_log", "gdn_dt_bias", "gdn_norm_g", "mla_q_norm_g", "w_uq",
                "mla_kv_norm_g", "w_ukv", "w_o_gdn", "w_o_mla", "w_out", "norm_ffn_g", "w_up", "conv_ffn_w", "w_down",
                "norm_final_g"]


def _padc(w, n):
    return jnp.pad(w, ((0, 0), (0, n - w.shape[1])))


def _padrc(w, r, n):
    return jnp.pad(w, ((0, r - w.shape[0]), (0, n - w.shape[1])))


def _slabs(p, dtype):
    A = jnp.concatenate([_padc(p["w_in"], A_W), _padc(p["w_up"], A_W)], axis=0).astype(dtype)
    Q = jnp.concatenate([_padc(p["w_uq"], Q_W), p["w_ukv"]], axis=0).astype(dtype)
    C = jnp.concatenate([p["w_o_gdn"], p["w_o_mla"], p["w_out"], p["w_down"]], axis=0).astype(dtype)
    V = jnp.concatenate([_padrc(p["conv_qkv_w"], 8, V_W), _padrc(p["conv_ffn_w"], 8, V_W)], axis=0).astype(F32)
    return {"A": A, "Q": Q, "C": C, "V": V}


def _unslab(sl):
    A, Q, C, V = sl["A"], sl["Q"], sl["C"], sl["V"]
    out = {"w_in": A[:1024, :IN_C], "w_up": A[1024:, :UP_C], "w_uq": Q[:384, :UQ_C], "w_ukv": Q[384:],
           "w_o_gdn": C[0:128], "w_o_mla": C[128:256], "w_out": C[256:384], "w_down": C[384:],
           "conv_qkv_w": V[0:GDN_CONV, :QKV_C], "conv_ffn_w": V[8:8 + FFN_CONV, :UP_C]}
    return {k: a[None] for k, a in out.items()}


def _take_cols(pieces, lo, hi):
    out, off = [], 0
    for arr, a, b in pieces:
        s, e = max(lo, off), min(hi, off + b - a)
        if s < e:
            out.append(arr[:, a + s - off:a + e - off])
        off += b - a
    return out[0] if len(out) == 1 else jnp.concatenate(out, axis=1)


LOSS_SLOT = sum(n for _, n in SMALL)


def _pack_small(d, loss=None):
    flat = jnp.concatenate([d[n].reshape(-1).astype(F32) for n, _ in SMALL]
                           + ([] if loss is None else [loss.reshape(1).astype(F32)]))
    return jnp.pad(flat, (0, SMALL_ROWS * LANES - flat.shape[0])).reshape(SMALL_ROWS, LANES)


def _unpack_small(buf, shapes):
    flat, out, off = buf.reshape(-1), {}, 0
    for name, n in SMALL:
        out[name] = flat[off:off + n].reshape(shapes[name])
        off += n
    return out


def _rot_cols(w):
    h = ROPE // 2
    return jnp.concatenate([-w[:, h:], w[:, :h]], axis=1)


def _unrot_cols(dw):
    h = ROPE // 2
    return jnp.concatenate([dw[:, h:], -dw[:, :h]], axis=1)


IN_SPLITS = [0, 3072, 4096, 4104, 4112, 4496, 4752, 4816, 5840, 6864]


def _layout_late(A_up, C):
    W = {"w_up": jnp.concatenate([A_up[j, :, :UP_C] for j in range(N_DEV)], axis=1),
         "w_o_gdn": C[:, 0:128].reshape(1024, D_MODEL), "w_o_mla": C[:, 128:256].reshape(1024, D_MODEL),
         "w_out": C[:, 256:384].reshape(1024, D_MODEL), "w_down": C[:, 384:].reshape(D_FF, D_MODEL)}
    return {k: v.astype(BF16) for k, v in W.items()}


def _layout_weights(g):
    A_in, Q, V = g["A_in"], g["Q"], g["V"]
    in_pieces = [(A_in[j], 0, IN_C) for j in range(N_DEV)]
    o = IN_SPLITS
    take = lambda lo, hi: _take_cols(in_pieces, lo, hi)
    kpe = take(o[6], o[7])
    W = {
        "in_qkv": take(o[0], o[1]),
        "in_ga": take(o[1], o[2]),
        "in_ab": jnp.concatenate([_padc(take(o[2], o[3]), LANES), _padc(take(o[3], o[4]), LANES)], axis=1),
        "in_small": jnp.concatenate([take(o[4], o[6]), _padc(kpe, LANES), _padc(_rot_cols(kpe), LANES)], axis=1),
        "in_gbr": take(o[7], o[9]),
        "uq_n": jnp.concatenate([Q[j, :384, :HD] for j in range(N_DEV)], axis=1),
        "ukv_k": jnp.concatenate([Q[j, 384:, :HD] for j in range(N_DEV)], axis=1),
        "ukv_v": jnp.concatenate([Q[j, 384:, HD:] for j in range(N_DEV)], axis=1),
    }
    pe = [Q[j, :384, HD:HD + ROPE] for j in range(N_DEV)]
    W["uq_p"] = jnp.concatenate([_padc(p, HD) for p in pe] + [_padc(_rot_cols(p), HD) for p in pe], axis=1)
    conv_qkv = jnp.concatenate([V[j, 0:GDN_CONV, :QKV_C] for j in range(N_DEV)], axis=1)
    conv_ffn = jnp.concatenate([V[j, 8:8 + FFN_CONV, :UP_C] for j in range(N_DEV)], axis=1)
    return {k: v.astype(BF16) for k, v in W.items()}, conv_qkv, conv_ffn


def _full_grads(dW):
    s = dW["in_small"]
    dkpe = s[:, 640:704] + _unrot_cols(s[:, 768:832])
    in_pieces = [(dW["in_qkv"], 0, 3072), (dW["in_ga"], 0, 1024), (dW["in_ab"], 0, 8), (dW["in_ab"], 128, 136),
                 (s, 0, 640), (dkpe, 0, ROPE), (dW["in_gbr"], 0, 2048)]
    pe = []
    for j in range(N_DEV):
        lin = dW["uq_p"][:, j * HD:j * HD + ROPE]
        rot = dW["uq_p"][:, 1024 + j * HD:1024 + j * HD + ROPE]
        pe.append(lin + _unrot_cols(rot))
    return in_pieces, pe


def _send_slabs(dW, d_conv_qkv, d_conv_ffn):
    in_pieces, pe = _full_grads(dW)
    A, Q, V = [], [], []
    for j in range(N_DEV):
        gin = _padc(_take_cols(in_pieces, j * IN_C, (j + 1) * IN_C), A_W)
        gup = _padc(dW["w_up"][:, j * UP_C:(j + 1) * UP_C], A_W)
        A.append(jnp.concatenate([gin, gup], axis=0))
        guq = _padc(jnp.concatenate([dW["uq_n"][:, j * HD:(j + 1) * HD], pe[j]], axis=1), Q_W)
        gukv = jnp.concatenate([dW["ukv_k"][:, j * HD:(j + 1) * HD], dW["ukv_v"][:, j * HD:(j + 1) * HD]], axis=1)
        Q.append(jnp.concatenate([guq, gukv], axis=0))
        V.append(jnp.concatenate([_padrc(d_conv_qkv[:, j * QKV_C:(j + 1) * QKV_C], 8, V_W),
                                  _padrc(d_conv_ffn[:, j * UP_C:(j + 1) * UP_C], 8, V_W)], axis=0))
    C = jnp.concatenate([dW["w_o_gdn"].reshape(N_DEV, 128, D_MODEL), dW["w_o_mla"].reshape(N_DEV, 128, D_MODEL),
                         dW["w_out"].reshape(N_DEV, 128, D_MODEL), dW["w_down"].reshape(N_DEV, 352, D_MODEL)], axis=1)
    return {"A": jnp.concatenate(A, axis=0).astype(BF16), "Q": jnp.concatenate(Q, axis=0).astype(BF16),
            "C": C.reshape(N_DEV * 736, D_MODEL).astype(BF16), "V": jnp.concatenate(V, axis=0)}


def _rope_tables(S):
    half = ROPE // 2
    inv = ROPE_THETA ** (-jnp.arange(half, dtype=F32) / half)
    ang = jnp.arange(S, dtype=F32)[:, None] * inv[None, :]
    cos = jnp.concatenate([jnp.cos(ang), jnp.cos(ang)], axis=1)
    sin = jnp.concatenate([jnp.sin(ang), jnp.sin(ang)], axis=1)
    return _padc(cos, HD), _padc(sin, HD)


def _local_step(x, tgt, W, late_weights, conv_qkv_w, conv_ffn_w, small, tm=None, ta=None):
    B, S, _ = x.shape
    T = B * S
    tm = tm or _pick(S, 512, CHUNK)
    ta = ta or _pick(S, 512, LANES)
    x2d, tgt2d = x.reshape(T, D_MODEL), tgt.reshape(T, D_MODEL)
    row = lambda v: v.reshape(1, -1).astype(F32)
    pad_row = lambda v: _padc(row(v), LANES)
    g_mix, g_ffn, g_fin = row(small["norm_mix_g"]), row(small["norm_ffn_g"]), row(small["norm_final_g"])
    g_gdn, g_q, g_kv = row(small["gdn_norm_g"]), row(small["mla_q_norm_g"]), row(small["mla_kv_norm_g"])
    alog, dtb = pad_row(small["gdn_a_log"]), pad_row(small["gdn_dt_bias"])
    cos, sin = _rope_tables(S)
    tps = S // tm
    tab = lambda a: (a, (tm, HD), lambda i: (i % tps, 0))
    col = lambda a, c, w: (a, (tm, w), lambda i, c=c: (i, c))

    h1 = _norm_fwd(x2d, g_mix, T=T, tm=tm, name="norm_mix_fwd")
    z_qkv = _mm(h1, W["in_qkv"], "nn", BF16, name="in_qkv_fwd")
    z_ga = _mm(h1, W["in_ga"], "nn", BF16, name="in_ga_fwd")
    z_ab = _mm(h1, W["in_ab"], "nn", F32, name="in_ab_fwd")
    z_small = _mm(h1, W["in_small"], "nn", F32, name="in_small_fwd", tn=896)
    z_gbr = _mm(h1, W["in_gbr"], "nn", BF16, name="in_gbr_fwd")

    qkvn = _conv_fwd(_qkv_fn, [(z_qkv, 0)], [(conv_qkv_w, 0)], 3072, BF16, T=T, S=S, tm=tm, cb=QKV_CB,
                     ncb=3072 // QKV_CB, name="gdn_qkv_fwd")
    gcum, beta = _row_call(lambda za, zb, al, db: _gate_fn(za, zb, al, db), [col(z_ab, 0, LANES), col(z_ab, 1, LANES)],
                           [alog, dtb], [(LANES, F32), (LANES, F32)], T=T, tm=tm, name="gdn_gate_fwd")
    grT = gcum[:, :HEADS].reshape(T // CHUNK, CHUNK, HEADS).transpose(0, 2, 1)[:, :, None, :]
    qkvn, late = late_weights(qkvn)
    W = {**W, **late}
    o_gdn, states = _gdn_fwd(qkvn, gcum, grT, beta, B=B, S=S)

    def gdn_out_fn(o, ga, g):
        parts = []
        for h in range(HEADS):
            sl = slice(h * HD, (h + 1) * HD)
            parts.append(_rms(o[:, sl], g) * jax.nn.silu(ga[:, sl].astype(F32)))
        return jnp.concatenate(parts, axis=1)

    oa = _row_call(lambda o, ga, g: (gdn_out_fn(o, ga, g),), [o_gdn, z_ga], [g_gdn], [(1024, BF16)], T=T, tm=tm,
                   name="gdn_out_fwd")[0]

    def mla_prep_fn(zq, zkv, zpl, zpr, c, s, gq, gkv):
        return _rms(zq, gq), _rms(zkv, gkv), zpl * c + zpr * s

    small_cols = [(z_small, (tm, Q_RANK), lambda i: (i, 0)), (z_small, (tm, LANES), lambda i: (i, 3)),
                  (z_small, (tm, LANES), lambda i: (i, 4)), (z_small, (tm, LANES), lambda i: (i, 5)),
                  (z_small, (tm, LANES), lambda i: (i, 6))]

    def mla_prep_fwd(zq, zkv0, zkv1, zpl, zpr, c, s, gq, gkv):
        return mla_prep_fn(zq, jnp.concatenate([zkv0, zkv1], axis=1), zpl, zpr, c, s, gq, gkv)

    cq, ckv, kpe = _row_call(mla_prep_fwd, small_cols + [tab(cos), tab(sin)], [g_q, g_kv],
                             [(Q_RANK, BF16), (KV_RANK, BF16), (HD, BF16)], T=T, tm=tm, name="mla_prep_fwd")
    qn = _mm(cq, W["uq_n"], "nn", BF16, name="uq_n_fwd")
    qpl = _mm(cq, W["uq_p"], "nn", F32, name="uq_p_fwd")
    kn = _mm(ckv, W["ukv_k"], "nn", BF16, name="ukv_k_fwd")
    vb = _mm(ckv, W["ukv_v"], "nn", BF16, name="ukv_v_fwd")

    def qrope_fn(lin, rot, c, s):
        return lin * jnp.tile(c, (1, HEADS)) + rot * jnp.tile(s, (1, HEADS))

    qp = _row_call(lambda lin, rot, c, s: (qrope_fn(lin, rot, c, s),), [col(qpl, 0, 1024), col(qpl, 1, 1024), tab(cos), tab(sin)],
                   [], [(1024, BF16)], T=T, tm=tm, name="q_rope_fwd")[0]
    ob, lse = _flash_fwd(qn, qp, kn, kpe, vb, B=B, S=S, t=ta)

    def merge_fn(ya, yb, ga, gb):
        return jax.nn.sigmoid(ga.astype(F32)) * ya + jax.nn.sigmoid(gb.astype(F32)) * yb

    def merge_fwd(oat, obt, ga, gb, wog, wom):
        ya, yb = _dot(oat, wog), _dot(obt, wom)
        return ya, yb, merge_fn(ya, yb, ga, gb)

    ya, yb, merged = _row_call(merge_fwd, [oa, ob, col(z_gbr, 0, 1024), col(z_gbr, 1, 1024)], [W["w_o_gdn"], W["w_o_mla"]],
                               [(1024, BF16), (1024, BF16), (1024, BF16)], T=T, tm=tm, name="merge_fwd")
    x1 = _mm(merged, W["w_out"], "nn", F32, add=x2d, name="w_out_fwd")

    h2 = _norm_fwd(x1, g_ffn, T=T, tm=tm, name="norm_ffn_fwd")
    up = _mm(h2, W["w_up"], "nn", BF16, name="w_up_fwd")
    FCB = 256
    nfb = D_FF // FCB
    f = _conv_fwd(_ffn_fn, [(up, 0), (up, 2)], [(conv_ffn_w, 0), (conv_ffn_w, 2)], D_FF, BF16, T=T, S=S, tm=tm,
                  cb=D_FF // 2, ncb=2, name="ffn_act_fwd")
    x2 = _mm(f, W["w_down"], "nn", F32, add=x1, name="w_down_fwd", tk=1408)

    def final_fn(xt, tt, g):
        def lossf(xv, gv):
            e = _rms(xv, gv) - tt
            return 0.5 * jnp.sum(jnp.mean(e * e, axis=-1))

        l, vjp = jax.vjp(lossf, xt, g)
        dx, dg = vjp(jnp.ones((), F32))
        return dx, jnp.full((1, LANES), l, F32), dg

    dx2, loss_v, dg_fin = _row_call(final_fn, [x2, tgt2d], [g_fin], [(1024, F32)], [((1, LANES), F32), ((1, 1024), F32)],
                                    T=T, tm=tm, name="loss_head")

    dW = {}
    df = _mm(dx2, W["w_down"], "nt", BF16, name="w_down_dx")
    dW["w_down"] = _mm(f, dx2, "tn", F32, name="w_down_dw")
    dug, duu, dcw_g, dcw_u = _conv_bwd(_ffn_fn, [(up, 0), (up, nfb)], [(conv_ffn_w, 0), (conv_ffn_w, nfb)], df, BF16,
                                       T=T, S=S, tm=tm, cb=FCB, ncb=nfb, name="ffn_act_bwd")
    d_conv_ffn = jnp.concatenate([dcw_g, dcw_u], axis=1)
    wup_g, wup_u = W["w_up"][:, :D_FF], W["w_up"][:, D_FF:]
    dh2 = _mm(dug, wup_g, "nt", F32, name="w_up_dx_g")
    dh2 = _mm(duu, wup_u, "nt", F32, add=dh2, name="w_up_dx_u")
    dW["w_up"] = jnp.concatenate([_mm(h2, dug, "tn", F32, name="w_up_dw_g"), _mm(h2, duu, "tn", F32, name="w_up_dw_u")], axis=1)
    dx1, dg_ffn = _norm_bwd(x1, g_ffn, dh2, dx2, T=T, tm=tm, name="norm_ffn_bwd")

    dmerged = _mm(dx1, W["w_out"], "nt", F32, name="w_out_dx")
    dW["w_out"] = _mm(merged, dx1, "tn", F32, name="w_out_dw")

    def merge_bwd(dm, yat, ybt, ga, gb):
        _, vjp = jax.vjp(merge_fn, yat.astype(F32), ybt.astype(F32), ga, gb)
        return vjp(dm)

    dya, dyb, dgbr_a, dgbr_b = _row_call(merge_bwd, [dmerged, ya, yb, col(z_gbr, 0, 1024), col(z_gbr, 1, 1024)], [],
                                         [(1024, BF16)] * 4, T=T, tm=tm, name="merge_bwd")
    doa = _mm(dya, W["w_o_gdn"], "nt", F32, name="w_o_gdn_dx")
    dob = _mm(dyb, W["w_o_mla"], "nt", BF16, name="w_o_mla_dx")
    dW["w_o_gdn"] = _mm(oa, dya, "tn", F32, name="w_o_gdn_dw")
    dW["w_o_mla"] = _mm(ob, dyb, "tn", F32, name="w_o_mla_dw")

    dqn, dqp, dl = _flash_bwd_dq(qn, qp, kn, kpe, vb, ob, dob, lse, B=B, S=S, t=ta)
    dkn, dkp, dvb = _flash_bwd_dkv(qn, qp, kn, kpe, vb, dob, lse.reshape(HEADS, 1, T), dl.reshape(HEADS, 1, T),
                                   B=B, S=S, t=ta)

    def qrope_bwd(d, c, s):
        return d * jnp.tile(c, (1, HEADS)), d * jnp.tile(s, (1, HEADS))

    dq_lin, dq_rot = _row_call(qrope_bwd, [dqp, tab(cos), tab(sin)], [], [(1024, BF16), (1024, BF16)], T=T, tm=tm,
                               name="q_rope_bwd")
    wp_lin, wp_rot = W["uq_p"][:, :1024], W["uq_p"][:, 1024:]
    dcq = _mm(dqn, W["uq_n"], "nt", F32, name="uq_n_dx")
    dcq = _mm(dq_lin, wp_lin, "nt", F32, add=dcq, name="uq_pl_dx")
    dcq = _mm(dq_rot, wp_rot, "nt", F32, add=dcq, name="uq_pr_dx")
    dW["uq_n"] = _mm(cq, dqn, "tn", F32, name="uq_n_dw")
    dW["uq_p"] = jnp.concatenate([_mm(cq, dq_lin, "tn", F32, name="uq_pl_dw"), _mm(cq, dq_rot, "tn", F32, name="uq_pr_dw")], axis=1)
    dckv = _mm(dkn, W["ukv_k"], "nt", F32, name="ukv_k_dx")
    dckv = _mm(dvb, W["ukv_v"], "nt", F32, add=dckv, name="ukv_v_dx")
    dW["ukv_k"] = _mm(ckv, dkn, "tn", F32, name="ukv_k_dw")
    dW["ukv_v"] = _mm(ckv, dvb, "tn", F32, name="ukv_v_dw")

    def mla_prep_bwd(zq, zkv0, zkv1, zpl, zpr, c, s, dcqt, dckvt, dkpt, gq, gkv):
        zkv = jnp.concatenate([zkv0, zkv1], axis=1)
        _, vjp = jax.vjp(lambda a, b, p, r, g1, g2: mla_prep_fn(a, b, p, r, c, s, g1, g2), zq, zkv, zpl, zpr, gq, gkv)
        dk = dkpt[0]
        for h in range(1, HEADS):
            dk = dk + dkpt[h]
        dzq, dzkv, dzpl, dzpr, dgq, dgkv = vjp((dcqt, dckvt, dk))
        return jnp.concatenate([dzq, dzkv, dzpl, dzpr], axis=1), dgq, dgkv

    dz_small, dg_q, dg_kv = _row_call(
        mla_prep_bwd, small_cols + [tab(cos), tab(sin), dcq, dckv, (dkp, (HEADS, tm, HD), lambda i: (0, i, 0))],
        [g_q, g_kv], [(896, BF16)], [((1, Q_RANK), F32), ((1, KV_RANK), F32)], T=T, tm=tm, name="mla_prep_bwd")

    def gdn_out_bwd(o, ga, dot_, g):
        _, vjp = jax.vjp(gdn_out_fn, o, ga, g)
        return vjp(dot_)

    do_gdn, dz_ga, dg_gdn = _row_call(gdn_out_bwd, [o_gdn, z_ga, doa], [g_gdn], [(1024, F32), (1024, BF16)],
                                      [((1, HD), F32)], T=T, tm=tm, name="gdn_out_bwd")
    dqkvn, dgc, dgrT, dbeta = _gdn_bwd(qkvn, gcum, grT, beta, states, do_gdn, B=B, S=S)
    dgc_tot = dgc + _padc(dgrT[:, :, 0, :].transpose(0, 2, 1).reshape(T, HEADS), LANES)

    def gate_bwd(za, zb, dg, db, al, db_):
        _, vjp = jax.vjp(_gate_fn, za, zb, al, db_)
        return vjp((dg, db))

    dz_a, dz_b, d_alog, d_dtb = _row_call(gate_bwd, [col(z_ab, 0, LANES), col(z_ab, 1, LANES), dgc_tot, dbeta], [alog, dtb],
                                          [(LANES, BF16), (LANES, BF16)], [((1, LANES), F32), ((1, LANES), F32)],
                                          T=T, tm=tm, name="gdn_gate_bwd")
    dz_qkv, d_conv_qkv = _conv_bwd(_qkv_fn, [(z_qkv, 0)], [(conv_qkv_w, 0)], dqkvn, BF16, T=T, S=S, tm=tm, cb=QKV_CB,
                                   ncb=3072 // QKV_CB, name="gdn_qkv_bwd")

    dz_ab = jnp.concatenate([dz_a, dz_b], axis=1)
    dz_gbr = jnp.concatenate([dgbr_a, dgbr_b], axis=1)
    dh1 = None
    for key, dz in (("in_qkv", dz_qkv), ("in_ga", dz_ga), ("in_ab", dz_ab), ("in_small", dz_small), ("in_gbr", dz_gbr)):
        dh1 = _mm(dz, W[key], "nt", F32, add=dh1, name=key + "_dx", tk=896 if key == "in_small" else 1024)
        dW[key] = _mm(h1, dz, "tn", F32, name=key + "_dw", tn=896 if key == "in_small" else 1024)
    dx, dg_mix = _norm_bwd(x2d, g_mix, dh1, dx1, T=T, tm=tm, name="norm_mix_bwd")

    dsmall = {"norm_mix_g": dg_mix, "gdn_a_log": d_alog[:, :HEADS], "gdn_dt_bias": d_dtb[:, :HEADS], "gdn_norm_g": dg_gdn,
              "mla_q_norm_g": dg_q, "mla_kv_norm_g": dg_kv, "norm_ffn_g": dg_ffn, "norm_final_g": dg_fin}
    return loss_v[0, 0], dx.reshape(B, S, D_MODEL), dW, d_conv_qkv, d_conv_ffn, dsmall


def kernel(x, norm_mix_g, w_in, conv_qkv_w, gdn_a_log, gdn_dt_bias, gdn_norm_g, mla_q_norm_g, w_uq, mla_kv_norm_g, w_ukv, w_o_gdn, w_o_mla, w_out, norm_ffn_g, w_up, conv_ffn_w, w_down, norm_final_g, loss_target, m_norm_mix_g, m_w_in, m_conv_qkv_w, m_gdn_a_log, m_gdn_dt_bias, m_gdn_norm_g, m_mla_q_norm_g, m_w_uq, m_mla_kv_norm_g, m_w_ukv, m_w_o_gdn, m_w_o_mla, m_w_out, m_norm_ffn_g, m_w_up, m_conv_ffn_w, m_w_down, m_norm_final_g, v_norm_mix_g, v_w_in, v_conv_qkv_w, v_gdn_a_log, v_gdn_dt_bias, v_gdn_norm_g, v_mla_q_norm_g, v_w_uq, v_mla_kv_norm_g, v_w_ukv, v_w_o_gdn, v_w_o_mla, v_w_out, v_norm_ffn_g, v_w_up, v_conv_ffn_w, v_w_down, v_norm_final_g):
    w = dict(norm_mix_g=norm_mix_g, w_in=w_in, conv_qkv_w=conv_qkv_w, gdn_a_log=gdn_a_log, gdn_dt_bias=gdn_dt_bias,
             gdn_norm_g=gdn_norm_g, mla_q_norm_g=mla_q_norm_g, w_uq=w_uq, mla_kv_norm_g=mla_kv_norm_g, w_ukv=w_ukv,
             w_o_gdn=w_o_gdn, w_o_mla=w_o_mla, w_out=w_out, norm_ffn_g=norm_ffn_g, w_up=w_up, conv_ffn_w=conv_ffn_w,
             w_down=w_down, norm_final_g=norm_final_g)
    m = dict(norm_mix_g=m_norm_mix_g, w_in=m_w_in, conv_qkv_w=m_conv_qkv_w, gdn_a_log=m_gdn_a_log, gdn_dt_bias=m_gdn_dt_bias,
             gdn_norm_g=m_gdn_norm_g, mla_q_norm_g=m_mla_q_norm_g, w_uq=m_w_uq, mla_kv_norm_g=m_mla_kv_norm_g, w_ukv=m_w_ukv,
             w_o_gdn=m_w_o_gdn, w_o_mla=m_w_o_mla, w_out=m_w_out, norm_ffn_g=m_norm_ffn_g, w_up=m_w_up,
             conv_ffn_w=m_conv_ffn_w, w_down=m_w_down, norm_final_g=m_norm_final_g)
    v = dict(norm_mix_g=v_norm_mix_g, w_in=v_w_in, conv_qkv_w=v_conv_qkv_w, gdn_a_log=v_gdn_a_log, gdn_dt_bias=v_gdn_dt_bias,
             gdn_norm_g=v_gdn_norm_g, mla_q_norm_g=v_mla_q_norm_g, w_uq=v_w_uq, mla_kv_norm_g=v_mla_kv_norm_g, w_ukv=v_w_ukv,
             w_o_gdn=v_w_o_gdn, w_o_mla=v_w_o_mla, w_out=v_w_out, norm_ffn_g=v_norm_ffn_g, w_up=v_w_up,
             conv_ffn_w=v_conv_ffn_w, w_down=v_w_down, norm_final_g=v_norm_final_g)
    slab_names = ("A", "Q", "C", "V")
    big_names = ("w_in", "w_up", "w_uq", "w_ukv", "w_o_gdn", "w_o_mla", "w_out", "w_down", "conv_qkv_w", "conv_ffn_w")
    small_names = [n for n, _ in SMALL]
    small_shapes = {n: w[n].shape for n in small_names}
    local2d = lambda d: {n: d[n][0] for n in big_names}

    w_slabs = _slabs(local2d(w), F32)
    a_bf = w_slabs["A"].astype(BF16)
    first = _allgather_async([a_bf[:1024]], name="allgather_w_in", collective_id=1)
    second = _allgather_async([w_slabs["Q"].astype(BF16), w_slabs["V"]], name="allgather_mixers", collective_id=2)
    third = _allgather_async([a_bf[1024:], w_slabs["C"].astype(BF16)], name="allgather_ffn_out", collective_id=3)
    gathered = {k: g.reshape(N_DEV, -1, g.shape[1])
                for k, g in zip(("A_in", "Q", "V", "A_up", "C"), first + second + third)}
    W, conv_qkv_full, conv_ffn_full = _layout_weights(gathered)

    def late_weights(tie):
        tie, a_up, c_all = lax.optimization_barrier((tie, gathered["A_up"], gathered["C"]))
        return tie, _layout_late(a_up, c_all)

    loss_local, dx, dW, d_conv_qkv, d_conv_ffn, dsmall = _local_step(
        x, loss_target, W, late_weights, conv_qkv_full, conv_ffn_full, {n: w[n] for n in small_names})

    g_send = _send_slabs(dW, d_conv_qkv, d_conv_ffn)
    recv = _alltoall_async([g_send[k] for k in slab_names], name="alltoall_grads", collective_id=0)
    small_parts = _alltoall_async([jnp.tile(_pack_small(dsmall, loss_local), (N_DEV, 1))], name="alltoall_small_grads",
                                  collective_id=4)[0]
    m_slabs, v_slabs = _slabs(local2d(m), F32), _slabs(local2d(v), F32)
    upd = {k: _reduce_adam(r, w_slabs[k], m_slabs[k], v_slabs[k], tr=SLAB_TR[k], name="adam_" + k)
           for k, r in zip(slab_names, recv)}
    upd_small = _reduce_adam(small_parts, _pack_small({n: w[n] for n in small_names}), _pack_small({n: m[n] for n in small_names}),
                             _pack_small({n: v[n] for n in small_names}), tr=SMALL_ROWS, name="adam_small")

    loss = upd_small[0].reshape(-1)[LOSS_SLOT]
    groups = []
    for i in range(4):
        merged = {**_unslab({k: upd[k][i] for k in slab_names}), **_unpack_small(upd_small[i], small_shapes)}
        groups.append([merged[n] for n in WEIGHT_ORDER])
    return (loss, dx, *groups[0], *groups[1], *groups[2], *groups[3])
```

```python
import functools
import math

import numpy as np
import jax
import jax.numpy as jnp
from jax import lax
from jax.experimental import pallas as pl
from jax.experimental.pallas import tpu as pltpu
from jax.experimental.pallas import tpu_sc as plsc

F32 = jnp.float32
BF16 = jnp.bfloat16

D_MODEL = 1024
HEADS = 8
HD = 128
GDN_CONV = 4
CHUNK = 64
Q_RANK = 384
KV_RANK = 256
ROPE = 64
ROPE_THETA = 10000.0
D_FF = 2816
FFN_CONV = 3
EPS = 1e-6
SM_SCALE = (HD + ROPE) ** -0.5
N_DEV = 8

ADAM_LR, ADAM_B1, ADAM_B2, ADAM_EPS, ADAM_WD, ADAM_STEP = 0.001, 0.9, 0.999, 1e-08, 0.01, 10

LANES = 128
SUBLANES = 8
HALO = 2 * SUBLANES
VMEM_LIMIT = 56 * 1024 * 1024
HI = lax.Precision.HIGHEST
TRI_PRECISION = None

NN = (((1,), (0,)), ((), ()))
NT = (((1,), (1,)), ((), ()))
TN = (((0,), (0,)), ((), ()))


def _dot(a, b, dims=NN, precision=None):
    return lax.dot_general(a, b, dims, precision=precision, preferred_element_type=F32)


def _pick(dim, target, align):
    best = None
    for t in range(align, min(dim, target) + 1, align):
        if dim % t == 0:
            best = t
    return dim if best is None else best


def _call(body, ins, outs, grid, *, name, scratch=(), semantics=None):
    n_in, n_out = len(ins), len(outs)

    def kern(*refs):
        body(refs[:n_in], refs[n_in:n_in + n_out], refs[n_in + n_out:])

    res = pl.pallas_call(
        kern,
        grid=grid,
        in_specs=[pl.BlockSpec(bs, im) for _, bs, im in ins],
        out_specs=[pl.BlockSpec(bs, im) for _, _, bs, im in outs],
        out_shape=[jax.ShapeDtypeStruct(s, d) for s, d, _, _ in outs],
        scratch_shapes=list(scratch),
        name=name,
        compiler_params=pltpu.CompilerParams(
            dimension_semantics=semantics or ("arbitrary",) * len(grid), vmem_limit_bytes=VMEM_LIMIT),
    )(*[a for a, _, _ in ins])
    return res


def _mm(a, b, mode, out_dtype, *, name, add=None, tm=1408, tn=1408, tk=1408):
    if mode == "nn":
        (M, K), (K2, N) = a.shape, b.shape
    elif mode == "nt":
        (M, K), (N, K2) = a.shape, b.shape
    else:
        (K, M), (K2, N) = a.shape, b.shape
    assert K == K2, (a.shape, b.shape, mode)
    tm = _pick(M, tm, LANES if mode == "tn" else 16)
    tn = _pick(N, tn, LANES)
    tk = _pick(K, tk, 16 if mode == "tn" else LANES)
    nk = K // tk
    dims = {"nn": NN, "nt": NT, "tn": TN}[mode]
    if mode == "nn":
        a_spec, b_spec = ((tm, tk), lambda i, j, k: (i, k)), ((tk, tn), lambda i, j, k: (k, j))
    elif mode == "nt":
        a_spec, b_spec = ((tm, tk), lambda i, j, k: (i, k)), ((tn, tk), lambda i, j, k: (j, k))
    else:
        a_spec, b_spec = ((tk, tm), lambda i, j, k: (k, i)), ((tk, tn), lambda i, j, k: (k, j))
    ins = [(a,) + a_spec, (b,) + b_spec]
    if add is not None:
        ins.append((add, (tm, tn), lambda i, j, k: (i, j)))
    outs = [((M, N), out_dtype, (tm, tn), lambda i, j, k: (i, j))]

    def body(in_refs, out_refs, scr):
        prod = _dot(in_refs[0][...].astype(BF16), in_refs[1][...].astype(BF16), dims)

        def finish(r):
            if add is not None:
                r = r + in_refs[2][...].astype(F32)
            out_refs[0][...] = r.astype(out_dtype)

        if nk == 1:
            finish(prod)
            return
        k = pl.program_id(2)
        acc = scr[0]

        @pl.when(k == 0)
        def _():
            acc[...] = prod

        @pl.when(k > 0)
        def _():
            acc[...] += prod

        @pl.when(k == nk - 1)
        def _():
            finish(acc[...])

    return _call(body, ins, outs, (M // tm, N // tn, nk), name=name,
                 scratch=[pltpu.VMEM((tm, tn), F32)] if nk > 1 else [],
                 semantics=("parallel", "parallel", "arbitrary"))[0]


def _row_call(fn, rows, consts, out_rows, out_accs=(), *, T, tm, name):
    nt = T // tm
    ins = []
    for r in rows:
        ins.append(r if isinstance(r, tuple) else (r, (tm, r.shape[1]), lambda i: (i, 0)))
    for c in consts:
        ins.append((c, c.shape, lambda i, nd=c.ndim: (0,) * nd))
    outs = []
    for o in out_rows:
        outs.append(((T, o[0]), o[1], (tm, o[0]), lambda i: (i, 0)) if len(o) == 2 else o)
    for shp, dt in out_accs:
        outs.append((shp, dt, shp, lambda i, nd=len(shp): (0,) * nd))
    n_r = len(out_rows)

    def body(in_refs, out_refs, _):
        i = pl.program_id(0)
        vals = fn(*[r[...] for r in in_refs])
        for o_ref, v in zip(out_refs[:n_r], vals[:n_r]):
            o_ref[...] = v.astype(o_ref.dtype)
        for o_ref, v in zip(out_refs[n_r:], vals[n_r:]):
            @pl.when(i == 0)
            def _(o_ref=o_ref):
                o_ref[...] = jnp.zeros_like(o_ref)

            o_ref[...] += v.astype(o_ref.dtype)

    return _call(body, ins, outs, (nt,), name=name)


def _rms(x, g):
    return x * lax.rsqrt(jnp.mean(x * x, axis=-1, keepdims=True) + EPS) * g


def _norm_fwd(x, g, *, T, tm, name):
    return _row_call(lambda xt, gt: (_rms(xt, gt),), [x], [g], [(x.shape[1], BF16)], T=T, tm=tm, name=name)[0]


def _norm_bwd(x, g, dh, dres, *, T, tm, name):
    def fn(xt, dht, drt, gt):
        _, vjp = jax.vjp(_rms, xt, gt)
        dx, dg = vjp(dht)
        return drt + dx, dg

    return _row_call(fn, [x, dh, dres], [g], [(x.shape[1], F32)], [(g.shape, F32)], T=T, tm=tm, name=name)


def _rows16(c):
    return lax.broadcasted_iota(jnp.int32, (HALO, c), 0)


@functools.lru_cache(maxsize=None)
def _shift_fn(j):
    @jax.custom_vjp
    def shift(x, halo):
        xr = pltpu.roll(x, j, 0)
        top = jnp.where(_rows16(x.shape[1]) < j, pltpu.roll(halo, j, 0), xr[:HALO])
        return jnp.concatenate([top, xr[HALO:]], axis=0)

    def fwd(x, halo):
        return shift(x, halo), None

    def bwd(_, dy):
        tm, c = dy.shape
        keep = _rows16(c) >= HALO - j
        dxr = pltpu.roll(dy, tm - j, 0)
        dx = jnp.concatenate([dxr[:tm - HALO], jnp.where(keep, 0.0, dxr[tm - HALO:])], axis=0)
        dhalo = jnp.where(keep, pltpu.roll(dy[:HALO], HALO - j, 0), 0.0)
        return dx, dhalo

    shift.defvjp(fwd, bwd)
    return shift


def _dwconv(tail, x, w):
    K = w.shape[0]
    acc = w[K - 1:K, :] * x
    for k in range(K - 1):
        acc = acc + w[k:k + 1, :] * _shift_fn(K - 1 - k)(x, tail)
    return acc


STRIP = 64


def _conv_fwd(fn, xs, ws, out_c, out_dtype, *, T, S, tm, cb, ncb, name):
    nt, tps, hb = T // tm, S // tm, tm // HALO
    ins = []
    for arr, off in xs:
        ins.append((arr, (tm, cb), lambda j, i, off=off: (i, off + j)))
        ins.append((arr, (HALO, cb), lambda j, i, off=off: (jnp.maximum(i * hb - 1, 0), off + j)))
    for arr, off in ws:
        ins.append((arr, (arr.shape[0], cb), lambda j, i, off=off: (0, off + j)))
    outs = [((T, out_c), out_dtype, (tm, cb), lambda j, i: (i, j))]
    nx = len(xs)

    def body(in_refs, out_refs, _):
        j, i = pl.program_id(0), pl.program_id(1)
        first = (i % tps) == 0
        wts = [r[...] for r in in_refs[2 * nx:]]
        for r in range(0, tm, STRIP):
            xts = [in_refs[2 * m][r:r + STRIP, :].astype(F32) for m in range(nx)]
            if r == 0:
                tails = [jnp.where(first, 0.0, in_refs[2 * m + 1][...].astype(F32)) for m in range(nx)]
            else:
                tails = [in_refs[2 * m][r - HALO:r, :].astype(F32) for m in range(nx)]
            out_refs[0][r:r + STRIP, :] = fn(j, tails, xts, wts).astype(out_dtype)

    return _call(body, ins, outs, (ncb, nt), name=name)[0]


def _conv_bwd(fn, xs, ws, dout, dx_dtype, *, T, S, tm, cb, ncb, name):
    nt, tps, hb = T // tm, S // tm, tm // HALO
    ins = []
    for arr, off in xs:
        ins.append((arr, (tm, cb), lambda j, i, off=off: (nt - 1 - i, off + j)))
        ins.append((arr, (HALO, cb), lambda j, i, off=off: (jnp.maximum((nt - 1 - i) * hb - 1, 0), off + j)))
    for arr, off in ws:
        ins.append((arr, (arr.shape[0], cb), lambda j, i, off=off: (0, off + j)))
    ins.append((dout, (tm, cb), lambda j, i: (nt - 1 - i, j)))
    nx, nw = len(xs), len(ws)
    outs = [((T, ncb * cb), dx_dtype, (tm, cb), lambda j, i: (nt - 1 - i, j)) for _ in xs]
    outs += [((arr.shape[0], ncb * cb), F32, (arr.shape[0], cb), lambda j, i: (0, j)) for arr, _ in ws]
    scratch = [pltpu.VMEM((HALO, cb), F32) for _ in xs]

    def body(in_refs, out_refs, carry):
        j, i = pl.program_id(0), pl.program_id(1)
        first = ((nt - 1 - i) % tps) == 0
        wts = [ref[...] for ref in in_refs[2 * nx:2 * nx + nw]]

        @pl.when(i == 0)
        def _():
            for c in carry:
                c[...] = jnp.zeros_like(c)

        carried = [c[...] for c in carry]
        dw_sum = None
        for r in reversed(range(0, tm, STRIP)):
            xts = [in_refs[2 * m][r:r + STRIP, :].astype(F32) for m in range(nx)]
            if r == 0:
                tails = [jnp.where(first, 0.0, in_refs[2 * m + 1][...].astype(F32)) for m in range(nx)]
            else:
                tails = [in_refs[2 * m][r - HALO:r, :].astype(F32) for m in range(nx)]
            _, vjp = jax.vjp(lambda tl, xt, wt: fn(j, tl, xt, wt), tails, xts, wts)
            dtails, dxts, dwts = vjp(in_refs[-1][r:r + STRIP, :].astype(F32))
            for m in range(nx):
                pad = jnp.concatenate([jnp.zeros((STRIP - HALO, cb), F32), carried[m]], axis=0)
                out_refs[m][r:r + STRIP, :] = (dxts[m] + pad).astype(dx_dtype)
            carried = [jnp.where(first, 0.0, dt) for dt in dtails] if r == 0 else list(dtails)
            dw_sum = list(dwts) if dw_sum is None else [a + b for a, b in zip(dw_sum, dwts)]
        for m in range(nx):
            carry[m][...] = carried[m]
        for m in range(nw):
            o_ref = out_refs[nx + m]

            @pl.when(i == 0)
            def _(o_ref=o_ref):
                o_ref[...] = jnp.zeros_like(o_ref)

            o_ref[...] += dw_sum[m]

    return _call(body, ins, outs, (ncb, nt), name=name, scratch=scratch)


QKV_CB = 512


def _qkv_fn(j, tails, xts, wts):
    y = jax.nn.silu(_dwconv(tails[0], xts[0], wts[0]))
    scale = jnp.where(j < 1024 // QKV_CB, HD ** -0.5, 1.0)
    parts = []
    for h in range(QKV_CB // HD):
        yh = y[:, h * HD:(h + 1) * HD]
        nh = yh * lax.rsqrt(jnp.sum(yh * yh, axis=-1, keepdims=True) + EPS)
        parts.append(jnp.where(j < 2048 // QKV_CB, nh * scale, yh))
    return jnp.concatenate(parts, axis=1)


def _ffn_fn(j, tails, xts, wts):
    return jax.nn.silu(_dwconv(tails[0], xts[0], wts[0])) * _dwconv(tails[1], xts[1], wts[1])


BNN = (((2,), (1,)), ((0,), (0,)))
BNT = (((2,), (2,)), ((0,), (0,)))
BTN = (((1,), (1,)), ((0,), (0,)))


@jax.custom_vjp
def _tri_inv(L):
    C = L.shape[-1]
    ii = lax.broadcasted_iota(jnp.int32, (C, C), 0)
    jj = lax.broadcasted_iota(jnp.int32, (C, C), 1)
    eye = (ii == jj).astype(F32)
    X = eye - jnp.where((ii >> 1) == (jj >> 1), L, 0.0)
    s = 1
    while (2 << s) <= C:
        E = jnp.where(((ii >> (s + 1)) == (jj >> (s + 1))) & ((ii >> s) != (jj >> s)), L, 0.0)
        X = X - _dot(_dot(X, E, BNN, precision=TRI_PRECISION), X, BNN, precision=TRI_PRECISION)
        s += 1
    return X


def _tri_inv_fwd(L):
    X = _tri_inv(L)
    return X, X


def _tri_inv_bwd(X, dX):
    return (-_dot(_dot(X, dX, BTN, precision=TRI_PRECISION), X, BNT, precision=TRI_PRECISION),)


_tri_inv.defvjp(_tri_inv_fwd, _tri_inv_bwd)


def _gdn_chunk(q, k, v, gc, gr, beta, S):
    C = q.shape[1]
    ii = lax.broadcasted_iota(jnp.int32, (C, C), 0)
    jj = lax.broadcasted_iota(jnp.int32, (C, C), 1)
    lower = ii >= jj
    decay = jnp.where(lower, jnp.exp(jnp.where(lower, gc - gr, 0.0)), 0.0)
    kb, vb = k * beta, v * beta
    L = jnp.where(ii > jj, _dot(kb, k, BNT) * decay, 0.0)
    Tinv = _tri_inv(L)
    eg = jnp.exp(gc)
    u = _dot(Tinv, vb, BNN, precision=TRI_PRECISION)
    w = _dot(Tinv, kb * eg, BNN, precision=TRI_PRECISION)
    a = _dot(q, k, BNT) * decay
    g_last = gc[:, C - 1:C, :]
    kd = k * jnp.exp(g_last - gc)
    v_new = u - _dot(w, S, BNN)
    o = _dot(q * eg, S, BNN) + _dot(a, v_new, BNN)
    S_new = S * jnp.exp(g_last) + _dot(kd, v_new, BTN)
    return o, S_new


def _heads(ref, nb, width=HD):
    return jnp.stack([ref[b, :, h * width:(h + 1) * width].astype(F32) for b in range(nb) for h in range(HEADS)])


def _gdn_fwd(qkvn, gcum, grT, beta, *, B, S):
    N, T = S // CHUNK, B * S
    row = lambda c: (lambda n: (0, n, c))
    qkv3, gc3, b3 = qkvn.reshape(B, S, 3072), gcum.reshape(B, S, LANES), beta.reshape(B, S, LANES)
    gr5 = grT.reshape(B, N, HEADS, 1, CHUNK)
    ins = [(qkv3, (B, CHUNK, 1024), row(0)), (qkv3, (B, CHUNK, 1024), row(1)), (qkv3, (B, CHUNK, 1024), row(2)),
           (gc3, (B, CHUNK, LANES), row(0)), (gr5, (B, 1, HEADS, 1, CHUNK), lambda n: (0, n, 0, 0, 0)),
           (b3, (B, CHUNK, LANES), row(0))]
    outs = [((B, S, 1024), F32, (B, CHUNK, 1024), row(0)),
            ((B, N, HEADS, HD, HD), BF16, (B, 1, HEADS, HD, HD), lambda n: (0, n, 0, 0, 0))]

    def body(in_refs, out_refs, scr):
        q_ref, k_ref, v_ref, gc_ref, gr_ref, b_ref = in_refs
        o_ref, st_ref = out_refs
        S_ref = scr[0]

        @pl.when(pl.program_id(0) == 0)
        def _():
            S_ref[...] = jnp.zeros_like(S_ref)

        S0 = S_ref[...]
        for b in range(B):
            st_ref[b, 0] = S0[b * HEADS:(b + 1) * HEADS].astype(BF16)
        gr = jnp.concatenate([gr_ref[b, 0] for b in range(B)], axis=0)
        o, Sn = _gdn_chunk(_heads(q_ref, B), _heads(k_ref, B), _heads(v_ref, B), _heads(gc_ref, B, 1), gr,
                           _heads(b_ref, B, 1), S0)
        for b in range(B):
            for h in range(HEADS):
                o_ref[b, :, h * HD:(h + 1) * HD] = o[b * HEADS + h]
        S_ref[...] = Sn

    o, st = _call(body, ins, outs, (N,), name="gdn_core_fwd", scratch=[pltpu.VMEM((B * HEADS, HD, HD), F32)])
    return o.reshape(T, 1024), st


def _gdn_bwd(qkvn, gcum, grT, beta, states, do, *, B, S):
    N, T = S // CHUNK, B * S
    row = lambda c: (lambda n: (0, N - 1 - n, c))
    qkv3, gc3, b3 = qkvn.reshape(B, S, 3072), gcum.reshape(B, S, LANES), beta.reshape(B, S, LANES)
    gr5, do3 = grT.reshape(B, N, HEADS, 1, CHUNK), do.reshape(B, S, 1024)
    ins = [(qkv3, (B, CHUNK, 1024), row(0)), (qkv3, (B, CHUNK, 1024), row(1)), (qkv3, (B, CHUNK, 1024), row(2)),
           (gc3, (B, CHUNK, LANES), row(0)), (gr5, (B, 1, HEADS, 1, CHUNK), lambda n: (0, N - 1 - n, 0, 0, 0)),
           (b3, (B, CHUNK, LANES), row(0)),
           (states, (B, 1, HEADS, HD, HD), lambda n: (0, N - 1 - n, 0, 0, 0)), (do3, (B, CHUNK, 1024), row(0))]
    outs = [((B, S, 3072), BF16, (B, CHUNK, 3072), row(0)), ((B, S, LANES), F32, (B, CHUNK, LANES), row(0)),
            ((B, N, HEADS, 1, CHUNK), F32, (B, 1, HEADS, 1, CHUNK), lambda n: (0, N - 1 - n, 0, 0, 0)),
            ((B, S, LANES), F32, (B, CHUNK, LANES), row(0))]

    def body(in_refs, out_refs, scr):
        q_ref, k_ref, v_ref, gc_ref, gr_ref, b_ref, st_ref, do_ref = in_refs
        dqkv_ref, dgc_ref, dgr_ref, db_ref = out_refs
        dS_ref = scr[0]

        @pl.when(pl.program_id(0) == 0)
        def _():
            dS_ref[...] = jnp.zeros_like(dS_ref)

        gr = jnp.concatenate([gr_ref[b, 0] for b in range(B)], axis=0)
        st = jnp.concatenate([st_ref[b, 0] for b in range(B)], axis=0).astype(F32)
        args = (_heads(q_ref, B), _heads(k_ref, B), _heads(v_ref, B), _heads(gc_ref, B, 1), gr, _heads(b_ref, B, 1), st)
        _, vjp = jax.vjp(_gdn_chunk, *args)
        dq, dk, dv, dgc, dgr, db, dS = vjp((_heads(do_ref, B), dS_ref[...]))
        lane = lax.broadcasted_iota(jnp.int32, (CHUNK, LANES), 1)
        for b in range(B):
            dgc_all = jnp.zeros((CHUNK, LANES), F32)
            db_all = jnp.zeros((CHUNK, LANES), F32)
            for h in range(HEADS):
                i = b * HEADS + h
                dqkv_ref[b, :, h * HD:(h + 1) * HD] = dq[i].astype(BF16)
                dqkv_ref[b, :, 1024 + h * HD:1024 + (h + 1) * HD] = dk[i].astype(BF16)
                dqkv_ref[b, :, 2048 + h * HD:2048 + (h + 1) * HD] = dv[i].astype(BF16)
                dgc_all = jnp.where(lane == h, dgc[i], dgc_all)
                db_all = jnp.where(lane == h, db[i], db_all)
            dgc_ref[b] = dgc_all
            db_ref[b] = db_all
            dgr_ref[b, 0] = dgr[b * HEADS:(b + 1) * HEADS]
        dS_ref[...] = dS

    dqkv, dgc, dgr, db = _call(body, ins, outs, (N,), name="gdn_core_bwd",
                               scratch=[pltpu.VMEM((B * HEADS, HD, HD), F32)])
    return dqkv.reshape(T, 3072), dgc.reshape(T, LANES), dgr.reshape(B * N, HEADS, 1, CHUNK), db.reshape(T, LANES)


def _gate_fn(za, zb, alog, dtb):
    tm = za.shape[0]
    g = -jnp.exp(alog) * jax.nn.softplus(za + dtb)
    ii = lax.broadcasted_iota(jnp.int32, (tm, tm), 0)
    jj = lax.broadcasted_iota(jnp.int32, (tm, tm), 1)
    tri = ((ii >= jj) & ((ii >> 6) == (jj >> 6))).astype(F32)
    return _dot(tri, g, precision=HI), jax.nn.sigmoid(zb)


def _scores(qn_ref, qp_ref, kn_ref, kp_ref, diag):
    q = jnp.concatenate([qn_ref[...], qp_ref[...]], axis=1)
    k = jnp.concatenate([kn_ref[...], kp_ref[...]], axis=1)
    s = _dot(q, k, NT) * SM_SCALE
    if diag:
        t = s.shape[0]
        ii = lax.broadcasted_iota(jnp.int32, (t, t), 0)
        jj = lax.broadcasted_iota(jnp.int32, (t, t), 1)
        s = jnp.where(ii >= jj, s, -jnp.inf)
    return s, q, k


HPB = 8
HW = HPB * HD


def _head_refs(refs, hh):
    return [r.at[:, hh * HD:(hh + 1) * HD] for r in refs]


def _flash_fwd(qn, qp, kn, kp, v, *, B, S, t):
    nb, T = S // t, B * S
    qmap = lambda b, h, qi, ki: (b * nb + qi, h)
    kmap = lambda b, h, qi, ki: (b * nb + jnp.minimum(ki, qi), h)
    kpmap = lambda b, h, qi, ki: (b * nb + jnp.minimum(ki, qi), 0)
    ins = [(qn, (t, HW), qmap), (qp, (t, HW), qmap), (kn, (t, HW), kmap), (kp, (t, HD), kpmap), (v, (t, HW), kmap)]
    outs = [((T, 1024), BF16, (t, HW), qmap),
            ((HEADS, T, 1), F32, (HPB, t, 1), lambda b, h, qi, ki: (h, b * nb + qi, 0))]
    scratch = [pltpu.VMEM((HPB, t, 1), F32), pltpu.VMEM((HPB, t, 2 * HD), F32)]

    def body(in_refs, out_refs, scr):
        qn_ref, qp_ref, kn_ref, kp_ref, v_ref = in_refs
        o_ref, lse_ref = out_refs
        m_ref, acc_ref = scr
        qi, ki = pl.program_id(2), pl.program_id(3)

        @pl.when(ki == 0)
        def _():
            m_ref[...] = jnp.full_like(m_ref, -jnp.inf)
            acc_ref[...] = jnp.zeros_like(acc_ref)

        def step(diag):
            for hh in range(HPB):
                qn_h, qp_h, kn_h, v_h = _head_refs((qn_ref, qp_ref, kn_ref, v_ref), hh)
                s, _, _ = _scores(qn_h, qp_h, kn_h, kp_ref, diag)
                m_old = m_ref[hh]
                m_new = jnp.maximum(m_old, jnp.max(s, axis=-1, keepdims=True))
                p = jnp.exp(s - m_new)
                alpha = jnp.exp(m_old - m_new)
                v1 = jnp.concatenate([v_h[...], jnp.ones((t, HD), BF16)], axis=1)
                acc_ref[hh] = alpha * acc_ref[hh] + _dot(p.astype(BF16), v1)
                m_ref[hh] = m_new

        @pl.when(ki < qi)
        def _():
            step(False)

        @pl.when(ki == qi)
        def _():
            step(True)
            for hh in range(HPB):
                o_ref[:, hh * HD:(hh + 1) * HD] = (acc_ref[hh, :, :HD] / acc_ref[hh, :, HD:]).astype(BF16)
                lse_ref[hh] = m_ref[hh] + jnp.log(acc_ref[hh, :, HD:HD + 1])

    return _call(body, ins, outs, (B, HEADS // HPB, nb, nb), name="mla_flash_fwd", scratch=scratch,
                 semantics=("parallel", "parallel", "parallel", "arbitrary"))


def _flash_bwd_dq(qn, qp, kn, kp, v, o, do, lse, *, B, S, t):
    nb, T = S // t, B * S
    qmap = lambda b, h, qi, ki: (b * nb + qi, h)
    kmap = lambda b, h, qi, ki: (b * nb + jnp.minimum(ki, qi), h)
    kpmap = lambda b, h, qi, ki: (b * nb + jnp.minimum(ki, qi), 0)
    ins = [(qn, (t, HW), qmap), (qp, (t, HW), qmap), (kn, (t, HW), kmap), (kp, (t, HD), kpmap), (v, (t, HW), kmap),
           (o, (t, HW), qmap), (do, (t, HW), qmap), (lse, (HPB, t, 1), lambda b, h, qi, ki: (h, b * nb + qi, 0))]
    outs = [((T, 1024), BF16, (t, HW), qmap), ((T, 1024), F32, (t, HW), qmap),
            ((HEADS, T, 1), F32, (HPB, t, 1), lambda b, h, qi, ki: (h, b * nb + qi, 0))]
    scratch = [pltpu.VMEM((HPB, t, 1), F32), pltpu.VMEM((HPB, t, 2 * HD), F32)]

    def body(in_refs, out_refs, scr):
        qn_ref, qp_ref, kn_ref, kp_ref, v_ref, o_ref, do_ref, lse_ref = in_refs
        dqn_ref, dqp_ref, dlo_ref = out_refs
        dl_ref, acc_ref = scr
        qi, ki = pl.program_id(2), pl.program_id(3)

        @pl.when(ki == 0)
        def _():
            for hh in range(HPB):
                o_h, do_h = _head_refs((o_ref, do_ref), hh)
                dl_ref[hh] = jnp.sum(do_h[...].astype(F32) * o_h[...].astype(F32), axis=-1, keepdims=True)
            acc_ref[...] = jnp.zeros_like(acc_ref)

        def step(diag):
            for hh in range(HPB):
                qn_h, qp_h, kn_h, v_h, do_h = _head_refs((qn_ref, qp_ref, kn_ref, v_ref, do_ref), hh)
                s, _, k = _scores(qn_h, qp_h, kn_h, kp_ref, diag)
                p = jnp.exp(s - lse_ref[hh])
                dp = _dot(do_h[...], v_h[...], NT)
                ds = p * (dp - dl_ref[hh]) * SM_SCALE
                acc_ref[hh] += _dot(ds.astype(BF16), k)

        @pl.when(ki < qi)
        def _():
            step(False)

        @pl.when(ki == qi)
        def _():
            step(True)
            for hh in range(HPB):
                dqn_ref[:, hh * HD:(hh + 1) * HD] = acc_ref[hh, :, :HD].astype(BF16)
                dqp_ref[:, hh * HD:(hh + 1) * HD] = acc_ref[hh, :, HD:]
            dlo_ref[...] = dl_ref[...]

    return _call(body, ins, outs, (B, HEADS // HPB, nb, nb), name="mla_flash_bwd_dq", scratch=scratch,
                 semantics=("parallel", "parallel", "parallel", "arbitrary"))


def _flash_bwd_dkv(qn, qp, kn, kp, v, do, lse_t, dl_t, *, B, S, t):
    nb, T = S // t, B * S
    qmap = lambda b, h, ki, qi: (b * nb + jnp.maximum(qi, ki), h)
    kmap = lambda b, h, ki, qi: (b * nb + ki, h)
    tmap = lambda b, h, ki, qi: (h, 0, b * nb + jnp.maximum(qi, ki))
    ins = [(qn, (t, HW), qmap), (qp, (t, HW), qmap), (kn, (t, HW), kmap),
           (kp, (t, HD), lambda b, h, ki, qi: (b * nb + ki, 0)), (v, (t, HW), kmap), (do, (t, HW), qmap),
           (lse_t, (HPB, 1, t), tmap), (dl_t, (HPB, 1, t), tmap)]
    outs = [((T, 1024), BF16, (t, HW), kmap), ((HEADS, T, HD), F32, (HPB, t, HD), lambda b, h, ki, qi: (h, b * nb + ki, 0)),
            ((T, 1024), BF16, (t, HW), kmap)]
    scratch = [pltpu.VMEM((HPB, t, 2 * HD), F32), pltpu.VMEM((HPB, t, HD), F32)]

    def body(in_refs, out_refs, scr):
        qn_ref, qp_ref, kn_ref, kp_ref, v_ref, do_ref, lse_ref, dl_ref = in_refs
        dkn_ref, dkp_ref, dv_ref = out_refs
        dk_acc, dv_acc = scr
        ki, qi = pl.program_id(2), pl.program_id(3)

        @pl.when(qi == 0)
        def _():
            dk_acc[...] = jnp.zeros_like(dk_acc)
            dv_acc[...] = jnp.zeros_like(dv_acc)

        def step(diag):
            for hh in range(HPB):
                qn_h, qp_h, kn_h, v_h, do_h = _head_refs((qn_ref, qp_ref, kn_ref, v_ref, do_ref), hh)
                q = jnp.concatenate([qn_h[...], qp_h[...]], axis=1)
                k = jnp.concatenate([kn_h[...], kp_ref[...]], axis=1)
                st = _dot(k, q, NT) * SM_SCALE
                if diag:
                    ii = lax.broadcasted_iota(jnp.int32, (t, t), 0)
                    jj = lax.broadcasted_iota(jnp.int32, (t, t), 1)
                    st = jnp.where(ii <= jj, st, -jnp.inf)
                do_t = do_h[...]
                pt = jnp.exp(st - lse_ref[hh])
                dst = pt * (_dot(v_h[...], do_t, NT) - dl_ref[hh]) * SM_SCALE
                dv_acc[hh] += _dot(pt.astype(BF16), do_t)
                dk_acc[hh] += _dot(dst.astype(BF16), q)

        @pl.when(qi > ki)
        def _():
            step(False)

        @pl.when(qi == ki)
        def _():
            step(True)

        @pl.when(qi == nb - 1)
        def _():
            for hh in range(HPB):
                dkn_ref[:, hh * HD:(hh + 1) * HD] = dk_acc[hh, :, :HD].astype(BF16)
                dkp_ref[hh] = dk_acc[hh, :, HD:]
                dv_ref[:, hh * HD:(hh + 1) * HD] = dv_acc[hh].astype(BF16)

    return _call(body, ins, outs, (B, HEADS // HPB, nb, nb), name="mla_flash_bwd_dkv", scratch=scratch,
                 semantics=("parallel", "parallel", "parallel", "arbitrary"))


def _allgather_async(shards, *, name, collective_id):
    n_arr = len(shards)
    hbm = pltpu.MemorySpace.HBM
    x_refs = [jax.new_ref(a, memory_space=hbm) for a in shards]
    out_refs = [jax.empty_ref(jax.ShapeDtypeStruct((N_DEV * a.shape[0], a.shape[1]), a.dtype), memory_space=hbm)
                for a in shards]

    @pl.kernel(mesh=plsc.ScalarSubcoreMesh(axis_name="seq", num_cores=1), name=name,
               scratch_types=(pltpu.SemaphoreType.DMA((n_arr, 7)), pltpu.SemaphoreType.DMA((n_arr, 7)),
                              pltpu.SemaphoreType.DMA((n_arr,))),
               compiler_params=pltpu.CompilerParams(collective_id=collective_id))
    def launch(send_sems, recv_sems, local_sems):
        x, y, c = lax.axis_index("x"), lax.axis_index("y"), lax.axis_index("c")
        me, sibling = (x, y, c), (x, y, 1 - c)
        chips = [(1 - x, y), (x, 1 - y), (1 - x, 1 - y)]
        barrier = pltpu.get_barrier_semaphore()
        for p in [sibling] + [(*chip, c) for chip in chips]:
            pl.semaphore_signal(barrier, inc=1, device_id=p, device_id_type=pl.DeviceIdType.MESH)
        pl.semaphore_wait(barrier, 4)

        def rows(a, px, py, pc):
            m_per = shards[a].shape[0]
            return out_refs[a].at[pl.ds((4 * px + 2 * py + pc) * m_per, m_per), :]

        def copy(a, k, block, to, src=None):
            return pltpu.make_async_remote_copy(
                src_ref=rows(a, *block) if src is None else src, dst_ref=rows(a, *block),
                send_sem=send_sems.at[a, k], recv_sem=recv_sems.at[a, k], device_id=to,
                device_id_type=pl.DeviceIdType.MESH)

        mine = [pltpu.make_async_copy(x_refs[a], rows(a, *me), local_sems.at[a]) for a in range(n_arr)]
        for cp in mine:
            cp.start()
        first = []
        for a in range(n_arr):
            first.append(copy(a, 0, me, sibling, src=x_refs[a]))
            first += [copy(a, 1 + j, me, (*chip, c), src=x_refs[a]) for j, chip in enumerate(chips)]
        for cp in first:
            cp.start()
        passed = []
        for j, chip in enumerate(chips):
            for a in range(n_arr):
                copy(a, 1 + j, (*chip, c), me).wait_recv()
                cp = copy(a, 4 + j, (*chip, c), sibling)
                cp.start()
                passed.append(cp)
        for a in range(n_arr):
            copy(a, 0, sibling, me).wait_recv()
        for j, chip in enumerate(chips):
            for a in range(n_arr):
                copy(a, 4 + j, (*chip, 1 - c), me).wait_recv()
        for cp in first + passed:
            cp.wait_send()
        for cp in mine:
            cp.wait()

    launch()
    return [r[...] for r in out_refs]


def _alltoall_async(sends, *, name, collective_id):
    n_arr = len(sends)
    hbm = pltpu.MemorySpace.HBM
    s_refs = [jax.new_ref(a, memory_space=hbm) for a in sends]
    r_refs = [jax.empty_ref(jax.ShapeDtypeStruct(a.shape, a.dtype), memory_space=hbm) for a in sends]

    @pl.kernel(mesh=plsc.ScalarSubcoreMesh(axis_name="seq", num_cores=1), name=name,
               scratch_types=(pltpu.SemaphoreType.DMA((n_arr, 7)), pltpu.SemaphoreType.DMA((n_arr, 7)),
                              pltpu.SemaphoreType.DMA((n_arr,))),
               compiler_params=pltpu.CompilerParams(collective_id=collective_id))
    def launch(send_sems, recv_sems, local_sems):
        x, y, c = lax.axis_index("x"), lax.axis_index("y"), lax.axis_index("c")
        me = 4 * x + 2 * y + c
        peers = [(1 - x if k & 4 else x, 1 - y if k & 2 else y, 1 - c if k & 1 else c) for k in range(1, N_DEV)]
        barrier = pltpu.get_barrier_semaphore()
        for p in peers:
            pl.semaphore_signal(barrier, inc=1, device_id=p, device_id_type=pl.DeviceIdType.MESH)
        pl.semaphore_wait(barrier, N_DEV - 1)

        def rows(ref, a, idx):
            m_per = sends[a].shape[0] // N_DEV
            return ref.at[pl.ds(idx * m_per, m_per), :]

        local = [pltpu.make_async_copy(rows(s_refs[a], a, me), rows(r_refs[a], a, me), local_sems.at[a])
                 for a in range(n_arr)]
        for cp in local:
            cp.start()
        copies = []
        for k, (px, py, pc) in enumerate(peers):
            for a in range(n_arr):
                cp = pltpu.make_async_remote_copy(
                    src_ref=rows(s_refs[a], a, 4 * px + 2 * py + pc), dst_ref=rows(r_refs[a], a, me),
                    send_sem=send_sems.at[a, k], recv_sem=recv_sems.at[a, k],
                    device_id=(px, py, pc), device_id_type=pl.DeviceIdType.MESH)
                cp.start()
                copies.append(cp)
        for cp in copies:
            cp.wait()
        for cp in local:
            cp.wait()

    launch()
    return [r[...] for r in r_refs]


def _reduce_adam(parts, w, m, v, *, tr, name):
    R, C = w.shape
    nR = R // tr
    ins = [(parts, (tr, C), lambda i, s=s: (s * nR + i, 0)) for s in range(N_DEV)]
    ins += [(a, (tr, C), lambda i: (i, 0)) for a in (w, m, v)]
    outs = [((R, C), F32, (tr, C), lambda i: (i, 0)) for _ in range(4)]
    c1 = 1.0 - ADAM_B1 ** ADAM_STEP
    c2 = 1.0 - ADAM_B2 ** ADAM_STEP

    def body(in_refs, out_refs, _):
        g = in_refs[0][...].astype(F32)
        for s in range(1, N_DEV):
            g = g + in_refs[s][...].astype(F32)
        wv, mv, vv = in_refs[8][...], in_refs[9][...], in_refs[10][...]
        mn = ADAM_B1 * mv + (1.0 - ADAM_B1) * g
        vn = ADAM_B2 * vv + (1.0 - ADAM_B2) * (g * g)
        delta = -ADAM_LR * ((mn / c1) / (jnp.sqrt(vn / c2) + ADAM_EPS) + ADAM_WD * wv)
        out_refs[0][...] = g
        out_refs[1][...] = delta
        out_refs[2][...] = mn
        out_refs[3][...] = vn

    return _call(body, ins, outs, (nR,), name=name, semantics=("parallel",))


IN_C, UP_C, UQ_C, QKV_C = 858, 704, 192, 384
A_W, Q_W, V_W = 896, 256, 768
SLAB_TR = {"A_in": 256, "A_up": 256, "Q": 128, "C": 368, "V": 16}
SMALL = [("norm_mix_g", 1024), ("gdn_a_log", 8), ("gdn_dt_bias", 8), ("gdn_norm_g", 128), ("mla_q_norm_g", 384),
         ("mla_kv_norm_g", 256), ("norm_ffn_g", 1024), ("norm_final_g", 1024)]
SMALL_ROWS = 32
WEIGHT_ORDER = ["norm_mix_g", "w_in", "conv_qkv_w", "gdn_a_log", "gdn_dt_bias", "gdn_norm_g", "mla_q_norm_g", "w_uq",
                "mla_kv_norm_g", "w_ukv", "w_o_gdn", "w_o_mla", "w_out", "norm_ffn_g", "w_up", "conv_ffn_w", "w_down",
                "norm_final_g"]


def _padc(w, n):
    return jnp.pad(w, ((0, 0), (0, n - w.shape[1])))


def _padrc(w, r, n):
    return jnp.pad(w, ((0, r - w.shape[0]), (0, n - w.shape[1])))


def _slabs(p, dtype):
    A = jnp.concatenate([_padc(p["w_in"], A_W), _padc(p["w_up"], A_W)], axis=0).astype(dtype)
    Q = jnp.concatenate([_padc(p["w_uq"], Q_W), p["w_ukv"]], axis=0).astype(dtype)
    C = jnp.concatenate([p["w_o_gdn"], p["w_o_mla"], p["w_out"], p["w_down"]], axis=0).astype(dtype)
    V = jnp.concatenate([_padrc(p["conv_qkv_w"], 8, V_W), _padrc(p["conv_ffn_w"], 8, V_W)], axis=0).astype(F32)
    return {"A": A, "Q": Q, "C": C, "V": V}


def _unslab(sl):
    A_in, A_up, Q, C, V = sl["A_in"], sl["A_up"], sl["Q"], sl["C"], sl["V"]
    out = {"w_in": A_in[:, :IN_C], "w_up": A_up[:, :UP_C], "w_uq": Q[:384, :UQ_C], "w_ukv": Q[384:],
           "w_o_gdn": C[0:128], "w_o_mla": C[128:256], "w_out": C[256:384], "w_down": C[384:],
           "conv_qkv_w": V[0:GDN_CONV, :QKV_C], "conv_ffn_w": V[8:8 + FFN_CONV, :UP_C]}
    return {k: a[None] for k, a in out.items()}


def _take_cols(pieces, lo, hi):
    out, off = [], 0
    for arr, a, b in pieces:
        s, e = max(lo, off), min(hi, off + b - a)
        if s < e:
            out.append(arr[:, a + s - off:a + e - off])
        off += b - a
    return out[0] if len(out) == 1 else jnp.concatenate(out, axis=1)


LOSS_SLOT = sum(n for _, n in SMALL)


def _pack_small(d, loss=None):
    flat = jnp.concatenate([d[n].reshape(-1).astype(F32) for n, _ in SMALL]
                           + ([] if loss is None else [loss.reshape(1).astype(F32)]))
    return jnp.pad(flat, (0, SMALL_ROWS * LANES - flat.shape[0])).reshape(SMALL_ROWS, LANES)


def _unpack_small(buf, shapes):
    flat, out, off = buf.reshape(-1), {}, 0
    for name, n in SMALL:
        out[name] = flat[off:off + n].reshape(shapes[name])
        off += n
    return out


def _rot_cols(w):
    h = ROPE // 2
    return jnp.concatenate([-w[:, h:], w[:, :h]], axis=1)


def _unrot_cols(dw):
    h = ROPE // 2
    return jnp.concatenate([dw[:, h:], -dw[:, :h]], axis=1)


IN_SPLITS = [0, 3072, 4096, 4104, 4112, 4496, 4752, 4816, 5840, 6864]


def _layout_late(A_up, C):
    W = {"w_up": jnp.concatenate([A_up[j, :, :UP_C] for j in range(N_DEV)], axis=1),
         "w_o_gdn": C[:, 0:128].reshape(1024, D_MODEL), "w_o_mla": C[:, 128:256].reshape(1024, D_MODEL),
         "w_out": C[:, 256:384].reshape(1024, D_MODEL), "w_down": C[:, 384:].reshape(D_FF, D_MODEL)}
    return {k: v.astype(BF16) for k, v in W.items()}


def _layout_weights(g):
    A_in, Q, V = g["A_in"], g["Q"], g["V"]
    in_pieces = [(A_in[j], 0, IN_C) for j in range(N_DEV)]
    o = IN_SPLITS
    take = lambda lo, hi: _take_cols(in_pieces, lo, hi)
    kpe = take(o[6], o[7])
    W = {
        "in_qkv": take(o[0], o[1]),
        "in_ga": take(o[1], o[2]),
        "in_ab": jnp.concatenate([_padc(take(o[2], o[3]), LANES), _padc(take(o[3], o[4]), LANES)], axis=1),
        "in_small": jnp.concatenate([take(o[4], o[6]), _padc(kpe, LANES), _padc(_rot_cols(kpe), LANES)], axis=1),
        "in_gbr": take(o[7], o[9]),
        "uq_n": jnp.concatenate([Q[j, :384, :HD] for j in range(N_DEV)], axis=1),
        "ukv_k": jnp.concatenate([Q[j, 384:, :HD] for j in range(N_DEV)], axis=1),
        "ukv_v": jnp.concatenate([Q[j, 384:, HD:] for j in range(N_DEV)], axis=1),
    }
    pe = [Q[j, :384, HD:HD + ROPE] for j in range(N_DEV)]
    W["uq_p"] = jnp.concatenate([_padc(p, HD) for p in pe] + [_padc(_rot_cols(p), HD) for p in pe], axis=1)
    conv_qkv = jnp.concatenate([V[j, 0:GDN_CONV, :QKV_C] for j in range(N_DEV)], axis=1)
    conv_ffn = jnp.concatenate([V[j, 8:8 + FFN_CONV, :UP_C] for j in range(N_DEV)], axis=1)
    return {k: v.astype(BF16) for k, v in W.items()}, conv_qkv, conv_ffn


def _full_grads(dW):
    s = dW["in_small"]
    dkpe = s[:, 640:704] + _unrot_cols(s[:, 768:832])
    in_pieces = [(dW["in_qkv"], 0, 3072), (dW["in_ga"], 0, 1024), (dW["in_ab"], 0, 8), (dW["in_ab"], 128, 136),
                 (s, 0, 640), (dkpe, 0, ROPE), (dW["in_gbr"], 0, 2048)]
    pe = []
    for j in range(N_DEV):
        lin = dW["uq_p"][:, j * HD:j * HD + ROPE]
        rot = dW["uq_p"][:, 1024 + j * HD:1024 + j * HD + ROPE]
        pe.append(lin + _unrot_cols(rot))
    return in_pieces, pe


def _send_slabs(dW, d_conv_qkv, d_conv_ffn):
    in_pieces, pe = _full_grads(dW)
    A_in, A_up, Q, V = [], [], [], []
    for j in range(N_DEV):
        A_in.append(_padc(_take_cols(in_pieces, j * IN_C, (j + 1) * IN_C), A_W))
        A_up.append(_padc(dW["w_up"][:, j * UP_C:(j + 1) * UP_C], A_W))
        guq = _padc(jnp.concatenate([dW["uq_n"][:, j * HD:(j + 1) * HD], pe[j]], axis=1), Q_W)
        gukv = jnp.concatenate([dW["ukv_k"][:, j * HD:(j + 1) * HD], dW["ukv_v"][:, j * HD:(j + 1) * HD]], axis=1)
        Q.append(jnp.concatenate([guq, gukv], axis=0))
        V.append(jnp.concatenate([_padrc(d_conv_qkv[:, j * QKV_C:(j + 1) * QKV_C], 8, V_W),
                                  _padrc(d_conv_ffn[:, j * UP_C:(j + 1) * UP_C], 8, V_W)], axis=0))
    C = jnp.concatenate([dW["w_o_gdn"].reshape(N_DEV, 128, D_MODEL), dW["w_o_mla"].reshape(N_DEV, 128, D_MODEL),
                         dW["w_out"].reshape(N_DEV, 128, D_MODEL), dW["w_down"].reshape(N_DEV, 352, D_MODEL)], axis=1)
    return {"A_in": jnp.concatenate(A_in, axis=0).astype(BF16), "A_up": jnp.concatenate(A_up, axis=0).astype(BF16),
            "Q": jnp.concatenate(Q, axis=0).astype(BF16),
            "C": C.reshape(N_DEV * 736, D_MODEL).astype(BF16), "V": jnp.concatenate(V, axis=0)}


def _rope_tables(S):
    half = ROPE // 2
    inv = ROPE_THETA ** (-jnp.arange(half, dtype=F32) / half)
    ang = jnp.arange(S, dtype=F32)[:, None] * inv[None, :]
    cos = jnp.concatenate([jnp.cos(ang), jnp.cos(ang)], axis=1)
    sin = jnp.concatenate([jnp.sin(ang), jnp.sin(ang)], axis=1)
    return _padc(cos, HD), _padc(sin, HD)


def _local_step(x, tgt, W, late_weights, conv_qkv_w, conv_ffn_w, small, tm=None, ta=None):
    B, S, _ = x.shape
    T = B * S
    tm = tm or _pick(S, 512, CHUNK)
    ta = ta or _pick(S, 512, LANES)
    x2d, tgt2d = x.reshape(T, D_MODEL), tgt.reshape(T, D_MODEL)
    row = lambda v: v.reshape(1, -1).astype(F32)
    pad_row = lambda v: _padc(row(v), LANES)
    g_mix, g_ffn, g_fin = row(small["norm_mix_g"]), row(small["norm_ffn_g"]), row(small["norm_final_g"])
    g_gdn, g_q, g_kv = row(small["gdn_norm_g"]), row(small["mla_q_norm_g"]), row(small["mla_kv_norm_g"])
    alog, dtb = pad_row(small["gdn_a_log"]), pad_row(small["gdn_dt_bias"])
    cos, sin = _rope_tables(S)
    tps = S // tm
    tab = lambda a: (a, (tm, HD), lambda i: (i % tps, 0))
    col = lambda a, c, w: (a, (tm, w), lambda i, c=c: (i, c))

    h1 = _norm_fwd(x2d, g_mix, T=T, tm=tm, name="norm_mix_fwd")
    z_qkv = _mm(h1, W["in_qkv"], "nn", BF16, name="in_qkv_fwd")
    z_ga = _mm(h1, W["in_ga"], "nn", BF16, name="in_ga_fwd")
    z_ab = _mm(h1, W["in_ab"], "nn", F32, name="in_ab_fwd")
    z_small = _mm(h1, W["in_small"], "nn", F32, name="in_small_fwd", tn=896)
    z_gbr = _mm(h1, W["in_gbr"], "nn", BF16, name="in_gbr_fwd")

    qkvn = _conv_fwd(_qkv_fn, [(z_qkv, 0)], [(conv_qkv_w, 0)], 3072, BF16, T=T, S=S, tm=tm, cb=QKV_CB,
                     ncb=3072 // QKV_CB, name="gdn_qkv_fwd")
    gcum, beta = _row_call(lambda za, zb, al, db: _gate_fn(za, zb, al, db), [col(z_ab, 0, LANES), col(z_ab, 1, LANES)],
                           [alog, dtb], [(LANES, F32), (LANES, F32)], T=T, tm=tm, name="gdn_gate_fwd")
    grT = gcum[:, :HEADS].reshape(T // CHUNK, CHUNK, HEADS).transpose(0, 2, 1)[:, :, None, :]
    qkvn, late = late_weights(qkvn)
    W = {**W, **late}
    o_gdn, states = _gdn_fwd(qkvn, gcum, grT, beta, B=B, S=S)

    def gdn_out_fn(o, ga, g):
        parts = []
        for h in range(HEADS):
            sl = slice(h * HD, (h + 1) * HD)
            parts.append(_rms(o[:, sl], g) * jax.nn.silu(ga[:, sl].astype(F32)))
        return jnp.concatenate(parts, axis=1)

    oa = _row_call(lambda o, ga, g: (gdn_out_fn(o, ga, g),), [o_gdn, z_ga], [g_gdn], [(1024, BF16)], T=T, tm=tm,
                   name="gdn_out_fwd")[0]

    def mla_prep_fn(zq, zkv, zpl, zpr, c, s, gq, gkv):
        return _rms(zq, gq), _rms(zkv, gkv), zpl * c + zpr * s

    small_cols = [(z_small, (tm, Q_RANK), lambda i: (i, 0)), (z_small, (tm, LANES), lambda i: (i, 3)),
                  (z_small, (tm, LANES), lambda i: (i, 4)), (z_small, (tm, LANES), lambda i: (i, 5)),
                  (z_small, (tm, LANES), lambda i: (i, 6))]

    def mla_prep_fwd(zq, zkv0, zkv1, zpl, zpr, c, s, gq, gkv):
        return mla_prep_fn(zq, jnp.concatenate([zkv0, zkv1], axis=1), zpl, zpr, c, s, gq, gkv)

    cq, ckv, kpe = _row_call(mla_prep_fwd, small_cols + [tab(cos), tab(sin)], [g_q, g_kv],
                             [(Q_RANK, BF16), (KV_RANK, BF16), (HD, BF16)], T=T, tm=tm, name="mla_prep_fwd")
    qn = _mm(cq, W["uq_n"], "nn", BF16, name="uq_n_fwd")
    qpl = _mm(cq, W["uq_p"], "nn", F32, name="uq_p_fwd")
    kn = _mm(ckv, W["ukv_k"], "nn", BF16, name="ukv_k_fwd")
    vb = _mm(ckv, W["ukv_v"], "nn", BF16, name="ukv_v_fwd")

    def qrope_fn(lin, rot, c, s):
        return lin * jnp.tile(c, (1, HEADS)) + rot * jnp.tile(s, (1, HEADS))

    qp = _row_call(lambda lin, rot, c, s: (qrope_fn(lin, rot, c, s),), [col(qpl, 0, 1024), col(qpl, 1, 1024), tab(cos), tab(sin)],
                   [], [(1024, BF16)], T=T, tm=tm, name="q_rope_fwd")[0]
    ob, lse = _flash_fwd(qn, qp, kn, kpe, vb, B=B, S=S, t=ta)

    def merge_fn(ya, yb, ga, gb):
        return jax.nn.sigmoid(ga.astype(F32)) * ya + jax.nn.sigmoid(gb.astype(F32)) * yb

    def merge_fwd(oat, obt, ga, gb, wog, wom):
        ya, yb = _dot(oat, wog), _dot(obt, wom)
        return ya, yb, merge_fn(ya, yb, ga, gb)

    ya, yb, merged = _row_call(merge_fwd, [oa, ob, col(z_gbr, 0, 1024), col(z_gbr, 1, 1024)], [W["w_o_gdn"], W["w_o_mla"]],
                               [(1024, BF16), (1024, BF16), (1024, BF16)], T=T, tm=tm, name="merge_fwd")
    x1 = _mm(merged, W["w_out"], "nn", F32, add=x2d, name="w_out_fwd")

    h2 = _norm_fwd(x1, g_ffn, T=T, tm=tm, name="norm_ffn_fwd")
    up = _mm(h2, W["w_up"], "nn", BF16, name="w_up_fwd")
    FCB = 256
    nfb = D_FF // FCB
    f = _conv_fwd(_ffn_fn, [(up, 0), (up, 2)], [(conv_ffn_w, 0), (conv_ffn_w, 2)], D_FF, BF16, T=T, S=S, tm=tm,
                  cb=D_FF // 2, ncb=2, name="ffn_act_fwd")
    x2 = _mm(f, W["w_down"], "nn", F32, add=x1, name="w_down_fwd", tk=1408)

    def final_fn(xt, tt, g):
        def lossf(xv, gv):
            e = _rms(xv, gv) - tt
            return 0.5 * jnp.sum(jnp.mean(e * e, axis=-1))

        l, vjp = jax.vjp(lossf, xt, g)
        dx, dg = vjp(jnp.ones((), F32))
        return dx, jnp.full((1, LANES), l, F32), dg

    dx2, loss_v, dg_fin = _row_call(final_fn, [x2, tgt2d], [g_fin], [(1024, F32)], [((1, LANES), F32), ((1, 1024), F32)],
                                    T=T, tm=tm, name="loss_head")

    dW = {}
    df = _mm(dx2, W["w_down"], "nt", BF16, name="w_down_dx")
    dW["w_down"] = _mm(f, dx2, "tn", F32, name="w_down_dw")
    dug, duu, dcw_g, dcw_u = _conv_bwd(_ffn_fn, [(up, 0), (up, nfb)], [(conv_ffn_w, 0), (conv_ffn_w, nfb)], df, BF16,
                                       T=T, S=S, tm=tm, cb=FCB, ncb=nfb, name="ffn_act_bwd")
    d_conv_ffn = jnp.concatenate([dcw_g, dcw_u], axis=1)
    wup_g, wup_u = W["w_up"][:, :D_FF], W["w_up"][:, D_FF:]
    dh2 = _mm(dug, wup_g, "nt", F32, name="w_up_dx_g")
    dh2 = _mm(duu, wup_u, "nt", F32, add=dh2, name="w_up_dx_u")
    dW["w_up"] = jnp.concatenate([_mm(h2, dug, "tn", F32, name="w_up_dw_g"), _mm(h2, duu, "tn", F32, name="w_up_dw_u")], axis=1)
    dx1, dg_ffn = _norm_bwd(x1, g_ffn, dh2, dx2, T=T, tm=tm, name="norm_ffn_bwd")

    dmerged = _mm(dx1, W["w_out"], "nt", F32, name="w_out_dx")
    dW["w_out"] = _mm(merged, dx1, "tn", F32, name="w_out_dw")

    def merge_bwd(dm, yat, ybt, ga, gb):
        _, vjp = jax.vjp(merge_fn, yat.astype(F32), ybt.astype(F32), ga, gb)
        return vjp(dm)

    dya, dyb, dgbr_a, dgbr_b = _row_call(merge_bwd, [dmerged, ya, yb, col(z_gbr, 0, 1024), col(z_gbr, 1, 1024)], [],
                                         [(1024, BF16)] * 4, T=T, tm=tm, name="merge_bwd")
    doa = _mm(dya, W["w_o_gdn"], "nt", F32, name="w_o_gdn_dx")
    dob = _mm(dyb, W["w_o_mla"], "nt", BF16, name="w_o_mla_dx")
    dW["w_o_gdn"] = _mm(oa, dya, "tn", F32, name="w_o_gdn_dw")
    dW["w_o_mla"] = _mm(ob, dyb, "tn", F32, name="w_o_mla_dw")

    dqn, dqp, dl = _flash_bwd_dq(qn, qp, kn, kpe, vb, ob, dob, lse, B=B, S=S, t=ta)
    dkn, dkp, dvb = _flash_bwd_dkv(qn, qp, kn, kpe, vb, dob, lse.reshape(HEADS, 1, T), dl.reshape(HEADS, 1, T),
                                   B=B, S=S, t=ta)

    def qrope_bwd(d, c, s):
        return d * jnp.tile(c, (1, HEADS)), d * jnp.tile(s, (1, HEADS))

    dq_lin, dq_rot = _row_call(qrope_bwd, [dqp, tab(cos), tab(sin)], [], [(1024, BF16), (1024, BF16)], T=T, tm=tm,
                               name="q_rope_bwd")
    wp_lin, wp_rot = W["uq_p"][:, :1024], W["uq_p"][:, 1024:]
    dcq = _mm(dqn, W["uq_n"], "nt", F32, name="uq_n_dx")
    dcq = _mm(dq_lin, wp_lin, "nt", F32, add=dcq, name="uq_pl_dx")
    dcq = _mm(dq_rot, wp_rot, "nt", F32, add=dcq, name="uq_pr_dx")
    dW["uq_n"] = _mm(cq, dqn, "tn", F32, name="uq_n_dw")
    dW["uq_p"] = jnp.concatenate([_mm(cq, dq_lin, "tn", F32, name="uq_pl_dw"), _mm(cq, dq_rot, "tn", F32, name="uq_pr_dw")], axis=1)
    dckv = _mm(dkn, W["ukv_k"], "nt", F32, name="ukv_k_dx")
    dckv = _mm(dvb, W["ukv_v"], "nt", F32, add=dckv, name="ukv_v_dx")
    dW["ukv_k"] = _mm(ckv, dkn, "tn", F32, name="ukv_k_dw")
    dW["ukv_v"] = _mm(ckv, dvb, "tn", F32, name="ukv_v_dw")

    def mla_prep_bwd(zq, zkv0, zkv1, zpl, zpr, c, s, dcqt, dckvt, dkpt, gq, gkv):
        zkv = jnp.concatenate([zkv0, zkv1], axis=1)
        _, vjp = jax.vjp(lambda a, b, p, r, g1, g2: mla_prep_fn(a, b, p, r, c, s, g1, g2), zq, zkv, zpl, zpr, gq, gkv)
        dk = dkpt[0]
        for h in range(1, HEADS):
            dk = dk + dkpt[h]
        dzq, dzkv, dzpl, dzpr, dgq, dgkv = vjp((dcqt, dckvt, dk))
        return jnp.concatenate([dzq, dzkv, dzpl, dzpr], axis=1), dgq, dgkv

    dz_small, dg_q, dg_kv = _row_call(
        mla_prep_bwd, small_cols + [tab(cos), tab(sin), dcq, dckv, (dkp, (HEADS, tm, HD), lambda i: (0, i, 0))],
        [g_q, g_kv], [(896, BF16)], [((1, Q_RANK), F32), ((1, KV_RANK), F32)], T=T, tm=tm, name="mla_prep_bwd")

    def gdn_out_bwd(o, ga, dot_, g):
        _, vjp = jax.vjp(gdn_out_fn, o, ga, g)
        return vjp(dot_)

    do_gdn, dz_ga, dg_gdn = _row_call(gdn_out_bwd, [o_gdn, z_ga, doa], [g_gdn], [(1024, F32), (1024, BF16)],
                                      [((1, HD), F32)], T=T, tm=tm, name="gdn_out_bwd")
    dqkvn, dgc, dgrT, dbeta = _gdn_bwd(qkvn, gcum, grT, beta, states, do_gdn, B=B, S=S)
    dgc_tot = dgc + _padc(dgrT[:, :, 0, :].transpose(0, 2, 1).reshape(T, HEADS), LANES)

    def gate_bwd(za, zb, dg, db, al, db_):
        _, vjp = jax.vjp(_gate_fn, za, zb, al, db_)
        return vjp((dg, db))

    dz_a, dz_b, d_alog, d_dtb = _row_call(gate_bwd, [col(z_ab, 0, LANES), col(z_ab, 1, LANES), dgc_tot, dbeta], [alog, dtb],
                                          [(LANES, BF16), (LANES, BF16)], [((1, LANES), F32), ((1, LANES), F32)],
                                          T=T, tm=tm, name="gdn_gate_bwd")
    dz_qkv, d_conv_qkv = _conv_bwd(_qkv_fn, [(z_qkv, 0)], [(conv_qkv_w, 0)], dqkvn, BF16, T=T, S=S, tm=tm, cb=QKV_CB,
                                   ncb=3072 // QKV_CB, name="gdn_qkv_bwd")

    dz_ab = jnp.concatenate([dz_a, dz_b], axis=1)
    dz_gbr = jnp.concatenate([dgbr_a, dgbr_b], axis=1)
    dh1 = None
    for key, dz in (("in_qkv", dz_qkv), ("in_ga", dz_ga), ("in_ab", dz_ab), ("in_small", dz_small), ("in_gbr", dz_gbr)):
        dh1 = _mm(dz, W[key], "nt", F32, add=dh1, name=key + "_dx", tk=896 if key == "in_small" else 1024)
        dW[key] = _mm(h1, dz, "tn", F32, name=key + "_dw", tn=896 if key == "in_small" else 1024)
    dx, dg_mix = _norm_bwd(x2d, g_mix, dh1, dx1, T=T, tm=tm, name="norm_mix_bwd")

    dsmall = {"norm_mix_g": dg_mix, "gdn_a_log": d_alog[:, :HEADS], "gdn_dt_bias": d_dtb[:, :HEADS], "gdn_norm_g": dg_gdn,
              "mla_q_norm_g": dg_q, "mla_kv_norm_g": dg_kv, "norm_ffn_g": dg_ffn, "norm_final_g": dg_fin}
    return loss_v[0, 0], dx.reshape(B, S, D_MODEL), dW, d_conv_qkv, d_conv_ffn, dsmall


def kernel(x, norm_mix_g, w_in, conv_qkv_w, gdn_a_log, gdn_dt_bias, gdn_norm_g, mla_q_norm_g, w_uq, mla_kv_norm_g, w_ukv, w_o_gdn, w_o_mla, w_out, norm_ffn_g, w_up, conv_ffn_w, w_down, norm_final_g, loss_target, m_norm_mix_g, m_w_in, m_conv_qkv_w, m_gdn_a_log, m_gdn_dt_bias, m_gdn_norm_g, m_mla_q_norm_g, m_w_uq, m_mla_kv_norm_g, m_w_ukv, m_w_o_gdn, m_w_o_mla, m_w_out, m_norm_ffn_g, m_w_up, m_conv_ffn_w, m_w_down, m_norm_final_g, v_norm_mix_g, v_w_in, v_conv_qkv_w, v_gdn_a_log, v_gdn_dt_bias, v_gdn_norm_g, v_mla_q_norm_g, v_w_uq, v_mla_kv_norm_g, v_w_ukv, v_w_o_gdn, v_w_o_mla, v_w_out, v_norm_ffn_g, v_w_up, v_conv_ffn_w, v_w_down, v_norm_final_g):
    w = dict(norm_mix_g=norm_mix_g, w_in=w_in, conv_qkv_w=conv_qkv_w, gdn_a_log=gdn_a_log, gdn_dt_bias=gdn_dt_bias,
             gdn_norm_g=gdn_norm_g, mla_q_norm_g=mla_q_norm_g, w_uq=w_uq, mla_kv_norm_g=mla_kv_norm_g, w_ukv=w_ukv,
             w_o_gdn=w_o_gdn, w_o_mla=w_o_mla, w_out=w_out, norm_ffn_g=norm_ffn_g, w_up=w_up, conv_ffn_w=conv_ffn_w,
             w_down=w_down, norm_final_g=norm_final_g)
    m = dict(norm_mix_g=m_norm_mix_g, w_in=m_w_in, conv_qkv_w=m_conv_qkv_w, gdn_a_log=m_gdn_a_log, gdn_dt_bias=m_gdn_dt_bias,
             gdn_norm_g=m_gdn_norm_g, mla_q_norm_g=m_mla_q_norm_g, w_uq=m_w_uq, mla_kv_norm_g=m_mla_kv_norm_g, w_ukv=m_w_ukv,
             w_o_gdn=m_w_o_gdn, w_o_mla=m_w_o_mla, w_out=m_w_out, norm_ffn_g=m_norm_ffn_g, w_up=m_w_up,
             conv_ffn_w=m_conv_ffn_w, w_down=m_w_down, norm_final_g=m_norm_final_g)
    v = dict(norm_mix_g=v_norm_mix_g, w_in=v_w_in, conv_qkv_w=v_conv_qkv_w, gdn_a_log=v_gdn_a_log, gdn_dt_bias=v_gdn_dt_bias,
             gdn_norm_g=v_gdn_norm_g, mla_q_norm_g=v_mla_q_norm_g, w_uq=v_w_uq, mla_kv_norm_g=v_mla_kv_norm_g, w_ukv=v_w_ukv,
             w_o_gdn=v_w_o_gdn, w_o_mla=v_w_o_mla, w_out=v_w_out, norm_ffn_g=v_norm_ffn_g, w_up=v_w_up,
             conv_ffn_w=v_conv_ffn_w, w_down=v_w_down, norm_final_g=v_norm_final_g)
    big_names = ("w_in", "w_up", "w_uq", "w_ukv", "w_o_gdn", "w_o_mla", "w_out", "w_down", "conv_qkv_w", "conv_ffn_w")
    small_names = [n for n, _ in SMALL]
    small_shapes = {n: w[n].shape for n in small_names}
    local2d = lambda d: {n: d[n][0] for n in big_names}

    w_slabs = _slabs(local2d(w), F32)
    a_bf = w_slabs["A"].astype(BF16)
    first = _allgather_async([a_bf[:1024]], name="allgather_w_in", collective_id=1)
    second = _allgather_async([w_slabs["Q"].astype(BF16), w_slabs["V"]], name="allgather_mixers", collective_id=2)
    third = _allgather_async([a_bf[1024:], w_slabs["C"].astype(BF16)], name="allgather_ffn_out", collective_id=3)
    gathered = {k: g.reshape(N_DEV, -1, g.shape[1])
                for k, g in zip(("A_in", "Q", "V", "A_up", "C"), first + second + third)}
    W, conv_qkv_full, conv_ffn_full = _layout_weights(gathered)

    def late_weights(tie):
        tie, a_up, c_all = lax.optimization_barrier((tie, gathered["A_up"], gathered["C"]))
        return tie, _layout_late(a_up, c_all)

    loss_local, dx, dW, d_conv_qkv, d_conv_ffn, dsmall = _local_step(
        x, loss_target, W, late_weights, conv_qkv_full, conv_ffn_full, {n: w[n] for n in small_names})

    g_send = _send_slabs(dW, d_conv_qkv, d_conv_ffn)
    early_names, late_names = ("A_up", "C", "Q"), ("A_in", "V")
    recv = dict(zip(early_names, _alltoall_async([g_send[k] for k in early_names], name="alltoall_grads_early",
                                                 collective_id=0)))
    recv.update(zip(late_names, _alltoall_async([g_send[k] for k in late_names], name="alltoall_grads_late",
                                                collective_id=5)))
    small_parts = _alltoall_async([jnp.tile(_pack_small(dsmall, loss_local), (N_DEV, 1))], name="alltoall_small_grads",
                                  collective_id=4)[0]
    def halves(sl):
        return {"A_in": sl["A"][:1024], "A_up": sl["A"][1024:], "Q": sl["Q"], "C": sl["C"], "V": sl["V"]}

    w_h, m_h, v_h = halves(w_slabs), halves(_slabs(local2d(m), F32)), halves(_slabs(local2d(v), F32))
    upd = {k: _reduce_adam(recv[k], w_h[k], m_h[k], v_h[k], tr=SLAB_TR[k], name="adam_" + k) for k in SLAB_TR}
    upd_small = _reduce_adam(small_parts, _pack_small({n: w[n] for n in small_names}), _pack_small({n: m[n] for n in small_names}),
                             _pack_small({n: v[n] for n in small_names}), tr=SMALL_ROWS, name="adam_small")

    loss = upd_small[0].reshape(-1)[LOSS_SLOT]
    groups = []
    for i in range(4):
        merged = {**_unslab({k: upd[k][i] for k in SLAB_TR}), **_unpack_small(upd_small[i], small_shapes)}
        groups.append([merged[n] for n in WEIGHT_ORDER])
    return (loss, dx, *groups[0], *groups[1], *groups[2], *groups[3])
```

```python
import functools
import math

import numpy as np
import jax
import jax.numpy as jnp
from jax import lax
from jax.experimental import pallas as pl
from jax.experimental.pallas import tpu as pltpu
from jax.experimental.pallas import tpu_sc as plsc

F32 = jnp.float32
BF16 = jnp.bfloat16

D_MODEL = 1024
HEADS = 8
HD = 128
GDN_CONV = 4
CHUNK = 64
Q_RANK = 384
KV_RANK = 256
ROPE = 64
ROPE_THETA = 10000.0
D_FF = 2816
FFN_CONV = 3
EPS = 1e-6
SM_SCALE = (HD + ROPE) ** -0.5
N_DEV = 8

ADAM_LR, ADAM_B1, ADAM_B2, ADAM_EPS, ADAM_WD, ADAM_STEP = 0.001, 0.9, 0.999, 1e-08, 0.01, 10

LANES = 128
SUBLANES = 8
HALO = 2 * SUBLANES
VMEM_LIMIT = 56 * 1024 * 1024
HI = lax.Precision.HIGHEST
TRI_PRECISION = None

NN = (((1,), (0,)), ((), ()))
NT = (((1,), (1,)), ((), ()))
TN = (((0,), (0,)), ((), ()))


def _dot(a, b, dims=NN, precision=None):
    return lax.dot_general(a, b, dims, precision=precision, preferred_element_type=F32)


def _pick(dim, target, align):
    best = None
    for t in range(align, min(dim, target) + 1, align):
        if dim % t == 0:
            best = t
    return dim if best is None else best


def _call(body, ins, outs, grid, *, name, scratch=(), semantics=None):
    n_in, n_out = len(ins), len(outs)

    def kern(*refs):
        body(refs[:n_in], refs[n_in:n_in + n_out], refs[n_in + n_out:])

    res = pl.pallas_call(
        kern,
        grid=grid,
        in_specs=[pl.BlockSpec(bs, im) for _, bs, im in ins],
        out_specs=[pl.BlockSpec(bs, im) for _, _, bs, im in outs],
        out_shape=[jax.ShapeDtypeStruct(s, d) for s, d, _, _ in outs],
        scratch_shapes=list(scratch),
        name=name,
        compiler_params=pltpu.CompilerParams(
            dimension_semantics=semantics or ("arbitrary",) * len(grid), vmem_limit_bytes=VMEM_LIMIT),
    )(*[a for a, _, _ in ins])
    return res


def _mm(a, b, mode, out_dtype, *, name, add=None, tm=1408, tn=1408, tk=1408):
    if mode == "nn":
        (M, K), (K2, N) = a.shape, b.shape
    elif mode == "nt":
        (M, K), (N, K2) = a.shape, b.shape
    else:
        (K, M), (K2, N) = a.shape, b.shape
    assert K == K2, (a.shape, b.shape, mode)
    tm = _pick(M, tm, LANES if mode == "tn" else 16)
    tn = _pick(N, tn, LANES)
    tk = _pick(K, tk, 16 if mode == "tn" else LANES)
    nk = K // tk
    dims = {"nn": NN, "nt": NT, "tn": TN}[mode]
    if mode == "nn":
        a_spec, b_spec = ((tm, tk), lambda i, j, k: (i, k)), ((tk, tn), lambda i, j, k: (k, j))
    elif mode == "nt":
        a_spec, b_spec = ((tm, tk), lambda i, j, k: (i, k)), ((tn, tk), lambda i, j, k: (j, k))
    else:
        a_spec, b_spec = ((tk, tm), lambda i, j, k: (k, i)), ((tk, tn), lambda i, j, k: (k, j))
    ins = [(a,) + a_spec, (b,) + b_spec]
    if add is not None:
        ins.append((add, (tm, tn), lambda i, j, k: (i, j)))
    outs = [((M, N), out_dtype, (tm, tn), lambda i, j, k: (i, j))]

    def body(in_refs, out_refs, scr):
        prod = _dot(in_refs[0][...].astype(BF16), in_refs[1][...].astype(BF16), dims)

        def finish(r):
            if add is not None:
                r = r + in_refs[2][...].astype(F32)
            out_refs[0][...] = r.astype(out_dtype)

        if nk == 1:
            finish(prod)
            return
        k = pl.program_id(2)
        acc = scr[0]

        @pl.when(k == 0)
        def _():
            acc[...] = prod

        @pl.when(k > 0)
        def _():
            acc[...] += prod

        @pl.when(k == nk - 1)
        def _():
            finish(acc[...])

    return _call(body, ins, outs, (M // tm, N // tn, nk), name=name,
                 scratch=[pltpu.VMEM((tm, tn), F32)] if nk > 1 else [],
                 semantics=("parallel", "parallel", "arbitrary"))[0]


def _row_call(fn, rows, consts, out_rows, out_accs=(), *, T, tm, name):
    nt = T // tm
    ins = []
    for r in rows:
        ins.append(r if isinstance(r, tuple) else (r, (tm, r.shape[1]), lambda i: (i, 0)))
    for c in consts:
        ins.append((c, c.shape, lambda i, nd=c.ndim: (0,) * nd))
    outs = []
    for o in out_rows:
        outs.append(((T, o[0]), o[1], (tm, o[0]), lambda i: (i, 0)) if len(o) == 2 else o)
    for shp, dt in out_accs:
        outs.append((shp, dt, shp, lambda i, nd=len(shp): (0,) * nd))
    n_r = len(out_rows)

    def body(in_refs, out_refs, _):
        i = pl.program_id(0)
        vals = fn(*[r[...] for r in in_refs])
        for o_ref, v in zip(out_refs[:n_r], vals[:n_r]):
            o_ref[...] = v.astype(o_ref.dtype)
        for o_ref, v in zip(out_refs[n_r:], vals[n_r:]):
            @pl.when(i == 0)
            def _(o_ref=o_ref):
                o_ref[...] = jnp.zeros_like(o_ref)

            o_ref[...] += v.astype(o_ref.dtype)

    return _call(body, ins, outs, (nt,), name=name)


def _rms(x, g):
    return x * lax.rsqrt(jnp.mean(x * x, axis=-1, keepdims=True) + EPS) * g


def _norm_fwd(x, g, *, T, tm, name):
    return _row_call(lambda xt, gt: (_rms(xt, gt),), [x], [g], [(x.shape[1], BF16)], T=T, tm=tm, name=name)[0]


def _norm_bwd(x, g, dh, dres, *, T, tm, name):
    def fn(xt, dht, drt, gt):
        _, vjp = jax.vjp(_rms, xt, gt)
        dx, dg = vjp(dht)
        return drt + dx, dg

    return _row_call(fn, [x, dh, dres], [g], [(x.shape[1], F32)], [(g.shape, F32)], T=T, tm=tm, name=name)


def _rows16(c):
    return lax.broadcasted_iota(jnp.int32, (HALO, c), 0)


@functools.lru_cache(maxsize=None)
def _shift_fn(j):
    @jax.custom_vjp
    def shift(x, halo):
        xr = pltpu.roll(x, j, 0)
        top = jnp.where(_rows16(x.shape[1]) < j, pltpu.roll(halo, j, 0), xr[:HALO])
        return jnp.concatenate([top, xr[HALO:]], axis=0)

    def fwd(x, halo):
        return shift(x, halo), None

    def bwd(_, dy):
        tm, c = dy.shape
        keep = _rows16(c) >= HALO - j
        dxr = pltpu.roll(dy, tm - j, 0)
        dx = jnp.concatenate([dxr[:tm - HALO], jnp.where(keep, 0.0, dxr[tm - HALO:])], axis=0)
        dhalo = jnp.where(keep, pltpu.roll(dy[:HALO], HALO - j, 0), 0.0)
        return dx, dhalo

    shift.defvjp(fwd, bwd)
    return shift


def _dwconv(tail, x, w):
    K = w.shape[0]
    acc = w[K - 1:K, :] * x
    for k in range(K - 1):
        acc = acc + w[k:k + 1, :] * _shift_fn(K - 1 - k)(x, tail)
    return acc


STRIP = 64


def _conv_fwd(fn, xs, ws, out_c, out_dtype, *, T, S, tm, cb, ncb, name):
    nt, tps, hb = T // tm, S // tm, tm // HALO
    ins = []
    for arr, off in xs:
        ins.append((arr, (tm, cb), lambda j, i, off=off: (i, off + j)))
        ins.append((arr, (HALO, cb), lambda j, i, off=off: (jnp.maximum(i * hb - 1, 0), off + j)))
    for arr, off in ws:
        ins.append((arr, (arr.shape[0], cb), lambda j, i, off=off: (0, off + j)))
    outs = [((T, out_c), out_dtype, (tm, cb), lambda j, i: (i, j))]
    nx = len(xs)

    def body(in_refs, out_refs, _):
        j, i = pl.program_id(0), pl.program_id(1)
        first = (i % tps) == 0
        wts = [r[...] for r in in_refs[2 * nx:]]
        for r in range(0, tm, STRIP):
            xts = [in_refs[2 * m][r:r + STRIP, :].astype(F32) for m in range(nx)]
            if r == 0:
                tails = [jnp.where(first, 0.0, in_refs[2 * m + 1][...].astype(F32)) for m in range(nx)]
            else:
                tails = [in_refs[2 * m][r - HALO:r, :].astype(F32) for m in range(nx)]
            out_refs[0][r:r + STRIP, :] = fn(j, tails, xts, wts).astype(out_dtype)

    return _call(body, ins, outs, (ncb, nt), name=name)[0]


def _conv_bwd(fn, xs, ws, dout, dx_dtype, *, T, S, tm, cb, ncb, name):
    nt, tps, hb = T // tm, S // tm, tm // HALO
    ins = []
    for arr, off in xs:
        ins.append((arr, (tm, cb), lambda j, i, off=off: (nt - 1 - i, off + j)))
        ins.append((arr, (HALO, cb), lambda j, i, off=off: (jnp.maximum((nt - 1 - i) * hb - 1, 0), off + j)))
    for arr, off in ws:
        ins.append((arr, (arr.shape[0], cb), lambda j, i, off=off: (0, off + j)))
    ins.append((dout, (tm, cb), lambda j, i: (nt - 1 - i, j)))
    nx, nw = len(xs), len(ws)
    outs = [((T, ncb * cb), dx_dtype, (tm, cb), lambda j, i: (nt - 1 - i, j)) for _ in xs]
    outs += [((arr.shape[0], ncb * cb), F32, (arr.shape[0], cb), lambda j, i: (0, j)) for arr, _ in ws]
    scratch = [pltpu.VMEM((HALO, cb), F32) for _ in xs]

    def body(in_refs, out_refs, carry):
        j, i = pl.program_id(0), pl.program_id(1)
        first = ((nt - 1 - i) % tps) == 0
        wts = [ref[...] for ref in in_refs[2 * nx:2 * nx + nw]]

        @pl.when(i == 0)
        def _():
            for c in carry:
                c[...] = jnp.zeros_like(c)

        carried = [c[...] for c in carry]
        dw_sum = None
        for r in reversed(range(0, tm, STRIP)):
            xts = [in_refs[2 * m][r:r + STRIP, :].astype(F32) for m in range(nx)]
            if r == 0:
                tails = [jnp.where(first, 0.0, in_refs[2 * m + 1][...].astype(F32)) for m in range(nx)]
            else:
                tails = [in_refs[2 * m][r - HALO:r, :].astype(F32) for m in range(nx)]
            _, vjp = jax.vjp(lambda tl, xt, wt: fn(j, tl, xt, wt), tails, xts, wts)
            dtails, dxts, dwts = vjp(in_refs[-1][r:r + STRIP, :].astype(F32))
            for m in range(nx):
                pad = jnp.concatenate([jnp.zeros((STRIP - HALO, cb), F32), carried[m]], axis=0)
                out_refs[m][r:r + STRIP, :] = (dxts[m] + pad).astype(dx_dtype)
            carried = [jnp.where(first, 0.0, dt) for dt in dtails] if r == 0 else list(dtails)
            dw_sum = list(dwts) if dw_sum is None else [a + b for a, b in zip(dw_sum, dwts)]
        for m in range(nx):
            carry[m][...] = carried[m]
        for m in range(nw):
            o_ref = out_refs[nx + m]

            @pl.when(i == 0)
            def _(o_ref=o_ref):
                o_ref[...] = jnp.zeros_like(o_ref)

            o_ref[...] += dw_sum[m]

    return _call(body, ins, outs, (ncb, nt), name=name, scratch=scratch)


QKV_CB = 512


def _qkv_fn(j, tails, xts, wts):
    y = jax.nn.silu(_dwconv(tails[0], xts[0], wts[0]))
    scale = jnp.where(j < 1024 // QKV_CB, HD ** -0.5, 1.0)
    parts = []
    for h in range(QKV_CB // HD):
        yh = y[:, h * HD:(h + 1) * HD]
        nh = yh * lax.rsqrt(jnp.sum(yh * yh, axis=-1, keepdims=True) + EPS)
        parts.append(jnp.where(j < 2048 // QKV_CB, nh * scale, yh))
    return jnp.concatenate(parts, axis=1)


def _ffn_fn(j, tails, xts, wts):
    return jax.nn.silu(_dwconv(tails[0], xts[0], wts[0])) * _dwconv(tails[1], xts[1], wts[1])


BNN = (((2,), (1,)), ((0,), (0,)))
BNT = (((2,), (2,)), ((0,), (0,)))
BTN = (((1,), (1,)), ((0,), (0,)))


@jax.custom_vjp
def _tri_inv(L):
    C = L.shape[-1]
    ii = lax.broadcasted_iota(jnp.int32, (C, C), 0)
    jj = lax.broadcasted_iota(jnp.int32, (C, C), 1)
    eye = (ii == jj).astype(F32)
    X = eye - jnp.where((ii >> 1) == (jj >> 1), L, 0.0)
    s = 1
    while (2 << s) <= C:
        E = jnp.where(((ii >> (s + 1)) == (jj >> (s + 1))) & ((ii >> s) != (jj >> s)), L, 0.0)
        X = X - _dot(_dot(X, E, BNN, precision=TRI_PRECISION), X, BNN, precision=TRI_PRECISION)
        s += 1
    return X


def _tri_inv_fwd(L):
    X = _tri_inv(L)
    return X, X


def _tri_inv_bwd(X, dX):
    return (-_dot(_dot(X, dX, BTN, precision=TRI_PRECISION), X, BNT, precision=TRI_PRECISION),)


_tri_inv.defvjp(_tri_inv_fwd, _tri_inv_bwd)


def _gdn_chunk(q, k, v, gc, gr, beta, S):
    C = q.shape[1]
    ii = lax.broadcasted_iota(jnp.int32, (C, C), 0)
    jj = lax.broadcasted_iota(jnp.int32, (C, C), 1)
    lower = ii >= jj
    decay = jnp.where(lower, jnp.exp(jnp.where(lower, gc - gr, 0.0)), 0.0)
    kb, vb = k * beta, v * beta
    L = jnp.where(ii > jj, _dot(kb, k, BNT) * decay, 0.0)
    Tinv = _tri_inv(L)
    eg = jnp.exp(gc)
    u = _dot(Tinv, vb, BNN, precision=TRI_PRECISION)
    w = _dot(Tinv, kb * eg, BNN, precision=TRI_PRECISION)
    a = _dot(q, k, BNT) * decay
    g_last = gc[:, C - 1:C, :]
    kd = k * jnp.exp(g_last - gc)
    v_new = u - _dot(w, S, BNN)
    o = _dot(q * eg, S, BNN) + _dot(a, v_new, BNN)
    S_new = S * jnp.exp(g_last) + _dot(kd, v_new, BTN)
    return o, S_new


def _heads(ref, nb, width=HD):
    return jnp.stack([ref[b, :, h * width:(h + 1) * width].astype(F32) for b in range(nb) for h in range(HEADS)])


def _gdn_fwd(qkvn, gcum, grT, beta, *, B, S):
    N, T = S // CHUNK, B * S
    row = lambda c: (lambda n: (0, n, c))
    qkv3, gc3, b3 = qkvn.reshape(B, S, 3072), gcum.reshape(B, S, LANES), beta.reshape(B, S, LANES)
    gr5 = grT.reshape(B, N, HEADS, 1, CHUNK)
    ins = [(qkv3, (B, CHUNK, 1024), row(0)), (qkv3, (B, CHUNK, 1024), row(1)), (qkv3, (B, CHUNK, 1024), row(2)),
           (gc3, (B, CHUNK, LANES), row(0)), (gr5, (B, 1, HEADS, 1, CHUNK), lambda n: (0, n, 0, 0, 0)),
           (b3, (B, CHUNK, LANES), row(0))]
    outs = [((B, S, 1024), F32, (B, CHUNK, 1024), row(0)),
            ((B, N, HEADS, HD, HD), BF16, (B, 1, HEADS, HD, HD), lambda n: (0, n, 0, 0, 0))]

    def body(in_refs, out_refs, scr):
        q_ref, k_ref, v_ref, gc_ref, gr_ref, b_ref = in_refs
        o_ref, st_ref = out_refs
        S_ref = scr[0]

        @pl.when(pl.program_id(0) == 0)
        def _():
            S_ref[...] = jnp.zeros_like(S_ref)

        S0 = S_ref[...]
        for b in range(B):
            st_ref[b, 0] = S0[b * HEADS:(b + 1) * HEADS].astype(BF16)
        gr = jnp.concatenate([gr_ref[b, 0] for b in range(B)], axis=0)
        o, Sn = _gdn_chunk(_heads(q_ref, B), _heads(k_ref, B), _heads(v_ref, B), _heads(gc_ref, B, 1), gr,
                           _heads(b_ref, B, 1), S0)
        for b in range(B):
            for h in range(HEADS):
                o_ref[b, :, h * HD:(h + 1) * HD] = o[b * HEADS + h]
        S_ref[...] = Sn

    o, st = _call(body, ins, outs, (N,), name="gdn_core_fwd", scratch=[pltpu.VMEM((B * HEADS, HD, HD), F32)])
    return o.reshape(T, 1024), st


def _gdn_bwd(qkvn, gcum, grT, beta, states, do, *, B, S):
    N, T = S // CHUNK, B * S
    row = lambda c: (lambda n: (0, N - 1 - n, c))
    qkv3, gc3, b3 = qkvn.reshape(B, S, 3072), gcum.reshape(B, S, LANES), beta.reshape(B, S, LANES)
    gr5, do3 = grT.reshape(B, N, HEADS, 1, CHUNK), do.reshape(B, S, 1024)
    ins = [(qkv3, (B, CHUNK, 1024), row(0)), (qkv3, (B, CHUNK, 1024), row(1)), (qkv3, (B, CHUNK, 1024), row(2)),
           (gc3, (B, CHUNK, LANES), row(0)), (gr5, (B, 1, HEADS, 1, CHUNK), lambda n: (0, N - 1 - n, 0, 0, 0)),
           (b3, (B, CHUNK, LANES), row(0)),
           (states, (B, 1, HEADS, HD, HD), lambda n: (0, N - 1 - n, 0, 0, 0)), (do3, (B, CHUNK, 1024), row(0))]
    outs = [((B, S, 3072), BF16, (B, CHUNK, 3072), row(0)), ((B, S, LANES), F32, (B, CHUNK, LANES), row(0)),
            ((B, N, HEADS, 1, CHUNK), F32, (B, 1, HEADS, 1, CHUNK), lambda n: (0, N - 1 - n, 0, 0, 0)),
            ((B, S, LANES), F32, (B, CHUNK, LANES), row(0))]

    def body(in_refs, out_refs, scr):
        q_ref, k_ref, v_ref, gc_ref, gr_ref, b_ref, st_ref, do_ref = in_refs
        dqkv_ref, dgc_ref, dgr_ref, db_ref = out_refs
        dS_ref = scr[0]

        @pl.when(pl.program_id(0) == 0)
        def _():
            dS_ref[...] = jnp.zeros_like(dS_ref)

        gr = jnp.concatenate([gr_ref[b, 0] for b in range(B)], axis=0)
        st = jnp.concatenate([st_ref[b, 0] for b in range(B)], axis=0).astype(F32)
        args = (_heads(q_ref, B), _heads(k_ref, B), _heads(v_ref, B), _heads(gc_ref, B, 1), gr, _heads(b_ref, B, 1), st)
        _, vjp = jax.vjp(_gdn_chunk, *args)
        dq, dk, dv, dgc, dgr, db, dS = vjp((_heads(do_ref, B), dS_ref[...]))
        lane = lax.broadcasted_iota(jnp.int32, (CHUNK, LANES), 1)
        for b in range(B):
            dgc_all = jnp.zeros((CHUNK, LANES), F32)
            db_all = jnp.zeros((CHUNK, LANES), F32)
            for h in range(HEADS):
                i = b * HEADS + h
                dqkv_ref[b, :, h * HD:(h + 1) * HD] = dq[i].astype(BF16)
                dqkv_ref[b, :, 1024 + h * HD:1024 + (h + 1) * HD] = dk[i].astype(BF16)
                dqkv_ref[b, :, 2048 + h * HD:2048 + (h + 1) * HD] = dv[i].astype(BF16)
                dgc_all = jnp.where(lane == h, dgc[i], dgc_all)
                db_all = jnp.where(lane == h, db[i], db_all)
            dgc_ref[b] = dgc_all
            db_ref[b] = db_all
            dgr_ref[b, 0] = dgr[b * HEADS:(b + 1) * HEADS]
        dS_ref[...] = dS

    dqkv, dgc, dgr, db = _call(body, ins, outs, (N,), name="gdn_core_bwd",
                               scratch=[pltpu.VMEM((B * HEADS, HD, HD), F32)])
    return dqkv.reshape(T, 3072), dgc.reshape(T, LANES), dgr.reshape(B * N, HEADS, 1, CHUNK), db.reshape(T, LANES)


def _gate_fn(za, zb, alog, dtb):
    tm = za.shape[0]
    g = -jnp.exp(alog) * jax.nn.softplus(za + dtb)
    ii = lax.broadcasted_iota(jnp.int32, (tm, tm), 0)
    jj = lax.broadcasted_iota(jnp.int32, (tm, tm), 1)
    tri = ((ii >= jj) & ((ii >> 6) == (jj >> 6))).astype(F32)
    return _dot(tri, g, precision=HI), jax.nn.sigmoid(zb)


def _scores(qn_ref, qp_ref, kn_ref, kp_ref, diag):
    q = jnp.concatenate([qn_ref[...], qp_ref[...]], axis=1)
    k = jnp.concatenate([kn_ref[...], kp_ref[...]], axis=1)
    s = _dot(q, k, NT) * SM_SCALE
    if diag:
        t = s.shape[0]
        ii = lax.broadcasted_iota(jnp.int32, (t, t), 0)
        jj = lax.broadcasted_iota(jnp.int32, (t, t), 1)
        s = jnp.where(ii >= jj, s, -jnp.inf)
    return s, q, k


HPB = 8
HW = HPB * HD


def _head_refs(refs, hh):
    return [r.at[:, hh * HD:(hh + 1) * HD] for r in refs]


def _flash_fwd(qn, qp, kn, kp, v, *, B, S, t):
    nb, T = S // t, B * S
    qmap = lambda b, h, qi, ki: (b * nb + qi, h)
    kmap = lambda b, h, qi, ki: (b * nb + jnp.minimum(ki, qi), h)
    kpmap = lambda b, h, qi, ki: (b * nb + jnp.minimum(ki, qi), 0)
    ins = [(qn, (t, HW), qmap), (qp, (t, HW), qmap), (kn, (t, HW), kmap), (kp, (t, HD), kpmap), (v, (t, HW), kmap)]
    outs = [((T, 1024), BF16, (t, HW), qmap),
            ((HEADS, T, 1), F32, (HPB, t, 1), lambda b, h, qi, ki: (h, b * nb + qi, 0))]
    scratch = [pltpu.VMEM((HPB, t, 1), F32), pltpu.VMEM((HPB, t, 2 * HD), F32)]

    def body(in_refs, out_refs, scr):
        qn_ref, qp_ref, kn_ref, kp_ref, v_ref = in_refs
        o_ref, lse_ref = out_refs
        m_ref, acc_ref = scr
        qi, ki = pl.program_id(2), pl.program_id(3)

        @pl.when(ki == 0)
        def _():
            m_ref[...] = jnp.full_like(m_ref, -jnp.inf)
            acc_ref[...] = jnp.zeros_like(acc_ref)

        def step(diag):
            for hh in range(HPB):
                qn_h, qp_h, kn_h, v_h = _head_refs((qn_ref, qp_ref, kn_ref, v_ref), hh)
                s, _, _ = _scores(qn_h, qp_h, kn_h, kp_ref, diag)
                m_old = m_ref[hh]
                m_new = jnp.maximum(m_old, jnp.max(s, axis=-1, keepdims=True))
                p = jnp.exp(s - m_new)
                alpha = jnp.exp(m_old - m_new)
                v1 = jnp.concatenate([v_h[...], jnp.ones((t, HD), BF16)], axis=1)
                acc_ref[hh] = alpha * acc_ref[hh] + _dot(p.astype(BF16), v1)
                m_ref[hh] = m_new

        @pl.when(ki < qi)
        def _():
            step(False)

        @pl.when(ki == qi)
        def _():
            step(True)
            for hh in range(HPB):
                o_ref[:, hh * HD:(hh + 1) * HD] = (acc_ref[hh, :, :HD] / acc_ref[hh, :, HD:]).astype(BF16)
                lse_ref[hh] = m_ref[hh] + jnp.log(acc_ref[hh, :, HD:HD + 1])

    return _call(body, ins, outs, (B, HEADS // HPB, nb, nb), name="mla_flash_fwd", scratch=scratch,
                 semantics=("parallel", "parallel", "parallel", "arbitrary"))


def _flash_bwd_dq(qn, qp, kn, kp, v, o, do, lse, *, B, S, t):
    nb, T = S // t, B * S
    qmap = lambda b, h, qi, ki: (b * nb + qi, h)
    kmap = lambda b, h, qi, ki: (b * nb + jnp.minimum(ki, qi), h)
    kpmap = lambda b, h, qi, ki: (b * nb + jnp.minimum(ki, qi), 0)
    ins = [(qn, (t, HW), qmap), (qp, (t, HW), qmap), (kn, (t, HW), kmap), (kp, (t, HD), kpmap), (v, (t, HW), kmap),
           (o, (t, HW), qmap), (do, (t, HW), qmap), (lse, (HPB, t, 1), lambda b, h, qi, ki: (h, b * nb + qi, 0))]
    outs = [((T, 1024), BF16, (t, HW), qmap), ((T, 1024), F32, (t, HW), qmap),
            ((HEADS, T, 1), F32, (HPB, t, 1), lambda b, h, qi, ki: (h, b * nb + qi, 0))]
    scratch = [pltpu.VMEM((HPB, t, 1), F32), pltpu.VMEM((HPB, t, 2 * HD), F32)]

    def body(in_refs, out_refs, scr):
        qn_ref, qp_ref, kn_ref, kp_ref, v_ref, o_ref, do_ref, lse_ref = in_refs
        dqn_ref, dqp_ref, dlo_ref = out_refs
        dl_ref, acc_ref = scr
        qi, ki = pl.program_id(2), pl.program_id(3)

        @pl.when(ki == 0)
        def _():
            for hh in range(HPB):
                o_h, do_h = _head_refs((o_ref, do_ref), hh)
                dl_ref[hh] = jnp.sum(do_h[...].astype(F32) * o_h[...].astype(F32), axis=-1, keepdims=True)
            acc_ref[...] = jnp.zeros_like(acc_ref)

        def step(diag):
            for hh in range(HPB):
                qn_h, qp_h, kn_h, v_h, do_h = _head_refs((qn_ref, qp_ref, kn_ref, v_ref, do_ref), hh)
                s, _, k = _scores(qn_h, qp_h, kn_h, kp_ref, diag)
                p = jnp.exp(s - lse_ref[hh])
                dp = _dot(do_h[...], v_h[...], NT)
                ds = p * (dp - dl_ref[hh]) * SM_SCALE
                acc_ref[hh] += _dot(ds.astype(BF16), k)

        @pl.when(ki < qi)
        def _():
            step(False)

        @pl.when(ki == qi)
        def _():
            step(True)
            for hh in range(HPB):
                dqn_ref[:, hh * HD:(hh + 1) * HD] = acc_ref[hh, :, :HD].astype(BF16)
                dqp_ref[:, hh * HD:(hh + 1) * HD] = acc_ref[hh, :, HD:]
            dlo_ref[...] = dl_ref[...]

    return _call(body, ins, outs, (B, HEADS // HPB, nb, nb), name="mla_flash_bwd_dq", scratch=scratch,
                 semantics=("parallel", "parallel", "parallel", "arbitrary"))


def _flash_bwd_dkv(qn, qp, kn, kp, v, do, lse_t, dl_t, *, B, S, t):
    nb, T = S // t, B * S
    qmap = lambda b, h, ki, qi: (b * nb + jnp.maximum(qi, ki), h)
    kmap = lambda b, h, ki, qi: (b * nb + ki, h)
    tmap = lambda b, h, ki, qi: (h, 0, b * nb + jnp.maximum(qi, ki))
    ins = [(qn, (t, HW), qmap), (qp, (t, HW), qmap), (kn, (t, HW), kmap),
           (kp, (t, HD), lambda b, h, ki, qi: (b * nb + ki, 0)), (v, (t, HW), kmap), (do, (t, HW), qmap),
           (lse_t, (HPB, 1, t), tmap), (dl_t, (HPB, 1, t), tmap)]
    outs = [((T, 1024), BF16, (t, HW), kmap), ((HEADS, T, HD), F32, (HPB, t, HD), lambda b, h, ki, qi: (h, b * nb + ki, 0)),
            ((T, 1024), BF16, (t, HW), kmap)]
    scratch = [pltpu.VMEM((HPB, t, 2 * HD), F32), pltpu.VMEM((HPB, t, HD), F32)]

    def body(in_refs, out_refs, scr):
        qn_ref, qp_ref, kn_ref, kp_ref, v_ref, do_ref, lse_ref, dl_ref = in_refs
        dkn_ref, dkp_ref, dv_ref = out_refs
        dk_acc, dv_acc = scr
        ki, qi = pl.program_id(2), pl.program_id(3)

        @pl.when(qi == 0)
        def _():
            dk_acc[...] = jnp.zeros_like(dk_acc)
            dv_acc[...] = jnp.zeros_like(dv_acc)

        def step(diag):
            for hh in range(HPB):
                qn_h, qp_h, kn_h, v_h, do_h = _head_refs((qn_ref, qp_ref, kn_ref, v_ref, do_ref), hh)
                q = jnp.concatenate([qn_h[...], qp_h[...]], axis=1)
                k = jnp.concatenate([kn_h[...], kp_ref[...]], axis=1)
                st = _dot(k, q, NT) * SM_SCALE
                if diag:
                    ii = lax.broadcasted_iota(jnp.int32, (t, t), 0)
                    jj = lax.broadcasted_iota(jnp.int32, (t, t), 1)
                    st = jnp.where(ii <= jj, st, -jnp.inf)
                do_t = do_h[...]
                pt = jnp.exp(st - lse_ref[hh])
                dst = pt * (_dot(v_h[...], do_t, NT) - dl_ref[hh]) * SM_SCALE
                dv_acc[hh] += _dot(pt.astype(BF16), do_t)
                dk_acc[hh] += _dot(dst.astype(BF16), q)

        @pl.when(qi > ki)
        def _():
            step(False)

        @pl.when(qi == ki)
        def _():
            step(True)

        @pl.when(qi == nb - 1)
        def _():
            for hh in range(HPB):
                dkn_ref[:, hh * HD:(hh + 1) * HD] = dk_acc[hh, :, :HD].astype(BF16)
                dkp_ref[hh] = dk_acc[hh, :, HD:]
                dv_ref[:, hh * HD:(hh + 1) * HD] = dv_acc[hh].astype(BF16)

    return _call(body, ins, outs, (B, HEADS // HPB, nb, nb), name="mla_flash_bwd_dkv", scratch=scratch,
                 semantics=("parallel", "parallel", "parallel", "arbitrary"))


def _allgather_async(shards, *, name, collective_id):
    n_arr = len(shards)
    hbm = pltpu.MemorySpace.HBM
    x_refs = [jax.new_ref(a, memory_space=hbm) for a in shards]
    out_refs = [jax.empty_ref(jax.ShapeDtypeStruct((N_DEV * a.shape[0], a.shape[1]), a.dtype), memory_space=hbm)
                for a in shards]

    @pl.kernel(mesh=plsc.ScalarSubcoreMesh(axis_name="seq", num_cores=1), name=name,
               scratch_types=(pltpu.SemaphoreType.DMA((n_arr, 7)), pltpu.SemaphoreType.DMA((n_arr, 7)),
                              pltpu.SemaphoreType.DMA((n_arr,))),
               compiler_params=pltpu.CompilerParams(collective_id=collective_id))
    def launch(send_sems, recv_sems, local_sems):
        x, y, c = lax.axis_index("x"), lax.axis_index("y"), lax.axis_index("c")
        me, sibling = (x, y, c), (x, y, 1 - c)
        chips = [(1 - x, y), (x, 1 - y), (1 - x, 1 - y)]
        barrier = pltpu.get_barrier_semaphore()
        for p in [sibling] + [(*chip, c) for chip in chips]:
            pl.semaphore_signal(barrier, inc=1, device_id=p, device_id_type=pl.DeviceIdType.MESH)
        pl.semaphore_wait(barrier, 4)

        def rows(a, px, py, pc):
            m_per = shards[a].shape[0]
            return out_refs[a].at[pl.ds((4 * px + 2 * py + pc) * m_per, m_per), :]

        def copy(a, k, block, to, src=None):
            return pltpu.make_async_remote_copy(
                src_ref=rows(a, *block) if src is None else src, dst_ref=rows(a, *block),
                send_sem=send_sems.at[a, k], recv_sem=recv_sems.at[a, k], device_id=to,
                device_id_type=pl.DeviceIdType.MESH)

        mine = [pltpu.make_async_copy(x_refs[a], rows(a, *me), local_sems.at[a]) for a in range(n_arr)]
        for cp in mine:
            cp.start()
        first = []
        for a in range(n_arr):
            first.append(copy(a, 0, me, sibling, src=x_refs[a]))
            first += [copy(a, 1 + j, me, (*chip, c), src=x_refs[a]) for j, chip in enumerate(chips)]
        for cp in first:
            cp.start()
        passed = []
        for j, chip in enumerate(chips):
            for a in range(n_arr):
                copy(a, 1 + j, (*chip, c), me).wait_recv()
                cp = copy(a, 4 + j, (*chip, c), sibling)
                cp.start()
                passed.append(cp)
        for a in range(n_arr):
            copy(a, 0, sibling, me).wait_recv()
        for j, chip in enumerate(chips):
            for a in range(n_arr):
                copy(a, 4 + j, (*chip, 1 - c), me).wait_recv()
        for cp in first + passed:
            cp.wait_send()
        for cp in mine:
            cp.wait()

    launch()
    return [r[...] for r in out_refs]


def _alltoall_async(sends, *, name, collective_id):
    n_arr = len(sends)
    hbm = pltpu.MemorySpace.HBM
    s_refs = [jax.new_ref(a, memory_space=hbm) for a in sends]
    r_refs = [jax.empty_ref(jax.ShapeDtypeStruct(a.shape, a.dtype), memory_space=hbm) for a in sends]

    @pl.kernel(mesh=plsc.ScalarSubcoreMesh(axis_name="seq", num_cores=1), name=name,
               scratch_types=(pltpu.SemaphoreType.DMA((n_arr, 7)), pltpu.SemaphoreType.DMA((n_arr, 7)),
                              pltpu.SemaphoreType.DMA((n_arr,))),
               compiler_params=pltpu.CompilerParams(collective_id=collective_id))
    def launch(send_sems, recv_sems, local_sems):
        x, y, c = lax.axis_index("x"), lax.axis_index("y"), lax.axis_index("c")
        me = 4 * x + 2 * y + c
        peers = [(1 - x if k & 4 else x, 1 - y if k & 2 else y, 1 - c if k & 1 else c) for k in range(1, N_DEV)]
        barrier = pltpu.get_barrier_semaphore()
        for p in peers:
            pl.semaphore_signal(barrier, inc=1, device_id=p, device_id_type=pl.DeviceIdType.MESH)
        pl.semaphore_wait(barrier, N_DEV - 1)

        def rows(ref, a, idx):
            m_per = sends[a].shape[0] // N_DEV
            return ref.at[pl.ds(idx * m_per, m_per), :]

        local = [pltpu.make_async_copy(rows(s_refs[a], a, me), rows(r_refs[a], a, me), local_sems.at[a])
                 for a in range(n_arr)]
        for cp in local:
            cp.start()
        copies = []
        for k, (px, py, pc) in enumerate(peers):
            for a in range(n_arr):
                cp = pltpu.make_async_remote_copy(
                    src_ref=rows(s_refs[a], a, 4 * px + 2 * py + pc), dst_ref=rows(r_refs[a], a, me),
                    send_sem=send_sems.at[a, k], recv_sem=recv_sems.at[a, k],
                    device_id=(px, py, pc), device_id_type=pl.DeviceIdType.MESH)
                cp.start()
                copies.append(cp)
        for cp in copies:
            cp.wait()
        for cp in local:
            cp.wait()

    launch()
    return [r[...] for r in r_refs]


def _reduce_adam(parts, w, m, v, *, tr, name):
    R, C = w.shape
    nR = R // tr
    ins = [(parts, (tr, C), lambda i, s=s: (s * nR + i, 0)) for s in range(N_DEV)]
    ins += [(a, (tr, C), lambda i: (i, 0)) for a in (w, m, v)]
    outs = [((R, C), F32, (tr, C), lambda i: (i, 0)) for _ in range(4)]
    c1 = 1.0 - ADAM_B1 ** ADAM_STEP
    c2 = 1.0 - ADAM_B2 ** ADAM_STEP

    def body(in_refs, out_refs, _):
        g = in_refs[0][...].astype(F32)
        for s in range(1, N_DEV):
            g = g + in_refs[s][...].astype(F32)
        wv, mv, vv = in_refs[8][...], in_refs[9][...], in_refs[10][...]
        mn = ADAM_B1 * mv + (1.0 - ADAM_B1) * g
        vn = ADAM_B2 * vv + (1.0 - ADAM_B2) * (g * g)
        delta = -ADAM_LR * ((mn / c1) / (jnp.sqrt(vn / c2) + ADAM_EPS) + ADAM_WD * wv)
        out_refs[0][...] = g
        out_refs[1][...] = delta
        out_refs[2][...] = mn
        out_refs[3][...] = vn

    return _call(body, ins, outs, (nR,), name=name, semantics=("parallel",))


IN_C, UP_C, UQ_C, QKV_C = 858, 704, 192, 384
A_W, Q_W, V_W = 896, 256, 768
SLAB_TR = {"A_in": 256, "A_up": 256, "Q": 128, "C": 368, "V": 16}
SMALL = [("norm_mix_g", 1024), ("gdn_a_log", 8), ("gdn_dt_bias", 8), ("gdn_norm_g", 128), ("mla_q_norm_g", 384),
         ("mla_kv_norm_g", 256), ("norm_ffn_g", 1024), ("norm_final_g", 1024)]
SMALL_ROWS = 32
WEIGHT_ORDER = ["norm_mix_g", "w_in", "conv_qkv_w", "gdn_a_log", "gdn_dt_bias", "gdn_norm_g", "mla_q_norm_g", "w_uq",
                "mla_kv_norm_g", "w_ukv", "w_o_gdn", "w_o_mla", "w_out", "norm_ffn_g", "w_up", "conv_ffn_w", "w_down",
                "norm_final_g"]


def _padc(w, n):
    return jnp.pad(w, ((0, 0), (0, n - w.shape[1])))


def _padrc(w, r, n):
    return jnp.pad(w, ((0, r - w.shape[0]), (0, n - w.shape[1])))


def _slabs(p, dtype):
    A = jnp.concatenate([_padc(p["w_in"], A_W), _padc(p["w_up"], A_W)], axis=0).astype(dtype)
    Q = jnp.concatenate([_padc(p["w_uq"], Q_W), p["w_ukv"]], axis=0).astype(dtype)
    C = jnp.concatenate([p["w_o_gdn"], p["w_o_mla"], p["w_out"], p["w_down"]], axis=0).astype(dtype)
    V = jnp.concatenate([_padrc(p["conv_qkv_w"], 8, V_W), _padrc(p["conv_ffn_w"], 8, V_W)], axis=0).astype(F32)
    return {"A": A, "Q": Q, "C": C, "V": V}


def _unslab(sl):
    A_in, A_up, Q, C, V = sl["A_in"], sl["A_up"], sl["Q"], sl["C"], sl["V"]
    out = {"w_in": A_in[:, :IN_C], "w_up": A_up[:, :UP_C], "w_uq": Q[:384, :UQ_C], "w_ukv": Q[384:],
           "w_o_gdn": C[0:128], "w_o_mla": C[128:256], "w_out": C[256:384], "w_down": C[384:],
           "conv_qkv_w": V[0:GDN_CONV, :QKV_C], "conv_ffn_w": V[8:8 + FFN_CONV, :UP_C]}
    return {k: a[None] for k, a in out.items()}


def _take_cols(pieces, lo, hi):
    out, off = [], 0
    for arr, a, b in pieces:
        s, e = max(lo, off), min(hi, off + b - a)
        if s < e:
            out.append(arr[:, a + s - off:a + e - off])
        off += b - a
    return out[0] if len(out) == 1 else jnp.concatenate(out, axis=1)


LOSS_SLOT = sum(n for _, n in SMALL)


def _pack_small(d, loss=None):
    flat = jnp.concatenate([d[n].reshape(-1).astype(F32) for n, _ in SMALL]
                           + ([] if loss is None else [loss.reshape(1).astype(F32)]))
    return jnp.pad(flat, (0, SMALL_ROWS * LANES - flat.shape[0])).reshape(SMALL_ROWS, LANES)


def _unpack_small(buf, shapes):
    flat, out, off = buf.reshape(-1), {}, 0
    for name, n in SMALL:
        out[name] = flat[off:off + n].reshape(shapes[name])
        off += n
    return out


def _rot_cols(w):
    h = ROPE // 2
    return jnp.concatenate([-w[:, h:], w[:, :h]], axis=1)


def _unrot_cols(dw):
    h = ROPE // 2
    return jnp.concatenate([dw[:, h:], -dw[:, :h]], axis=1)


IN_SPLITS = [0, 3072, 4096, 4104, 4112, 4496, 4752, 4816, 5840, 6864]


def _layout_late(A_up, C):
    W = {"w_up": jnp.concatenate([A_up[j, :, :UP_C] for j in range(N_DEV)], axis=1),
         "w_o_gdn": C[:, 0:128].reshape(1024, D_MODEL), "w_o_mla": C[:, 128:256].reshape(1024, D_MODEL),
         "w_out": C[:, 256:384].reshape(1024, D_MODEL), "w_down": C[:, 384:].reshape(D_FF, D_MODEL)}
    return {k: v.astype(BF16) for k, v in W.items()}


def _layout_weights(g):
    A_in, Q, V = g["A_in"], g["Q"], g["V"]
    in_pieces = [(A_in[j], 0, IN_C) for j in range(N_DEV)]
    o = IN_SPLITS
    take = lambda lo, hi: _take_cols(in_pieces, lo, hi)
    kpe = take(o[6], o[7])
    W = {
        "in_qkv": take(o[0], o[1]),
        "in_ga": take(o[1], o[2]),
        "in_ab": jnp.concatenate([_padc(take(o[2], o[3]), LANES), _padc(take(o[3], o[4]), LANES)], axis=1),
        "in_small": jnp.concatenate([take(o[4], o[6]), _padc(kpe, LANES), _padc(_rot_cols(kpe), LANES)], axis=1),
        "in_gbr": take(o[7], o[9]),
        "uq_n": jnp.concatenate([Q[j, :384, :HD] for j in range(N_DEV)], axis=1),
        "ukv_k": jnp.concatenate([Q[j, 384:, :HD] for j in range(N_DEV)], axis=1),
        "ukv_v": jnp.concatenate([Q[j, 384:, HD:] for j in range(N_DEV)], axis=1),
    }
    pe = [Q[j, :384, HD:HD + ROPE] for j in range(N_DEV)]
    W["uq_p"] = jnp.concatenate([_padc(p, HD) for p in pe] + [_padc(_rot_cols(p), HD) for p in pe], axis=1)
    conv_qkv = jnp.concatenate([V[j, 0:GDN_CONV, :QKV_C] for j in range(N_DEV)], axis=1)
    conv_ffn = jnp.concatenate([V[j, 8:8 + FFN_CONV, :UP_C] for j in range(N_DEV)], axis=1)
    return {k: v.astype(BF16) for k, v in W.items()}, conv_qkv, conv_ffn


def _full_grads(dW):
    s = dW["in_small"]
    dkpe = s[:, 640:704] + _unrot_cols(s[:, 768:832])
    in_pieces = [(dW["in_qkv"], 0, 3072), (dW["in_ga"], 0, 1024), (dW["in_ab"], 0, 8), (dW["in_ab"], 128, 136),
                 (s, 0, 640), (dkpe, 0, ROPE), (dW["in_gbr"], 0, 2048)]
    pe = []
    for j in range(N_DEV):
        lin = dW["uq_p"][:, j * HD:j * HD + ROPE]
        rot = dW["uq_p"][:, 1024 + j * HD:1024 + j * HD + ROPE]
        pe.append(lin + _unrot_cols(rot))
    return in_pieces, pe


def _send_slabs(dW, d_conv_qkv, d_conv_ffn):
    in_pieces, pe = _full_grads(dW)
    A_in, A_up, Q, V = [], [], [], []
    for j in range(N_DEV):
        A_in.append(_padc(_take_cols(in_pieces, j * IN_C, (j + 1) * IN_C), A_W))
        A_up.append(_padc(dW["w_up"][:, j * UP_C:(j + 1) * UP_C], A_W))
        guq = _padc(jnp.concatenate([dW["uq_n"][:, j * HD:(j + 1) * HD], pe[j]], axis=1), Q_W)
        gukv = jnp.concatenate([dW["ukv_k"][:, j * HD:(j + 1) * HD], dW["ukv_v"][:, j * HD:(j + 1) * HD]], axis=1)
        Q.append(jnp.concatenate([guq, gukv], axis=0))
        V.append(jnp.concatenate([_padrc(d_conv_qkv[:, j * QKV_C:(j + 1) * QKV_C], 8, V_W),
                                  _padrc(d_conv_ffn[:, j * UP_C:(j + 1) * UP_C], 8, V_W)], axis=0))
    C = jnp.concatenate([dW["w_o_gdn"].reshape(N_DEV, 128, D_MODEL), dW["w_o_mla"].reshape(N_DEV, 128, D_MODEL),
                         dW["w_out"].reshape(N_DEV, 128, D_MODEL), dW["w_down"].reshape(N_DEV, 352, D_MODEL)], axis=1)
    return {"A_in": jnp.concatenate(A_in, axis=0).astype(BF16), "A_up": jnp.concatenate(A_up, axis=0).astype(BF16),
            "Q": jnp.concatenate(Q, axis=0).astype(BF16),
            "C": C.reshape(N_DEV * 736, D_MODEL).astype(BF16), "V": jnp.concatenate(V, axis=0)}


def _rope_tables(S):
    half = ROPE // 2
    inv = ROPE_THETA ** (-jnp.arange(half, dtype=F32) / half)
    ang = jnp.arange(S, dtype=F32)[:, None] * inv[None, :]
    cos = jnp.concatenate([jnp.cos(ang), jnp.cos(ang)], axis=1)
    sin = jnp.concatenate([jnp.sin(ang), jnp.sin(ang)], axis=1)
    return _padc(cos, HD), _padc(sin, HD)


def _local_step(x, tgt, W, late_weights, conv_qkv_w, conv_ffn_w, small, tm=None, ta=None):
    B, S, _ = x.shape
    T = B * S
    tm = tm or _pick(S, 1024, CHUNK)
    ta = ta or _pick(S, 512, LANES)
    x2d, tgt2d = x.reshape(T, D_MODEL), tgt.reshape(T, D_MODEL)
    row = lambda v: v.reshape(1, -1).astype(F32)
    pad_row = lambda v: _padc(row(v), LANES)
    g_mix, g_ffn, g_fin = row(small["norm_mix_g"]), row(small["norm_ffn_g"]), row(small["norm_final_g"])
    g_gdn, g_q, g_kv = row(small["gdn_norm_g"]), row(small["mla_q_norm_g"]), row(small["mla_kv_norm_g"])
    alog, dtb = pad_row(small["gdn_a_log"]), pad_row(small["gdn_dt_bias"])
    cos, sin = _rope_tables(S)
    tps = S // tm
    tab = lambda a: (a, (tm, HD), lambda i: (i % tps, 0))
    col = lambda a, c, w: (a, (tm, w), lambda i, c=c: (i, c))

    h1 = _norm_fwd(x2d, g_mix, T=T, tm=tm, name="norm_mix_fwd")
    z_qkv = _mm(h1, W["in_qkv"], "nn", BF16, name="in_qkv_fwd")
    z_ga = _mm(h1, W["in_ga"], "nn", BF16, name="in_ga_fwd")
    z_ab = _mm(h1, W["in_ab"], "nn", F32, name="in_ab_fwd")
    z_small = _mm(h1, W["in_small"], "nn", F32, name="in_small_fwd", tn=896)
    z_gbr = _mm(h1, W["in_gbr"], "nn", BF16, name="in_gbr_fwd")

    qkvn = _conv_fwd(_qkv_fn, [(z_qkv, 0)], [(conv_qkv_w, 0)], 3072, BF16, T=T, S=S, tm=tm, cb=QKV_CB,
                     ncb=3072 // QKV_CB, name="gdn_qkv_fwd")
    gcum, beta = _row_call(lambda za, zb, al, db: _gate_fn(za, zb, al, db), [col(z_ab, 0, LANES), col(z_ab, 1, LANES)],
                           [alog, dtb], [(LANES, F32), (LANES, F32)], T=T, tm=tm, name="gdn_gate_fwd")
    grT = gcum[:, :HEADS].reshape(T // CHUNK, CHUNK, HEADS).transpose(0, 2, 1)[:, :, None, :]
    qkvn, late = late_weights(qkvn)
    W = {**W, **late}
    o_gdn, states = _gdn_fwd(qkvn, gcum, grT, beta, B=B, S=S)

    def gdn_out_fn(o, ga, g):
        parts = []
        for h in range(HEADS):
            sl = slice(h * HD, (h + 1) * HD)
            parts.append(_rms(o[:, sl], g) * jax.nn.silu(ga[:, sl].astype(F32)))
        return jnp.concatenate(parts, axis=1)

    oa = _row_call(lambda o, ga, g: (gdn_out_fn(o, ga, g),), [o_gdn, z_ga], [g_gdn], [(1024, BF16)], T=T, tm=tm,
                   name="gdn_out_fwd")[0]

    def mla_prep_fn(zq, zkv, zpl, zpr, c, s, gq, gkv):
        return _rms(zq, gq), _rms(zkv, gkv), zpl * c + zpr * s

    small_cols = [(z_small, (tm, Q_RANK), lambda i: (i, 0)), (z_small, (tm, LANES), lambda i: (i, 3)),
                  (z_small, (tm, LANES), lambda i: (i, 4)), (z_small, (tm, LANES), lambda i: (i, 5)),
                  (z_small, (tm, LANES), lambda i: (i, 6))]

    def mla_prep_fwd(zq, zkv0, zkv1, zpl, zpr, c, s, gq, gkv):
        return mla_prep_fn(zq, jnp.concatenate([zkv0, zkv1], axis=1), zpl, zpr, c, s, gq, gkv)

    cq, ckv, kpe = _row_call(mla_prep_fwd, small_cols + [tab(cos), tab(sin)], [g_q, g_kv],
                             [(Q_RANK, BF16), (KV_RANK, BF16), (HD, BF16)], T=T, tm=tm, name="mla_prep_fwd")
    qn = _mm(cq, W["uq_n"], "nn", BF16, name="uq_n_fwd")
    qpl = _mm(cq, W["uq_p"], "nn", F32, name="uq_p_fwd")
    kn = _mm(ckv, W["ukv_k"], "nn", BF16, name="ukv_k_fwd")
    vb = _mm(ckv, W["ukv_v"], "nn", BF16, name="ukv_v_fwd")

    def qrope_fn(lin, rot, c, s):
        return lin * jnp.tile(c, (1, HEADS)) + rot * jnp.tile(s, (1, HEADS))

    qp = _row_call(lambda lin, rot, c, s: (qrope_fn(lin, rot, c, s),), [col(qpl, 0, 1024), col(qpl, 1, 1024), tab(cos), tab(sin)],
                   [], [(1024, BF16)], T=T, tm=tm, name="q_rope_fwd")[0]
    ob, lse = _flash_fwd(qn, qp, kn, kpe, vb, B=B, S=S, t=ta)

    def merge_fn(ya, yb, ga, gb):
        return jax.nn.sigmoid(ga.astype(F32)) * ya + jax.nn.sigmoid(gb.astype(F32)) * yb

    def merge_fwd(oat, obt, ga, gb, wog, wom):
        ya, yb = _dot(oat, wog), _dot(obt, wom)
        return ya, yb, merge_fn(ya, yb, ga, gb)

    ya, yb, merged = _row_call(merge_fwd, [oa, ob, col(z_gbr, 0, 1024), col(z_gbr, 1, 1024)], [W["w_o_gdn"], W["w_o_mla"]],
                               [(1024, BF16), (1024, BF16), (1024, BF16)], T=T, tm=tm, name="merge_fwd")
    x1 = _mm(merged, W["w_out"], "nn", F32, add=x2d, name="w_out_fwd")

    h2 = _norm_fwd(x1, g_ffn, T=T, tm=tm, name="norm_ffn_fwd")
    up = _mm(h2, W["w_up"], "nn", BF16, name="w_up_fwd")
    FCB = 256
    nfb = D_FF // FCB
    f = _conv_fwd(_ffn_fn, [(up, 0), (up, 2)], [(conv_ffn_w, 0), (conv_ffn_w, 2)], D_FF, BF16, T=T, S=S, tm=tm,
                  cb=D_FF // 2, ncb=2, name="ffn_act_fwd")
    x2 = _mm(f, W["w_down"], "nn", F32, add=x1, name="w_down_fwd", tk=1408)

    def final_fn(xt, tt, g):
        def lossf(xv, gv):
            e = _rms(xv, gv) - tt
            return 0.5 * jnp.sum(jnp.mean(e * e, axis=-1))

        l, vjp = jax.vjp(lossf, xt, g)
        dx, dg = vjp(jnp.ones((), F32))
        return dx, jnp.full((1, LANES), l, F32), dg

    dx2, loss_v, dg_fin = _row_call(final_fn, [x2, tgt2d], [g_fin], [(1024, F32)], [((1, LANES), F32), ((1, 1024), F32)],
                                    T=T, tm=tm, name="loss_head")

    dW = {}
    df = _mm(dx2, W["w_down"], "nt", BF16, name="w_down_dx")
    dW["w_down"] = _mm(f, dx2, "tn", F32, name="w_down_dw")
    dug, duu, dcw_g, dcw_u = _conv_bwd(_ffn_fn, [(up, 0), (up, nfb)], [(conv_ffn_w, 0), (conv_ffn_w, nfb)], df, BF16,
                                       T=T, S=S, tm=tm, cb=FCB, ncb=nfb, name="ffn_act_bwd")
    d_conv_ffn = jnp.concatenate([dcw_g, dcw_u], axis=1)
    wup_g, wup_u = W["w_up"][:, :D_FF], W["w_up"][:, D_FF:]
    dh2 = _mm(dug, wup_g, "nt", F32, name="w_up_dx_g")
    dh2 = _mm(duu, wup_u, "nt", F32, add=dh2, name="w_up_dx_u")
    dW["w_up"] = jnp.concatenate([_mm(h2, dug, "tn", F32, name="w_up_dw_g"), _mm(h2, duu, "tn", F32, name="w_up_dw_u")], axis=1)
    dx1, dg_ffn = _norm_bwd(x1, g_ffn, dh2, dx2, T=T, tm=tm, name="norm_ffn_bwd")

    dmerged = _mm(dx1, W["w_out"], "nt", F32, name="w_out_dx")
    dW["w_out"] = _mm(merged, dx1, "tn", F32, name="w_out_dw")

    def merge_bwd(dm, yat, ybt, ga, gb):
        _, vjp = jax.vjp(merge_fn, yat.astype(F32), ybt.astype(F32), ga, gb)
        return vjp(dm)

    dya, dyb, dgbr_a, dgbr_b = _row_call(merge_bwd, [dmerged, ya, yb, col(z_gbr, 0, 1024), col(z_gbr, 1, 1024)], [],
                                         [(1024, BF16)] * 4, T=T, tm=tm, name="merge_bwd")
    doa = _mm(dya, W["w_o_gdn"], "nt", F32, name="w_o_gdn_dx")
    dob = _mm(dyb, W["w_o_mla"], "nt", BF16, name="w_o_mla_dx")
    dW["w_o_gdn"] = _mm(oa, dya, "tn", F32, name="w_o_gdn_dw")
    dW["w_o_mla"] = _mm(ob, dyb, "tn", F32, name="w_o_mla_dw")

    dqn, dqp, dl = _flash_bwd_dq(qn, qp, kn, kpe, vb, ob, dob, lse, B=B, S=S, t=ta)
    dkn, dkp, dvb = _flash_bwd_dkv(qn, qp, kn, kpe, vb, dob, lse.reshape(HEADS, 1, T), dl.reshape(HEADS, 1, T),
                                   B=B, S=S, t=ta)

    def qrope_bwd(d, c, s):
        return d * jnp.tile(c, (1, HEADS)), d * jnp.tile(s, (1, HEADS))

    dq_lin, dq_rot = _row_call(qrope_bwd, [dqp, tab(cos), tab(sin)], [], [(1024, BF16), (1024, BF16)], T=T, tm=tm,
                               name="q_rope_bwd")
    wp_lin, wp_rot = W["uq_p"][:, :1024], W["uq_p"][:, 1024:]
    dcq = _mm(dqn, W["uq_n"], "nt", F32, name="uq_n_dx")
    dcq = _mm(dq_lin, wp_lin, "nt", F32, add=dcq, name="uq_pl_dx")
    dcq = _mm(dq_rot, wp_rot, "nt", F32, add=dcq, name="uq_pr_dx")
    dW["uq_n"] = _mm(cq, dqn, "tn", F32, name="uq_n_dw")
    dW["uq_p"] = jnp.concatenate([_mm(cq, dq_lin, "tn", F32, name="uq_pl_dw"), _mm(cq, dq_rot, "tn", F32, name="uq_pr_dw")], axis=1)
    dckv = _mm(dkn, W["ukv_k"], "nt", F32, name="ukv_k_dx")
    dckv = _mm(dvb, W["ukv_v"], "nt", F32, add=dckv, name="ukv_v_dx")
    dW["ukv_k"] = _mm(ckv, dkn, "tn", F32, name="ukv_k_dw")
    dW["ukv_v"] = _mm(ckv, dvb, "tn", F32, name="ukv_v_dw")

    def mla_prep_bwd(zq, zkv0, zkv1, zpl, zpr, c, s, dcqt, dckvt, dkpt, gq, gkv):
        zkv = jnp.concatenate([zkv0, zkv1], axis=1)
        _, vjp = jax.vjp(lambda a, b, p, r, g1, g2: mla_prep_fn(a, b, p, r, c, s, g1, g2), zq, zkv, zpl, zpr, gq, gkv)
        dk = dkpt[0]
        for h in range(1, HEADS):
            dk = dk + dkpt[h]
        dzq, dzkv, dzpl, dzpr, dgq, dgkv = vjp((dcqt, dckvt, dk))
        return jnp.concatenate([dzq, dzkv, dzpl, dzpr], axis=1), dgq, dgkv

    dz_small, dg_q, dg_kv = _row_call(
        mla_prep_bwd, small_cols + [tab(cos), tab(sin), dcq, dckv, (dkp, (HEADS, tm, HD), lambda i: (0, i, 0))],
        [g_q, g_kv], [(896, BF16)], [((1, Q_RANK), F32), ((1, KV_RANK), F32)], T=T, tm=tm, name="mla_prep_bwd")

    def gdn_out_bwd(o, ga, dot_, g):
        _, vjp = jax.vjp(gdn_out_fn, o, ga, g)
        return vjp(dot_)

    do_gdn, dz_ga, dg_gdn = _row_call(gdn_out_bwd, [o_gdn, z_ga, doa], [g_gdn], [(1024, F32), (1024, BF16)],
                                      [((1, HD), F32)], T=T, tm=tm, name="gdn_out_bwd")
    dqkvn, dgc, dgrT, dbeta = _gdn_bwd(qkvn, gcum, grT, beta, states, do_gdn, B=B, S=S)
    dgc_tot = dgc + _padc(dgrT[:, :, 0, :].transpose(0, 2, 1).reshape(T, HEADS), LANES)

    def gate_bwd(za, zb, dg, db, al, db_):
        _, vjp = jax.vjp(_gate_fn, za, zb, al, db_)
        return vjp((dg, db))

    dz_a, dz_b, d_alog, d_dtb = _row_call(gate_bwd, [col(z_ab, 0, LANES), col(z_ab, 1, LANES), dgc_tot, dbeta], [alog, dtb],
                                          [(LANES, BF16), (LANES, BF16)], [((1, LANES), F32), ((1, LANES), F32)],
                                          T=T, tm=tm, name="gdn_gate_bwd")
    dz_qkv, d_conv_qkv = _conv_bwd(_qkv_fn, [(z_qkv, 0)], [(conv_qkv_w, 0)], dqkvn, BF16, T=T, S=S, tm=tm, cb=QKV_CB,
                                   ncb=3072 // QKV_CB, name="gdn_qkv_bwd")

    dz_ab = jnp.concatenate([dz_a, dz_b], axis=1)
    dz_gbr = jnp.concatenate([dgbr_a, dgbr_b], axis=1)
    dh1 = None
    for key, dz in (("in_qkv", dz_qkv), ("in_ga", dz_ga), ("in_ab", dz_ab), ("in_small", dz_small), ("in_gbr", dz_gbr)):
        dh1 = _mm(dz, W[key], "nt", F32, add=dh1, name=key + "_dx", tk=896 if key == "in_small" else 1024)
        dW[key] = _mm(h1, dz, "tn", F32, name=key + "_dw", tn=896 if key == "in_small" else 1024)
    dx, dg_mix = _norm_bwd(x2d, g_mix, dh1, dx1, T=T, tm=tm, name="norm_mix_bwd")

    dsmall = {"norm_mix_g": dg_mix, "gdn_a_log": d_alog[:, :HEADS], "gdn_dt_bias": d_dtb[:, :HEADS], "gdn_norm_g": dg_gdn,
              "mla_q_norm_g": dg_q, "mla_kv_norm_g": dg_kv, "norm_ffn_g": dg_ffn, "norm_final_g": dg_fin}
    return loss_v[0, 0], dx.reshape(B, S, D_MODEL), dW, d_conv_qkv, d_conv_ffn, dsmall


def kernel(x, norm_mix_g, w_in, conv_qkv_w, gdn_a_log, gdn_dt_bias, gdn_norm_g, mla_q_norm_g, w_uq, mla_kv_norm_g, w_ukv, w_o_gdn, w_o_mla, w_out, norm_ffn_g, w_up, conv_ffn_w, w_down, norm_final_g, loss_target, m_norm_mix_g, m_w_in, m_conv_qkv_w, m_gdn_a_log, m_gdn_dt_bias, m_gdn_norm_g, m_mla_q_norm_g, m_w_uq, m_mla_kv_norm_g, m_w_ukv, m_w_o_gdn, m_w_o_mla, m_w_out, m_norm_ffn_g, m_w_up, m_conv_ffn_w, m_w_down, m_norm_final_g, v_norm_mix_g, v_w_in, v_conv_qkv_w, v_gdn_a_log, v_gdn_dt_bias, v_gdn_norm_g, v_mla_q_norm_g, v_w_uq, v_mla_kv_norm_g, v_w_ukv, v_w_o_gdn, v_w_o_mla, v_w_out, v_norm_ffn_g, v_w_up, v_conv_ffn_w, v_w_down, v_norm_final_g):
    w = dict(norm_mix_g=norm_mix_g, w_in=w_in, conv_qkv_w=conv_qkv_w, gdn_a_log=gdn_a_log, gdn_dt_bias=gdn_dt_bias,
             gdn_norm_g=gdn_norm_g, mla_q_norm_g=mla_q_norm_g, w_uq=w_uq, mla_kv_norm_g=mla_kv_norm_g, w_ukv=w_ukv,
             w_o_gdn=w_o_gdn, w_o_mla=w_o_mla, w_out=w_out, norm_ffn_g=norm_ffn_g, w_up=w_up, conv_ffn_w=conv_ffn_w,
             w_down=w_down, norm_final_g=norm_final_g)
    m = dict(norm_mix_g=m_norm_mix_g, w_in=m_w_in, conv_qkv_w=m_conv_qkv_w, gdn_a_log=m_gdn_a_log, gdn_dt_bias=m_gdn_dt_bias,
             gdn_norm_g=m_gdn_norm_g, mla_q_norm_g=m_mla_q_norm_g, w_uq=m_w_uq, mla_kv_norm_g=m_mla_kv_norm_g, w_ukv=m_w_ukv,
             w_o_gdn=m_w_o_gdn, w_o_mla=m_w_o_mla, w_out=m_w_out, norm_ffn_g=m_norm_ffn_g, w_up=m_w_up,
             conv_ffn_w=m_conv_ffn_w, w_down=m_w_down, norm_final_g=m_norm_final_g)
    v = dict(norm_mix_g=v_norm_mix_g, w_in=v_w_in, conv_qkv_w=v_conv_qkv_w, gdn_a_log=v_gdn_a_log, gdn_dt_bias=v_gdn_dt_bias,
             gdn_norm_g=v_gdn_norm_g, mla_q_norm_g=v_mla_q_norm_g, w_uq=v_w_uq, mla_kv_norm_g=v_mla_kv_norm_g, w_ukv=v_w_ukv,
             w_o_gdn=v_w_o_gdn, w_o_mla=v_w_o_mla, w_out=v_w_out, norm_ffn_g=v_norm_ffn_g, w_up=v_w_up,
             conv_ffn_w=v_conv_ffn_w, w_down=v_w_down, norm_final_g=v_norm_final_g)
    big_names = ("w_in", "w_up", "w_uq", "w_ukv", "w_o_gdn", "w_o_mla", "w_out", "w_down", "conv_qkv_w", "conv_ffn_w")
    small_names = [n for n, _ in SMALL]
    small_shapes = {n: w[n].shape for n in small_names}
    local2d = lambda d: {n: d[n][0] for n in big_names}

    w_slabs = _slabs(local2d(w), F32)
    a_bf = w_slabs["A"].astype(BF16)
    first = _allgather_async([a_bf[:1024]], name="allgather_w_in", collective_id=1)
    second = _allgather_async([w_slabs["Q"].astype(BF16), w_slabs["V"]], name="allgather_mixers", collective_id=2)
    third = _allgather_async([a_bf[1024:], w_slabs["C"].astype(BF16)], name="allgather_ffn_out", collective_id=3)
    gathered = {k: g.reshape(N_DEV, -1, g.shape[1])
                for k, g in zip(("A_in", "Q", "V", "A_up", "C"), first + second + third)}
    W, conv_qkv_full, conv_ffn_full = _layout_weights(gathered)

    def late_weights(tie):
        tie, a_up, c_all = lax.optimization_barrier((tie, gathered["A_up"], gathered["C"]))
        return tie, _layout_late(a_up, c_all)

    loss_local, dx, dW, d_conv_qkv, d_conv_ffn, dsmall = _local_step(
        x, loss_target, W, late_weights, conv_qkv_full, conv_ffn_full, {n: w[n] for n in small_names})

    g_send = _send_slabs(dW, d_conv_qkv, d_conv_ffn)
    early_names, late_names = ("A_up", "C", "Q"), ("A_in", "V")
    recv = dict(zip(early_names, _alltoall_async([g_send[k] for k in early_names], name="alltoall_grads_early",
                                                 collective_id=0)))
    recv.update(zip(late_names, _alltoall_async([g_send[k] for k in late_names], name="alltoall_grads_late",
                                                collective_id=5)))
    small_parts = _alltoall_async([jnp.tile(_pack_small(dsmall, loss_local), (N_DEV, 1))], name="alltoall_small_grads",
                                  collective_id=4)[0]
    def halves(sl):
        return {"A_in": sl["A"][:1024], "A_up": sl["A"][1024:], "Q": sl["Q"], "C": sl["C"], "V": sl["V"]}

    w_h, m_h, v_h = halves(w_slabs), halves(_slabs(local2d(m), F32)), halves(_slabs(local2d(v), F32))
    upd = {k: _reduce_adam(recv[k], w_h[k], m_h[k], v_h[k], tr=SLAB_TR[k], name="adam_" + k) for k in SLAB_TR}
    upd_small = _reduce_adam(small_parts, _pack_small({n: w[n] for n in small_names}), _pack_small({n: m[n] for n in small_names}),
                             _pack_small({n: v[n] for n in small_names}), tr=SMALL_ROWS, name="adam_small")

    loss = upd_small[0].reshape(-1)[LOSS_SLOT]
    groups = []
    for i in range(4):
        merged = {**_unslab({k: upd[k][i] for k in SLAB_TR}), **_unpack_small(upd_small[i], small_shapes)}
        groups.append([merged[n] for n in WEIGHT_ORDER])
    return (loss, dx, *groups[0], *groups[1], *groups[2], *groups[3])
```

```python
import functools
import math

import numpy as np
import jax
import jax.numpy as jnp
from jax import lax
from jax.experimental import pallas as pl
from jax.experimental.pallas import tpu as pltpu
from jax.experimental.pallas import tpu_sc as plsc

F32 = jnp.float32
BF16 = jnp.bfloat16

D_MODEL = 1024
HEADS = 8
HD = 128
GDN_CONV = 4
CHUNK = 64
Q_RANK = 384
KV_RANK = 256
ROPE = 64
ROPE_THETA = 10000.0
D_FF = 2816
FFN_CONV = 3
EPS = 1e-6
SM_SCALE = (HD + ROPE) ** -0.5
N_DEV = 8

ADAM_LR, ADAM_B1, ADAM_B2, ADAM_EPS, ADAM_WD, ADAM_STEP = 0.001, 0.9, 0.999, 1e-08, 0.01, 10

LANES = 128
SUBLANES = 8
HALO = 2 * SUBLANES
VMEM_LIMIT = 56 * 1024 * 1024
HI = lax.Precision.HIGHEST
TRI_PRECISION = None

NN = (((1,), (0,)), ((), ()))
NT = (((1,), (1,)), ((), ()))
TN = (((0,), (0,)), ((), ()))


def _dot(a, b, dims=NN, precision=None):
    return lax.dot_general(a, b, dims, precision=precision, preferred_element_type=F32)


def _pick(dim, target, align):
    best = None
    for t in range(align, min(dim, target) + 1, align):
        if dim % t == 0:
            best = t
    return dim if best is None else best


def _call(body, ins, outs, grid, *, name, scratch=(), semantics=None):
    n_in, n_out = len(ins), len(outs)

    def kern(*refs):
        body(refs[:n_in], refs[n_in:n_in + n_out], refs[n_in + n_out:])

    res = pl.pallas_call(
        kern,
        grid=grid,
        in_specs=[pl.BlockSpec(bs, im) for _, bs, im in ins],
        out_specs=[pl.BlockSpec(bs, im) for _, _, bs, im in outs],
        out_shape=[jax.ShapeDtypeStruct(s, d) for s, d, _, _ in outs],
        scratch_shapes=list(scratch),
        name=name,
        compiler_params=pltpu.CompilerParams(
            dimension_semantics=semantics or ("arbitrary",) * len(grid), vmem_limit_bytes=VMEM_LIMIT),
    )(*[a for a, _, _ in ins])
    return res


def _mm(a, b, mode, out_dtype, *, name, add=None, tm=1408, tn=1408, tk=1408):
    if mode == "nn":
        (M, K), (K2, N) = a.shape, b.shape
    elif mode == "nt":
        (M, K), (N, K2) = a.shape, b.shape
    else:
        (K, M), (K2, N) = a.shape, b.shape
    assert K == K2, (a.shape, b.shape, mode)
    tm = _pick(M, tm, LANES if mode == "tn" else 16)
    tn = _pick(N, tn, LANES)
    tk = _pick(K, tk, 16 if mode == "tn" else LANES)
    nk = K // tk
    dims = {"nn": NN, "nt": NT, "tn": TN}[mode]
    if mode == "nn":
        a_spec, b_spec = ((tm, tk), lambda i, j, k: (i, k)), ((tk, tn), lambda i, j, k: (k, j))
    elif mode == "nt":
        a_spec, b_spec = ((tm, tk), lambda i, j, k: (i, k)), ((tn, tk), lambda i, j, k: (j, k))
    else:
        a_spec, b_spec = ((tk, tm), lambda i, j, k: (k, i)), ((tk, tn), lambda i, j, k: (k, j))
    ins = [(a,) + a_spec, (b,) + b_spec]
    if add is not None:
        ins.append((add, (tm, tn), lambda i, j, k: (i, j)))
    outs = [((M, N), out_dtype, (tm, tn), lambda i, j, k: (i, j))]

    def body(in_refs, out_refs, scr):
        prod = _dot(in_refs[0][...].astype(BF16), in_refs[1][...].astype(BF16), dims)

        def finish(r):
            if add is not None:
                r = r + in_refs[2][...].astype(F32)
            out_refs[0][...] = r.astype(out_dtype)

        if nk == 1:
            finish(prod)
            return
        k = pl.program_id(2)
        acc = scr[0]

        @pl.when(k == 0)
        def _():
            acc[...] = prod

        @pl.when(k > 0)
        def _():
            acc[...] += prod

        @pl.when(k == nk - 1)
        def _():
            finish(acc[...])

    return _call(body, ins, outs, (M // tm, N // tn, nk), name=name,
                 scratch=[pltpu.VMEM((tm, tn), F32)] if nk > 1 else [],
                 semantics=("parallel", "parallel", "arbitrary"))[0]


def _row_call(fn, rows, consts, out_rows, out_accs=(), *, T, tm, name):
    nt = T // tm
    ins = []
    for r in rows:
        ins.append(r if isinstance(r, tuple) else (r, (tm, r.shape[1]), lambda i: (i, 0)))
    for c in consts:
        ins.append((c, c.shape, lambda i, nd=c.ndim: (0,) * nd))
    outs = []
    for o in out_rows:
        outs.append(((T, o[0]), o[1], (tm, o[0]), lambda i: (i, 0)) if len(o) == 2 else o)
    for shp, dt in out_accs:
        outs.append((shp, dt, shp, lambda i, nd=len(shp): (0,) * nd))
    n_r = len(out_rows)

    def body(in_refs, out_refs, _):
        i = pl.program_id(0)
        vals = fn(*[r[...] for r in in_refs])
        for o_ref, v in zip(out_refs[:n_r], vals[:n_r]):
            o_ref[...] = v.astype(o_ref.dtype)
        for o_ref, v in zip(out_refs[n_r:], vals[n_r:]):
            @pl.when(i == 0)
            def _(o_ref=o_ref):
                o_ref[...] = jnp.zeros_like(o_ref)

            o_ref[...] += v.astype(o_ref.dtype)

    return _call(body, ins, outs, (nt,), name=name)


def _rms(x, g):
    return x * lax.rsqrt(jnp.mean(x * x, axis=-1, keepdims=True) + EPS) * g


def _norm_fwd(x, g, *, T, tm, name):
    return _row_call(lambda xt, gt: (_rms(xt, gt),), [x], [g], [(x.shape[1], BF16)], T=T, tm=tm, name=name)[0]


def _norm_bwd(x, g, dh, dres, *, T, tm, name):
    def fn(xt, dht, drt, gt):
        _, vjp = jax.vjp(_rms, xt, gt)
        dx, dg = vjp(dht.astype(F32))
        return drt + dx, dg

    return _row_call(fn, [x, dh, dres], [g], [(x.shape[1], F32)], [(g.shape, F32)], T=T, tm=tm, name=name)


def _rows16(c):
    return lax.broadcasted_iota(jnp.int32, (HALO, c), 0)


@functools.lru_cache(maxsize=None)
def _shift_fn(j):
    @jax.custom_vjp
    def shift(x, halo):
        xr = pltpu.roll(x, j, 0)
        top = jnp.where(_rows16(x.shape[1]) < j, pltpu.roll(halo, j, 0), xr[:HALO])
        return jnp.concatenate([top, xr[HALO:]], axis=0)

    def fwd(x, halo):
        return shift(x, halo), None

    def bwd(_, dy):
        tm, c = dy.shape
        keep = _rows16(c) >= HALO - j
        dxr = pltpu.roll(dy, tm - j, 0)
        dx = jnp.concatenate([dxr[:tm - HALO], jnp.where(keep, 0.0, dxr[tm - HALO:])], axis=0)
        dhalo = jnp.where(keep, pltpu.roll(dy[:HALO], HALO - j, 0), 0.0)
        return dx, dhalo

    shift.defvjp(fwd, bwd)
    return shift


def _dwconv(tail, x, w):
    K = w.shape[0]
    acc = w[K - 1:K, :] * x
    for k in range(K - 1):
        acc = acc + w[k:k + 1, :] * _shift_fn(K - 1 - k)(x, tail)
    return acc


STRIP = 64


def _conv_fwd(fn, xs, ws, out_c, out_dtype, *, T, S, tm, cb, ncb, name):
    nt, tps, hb = T // tm, S // tm, tm // HALO
    ins = []
    for arr, off in xs:
        ins.append((arr, (tm, cb), lambda j, i, off=off: (i, off + j)))
        ins.append((arr, (HALO, cb), lambda j, i, off=off: (jnp.maximum(i * hb - 1, 0), off + j)))
    for arr, off in ws:
        ins.append((arr, (arr.shape[0], cb), lambda j, i, off=off: (0, off + j)))
    outs = [((T, out_c), out_dtype, (tm, cb), lambda j, i: (i, j))]
    nx = len(xs)

    def body(in_refs, out_refs, _):
        j, i = pl.program_id(0), pl.program_id(1)
        first = (i % tps) == 0
        wts = [r[...] for r in in_refs[2 * nx:]]
        for r in range(0, tm, STRIP):
            xts = [in_refs[2 * m][r:r + STRIP, :].astype(F32) for m in range(nx)]
            if r == 0:
                tails = [jnp.where(first, 0.0, in_refs[2 * m + 1][...].astype(F32)) for m in range(nx)]
            else:
                tails = [in_refs[2 * m][r - HALO:r, :].astype(F32) for m in range(nx)]
            out_refs[0][r:r + STRIP, :] = fn(j, tails, xts, wts).astype(out_dtype)

    return _call(body, ins, outs, (ncb, nt), name=name)[0]


def _conv_bwd(fn, xs, ws, dout, dx_dtype, *, T, S, tm, cb, ncb, name):
    nt, tps, hb = T // tm, S // tm, tm // HALO
    ins = []
    for arr, off in xs:
        ins.append((arr, (tm, cb), lambda j, i, off=off: (nt - 1 - i, off + j)))
        ins.append((arr, (HALO, cb), lambda j, i, off=off: (jnp.maximum((nt - 1 - i) * hb - 1, 0), off + j)))
    for arr, off in ws:
        ins.append((arr, (arr.shape[0], cb), lambda j, i, off=off: (0, off + j)))
    ins.append((dout, (tm, cb), lambda j, i: (nt - 1 - i, j)))
    nx, nw = len(xs), len(ws)
    outs = [((T, ncb * cb), dx_dtype, (tm, cb), lambda j, i: (nt - 1 - i, j)) for _ in xs]
    outs += [((arr.shape[0], ncb * cb), F32, (arr.shape[0], cb), lambda j, i: (0, j)) for arr, _ in ws]
    scratch = [pltpu.VMEM((HALO, cb), F32) for _ in xs]

    def body(in_refs, out_refs, carry):
        j, i = pl.program_id(0), pl.program_id(1)
        first = ((nt - 1 - i) % tps) == 0
        wts = [ref[...] for ref in in_refs[2 * nx:2 * nx + nw]]

        @pl.when(i == 0)
        def _():
            for c in carry:
                c[...] = jnp.zeros_like(c)

        carried = [c[...] for c in carry]
        dw_sum = None
        for r in reversed(range(0, tm, STRIP)):
            xts = [in_refs[2 * m][r:r + STRIP, :].astype(F32) for m in range(nx)]
            if r == 0:
                tails = [jnp.where(first, 0.0, in_refs[2 * m + 1][...].astype(F32)) for m in range(nx)]
            else:
                tails = [in_refs[2 * m][r - HALO:r, :].astype(F32) for m in range(nx)]
            _, vjp = jax.vjp(lambda tl, xt, wt: fn(j, tl, xt, wt), tails, xts, wts)
            dtails, dxts, dwts = vjp(in_refs[-1][r:r + STRIP, :].astype(F32))
            for m in range(nx):
                pad = jnp.concatenate([jnp.zeros((STRIP - HALO, cb), F32), carried[m]], axis=0)
                out_refs[m][r:r + STRIP, :] = (dxts[m] + pad).astype(dx_dtype)
            carried = [jnp.where(first, 0.0, dt) for dt in dtails] if r == 0 else list(dtails)
            dw_sum = list(dwts) if dw_sum is None else [a + b for a, b in zip(dw_sum, dwts)]
        for m in range(nx):
            carry[m][...] = carried[m]
        for m in range(nw):
            o_ref = out_refs[nx + m]

            @pl.when(i == 0)
            def _(o_ref=o_ref):
                o_ref[...] = jnp.zeros_like(o_ref)

            o_ref[...] += dw_sum[m]

    return _call(body, ins, outs, (ncb, nt), name=name, scratch=scratch)


QKV_CB = 512


def _qkv_fn(j, tails, xts, wts):
    y = jax.nn.silu(_dwconv(tails[0], xts[0], wts[0]))
    scale = jnp.where(j < 1024 // QKV_CB, HD ** -0.5, 1.0)
    parts = []
    for h in range(QKV_CB // HD):
        yh = y[:, h * HD:(h + 1) * HD]
        nh = yh * lax.rsqrt(jnp.sum(yh * yh, axis=-1, keepdims=True) + EPS)
        parts.append(jnp.where(j < 2048 // QKV_CB, nh * scale, yh))
    return jnp.concatenate(parts, axis=1)


def _ffn_fn(j, tails, xts, wts):
    return jax.nn.silu(_dwconv(tails[0], xts[0], wts[0])) * _dwconv(tails[1], xts[1], wts[1])


BNN = (((2,), (1,)), ((0,), (0,)))
BNT = (((2,), (2,)), ((0,), (0,)))
BTN = (((1,), (1,)), ((0,), (0,)))


@jax.custom_vjp
def _tri_inv(L):
    C = L.shape[-1]
    ii = lax.broadcasted_iota(jnp.int32, (C, C), 0)
    jj = lax.broadcasted_iota(jnp.int32, (C, C), 1)
    eye = (ii == jj).astype(F32)
    X = eye - jnp.where((ii >> 1) == (jj >> 1), L, 0.0)
    s = 1
    while (2 << s) <= C:
        E = jnp.where(((ii >> (s + 1)) == (jj >> (s + 1))) & ((ii >> s) != (jj >> s)), L, 0.0)
        X = X - _dot(_dot(X, E, BNN, precision=TRI_PRECISION), X, BNN, precision=TRI_PRECISION)
        s += 1
    return X


def _tri_inv_fwd(L):
    X = _tri_inv(L)
    return X, X


def _tri_inv_bwd(X, dX):
    return (-_dot(_dot(X, dX, BTN, precision=TRI_PRECISION), X, BNT, precision=TRI_PRECISION),)


_tri_inv.defvjp(_tri_inv_fwd, _tri_inv_bwd)


def _gdn_chunk(q, k, v, gc, gr, beta, S):
    C = q.shape[1]
    ii = lax.broadcasted_iota(jnp.int32, (C, C), 0)
    jj = lax.broadcasted_iota(jnp.int32, (C, C), 1)
    lower = ii >= jj
    decay = jnp.where(lower, jnp.exp(jnp.where(lower, gc - gr, 0.0)), 0.0)
    kb, vb = k * beta, v * beta
    L = jnp.where(ii > jj, _dot(kb, k, BNT) * decay, 0.0)
    Tinv = _tri_inv(L)
    eg = jnp.exp(gc)
    u = _dot(Tinv, vb, BNN, precision=TRI_PRECISION)
    w = _dot(Tinv, kb * eg, BNN, precision=TRI_PRECISION)
    a = _dot(q, k, BNT) * decay
    g_last = gc[:, C - 1:C, :]
    kd = k * jnp.exp(g_last - gc)
    v_new = u - _dot(w, S, BNN)
    o = _dot(q * eg, S, BNN) + _dot(a, v_new, BNN)
    S_new = S * jnp.exp(g_last) + _dot(kd, v_new, BTN)
    return o, S_new


def _heads(ref, nb, width=HD):
    return jnp.stack([ref[b, :, h * width:(h + 1) * width].astype(F32) for b in range(nb) for h in range(HEADS)])


def _gdn_fwd(qkvn, gcum, grT, beta, *, B, S):
    N, T = S // CHUNK, B * S
    row = lambda c: (lambda n: (0, n, c))
    qkv3, gc3, b3 = qkvn.reshape(B, S, 3072), gcum.reshape(B, S, LANES), beta.reshape(B, S, LANES)
    gr5 = grT.reshape(B, N, HEADS, 1, CHUNK)
    ins = [(qkv3, (B, CHUNK, 1024), row(0)), (qkv3, (B, CHUNK, 1024), row(1)), (qkv3, (B, CHUNK, 1024), row(2)),
           (gc3, (B, CHUNK, LANES), row(0)), (gr5, (B, 1, HEADS, 1, CHUNK), lambda n: (0, n, 0, 0, 0)),
           (b3, (B, CHUNK, LANES), row(0))]
    outs = [((B, S, 1024), F32, (B, CHUNK, 1024), row(0)),
            ((B, N, HEADS, HD, HD), BF16, (B, 1, HEADS, HD, HD), lambda n: (0, n, 0, 0, 0))]

    def body(in_refs, out_refs, scr):
        q_ref, k_ref, v_ref, gc_ref, gr_ref, b_ref = in_refs
        o_ref, st_ref = out_refs
        S_ref = scr[0]

        @pl.when(pl.program_id(0) == 0)
        def _():
            S_ref[...] = jnp.zeros_like(S_ref)

        S0 = S_ref[...]
        for b in range(B):
            st_ref[b, 0] = S0[b * HEADS:(b + 1) * HEADS].astype(BF16)
        gr = jnp.concatenate([gr_ref[b, 0] for b in range(B)], axis=0)
        o, Sn = _gdn_chunk(_heads(q_ref, B), _heads(k_ref, B), _heads(v_ref, B), _heads(gc_ref, B, 1), gr,
                           _heads(b_ref, B, 1), S0)
        for b in range(B):
            for h in range(HEADS):
                o_ref[b, :, h * HD:(h + 1) * HD] = o[b * HEADS + h]
        S_ref[...] = Sn

    o, st = _call(body, ins, outs, (N,), name="gdn_core_fwd", scratch=[pltpu.VMEM((B * HEADS, HD, HD), F32)])
    return o.reshape(T, 1024), st


def _gdn_bwd(qkvn, gcum, grT, beta, states, do, *, B, S):
    N, T = S // CHUNK, B * S
    row = lambda c: (lambda n: (0, N - 1 - n, c))
    qkv3, gc3, b3 = qkvn.reshape(B, S, 3072), gcum.reshape(B, S, LANES), beta.reshape(B, S, LANES)
    gr5, do3 = grT.reshape(B, N, HEADS, 1, CHUNK), do.reshape(B, S, 1024)
    ins = [(qkv3, (B, CHUNK, 1024), row(0)), (qkv3, (B, CHUNK, 1024), row(1)), (qkv3, (B, CHUNK, 1024), row(2)),
           (gc3, (B, CHUNK, LANES), row(0)), (gr5, (B, 1, HEADS, 1, CHUNK), lambda n: (0, N - 1 - n, 0, 0, 0)),
           (b3, (B, CHUNK, LANES), row(0)),
           (states, (B, 1, HEADS, HD, HD), lambda n: (0, N - 1 - n, 0, 0, 0)), (do3, (B, CHUNK, 1024), row(0))]
    outs = [((B, S, 3072), BF16, (B, CHUNK, 3072), row(0)), ((B, S, LANES), F32, (B, CHUNK, LANES), row(0)),
            ((B, N, HEADS, 1, CHUNK), F32, (B, 1, HEADS, 1, CHUNK), lambda n: (0, N - 1 - n, 0, 0, 0)),
            ((B, S, LANES), F32, (B, CHUNK, LANES), row(0))]

    def body(in_refs, out_refs, scr):
        q_ref, k_ref, v_ref, gc_ref, gr_ref, b_ref, st_ref, do_ref = in_refs
        dqkv_ref, dgc_ref, dgr_ref, db_ref = out_refs
        dS_ref = scr[0]

        @pl.when(pl.program_id(0) == 0)
        def _():
            dS_ref[...] = jnp.zeros_like(dS_ref)

        gr = jnp.concatenate([gr_ref[b, 0] for b in range(B)], axis=0)
        st = jnp.concatenate([st_ref[b, 0] for b in range(B)], axis=0).astype(F32)
        args = (_heads(q_ref, B), _heads(k_ref, B), _heads(v_ref, B), _heads(gc_ref, B, 1), gr, _heads(b_ref, B, 1), st)
        _, vjp = jax.vjp(_gdn_chunk, *args)
        dq, dk, dv, dgc, dgr, db, dS = vjp((_heads(do_ref, B), dS_ref[...]))
        lane = lax.broadcasted_iota(jnp.int32, (CHUNK, LANES), 1)
        for b in range(B):
            dgc_all = jnp.zeros((CHUNK, LANES), F32)
            db_all = jnp.zeros((CHUNK, LANES), F32)
            for h in range(HEADS):
                i = b * HEADS + h
                dqkv_ref[b, :, h * HD:(h + 1) * HD] = dq[i].astype(BF16)
                dqkv_ref[b, :, 1024 + h * HD:1024 + (h + 1) * HD] = dk[i].astype(BF16)
                dqkv_ref[b, :, 2048 + h * HD:2048 + (h + 1) * HD] = dv[i].astype(BF16)
                dgc_all = jnp.where(lane == h, dgc[i], dgc_all)
                db_all = jnp.where(lane == h, db[i], db_all)
            dgc_ref[b] = dgc_all
            db_ref[b] = db_all
            dgr_ref[b, 0] = dgr[b * HEADS:(b + 1) * HEADS]
        dS_ref[...] = dS

    dqkv, dgc, dgr, db = _call(body, ins, outs, (N,), name="gdn_core_bwd",
                               scratch=[pltpu.VMEM((B * HEADS, HD, HD), F32)])
    return dqkv.reshape(T, 3072), dgc.reshape(T, LANES), dgr.reshape(B * N, HEADS, 1, CHUNK), db.reshape(T, LANES)


def _gate_fn(za, zb, alog, dtb):
    tm = za.shape[0]
    g = -jnp.exp(alog) * jax.nn.softplus(za + dtb)
    ii = lax.broadcasted_iota(jnp.int32, (tm, tm), 0)
    jj = lax.broadcasted_iota(jnp.int32, (tm, tm), 1)
    tri = ((ii >= jj) & ((ii >> 6) == (jj >> 6))).astype(F32)
    return _dot(tri, g, precision=HI), jax.nn.sigmoid(zb)


def _scores(qn_ref, qp_ref, kn_ref, kp_ref, diag):
    q = jnp.concatenate([qn_ref[...], qp_ref[...]], axis=1)
    k = jnp.concatenate([kn_ref[...], kp_ref[...]], axis=1)
    s = _dot(q, k, NT) * SM_SCALE
    if diag:
        t = s.shape[0]
        ii = lax.broadcasted_iota(jnp.int32, (t, t), 0)
        jj = lax.broadcasted_iota(jnp.int32, (t, t), 1)
        s = jnp.where(ii >= jj, s, -jnp.inf)
    return s, q, k


HPB = 8
HW = HPB * HD


def _head_refs(refs, hh):
    return [r.at[:, hh * HD:(hh + 1) * HD] for r in refs]


def _flash_fwd(qn, qp, kn, kp, v, *, B, S, t):
    nb, T = S // t, B * S
    qmap = lambda b, h, qi, ki: (b * nb + qi, h)
    kmap = lambda b, h, qi, ki: (b * nb + jnp.minimum(ki, qi), h)
    kpmap = lambda b, h, qi, ki: (b * nb + jnp.minimum(ki, qi), 0)
    ins = [(qn, (t, HW), qmap), (qp, (t, HW), qmap), (kn, (t, HW), kmap), (kp, (t, HD), kpmap), (v, (t, HW), kmap)]
    outs = [((T, 1024), BF16, (t, HW), qmap),
            ((HEADS, T, 1), F32, (HPB, t, 1), lambda b, h, qi, ki: (h, b * nb + qi, 0))]
    scratch = [pltpu.VMEM((HPB, t, 1), F32), pltpu.VMEM((HPB, t, 2 * HD), F32)]

    def body(in_refs, out_refs, scr):
        qn_ref, qp_ref, kn_ref, kp_ref, v_ref = in_refs
        o_ref, lse_ref = out_refs
        m_ref, acc_ref = scr
        qi, ki = pl.program_id(2), pl.program_id(3)

        @pl.when(ki == 0)
        def _():
            m_ref[...] = jnp.full_like(m_ref, -jnp.inf)
            acc_ref[...] = jnp.zeros_like(acc_ref)

        def step(diag):
            for hh in range(HPB):
                qn_h, qp_h, kn_h, v_h = _head_refs((qn_ref, qp_ref, kn_ref, v_ref), hh)
                s, _, _ = _scores(qn_h, qp_h, kn_h, kp_ref, diag)
                m_old = m_ref[hh]
                m_new = jnp.maximum(m_old, jnp.max(s, axis=-1, keepdims=True))
                p = jnp.exp(s - m_new)
                alpha = jnp.exp(m_old - m_new)
                v1 = jnp.concatenate([v_h[...], jnp.ones((t, HD), BF16)], axis=1)
                acc_ref[hh] = alpha * acc_ref[hh] + _dot(p.astype(BF16), v1)
                m_ref[hh] = m_new

        @pl.when(ki < qi)
        def _():
            step(False)

        @pl.when(ki == qi)
        def _():
            step(True)
            for hh in range(HPB):
                o_ref[:, hh * HD:(hh + 1) * HD] = (acc_ref[hh, :, :HD] / acc_ref[hh, :, HD:]).astype(BF16)
                lse_ref[hh] = m_ref[hh] + jnp.log(acc_ref[hh, :, HD:HD + 1])

    return _call(body, ins, outs, (B, HEADS // HPB, nb, nb), name="mla_flash_fwd", scratch=scratch,
                 semantics=("parallel", "parallel", "parallel", "arbitrary"))


def _flash_bwd_dq(qn, qp, kn, kp, v, o, do, lse, *, B, S, t):
    nb, T = S // t, B * S
    qmap = lambda b, h, qi, ki: (b * nb + qi, h)
    kmap = lambda b, h, qi, ki: (b * nb + jnp.minimum(ki, qi), h)
    kpmap = lambda b, h, qi, ki: (b * nb + jnp.minimum(ki, qi), 0)
    ins = [(qn, (t, HW), qmap), (qp, (t, HW), qmap), (kn, (t, HW), kmap), (kp, (t, HD), kpmap), (v, (t, HW), kmap),
           (o, (t, HW), qmap), (do, (t, HW), qmap), (lse, (HPB, t, 1), lambda b, h, qi, ki: (h, b * nb + qi, 0))]
    outs = [((T, 1024), BF16, (t, HW), qmap), ((T, 1024), F32, (t, HW), qmap),
            ((HEADS, T, 1), F32, (HPB, t, 1), lambda b, h, qi, ki: (h, b * nb + qi, 0))]
    scratch = [pltpu.VMEM((HPB, t, 1), F32), pltpu.VMEM((HPB, t, 2 * HD), F32)]

    def body(in_refs, out_refs, scr):
        qn_ref, qp_ref, kn_ref, kp_ref, v_ref, o_ref, do_ref, lse_ref = in_refs
        dqn_ref, dqp_ref, dlo_ref = out_refs
        dl_ref, acc_ref = scr
        qi, ki = pl.program_id(2), pl.program_id(3)

        @pl.when(ki == 0)
        def _():
            for hh in range(HPB):
                o_h, do_h = _head_refs((o_ref, do_ref), hh)
                dl_ref[hh] = jnp.sum(do_h[...].astype(F32) * o_h[...].astype(F32), axis=-1, keepdims=True)
            acc_ref[...] = jnp.zeros_like(acc_ref)

        def step(diag):
            for hh in range(HPB):
                qn_h, qp_h, kn_h, v_h, do_h = _head_refs((qn_ref, qp_ref, kn_ref, v_ref, do_ref), hh)
                s, _, k = _scores(qn_h, qp_h, kn_h, kp_ref, diag)
                p = jnp.exp(s - lse_ref[hh])
                dp = _dot(do_h[...], v_h[...], NT)
                ds = p * (dp - dl_ref[hh]) * SM_SCALE
                acc_ref[hh] += _dot(ds.astype(BF16), k)

        @pl.when(ki < qi)
        def _():
            step(False)

        @pl.when(ki == qi)
        def _():
            step(True)
            for hh in range(HPB):
                dqn_ref[:, hh * HD:(hh + 1) * HD] = acc_ref[hh, :, :HD].astype(BF16)
                dqp_ref[:, hh * HD:(hh + 1) * HD] = acc_ref[hh, :, HD:]
            dlo_ref[...] = dl_ref[...]

    return _call(body, ins, outs, (B, HEADS // HPB, nb, nb), name="mla_flash_bwd_dq", scratch=scratch,
                 semantics=("parallel", "parallel", "parallel", "arbitrary"))


def _flash_bwd_dkv(qn, qp, kn, kp, v, do, lse_t, dl_t, *, B, S, t):
    nb, T = S // t, B * S
    qmap = lambda b, h, ki, qi: (b * nb + jnp.maximum(qi, ki), h)
    kmap = lambda b, h, ki, qi: (b * nb + ki, h)
    tmap = lambda b, h, ki, qi: (h, 0, b * nb + jnp.maximum(qi, ki))
    ins = [(qn, (t, HW), qmap), (qp, (t, HW), qmap), (kn, (t, HW), kmap),
           (kp, (t, HD), lambda b, h, ki, qi: (b * nb + ki, 0)), (v, (t, HW), kmap), (do, (t, HW), qmap),
           (lse_t, (HPB, 1, t), tmap), (dl_t, (HPB, 1, t), tmap)]
    outs = [((T, 1024), BF16, (t, HW), kmap), ((HEADS, T, HD), F32, (HPB, t, HD), lambda b, h, ki, qi: (h, b * nb + ki, 0)),
            ((T, 1024), BF16, (t, HW), kmap)]
    scratch = [pltpu.VMEM((HPB, t, 2 * HD), F32), pltpu.VMEM((HPB, t, HD), F32)]

    def body(in_refs, out_refs, scr):
        qn_ref, qp_ref, kn_ref, kp_ref, v_ref, do_ref, lse_ref, dl_ref = in_refs
        dkn_ref, dkp_ref, dv_ref = out_refs
        dk_acc, dv_acc = scr
        ki, qi = pl.program_id(2), pl.program_id(3)

        @pl.when(qi == 0)
        def _():
            dk_acc[...] = jnp.zeros_like(dk_acc)
            dv_acc[...] = jnp.zeros_like(dv_acc)

        def step(diag):
            for hh in range(HPB):
                qn_h, qp_h, kn_h, v_h, do_h = _head_refs((qn_ref, qp_ref, kn_ref, v_ref, do_ref), hh)
                q = jnp.concatenate([qn_h[...], qp_h[...]], axis=1)
                k = jnp.concatenate([kn_h[...], kp_ref[...]], axis=1)
                st = _dot(k, q, NT) * SM_SCALE
                if diag:
                    ii = lax.broadcasted_iota(jnp.int32, (t, t), 0)
                    jj = lax.broadcasted_iota(jnp.int32, (t, t), 1)
                    st = jnp.where(ii <= jj, st, -jnp.inf)
                do_t = do_h[...]
                pt = jnp.exp(st - lse_ref[hh])
                dst = pt * (_dot(v_h[...], do_t, NT) - dl_ref[hh]) * SM_SCALE
                dv_acc[hh] += _dot(pt.astype(BF16), do_t)
                dk_acc[hh] += _dot(dst.astype(BF16), q)

        @pl.when(qi > ki)
        def _():
            step(False)

        @pl.when(qi == ki)
        def _():
            step(True)

        @pl.when(qi == nb - 1)
        def _():
            for hh in range(HPB):
                dkn_ref[:, hh * HD:(hh + 1) * HD] = dk_acc[hh, :, :HD].astype(BF16)
                dkp_ref[hh] = dk_acc[hh, :, HD:]
                dv_ref[:, hh * HD:(hh + 1) * HD] = dv_acc[hh].astype(BF16)

    return _call(body, ins, outs, (B, HEADS // HPB, nb, nb), name="mla_flash_bwd_dkv", scratch=scratch,
                 semantics=("parallel", "parallel", "parallel", "arbitrary"))


def _allgather_async(shards, *, name, collective_id):
    n_arr = len(shards)
    hbm = pltpu.MemorySpace.HBM
    x_refs = [jax.new_ref(a, memory_space=hbm) for a in shards]
    out_refs = [jax.empty_ref(jax.ShapeDtypeStruct((N_DEV * a.shape[0], a.shape[1]), a.dtype), memory_space=hbm)
                for a in shards]

    @pl.kernel(mesh=plsc.ScalarSubcoreMesh(axis_name="seq", num_cores=1), name=name,
               scratch_types=(pltpu.SemaphoreType.DMA((n_arr, 7)), pltpu.SemaphoreType.DMA((n_arr, 7)),
                              pltpu.SemaphoreType.DMA((n_arr,))),
               compiler_params=pltpu.CompilerParams(collective_id=collective_id))
    def launch(send_sems, recv_sems, local_sems):
        x, y, c = lax.axis_index("x"), lax.axis_index("y"), lax.axis_index("c")
        me, sibling = (x, y, c), (x, y, 1 - c)
        chips = [(1 - x, y), (x, 1 - y), (1 - x, 1 - y)]
        barrier = pltpu.get_barrier_semaphore()
        for p in [sibling] + [(*chip, c) for chip in chips]:
            pl.semaphore_signal(barrier, inc=1, device_id=p, device_id_type=pl.DeviceIdType.MESH)
        pl.semaphore_wait(barrier, 4)

        def rows(a, px, py, pc):
            m_per = shards[a].shape[0]
            return out_refs[a].at[pl.ds((4 * px + 2 * py + pc) * m_per, m_per), :]

        def copy(a, k, block, to, src=None):
            return pltpu.make_async_remote_copy(
                src_ref=rows(a, *block) if src is None else src, dst_ref=rows(a, *block),
                send_sem=send_sems.at[a, k], recv_sem=recv_sems.at[a, k], device_id=to,
                device_id_type=pl.DeviceIdType.MESH)

        mine = [pltpu.make_async_copy(x_refs[a], rows(a, *me), local_sems.at[a]) for a in range(n_arr)]
        for cp in mine:
            cp.start()
        first = []
        for a in range(n_arr):
            first.append(copy(a, 0, me, sibling, src=x_refs[a]))
            first += [copy(a, 1 + j, me, (*chip, c), src=x_refs[a]) for j, chip in enumerate(chips)]
        for cp in first:
            cp.start()
        passed = []
        for j, chip in enumerate(chips):
            for a in range(n_arr):
                copy(a, 1 + j, (*chip, c), me).wait_recv()
                cp = copy(a, 4 + j, (*chip, c), sibling)
                cp.start()
                passed.append(cp)
        for a in range(n_arr):
            copy(a, 0, sibling, me).wait_recv()
        for j, chip in enumerate(chips):
            for a in range(n_arr):
                copy(a, 4 + j, (*chip, 1 - c), me).wait_recv()
        for cp in first + passed:
            cp.wait_send()
        for cp in mine:
            cp.wait()

    launch()
    return [r[...] for r in out_refs]


def _alltoall_async(sends, *, name, collective_id):
    n_arr = len(sends)
    hbm = pltpu.MemorySpace.HBM
    s_refs = [jax.new_ref(a, memory_space=hbm) for a in sends]
    r_refs = [jax.empty_ref(jax.ShapeDtypeStruct(a.shape, a.dtype), memory_space=hbm) for a in sends]

    @pl.kernel(mesh=plsc.ScalarSubcoreMesh(axis_name="seq", num_cores=1), name=name,
               scratch_types=(pltpu.SemaphoreType.DMA((n_arr, 7)), pltpu.SemaphoreType.DMA((n_arr, 7)),
                              pltpu.SemaphoreType.DMA((n_arr,))),
               compiler_params=pltpu.CompilerParams(collective_id=collective_id))
    def launch(send_sems, recv_sems, local_sems):
        x, y, c = lax.axis_index("x"), lax.axis_index("y"), lax.axis_index("c")
        me = 4 * x + 2 * y + c
        peers = [(1 - x if k & 4 else x, 1 - y if k & 2 else y, 1 - c if k & 1 else c) for k in range(1, N_DEV)]
        barrier = pltpu.get_barrier_semaphore()
        for p in peers:
            pl.semaphore_signal(barrier, inc=1, device_id=p, device_id_type=pl.DeviceIdType.MESH)
        pl.semaphore_wait(barrier, N_DEV - 1)

        def rows(ref, a, idx):
            m_per = sends[a].shape[0] // N_DEV
            return ref.at[pl.ds(idx * m_per, m_per), :]

        local = [pltpu.make_async_copy(rows(s_refs[a], a, me), rows(r_refs[a], a, me), local_sems.at[a])
                 for a in range(n_arr)]
        for cp in local:
            cp.start()
        copies = []
        for k, (px, py, pc) in enumerate(peers):
            for a in range(n_arr):
                cp = pltpu.make_async_remote_copy(
                    src_ref=rows(s_refs[a], a, 4 * px + 2 * py + pc), dst_ref=rows(r_refs[a], a, me),
                    send_sem=send_sems.at[a, k], recv_sem=recv_sems.at[a, k],
                    device_id=(px, py, pc), device_id_type=pl.DeviceIdType.MESH)
                cp.start()
                copies.append(cp)
        for cp in copies:
            cp.wait()
        for cp in local:
            cp.wait()

    launch()
    return [r[...] for r in r_refs]


def _reduce_adam(parts, w, m, v, *, tr, name):
    R, C = w.shape
    nR = R // tr
    ins = [(parts, (tr, C), lambda i, s=s: (s * nR + i, 0)) for s in range(N_DEV)]
    ins += [(a, (tr, C), lambda i: (i, 0)) for a in (w, m, v)]
    outs = [((R, C), F32, (tr, C), lambda i: (i, 0)) for _ in range(4)]
    c1 = 1.0 - ADAM_B1 ** ADAM_STEP
    c2 = 1.0 - ADAM_B2 ** ADAM_STEP

    def body(in_refs, out_refs, _):
        g = in_refs[0][...].astype(F32)
        for s in range(1, N_DEV):
            g = g + in_refs[s][...].astype(F32)
        wv, mv, vv = in_refs[8][...], in_refs[9][...], in_refs[10][...]
        mn = ADAM_B1 * mv + (1.0 - ADAM_B1) * g
        vn = ADAM_B2 * vv + (1.0 - ADAM_B2) * (g * g)
        delta = -ADAM_LR * ((mn / c1) / (jnp.sqrt(vn / c2) + ADAM_EPS) + ADAM_WD * wv)
        out_refs[0][...] = g
        out_refs[1][...] = delta
        out_refs[2][...] = mn
        out_refs[3][...] = vn

    return _call(body, ins, outs, (nR,), name=name, semantics=("parallel",))


IN_C, UP_C, UQ_C, QKV_C = 858, 704, 192, 384
A_W, Q_W, V_W = 896, 256, 768
SLAB_TR = {"A_in": 256, "A_up": 256, "Q": 128, "C": 368, "V": 16}
SMALL = [("norm_mix_g", 1024), ("gdn_a_log", 8), ("gdn_dt_bias", 8), ("gdn_norm_g", 128), ("mla_q_norm_g", 384),
         ("mla_kv_norm_g", 256), ("norm_ffn_g", 1024), ("norm_final_g", 1024)]
SMALL_ROWS = 32
WEIGHT_ORDER = ["norm_mix_g", "w_in", "conv_qkv_w", "gdn_a_log", "gdn_dt_bias", "gdn_norm_g", "mla_q_norm_g", "w_uq",
                "mla_kv_norm_g", "w_ukv", "w_o_gdn", "w_o_mla", "w_out", "norm_ffn_g", "w_up", "conv_ffn_w", "w_down",
                "norm_final_g"]


def _padc(w, n):
    return jnp.pad(w, ((0, 0), (0, n - w.shape[1])))


def _padrc(w, r, n):
    return jnp.pad(w, ((0, r - w.shape[0]), (0, n - w.shape[1])))


def _slabs(p, dtype):
    A = jnp.concatenate([_padc(p["w_in"], A_W), _padc(p["w_up"], A_W)], axis=0).astype(dtype)
    Q = jnp.concatenate([_padc(p["w_uq"], Q_W), p["w_ukv"]], axis=0).astype(dtype)
    C = jnp.concatenate([p["w_o_gdn"], p["w_o_mla"], p["w_out"], p["w_down"]], axis=0).astype(dtype)
    V = jnp.concatenate([_padrc(p["conv_qkv_w"], 8, V_W), _padrc(p["conv_ffn_w"], 8, V_W)], axis=0).astype(F32)
    return {"A": A, "Q": Q, "C": C, "V": V}


def _unslab(sl):
    A_in, A_up, Q, C, V = sl["A_in"], sl["A_up"], sl["Q"], sl["C"], sl["V"]
    out = {"w_in": A_in[:, :IN_C], "w_up": A_up[:, :UP_C], "w_uq": Q[:384, :UQ_C], "w_ukv": Q[384:],
           "w_o_gdn": C[0:128], "w_o_mla": C[128:256], "w_out": C[256:384], "w_down": C[384:],
           "conv_qkv_w": V[0:GDN_CONV, :QKV_C], "conv_ffn_w": V[8:8 + FFN_CONV, :UP_C]}
    return {k: a[None] for k, a in out.items()}


def _take_cols(pieces, lo, hi):
    out, off = [], 0
    for arr, a, b in pieces:
        s, e = max(lo, off), min(hi, off + b - a)
        if s < e:
            out.append(arr[:, a + s - off:a + e - off])
        off += b - a
    return out[0] if len(out) == 1 else jnp.concatenate(out, axis=1)


LOSS_SLOT = sum(n for _, n in SMALL)


def _pack_small(d, loss=None):
    flat = jnp.concatenate([d[n].reshape(-1).astype(F32) for n, _ in SMALL]
                           + ([] if loss is None else [loss.reshape(1).astype(F32)]))
    return jnp.pad(flat, (0, SMALL_ROWS * LANES - flat.shape[0])).reshape(SMALL_ROWS, LANES)


def _unpack_small(buf, shapes):
    flat, out, off = buf.reshape(-1), {}, 0
    for name, n in SMALL:
        out[name] = flat[off:off + n].reshape(shapes[name])
        off += n
    return out


def _rot_cols(w):
    h = ROPE // 2
    return jnp.concatenate([-w[:, h:], w[:, :h]], axis=1)


def _unrot_cols(dw):
    h = ROPE // 2
    return jnp.concatenate([dw[:, h:], -dw[:, :h]], axis=1)


IN_SPLITS = [0, 3072, 4096, 4104, 4112, 4496, 4752, 4816, 5840, 6864]


def _layout_late(A_up, C):
    W = {"w_up": jnp.concatenate([A_up[j, :, :UP_C] for j in range(N_DEV)], axis=1),
         "w_o_gdn": C[:, 0:128].reshape(1024, D_MODEL), "w_o_mla": C[:, 128:256].reshape(1024, D_MODEL),
         "w_out": C[:, 256:384].reshape(1024, D_MODEL), "w_down": C[:, 384:].reshape(D_FF, D_MODEL)}
    return {k: v.astype(BF16) for k, v in W.items()}


def _layout_weights(g):
    A_in, Q, V = g["A_in"], g["Q"], g["V"]
    in_pieces = [(A_in[j], 0, IN_C) for j in range(N_DEV)]
    o = IN_SPLITS
    take = lambda lo, hi: _take_cols(in_pieces, lo, hi)
    kpe = take(o[6], o[7])
    W = {
        "in_qkv": take(o[0], o[1]),
        "in_ga": take(o[1], o[2]),
        "in_ab": jnp.concatenate([_padc(take(o[2], o[3]), LANES), _padc(take(o[3], o[4]), LANES)], axis=1),
        "in_small": jnp.concatenate([take(o[4], o[6]), _padc(kpe, LANES), _padc(_rot_cols(kpe), LANES)], axis=1),
        "in_gbr": take(o[7], o[9]),
        "uq_n": jnp.concatenate([Q[j, :384, :HD] for j in range(N_DEV)], axis=1),
        "ukv_k": jnp.concatenate([Q[j, 384:, :HD] for j in range(N_DEV)], axis=1),
        "ukv_v": jnp.concatenate([Q[j, 384:, HD:] for j in range(N_DEV)], axis=1),
    }
    pe = [Q[j, :384, HD:HD + ROPE] for j in range(N_DEV)]
    W["uq_p"] = jnp.concatenate([_padc(p, HD) for p in pe] + [_padc(_rot_cols(p), HD) for p in pe], axis=1)
    conv_qkv = jnp.concatenate([V[j, 0:GDN_CONV, :QKV_C] for j in range(N_DEV)], axis=1)
    conv_ffn = jnp.concatenate([V[j, 8:8 + FFN_CONV, :UP_C] for j in range(N_DEV)], axis=1)
    return {k: v.astype(BF16) for k, v in W.items()}, conv_qkv, conv_ffn


def _full_grads(dW):
    s = dW["in_small"]
    dkpe = s[:, 640:704] + _unrot_cols(s[:, 768:832])
    in_pieces = [(dW["in_qkv"], 0, 3072), (dW["in_ga"], 0, 1024), (dW["in_ab"], 0, 8), (dW["in_ab"], 128, 136),
                 (s, 0, 640), (dkpe, 0, ROPE), (dW["in_gbr"], 0, 2048)]
    pe = []
    for j in range(N_DEV):
        lin = dW["uq_p"][:, j * HD:j * HD + ROPE]
        rot = dW["uq_p"][:, 1024 + j * HD:1024 + j * HD + ROPE]
        pe.append(lin + _unrot_cols(rot))
    return in_pieces, pe


def _send_slabs(dW, d_conv_qkv, d_conv_ffn):
    in_pieces, pe = _full_grads(dW)
    A_in, A_up, Q, V = [], [], [], []
    for j in range(N_DEV):
        A_in.append(_padc(_take_cols(in_pieces, j * IN_C, (j + 1) * IN_C), A_W))
        A_up.append(_padc(dW["w_up"][:, j * UP_C:(j + 1) * UP_C], A_W))
        guq = _padc(jnp.concatenate([dW["uq_n"][:, j * HD:(j + 1) * HD], pe[j]], axis=1), Q_W)
        gukv = jnp.concatenate([dW["ukv_k"][:, j * HD:(j + 1) * HD], dW["ukv_v"][:, j * HD:(j + 1) * HD]], axis=1)
        Q.append(jnp.concatenate([guq, gukv], axis=0))
        V.append(jnp.concatenate([_padrc(d_conv_qkv[:, j * QKV_C:(j + 1) * QKV_C], 8, V_W),
                                  _padrc(d_conv_ffn[:, j * UP_C:(j + 1) * UP_C], 8, V_W)], axis=0))
    C = jnp.concatenate([dW["w_o_gdn"].reshape(N_DEV, 128, D_MODEL), dW["w_o_mla"].reshape(N_DEV, 128, D_MODEL),
                         dW["w_out"].reshape(N_DEV, 128, D_MODEL), dW["w_down"].reshape(N_DEV, 352, D_MODEL)], axis=1)
    return {"A_in": jnp.concatenate(A_in, axis=0).astype(BF16), "A_up": jnp.concatenate(A_up, axis=0).astype(BF16),
            "Q": jnp.concatenate(Q, axis=0).astype(BF16),
            "C": C.reshape(N_DEV * 736, D_MODEL).astype(BF16), "V": jnp.concatenate(V, axis=0)}


def _rope_tables(S):
    half = ROPE // 2
    inv = ROPE_THETA ** (-jnp.arange(half, dtype=F32) / half)
    ang = jnp.arange(S, dtype=F32)[:, None] * inv[None, :]
    cos = jnp.concatenate([jnp.cos(ang), jnp.cos(ang)], axis=1)
    sin = jnp.concatenate([jnp.sin(ang), jnp.sin(ang)], axis=1)
    return _padc(cos, HD), _padc(sin, HD)


def _local_step(x, tgt, W, late_weights, conv_qkv_w, conv_ffn_w, small, tm=None, ta=None):
    B, S, _ = x.shape
    T = B * S
    tm = tm or _pick(S, 1024, CHUNK)
    ta = ta or _pick(S, 512, LANES)
    x2d, tgt2d = x.reshape(T, D_MODEL), tgt.reshape(T, D_MODEL)
    row = lambda v: v.reshape(1, -1).astype(F32)
    pad_row = lambda v: _padc(row(v), LANES)
    g_mix, g_ffn, g_fin = row(small["norm_mix_g"]), row(small["norm_ffn_g"]), row(small["norm_final_g"])
    g_gdn, g_q, g_kv = row(small["gdn_norm_g"]), row(small["mla_q_norm_g"]), row(small["mla_kv_norm_g"])
    alog, dtb = pad_row(small["gdn_a_log"]), pad_row(small["gdn_dt_bias"])
    cos, sin = _rope_tables(S)
    tps = S // tm
    tab = lambda a: (a, (tm, HD), lambda i: (i % tps, 0))
    col = lambda a, c, w: (a, (tm, w), lambda i, c=c: (i, c))

    h1 = _norm_fwd(x2d, g_mix, T=T, tm=tm, name="norm_mix_fwd")
    z_qkv = _mm(h1, W["in_qkv"], "nn", BF16, name="in_qkv_fwd")
    z_ga = _mm(h1, W["in_ga"], "nn", BF16, name="in_ga_fwd")
    z_ab = _mm(h1, W["in_ab"], "nn", F32, name="in_ab_fwd")
    z_small = _mm(h1, W["in_small"], "nn", F32, name="in_small_fwd", tn=896)
    z_gbr = _mm(h1, W["in_gbr"], "nn", BF16, name="in_gbr_fwd")

    qkvn = _conv_fwd(_qkv_fn, [(z_qkv, 0)], [(conv_qkv_w, 0)], 3072, BF16, T=T, S=S, tm=tm, cb=QKV_CB,
                     ncb=3072 // QKV_CB, name="gdn_qkv_fwd")
    gcum, beta = _row_call(lambda za, zb, al, db: _gate_fn(za, zb, al, db), [col(z_ab, 0, LANES), col(z_ab, 1, LANES)],
                           [alog, dtb], [(LANES, F32), (LANES, F32)], T=T, tm=tm, name="gdn_gate_fwd")
    grT = gcum[:, :HEADS].reshape(T // CHUNK, CHUNK, HEADS).transpose(0, 2, 1)[:, :, None, :]
    qkvn, late = late_weights(qkvn)
    W = {**W, **late}
    o_gdn, states = _gdn_fwd(qkvn, gcum, grT, beta, B=B, S=S)

    def gdn_out_fn(o, ga, g):
        parts = []
        for h in range(HEADS):
            sl = slice(h * HD, (h + 1) * HD)
            parts.append(_rms(o[:, sl], g) * jax.nn.silu(ga[:, sl].astype(F32)))
        return jnp.concatenate(parts, axis=1)

    oa = _row_call(lambda o, ga, g: (gdn_out_fn(o, ga, g),), [o_gdn, z_ga], [g_gdn], [(1024, BF16)], T=T, tm=tm,
                   name="gdn_out_fwd")[0]

    def mla_prep_fn(zq, zkv, zpl, zpr, c, s, gq, gkv):
        return _rms(zq, gq), _rms(zkv, gkv), zpl * c + zpr * s

    small_cols = [(z_small, (tm, Q_RANK), lambda i: (i, 0)), (z_small, (tm, LANES), lambda i: (i, 3)),
                  (z_small, (tm, LANES), lambda i: (i, 4)), (z_small, (tm, LANES), lambda i: (i, 5)),
                  (z_small, (tm, LANES), lambda i: (i, 6))]

    def mla_prep_fwd(zq, zkv0, zkv1, zpl, zpr, c, s, gq, gkv):
        return mla_prep_fn(zq, jnp.concatenate([zkv0, zkv1], axis=1), zpl, zpr, c, s, gq, gkv)

    cq, ckv, kpe = _row_call(mla_prep_fwd, small_cols + [tab(cos), tab(sin)], [g_q, g_kv],
                             [(Q_RANK, BF16), (KV_RANK, BF16), (HD, BF16)], T=T, tm=tm, name="mla_prep_fwd")
    qn = _mm(cq, W["uq_n"], "nn", BF16, name="uq_n_fwd")
    qpl = _mm(cq, W["uq_p"], "nn", BF16, name="uq_p_fwd")
    kn = _mm(ckv, W["ukv_k"], "nn", BF16, name="ukv_k_fwd")
    vb = _mm(ckv, W["ukv_v"], "nn", BF16, name="ukv_v_fwd")

    def qrope_fn(lin, rot, c, s):
        return lin * jnp.tile(c, (1, HEADS)) + rot * jnp.tile(s, (1, HEADS))

    qp = _row_call(lambda lin, rot, c, s: (qrope_fn(lin, rot, c, s),), [col(qpl, 0, 1024), col(qpl, 1, 1024), tab(cos), tab(sin)],
                   [], [(1024, BF16)], T=T, tm=tm, name="q_rope_fwd")[0]
    ob, lse = _flash_fwd(qn, qp, kn, kpe, vb, B=B, S=S, t=ta)

    def merge_fn(ya, yb, ga, gb):
        return jax.nn.sigmoid(ga.astype(F32)) * ya + jax.nn.sigmoid(gb.astype(F32)) * yb

    def merge_fwd(oat, obt, ga, gb, wog, wom):
        ya, yb = _dot(oat, wog), _dot(obt, wom)
        return ya, yb, merge_fn(ya, yb, ga, gb)

    ya, yb, merged = _row_call(merge_fwd, [oa, ob, col(z_gbr, 0, 1024), col(z_gbr, 1, 1024)], [W["w_o_gdn"], W["w_o_mla"]],
                               [(1024, BF16), (1024, BF16), (1024, BF16)], T=T, tm=tm, name="merge_fwd")
    x1 = _mm(merged, W["w_out"], "nn", F32, add=x2d, name="w_out_fwd")

    h2 = _norm_fwd(x1, g_ffn, T=T, tm=tm, name="norm_ffn_fwd")
    up = _mm(h2, W["w_up"], "nn", BF16, name="w_up_fwd")
    FCB = 256
    nfb = D_FF // FCB
    f = _conv_fwd(_ffn_fn, [(up, 0), (up, 2)], [(conv_ffn_w, 0), (conv_ffn_w, 2)], D_FF, BF16, T=T, S=S, tm=tm,
                  cb=D_FF // 2, ncb=2, name="ffn_act_fwd")
    x2 = _mm(f, W["w_down"], "nn", F32, add=x1, name="w_down_fwd", tk=1408)

    def final_fn(xt, tt, g):
        def lossf(xv, gv):
            e = _rms(xv, gv) - tt
            return 0.5 * jnp.sum(jnp.mean(e * e, axis=-1))

        l, vjp = jax.vjp(lossf, xt, g)
        dx, dg = vjp(jnp.ones((), F32))
        return dx, jnp.full((1, LANES), l, F32), dg

    dx2, loss_v, dg_fin = _row_call(final_fn, [x2, tgt2d], [g_fin], [(1024, F32)], [((1, LANES), F32), ((1, 1024), F32)],
                                    T=T, tm=tm, name="loss_head")

    dW = {}
    df = _mm(dx2, W["w_down"], "nt", BF16, name="w_down_dx")
    dW["w_down"] = _mm(f, dx2, "tn", F32, name="w_down_dw")
    dug, duu, dcw_g, dcw_u = _conv_bwd(_ffn_fn, [(up, 0), (up, nfb)], [(conv_ffn_w, 0), (conv_ffn_w, nfb)], df, BF16,
                                       T=T, S=S, tm=tm, cb=FCB, ncb=nfb, name="ffn_act_bwd")
    d_conv_ffn = jnp.concatenate([dcw_g, dcw_u], axis=1)
    wup_g, wup_u = W["w_up"][:, :D_FF], W["w_up"][:, D_FF:]
    dh2 = _mm(dug, wup_g, "nt", F32, name="w_up_dx_g")
    dh2 = _mm(duu, wup_u, "nt", BF16, add=dh2, name="w_up_dx_u")
    dW["w_up"] = jnp.concatenate([_mm(h2, dug, "tn", F32, name="w_up_dw_g"), _mm(h2, duu, "tn", F32, name="w_up_dw_u")], axis=1)
    dx1, dg_ffn = _norm_bwd(x1, g_ffn, dh2, dx2, T=T, tm=tm, name="norm_ffn_bwd")

    dmerged = _mm(dx1, W["w_out"], "nt", BF16, name="w_out_dx")
    dW["w_out"] = _mm(merged, dx1, "tn", F32, name="w_out_dw")

    def merge_bwd(dm, yat, ybt, ga, gb):
        _, vjp = jax.vjp(merge_fn, yat.astype(F32), ybt.astype(F32), ga, gb)
        return vjp(dm.astype(F32))

    dya, dyb, dgbr_a, dgbr_b = _row_call(merge_bwd, [dmerged, ya, yb, col(z_gbr, 0, 1024), col(z_gbr, 1, 1024)], [],
                                         [(1024, BF16)] * 4, T=T, tm=tm, name="merge_bwd")
    doa = _mm(dya, W["w_o_gdn"], "nt", BF16, name="w_o_gdn_dx")
    dob = _mm(dyb, W["w_o_mla"], "nt", BF16, name="w_o_mla_dx")
    dW["w_o_gdn"] = _mm(oa, dya, "tn", F32, name="w_o_gdn_dw")
    dW["w_o_mla"] = _mm(ob, dyb, "tn", F32, name="w_o_mla_dw")

    dqn, dqp, dl = _flash_bwd_dq(qn, qp, kn, kpe, vb, ob, dob, lse, B=B, S=S, t=ta)
    dkn, dkp, dvb = _flash_bwd_dkv(qn, qp, kn, kpe, vb, dob, lse.reshape(HEADS, 1, T), dl.reshape(HEADS, 1, T),
                                   B=B, S=S, t=ta)

    def qrope_bwd(d, c, s):
        return d * jnp.tile(c, (1, HEADS)), d * jnp.tile(s, (1, HEADS))

    dq_lin, dq_rot = _row_call(qrope_bwd, [dqp, tab(cos), tab(sin)], [], [(1024, BF16), (1024, BF16)], T=T, tm=tm,
                               name="q_rope_bwd")
    wp_lin, wp_rot = W["uq_p"][:, :1024], W["uq_p"][:, 1024:]
    dcq = _mm(dqn, W["uq_n"], "nt", F32, name="uq_n_dx")
    dcq = _mm(dq_lin, wp_lin, "nt", F32, add=dcq, name="uq_pl_dx")
    dcq = _mm(dq_rot, wp_rot, "nt", F32, add=dcq, name="uq_pr_dx")
    dW["uq_n"] = _mm(cq, dqn, "tn", F32, name="uq_n_dw")
    dW["uq_p"] = jnp.concatenate([_mm(cq, dq_lin, "tn", F32, name="uq_pl_dw"), _mm(cq, dq_rot, "tn", F32, name="uq_pr_dw")], axis=1)
    dckv = _mm(dkn, W["ukv_k"], "nt", F32, name="ukv_k_dx")
    dckv = _mm(dvb, W["ukv_v"], "nt", F32, add=dckv, name="ukv_v_dx")
    dW["ukv_k"] = _mm(ckv, dkn, "tn", F32, name="ukv_k_dw")
    dW["ukv_v"] = _mm(ckv, dvb, "tn", F32, name="ukv_v_dw")

    def mla_prep_bwd(zq, zkv0, zkv1, zpl, zpr, c, s, dcqt, dckvt, dkpt, gq, gkv):
        zkv = jnp.concatenate([zkv0, zkv1], axis=1)
        _, vjp = jax.vjp(lambda a, b, p, r, g1, g2: mla_prep_fn(a, b, p, r, c, s, g1, g2), zq, zkv, zpl, zpr, gq, gkv)
        dk = dkpt[0]
        for h in range(1, HEADS):
            dk = dk + dkpt[h]
        dzq, dzkv, dzpl, dzpr, dgq, dgkv = vjp((dcqt, dckvt, dk))
        return jnp.concatenate([dzq, dzkv, dzpl, dzpr], axis=1), dgq, dgkv

    dz_small, dg_q, dg_kv = _row_call(
        mla_prep_bwd, small_cols + [tab(cos), tab(sin), dcq, dckv, (dkp, (HEADS, tm, HD), lambda i: (0, i, 0))],
        [g_q, g_kv], [(896, BF16)], [((1, Q_RANK), F32), ((1, KV_RANK), F32)], T=T, tm=tm, name="mla_prep_bwd")

    def gdn_out_bwd(o, ga, dot_, g):
        _, vjp = jax.vjp(gdn_out_fn, o, ga, g)
        return vjp(dot_.astype(F32))

    do_gdn, dz_ga, dg_gdn = _row_call(gdn_out_bwd, [o_gdn, z_ga, doa], [g_gdn], [(1024, F32), (1024, BF16)],
                                      [((1, HD), F32)], T=T, tm=tm, name="gdn_out_bwd")
    dqkvn, dgc, dgrT, dbeta = _gdn_bwd(qkvn, gcum, grT, beta, states, do_gdn, B=B, S=S)
    dgc_tot = dgc + _padc(dgrT[:, :, 0, :].transpose(0, 2, 1).reshape(T, HEADS), LANES)

    def gate_bwd(za, zb, dg, db, al, db_):
        _, vjp = jax.vjp(_gate_fn, za, zb, al, db_)
        return vjp((dg, db))

    dz_a, dz_b, d_alog, d_dtb = _row_call(gate_bwd, [col(z_ab, 0, LANES), col(z_ab, 1, LANES), dgc_tot, dbeta], [alog, dtb],
                                          [(LANES, BF16), (LANES, BF16)], [((1, LANES), F32), ((1, LANES), F32)],
                                          T=T, tm=tm, name="gdn_gate_bwd")
    dz_qkv, d_conv_qkv = _conv_bwd(_qkv_fn, [(z_qkv, 0)], [(conv_qkv_w, 0)], dqkvn, BF16, T=T, S=S, tm=tm, cb=QKV_CB,
                                   ncb=3072 // QKV_CB, name="gdn_qkv_bwd")

    dz_ab = jnp.concatenate([dz_a, dz_b], axis=1)
    dz_gbr = jnp.concatenate([dgbr_a, dgbr_b], axis=1)
    dh1 = None
    for key, dz in (("in_qkv", dz_qkv), ("in_ga", dz_ga), ("in_ab", dz_ab), ("in_small", dz_small), ("in_gbr", dz_gbr)):
        dh1 = _mm(dz, W[key], "nt", BF16 if key == "in_gbr" else F32, add=dh1, name=key + "_dx",
                  tk=896 if key == "in_small" else 1024)
        dW[key] = _mm(h1, dz, "tn", F32, name=key + "_dw", tn=896 if key == "in_small" else 1024)
    dx, dg_mix = _norm_bwd(x2d, g_mix, dh1, dx1, T=T, tm=tm, name="norm_mix_bwd")

    dsmall = {"norm_mix_g": dg_mix, "gdn_a_log": d_alog[:, :HEADS], "gdn_dt_bias": d_dtb[:, :HEADS], "gdn_norm_g": dg_gdn,
              "mla_q_norm_g": dg_q, "mla_kv_norm_g": dg_kv, "norm_ffn_g": dg_ffn, "norm_final_g": dg_fin}
    return loss_v[0, 0], dx.reshape(B, S, D_MODEL), dW, d_conv_qkv, d_conv_ffn, dsmall


def kernel(x, norm_mix_g, w_in, conv_qkv_w, gdn_a_log, gdn_dt_bias, gdn_norm_g, mla_q_norm_g, w_uq, mla_kv_norm_g, w_ukv, w_o_gdn, w_o_mla, w_out, norm_ffn_g, w_up, conv_ffn_w, w_down, norm_final_g, loss_target, m_norm_mix_g, m_w_in, m_conv_qkv_w, m_gdn_a_log, m_gdn_dt_bias, m_gdn_norm_g, m_mla_q_norm_g, m_w_uq, m_mla_kv_norm_g, m_w_ukv, m_w_o_gdn, m_w_o_mla, m_w_out, m_norm_ffn_g, m_w_up, m_conv_ffn_w, m_w_down, m_norm_final_g, v_norm_mix_g, v_w_in, v_conv_qkv_w, v_gdn_a_log, v_gdn_dt_bias, v_gdn_norm_g, v_mla_q_norm_g, v_w_uq, v_mla_kv_norm_g, v_w_ukv, v_w_o_gdn, v_w_o_mla, v_w_out, v_norm_ffn_g, v_w_up, v_conv_ffn_w, v_w_down, v_norm_final_g):
    w = dict(norm_mix_g=norm_mix_g, w_in=w_in, conv_qkv_w=conv_qkv_w, gdn_a_log=gdn_a_log, gdn_dt_bias=gdn_dt_bias,
             gdn_norm_g=gdn_norm_g, mla_q_norm_g=mla_q_norm_g, w_uq=w_uq, mla_kv_norm_g=mla_kv_norm_g, w_ukv=w_ukv,
             w_o_gdn=w_o_gdn, w_o_mla=w_o_mla, w_out=w_out, norm_ffn_g=norm_ffn_g, w_up=w_up, conv_ffn_w=conv_ffn_w,
             w_down=w_down, norm_final_g=norm_final_g)
    m = dict(norm_mix_g=m_norm_mix_g, w_in=m_w_in, conv_qkv_w=m_conv_qkv_w, gdn_a_log=m_gdn_a_log, gdn_dt_bias=m_gdn_dt_bias,
             gdn_norm_g=m_gdn_norm_g, mla_q_norm_g=m_mla_q_norm_g, w_uq=m_w_uq, mla_kv_norm_g=m_mla_kv_norm_g, w_ukv=m_w_ukv,
             w_o_gdn=m_w_o_gdn, w_o_mla=m_w_o_mla, w_out=m_w_out, norm_ffn_g=m_norm_ffn_g, w_up=m_w_up,
             conv_ffn_w=m_conv_ffn_w, w_down=m_w_down, norm_final_g=m_norm_final_g)
    v = dict(norm_mix_g=v_norm_mix_g, w_in=v_w_in, conv_qkv_w=v_conv_qkv_w, gdn_a_log=v_gdn_a_log, gdn_dt_bias=v_gdn_dt_bias,
             gdn_norm_g=v_gdn_norm_g, mla_q_norm_g=v_mla_q_norm_g, w_uq=v_w_uq, mla_kv_norm_g=v_mla_kv_norm_g, w_ukv=v_w_ukv,
             w_o_gdn=v_w_o_gdn, w_o_mla=v_w_o_mla, w_out=v_w_out, norm_ffn_g=v_norm_ffn_g, w_up=v_w_up,
             conv_ffn_w=v_conv_ffn_w, w_down=v_w_down, norm_final_g=v_norm_final_g)
    big_names = ("w_in", "w_up", "w_uq", "w_ukv", "w_o_gdn", "w_o_mla", "w_out", "w_down", "conv_qkv_w", "conv_ffn_w")
    small_names = [n for n, _ in SMALL]
    small_shapes = {n: w[n].shape for n in small_names}
    local2d = lambda d: {n: d[n][0] for n in big_names}

    w_slabs = _slabs(local2d(w), F32)
    a_bf = w_slabs["A"].astype(BF16)
    first = _allgather_async([a_bf[:1024]], name="allgather_w_in", collective_id=1)
    second = _allgather_async([w_slabs["Q"].astype(BF16), w_slabs["V"]], name="allgather_mixers", collective_id=2)
    third = _allgather_async([a_bf[1024:], w_slabs["C"].astype(BF16)], name="allgather_ffn_out", collective_id=3)
    gathered = {k: g.reshape(N_DEV, -1, g.shape[1])
                for k, g in zip(("A_in", "Q", "V", "A_up", "C"), first + second + third)}
    W, conv_qkv_full, conv_ffn_full = _layout_weights(gathered)

    def late_weights(tie):
        tie, a_up, c_all = lax.optimization_barrier((tie, gathered["A_up"], gathered["C"]))
        return tie, _layout_late(a_up, c_all)

    loss_local, dx, dW, d_conv_qkv, d_conv_ffn, dsmall = _local_step(
        x, loss_target, W, late_weights, conv_qkv_full, conv_ffn_full, {n: w[n] for n in small_names})

    g_send = _send_slabs(dW, d_conv_qkv, d_conv_ffn)
    early_names, late_names = ("A_up", "C", "Q"), ("A_in", "V")
    recv = dict(zip(early_names, _alltoall_async([g_send[k] for k in early_names], name="alltoall_grads_early",
                                                 collective_id=0)))
    recv.update(zip(late_names, _alltoall_async([g_send[k] for k in late_names], name="alltoall_grads_late",
                                                collective_id=5)))
    small_parts = _alltoall_async([jnp.tile(_pack_small(dsmall, loss_local), (N_DEV, 1))], name="alltoall_small_grads",
                                  collective_id=4)[0]
    def halves(sl):
        return {"A_in": sl["A"][:1024], "A_up": sl["A"][1024:], "Q": sl["Q"], "C": sl["C"], "V": sl["V"]}

    w_h, m_h, v_h = halves(w_slabs), halves(_slabs(local2d(m), F32)), halves(_slabs(local2d(v), F32))
    upd = {k: _reduce_adam(recv[k], w_h[k], m_h[k], v_h[k], tr=SLAB_TR[k], name="adam_" + k) for k in SLAB_TR}
    upd_small = _reduce_adam(small_parts, _pack_small({n: w[n] for n in small_names}), _pack_small({n: m[n] for n in small_names}),
                             _pack_small({n: v[n] for n in small_names}), tr=SMALL_ROWS, name="adam_small")

    loss = upd_small[0].reshape(-1)[LOSS_SLOT]
    groups = []
    for i in range(4):
        merged = {**_unslab({k: upd[k][i] for k in SLAB_TR}), **_unpack_small(upd_small[i], small_shapes)}
        groups.append([merged[n] for n in WEIGHT_ORDER])
    return (loss, dx, *groups[0], *groups[1], *groups[2], *groups[3])
```

```python
import functools
import math

import numpy as np
import jax
import jax.numpy as jnp
from jax import lax
from jax.experimental import pallas as pl
from jax.experimental.pallas import tpu as pltpu
from jax.experimental.pallas import tpu_sc as plsc

F32 = jnp.float32
BF16 = jnp.bfloat16

D_MODEL = 1024
HEADS = 8
HD = 128
GDN_CONV = 4
CHUNK = 64
Q_RANK = 384
KV_RANK = 256
ROPE = 64
ROPE_THETA = 10000.0
D_FF = 2816
FFN_CONV = 3
EPS = 1e-6
SM_SCALE = (HD + ROPE) ** -0.5
N_DEV = 8

ADAM_LR, ADAM_B1, ADAM_B2, ADAM_EPS, ADAM_WD, ADAM_STEP = 0.001, 0.9, 0.999, 1e-08, 0.01, 10

LANES = 128
SUBLANES = 8
HALO = 2 * SUBLANES
VMEM_LIMIT = 56 * 1024 * 1024
HI = lax.Precision.HIGHEST
TRI_PRECISION = None

NN = (((1,), (0,)), ((), ()))
NT = (((1,), (1,)), ((), ()))
TN = (((0,), (0,)), ((), ()))


def _dot(a, b, dims=NN, precision=None):
    return lax.dot_general(a, b, dims, precision=precision, preferred_element_type=F32)


def _pick(dim, target, align):
    best = None
    for t in range(align, min(dim, target) + 1, align):
        if dim % t == 0:
            best = t
    return dim if best is None else best


def _call(body, ins, outs, grid, *, name, scratch=(), semantics=None):
    n_in, n_out = len(ins), len(outs)

    def kern(*refs):
        body(refs[:n_in], refs[n_in:n_in + n_out], refs[n_in + n_out:])

    res = pl.pallas_call(
        kern,
        grid=grid,
        in_specs=[pl.BlockSpec(bs, im) for _, bs, im in ins],
        out_specs=[pl.BlockSpec(bs, im) for _, _, bs, im in outs],
        out_shape=[jax.ShapeDtypeStruct(s, d) for s, d, _, _ in outs],
        scratch_shapes=list(scratch),
        name=name,
        compiler_params=pltpu.CompilerParams(
            dimension_semantics=semantics or ("arbitrary",) * len(grid), vmem_limit_bytes=VMEM_LIMIT),
    )(*[a for a, _, _ in ins])
    return res


def _mm(a, b, mode, out_dtype, *, name, add=None, tm=1408, tn=1408, tk=1408):
    if mode == "nn":
        (M, K), (K2, N) = a.shape, b.shape
    elif mode == "nt":
        (M, K), (N, K2) = a.shape, b.shape
    else:
        (K, M), (K2, N) = a.shape, b.shape
    assert K == K2, (a.shape, b.shape, mode)
    tm = _pick(M, tm, LANES if mode == "tn" else 16)
    tn = _pick(N, tn, LANES)
    tk = _pick(K, tk, 16 if mode == "tn" else LANES)
    nk = K // tk
    dims = {"nn": NN, "nt": NT, "tn": TN}[mode]
    if mode == "nn":
        a_spec, b_spec = ((tm, tk), lambda i, j, k: (i, k)), ((tk, tn), lambda i, j, k: (k, j))
    elif mode == "nt":
        a_spec, b_spec = ((tm, tk), lambda i, j, k: (i, k)), ((tn, tk), lambda i, j, k: (j, k))
    else:
        a_spec, b_spec = ((tk, tm), lambda i, j, k: (k, i)), ((tk, tn), lambda i, j, k: (k, j))
    ins = [(a,) + a_spec, (b,) + b_spec]
    if add is not None:
        ins.append((add, (tm, tn), lambda i, j, k: (i, j)))
    outs = [((M, N), out_dtype, (tm, tn), lambda i, j, k: (i, j))]

    def body(in_refs, out_refs, scr):
        prod = _dot(in_refs[0][...].astype(BF16), in_refs[1][...].astype(BF16), dims)

        def finish(r):
            if add is not None:
                r = r + in_refs[2][...].astype(F32)
            out_refs[0][...] = r.astype(out_dtype)

        if nk == 1:
            finish(prod)
            return
        k = pl.program_id(2)
        acc = scr[0]

        @pl.when(k == 0)
        def _():
            acc[...] = prod

        @pl.when(k > 0)
        def _():
            acc[...] += prod

        @pl.when(k == nk - 1)
        def _():
            finish(acc[...])

    return _call(body, ins, outs, (M // tm, N // tn, nk), name=name,
                 scratch=[pltpu.VMEM((tm, tn), F32)] if nk > 1 else [],
                 semantics=("parallel", "parallel", "arbitrary"))[0]


def _row_call(fn, rows, consts, out_rows, out_accs=(), *, T, tm, name):
    nt = T // tm
    ins = []
    for r in rows:
        ins.append(r if isinstance(r, tuple) else (r, (tm, r.shape[1]), lambda i: (i, 0)))
    for c in consts:
        ins.append((c, c.shape, lambda i, nd=c.ndim: (0,) * nd))
    outs = []
    for o in out_rows:
        outs.append(((T, o[0]), o[1], (tm, o[0]), lambda i: (i, 0)) if len(o) == 2 else o)
    for shp, dt in out_accs:
        outs.append((shp, dt, shp, lambda i, nd=len(shp): (0,) * nd))
    n_r = len(out_rows)

    def body(in_refs, out_refs, _):
        i = pl.program_id(0)
        vals = fn(*[r[...] for r in in_refs])
        for o_ref, v in zip(out_refs[:n_r], vals[:n_r]):
            o_ref[...] = v.astype(o_ref.dtype)
        for o_ref, v in zip(out_refs[n_r:], vals[n_r:]):
            @pl.when(i == 0)
            def _(o_ref=o_ref):
                o_ref[...] = jnp.zeros_like(o_ref)

            o_ref[...] += v.astype(o_ref.dtype)

    return _call(body, ins, outs, (nt,), name=name)


def _rms(x, g):
    return x * lax.rsqrt(jnp.mean(x * x, axis=-1, keepdims=True) + EPS) * g


def _norm_fwd(x, g, *, T, tm, name):
    return _row_call(lambda xt, gt: (_rms(xt, gt),), [x], [g], [(x.shape[1], BF16)], T=T, tm=tm, name=name)[0]


def _norm_bwd(x, g, dh, dres, *, T, tm, name):
    def fn(xt, dht, drt, gt):
        _, vjp = jax.vjp(_rms, xt, gt)
        dx, dg = vjp(dht.astype(F32))
        return drt + dx, dg

    return _row_call(fn, [x, dh, dres], [g], [(x.shape[1], F32)], [(g.shape, F32)], T=T, tm=tm, name=name)


def _rows16(c):
    return lax.broadcasted_iota(jnp.int32, (HALO, c), 0)


@functools.lru_cache(maxsize=None)
def _shift_fn(j):
    @jax.custom_vjp
    def shift(x, halo):
        xr = pltpu.roll(x, j, 0)
        top = jnp.where(_rows16(x.shape[1]) < j, pltpu.roll(halo, j, 0), xr[:HALO])
        return jnp.concatenate([top, xr[HALO:]], axis=0)

    def fwd(x, halo):
        return shift(x, halo), None

    def bwd(_, dy):
        tm, c = dy.shape
        keep = _rows16(c) >= HALO - j
        dxr = pltpu.roll(dy, tm - j, 0)
        dx = jnp.concatenate([dxr[:tm - HALO], jnp.where(keep, 0.0, dxr[tm - HALO:])], axis=0)
        dhalo = jnp.where(keep, pltpu.roll(dy[:HALO], HALO - j, 0), 0.0)
        return dx, dhalo

    shift.defvjp(fwd, bwd)
    return shift


def _dwconv(tail, x, w):
    K = w.shape[0]
    acc = w[K - 1:K, :] * x
    for k in range(K - 1):
        acc = acc + w[k:k + 1, :] * _shift_fn(K - 1 - k)(x, tail)
    return acc


STRIP = 128


def _conv_fwd(fn, xs, ws, out_c, out_dtype, *, T, S, tm, cb, ncb, name):
    nt, tps, hb = T // tm, S // tm, tm // HALO
    ins = []
    for arr, off in xs:
        ins.append((arr, (tm, cb), lambda j, i, off=off: (i, off + j)))
        ins.append((arr, (HALO, cb), lambda j, i, off=off: (jnp.maximum(i * hb - 1, 0), off + j)))
    for arr, off in ws:
        ins.append((arr, (arr.shape[0], cb), lambda j, i, off=off: (0, off + j)))
    outs = [((T, out_c), out_dtype, (tm, cb), lambda j, i: (i, j))]
    nx = len(xs)

    def body(in_refs, out_refs, _):
        j, i = pl.program_id(0), pl.program_id(1)
        first = (i % tps) == 0
        wts = [r[...] for r in in_refs[2 * nx:]]
        for r in range(0, tm, STRIP):
            xts = [in_refs[2 * m][r:r + STRIP, :].astype(F32) for m in range(nx)]
            if r == 0:
                tails = [jnp.where(first, 0.0, in_refs[2 * m + 1][...].astype(F32)) for m in range(nx)]
            else:
                tails = [in_refs[2 * m][r - HALO:r, :].astype(F32) for m in range(nx)]
            out_refs[0][r:r + STRIP, :] = fn(j, tails, xts, wts).astype(out_dtype)

    return _call(body, ins, outs, (ncb, nt), name=name)[0]


def _conv_bwd(fn, xs, ws, dout, dx_dtype, *, T, S, tm, cb, ncb, name):
    nt, tps, hb = T // tm, S // tm, tm // HALO
    ins = []
    for arr, off in xs:
        ins.append((arr, (tm, cb), lambda j, i, off=off: (nt - 1 - i, off + j)))
        ins.append((arr, (HALO, cb), lambda j, i, off=off: (jnp.maximum((nt - 1 - i) * hb - 1, 0), off + j)))
    for arr, off in ws:
        ins.append((arr, (arr.shape[0], cb), lambda j, i, off=off: (0, off + j)))
    ins.append((dout, (tm, cb), lambda j, i: (nt - 1 - i, j)))
    nx, nw = len(xs), len(ws)
    outs = [((T, ncb * cb), dx_dtype, (tm, cb), lambda j, i: (nt - 1 - i, j)) for _ in xs]
    outs += [((arr.shape[0], ncb * cb), F32, (arr.shape[0], cb), lambda j, i: (0, j)) for arr, _ in ws]
    scratch = [pltpu.VMEM((HALO, cb), F32) for _ in xs]

    def body(in_refs, out_refs, carry):
        j, i = pl.program_id(0), pl.program_id(1)
        first = ((nt - 1 - i) % tps) == 0
        wts = [ref[...] for ref in in_refs[2 * nx:2 * nx + nw]]

        @pl.when(i == 0)
        def _():
            for c in carry:
                c[...] = jnp.zeros_like(c)

        carried = [c[...] for c in carry]
        dw_sum = None
        for r in reversed(range(0, tm, STRIP)):
            xts = [in_refs[2 * m][r:r + STRIP, :].astype(F32) for m in range(nx)]
            if r == 0:
                tails = [jnp.where(first, 0.0, in_refs[2 * m + 1][...].astype(F32)) for m in range(nx)]
            else:
                tails = [in_refs[2 * m][r - HALO:r, :].astype(F32) for m in range(nx)]
            _, vjp = jax.vjp(lambda tl, xt, wt: fn(j, tl, xt, wt), tails, xts, wts)
            dtails, dxts, dwts = vjp(in_refs[-1][r:r + STRIP, :].astype(F32))
            for m in range(nx):
                pad = jnp.concatenate([jnp.zeros((STRIP - HALO, cb), F32), carried[m]], axis=0)
                out_refs[m][r:r + STRIP, :] = (dxts[m] + pad).astype(dx_dtype)
            carried = [jnp.where(first, 0.0, dt) for dt in dtails] if r == 0 else list(dtails)
            dw_sum = list(dwts) if dw_sum is None else [a + b for a, b in zip(dw_sum, dwts)]
        for m in range(nx):
            carry[m][...] = carried[m]
        for m in range(nw):
            o_ref = out_refs[nx + m]

            @pl.when(i == 0)
            def _(o_ref=o_ref):
                o_ref[...] = jnp.zeros_like(o_ref)

            o_ref[...] += dw_sum[m]

    return _call(body, ins, outs, (ncb, nt), name=name, scratch=scratch)


QKV_CB = 512


def _qkv_fn(j, tails, xts, wts):
    y = jax.nn.silu(_dwconv(tails[0], xts[0], wts[0]))
    scale = jnp.where(j < 1024 // QKV_CB, HD ** -0.5, 1.0)
    parts = []
    for h in range(QKV_CB // HD):
        yh = y[:, h * HD:(h + 1) * HD]
        nh = yh * lax.rsqrt(jnp.sum(yh * yh, axis=-1, keepdims=True) + EPS)
        parts.append(jnp.where(j < 2048 // QKV_CB, nh * scale, yh))
    return jnp.concatenate(parts, axis=1)


def _ffn_fn(j, tails, xts, wts):
    return jax.nn.silu(_dwconv(tails[0], xts[0], wts[0])) * _dwconv(tails[1], xts[1], wts[1])


BNN = (((2,), (1,)), ((0,), (0,)))
BNT = (((2,), (2,)), ((0,), (0,)))
BTN = (((1,), (1,)), ((0,), (0,)))


@jax.custom_vjp
def _tri_inv(L):
    C = L.shape[-1]
    ii = lax.broadcasted_iota(jnp.int32, (C, C), 0)
    jj = lax.broadcasted_iota(jnp.int32, (C, C), 1)
    eye = (ii == jj).astype(F32)
    X = eye - jnp.where((ii >> 1) == (jj >> 1), L, 0.0)
    s = 1
    while (2 << s) <= C:
        E = jnp.where(((ii >> (s + 1)) == (jj >> (s + 1))) & ((ii >> s) != (jj >> s)), L, 0.0)
        X = X - _dot(_dot(X, E, BNN, precision=TRI_PRECISION), X, BNN, precision=TRI_PRECISION)
        s += 1
    return X


def _tri_inv_fwd(L):
    X = _tri_inv(L)
    return X, X


def _tri_inv_bwd(X, dX):
    return (-_dot(_dot(X, dX, BTN, precision=TRI_PRECISION), X, BNT, precision=TRI_PRECISION),)


_tri_inv.defvjp(_tri_inv_fwd, _tri_inv_bwd)


def _gdn_chunk(q, k, v, gc, gr, beta, S):
    C = q.shape[1]
    ii = lax.broadcasted_iota(jnp.int32, (C, C), 0)
    jj = lax.broadcasted_iota(jnp.int32, (C, C), 1)
    lower = ii >= jj
    decay = jnp.where(lower, jnp.exp(jnp.where(lower, gc - gr, 0.0)), 0.0)
    kb, vb = k * beta, v * beta
    L = jnp.where(ii > jj, _dot(kb, k, BNT) * decay, 0.0)
    Tinv = _tri_inv(L)
    eg = jnp.exp(gc)
    u = _dot(Tinv, vb, BNN, precision=TRI_PRECISION)
    w = _dot(Tinv, kb * eg, BNN, precision=TRI_PRECISION)
    a = _dot(q, k, BNT) * decay
    g_last = gc[:, C - 1:C, :]
    kd = k * jnp.exp(g_last - gc)
    v_new = u - _dot(w, S, BNN)
    o = _dot(q * eg, S, BNN) + _dot(a, v_new, BNN)
    S_new = S * jnp.exp(g_last) + _dot(kd, v_new, BTN)
    return o, S_new


def _heads(ref, nb, width=HD):
    return jnp.stack([ref[b, :, h * width:(h + 1) * width].astype(F32) for b in range(nb) for h in range(HEADS)])


def _gdn_fwd(qkvn, gcum, grT, beta, *, B, S):
    N, T = S // CHUNK, B * S
    row = lambda c: (lambda n: (0, n, c))
    qkv3, gc3, b3 = qkvn.reshape(B, S, 3072), gcum.reshape(B, S, LANES), beta.reshape(B, S, LANES)
    gr5 = grT.reshape(B, N, HEADS, 1, CHUNK)
    ins = [(qkv3, (B, CHUNK, 1024), row(0)), (qkv3, (B, CHUNK, 1024), row(1)), (qkv3, (B, CHUNK, 1024), row(2)),
           (gc3, (B, CHUNK, LANES), row(0)), (gr5, (B, 1, HEADS, 1, CHUNK), lambda n: (0, n, 0, 0, 0)),
           (b3, (B, CHUNK, LANES), row(0))]
    outs = [((B, S, 1024), F32, (B, CHUNK, 1024), row(0)),
            ((B, N, HEADS, HD, HD), BF16, (B, 1, HEADS, HD, HD), lambda n: (0, n, 0, 0, 0))]

    def body(in_refs, out_refs, scr):
        q_ref, k_ref, v_ref, gc_ref, gr_ref, b_ref = in_refs
        o_ref, st_ref = out_refs
        S_ref = scr[0]

        @pl.when(pl.program_id(0) == 0)
        def _():
            S_ref[...] = jnp.zeros_like(S_ref)

        S0 = S_ref[...]
        for b in range(B):
            st_ref[b, 0] = S0[b * HEADS:(b + 1) * HEADS].astype(BF16)
        gr = jnp.concatenate([gr_ref[b, 0] for b in range(B)], axis=0)
        o, Sn = _gdn_chunk(_heads(q_ref, B), _heads(k_ref, B), _heads(v_ref, B), _heads(gc_ref, B, 1), gr,
                           _heads(b_ref, B, 1), S0)
        for b in range(B):
            for h in range(HEADS):
                o_ref[b, :, h * HD:(h + 1) * HD] = o[b * HEADS + h]
        S_ref[...] = Sn

    o, st = _call(body, ins, outs, (N,), name="gdn_core_fwd", scratch=[pltpu.VMEM((B * HEADS, HD, HD), F32)])
    return o.reshape(T, 1024), st


def _gdn_bwd(qkvn, gcum, grT, beta, states, do, *, B, S):
    N, T = S // CHUNK, B * S
    row = lambda c: (lambda n: (0, N - 1 - n, c))
    qkv3, gc3, b3 = qkvn.reshape(B, S, 3072), gcum.reshape(B, S, LANES), beta.reshape(B, S, LANES)
    gr5, do3 = grT.reshape(B, N, HEADS, 1, CHUNK), do.reshape(B, S, 1024)
    ins = [(qkv3, (B, CHUNK, 1024), row(0)), (qkv3, (B, CHUNK, 1024), row(1)), (qkv3, (B, CHUNK, 1024), row(2)),
           (gc3, (B, CHUNK, LANES), row(0)), (gr5, (B, 1, HEADS, 1, CHUNK), lambda n: (0, N - 1 - n, 0, 0, 0)),
           (b3, (B, CHUNK, LANES), row(0)),
           (states, (B, 1, HEADS, HD, HD), lambda n: (0, N - 1 - n, 0, 0, 0)), (do3, (B, CHUNK, 1024), row(0))]
    outs = [((B, S, 3072), BF16, (B, CHUNK, 3072), row(0)), ((B, S, LANES), F32, (B, CHUNK, LANES), row(0)),
            ((B, N, HEADS, 1, CHUNK), F32, (B, 1, HEADS, 1, CHUNK), lambda n: (0, N - 1 - n, 0, 0, 0)),
            ((B, S, LANES), F32, (B, CHUNK, LANES), row(0))]

    def body(in_refs, out_refs, scr):
        q_ref, k_ref, v_ref, gc_ref, gr_ref, b_ref, st_ref, do_ref = in_refs
        dqkv_ref, dgc_ref, dgr_ref, db_ref = out_refs
        dS_ref = scr[0]

        @pl.when(pl.program_id(0) == 0)
        def _():
            dS_ref[...] = jnp.zeros_like(dS_ref)

        gr = jnp.concatenate([gr_ref[b, 0] for b in range(B)], axis=0)
        st = jnp.concatenate([st_ref[b, 0] for b in range(B)], axis=0).astype(F32)
        args = (_heads(q_ref, B), _heads(k_ref, B), _heads(v_ref, B), _heads(gc_ref, B, 1), gr, _heads(b_ref, B, 1), st)
        _, vjp = jax.vjp(_gdn_chunk, *args)
        dq, dk, dv, dgc, dgr, db, dS = vjp((_heads(do_ref, B), dS_ref[...]))
        lane = lax.broadcasted_iota(jnp.int32, (CHUNK, LANES), 1)
        for b in range(B):
            dgc_all = jnp.zeros((CHUNK, LANES), F32)
            db_all = jnp.zeros((CHUNK, LANES), F32)
            for h in range(HEADS):
                i = b * HEADS + h
                dqkv_ref[b, :, h * HD:(h + 1) * HD] = dq[i].astype(BF16)
                dqkv_ref[b, :, 1024 + h * HD:1024 + (h + 1) * HD] = dk[i].astype(BF16)
                dqkv_ref[b, :, 2048 + h * HD:2048 + (h + 1) * HD] = dv[i].astype(BF16)
                dgc_all = jnp.where(lane == h, dgc[i], dgc_all)
                db_all = jnp.where(lane == h, db[i], db_all)
            dgc_ref[b] = dgc_all
            db_ref[b] = db_all
            dgr_ref[b, 0] = dgr[b * HEADS:(b + 1) * HEADS]
        dS_ref[...] = dS

    dqkv, dgc, dgr, db = _call(body, ins, outs, (N,), name="gdn_core_bwd",
                               scratch=[pltpu.VMEM((B * HEADS, HD, HD), F32)])
    return dqkv.reshape(T, 3072), dgc.reshape(T, LANES), dgr.reshape(B * N, HEADS, 1, CHUNK), db.reshape(T, LANES)


def _gate_fn(za, zb, alog, dtb):
    tm = za.shape[0]
    g = -jnp.exp(alog) * jax.nn.softplus(za + dtb)
    ii = lax.broadcasted_iota(jnp.int32, (tm, tm), 0)
    jj = lax.broadcasted_iota(jnp.int32, (tm, tm), 1)
    tri = ((ii >= jj) & ((ii >> 6) == (jj >> 6))).astype(F32)
    return _dot(tri, g, precision=HI), jax.nn.sigmoid(zb)


def _scores(qn_ref, qp_ref, kn_ref, kp_ref, diag):
    q = jnp.concatenate([qn_ref[...], qp_ref[...]], axis=1)
    k = jnp.concatenate([kn_ref[...], kp_ref[...]], axis=1)
    s = _dot(q, k, NT) * SM_SCALE
    if diag:
        t = s.shape[0]
        ii = lax.broadcasted_iota(jnp.int32, (t, t), 0)
        jj = lax.broadcasted_iota(jnp.int32, (t, t), 1)
        s = jnp.where(ii >= jj, s, -jnp.inf)
    return s, q, k


HPB = 8
HW = HPB * HD


def _head_refs(refs, hh):
    return [r.at[:, hh * HD:(hh + 1) * HD] for r in refs]


def _flash_fwd(qn, qp, kn, kp, v, *, B, S, t):
    nb, T = S // t, B * S
    qmap = lambda b, h, qi, ki: (b * nb + qi, h)
    kmap = lambda b, h, qi, ki: (b * nb + jnp.minimum(ki, qi), h)
    kpmap = lambda b, h, qi, ki: (b * nb + jnp.minimum(ki, qi), 0)
    ins = [(qn, (t, HW), qmap), (qp, (t, HW), qmap), (kn, (t, HW), kmap), (kp, (t, HD), kpmap), (v, (t, HW), kmap)]
    outs = [((T, 1024), BF16, (t, HW), qmap),
            ((HEADS, T, 1), F32, (HPB, t, 1), lambda b, h, qi, ki: (h, b * nb + qi, 0))]
    scratch = [pltpu.VMEM((HPB, t, 1), F32), pltpu.VMEM((HPB, t, 2 * HD), F32)]

    def body(in_refs, out_refs, scr):
        qn_ref, qp_ref, kn_ref, kp_ref, v_ref = in_refs
        o_ref, lse_ref = out_refs
        m_ref, acc_ref = scr
        qi, ki = pl.program_id(2), pl.program_id(3)

        @pl.when(ki == 0)
        def _():
            m_ref[...] = jnp.full_like(m_ref, -jnp.inf)
            acc_ref[...] = jnp.zeros_like(acc_ref)

        def step(diag):
            for hh in range(HPB):
                qn_h, qp_h, kn_h, v_h = _head_refs((qn_ref, qp_ref, kn_ref, v_ref), hh)
                s, _, _ = _scores(qn_h, qp_h, kn_h, kp_ref, diag)
                m_old = m_ref[hh]
                m_new = jnp.maximum(m_old, jnp.max(s, axis=-1, keepdims=True))
                p = jnp.exp(s - m_new)
                alpha = jnp.exp(m_old - m_new)
                v1 = jnp.concatenate([v_h[...], jnp.ones((t, HD), BF16)], axis=1)
                acc_ref[hh] = alpha * acc_ref[hh] + _dot(p.astype(BF16), v1)
                m_ref[hh] = m_new

        @pl.when(ki < qi)
        def _():
            step(False)

        @pl.when(ki == qi)
        def _():
            step(True)
            for hh in range(HPB):
                o_ref[:, hh * HD:(hh + 1) * HD] = (acc_ref[hh, :, :HD] / acc_ref[hh, :, HD:]).astype(BF16)
                lse_ref[hh] = m_ref[hh] + jnp.log(acc_ref[hh, :, HD:HD + 1])

    return _call(body, ins, outs, (B, HEADS // HPB, nb, nb), name="mla_flash_fwd", scratch=scratch,
                 semantics=("parallel", "parallel", "parallel", "arbitrary"))


def _flash_bwd_dq(qn, qp, kn, kp, v, o, do, lse, *, B, S, t):
    nb, T = S // t, B * S
    qmap = lambda b, h, qi, ki: (b * nb + qi, h)
    kmap = lambda b, h, qi, ki: (b * nb + jnp.minimum(ki, qi), h)
    kpmap = lambda b, h, qi, ki: (b * nb + jnp.minimum(ki, qi), 0)
    ins = [(qn, (t, HW), qmap), (qp, (t, HW), qmap), (kn, (t, HW), kmap), (kp, (t, HD), kpmap), (v, (t, HW), kmap),
           (o, (t, HW), qmap), (do, (t, HW), qmap), (lse, (HPB, t, 1), lambda b, h, qi, ki: (h, b * nb + qi, 0))]
    outs = [((T, 1024), BF16, (t, HW), qmap), ((T, 1024), BF16, (t, HW), qmap),
            ((HEADS, T, 1), F32, (HPB, t, 1), lambda b, h, qi, ki: (h, b * nb + qi, 0))]
    scratch = [pltpu.VMEM((HPB, t, 1), F32), pltpu.VMEM((HPB, t, 2 * HD), F32)]

    def body(in_refs, out_refs, scr):
        qn_ref, qp_ref, kn_ref, kp_ref, v_ref, o_ref, do_ref, lse_ref = in_refs
        dqn_ref, dqp_ref, dlo_ref = out_refs
        dl_ref, acc_ref = scr
        qi, ki = pl.program_id(2), pl.program_id(3)

        @pl.when(ki == 0)
        def _():
            for hh in range(HPB):
                o_h, do_h = _head_refs((o_ref, do_ref), hh)
                dl_ref[hh] = jnp.sum(do_h[...].astype(F32) * o_h[...].astype(F32), axis=-1, keepdims=True)
            acc_ref[...] = jnp.zeros_like(acc_ref)

        def step(diag):
            for hh in range(HPB):
                qn_h, qp_h, kn_h, v_h, do_h = _head_refs((qn_ref, qp_ref, kn_ref, v_ref, do_ref), hh)
                s, _, k = _scores(qn_h, qp_h, kn_h, kp_ref, diag)
                p = jnp.exp(s - lse_ref[hh])
                dp = _dot(do_h[...], v_h[...], NT)
                ds = p * (dp - dl_ref[hh]) * SM_SCALE
                acc_ref[hh] += _dot(ds.astype(BF16), k)

        @pl.when(ki < qi)
        def _():
            step(False)

        @pl.when(ki == qi)
        def _():
            step(True)
            for hh in range(HPB):
                dqn_ref[:, hh * HD:(hh + 1) * HD] = acc_ref[hh, :, :HD].astype(BF16)
                dqp_ref[:, hh * HD:(hh + 1) * HD] = acc_ref[hh, :, HD:].astype(BF16)
            dlo_ref[...] = dl_ref[...]

    return _call(body, ins, outs, (B, HEADS // HPB, nb, nb), name="mla_flash_bwd_dq", scratch=scratch,
                 semantics=("parallel", "parallel", "parallel", "arbitrary"))


def _flash_bwd_dkv(qn, qp, kn, kp, v, do, lse_t, dl_t, *, B, S, t):
    nb, T = S // t, B * S
    qmap = lambda b, h, ki, qi: (b * nb + jnp.maximum(qi, ki), h)
    kmap = lambda b, h, ki, qi: (b * nb + ki, h)
    tmap = lambda b, h, ki, qi: (h, 0, b * nb + jnp.maximum(qi, ki))
    ins = [(qn, (t, HW), qmap), (qp, (t, HW), qmap), (kn, (t, HW), kmap),
           (kp, (t, HD), lambda b, h, ki, qi: (b * nb + ki, 0)), (v, (t, HW), kmap), (do, (t, HW), qmap),
           (lse_t, (HPB, 1, t), tmap), (dl_t, (HPB, 1, t), tmap)]
    outs = [((T, 1024), BF16, (t, HW), kmap), ((HEADS, T, HD), F32, (HPB, t, HD), lambda b, h, ki, qi: (h, b * nb + ki, 0)),
            ((T, 1024), BF16, (t, HW), kmap)]
    scratch = [pltpu.VMEM((HPB, t, 2 * HD), F32), pltpu.VMEM((HPB, t, HD), F32)]

    def body(in_refs, out_refs, scr):
        qn_ref, qp_ref, kn_ref, kp_ref, v_ref, do_ref, lse_ref, dl_ref = in_refs
        dkn_ref, dkp_ref, dv_ref = out_refs
        dk_acc, dv_acc = scr
        ki, qi = pl.program_id(2), pl.program_id(3)

        @pl.when(qi == 0)
        def _():
            dk_acc[...] = jnp.zeros_like(dk_acc)
            dv_acc[...] = jnp.zeros_like(dv_acc)

        def step(diag):
            for hh in range(HPB):
                qn_h, qp_h, kn_h, v_h, do_h = _head_refs((qn_ref, qp_ref, kn_ref, v_ref, do_ref), hh)
                q = jnp.concatenate([qn_h[...], qp_h[...]], axis=1)
                k = jnp.concatenate([kn_h[...], kp_ref[...]], axis=1)
                st = _dot(k, q, NT) * SM_SCALE
                if diag:
                    ii = lax.broadcasted_iota(jnp.int32, (t, t), 0)
                    jj = lax.broadcasted_iota(jnp.int32, (t, t), 1)
                    st = jnp.where(ii <= jj, st, -jnp.inf)
                do_t = do_h[...]
                pt = jnp.exp(st - lse_ref[hh])
                dst = pt * (_dot(v_h[...], do_t, NT) - dl_ref[hh]) * SM_SCALE
                dv_acc[hh] += _dot(pt.astype(BF16), do_t)
                dk_acc[hh] += _dot(dst.astype(BF16), q)

        @pl.when(qi > ki)
        def _():
            step(False)

        @pl.when(qi == ki)
        def _():
            step(True)

        @pl.when(qi == nb - 1)
        def _():
            for hh in range(HPB):
                dkn_ref[:, hh * HD:(hh + 1) * HD] = dk_acc[hh, :, :HD].astype(BF16)
                dkp_ref[hh] = dk_acc[hh, :, HD:]
                dv_ref[:, hh * HD:(hh + 1) * HD] = dv_acc[hh].astype(BF16)

    return _call(body, ins, outs, (B, HEADS // HPB, nb, nb), name="mla_flash_bwd_dkv", scratch=scratch,
                 semantics=("parallel", "parallel", "parallel", "arbitrary"))


def _allgather_async(shards, *, name, collective_id):
    n_arr = len(shards)
    hbm = pltpu.MemorySpace.HBM
    x_refs = [jax.new_ref(a, memory_space=hbm) for a in shards]
    out_refs = [jax.empty_ref(jax.ShapeDtypeStruct((N_DEV * a.shape[0], a.shape[1]), a.dtype), memory_space=hbm)
                for a in shards]

    @pl.kernel(mesh=plsc.ScalarSubcoreMesh(axis_name="seq", num_cores=1), name=name,
               scratch_types=(pltpu.SemaphoreType.DMA((n_arr, 7)), pltpu.SemaphoreType.DMA((n_arr, 7)),
                              pltpu.SemaphoreType.DMA((n_arr,))),
               compiler_params=pltpu.CompilerParams(collective_id=collective_id))
    def launch(send_sems, recv_sems, local_sems):
        x, y, c = lax.axis_index("x"), lax.axis_index("y"), lax.axis_index("c")
        me, sibling = (x, y, c), (x, y, 1 - c)
        chips = [(1 - x, y), (x, 1 - y), (1 - x, 1 - y)]
        barrier = pltpu.get_barrier_semaphore()
        for p in [sibling] + [(*chip, c) for chip in chips]:
            pl.semaphore_signal(barrier, inc=1, device_id=p, device_id_type=pl.DeviceIdType.MESH)
        pl.semaphore_wait(barrier, 4)

        def rows(a, px, py, pc):
            m_per = shards[a].shape[0]
            return out_refs[a].at[pl.ds((4 * px + 2 * py + pc) * m_per, m_per), :]

        def copy(a, k, block, to, src=None):
            return pltpu.make_async_remote_copy(
                src_ref=rows(a, *block) if src is None else src, dst_ref=rows(a, *block),
                send_sem=send_sems.at[a, k], recv_sem=recv_sems.at[a, k], device_id=to,
                device_id_type=pl.DeviceIdType.MESH)

        mine = [pltpu.make_async_copy(x_refs[a], rows(a, *me), local_sems.at[a]) for a in range(n_arr)]
        for cp in mine:
            cp.start()
        first = []
        for a in range(n_arr):
            first.append(copy(a, 0, me, sibling, src=x_refs[a]))
            first += [copy(a, 1 + j, me, (*chip, c), src=x_refs[a]) for j, chip in enumerate(chips)]
        for cp in first:
            cp.start()
        passed = []
        for j, chip in enumerate(chips):
            for a in range(n_arr):
                copy(a, 1 + j, (*chip, c), me).wait_recv()
                cp = copy(a, 4 + j, (*chip, c), sibling)
                cp.start()
                passed.append(cp)
        for a in range(n_arr):
            copy(a, 0, sibling, me).wait_recv()
        for j, chip in enumerate(chips):
            for a in range(n_arr):
                copy(a, 4 + j, (*chip, 1 - c), me).wait_recv()
        for cp in first + passed:
            cp.wait_send()
        for cp in mine:
            cp.wait()

    launch()
    return [r[...] for r in out_refs]


def _alltoall_async(sends, *, name, collective_id):
    n_arr = len(sends)
    hbm = pltpu.MemorySpace.HBM
    s_refs = [jax.new_ref(a, memory_space=hbm) for a in sends]
    r_refs = [jax.empty_ref(jax.ShapeDtypeStruct(a.shape, a.dtype), memory_space=hbm) for a in sends]

    @pl.kernel(mesh=plsc.ScalarSubcoreMesh(axis_name="seq", num_cores=1), name=name,
               scratch_types=(pltpu.SemaphoreType.DMA((n_arr, 7)), pltpu.SemaphoreType.DMA((n_arr, 7)),
                              pltpu.SemaphoreType.DMA((n_arr,))),
               compiler_params=pltpu.CompilerParams(collective_id=collective_id))
    def launch(send_sems, recv_sems, local_sems):
        x, y, c = lax.axis_index("x"), lax.axis_index("y"), lax.axis_index("c")
        me = 4 * x + 2 * y + c
        peers = [(1 - x if k & 4 else x, 1 - y if k & 2 else y, 1 - c if k & 1 else c) for k in range(1, N_DEV)]
        barrier = pltpu.get_barrier_semaphore()
        for p in peers:
            pl.semaphore_signal(barrier, inc=1, device_id=p, device_id_type=pl.DeviceIdType.MESH)
        pl.semaphore_wait(barrier, N_DEV - 1)

        def rows(ref, a, idx):
            m_per = sends[a].shape[0] // N_DEV
            return ref.at[pl.ds(idx * m_per, m_per), :]

        local = [pltpu.make_async_copy(rows(s_refs[a], a, me), rows(r_refs[a], a, me), local_sems.at[a])
                 for a in range(n_arr)]
        for cp in local:
            cp.start()
        copies = []
        for k, (px, py, pc) in enumerate(peers):
            for a in range(n_arr):
                cp = pltpu.make_async_remote_copy(
                    src_ref=rows(s_refs[a], a, 4 * px + 2 * py + pc), dst_ref=rows(r_refs[a], a, me),
                    send_sem=send_sems.at[a, k], recv_sem=recv_sems.at[a, k],
                    device_id=(px, py, pc), device_id_type=pl.DeviceIdType.MESH)
                cp.start()
                copies.append(cp)
        for cp in copies:
            cp.wait()
        for cp in local:
            cp.wait()

    launch()
    return [r[...] for r in r_refs]


def _reduce_adam(parts, w, m, v, *, tr, name):
    R, C = w.shape
    nR = R // tr
    ins = [(parts, (tr, C), lambda i, s=s: (s * nR + i, 0)) for s in range(N_DEV)]
    ins += [(a, (tr, C), lambda i: (i, 0)) for a in (w, m, v)]
    outs = [((R, C), F32, (tr, C), lambda i: (i, 0)) for _ in range(4)]
    c1 = 1.0 - ADAM_B1 ** ADAM_STEP
    c2 = 1.0 - ADAM_B2 ** ADAM_STEP

    def body(in_refs, out_refs, _):
        g = in_refs[0][...].astype(F32)
        for s in range(1, N_DEV):
            g = g + in_refs[s][...].astype(F32)
        wv, mv, vv = in_refs[8][...], in_refs[9][...], in_refs[10][...]
        mn = ADAM_B1 * mv + (1.0 - ADAM_B1) * g
        vn = ADAM_B2 * vv + (1.0 - ADAM_B2) * (g * g)
        delta = -ADAM_LR * ((mn / c1) / (jnp.sqrt(vn / c2) + ADAM_EPS) + ADAM_WD * wv)
        out_refs[0][...] = g
        out_refs[1][...] = delta
        out_refs[2][...] = mn
        out_refs[3][...] = vn

    return _call(body, ins, outs, (nR,), name=name, semantics=("parallel",))


IN_C, UP_C, UQ_C, QKV_C = 858, 704, 192, 384
A_W, Q_W, V_W = 896, 256, 768
SLAB_TR = {"A_in": 256, "A_up": 256, "Q": 128, "C": 368, "V": 16}
SMALL = [("norm_mix_g", 1024), ("gdn_a_log", 8), ("gdn_dt_bias", 8), ("gdn_norm_g", 128), ("mla_q_norm_g", 384),
         ("mla_kv_norm_g", 256), ("norm_ffn_g", 1024), ("norm_final_g", 1024)]
SMALL_ROWS = 32
WEIGHT_ORDER = ["norm_mix_g", "w_in", "conv_qkv_w", "gdn_a_log", "gdn_dt_bias", "gdn_norm_g", "mla_q_norm_g", "w_uq",
                "mla_kv_norm_g", "w_ukv", "w_o_gdn", "w_o_mla", "w_out", "norm_ffn_g", "w_up", "conv_ffn_w", "w_down",
                "norm_final_g"]


def _padc(w, n):
    return jnp.pad(w, ((0, 0), (0, n - w.shape[1])))


def _padrc(w, r, n):
    return jnp.pad(w, ((0, r - w.shape[0]), (0, n - w.shape[1])))


def _slabs(p, dtype):
    A = jnp.concatenate([_padc(p["w_in"], A_W), _padc(p["w_up"], A_W)], axis=0).astype(dtype)
    Q = jnp.concatenate([_padc(p["w_uq"], Q_W), p["w_ukv"]], axis=0).astype(dtype)
    C = jnp.concatenate([p["w_o_gdn"], p["w_o_mla"], p["w_out"], p["w_down"]], axis=0).astype(dtype)
    V = jnp.concatenate([_padrc(p["conv_qkv_w"], 8, V_W), _padrc(p["conv_ffn_w"], 8, V_W)], axis=0).astype(F32)
    return {"A": A, "Q": Q, "C": C, "V": V}


def _unslab(sl):
    A_in, A_up, Q, C, V = sl["A_in"], sl["A_up"], sl["Q"], sl["C"], sl["V"]
    out = {"w_in": A_in[:, :IN_C], "w_up": A_up[:, :UP_C], "w_uq": Q[:384, :UQ_C], "w_ukv": Q[384:],
           "w_o_gdn": C[0:128], "w_o_mla": C[128:256], "w_out": C[256:384], "w_down": C[384:],
           "conv_qkv_w": V[0:GDN_CONV, :QKV_C], "conv_ffn_w": V[8:8 + FFN_CONV, :UP_C]}
    return {k: a[None] for k, a in out.items()}


def _take_cols(pieces, lo, hi):
    out, off = [], 0
    for arr, a, b in pieces:
        s, e = max(lo, off), min(hi, off + b - a)
        if s < e:
            out.append(arr[:, a + s - off:a + e - off])
        off += b - a
    return out[0] if len(out) == 1 else jnp.concatenate(out, axis=1)


LOSS_SLOT = sum(n for _, n in SMALL)


def _pack_small(d, loss=None):
    flat = jnp.concatenate([d[n].reshape(-1).astype(F32) for n, _ in SMALL]
                           + ([] if loss is None else [loss.reshape(1).astype(F32)]))
    return jnp.pad(flat, (0, SMALL_ROWS * LANES - flat.shape[0])).reshape(SMALL_ROWS, LANES)


def _unpack_small(buf, shapes):
    flat, out, off = buf.reshape(-1), {}, 0
    for name, n in SMALL:
        out[name] = flat[off:off + n].reshape(shapes[name])
        off += n
    return out


def _rot_cols(w):
    h = ROPE // 2
    return jnp.concatenate([-w[:, h:], w[:, :h]], axis=1)


def _unrot_cols(dw):
    h = ROPE // 2
    return jnp.concatenate([dw[:, h:], -dw[:, :h]], axis=1)


IN_SPLITS = [0, 3072, 4096, 4104, 4112, 4496, 4752, 4816, 5840, 6864]


def _layout_late(A_up, C):
    W = {"w_up": jnp.concatenate([A_up[j, :, :UP_C] for j in range(N_DEV)], axis=1),
         "w_o_gdn": C[:, 0:128].reshape(1024, D_MODEL), "w_o_mla": C[:, 128:256].reshape(1024, D_MODEL),
         "w_out": C[:, 256:384].reshape(1024, D_MODEL), "w_down": C[:, 384:].reshape(D_FF, D_MODEL)}
    return {k: v.astype(BF16) for k, v in W.items()}


def _layout_weights(g):
    A_in, Q, V = g["A_in"], g["Q"], g["V"]
    in_pieces = [(A_in[j], 0, IN_C) for j in range(N_DEV)]
    o = IN_SPLITS
    take = lambda lo, hi: _take_cols(in_pieces, lo, hi)
    kpe = take(o[6], o[7])
    W = {
        "in_qkv": take(o[0], o[1]),
        "in_ga": take(o[1], o[2]),
        "in_ab": jnp.concatenate([_padc(take(o[2], o[3]), LANES), _padc(take(o[3], o[4]), LANES)], axis=1),
        "in_small": jnp.concatenate([take(o[4], o[6]), _padc(kpe, LANES), _padc(_rot_cols(kpe), LANES)], axis=1),
        "in_gbr": take(o[7], o[9]),
        "uq_n": jnp.concatenate([Q[j, :384, :HD] for j in range(N_DEV)], axis=1),
        "ukv_k": jnp.concatenate([Q[j, 384:, :HD] for j in range(N_DEV)], axis=1),
        "ukv_v": jnp.concatenate([Q[j, 384:, HD:] for j in range(N_DEV)], axis=1),
    }
    pe = [Q[j, :384, HD:HD + ROPE] for j in range(N_DEV)]
    W["uq_p"] = jnp.concatenate([_padc(p, HD) for p in pe] + [_padc(_rot_cols(p), HD) for p in pe], axis=1)
    conv_qkv = jnp.concatenate([V[j, 0:GDN_CONV, :QKV_C] for j in range(N_DEV)], axis=1)
    conv_ffn = jnp.concatenate([V[j, 8:8 + FFN_CONV, :UP_C] for j in range(N_DEV)], axis=1)
    return {k: v.astype(BF16) for k, v in W.items()}, conv_qkv, conv_ffn


def _full_grads(dW):
    s = dW["in_small"]
    dkpe = s[:, 640:704] + _unrot_cols(s[:, 768:832])
    in_pieces = [(dW["in_qkv"], 0, 3072), (dW["in_ga"], 0, 1024), (dW["in_ab"], 0, 8), (dW["in_ab"], 128, 136),
                 (s, 0, 640), (dkpe, 0, ROPE), (dW["in_gbr"], 0, 2048)]
    pe = []
    for j in range(N_DEV):
        lin = dW["uq_p"][:, j * HD:j * HD + ROPE]
        rot = dW["uq_p"][:, 1024 + j * HD:1024 + j * HD + ROPE]
        pe.append(lin + _unrot_cols(rot))
    return in_pieces, pe


def _send_slabs(dW, d_conv_qkv, d_conv_ffn):
    in_pieces, pe = _full_grads(dW)
    A_in, A_up, Q, V = [], [], [], []
    for j in range(N_DEV):
        A_in.append(_padc(_take_cols(in_pieces, j * IN_C, (j + 1) * IN_C), A_W))
        A_up.append(_padc(dW["w_up"][:, j * UP_C:(j + 1) * UP_C], A_W))
        guq = _padc(jnp.concatenate([dW["uq_n"][:, j * HD:(j + 1) * HD], pe[j]], axis=1), Q_W)
        gukv = jnp.concatenate([dW["ukv_k"][:, j * HD:(j + 1) * HD], dW["ukv_v"][:, j * HD:(j + 1) * HD]], axis=1)
        Q.append(jnp.concatenate([guq, gukv], axis=0))
        V.append(jnp.concatenate([_padrc(d_conv_qkv[:, j * QKV_C:(j + 1) * QKV_C], 8, V_W),
                                  _padrc(d_conv_ffn[:, j * UP_C:(j + 1) * UP_C], 8, V_W)], axis=0))
    C = jnp.concatenate([dW["w_o_gdn"].reshape(N_DEV, 128, D_MODEL), dW["w_o_mla"].reshape(N_DEV, 128, D_MODEL),
                         dW["w_out"].reshape(N_DEV, 128, D_MODEL), dW["w_down"].reshape(N_DEV, 352, D_MODEL)], axis=1)
    return {"A_in": jnp.concatenate(A_in, axis=0).astype(BF16), "A_up": jnp.concatenate(A_up, axis=0).astype(BF16),
            "Q": jnp.concatenate(Q, axis=0).astype(BF16),
            "C": C.reshape(N_DEV * 736, D_MODEL).astype(BF16), "V": jnp.concatenate(V, axis=0)}


def _rope_tables(S):
    half = ROPE // 2
    inv = ROPE_THETA ** (-jnp.arange(half, dtype=F32) / half)
    ang = jnp.arange(S, dtype=F32)[:, None] * inv[None, :]
    cos = jnp.concatenate([jnp.cos(ang), jnp.cos(ang)], axis=1)
    sin = jnp.concatenate([jnp.sin(ang), jnp.sin(ang)], axis=1)
    return _padc(cos, HD), _padc(sin, HD)


def _local_step(x, tgt, W, late_weights, conv_qkv_w, conv_ffn_w, small, tm=None, ta=None):
    B, S, _ = x.shape
    T = B * S
    tm = tm or _pick(S, 1024, CHUNK)
    ta = ta or _pick(S, 512, LANES)
    x2d, tgt2d = x.reshape(T, D_MODEL), tgt.reshape(T, D_MODEL)
    row = lambda v: v.reshape(1, -1).astype(F32)
    pad_row = lambda v: _padc(row(v), LANES)
    g_mix, g_ffn, g_fin = row(small["norm_mix_g"]), row(small["norm_ffn_g"]), row(small["norm_final_g"])
    g_gdn, g_q, g_kv = row(small["gdn_norm_g"]), row(small["mla_q_norm_g"]), row(small["mla_kv_norm_g"])
    alog, dtb = pad_row(small["gdn_a_log"]), pad_row(small["gdn_dt_bias"])
    cos, sin = _rope_tables(S)
    tps = S // tm
    tab = lambda a: (a, (tm, HD), lambda i: (i % tps, 0))
    col = lambda a, c, w: (a, (tm, w), lambda i, c=c: (i, c))

    h1 = _norm_fwd(x2d, g_mix, T=T, tm=tm, name="norm_mix_fwd")
    z_qkv = _mm(h1, W["in_qkv"], "nn", BF16, name="in_qkv_fwd")
    z_ga = _mm(h1, W["in_ga"], "nn", BF16, name="in_ga_fwd")
    z_ab = _mm(h1, W["in_ab"], "nn", F32, name="in_ab_fwd")
    z_small = _mm(h1, W["in_small"], "nn", BF16, name="in_small_fwd", tn=896)
    z_gbr = _mm(h1, W["in_gbr"], "nn", BF16, name="in_gbr_fwd")

    qkvn = _conv_fwd(_qkv_fn, [(z_qkv, 0)], [(conv_qkv_w, 0)], 3072, BF16, T=T, S=S, tm=tm, cb=QKV_CB,
                     ncb=3072 // QKV_CB, name="gdn_qkv_fwd")
    gcum, beta = _row_call(lambda za, zb, al, db: _gate_fn(za, zb, al, db), [col(z_ab, 0, LANES), col(z_ab, 1, LANES)],
                           [alog, dtb], [(LANES, F32), (LANES, F32)], T=T, tm=tm, name="gdn_gate_fwd")
    grT = gcum[:, :HEADS].reshape(T // CHUNK, CHUNK, HEADS).transpose(0, 2, 1)[:, :, None, :]
    qkvn, late = late_weights(qkvn)
    W = {**W, **late}
    o_gdn, states = _gdn_fwd(qkvn, gcum, grT, beta, B=B, S=S)

    def gdn_out_fn(o, ga, g):
        parts = []
        for h in range(HEADS):
            sl = slice(h * HD, (h + 1) * HD)
            parts.append(_rms(o[:, sl], g) * jax.nn.silu(ga[:, sl].astype(F32)))
        return jnp.concatenate(parts, axis=1)

    oa = _row_call(lambda o, ga, g: (gdn_out_fn(o, ga, g),), [o_gdn, z_ga], [g_gdn], [(1024, BF16)], T=T, tm=tm,
                   name="gdn_out_fwd")[0]

    def mla_prep_fn(zq, zkv, zpl, zpr, c, s, gq, gkv):
        return _rms(zq.astype(F32), gq), _rms(zkv.astype(F32), gkv), zpl.astype(F32) * c + zpr.astype(F32) * s

    small_cols = [(z_small, (tm, Q_RANK), lambda i: (i, 0)), (z_small, (tm, LANES), lambda i: (i, 3)),
                  (z_small, (tm, LANES), lambda i: (i, 4)), (z_small, (tm, LANES), lambda i: (i, 5)),
                  (z_small, (tm, LANES), lambda i: (i, 6))]

    def mla_prep_fwd(zq, zkv0, zkv1, zpl, zpr, c, s, gq, gkv):
        return mla_prep_fn(zq, jnp.concatenate([zkv0, zkv1], axis=1), zpl, zpr, c, s, gq, gkv)

    cq, ckv, kpe = _row_call(mla_prep_fwd, small_cols + [tab(cos), tab(sin)], [g_q, g_kv],
                             [(Q_RANK, BF16), (KV_RANK, BF16), (HD, BF16)], T=T, tm=tm, name="mla_prep_fwd")
    qn = _mm(cq, W["uq_n"], "nn", BF16, name="uq_n_fwd")
    qpl = _mm(cq, W["uq_p"], "nn", BF16, name="uq_p_fwd")
    kn = _mm(ckv, W["ukv_k"], "nn", BF16, name="ukv_k_fwd")
    vb = _mm(ckv, W["ukv_v"], "nn", BF16, name="ukv_v_fwd")

    def qrope_fn(lin, rot, c, s):
        return lin * jnp.tile(c, (1, HEADS)) + rot * jnp.tile(s, (1, HEADS))

    qp = _row_call(lambda lin, rot, c, s: (qrope_fn(lin, rot, c, s),), [col(qpl, 0, 1024), col(qpl, 1, 1024), tab(cos), tab(sin)],
                   [], [(1024, BF16)], T=T, tm=tm, name="q_rope_fwd")[0]
    ob, lse = _flash_fwd(qn, qp, kn, kpe, vb, B=B, S=S, t=ta)

    def merge_fn(ya, yb, ga, gb):
        return jax.nn.sigmoid(ga.astype(F32)) * ya + jax.nn.sigmoid(gb.astype(F32)) * yb

    def merge_fwd(oat, obt, ga, gb, wog, wom):
        ya, yb = _dot(oat, wog), _dot(obt, wom)
        return ya, yb, merge_fn(ya, yb, ga, gb)

    ya, yb, merged = _row_call(merge_fwd, [oa, ob, col(z_gbr, 0, 1024), col(z_gbr, 1, 1024)], [W["w_o_gdn"], W["w_o_mla"]],
                               [(1024, BF16), (1024, BF16), (1024, BF16)], T=T, tm=tm, name="merge_fwd")
    x1 = _mm(merged, W["w_out"], "nn", F32, add=x2d, name="w_out_fwd")

    h2 = _norm_fwd(x1, g_ffn, T=T, tm=tm, name="norm_ffn_fwd")
    up = _mm(h2, W["w_up"], "nn", BF16, name="w_up_fwd")
    FCB = 256
    nfb = D_FF // FCB
    f = _conv_fwd(_ffn_fn, [(up, 0), (up, 2)], [(conv_ffn_w, 0), (conv_ffn_w, 2)], D_FF, BF16, T=T, S=S, tm=tm,
                  cb=D_FF // 2, ncb=2, name="ffn_act_fwd")
    x2 = _mm(f, W["w_down"], "nn", F32, add=x1, name="w_down_fwd", tk=1408)

    def final_fn(xt, tt, g):
        def lossf(xv, gv):
            e = _rms(xv, gv) - tt
            return 0.5 * jnp.sum(jnp.mean(e * e, axis=-1))

        l, vjp = jax.vjp(lossf, xt, g)
        dx, dg = vjp(jnp.ones((), F32))
        return dx, jnp.full((1, LANES), l, F32), dg

    dx2, loss_v, dg_fin = _row_call(final_fn, [x2, tgt2d], [g_fin], [(1024, F32)], [((1, LANES), F32), ((1, 1024), F32)],
                                    T=T, tm=tm, name="loss_head")

    dW = {}
    df = _mm(dx2, W["w_down"], "nt", BF16, name="w_down_dx")
    dW["w_down"] = _mm(f, dx2, "tn", F32, name="w_down_dw")
    dug, duu, dcw_g, dcw_u = _conv_bwd(_ffn_fn, [(up, 0), (up, nfb)], [(conv_ffn_w, 0), (conv_ffn_w, nfb)], df, BF16,
                                       T=T, S=S, tm=tm, cb=FCB, ncb=nfb, name="ffn_act_bwd")
    d_conv_ffn = jnp.concatenate([dcw_g, dcw_u], axis=1)
    wup_g, wup_u = W["w_up"][:, :D_FF], W["w_up"][:, D_FF:]
    dh2 = _mm(dug, wup_g, "nt", F32, name="w_up_dx_g")
    dh2 = _mm(duu, wup_u, "nt", BF16, add=dh2, name="w_up_dx_u")
    dW["w_up"] = jnp.concatenate([_mm(h2, dug, "tn", F32, name="w_up_dw_g"), _mm(h2, duu, "tn", F32, name="w_up_dw_u")], axis=1)
    dx1, dg_ffn = _norm_bwd(x1, g_ffn, dh2, dx2, T=T, tm=tm, name="norm_ffn_bwd")

    dmerged = _mm(dx1, W["w_out"], "nt", BF16, name="w_out_dx")
    dW["w_out"] = _mm(merged, dx1, "tn", F32, name="w_out_dw")

    def merge_bwd(dm, yat, ybt, ga, gb):
        _, vjp = jax.vjp(merge_fn, yat.astype(F32), ybt.astype(F32), ga, gb)
        return vjp(dm.astype(F32))

    dya, dyb, dgbr_a, dgbr_b = _row_call(merge_bwd, [dmerged, ya, yb, col(z_gbr, 0, 1024), col(z_gbr, 1, 1024)], [],
                                         [(1024, BF16)] * 4, T=T, tm=tm, name="merge_bwd")
    doa = _mm(dya, W["w_o_gdn"], "nt", BF16, name="w_o_gdn_dx")
    dob = _mm(dyb, W["w_o_mla"], "nt", BF16, name="w_o_mla_dx")
    dW["w_o_gdn"] = _mm(oa, dya, "tn", F32, name="w_o_gdn_dw")
    dW["w_o_mla"] = _mm(ob, dyb, "tn", F32, name="w_o_mla_dw")

    dqn, dqp, dl = _flash_bwd_dq(qn, qp, kn, kpe, vb, ob, dob, lse, B=B, S=S, t=ta)
    dkn, dkp, dvb = _flash_bwd_dkv(qn, qp, kn, kpe, vb, dob, lse.reshape(HEADS, 1, T), dl.reshape(HEADS, 1, T),
                                   B=B, S=S, t=ta)

    def qrope_bwd(d, c, s):
        return d * jnp.tile(c, (1, HEADS)), d * jnp.tile(s, (1, HEADS))

    dq_lin, dq_rot = _row_call(qrope_bwd, [dqp, tab(cos), tab(sin)], [], [(1024, BF16), (1024, BF16)], T=T, tm=tm,
                               name="q_rope_bwd")
    wp_lin, wp_rot = W["uq_p"][:, :1024], W["uq_p"][:, 1024:]
    dcq = _mm(dqn, W["uq_n"], "nt", F32, name="uq_n_dx")
    dcq = _mm(dq_lin, wp_lin, "nt", F32, add=dcq, name="uq_pl_dx")
    dcq = _mm(dq_rot, wp_rot, "nt", F32, add=dcq, name="uq_pr_dx")
    dW["uq_n"] = _mm(cq, dqn, "tn", F32, name="uq_n_dw")
    dW["uq_p"] = jnp.concatenate([_mm(cq, dq_lin, "tn", F32, name="uq_pl_dw"), _mm(cq, dq_rot, "tn", F32, name="uq_pr_dw")], axis=1)
    dckv = _mm(dkn, W["ukv_k"], "nt", F32, name="ukv_k_dx")
    dckv = _mm(dvb, W["ukv_v"], "nt", F32, add=dckv, name="ukv_v_dx")
    dW["ukv_k"] = _mm(ckv, dkn, "tn", F32, name="ukv_k_dw")
    dW["ukv_v"] = _mm(ckv, dvb, "tn", F32, name="ukv_v_dw")

    def mla_prep_bwd(zq, zkv0, zkv1, zpl, zpr, c, s, dcqt, dckvt, dkpt, gq, gkv):
        zkv = jnp.concatenate([zkv0, zkv1], axis=1)
        _, vjp = jax.vjp(lambda a, b, p, r, g1, g2: mla_prep_fn(a, b, p, r, c, s, g1, g2), zq, zkv, zpl, zpr, gq, gkv)
        dk = dkpt[0]
        for h in range(1, HEADS):
            dk = dk + dkpt[h]
        dzq, dzkv, dzpl, dzpr, dgq, dgkv = vjp((dcqt, dckvt, dk))
        return jnp.concatenate([dzq, dzkv, dzpl, dzpr], axis=1), dgq, dgkv

    dz_small, dg_q, dg_kv = _row_call(
        mla_prep_bwd, small_cols + [tab(cos), tab(sin), dcq, dckv, (dkp, (HEADS, tm, HD), lambda i: (0, i, 0))],
        [g_q, g_kv], [(896, BF16)], [((1, Q_RANK), F32), ((1, KV_RANK), F32)], T=T, tm=tm, name="mla_prep_bwd")

    def gdn_out_bwd(o, ga, dot_, g):
        _, vjp = jax.vjp(gdn_out_fn, o, ga, g)
        return vjp(dot_.astype(F32))

    do_gdn, dz_ga, dg_gdn = _row_call(gdn_out_bwd, [o_gdn, z_ga, doa], [g_gdn], [(1024, F32), (1024, BF16)],
                                      [((1, HD), F32)], T=T, tm=tm, name="gdn_out_bwd")
    dqkvn, dgc, dgrT, dbeta = _gdn_bwd(qkvn, gcum, grT, beta, states, do_gdn, B=B, S=S)
    dgc_tot = dgc + _padc(dgrT[:, :, 0, :].transpose(0, 2, 1).reshape(T, HEADS), LANES)

    def gate_bwd(za, zb, dg, db, al, db_):
        _, vjp = jax.vjp(_gate_fn, za, zb, al, db_)
        return vjp((dg, db))

    dz_a, dz_b, d_alog, d_dtb = _row_call(gate_bwd, [col(z_ab, 0, LANES), col(z_ab, 1, LANES), dgc_tot, dbeta], [alog, dtb],
                                          [(LANES, BF16), (LANES, BF16)], [((1, LANES), F32), ((1, LANES), F32)],
                                          T=T, tm=tm, name="gdn_gate_bwd")
    dz_qkv, d_conv_qkv = _conv_bwd(_qkv_fn, [(z_qkv, 0)], [(conv_qkv_w, 0)], dqkvn, BF16, T=T, S=S, tm=tm, cb=QKV_CB,
                                   ncb=3072 // QKV_CB, name="gdn_qkv_bwd")

    dz_ab = jnp.concatenate([dz_a, dz_b], axis=1)
    dz_gbr = jnp.concatenate([dgbr_a, dgbr_b], axis=1)
    dh1 = None
    for key, dz in (("in_qkv", dz_qkv), ("in_ga", dz_ga), ("in_ab", dz_ab), ("in_small", dz_small), ("in_gbr", dz_gbr)):
        dh1 = _mm(dz, W[key], "nt", BF16 if key == "in_gbr" else F32, add=dh1, name=key + "_dx",
                  tk=896 if key == "in_small" else 1024)
        dW[key] = _mm(h1, dz, "tn", F32, name=key + "_dw", tn=896 if key == "in_small" else 1024)
    dx, dg_mix = _norm_bwd(x2d, g_mix, dh1, dx1, T=T, tm=tm, name="norm_mix_bwd")

    dsmall = {"norm_mix_g": dg_mix, "gdn_a_log": d_alog[:, :HEADS], "gdn_dt_bias": d_dtb[:, :HEADS], "gdn_norm_g": dg_gdn,
              "mla_q_norm_g": dg_q, "mla_kv_norm_g": dg_kv, "norm_ffn_g": dg_ffn, "norm_final_g": dg_fin}
    return loss_v[0, 0], dx.reshape(B, S, D_MODEL), dW, d_conv_qkv, d_conv_ffn, dsmall


def kernel(x, norm_mix_g, w_in, conv_qkv_w, gdn_a_log, gdn_dt_bias, gdn_norm_g, mla_q_norm_g, w_uq, mla_kv_norm_g, w_ukv, w_o_gdn, w_o_mla, w_out, norm_ffn_g, w_up, conv_ffn_w, w_down, norm_final_g, loss_target, m_norm_mix_g, m_w_in, m_conv_qkv_w, m_gdn_a_log, m_gdn_dt_bias, m_gdn_norm_g, m_mla_q_norm_g, m_w_uq, m_mla_kv_norm_g, m_w_ukv, m_w_o_gdn, m_w_o_mla, m_w_out, m_norm_ffn_g, m_w_up, m_conv_ffn_w, m_w_down, m_norm_final_g, v_norm_mix_g, v_w_in, v_conv_qkv_w, v_gdn_a_log, v_gdn_dt_bias, v_gdn_norm_g, v_mla_q_norm_g, v_w_uq, v_mla_kv_norm_g, v_w_ukv, v_w_o_gdn, v_w_o_mla, v_w_out, v_norm_ffn_g, v_w_up, v_conv_ffn_w, v_w_down, v_norm_final_g):
    w = dict(norm_mix_g=norm_mix_g, w_in=w_in, conv_qkv_w=conv_qkv_w, gdn_a_log=gdn_a_log, gdn_dt_bias=gdn_dt_bias,
             gdn_norm_g=gdn_norm_g, mla_q_norm_g=mla_q_norm_g, w_uq=w_uq, mla_kv_norm_g=mla_kv_norm_g, w_ukv=w_ukv,
             w_o_gdn=w_o_gdn, w_o_mla=w_o_mla, w_out=w_out, norm_ffn_g=norm_ffn_g, w_up=w_up, conv_ffn_w=conv_ffn_w,
             w_down=w_down, norm_final_g=norm_final_g)
    m = dict(norm_mix_g=m_norm_mix_g, w_in=m_w_in, conv_qkv_w=m_conv_qkv_w, gdn_a_log=m_gdn_a_log, gdn_dt_bias=m_gdn_dt_bias,
             gdn_norm_g=m_gdn_norm_g, mla_q_norm_g=m_mla_q_norm_g, w_uq=m_w_uq, mla_kv_norm_g=m_mla_kv_norm_g, w_ukv=m_w_ukv,
             w_o_gdn=m_w_o_gdn, w_o_mla=m_w_o_mla, w_out=m_w_out, norm_ffn_g=m_norm_ffn_g, w_up=m_w_up,
             conv_ffn_w=m_conv_ffn_w, w_down=m_w_down, norm_final_g=m_norm_final_g)
    v = dict(norm_mix_g=v_norm_mix_g, w_in=v_w_in, conv_qkv_w=v_conv_qkv_w, gdn_a_log=v_gdn_a_log, gdn_dt_bias=v_gdn_dt_bias,
             gdn_norm_g=v_gdn_norm_g, mla_q_norm_g=v_mla_q_norm_g, w_uq=v_w_uq, mla_kv_norm_g=v_mla_kv_norm_g, w_ukv=v_w_ukv,
             w_o_gdn=v_w_o_gdn, w_o_mla=v_w_o_mla, w_out=v_w_out, norm_ffn_g=v_norm_ffn_g, w_up=v_w_up,
             conv_ffn_w=v_conv_ffn_w, w_down=v_w_down, norm_final_g=v_norm_final_g)
    big_names = ("w_in", "w_up", "w_uq", "w_ukv", "w_o_gdn", "w_o_mla", "w_out", "w_down", "conv_qkv_w", "conv_ffn_w")
    small_names = [n for n, _ in SMALL]
    small_shapes = {n: w[n].shape for n in small_names}
    local2d = lambda d: {n: d[n][0] for n in big_names}

    w_slabs = _slabs(local2d(w), F32)
    a_bf = w_slabs["A"].astype(BF16)
    first = _allgather_async([a_bf[:1024]], name="allgather_w_in", collective_id=1)
    second = _allgather_async([w_slabs["Q"].astype(BF16), w_slabs["V"]], name="allgather_mixers", collective_id=2)
    third = _allgather_async([a_bf[1024:], w_slabs["C"].astype(BF16)], name="allgather_ffn_out", collective_id=3)
    gathered = {k: g.reshape(N_DEV, -1, g.shape[1])
                for k, g in zip(("A_in", "Q", "V", "A_up", "C"), first + second + third)}
    W, conv_qkv_full, conv_ffn_full = _layout_weights(gathered)

    def late_weights(tie):
        tie, a_up, c_all = lax.optimization_barrier((tie, gathered["A_up"], gathered["C"]))
        return tie, _layout_late(a_up, c_all)

    loss_local, dx, dW, d_conv_qkv, d_conv_ffn, dsmall = _local_step(
        x, loss_target, W, late_weights, conv_qkv_full, conv_ffn_full, {n: w[n] for n in small_names})

    g_send = _send_slabs(dW, d_conv_qkv, d_conv_ffn)
    early_names, late_names = ("A_up", "C", "Q"), ("A_in", "V")
    recv = dict(zip(early_names, _alltoall_async([g_send[k] for k in early_names], name="alltoall_grads_early",
                                                 collective_id=0)))
    recv.update(zip(late_names, _alltoall_async([g_send[k] for k in late_names], name="alltoall_grads_late",
                                                collective_id=5)))
    small_parts = _alltoall_async([jnp.tile(_pack_small(dsmall, loss_local), (N_DEV, 1))], name="alltoall_small_grads",
                                  collective_id=4)[0]
    def halves(sl):
        return {"A_in": sl["A"][:1024], "A_up": sl["A"][1024:], "Q": sl["Q"], "C": sl["C"], "V": sl["V"]}

    w_h, m_h, v_h = halves(w_slabs), halves(_slabs(local2d(m), F32)), halves(_slabs(local2d(v), F32))
    upd = {k: _reduce_adam(recv[k], w_h[k], m_h[k], v_h[k], tr=SLAB_TR[k], name="adam_" + k) for k in SLAB_TR}
    upd_small = _reduce_adam(small_parts, _pack_small({n: w[n] for n in small_names}), _pack_small({n: m[n] for n in small_names}),
                             _pack_small({n: v[n] for n in small_names}), tr=SMALL_ROWS, name="adam_small")

    loss = upd_small[0].reshape(-1)[LOSS_SLOT]
    groups = []
    for i in range(4):
        merged = {**_unslab({k: upd[k][i] for k in SLAB_TR}), **_unpack_small(upd_small[i], small_shapes)}
        groups.append([merged[n] for n in WEIGHT_ORDER])
    return (loss, dx, *groups[0], *groups[1], *groups[2], *groups[3])
```
